```python
import math
import jax, jax.numpy as jnp
from jax import lax
import numpy as np

D_MODEL = 1024
BATCH = 8
SEQ = 8192
DEPTH = 2

D_MIX = D_MODEL
D_FF = 2816
NORM_EPS = 1e-6

POOL_WINDOWS = (2, 4, 8, 16)
POOL_GROUPS = len(POOL_WINDOWS)
POOL_GROUP_DIM = 64
POOL_WIDTH = POOL_GROUPS * POOL_GROUP_DIM

ATTN_HEADS = 8
HEAD_DIM = 64
ATTN_WIDTH = ATTN_HEADS * HEAD_DIM
Q_BLOCK = 128

CONV_WIDTH_CH = D_MIX - POOL_WIDTH - ATTN_WIDTH
CONV_KERNEL = 31

IN_COLS = POOL_WIDTH + 3 * ATTN_WIDTH + ATTN_HEADS + 2 * CONV_WIDTH_CH

kernel_name = "hymba_style_pool_fox_conformer_macaron"


def rms_norm(x, g):
    x32 = x.astype(jnp.float32)
    y = x32 * lax.rsqrt(jnp.mean(x32 * x32, axis=-1, keepdims=True) + NORM_EPS)
    return (y * g.astype(jnp.float32)).astype(x.dtype)


def layer_norm(x, g, b):
    x32 = x.astype(jnp.float32)
    mu = jnp.mean(x32, axis=-1, keepdims=True)
    xc = x32 - mu
    var = jnp.mean(xc * xc, axis=-1, keepdims=True)
    y = xc * lax.rsqrt(var + NORM_EPS)
    return (y * g.astype(jnp.float32) + b.astype(jnp.float32)).astype(x.dtype)


def swiglu(h, w_gate, w_up, w_down):
    return (jax.nn.silu(h @ w_gate) * (h @ w_up)) @ w_down


def causal_window_mean(u, w):
    S = u.shape[1]
    u32 = u.astype(jnp.float32)
    cs = jnp.cumsum(u32, axis=1)
    lagged = jnp.pad(cs, ((0, 0), (w, 0), (0, 0)))[:, :S]
    count = jnp.minimum(jnp.arange(S) + 1, w).astype(jnp.float32)
    return ((cs - lagged) / count[None, :, None]).astype(u.dtype)


def pool_mixer(u, pool_w, pool_scale):
    B, S, _ = u.shape
    ug = u.reshape(B, S, POOL_GROUPS, POOL_GROUP_DIM)
    pooled = jnp.stack(
        [causal_window_mean(ug[:, :, g], w) - ug[:, :, g] for g, w in enumerate(POOL_WINDOWS)],
        axis=2)
    mixed = jnp.einsum('bsgc,gcd->bsgd', pooled, pool_w)
    return mixed.reshape(B, S, POOL_WIDTH) * pool_scale


def forgetting_attention(q, k, v, z_f, forget_bias):
    B, S, H, Dh = q.shape
    n_blk = S // Q_BLOCK
    scale = 1.0 / math.sqrt(Dh)
    log_f = jax.nn.log_sigmoid(z_f.astype(jnp.float32) + forget_bias.astype(jnp.float32))
    F = jnp.cumsum(log_f, axis=1).transpose(0, 2, 1)
    qh = q.transpose(0, 2, 1, 3)
    kh = k.transpose(0, 2, 1, 3)
    vh = v.transpose(0, 2, 1, 3)
    q_blocks = qh.reshape(B, H, n_blk, Q_BLOCK, Dh).transpose(2, 0, 1, 3, 4)
    F_blocks = F.reshape(B, H, n_blk, Q_BLOCK).transpose(2, 0, 1, 3)
    k_pos = jnp.arange(S)

    def one_block(args):
        q_i, F_i, i = args
        s = jnp.einsum('bhqd,bhkd->bhqk', q_i, kh).astype(jnp.float32) * scale
        s = s + F_i[..., None] - F[:, :, None, :]
        q_pos = i * Q_BLOCK + jnp.arange(Q_BLOCK)
        mask = k_pos[None, :] <= q_pos[:, None]
        s = jnp.where(mask[None, None], s, -jnp.inf)
        p = jax.nn.softmax(s, axis=-1)
        return jnp.einsum('bhqk,bhkd->bhqd', p.astype(vh.dtype), vh)

    out = lax.map(one_block, (q_blocks, F_blocks, jnp.arange(n_blk)))
    return out.transpose(1, 0, 3, 2, 4).reshape(B, S, H * Dh)


def conformer_conv(h_glu, conv_w, conv_b, ln_g, ln_b):
    a, g = jnp.split(h_glu, 2, axis=-1)
    u = a * jax.nn.sigmoid(g)
    y = lax.conv_general_dilated(
        u, conv_w[:, None, :].astype(u.dtype), window_strides=(1,),
        padding=[(CONV_KERNEL - 1, 0)], dimension_numbers=('NWC', 'WIO', 'NWC'),
        feature_group_count=CONV_WIDTH_CH) + conv_b
    return jax.nn.silu(layer_norm(y, ln_g, ln_b))


def token_mixer(h, w_in, pool_w, pool_scale, forget_bias, conv_w, conv_b, conv_ln_g, conv_ln_b, w_out):
    B, S, _ = h.shape
    p = h @ w_in
    o = 0
    u_pool = p[..., o:o + POOL_WIDTH]; o += POOL_WIDTH
    q = p[..., o:o + ATTN_WIDTH]; o += ATTN_WIDTH
    k = p[..., o:o + ATTN_WIDTH]; o += ATTN_WIDTH
    v = p[..., o:o + ATTN_WIDTH]; o += ATTN_WIDTH
    z_f = p[..., o:o + ATTN_HEADS]; o += ATTN_HEADS
    h_glu = p[..., o:o + 2 * CONV_WIDTH_CH]
    shp = (B, S, ATTN_HEADS, HEAD_DIM)
    y_a = pool_mixer(u_pool, pool_w, pool_scale)
    y_b = forgetting_attention(q.reshape(shp), k.reshape(shp), v.reshape(shp), z_f, forget_bias)
    y_c = conformer_conv(h_glu, conv_w, conv_b, conv_ln_g, conv_ln_b)
    return jnp.concatenate([y_a, y_b, y_c], axis=-1) @ w_out


def _fwd_setup_inputs(seed: int = 0) -> dict:
    key = jax.random.key(seed)
    ks = jax.random.split(key, 24)

    def nrm(k, shape, scale):
        return jax.random.normal(k, shape, jnp.float32) * scale

    def gain(k, shape):
        return 1.0 + 0.02 * jax.random.normal(k, shape, jnp.float32)

    L = DEPTH
    return {
        "x": nrm(ks[0], (BATCH, SEQ, D_MODEL), 1.0),
        "ffn1_norm": gain(ks[1], (L, D_MODEL)),
        "ffn1_w_gate": nrm(ks[2], (L, D_MODEL, D_FF), D_MODEL ** -0.5),
        "ffn1_w_up": nrm(ks[3], (L, D_MODEL, D_FF), D_MODEL ** -0.5),
        "ffn1_w_down": nrm(ks[4], (L, D_FF, D_MODEL), D_FF ** -0.5),
        "mix_norm": gain(ks[5], (L, D_MODEL)),
        "w_in": nrm(ks[6], (L, D_MODEL, IN_COLS), D_MODEL ** -0.5),
        "pool_w": nrm(ks[7], (L, POOL_GROUPS, POOL_GROUP_DIM, POOL_GROUP_DIM), POOL_GROUP_DIM ** -0.5),
        "pool_scale": gain(ks[8], (L, POOL_WIDTH)),
        "forget_bias": 2.0 + 0.1 * jax.random.normal(ks[9], (L, ATTN_HEADS), jnp.float32),
        "conv_w": nrm(ks[10], (L, CONV_KERNEL, CONV_WIDTH_CH), CONV_KERNEL ** -0.5),
        "conv_b": nrm(ks[11], (L, CONV_WIDTH_CH), 0.02),
        "conv_ln_g": gain(ks[12], (L, CONV_WIDTH_CH)),
        "conv_ln_b": nrm(ks[13], (L, CONV_WIDTH_CH), 0.02),
        "w_out": nrm(ks[14], (L, D_MIX, D_MODEL), D_MIX ** -0.5),
        "ffn2_norm": gain(ks[15], (L, D_MODEL)),
        "ffn2_w_gate": nrm(ks[16], (L, D_MODEL, D_FF), D_MODEL ** -0.5),
        "ffn2_w_up": nrm(ks[17], (L, D_MODEL, D_FF), D_MODEL ** -0.5),
        "ffn2_w_down": nrm(ks[18], (L, D_FF, D_MODEL), D_FF ** -0.5),
        "final_norm": gain(ks[19], (D_MODEL,)),
    }


def _fwd_reference(x, ffn1_norm, ffn1_w_gate, ffn1_w_up, ffn1_w_down, mix_norm, w_in, pool_w, pool_scale,
              forget_bias, conv_w, conv_b, conv_ln_g, conv_ln_b, w_out, ffn2_norm, ffn2_w_gate,
              ffn2_w_up, ffn2_w_down, final_norm):
    for l in range(DEPTH):
        x = x + 0.5 * swiglu(rms_norm(x, ffn1_norm[l]), ffn1_w_gate[l], ffn1_w_up[l], ffn1_w_down[l])
        x = x + token_mixer(rms_norm(x, mix_norm[l]), w_in[l], pool_w[l], pool_scale[l], forget_bias[l],
                            conv_w[l], conv_b[l], conv_ln_g[l], conv_ln_b[l], w_out[l])
        x = x + 0.5 * swiglu(rms_norm(x, ffn2_norm[l]), ffn2_w_gate[l], ffn2_w_up[l], ffn2_w_down[l])
    return rms_norm(x, final_norm)


import jax as _jax
import jax.numpy as _jnp

TWIN_FORMAT = 'train_step'
FWD_PARAMS = ['x', 'ffn1_norm', 'ffn1_w_gate', 'ffn1_w_up', 'ffn1_w_down', 'mix_norm', 'w_in', 'pool_w', 'pool_scale', 'forget_bias', 'conv_w', 'conv_b', 'conv_ln_g', 'conv_ln_b', 'w_out', 'ffn2_norm', 'ffn2_w_gate', 'ffn2_w_up', 'ffn2_w_down', 'final_norm']
TWIN_WEIGHTS = ['ffn1_norm', 'ffn1_w_gate', 'ffn1_w_up', 'ffn1_w_down', 'mix_norm', 'w_in', 'pool_w', 'pool_scale', 'forget_bias', 'conv_w', 'conv_b', 'conv_ln_g', 'conv_ln_b', 'w_out', 'ffn2_norm', 'ffn2_w_gate', 'ffn2_w_up', 'ffn2_w_down', 'final_norm']
TWIN_DIFF_INPUT = 'x'
TWIN_INPUTS = ['x', 'ffn1_norm', 'ffn1_w_gate', 'ffn1_w_up', 'ffn1_w_down', 'mix_norm', 'w_in', 'pool_w', 'pool_scale', 'forget_bias', 'conv_w', 'conv_b', 'conv_ln_g', 'conv_ln_b', 'w_out', 'ffn2_norm', 'ffn2_w_gate', 'ffn2_w_up', 'ffn2_w_down', 'final_norm', 'loss_target', 'm_ffn1_norm', 'm_ffn1_w_gate', 'm_ffn1_w_up', 'm_ffn1_w_down', 'm_mix_norm', 'm_w_in', 'm_pool_w', 'm_pool_scale', 'm_forget_bias', 'm_conv_w', 'm_conv_b', 'm_conv_ln_g', 'm_conv_ln_b', 'm_w_out', 'm_ffn2_norm', 'm_ffn2_w_gate', 'm_ffn2_w_up', 'm_ffn2_w_down', 'm_final_norm', 'v_ffn1_norm', 'v_ffn1_w_gate', 'v_ffn1_w_up', 'v_ffn1_w_down', 'v_mix_norm', 'v_w_in', 'v_pool_w', 'v_pool_scale', 'v_forget_bias', 'v_conv_w', 'v_conv_b', 'v_conv_ln_g', 'v_conv_ln_b', 'v_w_out', 'v_ffn2_norm', 'v_ffn2_w_gate', 'v_ffn2_w_up', 'v_ffn2_w_down', 'v_final_norm']
TWIN_OUTPUTS = ['loss', 'grad_x', 'grad_ffn1_norm', 'grad_ffn1_w_gate', 'grad_ffn1_w_up', 'grad_ffn1_w_down', 'grad_mix_norm', 'grad_w_in', 'grad_pool_w', 'grad_pool_scale', 'grad_forget_bias', 'grad_conv_w', 'grad_conv_b', 'grad_conv_ln_g', 'grad_conv_ln_b', 'grad_w_out', 'grad_ffn2_norm', 'grad_ffn2_w_gate', 'grad_ffn2_w_up', 'grad_ffn2_w_down', 'grad_final_norm', 'delta_ffn1_norm', 'delta_ffn1_w_gate', 'delta_ffn1_w_up', 'delta_ffn1_w_down', 'delta_mix_norm', 'delta_w_in', 'delta_pool_w', 'delta_pool_scale', 'delta_forget_bias', 'delta_conv_w', 'delta_conv_b', 'delta_conv_ln_g', 'delta_conv_ln_b', 'delta_w_out', 'delta_ffn2_norm', 'delta_ffn2_w_gate', 'delta_ffn2_w_up', 'delta_ffn2_w_down', 'delta_final_norm', 'new_m_ffn1_norm', 'new_m_ffn1_w_gate', 'new_m_ffn1_w_up', 'new_m_ffn1_w_down', 'new_m_mix_norm', 'new_m_w_in', 'new_m_pool_w', 'new_m_pool_scale', 'new_m_forget_bias', 'new_m_conv_w', 'new_m_conv_b', 'new_m_conv_ln_g', 'new_m_conv_ln_b', 'new_m_w_out', 'new_m_ffn2_norm', 'new_m_ffn2_w_gate', 'new_m_ffn2_w_up', 'new_m_ffn2_w_down', 'new_m_final_norm', 'new_v_ffn1_norm', 'new_v_ffn1_w_gate', 'new_v_ffn1_w_up', 'new_v_ffn1_w_down', 'new_v_mix_norm', 'new_v_w_in', 'new_v_pool_w', 'new_v_pool_scale', 'new_v_forget_bias', 'new_v_conv_w', 'new_v_conv_b', 'new_v_conv_ln_g', 'new_v_conv_ln_b', 'new_v_w_out', 'new_v_ffn2_norm', 'new_v_ffn2_w_gate', 'new_v_ffn2_w_up', 'new_v_ffn2_w_down', 'new_v_final_norm']
TWIN_LEAF_KINDS = {'loss': 'loss', 'grad_x': 'grad_x', 'grad_ffn1_norm': 'grad_w', 'grad_ffn1_w_gate': 'grad_w', 'grad_ffn1_w_up': 'grad_w', 'grad_ffn1_w_down': 'grad_w', 'grad_mix_norm': 'grad_w', 'grad_w_in': 'grad_w', 'grad_pool_w': 'grad_w', 'grad_pool_scale': 'grad_w', 'grad_forget_bias': 'grad_w', 'grad_conv_w': 'grad_w', 'grad_conv_b': 'grad_w', 'grad_conv_ln_g': 'grad_w', 'grad_conv_ln_b': 'grad_w', 'grad_w_out': 'grad_w', 'grad_ffn2_norm': 'grad_w', 'grad_ffn2_w_gate': 'grad_w', 'grad_ffn2_w_up': 'grad_w', 'grad_ffn2_w_down': 'grad_w', 'grad_final_norm': 'grad_w', 'delta_ffn1_norm': 'delta_w', 'delta_ffn1_w_gate': 'delta_w', 'delta_ffn1_w_up': 'delta_w', 'delta_ffn1_w_down': 'delta_w', 'delta_mix_norm': 'delta_w', 'delta_w_in': 'delta_w', 'delta_pool_w': 'delta_w', 'delta_pool_scale': 'delta_w', 'delta_forget_bias': 'delta_w', 'delta_conv_w': 'delta_w', 'delta_conv_b': 'delta_w', 'delta_conv_ln_g': 'delta_w', 'delta_conv_ln_b': 'delta_w', 'delta_w_out': 'delta_w', 'delta_ffn2_norm': 'delta_w', 'delta_ffn2_w_gate': 'delta_w', 'delta_ffn2_w_up': 'delta_w', 'delta_ffn2_w_down': 'delta_w', 'delta_final_norm': 'delta_w', 'new_m_ffn1_norm': 'new_m', 'new_m_ffn1_w_gate': 'new_m', 'new_m_ffn1_w_up': 'new_m', 'new_m_ffn1_w_down': 'new_m', 'new_m_mix_norm': 'new_m', 'new_m_w_in': 'new_m', 'new_m_pool_w': 'new_m', 'new_m_pool_scale': 'new_m', 'new_m_forget_bias': 'new_m', 'new_m_conv_w': 'new_m', 'new_m_conv_b': 'new_m', 'new_m_conv_ln_g': 'new_m', 'new_m_conv_ln_b': 'new_m', 'new_m_w_out': 'new_m', 'new_m_ffn2_norm': 'new_m', 'new_m_ffn2_w_gate': 'new_m', 'new_m_ffn2_w_up': 'new_m', 'new_m_ffn2_w_down': 'new_m', 'new_m_final_norm': 'new_m', 'new_v_ffn1_norm': 'new_v', 'new_v_ffn1_w_gate': 'new_v', 'new_v_ffn1_w_up': 'new_v', 'new_v_ffn1_w_down': 'new_v', 'new_v_mix_norm': 'new_v', 'new_v_w_in': 'new_v', 'new_v_pool_w': 'new_v', 'new_v_pool_scale': 'new_v', 'new_v_forget_bias': 'new_v', 'new_v_conv_w': 'new_v', 'new_v_conv_b': 'new_v', 'new_v_conv_ln_g': 'new_v', 'new_v_conv_ln_b': 'new_v', 'new_v_w_out': 'new_v', 'new_v_ffn2_norm': 'new_v', 'new_v_ffn2_w_gate': 'new_v', 'new_v_ffn2_w_up': 'new_v', 'new_v_ffn2_w_down': 'new_v', 'new_v_final_norm': 'new_v'}


def _forward(args):
    return _fwd_reference(*[args[k] for k in FWD_PARAMS])


def _output_shape():
    def fwd():
        inp = _fwd_setup_inputs(0)
        return _fwd_reference(*[inp[k] for k in FWD_PARAMS])
    out = _jax.eval_shape(fwd)
    return out.shape, out.dtype

N_MICROBATCH = 1
ADAM_LR = 0.001
ADAM_B1 = 0.9
ADAM_B2 = 0.999
ADAM_EPS = 1e-08
ADAM_WD = 0.01
ADAM_STEP = 10
PER_EXAMPLE_BATCH_AXIS = {'x': 0, 'loss_target': 0}
SHARED_INPUTS = []
_WEIGHT_DTYPES = {'ffn1_norm': _jnp.float32, 'ffn1_w_gate': _jnp.float32, 'ffn1_w_up': _jnp.float32, 'ffn1_w_down': _jnp.float32, 'mix_norm': _jnp.float32, 'w_in': _jnp.float32, 'pool_w': _jnp.float32, 'pool_scale': _jnp.float32, 'forget_bias': _jnp.float32, 'conv_w': _jnp.float32, 'conv_b': _jnp.float32, 'conv_ln_g': _jnp.float32, 'conv_ln_b': _jnp.float32, 'w_out': _jnp.float32, 'ffn2_norm': _jnp.float32, 'ffn2_w_gate': _jnp.float32, 'ffn2_w_up': _jnp.float32, 'ffn2_w_down': _jnp.float32, 'final_norm': _jnp.float32}
MOMENT_SCALE = {'ffn1_norm': 1.094153e-01, 'ffn1_w_gate': 4.553790e-02, 'ffn1_w_up': 4.410939e-02, 'ffn1_w_down': 7.327650e-02, 'mix_norm': 1.522008e-01, 'w_in': 9.694359e-02, 'pool_w': 1.807882e-01, 'pool_scale': 1.765368e-01, 'forget_bias': 4.601057e-01, 'conv_w': 1.314654e-01, 'conv_b': 2.803418e-01, 'conv_ln_g': 1.547584e-01, 'conv_ln_b': 1.352551e-01, 'w_out': 1.275635e-01, 'ffn2_norm': 8.580489e-02, 'ffn2_w_gate': 3.659039e-02, 'ffn2_w_up': 3.563172e-02, 'ffn2_w_down': 5.898833e-02, 'final_norm': 6.400445e+01}


def _to_microbatches(a, axis):
    t = _jnp.moveaxis(a, axis, 0)
    t = t.reshape((N_MICROBATCH, t.shape[0] // N_MICROBATCH) + t.shape[1:])
    return _jnp.moveaxis(t, 1, axis + 1)


def setup_inputs(seed: int = 0) -> dict:
    inp = _fwd_setup_inputs(seed)
    key = _jax.random.fold_in(_jax.random.key(seed), 7919)
    shape, _ = _output_shape()
    out = dict(inp)
    out["loss_target"] = _jax.random.normal(_jax.random.fold_in(key, 0), shape, _jnp.float32)
    for i, name in enumerate(TWIN_WEIGHTS):
        w = inp[name].astype(_jnp.float32)
        if MOMENT_SCALE is None:
            s = _jnp.sqrt(_jnp.mean(_jnp.square(w)) + 1e-30)
        else:
            s = MOMENT_SCALE[name]
        km, kv = _jax.random.split(_jax.random.fold_in(key, i + 1))
        out[name] = w
        out["m_" + name] = s * _jax.random.normal(km, w.shape, _jnp.float32)
        out["v_" + name] = (s * s) * _jax.random.uniform(kv, w.shape, _jnp.float32, 0.5, 1.5)
    if N_MICROBATCH > 1:
        for name, axis in PER_EXAMPLE_BATCH_AXIS.items():
            out[name] = _to_microbatches(out[name], axis)
    return {'x': out['x'], 'ffn1_norm': out['ffn1_norm'], 'ffn1_w_gate': out['ffn1_w_gate'], 'ffn1_w_up': out['ffn1_w_up'], 'ffn1_w_down': out['ffn1_w_down'], 'mix_norm': out['mix_norm'], 'w_in': out['w_in'], 'pool_w': out['pool_w'], 'pool_scale': out['pool_scale'], 'forget_bias': out['forget_bias'], 'conv_w': out['conv_w'], 'conv_b': out['conv_b'], 'conv_ln_g': out['conv_ln_g'], 'conv_ln_b': out['conv_ln_b'], 'w_out': out['w_out'], 'ffn2_norm': out['ffn2_norm'], 'ffn2_w_gate': out['ffn2_w_gate'], 'ffn2_w_up': out['ffn2_w_up'], 'ffn2_w_down': out['ffn2_w_down'], 'final_norm': out['final_norm'], 'loss_target': out['loss_target'], 'm_ffn1_norm': out['m_ffn1_norm'], 'm_ffn1_w_gate': out['m_ffn1_w_gate'], 'm_ffn1_w_up': out['m_ffn1_w_up'], 'm_ffn1_w_down': out['m_ffn1_w_down'], 'm_mix_norm': out['m_mix_norm'], 'm_w_in': out['m_w_in'], 'm_pool_w': out['m_pool_w'], 'm_pool_scale': out['m_pool_scale'], 'm_forget_bias': out['m_forget_bias'], 'm_conv_w': out['m_conv_w'], 'm_conv_b': out['m_conv_b'], 'm_conv_ln_g': out['m_conv_ln_g'], 'm_conv_ln_b': out['m_conv_ln_b'], 'm_w_out': out['m_w_out'], 'm_ffn2_norm': out['m_ffn2_norm'], 'm_ffn2_w_gate': out['m_ffn2_w_gate'], 'm_ffn2_w_up': out['m_ffn2_w_up'], 'm_ffn2_w_down': out['m_ffn2_w_down'], 'm_final_norm': out['m_final_norm'], 'v_ffn1_norm': out['v_ffn1_norm'], 'v_ffn1_w_gate': out['v_ffn1_w_gate'], 'v_ffn1_w_up': out['v_ffn1_w_up'], 'v_ffn1_w_down': out['v_ffn1_w_down'], 'v_mix_norm': out['v_mix_norm'], 'v_w_in': out['v_w_in'], 'v_pool_w': out['v_pool_w'], 'v_pool_scale': out['v_pool_scale'], 'v_forget_bias': out['v_forget_bias'], 'v_conv_w': out['v_conv_w'], 'v_conv_b': out['v_conv_b'], 'v_conv_ln_g': out['v_conv_ln_g'], 'v_conv_ln_b': out['v_conv_ln_b'], 'v_w_out': out['v_w_out'], 'v_ffn2_norm': out['v_ffn2_norm'], 'v_ffn2_w_gate': out['v_ffn2_w_gate'], 'v_ffn2_w_up': out['v_ffn2_w_up'], 'v_ffn2_w_down': out['v_ffn2_w_down'], 'v_final_norm': out['v_final_norm']}


def _loss(weights, diff, rest, loss_target):
    with _jax.named_scope("forward"):
        args = {**rest, TWIN_DIFF_INPUT: diff, **{k: w.astype(_WEIGHT_DTYPES[k]) for k, w in weights.items()}}
        y = _forward(args)
    with _jax.named_scope("loss_head"):
        err = _jnp.square(y.astype(_jnp.float32) - loss_target)
        return 0.5 * _jnp.sum(_jnp.mean(err, axis=-1)) if err.ndim else 0.5 * err


def _adamw(w, g, m, v):
    m = ADAM_B1 * m + (1.0 - ADAM_B1) * g
    v = ADAM_B2 * v + (1.0 - ADAM_B2) * _jnp.square(g)
    m_hat = m / (1.0 - ADAM_B1 ** ADAM_STEP)
    v_hat = v / (1.0 - ADAM_B2 ** ADAM_STEP)
    delta = -ADAM_LR * (m_hat / (_jnp.sqrt(v_hat) + ADAM_EPS) + ADAM_WD * w)
    return delta, m, v


def reference(x, ffn1_norm, ffn1_w_gate, ffn1_w_up, ffn1_w_down, mix_norm, w_in, pool_w, pool_scale, forget_bias, conv_w, conv_b, conv_ln_g, conv_ln_b, w_out, ffn2_norm, ffn2_w_gate, ffn2_w_up, ffn2_w_down, final_norm, loss_target, m_ffn1_norm, m_ffn1_w_gate, m_ffn1_w_up, m_ffn1_w_down, m_mix_norm, m_w_in, m_pool_w, m_pool_scale, m_forget_bias, m_conv_w, m_conv_b, m_conv_ln_g, m_conv_ln_b, m_w_out, m_ffn2_norm, m_ffn2_w_gate, m_ffn2_w_up, m_ffn2_w_down, m_final_norm, v_ffn1_norm, v_ffn1_w_gate, v_ffn1_w_up, v_ffn1_w_down, v_mix_norm, v_w_in, v_pool_w, v_pool_scale, v_forget_bias, v_conv_w, v_conv_b, v_conv_ln_g, v_conv_ln_b, v_w_out, v_ffn2_norm, v_ffn2_w_gate, v_ffn2_w_up, v_ffn2_w_down, v_final_norm):
    given = dict(x=x, ffn1_norm=ffn1_norm, ffn1_w_gate=ffn1_w_gate, ffn1_w_up=ffn1_w_up, ffn1_w_down=ffn1_w_down, mix_norm=mix_norm, w_in=w_in, pool_w=pool_w, pool_scale=pool_scale, forget_bias=forget_bias, conv_w=conv_w, conv_b=conv_b, conv_ln_g=conv_ln_g, conv_ln_b=conv_ln_b, w_out=w_out, ffn2_norm=ffn2_norm, ffn2_w_gate=ffn2_w_gate, ffn2_w_up=ffn2_w_up, ffn2_w_down=ffn2_w_down, final_norm=final_norm, loss_target=loss_target, m_ffn1_norm=m_ffn1_norm, m_ffn1_w_gate=m_ffn1_w_gate, m_ffn1_w_up=m_ffn1_w_up, m_ffn1_w_down=m_ffn1_w_down, m_mix_norm=m_mix_norm, m_w_in=m_w_in, m_pool_w=m_pool_w, m_pool_scale=m_pool_scale, m_forget_bias=m_forget_bias, m_conv_w=m_conv_w, m_conv_b=m_conv_b, m_conv_ln_g=m_conv_ln_g, m_conv_ln_b=m_conv_ln_b, m_w_out=m_w_out, m_ffn2_norm=m_ffn2_norm, m_ffn2_w_gate=m_ffn2_w_gate, m_ffn2_w_up=m_ffn2_w_up, m_ffn2_w_down=m_ffn2_w_down, m_final_norm=m_final_norm, v_ffn1_norm=v_ffn1_norm, v_ffn1_w_gate=v_ffn1_w_gate, v_ffn1_w_up=v_ffn1_w_up, v_ffn1_w_down=v_ffn1_w_down, v_mix_norm=v_mix_norm, v_w_in=v_w_in, v_pool_w=v_pool_w, v_pool_scale=v_pool_scale, v_forget_bias=v_forget_bias, v_conv_w=v_conv_w, v_conv_b=v_conv_b, v_conv_ln_g=v_conv_ln_g, v_conv_ln_b=v_conv_ln_b, v_w_out=v_w_out, v_ffn2_norm=v_ffn2_norm, v_ffn2_w_gate=v_ffn2_w_gate, v_ffn2_w_up=v_ffn2_w_up, v_ffn2_w_down=v_ffn2_w_down, v_final_norm=v_final_norm)
    weights = {n: given[n] for n in TWIN_WEIGHTS}
    shared = {n: given[n] for n in SHARED_INPUTS}
    per_example = {n: given[n] for n in ['x']}
    grad_fn = _jax.value_and_grad(_loss, argnums=(0, 1))

    def one_microbatch(ex, loss_target):
        ex = dict(ex)
        diff = ex.pop(TWIN_DIFF_INPUT)
        return grad_fn(weights, diff, {**shared, **ex}, loss_target)

    if N_MICROBATCH == 1:
        loss, (grad_w, grad_x) = one_microbatch(per_example, given["loss_target"])
    else:
        def body(carry, xs):
            loss_sum, grad_sum = carry
            l_k, (gw_k, gx_k) = one_microbatch(xs[0], xs[1])
            with _jax.named_scope("update"):
                return (loss_sum + l_k, _jax.tree.map(_jnp.add, grad_sum, gw_k)), gx_k

        init = (_jnp.zeros((), _jnp.float32), _jax.tree.map(_jnp.zeros_like, weights))
        (loss, grad_w), grad_x = _jax.lax.scan(body, init, (per_example, given["loss_target"]))
    with _jax.named_scope("update"):
        delta_w, new_m, new_v = {}, {}, {}
        for n in TWIN_WEIGHTS:
            delta_w[n], new_m[n], new_v[n] = _adamw(weights[n], grad_w[n], given["m_" + n], given["v_" + n])
    return (loss, grad_x, *[grad_w[n] for n in TWIN_WEIGHTS], *[delta_w[n] for n in TWIN_WEIGHTS],
            *[new_m[n] for n in TWIN_WEIGHTS], *[new_v[n] for n in TWIN_WEIGHTS])
```

```python
import math

import numpy as np
import jax
import jax.numpy as jnp
from jax import lax
from jax.experimental import pallas as pl
from jax.experimental.pallas import tpu as pltpu

F32 = jnp.float32
CDT = jnp.bfloat16
NORM_EPS = 1e-6
N_DEV = 8
LANES = 128
PACK_ALIGN = 8 * LANES
VMEM_LIMIT = 48 * 1024 * 1024

POOL_WINDOWS = (2, 4, 8, 16)
POOL_HALO = 16
CONV_K = 31
CONV_HALO = 32
HEAD_DIM = 64
N_HEADS = 8
N_PAIRS = N_HEADS // 2
ATT_SCALE = 1.0 / math.sqrt(HEAD_DIM)
NEG = -1e30

ADAM_LR, ADAM_B1, ADAM_B2, ADAM_EPS, ADAM_WD, ADAM_STEP = 0.001, 0.9, 0.999, 1e-08, 0.01, 10

REST_W = 896
REST_Z_BLK = 6


def _cp(sem):
    return pltpu.CompilerParams(dimension_semantics=sem, vmem_limit_bytes=VMEM_LIMIT)


def _tile(n, pref):
    t = min(n, pref)
    assert n % t == 0, (n, pref)
    return t


def _sigmoid(x):
    return 1.0 / (1.0 + jnp.exp(-x))


def _sds(shape, dtype):
    return jax.ShapeDtypeStruct(shape, dtype)


def _rms_fwd(x, g, name):
    T, D = x.shape
    tm = _tile(T, 1024)

    def body(x_ref, g_ref, o_ref):
        xv = x_ref[...]
        r = lax.rsqrt(jnp.mean(xv * xv, axis=-1, keepdims=True) + NORM_EPS)
        o_ref[...] = (xv * r * g_ref[...]).astype(o_ref.dtype)

    return pl.pallas_call(
        body, grid=(T // tm,),
        in_specs=[pl.BlockSpec((tm, D), lambda i: (i, 0)), pl.BlockSpec((1, D), lambda i: (0, 0))],
        out_specs=pl.BlockSpec((tm, D), lambda i: (i, 0)),
        out_shape=_sds((T, D), CDT), name=name, compiler_params=_cp(("parallel",)))(x, g)


def _rms_bwd(x, g, dh, gres, name):
    T, D = x.shape
    tm = _tile(T, 512)

    def body(x_ref, g_ref, dh_ref, gres_ref, gin_ref, dg_ref):
        i = pl.program_id(0)
        xv = x_ref[...]
        d = dh_ref[...]
        r = lax.rsqrt(jnp.mean(xv * xv, axis=-1, keepdims=True) + NORM_EPS)
        xh = xv * r
        dxh = d * g_ref[...]
        c = jnp.mean(dxh * xh, axis=-1, keepdims=True)
        gin_ref[...] = gres_ref[...] + r * (dxh - xh * c)
        part = jnp.sum(d * xh, axis=0, keepdims=True)

        @pl.when(i == 0)
        def _():
            dg_ref[...] = part

        @pl.when(i > 0)
        def _():
            dg_ref[...] += part

    row = pl.BlockSpec((tm, D), lambda i: (i, 0))
    vec = pl.BlockSpec((1, D), lambda i: (0, 0))
    return pl.pallas_call(
        body, grid=(T // tm,), in_specs=[row, vec, row, row], out_specs=[row, vec],
        out_shape=[_sds((T, D), F32), _sds((1, D), F32)], name=name, compiler_params=_cp(("arbitrary",)))(x, g, dh, gres)


def _loss_bwd(x, g, target, name):
    T, D = x.shape
    tm = _tile(T, 512)

    def body(x_ref, g_ref, t_ref, loss_ref, dx_ref, dg_ref):
        i = pl.program_id(0)
        xv = x_ref[...]
        gv = g_ref[...]
        r = lax.rsqrt(jnp.mean(xv * xv, axis=-1, keepdims=True) + NORM_EPS)
        xh = xv * r
        err = xh * gv - t_ref[...]
        lpart = 0.5 * jnp.sum(jnp.mean(err * err, axis=-1, keepdims=True), axis=0, keepdims=True)
        dy = err * (1.0 / D)
        dxh = dy * gv
        c = jnp.mean(dxh * xh, axis=-1, keepdims=True)
        dx_ref[...] = r * (dxh - xh * c)
        part = jnp.sum(dy * xh, axis=0, keepdims=True)
        lrow = jnp.broadcast_to(lpart, (1, LANES))

        @pl.when(i == 0)
        def _():
            dg_ref[...] = part
            loss_ref[...] = lrow

        @pl.when(i > 0)
        def _():
            dg_ref[...] += part
            loss_ref[...] += lrow

    row = pl.BlockSpec((tm, D), lambda i: (i, 0))
    vec = pl.BlockSpec((1, D), lambda i: (0, 0))
    return pl.pallas_call(
        body, grid=(T // tm,), in_specs=[row, vec, row],
        out_specs=[pl.BlockSpec((1, LANES), lambda i: (0, 0)), row, vec],
        out_shape=[_sds((1, LANES), F32), _sds((T, D), F32), _sds((1, D), F32)],
        name=name, compiler_params=_cp(("arbitrary",)))(x, g, target)


def _mm(pairs, *, name, res=None, alpha=1.0, out_dtype=F32, tm=512, tn=None):
    T = pairs[0][0].shape[0]
    N = pairs[0][1].shape[0] if pairs[0][2] else pairs[0][1].shape[1]
    tm = _tile(T, tm)
    tn = N if tn is None else _tile(N, tn)
    flags = [p[2] for p in pairs]
    n_in = 2 * len(pairs)

    def body(*refs):
        o_ref = refs[-1]
        acc = None
        for p, bt in enumerate(flags):
            a = refs[2 * p][...].astype(CDT)
            b = refs[2 * p + 1][...]
            dims = (((1,), (1,)), ((), ())) if bt else (((1,), (0,)), ((), ()))
            d = lax.dot_general(a, b, dims, preferred_element_type=F32)
            acc = d if acc is None else acc + d
        if alpha != 1.0:
            acc = acc * alpha
        if res is not None:
            acc = refs[n_in][...] + acc
        o_ref[...] = acc.astype(o_ref.dtype)

    in_specs, args = [], []
    for a, b, bt in pairs:
        K = a.shape[1]
        in_specs.append(pl.BlockSpec((tm, K), lambda i, j: (i, 0)))
        in_specs.append(pl.BlockSpec((tn, K), lambda i, j: (j, 0)) if bt else pl.BlockSpec((K, tn), lambda i, j: (0, j)))
        args += [a, b]
    if res is not None:
        in_specs.append(pl.BlockSpec((tm, tn), lambda i, j: (i, j)))
        args.append(res)
    return pl.pallas_call(
        body, grid=(T // tm, N // tn), in_specs=in_specs,
        out_specs=pl.BlockSpec((tm, tn), lambda i, j: (i, j)),
        out_shape=_sds((T, N), out_dtype), name=name, compiler_params=_cp(("parallel", "arbitrary")))(*args)


def _mm_tn(a, b, *, name, alpha=1.0, tk=512):
    T, M = a.shape
    N = b.shape[1]
    tm = M if M <= 1024 else M // 2
    tn = N if N <= 1536 else N // 2
    assert M % tm == 0 and N % tn == 0 and tm % LANES == 0 and tn % LANES == 0
    tk = _tile(T, tk)
    nk = T // tk

    def body(a_ref, b_ref, o_ref):
        k = pl.program_id(2)
        d = lax.dot_general(a_ref[...].astype(CDT), b_ref[...].astype(CDT), (((0,), (0,)), ((), ())),
                            preferred_element_type=F32)

        @pl.when(k == 0)
        def _():
            o_ref[...] = d

        @pl.when(k > 0)
        def _():
            o_ref[...] += d

        if alpha != 1.0:
            @pl.when(k == nk - 1)
            def _():
                o_ref[...] *= alpha

    return pl.pallas_call(
        body, grid=(M // tm, N // tn, nk),
        in_specs=[pl.BlockSpec((tk, tm), lambda i, j, k: (k, i)), pl.BlockSpec((tk, tn), lambda i, j, k: (k, j))],
        out_specs=pl.BlockSpec((tm, tn), lambda i, j, k: (i, j)),
        out_shape=_sds((M, N), F32), name=name, compiler_params=_cp(("parallel", "parallel", "arbitrary")))(a, b)


def _ffn_up(h, wg, wu, name):
    T, D = h.shape
    Fh = wg.shape[1]
    tm = _tile(T, 1024)
    tn = _tile(Fh, 256)

    def body(h_ref, wg_ref, wu_ref, a_ref, b_ref, s_ref):
        hv = h_ref[...]
        a = jnp.dot(hv, wg_ref[...], preferred_element_type=F32)
        b = jnp.dot(hv, wu_ref[...], preferred_element_type=F32)
        a_ref[...] = a
        b_ref[...] = b
        s_ref[...] = (a * _sigmoid(a) * b).astype(s_ref.dtype)

    wspec = pl.BlockSpec((D, tn), lambda i, j: (0, j))
    ospec = pl.BlockSpec((tm, tn), lambda i, j: (i, j))
    return pl.pallas_call(
        body, grid=(T // tm, Fh // tn),
        in_specs=[pl.BlockSpec((tm, D), lambda i, j: (i, 0)), wspec, wspec],
        out_specs=[ospec, ospec, ospec],
        out_shape=[_sds((T, Fh), F32), _sds((T, Fh), F32), _sds((T, Fh), CDT)],
        name=name, compiler_params=_cp(("parallel", "arbitrary")))(h, wg, wu)


def _ffn_bwd_ds(gout, wd, a, b, name):
    T, D = gout.shape
    Fh = wd.shape[0]
    tm = _tile(T, 512)
    tn = _tile(Fh, 256)

    def body(g_ref, wd_ref, a_ref, b_ref, da_ref, db_ref):
        dy = (0.5 * g_ref[...]).astype(CDT)
        ds = lax.dot_general(dy, wd_ref[...], (((1,), (1,)), ((), ())), preferred_element_type=F32)
        av = a_ref[...]
        sg = _sigmoid(av)
        da_ref[...] = (ds * b_ref[...] * (sg * (1.0 + av * (1.0 - sg)))).astype(da_ref.dtype)
        db_ref[...] = (ds * (av * sg)).astype(db_ref.dtype)

    ospec = pl.BlockSpec((tm, tn), lambda i, j: (i, j))
    return pl.pallas_call(
        body, grid=(T // tm, Fh // tn),
        in_specs=[pl.BlockSpec((tm, D), lambda i, j: (i, 0)), pl.BlockSpec((tn, D), lambda i, j: (j, 0)), ospec, ospec],
        out_specs=[ospec, ospec],
        out_shape=[_sds((T, Fh), CDT), _sds((T, Fh), CDT)],
        name=name, compiler_params=_cp(("parallel", "arbitrary")))(gout, wd, a, b)


def _ffn_fwd(x, gamma, wg, wu, wd, tag):
    h = _rms_fwd(x, gamma, f"{tag}_norm")
    a, b, s = _ffn_up(h, wg, wu, f"{tag}_up")
    y = _mm([(s, wd, False)], res=x, alpha=0.5, tn=512, name=f"{tag}_down")
    return y, (x, h, a, b, s)


def _ffn_bwd(saved, gamma, wg, wu, wd, gout, tag):
    x, h, a, b, s = saved
    da, db = _ffn_bwd_ds(gout, wd, a, b, f"{tag}_bwd_ds")
    dwd = _mm_tn(s, gout, alpha=0.5, name=f"{tag}_dwd")
    dwg = _mm_tn(h, da, name=f"{tag}_dwg")
    dwu = _mm_tn(h, db, name=f"{tag}_dwu")
    dh = _mm([(da, wg, True), (db, wu, True)], tn=512, name=f"{tag}_dh")
    gin, dgamma = _rms_bwd(x, gamma, dh, gout, f"{tag}_norm_bwd")
    return gin, dgamma, dwg, dwu, dwd


def _fgate_fwd(rest, bias, name, bt=512):
    T = rest.shape[0]
    bt = _tile(T, bt)

    def body(z_ref, b_ref, fc_ref, ft_ref, carry):
        i = pl.program_id(0)

        @pl.when(i == 0)
        def _():
            carry[...] = jnp.zeros_like(carry)

        zb = z_ref[...] + b_ref[...]
        e = jnp.exp(-jnp.abs(zb))
        u = 1.0 + e
        log1p_e = jnp.where(u == 1.0, e, jnp.log(u) * (e / (u - 1.0)))
        x = jnp.minimum(zb, 0.0) - log1p_e
        row = lax.broadcasted_iota(jnp.int32, x.shape, 0)
        sh = 1
        while sh < bt:
            x = x + jnp.where(row >= sh, pltpu.roll(x, sh, 0), 0.0)
            sh *= 2
        f = x + carry[...]
        carry[...] = f[bt - 1:bt, :]
        fc_ref[...] = f
        ft_ref[...] = jnp.transpose(f)[0:N_HEADS, :]

    return pl.pallas_call(
        body, grid=(T // bt,),
        in_specs=[pl.BlockSpec((bt, LANES), lambda i: (i, REST_Z_BLK)), pl.BlockSpec((1, LANES), lambda i: (0, 0))],
        out_specs=[pl.BlockSpec((bt, LANES), lambda i: (i, 0)), pl.BlockSpec((N_HEADS, bt), lambda i: (0, i))],
        out_shape=[_sds((T, LANES), F32), _sds((N_HEADS, T), F32)],
        scratch_shapes=[pltpu.VMEM((1, LANES), F32)],
        name=name, compiler_params=_cp(("arbitrary",)))(rest, bias)


def _fgate_bwd(dfk, rest, bias, name, bt=512):
    T = rest.shape[0]
    bt = _tile(T, bt)
    nb = T // bt

    def body(df_ref, z_ref, b_ref, dz_ref, db_ref, carry):
        i = pl.program_id(0)

        @pl.when(i == 0)
        def _():
            carry[...] = jnp.zeros_like(carry)

        dfv = df_ref[...]
        lane = lax.broadcasted_iota(jnp.int32, (bt, LANES), 1)
        x = jnp.zeros((bt, LANES), F32)
        for h in range(N_HEADS):
            x = jnp.where(lane == h, dfv[:, HEAD_DIM * h:HEAD_DIM * h + 1], x)
        row = lax.broadcasted_iota(jnp.int32, x.shape, 0)
        sh = 1
        while sh < bt:
            x = x + jnp.where(row + sh < bt, pltpu.roll(x, bt - sh, 0), 0.0)
            sh *= 2
        dlf = x + carry[...]
        carry[...] = dlf[0:1, :]
        zb = z_ref[...] + b_ref[...]
        dz = jnp.where(lane < N_HEADS, dlf * _sigmoid(-zb), 0.0)
        dz_ref[...] = dz.astype(dz_ref.dtype)
        part = jnp.sum(dz, axis=0, keepdims=True)

        @pl.when(i == 0)
        def _():
            db_ref[...] = part

        @pl.when(i > 0)
        def _():
            db_ref[...] += part

    return pl.pallas_call(
        body, grid=(nb,),
        in_specs=[pl.BlockSpec((bt, 4 * LANES), lambda i: (nb - 1 - i, 0)),
                  pl.BlockSpec((bt, LANES), lambda i: (nb - 1 - i, REST_Z_BLK)),
                  pl.BlockSpec((1, LANES), lambda i: (0, 0))],
        out_specs=[pl.BlockSpec((bt, LANES), lambda i: (nb - 1 - i, 0)), pl.BlockSpec((1, LANES), lambda i: (0, 0))],
        out_shape=[_sds((T, LANES), CDT), _sds((1, LANES), F32)],
        scratch_shapes=[pltpu.VMEM((1, LANES), F32)],
        name=name, compiler_params=_cp(("arbitrary",)))(dfk, rest, bias)


def _by_group(vals, lane):
    out = vals[-1]
    for g in range(len(vals) - 2, -1, -1):
        out = jnp.where(lane // 64 == g, vals[g], out)
    return out


def _pool_counts(t0, n, lane):
    t = t0 + lax.broadcasted_iota(jnp.int32, (n, 256), 0)
    return _by_group([jnp.minimum(t + 1, w) for w in POOL_WINDOWS], lane).astype(F32)


def _pooled(u, halo, i, bt):
    lane = lax.broadcasted_iota(jnp.int32, (bt, 256), 1)
    ext = jnp.concatenate([jnp.where(i > 0, halo, 0.0), u], axis=0)
    sums, s, sh = [], ext, 1
    for _ in POOL_WINDOWS:
        s = s + pltpu.roll(s, sh, 0)
        sums.append(s[POOL_HALO:, :])
        sh *= 2
    return _by_group(sums, lane) / _pool_counts(i * bt, bt, lane) - u


def _pool_fwd(rest, wbd, scale, name, bt=512):
    T = rest.shape[0]
    bt = _tile(T, bt)
    hb = bt // POOL_HALO

    def body(u_ref, halo_ref, w_ref, sc_ref, o_ref):
        i = pl.program_id(0)
        pooled = _pooled(u_ref[...], halo_ref[...], i, bt)
        mixed = jnp.dot(pooled.astype(CDT), w_ref[...], preferred_element_type=F32)
        o_ref[...] = (mixed * sc_ref[...]).astype(o_ref.dtype)

    return pl.pallas_call(
        body, grid=(T // bt,),
        in_specs=[pl.BlockSpec((bt, 256), lambda i: (i, 0)),
                  pl.BlockSpec((POOL_HALO, 256), lambda i: (jnp.maximum(i * hb - 1, 0), 0)),
                  pl.BlockSpec((256, 256), lambda i: (0, 0)), pl.BlockSpec((1, 256), lambda i: (0, 0))],
        out_specs=pl.BlockSpec((bt, 256), lambda i: (i, 0)),
        out_shape=_sds((T, 256), CDT), name=name, compiler_params=_cp(("parallel",)))(rest, rest, wbd, scale)


def _pool_bwd(dcat, rest, wbd, scale, name, bt=512):
    T = rest.shape[0]
    bt = _tile(T, bt)
    hb = bt // POOL_HALO
    nb = T // bt
    n = bt + POOL_HALO

    def body(dy_ref, dyn_ref, u_ref, halo_ref, w_ref, sc_ref, du_ref, dw_ref, dsc_ref):
        i = pl.program_id(0)
        lane = lax.broadcasted_iota(jnp.int32, (bt, 256), 1)
        w = w_ref[...]
        sc = sc_ref[...]
        pooled = _pooled(u_ref[...], halo_ref[...], i, bt)
        pooled_c = pooled.astype(CDT)
        mixed = jnp.dot(pooled_c, w, preferred_element_type=F32)
        dy = dy_ref[...]
        dm = (dy * sc).astype(CDT)
        dsc = jnp.sum(dy * mixed, axis=0, keepdims=True)
        dw = lax.dot_general(pooled_c, dm, (((0,), (0,)), ((), ())), preferred_element_type=F32)
        nt = (((1,), (1,)), ((), ()))
        dpl = lax.dot_general(dm, w, nt, preferred_element_type=F32)
        dmn = (jnp.where(i < nb - 1, dyn_ref[...], 0.0) * sc).astype(CDT)
        dpln = lax.dot_general(dmn, w, nt, preferred_element_type=F32)
        lane_h = lax.broadcasted_iota(jnp.int32, (POOL_HALO, 256), 1)
        ext = jnp.concatenate([dpl / _pool_counts(i * bt, bt, lane),
                               dpln / _pool_counts((i + 1) * bt, POOL_HALO, lane_h)], axis=0)
        sums, s, sh = [], ext, 1
        for _ in POOL_WINDOWS:
            s = s + pltpu.roll(s, n - sh, 0)
            sums.append(s[0:bt, :])
            sh *= 2
        du_ref[...] = (_by_group(sums, lane) - dpl).astype(du_ref.dtype)

        @pl.when(i == 0)
        def _():
            dw_ref[...] = dw
            dsc_ref[...] = dsc

        @pl.when(i > 0)
        def _():
            dw_ref[...] += dw
            dsc_ref[...] += dsc

    full = pl.BlockSpec((256, 256), lambda i: (0, 0))
    vec = pl.BlockSpec((1, 256), lambda i: (0, 0))
    return pl.pallas_call(
        body, grid=(nb,),
        in_specs=[pl.BlockSpec((bt, 256), lambda i: (i, 0)),
                  pl.BlockSpec((POOL_HALO, 256), lambda i: (jnp.minimum((i + 1) * hb, nb * hb - 1), 0)),
                  pl.BlockSpec((bt, 256), lambda i: (i, 0)),
                  pl.BlockSpec((POOL_HALO, 256), lambda i: (jnp.maximum(i * hb - 1, 0), 0)),
                  full, vec],
        out_specs=[pl.BlockSpec((bt, 256), lambda i: (i, 0)), full, vec],
        out_shape=[_sds((T, 256), CDT), _sds((256, 256), F32), _sds((1, 256), F32)],
        name=name, compiler_params=_cp(("arbitrary",)))(dcat, dcat, rest, rest, wbd, scale)


def _glu_ext(a_ref, g_ref, ah_ref, gh_ref, i):
    u = a_ref[...] * _sigmoid(g_ref[...])
    uh = jnp.where(i > 0, ah_ref[...] * _sigmoid(gh_ref[...]), 0.0)
    return jnp.concatenate([uh, u], axis=0)


def _conv_fwd(rest, cw, cb, lg, lb, name, bt=512):
    T = rest.shape[0]
    bt = _tile(T, bt)
    hb = bt // CONV_HALO

    def body(a_ref, g_ref, ah_ref, gh_ref, cw_ref, cb_ref, lg_ref, lb_ref, o_ref, y_ref):
        i = pl.program_id(0)
        ext = _glu_ext(a_ref, g_ref, ah_ref, gh_ref, i)
        w = cw_ref[...]
        acc = w[CONV_K - 1:CONV_K, :] * ext
        for k in range(CONV_K - 1):
            acc = acc + w[k:k + 1, :] * pltpu.roll(ext, CONV_K - 1 - k, 0)
        y = acc[CONV_HALO:, :] + cb_ref[...]
        y_ref[...] = y
        yc = y - jnp.mean(y, axis=-1, keepdims=True)
        yn = yc * lax.rsqrt(jnp.mean(yc * yc, axis=-1, keepdims=True) + NORM_EPS)
        z = yn * lg_ref[...] + lb_ref[...]
        o_ref[...] = (z * _sigmoid(z)).astype(o_ref.dtype)

    def cur(c):
        return pl.BlockSpec((bt, 256), lambda i: (i, c))

    def prev(c):
        return pl.BlockSpec((CONV_HALO, 256), lambda i: (jnp.maximum(i * hb - 1, 0), c))

    vec = pl.BlockSpec((1, 256), lambda i: (0, 0))
    return pl.pallas_call(
        body, grid=(T // bt,),
        in_specs=[cur(1), cur(2), prev(1), prev(2), pl.BlockSpec((CONV_HALO, 256), lambda i: (0, 0)), vec, vec, vec],
        out_specs=[pl.BlockSpec((bt, 256), lambda i: (i, 0)), pl.BlockSpec((bt, 256), lambda i: (i, 0))],
        out_shape=[_sds((T, 256), CDT), _sds((T, 256), F32)],
        name=name, compiler_params=_cp(("parallel",)))(rest, rest, rest, rest, cw, cb, lg, lb)


def _conv_bwd(dcat, yconv, rest, cw, lg, lb, name, bt=512):
    T = rest.shape[0]
    bt = _tile(T, bt)
    hb = bt // CONV_HALO
    nb = T // bt
    n = bt + CONV_HALO

    def body(dy_ref, dyn_ref, y_ref, yn_ref, a_ref, g_ref, ah_ref, gh_ref, cw_ref, lg_ref, lb_ref,
             da_ref, dg_ref, dcw_ref, dcb_ref, dlg_ref, dlb_ref):
        i = pl.program_id(0)
        lgv = lg_ref[...]
        lbv = lb_ref[...]

        def ln_swish_bwd(dout, y):
            yc = y - jnp.mean(y, axis=-1, keepdims=True)
            rs = lax.rsqrt(jnp.mean(yc * yc, axis=-1, keepdims=True) + NORM_EPS)
            yn = yc * rs
            z = yn * lgv + lbv
            sg = _sigmoid(z)
            dz = dout * (sg * (1.0 + z * (1.0 - sg)))
            dyn = dz * lgv
            dyc = rs * (dyn - jnp.mean(dyn, axis=-1, keepdims=True) - yn * jnp.mean(dyn * yn, axis=-1, keepdims=True))
            return dyc, dz, yn

        dyc, dz, yn = ln_swish_bwd(dy_ref[...], y_ref[...])
        dyc_next, _, _ = ln_swish_bwd(dyn_ref[...], yn_ref[...])
        dyc_next = jnp.where(i < nb - 1, dyc_next, 0.0)
        ext_u = _glu_ext(a_ref, g_ref, ah_ref, gh_ref, i)
        ext_d = jnp.concatenate([dyc, dyc_next], axis=0)
        w = cw_ref[...]
        du = w[CONV_K - 1:CONV_K, :] * ext_d
        rows = []
        for k in range(CONV_K):
            s = CONV_K - 1 - k
            if s > 0:
                du = du + w[k:k + 1, :] * pltpu.roll(ext_d, n - s, 0)
                us = pltpu.roll(ext_u, s, 0)[CONV_HALO:, :]
            else:
                us = ext_u[CONV_HALO:, :]
            rows.append(jnp.sum(dyc * us, axis=0, keepdims=True))
        rows.append(jnp.zeros((1, 256), F32))
        dcw = jnp.concatenate(rows, axis=0)
        du = du[0:bt, :]
        av = a_ref[...]
        sg = _sigmoid(g_ref[...])
        da_ref[...] = (du * sg).astype(da_ref.dtype)
        dg_ref[...] = (du * av * (sg * (1.0 - sg))).astype(dg_ref.dtype)
        dcb = jnp.sum(dyc, axis=0, keepdims=True)
        dlg = jnp.sum(dz * yn, axis=0, keepdims=True)
        dlb = jnp.sum(dz, axis=0, keepdims=True)

        @pl.when(i == 0)
        def _():
            dcw_ref[...] = dcw
            dcb_ref[...] = dcb
            dlg_ref[...] = dlg
            dlb_ref[...] = dlb

        @pl.when(i > 0)
        def _():
            dcw_ref[...] += dcw
            dcb_ref[...] += dcb
            dlg_ref[...] += dlg
            dlb_ref[...] += dlb

    def cur(c):
        return pl.BlockSpec((bt, 256), lambda i: (i, c))

    def prev(c):
        return pl.BlockSpec((CONV_HALO, 256), lambda i: (jnp.maximum(i * hb - 1, 0), c))

    def nxt(c):
        return pl.BlockSpec((CONV_HALO, 256), lambda i: (jnp.minimum((i + 1) * hb, nb * hb - 1), c))

    vec = pl.BlockSpec((1, 256), lambda i: (0, 0))
    wfull = pl.BlockSpec((CONV_HALO, 256), lambda i: (0, 0))
    return pl.pallas_call(
        body, grid=(nb,),
        in_specs=[cur(3), nxt(3), cur(0), nxt(0), cur(1), cur(2), prev(1), prev(2), wfull, vec, vec],
        out_specs=[cur(0), cur(0), wfull, vec, vec, vec],
        out_shape=[_sds((T, 256), CDT), _sds((T, 256), CDT), _sds((CONV_HALO, 256), F32),
                   _sds((1, 256), F32), _sds((1, 256), F32), _sds((1, 256), F32)],
        name=name, compiler_params=_cp(("arbitrary",)))(dcat, dcat, yconv, yconv, rest, rest, rest, rest, cw, lg, lb)


def _half_mask(shape, a):
    lane = lax.broadcasted_iota(jnp.int32, shape, 1)
    return (lane // HEAD_DIM) == a


def _attn_fwd(qkv, fcol, frow, name, blk=512):
    T = qkv.shape[0]
    blk = _tile(T, blk)
    nq = T // blk
    nt = (((1,), (1,)), ((), ()))

    def body(q_ref, k_ref, v_ref, fc_ref, fr_ref, o_ref, olo_ref, lse_ref):
        p_id = pl.program_id(0)
        i = pl.program_id(1)
        q2 = q_ref[...]
        fc = fc_ref[...]
        lane = lax.broadcasted_iota(jnp.int32, (blk, LANES), 1)
        tri = lax.broadcasted_iota(jnp.int32, (blk, blk), 1) <= lax.broadcasted_iota(jnp.int32, (blk, blk), 0)
        outs, los, lses = [], [], []
        for a in range(2):
            qa = jnp.where(_half_mask(q2.shape, a), q2, jnp.zeros_like(q2)) * ATT_SCALE
            fq = jnp.sum(jnp.where(lane == 2 * p_id + a, fc, 0.0), axis=1, keepdims=True)

            def tile(j, carry, masked):
                m, l, acc, acc_lo = carry
                kj = k_ref[pl.ds(pl.multiple_of(j * blk, blk), blk), :]
                vj = v_ref[pl.ds(pl.multiple_of(j * blk, blk), blk), :]
                fk = fr_ref[a:a + 1, pl.ds(pl.multiple_of(j * blk, blk), blk)]
                s = lax.dot_general(qa, kj, nt, preferred_element_type=F32) + (fq - fk)
                if masked:
                    s = jnp.where(tri, s, NEG)
                m_new = jnp.maximum(m, jnp.max(s, axis=1, keepdims=True))
                alpha = jnp.exp(m - m_new)
                pr = jnp.exp(s - m_new)
                l = alpha * l + jnp.sum(pr, axis=1, keepdims=True)
                pr_hi = pr.astype(CDT)
                pr_lo = (pr - pr_hi.astype(F32)).astype(CDT)
                acc = alpha * acc + jnp.dot(pr_hi, vj, preferred_element_type=F32)
                acc_lo = alpha * acc_lo + jnp.dot(pr_lo, vj, preferred_element_type=F32)
                return m_new, l, acc, acc_lo

            zero = jnp.zeros((blk, LANES), F32)
            init = (jnp.full((blk, 1), NEG, F32), jnp.zeros((blk, 1), F32), zero, zero)
            carry = lax.fori_loop(0, i, lambda j, c: tile(j, c, False), init)
            m, l, acc, acc_lo = tile(i, carry, True)
            outs.append(acc / l)
            los.append(acc_lo / l)
            lses.append(m + jnp.log(l))
        lo = lane < HEAD_DIM
        o_ref[...] = jnp.where(lo, outs[0], outs[1])
        olo_ref[...] = jnp.where(lo, los[0], los[1])
        lse_t = jnp.transpose(jnp.where(lo, lses[0], lses[1]))
        lse_ref[...] = jnp.concatenate([lse_t[0:1, :], lse_t[HEAD_DIM:HEAD_DIM + 1, :]], axis=0)

    return pl.pallas_call(
        body, grid=(N_PAIRS, nq),
        in_specs=[pl.BlockSpec((blk, LANES), lambda p, i: (i, p)),
                  pl.BlockSpec((T, LANES), lambda p, i: (0, N_PAIRS + p)),
                  pl.BlockSpec((T, LANES), lambda p, i: (0, 2 * N_PAIRS + p)),
                  pl.BlockSpec((blk, LANES), lambda p, i: (i, 0)),
                  pl.BlockSpec((None, 2, T), lambda p, i: (p, 0, 0))],
        out_specs=[pl.BlockSpec((blk, LANES), lambda p, i: (i, p)), pl.BlockSpec((blk, LANES), lambda p, i: (i, p)),
                   pl.BlockSpec((None, 2, blk), lambda p, i: (p, 0, i))],
        out_shape=[_sds((T, N_PAIRS * LANES), F32), _sds((T, N_PAIRS * LANES), F32), _sds((N_PAIRS, 2, T), F32)],
        name=name, compiler_params=_cp(("parallel", "arbitrary")))(qkv, qkv, qkv, fcol, frow)


def _attn_delta(dcat, o, o_lo, name, blk=512):
    T = o.shape[0]
    blk = _tile(T, blk)

    def body(d_ref, o_ref, olo_ref, out_ref):
        prod = d_ref[:, 256:768].astype(CDT).astype(F32) * (o_ref[...] + olo_ref[...])
        pt = jnp.transpose(prod)
        out_ref[...] = jnp.sum(pt.reshape(N_HEADS, HEAD_DIM, blk), axis=1)

    return pl.pallas_call(
        body, grid=(T // blk,),
        in_specs=[pl.BlockSpec((blk, 1024), lambda i: (i, 0)), pl.BlockSpec((blk, 512), lambda i: (i, 0)),
                  pl.BlockSpec((blk, 512), lambda i: (i, 0))],
        out_specs=pl.BlockSpec((N_HEADS, blk), lambda i: (0, i)),
        out_shape=_sds((N_HEADS, T), F32), name=name, compiler_params=_cp(("parallel",)))(dcat, o, o_lo)


def _attn_bwd(qkv, dcat, fcol, frow, lse, delta, name, blk=512):
    T = qkv.shape[0]
    blk = _tile(T, blk)
    nq = T // blk
    nt = (((1,), (1,)), ((), ()))
    tn = (((0,), (0,)), ((), ()))

    def body(q_ref, do_ref, k_ref, v_ref, fc_ref, fr_ref, lse_ref, dl_ref, dq_ref, dk_ref, dv_ref, df_ref):
        p_id = pl.program_id(0)
        j = pl.program_id(1)

        @pl.when(j == 0)
        def _():
            dq_ref[...] = jnp.zeros_like(dq_ref)

        k2 = k_ref[...]
        v2 = v_ref[...]
        fc = fc_ref[...]
        lane = lax.broadcasted_iota(jnp.int32, (blk, LANES), 1)
        tri = lax.broadcasted_iota(jnp.int32, (blk, blk), 0) <= lax.broadcasted_iota(jnp.int32, (blk, blk), 1)
        dks, dvs, dfs = [], [], []
        for a in range(2):
            hm = _half_mask(k2.shape, a)
            ka = jnp.where(hm, k2, jnp.zeros_like(k2)) * ATT_SCALE
            va = jnp.where(hm, v2, jnp.zeros_like(v2))
            fk = jnp.sum(jnp.where(lane == 2 * p_id + a, fc, 0.0), axis=1, keepdims=True)

            def tile(i, carry, masked):
                dk_acc, dv_acc, df_acc = carry
                rows = pl.ds(pl.multiple_of(i * blk, blk), blk)
                qi = q_ref[rows, :]
                doi = do_ref[rows, :].astype(CDT)
                st = lax.dot_general(ka, qi, nt, preferred_element_type=F32)
                e = st + (fr_ref[a:a + 1, rows] - fk) - lse_ref[a:a + 1, rows]
                if masked:
                    e = jnp.where(tri, e, NEG)
                pt = jnp.exp(e)
                dpt = lax.dot_general(va, doi, nt, preferred_element_type=F32)
                ds32 = pt * (dpt - dl_ref[a:a + 1, rows])
                dst = ds32.astype(CDT)
                df_acc = df_acc + jnp.sum(ds32, axis=1, keepdims=True)
                dv_acc = dv_acc + jnp.dot(pt.astype(CDT), doi, preferred_element_type=F32)
                dk_acc = dk_acc + jnp.dot(dst, qi, preferred_element_type=F32)
                dq_ref[rows, :] += lax.dot_general(dst, ka, tn, preferred_element_type=F32)
                return dk_acc, dv_acc, df_acc

            init = (jnp.zeros((blk, LANES), F32), jnp.zeros((blk, LANES), F32), jnp.zeros((blk, 1), F32))
            carry = tile(j, init, True)
            dk_acc, dv_acc, df_acc = lax.fori_loop(j + 1, nq, lambda i, c: tile(i, c, False), carry)
            dks.append(dk_acc)
            dvs.append(dv_acc)
            dfs.append(df_acc)
        lo = lane < HEAD_DIM
        dk_ref[...] = jnp.where(lo, dks[0], dks[1]) * ATT_SCALE
        dv_ref[...] = jnp.where(lo, dvs[0], dvs[1])
        df_ref[...] = -jnp.where(lo, dfs[0], dfs[1])

    res = pl.BlockSpec((T, LANES), lambda p, j: (0, p))
    rows = pl.BlockSpec((None, 2, T), lambda p, j: (p, 0, 0))
    kv_out = pl.BlockSpec((blk, LANES), lambda p, j: (j, p))
    return pl.pallas_call(
        body, grid=(N_PAIRS, nq),
        in_specs=[res, pl.BlockSpec((T, LANES), lambda p, j: (0, 2 + p)),
                  pl.BlockSpec((blk, LANES), lambda p, j: (j, N_PAIRS + p)),
                  pl.BlockSpec((blk, LANES), lambda p, j: (j, 2 * N_PAIRS + p)),
                  pl.BlockSpec((blk, LANES), lambda p, j: (j, 0)), rows, rows, rows],
        out_specs=[res, kv_out, kv_out, kv_out],
        out_shape=[_sds((T, N_PAIRS * LANES), F32)] * 4,
        name=name, compiler_params=_cp(("parallel", "arbitrary")))(qkv, dcat, qkv, qkv, fcol, frow, lse, delta)


def _mixer_fwd(x, wts, tag):
    T = x.shape[0]
    h = _rms_fwd(x, wts["mix_norm"], f"{tag}_norm")
    qkv = _mm([(h, wts["win_qkv"], False)], out_dtype=CDT, tm=1024, tn=768, name=f"{tag}_in_qkv")
    rest = _mm([(h, wts["win_rest"], False)], tm=1024, name=f"{tag}_in_rest")
    fcol, frow8 = _fgate_fwd(rest, wts["fbias"], f"{tag}_fgate")
    frow = frow8.reshape(N_PAIRS, 2, T)
    ya = _pool_fwd(rest, wts["pool_wbd"], wts["pool_scale"], f"{tag}_pool")
    o, o_lo, lse = _attn_fwd(qkv, fcol, frow, f"{tag}_attn")
    yc, yconv = _conv_fwd(rest, wts["conv_w"], wts["conv_b"], wts["conv_ln_g"], wts["conv_ln_b"], f"{tag}_conv")
    cat = jnp.concatenate([ya, o.astype(CDT), yc], axis=1)
    y = _mm([(cat, wts["w_out"], False)], res=x, tn=512, name=f"{tag}_out")
    return y, (x, h, qkv, rest, fcol, frow, o, o_lo, lse, yconv, cat)


def _mixer_bwd(saved, wts, gout, tag):
    x, h, qkv, rest, fcol, frow, o, o_lo, lse, yconv, cat = saved
    T = x.shape[0]
    dcat = _mm([(gout, wts["w_out"], True)], tn=512, name=f"{tag}_dcat")
    dwout = _mm_tn(cat, gout, name=f"{tag}_dwout")
    du, dpw, dpsc = _pool_bwd(dcat, rest, wts["pool_wbd"], wts["pool_scale"], f"{tag}_pool_bwd")
    delta = _attn_delta(dcat, o, o_lo, f"{tag}_attn_delta").reshape(N_PAIRS, 2, T)
    dq, dk, dv, dfk = _attn_bwd(qkv, dcat, fcol, frow, lse, delta, f"{tag}_attn_bwd")
    dz, dfb = _fgate_bwd(dfk, rest, wts["fbias"], f"{tag}_fgate_bwd")
    da, dg, dcw, dcb, dlg, dlb = _conv_bwd(dcat, yconv, rest, wts["conv_w"], wts["conv_ln_g"], wts["conv_ln_b"],
                                           f"{tag}_conv_bwd")
    dp_qkv = jnp.concatenate([dq, dk, dv], axis=1).astype(CDT)
    dp_rest = jnp.concatenate([du, da, dg, dz], axis=1)
    dwin_qkv = _mm_tn(h, dp_qkv, name=f"{tag}_dwin_qkv")
    dwin_rest = _mm_tn(h, dp_rest, name=f"{tag}_dwin_rest")
    dh = _mm([(dp_qkv, wts["win_qkv"], True), (dp_rest, wts["win_rest"], True)], tn=512, name=f"{tag}_dh")
    gin, dgamma = _rms_bwd(x, wts["mix_norm"], dh, gout, f"{tag}_norm_bwd")
    dwin = jnp.concatenate([dwin_rest[:, 0:256], dwin_qkv, dwin_rest[:, 768:776], dwin_rest[:, 256:768]], axis=1)
    dpool_w = jnp.stack([dpw[64 * g:64 * g + 64, 64 * g:64 * g + 64] for g in range(4)])
    grads = dict(mix_norm=dgamma[0], w_in=dwin, pool_w=dpool_w, pool_scale=dpsc[0], forget_bias=dfb[0, 0:N_HEADS],
                 conv_w=dcw[0:CONV_K], conv_b=dcb[0], conv_ln_g=dlg[0], conv_ln_b=dlb[0], w_out=dwout)
    return gin, grads


def _layer_weights(full, l):
    win = full["w_in"][l]
    zpad = jnp.zeros((win.shape[0], REST_W - 776), win.dtype)
    pw = full["pool_w"][l].astype(CDT)
    wbd = jnp.zeros((256, 256), CDT)
    for g in range(4):
        wbd = lax.dynamic_update_slice(wbd, pw[g], (64 * g, 64 * g))
    return dict(
        ffn1_norm=full["ffn1_norm"][l][None], ffn1_w_gate=full["ffn1_w_gate"][l], ffn1_w_up=full["ffn1_w_up"][l],
        ffn1_w_down=full["ffn1_w_down"][l],
        ffn2_norm=full["ffn2_norm"][l][None], ffn2_w_gate=full["ffn2_w_gate"][l], ffn2_w_up=full["ffn2_w_up"][l],
        ffn2_w_down=full["ffn2_w_down"][l],
        mix_norm=full["mix_norm"][l][None],
        win_qkv=win[:, 256:1792],
        win_rest=jnp.concatenate([win[:, 0:256], win[:, 1800:2312], win[:, 1792:1800], zpad], axis=1),
        fbias=jnp.pad(full["forget_bias"][l], (0, LANES - N_HEADS))[None],
        pool_wbd=wbd, pool_scale=full["pool_scale"][l][None],
        conv_w=jnp.pad(full["conv_w"][l], ((0, CONV_HALO - CONV_K), (0, 0))), conv_b=full["conv_b"][l][None],
        conv_ln_g=full["conv_ln_g"][l][None], conv_ln_b=full["conv_ln_b"][l][None],
        w_out=full["w_out"][l])


def _local_step(x, target, full):
    depth = full["ffn1_norm"].shape[0]
    lw = [_layer_weights(full, l) for l in range(depth)]
    saved = []
    for l in range(depth):
        w = lw[l]
        x, s1 = _ffn_fwd(x, w["ffn1_norm"], w["ffn1_w_gate"], w["ffn1_w_up"], w["ffn1_w_down"], f"l{l}_ffn1")
        x, s2 = _mixer_fwd(x, w, f"l{l}_mix")
        x, s3 = _ffn_fwd(x, w["ffn2_norm"], w["ffn2_w_gate"], w["ffn2_w_up"], w["ffn2_w_down"], f"l{l}_ffn2")
        saved.append((s1, s2, s3))
    loss, g, dfinal = _loss_bwd(x, full["final_norm"][None], target, "loss_head")
    per_layer = []
    for l in reversed(range(depth)):
        w = lw[l]
        s1, s2, s3 = saved[l]
        g, dn2, dwg2, dwu2, dwd2 = _ffn_bwd(s3, w["ffn2_norm"], w["ffn2_w_gate"], w["ffn2_w_up"], w["ffn2_w_down"], g,
                                            f"l{l}_ffn2")
        g, gm = _mixer_bwd(s2, w, g, f"l{l}_mix")
        g, dn1, dwg1, dwu1, dwd1 = _ffn_bwd(s1, w["ffn1_norm"], w["ffn1_w_gate"], w["ffn1_w_up"], w["ffn1_w_down"], g,
                                            f"l{l}_ffn1")
        gm.update(ffn1_norm=dn1[0], ffn1_w_gate=dwg1, ffn1_w_up=dwu1, ffn1_w_down=dwd1,
                  ffn2_norm=dn2[0], ffn2_w_gate=dwg2, ffn2_w_up=dwu2, ffn2_w_down=dwd2)
        per_layer.append(gm)
    per_layer.reverse()
    grads = {k: jnp.stack([pl_[k] for pl_ in per_layer]) for k in per_layer[0]}
    grads["final_norm"] = dfinal[0]
    return loss, g, grads


def _mesh_pos():
    return lax.axis_index("x"), lax.axis_index("y"), lax.axis_index("c")


def _all_gather(shard, name):
    R = shard.shape[0]

    def body(x_ref, out_ref, send_sems, recv_sems, local_sem):
        x, y, c = _mesh_pos()
        me, sibling = (x, y, c), (x, y, 1 - c)
        chips = [(1 - x, y), (x, 1 - y), (1 - x, 1 - y)]

        def rows(px, py, pc):
            return out_ref.at[4 * px + 2 * py + pc]

        def copy(k, block, to, src=None):
            return pltpu.make_async_remote_copy(
                src_ref=rows(*block) if src is None else src, dst_ref=rows(*block),
                send_sem=send_sems.at[k], recv_sem=recv_sems.at[k],
                device_id=to, device_id_type=pl.DeviceIdType.MESH)

        mine = pltpu.make_async_copy(x_ref, rows(*me), local_sem)
        mine.start()
        first = [copy(0, me, sibling, src=x_ref)]
        first += [copy(1 + j, me, (*chip, c), src=x_ref) for j, chip in enumerate(chips)]
        for cp in first:
            cp.start()
        passed = [copy(4 + j, (*chip, c), sibling) for j, chip in enumerate(chips)]
        for j, chip in enumerate(chips):
            copy(1 + j, (*chip, c), me).wait_recv()
            passed[j].start()
        copy(0, sibling, me).wait_recv()
        for j, chip in enumerate(chips):
            copy(4 + j, (*chip, 1 - c), me).wait_recv()
        for cp in first + passed:
            cp.wait_send()
        mine.wait()

    return pl.pallas_call(
        body, out_shape=_sds((N_DEV, R, LANES), shard.dtype),
        in_specs=[pl.BlockSpec(memory_space=pl.ANY)], out_specs=pl.BlockSpec(memory_space=pl.ANY),
        scratch_shapes=[pltpu.SemaphoreType.DMA((7,)), pltpu.SemaphoreType.DMA((7,)), pltpu.SemaphoreType.DMA],
        name=name)(shard)


def _exchange(parts, name):
    R = parts.shape[1]

    def body(p_ref, out_ref, send_sems, recv_sems, local_sem):
        x, y, c = _mesh_pos()
        my = 4 * x + 2 * y + c
        mine = pltpu.make_async_copy(p_ref.at[my], out_ref.at[my], local_sem)
        mine.start()
        copies = []
        for k in range(1, N_DEV):
            px, py, pc = x ^ (k >> 2), y ^ ((k >> 1) & 1), c ^ (k & 1)
            peer = 4 * px + 2 * py + pc
            copies.append(pltpu.make_async_remote_copy(
                src_ref=p_ref.at[peer], dst_ref=out_ref.at[my],
                send_sem=send_sems.at[k - 1], recv_sem=recv_sems.at[k - 1],
                device_id=(px, py, pc), device_id_type=pl.DeviceIdType.MESH))
        for cp in copies:
            cp.start()
        for cp in copies:
            cp.wait()
        mine.wait()

    return pl.pallas_call(
        body, out_shape=_sds((N_DEV, R, LANES), parts.dtype),
        in_specs=[pl.BlockSpec(memory_space=pl.ANY)], out_specs=pl.BlockSpec(memory_space=pl.ANY),
        scratch_shapes=[pltpu.SemaphoreType.DMA((7,)), pltpu.SemaphoreType.DMA((7,)), pltpu.SemaphoreType.DMA],
        name=name)(parts)


def _adamw(parts, w, m, v, name, tr=1536):
    R = w.shape[0]
    tr = max(t for t in range(8, tr + 1, 8) if R % t == 0)
    c1 = 1.0 - ADAM_B1 ** ADAM_STEP
    c2 = 1.0 - ADAM_B2 ** ADAM_STEP

    def body(p_ref, w_ref, m_ref, v_ref, g_ref, d_ref, nm_ref, nv_ref):
        g = p_ref[0]
        for i in range(1, N_DEV):
            g = g + p_ref[i]
        nm = ADAM_B1 * m_ref[...] + (1.0 - ADAM_B1) * g
        nv = ADAM_B2 * v_ref[...] + (1.0 - ADAM_B2) * (g * g)
        g_ref[...] = g
        nm_ref[...] = nm
        nv_ref[...] = nv
        d_ref[...] = -ADAM_LR * ((nm / c1) / (jnp.sqrt(nv / c2) + ADAM_EPS) + ADAM_WD * w_ref[...])

    row = pl.BlockSpec((tr, LANES), lambda i: (i, 0))
    return pl.pallas_call(
        body, grid=(R // tr,),
        in_specs=[pl.BlockSpec((N_DEV, tr, LANES), lambda i: (0, i, 0)), row, row, row],
        out_specs=[row, row, row, row], out_shape=[_sds((R, LANES), F32)] * 4,
        name=name, compiler_params=_cp(("parallel",)))(parts, w, m, v)


WEIGHTS = ["ffn1_norm", "ffn1_w_gate", "ffn1_w_up", "ffn1_w_down", "mix_norm", "w_in", "pool_w", "pool_scale",
           "forget_bias", "conv_w", "conv_b", "conv_ln_g", "conv_ln_b", "w_out", "ffn2_norm", "ffn2_w_gate",
           "ffn2_w_up", "ffn2_w_down", "final_norm"]
SPLIT_AXIS = {"ffn1_w_gate": 2, "ffn1_w_up": 2, "ffn1_w_down": 1, "w_in": 2, "conv_w": 2, "w_out": 1,
              "ffn2_w_gate": 2, "ffn2_w_up": 2, "ffn2_w_down": 1}


def _padded(n):
    return -(-n // PACK_ALIGN) * PACK_ALIGN


def _flat_pad(a):
    f = a.reshape(-1)
    return jnp.pad(f, (0, _padded(f.shape[0]) - f.shape[0]))


def _split8(a, axis):
    shp = a.shape
    a = a.reshape(shp[:axis] + (N_DEV, shp[axis] // N_DEV) + shp[axis + 1:])
    return jnp.moveaxis(a, axis, 0)


def _merge8(a, axis):
    a = jnp.moveaxis(a, 0, axis)
    shp = a.shape
    return a.reshape(shp[:axis] + (shp[axis] * shp[axis + 1],) + shp[axis + 2:])


def _pack_local(arrs):
    return jnp.concatenate([_flat_pad(arrs[n]) for n in WEIGHTS]).reshape(-1, LANES)


def _pack_parts(grads):
    cols = []
    for n in WEIGHTS:
        g = grads[n]
        if n in SPLIT_AXIS:
            s = _split8(g, SPLIT_AXIS[n]).reshape(N_DEV, -1)
        else:
            s = jnp.broadcast_to(g.reshape(1, -1), (N_DEV, g.size))
        cols.append(jnp.pad(s, ((0, 0), (0, _padded(s.shape[1]) - s.shape[1]))))
    return jnp.concatenate(cols, axis=1).reshape(N_DEV, -1, LANES)


def _unpack_local(buf, like):
    flat = buf.reshape(-1)
    out, off = {}, 0
    for n in WEIGHTS:
        size = like[n].size
        out[n] = flat[off:off + size].reshape(like[n].shape)
        off += _padded(size)
    return out


def _gather_weights(w):
    names = [n for n in WEIGHTS if n in SPLIT_AXIS]
    segs = []
    for n in names:
        if n == "conv_w":
            bits = lax.bitcast_convert_type(_flat_pad(w[n]), CDT).reshape(-1)
        else:
            bits = _flat_pad(w[n].astype(CDT))
        segs.append(bits)
    gathered = _all_gather(jnp.concatenate(segs).reshape(-1, LANES), "gather_weights").reshape(N_DEV, -1)
    full, off = {}, 0
    for n, seg in zip(names, segs):
        blk = gathered[:, off:off + seg.shape[0]]
        off += seg.shape[0]
        if n == "conv_w":
            blk = lax.bitcast_convert_type(blk.reshape(N_DEV, -1, 2), F32)
        blk = blk[:, :w[n].size].reshape((N_DEV,) + w[n].shape)
        full[n] = _merge8(blk, SPLIT_AXIS[n])
    return full


def kernel(x, ffn1_norm, ffn1_w_gate, ffn1_w_up, ffn1_w_down, mix_norm, w_in, pool_w, pool_scale, forget_bias, conv_w, conv_b, conv_ln_g, conv_ln_b, w_out, ffn2_norm, ffn2_w_gate, ffn2_w_up, ffn2_w_down, final_norm, loss_target, m_ffn1_norm, m_ffn1_w_gate, m_ffn1_w_up, m_ffn1_w_down, m_mix_norm, m_w_in, m_pool_w, m_pool_scale, m_forget_bias, m_conv_w, m_conv_b, m_conv_ln_g, m_conv_ln_b, m_w_out, m_ffn2_norm, m_ffn2_w_gate, m_ffn2_w_up, m_ffn2_w_down, m_final_norm, v_ffn1_norm, v_ffn1_w_gate, v_ffn1_w_up, v_ffn1_w_down, v_mix_norm, v_w_in, v_pool_w, v_pool_scale, v_forget_bias, v_conv_w, v_conv_b, v_conv_ln_g, v_conv_ln_b, v_w_out, v_ffn2_norm, v_ffn2_w_gate, v_ffn2_w_up, v_ffn2_w_down, v_final_norm):
    w = dict(zip(WEIGHTS, (ffn1_norm, ffn1_w_gate, ffn1_w_up, ffn1_w_down, mix_norm, w_in, pool_w, pool_scale, forget_bias,
                           conv_w, conv_b, conv_ln_g, conv_ln_b, w_out, ffn2_norm, ffn2_w_gate, ffn2_w_up, ffn2_w_down,
                           final_norm)))
    m = dict(zip(WEIGHTS, (m_ffn1_norm, m_ffn1_w_gate, m_ffn1_w_up, m_ffn1_w_down, m_mix_norm, m_w_in, m_pool_w, m_pool_scale,
                           m_forget_bias, m_conv_w, m_conv_b, m_conv_ln_g, m_conv_ln_b, m_w_out, m_ffn2_norm, m_ffn2_w_gate,
                           m_ffn2_w_up, m_ffn2_w_down, m_final_norm)))
    v = dict(zip(WEIGHTS, (v_ffn1_norm, v_ffn1_w_gate, v_ffn1_w_up, v_ffn1_w_down, v_mix_norm, v_w_in, v_pool_w, v_pool_scale,
                           v_forget_bias, v_conv_w, v_conv_b, v_conv_ln_g, v_conv_ln_b, v_w_out, v_ffn2_norm, v_ffn2_w_gate,
                           v_ffn2_w_up, v_ffn2_w_down, v_final_norm)))
    full = dict(w)
    full.update(_gather_weights(w))
    loss_row, gx, grads = _local_step(x[0], loss_target[0], full)
    loss = lax.psum(loss_row[0, 0], ("x", "y", "c"))
    parts = _exchange(_pack_parts(grads), "exchange_grads")
    g, d, nm, nv = _adamw(parts, _pack_local(w), _pack_local(m), _pack_local(v), "adamw")
    outs = [_unpack_local(b, w) for b in (g, d, nm, nv)]
    return (loss, gx[None], *[o[n] for o in outs for n in WEIGHTS])
```

```python
import math

import numpy as np
import jax
import jax.numpy as jnp
from jax import lax
from jax.experimental import pallas as pl
from jax.experimental.pallas import tpu as pltpu

F32 = jnp.float32
CDT = jnp.bfloat16
NORM_EPS = 1e-6
N_DEV = 8
LANES = 128
PACK_ALIGN = 8 * LANES
VMEM_LIMIT = 48 * 1024 * 1024

POOL_WINDOWS = (2, 4, 8, 16)
POOL_HALO = 16
CONV_K = 31
CONV_HALO = 32
HEAD_DIM = 64
N_HEADS = 8
N_PAIRS = N_HEADS // 2
ATT_SCALE = 1.0 / math.sqrt(HEAD_DIM)
NEG = -1e30

ADAM_LR, ADAM_B1, ADAM_B2, ADAM_EPS, ADAM_WD, ADAM_STEP = 0.001, 0.9, 0.999, 1e-08, 0.01, 10

REST_W = 896
REST_Z_BLK = 6


def _cp(sem):
    return pltpu.CompilerParams(dimension_semantics=sem, vmem_limit_bytes=VMEM_LIMIT)


def _tile(n, pref):
    t = min(n, pref)
    assert n % t == 0, (n, pref)
    return t


def _sigmoid(x):
    return 1.0 / (1.0 + jnp.exp(-x))


def _sds(shape, dtype):
    return jax.ShapeDtypeStruct(shape, dtype)


def _rms_fwd(x, g, name):
    T, D = x.shape
    tm = _tile(T, 1024)

    def body(x_ref, g_ref, o_ref):
        xv = x_ref[...]
        r = lax.rsqrt(jnp.mean(xv * xv, axis=-1, keepdims=True) + NORM_EPS)
        o_ref[...] = (xv * r * g_ref[...]).astype(o_ref.dtype)

    return pl.pallas_call(
        body, grid=(T // tm,),
        in_specs=[pl.BlockSpec((tm, D), lambda i: (i, 0)), pl.BlockSpec((1, D), lambda i: (0, 0))],
        out_specs=pl.BlockSpec((tm, D), lambda i: (i, 0)),
        out_shape=_sds((T, D), CDT), name=name, compiler_params=_cp(("parallel",)))(x, g)


def _rms_bwd(x, g, dh, gres, name):
    T, D = x.shape
    tm = _tile(T, 512)

    def body(x_ref, g_ref, dh_ref, gres_ref, gin_ref, dg_ref):
        i = pl.program_id(0)
        xv = x_ref[...]
        d = dh_ref[...]
        r = lax.rsqrt(jnp.mean(xv * xv, axis=-1, keepdims=True) + NORM_EPS)
        xh = xv * r
        dxh = d * g_ref[...]
        c = jnp.mean(dxh * xh, axis=-1, keepdims=True)
        gin_ref[...] = gres_ref[...] + r * (dxh - xh * c)
        part = jnp.sum(d * xh, axis=0, keepdims=True)

        @pl.when(i == 0)
        def _():
            dg_ref[...] = part

        @pl.when(i > 0)
        def _():
            dg_ref[...] += part

    row = pl.BlockSpec((tm, D), lambda i: (i, 0))
    vec = pl.BlockSpec((1, D), lambda i: (0, 0))
    return pl.pallas_call(
        body, grid=(T // tm,), in_specs=[row, vec, row, row], out_specs=[row, vec],
        out_shape=[_sds((T, D), F32), _sds((1, D), F32)], name=name, compiler_params=_cp(("arbitrary",)))(x, g, dh, gres)


def _loss_bwd(x, g, target, name):
    T, D = x.shape
    tm = _tile(T, 512)

    def body(x_ref, g_ref, t_ref, loss_ref, dx_ref, dg_ref):
        i = pl.program_id(0)
        xv = x_ref[...]
        gv = g_ref[...]
        r = lax.rsqrt(jnp.mean(xv * xv, axis=-1, keepdims=True) + NORM_EPS)
        xh = xv * r
        err = xh * gv - t_ref[...]
        lpart = 0.5 * jnp.sum(jnp.mean(err * err, axis=-1, keepdims=True), axis=0, keepdims=True)
        dy = err * (1.0 / D)
        dxh = dy * gv
        c = jnp.mean(dxh * xh, axis=-1, keepdims=True)
        dx_ref[...] = r * (dxh - xh * c)
        part = jnp.sum(dy * xh, axis=0, keepdims=True)
        lrow = jnp.broadcast_to(lpart, (1, LANES))

        @pl.when(i == 0)
        def _():
            dg_ref[...] = part
            loss_ref[...] = lrow

        @pl.when(i > 0)
        def _():
            dg_ref[...] += part
            loss_ref[...] += lrow

    row = pl.BlockSpec((tm, D), lambda i: (i, 0))
    vec = pl.BlockSpec((1, D), lambda i: (0, 0))
    return pl.pallas_call(
        body, grid=(T // tm,), in_specs=[row, vec, row],
        out_specs=[pl.BlockSpec((1, LANES), lambda i: (0, 0)), row, vec],
        out_shape=[_sds((1, LANES), F32), _sds((T, D), F32), _sds((1, D), F32)],
        name=name, compiler_params=_cp(("arbitrary",)))(x, g, target)


def _mm(pairs, *, name, res=None, alpha=1.0, out_dtype=F32, tm=512, tn=None):
    T = pairs[0][0].shape[0]
    N = pairs[0][1].shape[0] if pairs[0][2] else pairs[0][1].shape[1]
    tm = _tile(T, tm)
    tn = N if tn is None else _tile(N, tn)
    flags = [p[2] for p in pairs]
    n_in = 2 * len(pairs)

    def body(*refs):
        o_ref = refs[-1]
        acc = None
        for p, bt in enumerate(flags):
            a = refs[2 * p][...].astype(CDT)
            b = refs[2 * p + 1][...]
            dims = (((1,), (1,)), ((), ())) if bt else (((1,), (0,)), ((), ()))
            d = lax.dot_general(a, b, dims, preferred_element_type=F32)
            acc = d if acc is None else acc + d
        if alpha != 1.0:
            acc = acc * alpha
        if res is not None:
            acc = refs[n_in][...] + acc
        o_ref[...] = acc.astype(o_ref.dtype)

    in_specs, args = [], []
    for a, b, bt in pairs:
        K = a.shape[1]
        in_specs.append(pl.BlockSpec((tm, K), lambda i, j: (i, 0)))
        in_specs.append(pl.BlockSpec((tn, K), lambda i, j: (j, 0)) if bt else pl.BlockSpec((K, tn), lambda i, j: (0, j)))
        args += [a, b]
    if res is not None:
        in_specs.append(pl.BlockSpec((tm, tn), lambda i, j: (i, j)))
        args.append(res)
    return pl.pallas_call(
        body, grid=(T // tm, N // tn), in_specs=in_specs,
        out_specs=pl.BlockSpec((tm, tn), lambda i, j: (i, j)),
        out_shape=_sds((T, N), out_dtype), name=name, compiler_params=_cp(("parallel", "arbitrary")))(*args)


def _mm_tn(a, b, *, name, alpha=1.0, tk=512):
    T, M = a.shape
    N = b.shape[1]
    tm = M if M <= 1024 else M // 2
    tn = N if N <= 1536 else N // 2
    assert M % tm == 0 and N % tn == 0 and tm % LANES == 0 and tn % LANES == 0
    tk = _tile(T, tk)
    nk = T // tk

    def body(a_ref, b_ref, o_ref):
        k = pl.program_id(2)
        d = lax.dot_general(a_ref[...].astype(CDT), b_ref[...].astype(CDT), (((0,), (0,)), ((), ())),
                            preferred_element_type=F32)

        @pl.when(k == 0)
        def _():
            o_ref[...] = d

        @pl.when(k > 0)
        def _():
            o_ref[...] += d

        if alpha != 1.0:
            @pl.when(k == nk - 1)
            def _():
                o_ref[...] *= alpha

    return pl.pallas_call(
        body, grid=(M // tm, N // tn, nk),
        in_specs=[pl.BlockSpec((tk, tm), lambda i, j, k: (k, i)), pl.BlockSpec((tk, tn), lambda i, j, k: (k, j))],
        out_specs=pl.BlockSpec((tm, tn), lambda i, j, k: (i, j)),
        out_shape=_sds((M, N), F32), name=name, compiler_params=_cp(("parallel", "parallel", "arbitrary")))(a, b)


def _mm_tn_split(a, b, *, name, tk=512):
    T, M = a.shape
    N = b.shape[1]
    n = N // N_DEV
    parts = N_DEV // 2
    tn = parts * n
    assert M % LANES == 0 and N == N_DEV * n
    tk = _tile(T, tk)
    nk = T // tk

    def body(a_ref, b_ref, o_ref, acc):
        k = pl.program_id(1)
        d = lax.dot_general(a_ref[...].astype(CDT), b_ref[...].astype(CDT), (((0,), (0,)), ((), ())),
                            preferred_element_type=F32)

        @pl.when(k == 0)
        def _():
            acc[...] = d

        @pl.when(k > 0)
        def _():
            acc[...] += d

        @pl.when(k == nk - 1)
        def _():
            full = acc[...]
            for s in range(parts):
                o_ref[s] = full[:, s * n:(s + 1) * n]

    return pl.pallas_call(
        body, grid=(N // tn, nk),
        in_specs=[pl.BlockSpec((tk, M), lambda j, k: (k, 0)), pl.BlockSpec((tk, tn), lambda j, k: (k, j))],
        out_specs=pl.BlockSpec((parts, M, n), lambda j, k: (j, 0, 0)),
        out_shape=_sds((N_DEV, M, n), F32), scratch_shapes=[pltpu.VMEM((M, tn), F32)],
        name=name, compiler_params=_cp(("parallel", "arbitrary")))(a, b)


def _ffn_up(h, wg, wu, name):
    T, D = h.shape
    Fh = wg.shape[1]
    tm = _tile(T, 1024)
    tn = _tile(Fh, 256)

    def body(h_ref, wg_ref, wu_ref, a_ref, b_ref, s_ref):
        hv = h_ref[...]
        a = jnp.dot(hv, wg_ref[...], preferred_element_type=F32)
        b = jnp.dot(hv, wu_ref[...], preferred_element_type=F32)
        a_ref[...] = a
        b_ref[...] = b
        s_ref[...] = (a * _sigmoid(a) * b).astype(s_ref.dtype)

    wspec = pl.BlockSpec((D, tn), lambda i, j: (0, j))
    ospec = pl.BlockSpec((tm, tn), lambda i, j: (i, j))
    return pl.pallas_call(
        body, grid=(T // tm, Fh // tn),
        in_specs=[pl.BlockSpec((tm, D), lambda i, j: (i, 0)), wspec, wspec],
        out_specs=[ospec, ospec, ospec],
        out_shape=[_sds((T, Fh), F32), _sds((T, Fh), F32), _sds((T, Fh), CDT)],
        name=name, compiler_params=_cp(("parallel", "arbitrary")))(h, wg, wu)


def _ffn_bwd_ds(gout, wd, a, b, name):
    T, D = gout.shape
    Fh = wd.shape[0]
    tm = _tile(T, 512)
    tn = _tile(Fh, 256)

    def body(g_ref, wd_ref, a_ref, b_ref, da_ref, db_ref):
        dy = (0.5 * g_ref[...]).astype(CDT)
        ds = lax.dot_general(dy, wd_ref[...], (((1,), (1,)), ((), ())), preferred_element_type=F32)
        av = a_ref[...]
        sg = _sigmoid(av)
        da_ref[...] = (ds * b_ref[...] * (sg * (1.0 + av * (1.0 - sg)))).astype(da_ref.dtype)
        db_ref[...] = (ds * (av * sg)).astype(db_ref.dtype)

    ospec = pl.BlockSpec((tm, tn), lambda i, j: (i, j))
    return pl.pallas_call(
        body, grid=(T // tm, Fh // tn),
        in_specs=[pl.BlockSpec((tm, D), lambda i, j: (i, 0)), pl.BlockSpec((tn, D), lambda i, j: (j, 0)), ospec, ospec],
        out_specs=[ospec, ospec],
        out_shape=[_sds((T, Fh), CDT), _sds((T, Fh), CDT)],
        name=name, compiler_params=_cp(("parallel", "arbitrary")))(gout, wd, a, b)


def _ffn_fwd(x, gamma, wg, wu, wd, tag):
    h = _rms_fwd(x, gamma, f"{tag}_norm")
    a, b, s = _ffn_up(h, wg, wu, f"{tag}_up")
    y = _mm([(s, wd, False)], res=x, alpha=0.5, tn=512, name=f"{tag}_down")
    return y, (x, h, a, b, s)


def _ffn_bwd(saved, gamma, wg, wu, wd, gout, tag):
    x, h, a, b, s = saved
    da, db = _ffn_bwd_ds(gout, wd, a, b, f"{tag}_bwd_ds")
    dwd = _mm_tn(s, gout, alpha=0.5, name=f"{tag}_dwd")
    dwg = _mm_tn_split(h, da, name=f"{tag}_dwg")
    dwu = _mm_tn_split(h, db, name=f"{tag}_dwu")
    dh = _mm([(da, wg, True), (db, wu, True)], tn=512, name=f"{tag}_dh")
    gin, dgamma = _rms_bwd(x, gamma, dh, gout, f"{tag}_norm_bwd")
    return gin, dgamma, dwg, dwu, dwd


def _fgate_fwd(rest, bias, name, bt=512):
    T = rest.shape[0]
    bt = _tile(T, bt)

    def body(z_ref, b_ref, fc_ref, ft_ref, carry):
        i = pl.program_id(0)

        @pl.when(i == 0)
        def _():
            carry[...] = jnp.zeros_like(carry)

        zb = z_ref[...] + b_ref[...]
        e = jnp.exp(-jnp.abs(zb))
        u = 1.0 + e
        log1p_e = jnp.where(u == 1.0, e, jnp.log(u) * (e / (u - 1.0)))
        x = jnp.minimum(zb, 0.0) - log1p_e
        row = lax.broadcasted_iota(jnp.int32, x.shape, 0)
        sh = 1
        while sh < bt:
            x = x + jnp.where(row >= sh, pltpu.roll(x, sh, 0), 0.0)
            sh *= 2
        f = x + carry[...]
        carry[...] = f[bt - 1:bt, :]
        fc_ref[...] = f
        ft_ref[...] = jnp.transpose(f)[0:N_HEADS, :]

    return pl.pallas_call(
        body, grid=(T // bt,),
        in_specs=[pl.BlockSpec((bt, LANES), lambda i: (i, REST_Z_BLK)), pl.BlockSpec((1, LANES), lambda i: (0, 0))],
        out_specs=[pl.BlockSpec((bt, LANES), lambda i: (i, 0)), pl.BlockSpec((N_HEADS, bt), lambda i: (0, i))],
        out_shape=[_sds((T, LANES), F32), _sds((N_HEADS, T), F32)],
        scratch_shapes=[pltpu.VMEM((1, LANES), F32)],
        name=name, compiler_params=_cp(("arbitrary",)))(rest, bias)


def _fgate_bwd(dfk, rest, bias, name, bt=512):
    T = rest.shape[0]
    bt = _tile(T, bt)
    nb = T // bt

    def body(df_ref, z_ref, b_ref, dz_ref, db_ref, carry):
        i = pl.program_id(0)

        @pl.when(i == 0)
        def _():
            carry[...] = jnp.zeros_like(carry)

        dfv = df_ref[...]
        lane = lax.broadcasted_iota(jnp.int32, (bt, LANES), 1)
        x = jnp.zeros((bt, LANES), F32)
        for h in range(N_HEADS):
            x = jnp.where(lane == h, dfv[:, HEAD_DIM * h:HEAD_DIM * h + 1], x)
        row = lax.broadcasted_iota(jnp.int32, x.shape, 0)
        sh = 1
        while sh < bt:
            x = x + jnp.where(row + sh < bt, pltpu.roll(x, bt - sh, 0), 0.0)
            sh *= 2
        dlf = x + carry[...]
        carry[...] = dlf[0:1, :]
        zb = z_ref[...] + b_ref[...]
        dz = jnp.where(lane < N_HEADS, dlf * _sigmoid(-zb), 0.0)
        dz_ref[...] = dz.astype(dz_ref.dtype)
        part = jnp.sum(dz, axis=0, keepdims=True)

        @pl.when(i == 0)
        def _():
            db_ref[...] = part

        @pl.when(i > 0)
        def _():
            db_ref[...] += part

    return pl.pallas_call(
        body, grid=(nb,),
        in_specs=[pl.BlockSpec((bt, 4 * LANES), lambda i: (nb - 1 - i, 0)),
                  pl.BlockSpec((bt, LANES), lambda i: (nb - 1 - i, REST_Z_BLK)),
                  pl.BlockSpec((1, LANES), lambda i: (0, 0))],
        out_specs=[pl.BlockSpec((bt, LANES), lambda i: (nb - 1 - i, 0)), pl.BlockSpec((1, LANES), lambda i: (0, 0))],
        out_shape=[_sds((T, LANES), CDT), _sds((1, LANES), F32)],
        scratch_shapes=[pltpu.VMEM((1, LANES), F32)],
        name=name, compiler_params=_cp(("arbitrary",)))(dfk, rest, bias)


def _by_group(vals, lane):
    out = vals[-1]
    for g in range(len(vals) - 2, -1, -1):
        out = jnp.where(lane // 64 == g, vals[g], out)
    return out


def _pool_counts(t0, n, lane):
    t = t0 + lax.broadcasted_iota(jnp.int32, (n, 256), 0)
    return _by_group([jnp.minimum(t + 1, w) for w in POOL_WINDOWS], lane).astype(F32)


def _pooled(u, halo, i, bt):
    lane = lax.broadcasted_iota(jnp.int32, (bt, 256), 1)
    ext = jnp.concatenate([jnp.where(i > 0, halo, 0.0), u], axis=0)
    sums, s, sh = [], ext, 1
    for _ in POOL_WINDOWS:
        s = s + pltpu.roll(s, sh, 0)
        sums.append(s[POOL_HALO:, :])
        sh *= 2
    return _by_group(sums, lane) / _pool_counts(i * bt, bt, lane) - u


def _pool_fwd(rest, wbd, scale, name, bt=512):
    T = rest.shape[0]
    bt = _tile(T, bt)
    hb = bt // POOL_HALO

    def body(u_ref, halo_ref, w_ref, sc_ref, o_ref):
        i = pl.program_id(0)
        pooled = _pooled(u_ref[...], halo_ref[...], i, bt)
        mixed = jnp.dot(pooled.astype(CDT), w_ref[...], preferred_element_type=F32)
        o_ref[...] = (mixed * sc_ref[...]).astype(o_ref.dtype)

    return pl.pallas_call(
        body, grid=(T // bt,),
        in_specs=[pl.BlockSpec((bt, 256), lambda i: (i, 0)),
                  pl.BlockSpec((POOL_HALO, 256), lambda i: (jnp.maximum(i * hb - 1, 0), 0)),
                  pl.BlockSpec((256, 256), lambda i: (0, 0)), pl.BlockSpec((1, 256), lambda i: (0, 0))],
        out_specs=pl.BlockSpec((bt, 256), lambda i: (i, 0)),
        out_shape=_sds((T, 256), CDT), name=name, compiler_params=_cp(("parallel",)))(rest, rest, wbd, scale)


def _pool_bwd(dcat, rest, wbd, scale, name, bt=512):
    T = rest.shape[0]
    bt = _tile(T, bt)
    hb = bt // POOL_HALO
    nb = T // bt
    n = bt + POOL_HALO

    def body(dy_ref, dyn_ref, u_ref, halo_ref, w_ref, sc_ref, du_ref, dw_ref, dsc_ref):
        i = pl.program_id(0)
        lane = lax.broadcasted_iota(jnp.int32, (bt, 256), 1)
        w = w_ref[...]
        sc = sc_ref[...]
        pooled = _pooled(u_ref[...], halo_ref[...], i, bt)
        pooled_c = pooled.astype(CDT)
        mixed = jnp.dot(pooled_c, w, preferred_element_type=F32)
        dy = dy_ref[...]
        dm = (dy * sc).astype(CDT)
        dsc = jnp.sum(dy * mixed, axis=0, keepdims=True)
        dw = lax.dot_general(pooled_c, dm, (((0,), (0,)), ((), ())), preferred_element_type=F32)
        nt = (((1,), (1,)), ((), ()))
        dpl = lax.dot_general(dm, w, nt, preferred_element_type=F32)
        dmn = (jnp.where(i < nb - 1, dyn_ref[...], 0.0) * sc).astype(CDT)
        dpln = lax.dot_general(dmn, w, nt, preferred_element_type=F32)
        lane_h = lax.broadcasted_iota(jnp.int32, (POOL_HALO, 256), 1)
        ext = jnp.concatenate([dpl / _pool_counts(i * bt, bt, lane),
                               dpln / _pool_counts((i + 1) * bt, POOL_HALO, lane_h)], axis=0)
        sums, s, sh = [], ext, 1
        for _ in POOL_WINDOWS:
            s = s + pltpu.roll(s, n - sh, 0)
            sums.append(s[0:bt, :])
            sh *= 2
        du_ref[...] = (_by_group(sums, lane) - dpl).astype(du_ref.dtype)

        @pl.when(i == 0)
        def _():
            dw_ref[...] = dw
            dsc_ref[...] = dsc

        @pl.when(i > 0)
        def _():
            dw_ref[...] += dw
            dsc_ref[...] += dsc

    full = pl.BlockSpec((256, 256), lambda i: (0, 0))
    vec = pl.BlockSpec((1, 256), lambda i: (0, 0))
    return pl.pallas_call(
        body, grid=(nb,),
        in_specs=[pl.BlockSpec((bt, 256), lambda i: (i, 0)),
                  pl.BlockSpec((POOL_HALO, 256), lambda i: (jnp.minimum((i + 1) * hb, nb * hb - 1), 0)),
                  pl.BlockSpec((bt, 256), lambda i: (i, 0)),
                  pl.BlockSpec((POOL_HALO, 256), lambda i: (jnp.maximum(i * hb - 1, 0), 0)),
                  full, vec],
        out_specs=[pl.BlockSpec((bt, 256), lambda i: (i, 0)), full, vec],
        out_shape=[_sds((T, 256), CDT), _sds((256, 256), F32), _sds((1, 256), F32)],
        name=name, compiler_params=_cp(("arbitrary",)))(dcat, dcat, rest, rest, wbd, scale)


def _glu_ext(a_ref, g_ref, ah_ref, gh_ref, i):
    u = a_ref[...] * _sigmoid(g_ref[...])
    uh = jnp.where(i > 0, ah_ref[...] * _sigmoid(gh_ref[...]), 0.0)
    return jnp.concatenate([uh, u], axis=0)


def _conv_fwd(rest, cw, cb, lg, lb, name, bt=512):
    T = rest.shape[0]
    bt = _tile(T, bt)
    hb = bt // CONV_HALO

    def body(a_ref, g_ref, ah_ref, gh_ref, cw_ref, cb_ref, lg_ref, lb_ref, o_ref, y_ref):
        i = pl.program_id(0)
        ext = _glu_ext(a_ref, g_ref, ah_ref, gh_ref, i)
        w = cw_ref[...]
        acc = w[CONV_K - 1:CONV_K, :] * ext
        for k in range(CONV_K - 1):
            acc = acc + w[k:k + 1, :] * pltpu.roll(ext, CONV_K - 1 - k, 0)
        y = acc[CONV_HALO:, :] + cb_ref[...]
        y_ref[...] = y
        yc = y - jnp.mean(y, axis=-1, keepdims=True)
        yn = yc * lax.rsqrt(jnp.mean(yc * yc, axis=-1, keepdims=True) + NORM_EPS)
        z = yn * lg_ref[...] + lb_ref[...]
        o_ref[...] = (z * _sigmoid(z)).astype(o_ref.dtype)

    def cur(c):
        return pl.BlockSpec((bt, 256), lambda i: (i, c))

    def prev(c):
        return pl.BlockSpec((CONV_HALO, 256), lambda i: (jnp.maximum(i * hb - 1, 0), c))

    vec = pl.BlockSpec((1, 256), lambda i: (0, 0))
    return pl.pallas_call(
        body, grid=(T // bt,),
        in_specs=[cur(1), cur(2), prev(1), prev(2), pl.BlockSpec((CONV_HALO, 256), lambda i: (0, 0)), vec, vec, vec],
        out_specs=[pl.BlockSpec((bt, 256), lambda i: (i, 0)), pl.BlockSpec((bt, 256), lambda i: (i, 0))],
        out_shape=[_sds((T, 256), CDT), _sds((T, 256), F32)],
        name=name, compiler_params=_cp(("parallel",)))(rest, rest, rest, rest, cw, cb, lg, lb)


def _conv_bwd(dcat, yconv, rest, cw, lg, lb, name, bt=512):
    T = rest.shape[0]
    bt = _tile(T, bt)
    hb = bt // CONV_HALO
    nb = T // bt
    n = bt + CONV_HALO

    def body(dy_ref, dyn_ref, y_ref, yn_ref, a_ref, g_ref, ah_ref, gh_ref, cw_ref, lg_ref, lb_ref,
             da_ref, dg_ref, dcw_ref, dcb_ref, dlg_ref, dlb_ref):
        i = pl.program_id(0)
        lgv = lg_ref[...]
        lbv = lb_ref[...]

        def ln_swish_bwd(dout, y):
            yc = y - jnp.mean(y, axis=-1, keepdims=True)
            rs = lax.rsqrt(jnp.mean(yc * yc, axis=-1, keepdims=True) + NORM_EPS)
            yn = yc * rs
            z = yn * lgv + lbv
            sg = _sigmoid(z)
            dz = dout * (sg * (1.0 + z * (1.0 - sg)))
            dyn = dz * lgv
            dyc = rs * (dyn - jnp.mean(dyn, axis=-1, keepdims=True) - yn * jnp.mean(dyn * yn, axis=-1, keepdims=True))
            return dyc, dz, yn

        dyc, dz, yn = ln_swish_bwd(dy_ref[...], y_ref[...])
        dyc_next, _, _ = ln_swish_bwd(dyn_ref[...], yn_ref[...])
        dyc_next = jnp.where(i < nb - 1, dyc_next, 0.0)
        ext_u = _glu_ext(a_ref, g_ref, ah_ref, gh_ref, i)
        ext_d = jnp.concatenate([dyc, dyc_next], axis=0)
        w = cw_ref[...]
        du = w[CONV_K - 1:CONV_K, :] * ext_d
        rows = []
        for k in range(CONV_K):
            s = CONV_K - 1 - k
            if s > 0:
                du = du + w[k:k + 1, :] * pltpu.roll(ext_d, n - s, 0)
                us = pltpu.roll(ext_u, s, 0)[CONV_HALO:, :]
            else:
                us = ext_u[CONV_HALO:, :]
            rows.append(jnp.sum(dyc * us, axis=0, keepdims=True))
        rows.append(jnp.zeros((1, 256), F32))
        dcw = jnp.concatenate(rows, axis=0)
        du = du[0:bt, :]
        av = a_ref[...]
        sg = _sigmoid(g_ref[...])
        da_ref[...] = (du * sg).astype(da_ref.dtype)
        dg_ref[...] = (du * av * (sg * (1.0 - sg))).astype(dg_ref.dtype)
        dcb = jnp.sum(dyc, axis=0, keepdims=True)
        dlg = jnp.sum(dz * yn, axis=0, keepdims=True)
        dlb = jnp.sum(dz, axis=0, keepdims=True)

        @pl.when(i == 0)
        def _():
            dcw_ref[...] = dcw
            dcb_ref[...] = dcb
            dlg_ref[...] = dlg
            dlb_ref[...] = dlb

        @pl.when(i > 0)
        def _():
            dcw_ref[...] += dcw
            dcb_ref[...] += dcb
            dlg_ref[...] += dlg
            dlb_ref[...] += dlb

    def cur(c):
        return pl.BlockSpec((bt, 256), lambda i: (i, c))

    def prev(c):
        return pl.BlockSpec((CONV_HALO, 256), lambda i: (jnp.maximum(i * hb - 1, 0), c))

    def nxt(c):
        return pl.BlockSpec((CONV_HALO, 256), lambda i: (jnp.minimum((i + 1) * hb, nb * hb - 1), c))

    vec = pl.BlockSpec((1, 256), lambda i: (0, 0))
    wfull = pl.BlockSpec((CONV_HALO, 256), lambda i: (0, 0))
    return pl.pallas_call(
        body, grid=(nb,),
        in_specs=[cur(3), nxt(3), cur(0), nxt(0), cur(1), cur(2), prev(1), prev(2), wfull, vec, vec],
        out_specs=[cur(0), cur(0), wfull, vec, vec, vec],
        out_shape=[_sds((T, 256), CDT), _sds((T, 256), CDT), _sds((CONV_HALO, 256), F32),
                   _sds((1, 256), F32), _sds((1, 256), F32), _sds((1, 256), F32)],
        name=name, compiler_params=_cp(("arbitrary",)))(dcat, dcat, yconv, yconv, rest, rest, rest, rest, cw, lg, lb)


def _half_mask(shape, a):
    lane = lax.broadcasted_iota(jnp.int32, shape, 1)
    return (lane // HEAD_DIM) == a


def _attn_fwd(qkv, fcol, frow, name, blk=512):
    T = qkv.shape[0]
    blk = _tile(T, blk)
    nq = T // blk
    nt = (((1,), (1,)), ((), ()))

    def body(q_ref, k_ref, v_ref, fc_ref, fr_ref, o_ref, olo_ref, lse_ref):
        p_id = pl.program_id(0)
        i = pl.program_id(1)
        q2 = q_ref[...]
        fc = fc_ref[...]
        lane = lax.broadcasted_iota(jnp.int32, (blk, LANES), 1)
        tri = lax.broadcasted_iota(jnp.int32, (blk, blk), 1) <= lax.broadcasted_iota(jnp.int32, (blk, blk), 0)
        outs, los, lses = [], [], []
        for a in range(2):
            qa = jnp.where(_half_mask(q2.shape, a), q2, jnp.zeros_like(q2)) * ATT_SCALE
            fq = jnp.sum(jnp.where(lane == 2 * p_id + a, fc, 0.0), axis=1, keepdims=True)

            def tile(j, carry, masked):
                m, l, acc, acc_lo = carry
                kj = k_ref[pl.ds(pl.multiple_of(j * blk, blk), blk), :]
                vj = v_ref[pl.ds(pl.multiple_of(j * blk, blk), blk), :]
                fk = fr_ref[a:a + 1, pl.ds(pl.multiple_of(j * blk, blk), blk)]
                s = lax.dot_general(qa, kj, nt, preferred_element_type=F32) + (fq - fk)
                if masked:
                    s = jnp.where(tri, s, NEG)
                m_new = jnp.maximum(m, jnp.max(s, axis=1, keepdims=True))
                alpha = jnp.exp(m - m_new)
                pr = jnp.exp(s - m_new)
                l = alpha * l + jnp.sum(pr, axis=1, keepdims=True)
                pr_hi = pr.astype(CDT)
                pr_lo = (pr - pr_hi.astype(F32)).astype(CDT)
                acc = alpha * acc + jnp.dot(pr_hi, vj, preferred_element_type=F32)
                acc_lo = alpha * acc_lo + jnp.dot(pr_lo, vj, preferred_element_type=F32)
                return m_new, l, acc, acc_lo

            zero = jnp.zeros((blk, LANES), F32)
            init = (jnp.full((blk, 1), NEG, F32), jnp.zeros((blk, 1), F32), zero, zero)
            carry = lax.fori_loop(0, i, lambda j, c: tile(j, c, False), init)
            m, l, acc, acc_lo = tile(i, carry, True)
            outs.append(acc / l)
            los.append(acc_lo / l)
            lses.append(m + jnp.log(l))
        lo = lane < HEAD_DIM
        o_ref[...] = jnp.where(lo, outs[0], outs[1])
        olo_ref[...] = jnp.where(lo, los[0], los[1])
        lse_t = jnp.transpose(jnp.where(lo, lses[0], lses[1]))
        lse_ref[...] = jnp.concatenate([lse_t[0:1, :], lse_t[HEAD_DIM:HEAD_DIM + 1, :]], axis=0)

    return pl.pallas_call(
        body, grid=(N_PAIRS, nq),
        in_specs=[pl.BlockSpec((blk, LANES), lambda p, i: (i, p)),
                  pl.BlockSpec((T, LANES), lambda p, i: (0, N_PAIRS + p)),
                  pl.BlockSpec((T, LANES), lambda p, i: (0, 2 * N_PAIRS + p)),
                  pl.BlockSpec((blk, LANES), lambda p, i: (i, 0)),
                  pl.BlockSpec((None, 2, T), lambda p, i: (p, 0, 0))],
        out_specs=[pl.BlockSpec((blk, LANES), lambda p, i: (i, p)), pl.BlockSpec((blk, LANES), lambda p, i: (i, p)),
                   pl.BlockSpec((None, 2, blk), lambda p, i: (p, 0, i))],
        out_shape=[_sds((T, N_PAIRS * LANES), F32), _sds((T, N_PAIRS * LANES), F32), _sds((N_PAIRS, 2, T), F32)],
        name=name, compiler_params=_cp(("parallel", "arbitrary")))(qkv, qkv, qkv, fcol, frow)


def _attn_delta(dcat, o, o_lo, name, blk=512):
    T = o.shape[0]
    blk = _tile(T, blk)

    def body(d_ref, o_ref, olo_ref, out_ref):
        prod = d_ref[:, 256:768].astype(CDT).astype(F32) * (o_ref[...] + olo_ref[...])
        pt = jnp.transpose(prod)
        out_ref[...] = jnp.sum(pt.reshape(N_HEADS, HEAD_DIM, blk), axis=1)

    return pl.pallas_call(
        body, grid=(T // blk,),
        in_specs=[pl.BlockSpec((blk, 1024), lambda i: (i, 0)), pl.BlockSpec((blk, 512), lambda i: (i, 0)),
                  pl.BlockSpec((blk, 512), lambda i: (i, 0))],
        out_specs=pl.BlockSpec((N_HEADS, blk), lambda i: (0, i)),
        out_shape=_sds((N_HEADS, T), F32), name=name, compiler_params=_cp(("parallel",)))(dcat, o, o_lo)


def _attn_bwd(qkv, dcat, fcol, frow, lse, delta, name, blk=512):
    T = qkv.shape[0]
    blk = _tile(T, blk)
    nq = T // blk
    nt = (((1,), (1,)), ((), ()))
    tn = (((0,), (0,)), ((), ()))

    def body(q_ref, do_ref, k_ref, v_ref, fc_ref, fr_ref, lse_ref, dl_ref, dq_ref, dk_ref, dv_ref, df_ref):
        p_id = pl.program_id(0)
        j = pl.program_id(1)

        @pl.when(j == 0)
        def _():
            dq_ref[...] = jnp.zeros_like(dq_ref)

        k2 = k_ref[...]
        v2 = v_ref[...]
        fc = fc_ref[...]
        lane = lax.broadcasted_iota(jnp.int32, (blk, LANES), 1)
        tri = lax.broadcasted_iota(jnp.int32, (blk, blk), 0) <= lax.broadcasted_iota(jnp.int32, (blk, blk), 1)
        dks, dvs, dfs = [], [], []
        for a in range(2):
            hm = _half_mask(k2.shape, a)
            ka = jnp.where(hm, k2, jnp.zeros_like(k2)) * ATT_SCALE
            va = jnp.where(hm, v2, jnp.zeros_like(v2))
            fk = jnp.sum(jnp.where(lane == 2 * p_id + a, fc, 0.0), axis=1, keepdims=True)

            def tile(i, carry, masked):
                dk_acc, dv_acc, df_acc = carry
                rows = pl.ds(pl.multiple_of(i * blk, blk), blk)
                qi = q_ref[rows, :]
                doi = do_ref[rows, :].astype(CDT)
                st = lax.dot_general(ka, qi, nt, preferred_element_type=F32)
                e = st + (fr_ref[a:a + 1, rows] - fk) - lse_ref[a:a + 1, rows]
                if masked:
                    e = jnp.where(tri, e, NEG)
                pt = jnp.exp(e)
                dpt = lax.dot_general(va, doi, nt, preferred_element_type=F32)
                ds32 = pt * (dpt - dl_ref[a:a + 1, rows])
                dst = ds32.astype(CDT)
                df_acc = df_acc + jnp.sum(ds32, axis=1, keepdims=True)
                dv_acc = dv_acc + jnp.dot(pt.astype(CDT), doi, preferred_element_type=F32)
                dk_acc = dk_acc + jnp.dot(dst, qi, preferred_element_type=F32)
                dq_ref[rows, :] += lax.dot_general(dst, ka, tn, preferred_element_type=F32)
                return dk_acc, dv_acc, df_acc

            init = (jnp.zeros((blk, LANES), F32), jnp.zeros((blk, LANES), F32), jnp.zeros((blk, 1), F32))
            carry = tile(j, init, True)
            dk_acc, dv_acc, df_acc = lax.fori_loop(j + 1, nq, lambda i, c: tile(i, c, False), carry)
            dks.append(dk_acc)
            dvs.append(dv_acc)
            dfs.append(df_acc)
        lo = lane < HEAD_DIM
        dk_ref[...] = jnp.where(lo, dks[0], dks[1]) * ATT_SCALE
        dv_ref[...] = jnp.where(lo, dvs[0], dvs[1])
        df_ref[...] = -jnp.where(lo, dfs[0], dfs[1])

    res = pl.BlockSpec((T, LANES), lambda p, j: (0, p))
    rows = pl.BlockSpec((None, 2, T), lambda p, j: (p, 0, 0))
    kv_out = pl.BlockSpec((blk, LANES), lambda p, j: (j, p))
    return pl.pallas_call(
        body, grid=(N_PAIRS, nq),
        in_specs=[res, pl.BlockSpec((T, LANES), lambda p, j: (0, 2 + p)),
                  pl.BlockSpec((blk, LANES), lambda p, j: (j, N_PAIRS + p)),
                  pl.BlockSpec((blk, LANES), lambda p, j: (j, 2 * N_PAIRS + p)),
                  pl.BlockSpec((blk, LANES), lambda p, j: (j, 0)), rows, rows, rows],
        out_specs=[res, kv_out, kv_out, kv_out],
        out_shape=[_sds((T, N_PAIRS * LANES), F32)] * 4,
        name=name, compiler_params=_cp(("parallel", "arbitrary")))(qkv, dcat, qkv, qkv, fcol, frow, lse, delta)


def _mixer_fwd(x, wts, tag):
    T = x.shape[0]
    h = _rms_fwd(x, wts["mix_norm"], f"{tag}_norm")
    qkv = _mm([(h, wts["win_qkv"], False)], out_dtype=CDT, tm=1024, tn=768, name=f"{tag}_in_qkv")
    rest = _mm([(h, wts["win_rest"], False)], tm=1024, name=f"{tag}_in_rest")
    fcol, frow8 = _fgate_fwd(rest, wts["fbias"], f"{tag}_fgate")
    frow = frow8.reshape(N_PAIRS, 2, T)
    ya = _pool_fwd(rest, wts["pool_wbd"], wts["pool_scale"], f"{tag}_pool")
    o, o_lo, lse = _attn_fwd(qkv, fcol, frow, f"{tag}_attn")
    yc, yconv = _conv_fwd(rest, wts["conv_w"], wts["conv_b"], wts["conv_ln_g"], wts["conv_ln_b"], f"{tag}_conv")
    cat = jnp.concatenate([ya, o.astype(CDT), yc], axis=1)
    y = _mm([(cat, wts["w_out"], False)], res=x, tn=512, name=f"{tag}_out")
    return y, (x, h, qkv, rest, fcol, frow, o, o_lo, lse, yconv, cat)


def _mixer_bwd(saved, wts, gout, tag):
    x, h, qkv, rest, fcol, frow, o, o_lo, lse, yconv, cat = saved
    T = x.shape[0]
    dcat = _mm([(gout, wts["w_out"], True)], tn=512, name=f"{tag}_dcat")
    dwout = _mm_tn(cat, gout, name=f"{tag}_dwout")
    du, dpw, dpsc = _pool_bwd(dcat, rest, wts["pool_wbd"], wts["pool_scale"], f"{tag}_pool_bwd")
    delta = _attn_delta(dcat, o, o_lo, f"{tag}_attn_delta").reshape(N_PAIRS, 2, T)
    dq, dk, dv, dfk = _attn_bwd(qkv, dcat, fcol, frow, lse, delta, f"{tag}_attn_bwd")
    dz, dfb = _fgate_bwd(dfk, rest, wts["fbias"], f"{tag}_fgate_bwd")
    da, dg, dcw, dcb, dlg, dlb = _conv_bwd(dcat, yconv, rest, wts["conv_w"], wts["conv_ln_g"], wts["conv_ln_b"],
                                           f"{tag}_conv_bwd")
    dp_qkv = jnp.concatenate([dq, dk, dv], axis=1).astype(CDT)
    dp_rest = jnp.concatenate([du, da, dg, dz], axis=1)
    dwin_qkv = _mm_tn(h, dp_qkv, name=f"{tag}_dwin_qkv")
    dwin_rest = _mm_tn(h, dp_rest, name=f"{tag}_dwin_rest")
    dh = _mm([(dp_qkv, wts["win_qkv"], True), (dp_rest, wts["win_rest"], True)], tn=512, name=f"{tag}_dh")
    gin, dgamma = _rms_bwd(x, wts["mix_norm"], dh, gout, f"{tag}_norm_bwd")
    dwin = _split_win(dwin_qkv, dwin_rest, f"{tag}_dwin_split")
    dpool_w = jnp.stack([dpw[64 * g:64 * g + 64, 64 * g:64 * g + 64] for g in range(4)])
    grads = dict(mix_norm=dgamma[0], w_in=dwin, pool_w=dpool_w, pool_scale=dpsc[0], forget_bias=dfb[0, 0:N_HEADS],
                 conv_w=dcw[0:CONV_K], conv_b=dcb[0], conv_ln_g=dlg[0], conv_ln_b=dlb[0], w_out=dwout)
    return gin, grads


def _layer_weights(big, rep, l):
    pw = rep["pool_w"][l].astype(CDT)
    wbd = jnp.zeros((256, 256), CDT)
    for g in range(4):
        wbd = lax.dynamic_update_slice(wbd, pw[g], (64 * g, 64 * g))
    wa, wd = big["wa"], big["wd"]
    return dict(
        ffn1_norm=rep["ffn1_norm"][l][None], ffn1_w_gate=wa[l], ffn1_w_up=wa[2 + l], ffn1_w_down=wd[l],
        ffn2_norm=rep["ffn2_norm"][l][None], ffn2_w_gate=wa[4 + l], ffn2_w_up=wa[6 + l], ffn2_w_down=wd[2 + l],
        mix_norm=rep["mix_norm"][l][None], win_qkv=big["win_qkv"][l], win_rest=big["win_rest"][l],
        fbias=jnp.pad(rep["forget_bias"][l], (0, LANES - N_HEADS))[None],
        pool_wbd=wbd, pool_scale=rep["pool_scale"][l][None],
        conv_w=jnp.pad(big["conv_w"][l], ((0, CONV_HALO - CONV_K), (0, 0))), conv_b=rep["conv_b"][l][None],
        conv_ln_g=rep["conv_ln_g"][l][None], conv_ln_b=rep["conv_ln_b"][l][None],
        w_out=big["w_out"][l])


def _local_step(x, target, big, rep):
    depth = rep["ffn1_norm"].shape[0]
    lw = [_layer_weights(big, rep, l) for l in range(depth)]
    saved = []
    for l in range(depth):
        w = lw[l]
        x, s1 = _ffn_fwd(x, w["ffn1_norm"], w["ffn1_w_gate"], w["ffn1_w_up"], w["ffn1_w_down"], f"l{l}_ffn1")
        x, s2 = _mixer_fwd(x, w, f"l{l}_mix")
        x, s3 = _ffn_fwd(x, w["ffn2_norm"], w["ffn2_w_gate"], w["ffn2_w_up"], w["ffn2_w_down"], f"l{l}_ffn2")
        saved.append((s1, s2, s3))
    loss, g, dfinal = _loss_bwd(x, rep["final_norm"][None], target, "loss_head")
    per_layer = []
    for l in reversed(range(depth)):
        w = lw[l]
        s1, s2, s3 = saved[l]
        g, dn2, dwg2, dwu2, dwd2 = _ffn_bwd(s3, w["ffn2_norm"], w["ffn2_w_gate"], w["ffn2_w_up"], w["ffn2_w_down"], g,
                                            f"l{l}_ffn2")
        g, gm = _mixer_bwd(s2, w, g, f"l{l}_mix")
        g, dn1, dwg1, dwu1, dwd1 = _ffn_bwd(s1, w["ffn1_norm"], w["ffn1_w_gate"], w["ffn1_w_up"], w["ffn1_w_down"], g,
                                            f"l{l}_ffn1")
        gm.update(ffn1_norm=dn1[0], ffn1_w_gate=dwg1, ffn1_w_up=dwu1, ffn1_w_down=dwd1,
                  ffn2_norm=dn2[0], ffn2_w_gate=dwg2, ffn2_w_up=dwu2, ffn2_w_down=dwd2)
        per_layer.append(gm)
    per_layer.reverse()
    grads = {k: [pl_[k] for pl_ in per_layer] for k in per_layer[0]}
    grads["final_norm"] = dfinal[0]
    return loss, g, grads


def _mesh_pos():
    return lax.axis_index("x"), lax.axis_index("y"), lax.axis_index("c")


def _dev_block(ref, dev, by_rows):
    if by_rows:
        r = ref.shape[1] // N_DEV
        return ref.at[:, pl.ds(dev * r, r), :]
    return ref.at[dev]


def _all_gather(shards, by_rows, name):
    n_arr = len(shards)
    out_shape = [_sds((s.shape[0], N_DEV * s.shape[1], s.shape[2]) if br else (N_DEV,) + s.shape, s.dtype)
                 for s, br in zip(shards, by_rows)]

    def body(*refs):
        xs, outs = refs[:n_arr], refs[n_arr:2 * n_arr]
        send_sems, recv_sems, local_sems = refs[2 * n_arr:]
        x, y, c = _mesh_pos()
        me, sibling = (x, y, c), (x, y, 1 - c)
        chips = [(1 - x, y), (x, 1 - y), (1 - x, 1 - y)]

        def rows(a, px, py, pc):
            return _dev_block(outs[a], 4 * px + 2 * py + pc, by_rows[a])

        def copy(k, a, block, to, src=None):
            return pltpu.make_async_remote_copy(
                src_ref=rows(a, *block) if src is None else src, dst_ref=rows(a, *block),
                send_sem=send_sems.at[k, a], recv_sem=recv_sems.at[k, a],
                device_id=to, device_id_type=pl.DeviceIdType.MESH)

        arrs = range(n_arr)
        mine = [pltpu.make_async_copy(xs[a], rows(a, *me), local_sems.at[a]) for a in arrs]
        for cp in mine:
            cp.start()
        first = [copy(0, a, me, sibling, src=xs[a]) for a in arrs]
        first += [copy(1 + j, a, me, (*chip, c), src=xs[a]) for j, chip in enumerate(chips) for a in arrs]
        for cp in first:
            cp.start()
        passed = []
        for j, chip in enumerate(chips):
            for a in arrs:
                copy(1 + j, a, (*chip, c), me).wait_recv()
                passed.append(copy(4 + j, a, (*chip, c), sibling))
                passed[-1].start()
        for a in arrs:
            copy(0, a, sibling, me).wait_recv()
        for j, chip in enumerate(chips):
            for a in arrs:
                copy(4 + j, a, (*chip, 1 - c), me).wait_recv()
        for cp in first + passed:
            cp.wait_send()
        for cp in mine:
            cp.wait()

    hbm = pl.BlockSpec(memory_space=pl.ANY)
    return pl.pallas_call(
        body, out_shape=out_shape, in_specs=[hbm] * n_arr, out_specs=[hbm] * n_arr,
        scratch_shapes=[pltpu.SemaphoreType.DMA((7, n_arr)), pltpu.SemaphoreType.DMA((7, n_arr)),
                        pltpu.SemaphoreType.DMA((n_arr,))],
        name=name)(*shards)


def _exchange(parts, by_rows, name):
    n_arr = len(parts)
    out_shape = [_sds((N_DEV, p.shape[0], p.shape[1] // N_DEV, p.shape[2]) if br else p.shape, p.dtype)
                 for p, br in zip(parts, by_rows)]

    def body(*refs):
        ps, outs = refs[:n_arr], refs[n_arr:2 * n_arr]
        send_sems, recv_sems, local_sems = refs[2 * n_arr:]
        x, y, c = _mesh_pos()
        my = 4 * x + 2 * y + c
        arrs = range(n_arr)
        mine = [pltpu.make_async_copy(_dev_block(ps[a], my, by_rows[a]), outs[a].at[my], local_sems.at[a]) for a in arrs]
        for cp in mine:
            cp.start()
        copies = []
        for k in range(1, N_DEV):
            px, py, pc = x ^ (k >> 2), y ^ ((k >> 1) & 1), c ^ (k & 1)
            for a in arrs:
                copies.append(pltpu.make_async_remote_copy(
                    src_ref=_dev_block(ps[a], 4 * px + 2 * py + pc, by_rows[a]), dst_ref=outs[a].at[my],
                    send_sem=send_sems.at[k - 1, a], recv_sem=recv_sems.at[k - 1, a],
                    device_id=(px, py, pc), device_id_type=pl.DeviceIdType.MESH))
        for cp in copies:
            cp.start()
        for cp in copies:
            cp.wait()
        for cp in mine:
            cp.wait()

    hbm = pl.BlockSpec(memory_space=pl.ANY)
    return pl.pallas_call(
        body, out_shape=out_shape, in_specs=[hbm] * n_arr, out_specs=[hbm] * n_arr,
        scratch_shapes=[pltpu.SemaphoreType.DMA((7, n_arr)), pltpu.SemaphoreType.DMA((7, n_arr)),
                        pltpu.SemaphoreType.DMA((n_arr,))],
        name=name)(*parts)


def _adam_update(g, w, m, v):
    c1 = 1.0 - ADAM_B1 ** ADAM_STEP
    c2 = 1.0 - ADAM_B2 ** ADAM_STEP
    nm = ADAM_B1 * m + (1.0 - ADAM_B1) * g
    nv = ADAM_B2 * v + (1.0 - ADAM_B2) * (g * g)
    return -ADAM_LR * ((nm / c1) / (jnp.sqrt(nv / c2) + ADAM_EPS) + ADAM_WD * w), nm, nv


def _adamw_body(p_ref, w_ref, m_ref, v_ref, g_ref, d_ref, nm_ref, nv_ref):
    g = p_ref[0]
    for i in range(1, N_DEV):
        g = g + p_ref[i]
    g_ref[...] = g
    d_ref[...], nm_ref[...], nv_ref[...] = _adam_update(g, w_ref[...], m_ref[...], v_ref[...])


def _adamw(parts, w, m, v, name, tr=1536):
    R = w.shape[0]
    tr = max(t for t in range(8, tr + 1, 8) if R % t == 0)

    def body(*refs):
        _adamw_body(*refs)

    row = pl.BlockSpec((tr, LANES), lambda i: (i, 0))
    return pl.pallas_call(
        body, grid=(R // tr,),
        in_specs=[pl.BlockSpec((N_DEV, tr, LANES), lambda i: (0, i, 0)), row, row, row],
        out_specs=[row, row, row, row], out_shape=[_sds((R, LANES), F32)] * 4,
        name=name, compiler_params=_cp(("parallel",)))(parts, w, m, v)


def _adamw_split(recv, tsel, w, m, v, name, tr):
    depth, r, c = w.shape
    tr = _tile(r, tr)

    def body(*refs):
        _adamw_body(*refs)

    wspec = pl.BlockSpec((None, tr, c), lambda l, i: (l, i, 0))
    return pl.pallas_call(
        body, grid=(depth, r // tr),
        in_specs=[pl.BlockSpec((N_DEV, None, tr, c), lambda l, i: (0, tsel(l), i, 0)), wspec, wspec, wspec],
        out_specs=[wspec] * 4, out_shape=[_sds(w.shape, F32)] * 4,
        name=name, compiler_params=_cp(("parallel", "parallel")))(recv, w, m, v)


def _merge_cols(g, name, tr=256):
    _, nt, K, n = g.shape
    tr = _tile(K, tr)

    def body(g_ref, o_ref):
        o_ref[...] = jnp.concatenate([g_ref[j] for j in range(N_DEV)], axis=1)

    return pl.pallas_call(
        body, grid=(nt, K // tr),
        in_specs=[pl.BlockSpec((N_DEV, None, tr, n), lambda t, i: (0, t, i, 0))],
        out_specs=pl.BlockSpec((None, tr, N_DEV * n), lambda t, i: (t, i, 0)),
        out_shape=_sds((nt, K, N_DEV * n), g.dtype), name=name, compiler_params=_cp(("parallel", "parallel")))(g)


def _merge_win(g, name, tr=256):
    _, nt, K, n = g.shape
    tr = _tile(K, tr)

    def body(g_ref, q_ref, r_ref):
        full = jnp.concatenate([g_ref[j] for j in range(N_DEV)], axis=1)
        q_ref[...] = full[:, 256:1792]
        zpad = jnp.zeros((tr, REST_W - 776), full.dtype)
        r_ref[...] = jnp.concatenate([full[:, 0:256], full[:, 1800:2312], full[:, 1792:1800], zpad], axis=1)

    return pl.pallas_call(
        body, grid=(nt, K // tr),
        in_specs=[pl.BlockSpec((N_DEV, None, tr, n), lambda t, i: (0, t, i, 0))],
        out_specs=[pl.BlockSpec((None, tr, 1536), lambda t, i: (t, i, 0)), pl.BlockSpec((None, tr, REST_W), lambda t, i: (t, i, 0))],
        out_shape=[_sds((nt, K, 1536), g.dtype), _sds((nt, K, REST_W), g.dtype)],
        name=name, compiler_params=_cp(("parallel", "parallel")))(g)


def _split_win(dq, dr, name, tr=256):
    K = dq.shape[0]
    tr = _tile(K, tr)
    n = (dq.shape[1] + 776) // N_DEV

    def body(q_ref, r_ref, o_ref):
        r = r_ref[...]
        full = jnp.concatenate([r[:, 0:256], q_ref[...], r[:, 768:776], r[:, 256:768]], axis=1)
        for j in range(N_DEV):
            o_ref[j] = full[:, n * j:n * (j + 1)]

    return pl.pallas_call(
        body, grid=(K // tr,),
        in_specs=[pl.BlockSpec((tr, dq.shape[1]), lambda i: (i, 0)), pl.BlockSpec((tr, REST_W), lambda i: (i, 0))],
        out_specs=pl.BlockSpec((N_DEV, tr, n), lambda i: (0, i, 0)),
        out_shape=_sds((N_DEV, K, n), F32), name=name, compiler_params=_cp(("parallel",)))(dq, dr)


WEIGHTS = ["ffn1_norm", "ffn1_w_gate", "ffn1_w_up", "ffn1_w_down", "mix_norm", "w_in", "pool_w", "pool_scale",
           "forget_bias", "conv_w", "conv_b", "conv_ln_g", "conv_ln_b", "w_out", "ffn2_norm", "ffn2_w_gate",
           "ffn2_w_up", "ffn2_w_down", "final_norm"]
FFN_COL = ["ffn1_w_gate", "ffn1_w_up", "ffn2_w_gate", "ffn2_w_up"]
FFN_ROW = ["ffn1_w_down", "ffn2_w_down"]
BIG = FFN_COL + FFN_ROW + ["w_in", "w_out"]
SMALL = [n for n in WEIGHTS if n not in BIG]


def _padded(n):
    return -(-n // PACK_ALIGN) * PACK_ALIGN


def _flat_pad(a):
    f = a.reshape(-1)
    return jnp.pad(f, (0, _padded(f.shape[0]) - f.shape[0]))


def _split8(a, axis):
    shp = a.shape
    a = a.reshape(shp[:axis] + (N_DEV, shp[axis] // N_DEV) + shp[axis + 1:])
    return jnp.moveaxis(a, axis, 0)


def _merge8(a, axis):
    a = jnp.moveaxis(a, 0, axis)
    shp = a.shape
    return a.reshape(shp[:axis] + (shp[axis] * shp[axis + 1],) + shp[axis + 2:])


def _pack_small(arrs):
    return jnp.concatenate([_flat_pad(arrs[n]) for n in SMALL]).reshape(-1, LANES)


def _pack_small_parts(grads):
    cols = []
    for n in SMALL:
        g = grads[n]
        if n == "conv_w":
            s = _split8(g, 2).reshape(N_DEV, -1)
        else:
            s = jnp.broadcast_to(g.reshape(1, -1), (N_DEV, g.size))
        cols.append(jnp.pad(s, ((0, 0), (0, _padded(s.shape[1]) - s.shape[1]))))
    return jnp.concatenate(cols, axis=1).reshape(N_DEV, -1, LANES)


def _unpack_small(buf, like):
    flat = buf.reshape(-1)
    out, off = {}, 0
    for n in SMALL:
        size = like[n].size
        out[n] = flat[off:off + size].reshape(like[n].shape)
        off += _padded(size)
    return out


def _gather_weights(w):
    d, f = w["ffn1_w_gate"].shape[1], w["ffn1_w_down"].shape[1]
    cols = jnp.stack([w[n] for n in FFN_COL]).astype(CDT).reshape(-1, d, f)
    rows = jnp.stack([w[n] for n in FFN_ROW]).astype(CDT).reshape(-1, f, d)
    ga, gd, gi, go, gc = _all_gather([cols, rows, w["w_in"].astype(CDT), w["w_out"].astype(CDT), w["conv_w"]],
                                     [False, True, False, True, False], "gather_weights")
    win_qkv, win_rest = _merge_win(gi, "merge_w_in")
    return dict(wa=_merge_cols(ga, "merge_ffn_cols"), wd=gd, win_qkv=win_qkv, win_rest=win_rest, w_out=go,
                conv_w=_merge8(gc, 2))


def kernel(x, ffn1_norm, ffn1_w_gate, ffn1_w_up, ffn1_w_down, mix_norm, w_in, pool_w, pool_scale, forget_bias, conv_w, conv_b, conv_ln_g, conv_ln_b, w_out, ffn2_norm, ffn2_w_gate, ffn2_w_up, ffn2_w_down, final_norm, loss_target, m_ffn1_norm, m_ffn1_w_gate, m_ffn1_w_up, m_ffn1_w_down, m_mix_norm, m_w_in, m_pool_w, m_pool_scale, m_forget_bias, m_conv_w, m_conv_b, m_conv_ln_g, m_conv_ln_b, m_w_out, m_ffn2_norm, m_ffn2_w_gate, m_ffn2_w_up, m_ffn2_w_down, m_final_norm, v_ffn1_norm, v_ffn1_w_gate, v_ffn1_w_up, v_ffn1_w_down, v_mix_norm, v_w_in, v_pool_w, v_pool_scale, v_forget_bias, v_conv_w, v_conv_b, v_conv_ln_g, v_conv_ln_b, v_w_out, v_ffn2_norm, v_ffn2_w_gate, v_ffn2_w_up, v_ffn2_w_down, v_final_norm):
    w = dict(zip(WEIGHTS, (ffn1_norm, ffn1_w_gate, ffn1_w_up, ffn1_w_down, mix_norm, w_in, pool_w, pool_scale, forget_bias,
                           conv_w, conv_b, conv_ln_g, conv_ln_b, w_out, ffn2_norm, ffn2_w_gate, ffn2_w_up, ffn2_w_down,
                           final_norm)))
    m = dict(zip(WEIGHTS, (m_ffn1_norm, m_ffn1_w_gate, m_ffn1_w_up, m_ffn1_w_down, m_mix_norm, m_w_in, m_pool_w, m_pool_scale,
                           m_forget_bias, m_conv_w, m_conv_b, m_conv_ln_g, m_conv_ln_b, m_w_out, m_ffn2_norm, m_ffn2_w_gate,
                           m_ffn2_w_up, m_ffn2_w_down, m_final_norm)))
    v = dict(zip(WEIGHTS, (v_ffn1_norm, v_ffn1_w_gate, v_ffn1_w_up, v_ffn1_w_down, v_mix_norm, v_w_in, v_pool_w, v_pool_scale,
                           v_forget_bias, v_conv_w, v_conv_b, v_conv_ln_g, v_conv_ln_b, v_w_out, v_ffn2_norm, v_ffn2_w_gate,
                           v_ffn2_w_up, v_ffn2_w_down, v_final_norm)))
    loss_row, gx, grads = _local_step(x[0], loss_target[0], _gather_weights(w), w)
    loss = lax.psum(loss_row[0, 0], ("x", "y", "c"))
    depth = range(len(grads["w_in"]))
    small = {n: (grads[n] if n == "final_norm" else jnp.stack(grads[n])) for n in SMALL}
    ra, rd, ri, ro, rs = _exchange(
        [jnp.stack([grads[n][l] for n in FFN_COL for l in depth], axis=1),
         jnp.stack([grads[n][l] for n in FFN_ROW for l in depth]),
         jnp.stack(grads["w_in"], axis=1), jnp.stack(grads["w_out"]), _pack_small_parts(small)],
        [False, True, False, True, False], "exchange_grads")
    res = {}
    for k, n in enumerate(FFN_COL):
        res[n] = _adamw_split(ra, lambda l, k=k: 2 * k + l, w[n], m[n], v[n], f"adamw_{n}", 256)
    for k, n in enumerate(FFN_ROW):
        res[n] = _adamw_split(rd, lambda l, k=k: 2 * k + l, w[n], m[n], v[n], f"adamw_{n}", 176)
    res["w_in"] = _adamw_split(ri, lambda l: l, w["w_in"], m["w_in"], v["w_in"], "adamw_w_in", 256)
    res["w_out"] = _adamw_split(ro, lambda l: l, w["w_out"], m["w_out"], v["w_out"], "adamw_w_out", 128)
    packed = _adamw(rs, _pack_small(w), _pack_small(m), _pack_small(v), "adamw_small")
    unpacked = [_unpack_small(b, w) for b in packed]
    for n in SMALL:
        res[n] = [u[n] for u in unpacked]
    return (loss, gx[None], *[res[n][i] for i in range(4) for n in WEIGHTS])
```

```python
import math

import numpy as np
import jax
import jax.numpy as jnp
from jax import lax
from jax.experimental import pallas as pl
from jax.experimental.pallas import tpu as pltpu

F32 = jnp.float32
CDT = jnp.bfloat16
NORM_EPS = 1e-6
N_DEV = 8
LANES = 128
PACK_ALIGN = 8 * LANES
VMEM_LIMIT = 48 * 1024 * 1024

POOL_WINDOWS = (2, 4, 8, 16)
POOL_HALO = 16
CONV_K = 31
CONV_HALO = 32
HEAD_DIM = 64
N_HEADS = 8
N_PAIRS = N_HEADS // 2
ATT_SCALE = 1.0 / math.sqrt(HEAD_DIM)
NEG = -1e30

ADAM_LR, ADAM_B1, ADAM_B2, ADAM_EPS, ADAM_WD, ADAM_STEP = 0.001, 0.9, 0.999, 1e-08, 0.01, 10

REST_W = 896
REST_Z_BLK = 6


def _cp(sem):
    return pltpu.CompilerParams(dimension_semantics=sem, vmem_limit_bytes=VMEM_LIMIT)


def _tile(n, pref):
    t = min(n, pref)
    assert n % t == 0, (n, pref)
    return t


def _sigmoid(x):
    return 1.0 / (1.0 + jnp.exp(-x))


def _sds(shape, dtype):
    return jax.ShapeDtypeStruct(shape, dtype)


_ANY = pl.BlockSpec(memory_space=pl.ANY)


def _dep(dep):
    return ([], []) if dep is None else ([_ANY], [dep])


def _rms_fwd(x, g, name, dep=None):
    T, D = x.shape
    tm = _tile(T, 1024)

    def body(x_ref, g_ref, *rest):
        o_ref = rest[-1]
        xv = x_ref[...]
        r = lax.rsqrt(jnp.mean(xv * xv, axis=-1, keepdims=True) + NORM_EPS)
        o_ref[...] = (xv * r * g_ref[...]).astype(o_ref.dtype)

    dspec, darg = _dep(dep)
    return pl.pallas_call(
        body, grid=(T // tm,),
        in_specs=[pl.BlockSpec((tm, D), lambda i: (i, 0)), pl.BlockSpec((1, D), lambda i: (0, 0))] + dspec,
        out_specs=pl.BlockSpec((tm, D), lambda i: (i, 0)),
        out_shape=_sds((T, D), CDT), name=name, compiler_params=_cp(("parallel",)))(x, g, *darg)


def _rms_bwd(x, g, dh, gres, name):
    T, D = x.shape
    tm = _tile(T, 512)

    def body(x_ref, g_ref, dh_ref, gres_ref, gin_ref, dg_ref):
        i = pl.program_id(0)
        xv = x_ref[...]
        d = dh_ref[...]
        r = lax.rsqrt(jnp.mean(xv * xv, axis=-1, keepdims=True) + NORM_EPS)
        xh = xv * r
        dxh = d * g_ref[...]
        c = jnp.mean(dxh * xh, axis=-1, keepdims=True)
        gin_ref[...] = gres_ref[...] + r * (dxh - xh * c)
        part = jnp.sum(d * xh, axis=0, keepdims=True)

        @pl.when(i == 0)
        def _():
            dg_ref[...] = part

        @pl.when(i > 0)
        def _():
            dg_ref[...] += part

    row = pl.BlockSpec((tm, D), lambda i: (i, 0))
    vec = pl.BlockSpec((1, D), lambda i: (0, 0))
    return pl.pallas_call(
        body, grid=(T // tm,), in_specs=[row, vec, row, row], out_specs=[row, vec],
        out_shape=[_sds((T, D), F32), _sds((1, D), F32)], name=name, compiler_params=_cp(("arbitrary",)))(x, g, dh, gres)


def _loss_bwd(x, g, target, name):
    T, D = x.shape
    tm = _tile(T, 512)

    def body(x_ref, g_ref, t_ref, loss_ref, dx_ref, dg_ref):
        i = pl.program_id(0)
        xv = x_ref[...]
        gv = g_ref[...]
        r = lax.rsqrt(jnp.mean(xv * xv, axis=-1, keepdims=True) + NORM_EPS)
        xh = xv * r
        err = xh * gv - t_ref[...]
        lpart = 0.5 * jnp.sum(jnp.mean(err * err, axis=-1, keepdims=True), axis=0, keepdims=True)
        dy = err * (1.0 / D)
        dxh = dy * gv
        c = jnp.mean(dxh * xh, axis=-1, keepdims=True)
        dx_ref[...] = r * (dxh - xh * c)
        part = jnp.sum(dy * xh, axis=0, keepdims=True)
        lrow = jnp.broadcast_to(lpart, (1, LANES))

        @pl.when(i == 0)
        def _():
            dg_ref[...] = part
            loss_ref[...] = lrow

        @pl.when(i > 0)
        def _():
            dg_ref[...] += part
            loss_ref[...] += lrow

    row = pl.BlockSpec((tm, D), lambda i: (i, 0))
    vec = pl.BlockSpec((1, D), lambda i: (0, 0))
    return pl.pallas_call(
        body, grid=(T // tm,), in_specs=[row, vec, row],
        out_specs=[pl.BlockSpec((1, LANES), lambda i: (0, 0)), row, vec],
        out_shape=[_sds((1, LANES), F32), _sds((T, D), F32), _sds((1, D), F32)],
        name=name, compiler_params=_cp(("arbitrary",)))(x, g, target)


def _mm(pairs, *, name, res=None, alpha=1.0, out_dtype=F32, tm=512, tn=None, dep=None):
    T = pairs[0][0].shape[0]
    N = pairs[0][1].shape[0] if pairs[0][2] else pairs[0][1].shape[1]
    tm = _tile(T, tm)
    tn = N if tn is None else _tile(N, tn)
    flags = [p[2] for p in pairs]
    n_in = 2 * len(pairs)

    def body(*refs):
        o_ref = refs[-1]
        acc = None
        for p, bt in enumerate(flags):
            a = refs[2 * p][...].astype(CDT)
            b = refs[2 * p + 1][...]
            dims = (((1,), (1,)), ((), ())) if bt else (((1,), (0,)), ((), ()))
            d = lax.dot_general(a, b, dims, preferred_element_type=F32)
            acc = d if acc is None else acc + d
        if alpha != 1.0:
            acc = acc * alpha
        if res is not None:
            acc = refs[n_in][...] + acc
        o_ref[...] = acc.astype(o_ref.dtype)

    in_specs, args = [], []
    for a, b, bt in pairs:
        K = a.shape[1]
        in_specs.append(pl.BlockSpec((tm, K), lambda i, j: (i, 0)))
        in_specs.append(pl.BlockSpec((tn, K), lambda i, j: (j, 0)) if bt else pl.BlockSpec((K, tn), lambda i, j: (0, j)))
        args += [a, b]
    if res is not None:
        in_specs.append(pl.BlockSpec((tm, tn), lambda i, j: (i, j)))
        args.append(res)
    dspec, darg = _dep(dep)
    in_specs += dspec
    args += darg
    return pl.pallas_call(
        body, grid=(T // tm, N // tn), in_specs=in_specs,
        out_specs=pl.BlockSpec((tm, tn), lambda i, j: (i, j)),
        out_shape=_sds((T, N), out_dtype), name=name, compiler_params=_cp(("parallel", "arbitrary")))(*args)


def _mm_tn(a, b, *, name, alpha=1.0, tk=512):
    T, M = a.shape
    N = b.shape[1]
    tm = M if M <= 1024 else M // 2
    tn = N if N <= 1536 else N // 2
    assert M % tm == 0 and N % tn == 0 and tm % LANES == 0 and tn % LANES == 0
    tk = _tile(T, tk)
    nk = T // tk

    def body(a_ref, b_ref, o_ref):
        k = pl.program_id(2)
        d = lax.dot_general(a_ref[...].astype(CDT), b_ref[...].astype(CDT), (((0,), (0,)), ((), ())),
                            preferred_element_type=F32)

        @pl.when(k == 0)
        def _():
            o_ref[...] = d

        @pl.when(k > 0)
        def _():
            o_ref[...] += d

        if alpha != 1.0:
            @pl.when(k == nk - 1)
            def _():
                o_ref[...] *= alpha

    return pl.pallas_call(
        body, grid=(M // tm, N // tn, nk),
        in_specs=[pl.BlockSpec((tk, tm), lambda i, j, k: (k, i)), pl.BlockSpec((tk, tn), lambda i, j, k: (k, j))],
        out_specs=pl.BlockSpec((tm, tn), lambda i, j, k: (i, j)),
        out_shape=_sds((M, N), F32), name=name, compiler_params=_cp(("parallel", "parallel", "arbitrary")))(a, b)


def _mm_tn_split(a, b, *, name, tk=512):
    T, M = a.shape
    N = b.shape[1]
    n = N // N_DEV
    parts = N_DEV // 2
    tn = parts * n
    assert M % LANES == 0 and N == N_DEV * n
    tk = _tile(T, tk)
    nk = T // tk

    def body(a_ref, b_ref, o_ref, acc):
        k = pl.program_id(1)
        d = lax.dot_general(a_ref[...].astype(CDT), b_ref[...].astype(CDT), (((0,), (0,)), ((), ())),
                            preferred_element_type=F32)

        @pl.when(k == 0)
        def _():
            acc[...] = d

        @pl.when(k > 0)
        def _():
            acc[...] += d

        @pl.when(k == nk - 1)
        def _():
            full = acc[...]
            for s in range(parts):
                o_ref[s] = full[:, s * n:(s + 1) * n]

    return pl.pallas_call(
        body, grid=(N // tn, nk),
        in_specs=[pl.BlockSpec((tk, M), lambda j, k: (k, 0)), pl.BlockSpec((tk, tn), lambda j, k: (k, j))],
        out_specs=pl.BlockSpec((parts, M, n), lambda j, k: (j, 0, 0)),
        out_shape=_sds((N_DEV, M, n), F32), scratch_shapes=[pltpu.VMEM((M, tn), F32)],
        name=name, compiler_params=_cp(("parallel", "arbitrary")))(a, b)


def _ffn_up(h, wg, wu, name):
    T, D = h.shape
    Fh = wg.shape[1]
    tm = _tile(T, 1024)
    tn = _tile(Fh, 256)

    def body(h_ref, wg_ref, wu_ref, a_ref, b_ref, s_ref):
        hv = h_ref[...]
        a = jnp.dot(hv, wg_ref[...], preferred_element_type=F32)
        b = jnp.dot(hv, wu_ref[...], preferred_element_type=F32)
        a_ref[...] = a
        b_ref[...] = b
        s_ref[...] = (a * _sigmoid(a) * b).astype(s_ref.dtype)

    wspec = pl.BlockSpec((D, tn), lambda i, j: (0, j))
    ospec = pl.BlockSpec((tm, tn), lambda i, j: (i, j))
    return pl.pallas_call(
        body, grid=(T // tm, Fh // tn),
        in_specs=[pl.BlockSpec((tm, D), lambda i, j: (i, 0)), wspec, wspec],
        out_specs=[ospec, ospec, ospec],
        out_shape=[_sds((T, Fh), F32), _sds((T, Fh), F32), _sds((T, Fh), CDT)],
        name=name, compiler_params=_cp(("parallel", "arbitrary")))(h, wg, wu)


def _ffn_bwd_ds(gout, wd, a, b, name, dep=None):
    T, D = gout.shape
    Fh = wd.shape[0]
    tm = _tile(T, 512)
    tn = _tile(Fh, 256)

    def body(g_ref, wd_ref, a_ref, b_ref, *rest):
        da_ref, db_ref = rest[-2:]
        dy = (0.5 * g_ref[...]).astype(CDT)
        ds = lax.dot_general(dy, wd_ref[...], (((1,), (1,)), ((), ())), preferred_element_type=F32)
        av = a_ref[...]
        sg = _sigmoid(av)
        da_ref[...] = (ds * b_ref[...] * (sg * (1.0 + av * (1.0 - sg)))).astype(da_ref.dtype)
        db_ref[...] = (ds * (av * sg)).astype(db_ref.dtype)

    ospec = pl.BlockSpec((tm, tn), lambda i, j: (i, j))
    dspec, darg = _dep(dep)
    return pl.pallas_call(
        body, grid=(T // tm, Fh // tn),
        in_specs=[pl.BlockSpec((tm, D), lambda i, j: (i, 0)), pl.BlockSpec((tn, D), lambda i, j: (j, 0)), ospec, ospec] + dspec,
        out_specs=[ospec, ospec],
        out_shape=[_sds((T, Fh), CDT), _sds((T, Fh), CDT)],
        name=name, compiler_params=_cp(("parallel", "arbitrary")))(gout, wd, a, b, *darg)


def _ffn_fwd(x, gamma, wg, wu, wd, tag, dep=None):
    h = _rms_fwd(x, gamma, f"{tag}_norm", dep)
    a, b, s = _ffn_up(h, wg, wu, f"{tag}_up")
    y = _mm([(s, wd, False)], res=x, alpha=0.5, tn=512, name=f"{tag}_down")
    return y, (x, h, a, b, s)


def _ffn_bwd(saved, gamma, wg, wu, wd, gout, tag, dep=None):
    x, h, a, b, s = saved
    da, db = _ffn_bwd_ds(gout, wd, a, b, f"{tag}_bwd_ds", dep)
    dwd = _mm_tn(s, gout, alpha=0.5, name=f"{tag}_dwd")
    dwg = _mm_tn_split(h, da, name=f"{tag}_dwg")
    dwu = _mm_tn_split(h, db, name=f"{tag}_dwu")
    dh = _mm([(da, wg, True), (db, wu, True)], tn=512, name=f"{tag}_dh")
    gin, dgamma = _rms_bwd(x, gamma, dh, gout, f"{tag}_norm_bwd")
    return gin, dgamma, dwg, dwu, dwd


def _fgate_fwd(rest, bias, name, bt=512):
    T = rest.shape[0]
    bt = _tile(T, bt)

    def body(z_ref, b_ref, fc_ref, ft_ref, carry):
        i = pl.program_id(0)

        @pl.when(i == 0)
        def _():
            carry[...] = jnp.zeros_like(carry)

        zb = z_ref[...] + b_ref[...]
        e = jnp.exp(-jnp.abs(zb))
        u = 1.0 + e
        log1p_e = jnp.where(u == 1.0, e, jnp.log(u) * (e / (u - 1.0)))
        x = jnp.minimum(zb, 0.0) - log1p_e
        row = lax.broadcasted_iota(jnp.int32, x.shape, 0)
        sh = 1
        while sh < bt:
            x = x + jnp.where(row >= sh, pltpu.roll(x, sh, 0), 0.0)
            sh *= 2
        f = x + carry[...]
        carry[...] = f[bt - 1:bt, :]
        fc_ref[...] = f
        ft_ref[...] = jnp.transpose(f)[0:N_HEADS, :]

    return pl.pallas_call(
        body, grid=(T // bt,),
        in_specs=[pl.BlockSpec((bt, LANES), lambda i: (i, REST_Z_BLK)), pl.BlockSpec((1, LANES), lambda i: (0, 0))],
        out_specs=[pl.BlockSpec((bt, LANES), lambda i: (i, 0)), pl.BlockSpec((N_HEADS, bt), lambda i: (0, i))],
        out_shape=[_sds((T, LANES), F32), _sds((N_HEADS, T), F32)],
        scratch_shapes=[pltpu.VMEM((1, LANES), F32)],
        name=name, compiler_params=_cp(("arbitrary",)))(rest, bias)


def _fgate_bwd(dfk, rest, bias, name, bt=512):
    T = rest.shape[0]
    bt = _tile(T, bt)
    nb = T // bt

    def body(df_ref, z_ref, b_ref, dz_ref, db_ref, carry):
        i = pl.program_id(0)

        @pl.when(i == 0)
        def _():
            carry[...] = jnp.zeros_like(carry)

        dfv = df_ref[...]
        lane = lax.broadcasted_iota(jnp.int32, (bt, LANES), 1)
        x = jnp.zeros((bt, LANES), F32)
        for h in range(N_HEADS):
            x = jnp.where(lane == h, dfv[:, HEAD_DIM * h:HEAD_DIM * h + 1], x)
        row = lax.broadcasted_iota(jnp.int32, x.shape, 0)
        sh = 1
        while sh < bt:
            x = x + jnp.where(row + sh < bt, pltpu.roll(x, bt - sh, 0), 0.0)
            sh *= 2
        dlf = x + carry[...]
        carry[...] = dlf[0:1, :]
        zb = z_ref[...] + b_ref[...]
        dz = jnp.where(lane < N_HEADS, dlf * _sigmoid(-zb), 0.0)
        dz_ref[...] = dz.astype(dz_ref.dtype)
        part = jnp.sum(dz, axis=0, keepdims=True)

        @pl.when(i == 0)
        def _():
            db_ref[...] = part

        @pl.when(i > 0)
        def _():
            db_ref[...] += part

    return pl.pallas_call(
        body, grid=(nb,),
        in_specs=[pl.BlockSpec((bt, 4 * LANES), lambda i: (nb - 1 - i, 0)),
                  pl.BlockSpec((bt, LANES), lambda i: (nb - 1 - i, REST_Z_BLK)),
                  pl.BlockSpec((1, LANES), lambda i: (0, 0))],
        out_specs=[pl.BlockSpec((bt, LANES), lambda i: (nb - 1 - i, 0)), pl.BlockSpec((1, LANES), lambda i: (0, 0))],
        out_shape=[_sds((T, LANES), CDT), _sds((1, LANES), F32)],
        scratch_shapes=[pltpu.VMEM((1, LANES), F32)],
        name=name, compiler_params=_cp(("arbitrary",)))(dfk, rest, bias)


def _by_group(vals, lane):
    out = vals[-1]
    for g in range(len(vals) - 2, -1, -1):
        out = jnp.where(lane // 64 == g, vals[g], out)
    return out


def _pool_counts(t0, n, lane):
    t = t0 + lax.broadcasted_iota(jnp.int32, (n, 256), 0)
    return _by_group([jnp.minimum(t + 1, w) for w in POOL_WINDOWS], lane).astype(F32)


def _pooled(u, halo, i, bt):
    lane = lax.broadcasted_iota(jnp.int32, (bt, 256), 1)
    ext = jnp.concatenate([jnp.where(i > 0, halo, 0.0), u], axis=0)
    sums, s, sh = [], ext, 1
    for _ in POOL_WINDOWS:
        s = s + pltpu.roll(s, sh, 0)
        sums.append(s[POOL_HALO:, :])
        sh *= 2
    return _by_group(sums, lane) / _pool_counts(i * bt, bt, lane) - u


def _pool_fwd(rest, wbd, scale, name, bt=512):
    T = rest.shape[0]
    bt = _tile(T, bt)
    hb = bt // POOL_HALO

    def body(u_ref, halo_ref, w_ref, sc_ref, o_ref):
        i = pl.program_id(0)
        pooled = _pooled(u_ref[...], halo_ref[...], i, bt)
        mixed = jnp.dot(pooled.astype(CDT), w_ref[...], preferred_element_type=F32)
        o_ref[...] = (mixed * sc_ref[...]).astype(o_ref.dtype)

    return pl.pallas_call(
        body, grid=(T // bt,),
        in_specs=[pl.BlockSpec((bt, 256), lambda i: (i, 0)),
                  pl.BlockSpec((POOL_HALO, 256), lambda i: (jnp.maximum(i * hb - 1, 0), 0)),
                  pl.BlockSpec((256, 256), lambda i: (0, 0)), pl.BlockSpec((1, 256), lambda i: (0, 0))],
        out_specs=pl.BlockSpec((bt, 256), lambda i: (i, 0)),
        out_shape=_sds((T, 256), CDT), name=name, compiler_params=_cp(("parallel",)))(rest, rest, wbd, scale)


def _pool_bwd(dcat, rest, wbd, scale, name, bt=512):
    T = rest.shape[0]
    bt = _tile(T, bt)
    hb = bt // POOL_HALO
    nb = T // bt
    n = bt + POOL_HALO

    def body(dy_ref, dyn_ref, u_ref, halo_ref, w_ref, sc_ref, du_ref, dw_ref, dsc_ref):
        i = pl.program_id(0)
        lane = lax.broadcasted_iota(jnp.int32, (bt, 256), 1)
        w = w_ref[...]
        sc = sc_ref[...]
        pooled = _pooled(u_ref[...], halo_ref[...], i, bt)
        pooled_c = pooled.astype(CDT)
        mixed = jnp.dot(pooled_c, w, preferred_element_type=F32)
        dy = dy_ref[...]
        dm = (dy * sc).astype(CDT)
        dsc = jnp.sum(dy * mixed, axis=0, keepdims=True)
        dw = lax.dot_general(pooled_c, dm, (((0,), (0,)), ((), ())), preferred_element_type=F32)
        nt = (((1,), (1,)), ((), ()))
        dpl = lax.dot_general(dm, w, nt, preferred_element_type=F32)
        dmn = (jnp.where(i < nb - 1, dyn_ref[...], 0.0) * sc).astype(CDT)
        dpln = lax.dot_general(dmn, w, nt, preferred_element_type=F32)
        lane_h = lax.broadcasted_iota(jnp.int32, (POOL_HALO, 256), 1)
        ext = jnp.concatenate([dpl / _pool_counts(i * bt, bt, lane),
                               dpln / _pool_counts((i + 1) * bt, POOL_HALO, lane_h)], axis=0)
        sums, s, sh = [], ext, 1
        for _ in POOL_WINDOWS:
            s = s + pltpu.roll(s, n - sh, 0)
            sums.append(s[0:bt, :])
            sh *= 2
        du_ref[...] = (_by_group(sums, lane) - dpl).astype(du_ref.dtype)

        @pl.when(i == 0)
        def _():
            dw_ref[...] = dw
            dsc_ref[...] = dsc

        @pl.when(i > 0)
        def _():
            dw_ref[...] += dw
            dsc_ref[...] += dsc

    full = pl.BlockSpec((256, 256), lambda i: (0, 0))
    vec = pl.BlockSpec((1, 256), lambda i: (0, 0))
    return pl.pallas_call(
        body, grid=(nb,),
        in_specs=[pl.BlockSpec((bt, 256), lambda i: (i, 0)),
                  pl.BlockSpec((POOL_HALO, 256), lambda i: (jnp.minimum((i + 1) * hb, nb * hb - 1), 0)),
                  pl.BlockSpec((bt, 256), lambda i: (i, 0)),
                  pl.BlockSpec((POOL_HALO, 256), lambda i: (jnp.maximum(i * hb - 1, 0), 0)),
                  full, vec],
        out_specs=[pl.BlockSpec((bt, 256), lambda i: (i, 0)), full, vec],
        out_shape=[_sds((T, 256), CDT), _sds((256, 256), F32), _sds((1, 256), F32)],
        name=name, compiler_params=_cp(("arbitrary",)))(dcat, dcat, rest, rest, wbd, scale)


def _glu_ext(a_ref, g_ref, ah_ref, gh_ref, i):
    u = a_ref[...] * _sigmoid(g_ref[...])
    uh = jnp.where(i > 0, ah_ref[...] * _sigmoid(gh_ref[...]), 0.0)
    return jnp.concatenate([uh, u], axis=0)


def _conv_fwd(rest, cw, cb, lg, lb, name, bt=512):
    T = rest.shape[0]
    bt = _tile(T, bt)
    hb = bt // CONV_HALO

    def body(a_ref, g_ref, ah_ref, gh_ref, cw_ref, cb_ref, lg_ref, lb_ref, o_ref, y_ref):
        i = pl.program_id(0)
        ext = _glu_ext(a_ref, g_ref, ah_ref, gh_ref, i)
        w = cw_ref[...]
        acc = w[CONV_K - 1:CONV_K, :] * ext
        for k in range(CONV_K - 1):
            acc = acc + w[k:k + 1, :] * pltpu.roll(ext, CONV_K - 1 - k, 0)
        y = acc[CONV_HALO:, :] + cb_ref[...]
        y_ref[...] = y
        yc = y - jnp.mean(y, axis=-1, keepdims=True)
        yn = yc * lax.rsqrt(jnp.mean(yc * yc, axis=-1, keepdims=True) + NORM_EPS)
        z = yn * lg_ref[...] + lb_ref[...]
        o_ref[...] = (z * _sigmoid(z)).astype(o_ref.dtype)

    def cur(c):
        return pl.BlockSpec((bt, 256), lambda i: (i, c))

    def prev(c):
        return pl.BlockSpec((CONV_HALO, 256), lambda i: (jnp.maximum(i * hb - 1, 0), c))

    vec = pl.BlockSpec((1, 256), lambda i: (0, 0))
    return pl.pallas_call(
        body, grid=(T // bt,),
        in_specs=[cur(1), cur(2), prev(1), prev(2), pl.BlockSpec((CONV_HALO, 256), lambda i: (0, 0)), vec, vec, vec],
        out_specs=[pl.BlockSpec((bt, 256), lambda i: (i, 0)), pl.BlockSpec((bt, 256), lambda i: (i, 0))],
        out_shape=[_sds((T, 256), CDT), _sds((T, 256), F32)],
        name=name, compiler_params=_cp(("parallel",)))(rest, rest, rest, rest, cw, cb, lg, lb)


def _conv_bwd(dcat, yconv, rest, cw, lg, lb, name, bt=512):
    T = rest.shape[0]
    bt = _tile(T, bt)
    hb = bt // CONV_HALO
    nb = T // bt
    n = bt + CONV_HALO

    def body(dy_ref, dyn_ref, y_ref, yn_ref, a_ref, g_ref, ah_ref, gh_ref, cw_ref, lg_ref, lb_ref,
             da_ref, dg_ref, dcw_ref, dcb_ref, dlg_ref, dlb_ref):
        i = pl.program_id(0)
        lgv = lg_ref[...]
        lbv = lb_ref[...]

        def ln_swish_bwd(dout, y):
            yc = y - jnp.mean(y, axis=-1, keepdims=True)
            rs = lax.rsqrt(jnp.mean(yc * yc, axis=-1, keepdims=True) + NORM_EPS)
            yn = yc * rs
            z = yn * lgv + lbv
            sg = _sigmoid(z)
            dz = dout * (sg * (1.0 + z * (1.0 - sg)))
            dyn = dz * lgv
            dyc = rs * (dyn - jnp.mean(dyn, axis=-1, keepdims=True) - yn * jnp.mean(dyn * yn, axis=-1, keepdims=True))
            return dyc, dz, yn

        dyc, dz, yn = ln_swish_bwd(dy_ref[...], y_ref[...])
        dyc_next, _, _ = ln_swish_bwd(dyn_ref[...], yn_ref[...])
        dyc_next = jnp.where(i < nb - 1, dyc_next, 0.0)
        ext_u = _glu_ext(a_ref, g_ref, ah_ref, gh_ref, i)
        ext_d = jnp.concatenate([dyc, dyc_next], axis=0)
        w = cw_ref[...]
        du = w[CONV_K - 1:CONV_K, :] * ext_d
        rows = []
        for k in range(CONV_K):
            s = CONV_K - 1 - k
            if s > 0:
                du = du + w[k:k + 1, :] * pltpu.roll(ext_d, n - s, 0)
                us = pltpu.roll(ext_u, s, 0)[CONV_HALO:, :]
            else:
                us = ext_u[CONV_HALO:, :]
            rows.append(jnp.sum(dyc * us, axis=0, keepdims=True))
        rows.append(jnp.zeros((1, 256), F32))
        dcw = jnp.concatenate(rows, axis=0)
        du = du[0:bt, :]
        av = a_ref[...]
        sg = _sigmoid(g_ref[...])
        da_ref[...] = (du * sg).astype(da_ref.dtype)
        dg_ref[...] = (du * av * (sg * (1.0 - sg))).astype(dg_ref.dtype)
        dcb = jnp.sum(dyc, axis=0, keepdims=True)
        dlg = jnp.sum(dz * yn, axis=0, keepdims=True)
        dlb = jnp.sum(dz, axis=0, keepdims=True)

        @pl.when(i == 0)
        def _():
            dcw_ref[...] = dcw
            dcb_ref[...] = dcb
            dlg_ref[...] = dlg
            dlb_ref[...] = dlb

        @pl.when(i > 0)
        def _():
            dcw_ref[...] += dcw
            dcb_ref[...] += dcb
            dlg_ref[...] += dlg
            dlb_ref[...] += dlb

    def cur(c):
        return pl.BlockSpec((bt, 256), lambda i: (i, c))

    def prev(c):
        return pl.BlockSpec((CONV_HALO, 256), lambda i: (jnp.maximum(i * hb - 1, 0), c))

    def nxt(c):
        return pl.BlockSpec((CONV_HALO, 256), lambda i: (jnp.minimum((i + 1) * hb, nb * hb - 1), c))

    vec = pl.BlockSpec((1, 256), lambda i: (0, 0))
    wfull = pl.BlockSpec((CONV_HALO, 256), lambda i: (0, 0))
    return pl.pallas_call(
        body, grid=(nb,),
        in_specs=[cur(3), nxt(3), cur(0), nxt(0), cur(1), cur(2), prev(1), prev(2), wfull, vec, vec],
        out_specs=[cur(0), cur(0), wfull, vec, vec, vec],
        out_shape=[_sds((T, 256), CDT), _sds((T, 256), CDT), _sds((CONV_HALO, 256), F32),
                   _sds((1, 256), F32), _sds((1, 256), F32), _sds((1, 256), F32)],
        name=name, compiler_params=_cp(("arbitrary",)))(dcat, dcat, yconv, yconv, rest, rest, rest, rest, cw, lg, lb)


def _half_mask(shape, a):
    lane = lax.broadcasted_iota(jnp.int32, shape, 1)
    return (lane // HEAD_DIM) == a


def _attn_fwd(qkv, fcol, frow, name, blk=512):
    T = qkv.shape[0]
    blk = _tile(T, blk)
    nq = T // blk
    nt = (((1,), (1,)), ((), ()))

    def body(q_ref, k_ref, v_ref, fc_ref, fr_ref, o_ref, olo_ref, lse_ref):
        p_id = pl.program_id(0)
        i = pl.program_id(1)
        q2 = q_ref[...]
        fc = fc_ref[...]
        lane = lax.broadcasted_iota(jnp.int32, (blk, LANES), 1)
        tri = lax.broadcasted_iota(jnp.int32, (blk, blk), 1) <= lax.broadcasted_iota(jnp.int32, (blk, blk), 0)
        outs, los, lses = [], [], []
        for a in range(2):
            qa = jnp.where(_half_mask(q2.shape, a), q2, jnp.zeros_like(q2)) * ATT_SCALE
            fq = jnp.sum(jnp.where(lane == 2 * p_id + a, fc, 0.0), axis=1, keepdims=True)

            def tile(j, carry, masked):
                m, l, acc, acc_lo = carry
                kj = k_ref[pl.ds(pl.multiple_of(j * blk, blk), blk), :]
                vj = v_ref[pl.ds(pl.multiple_of(j * blk, blk), blk), :]
                fk = fr_ref[a:a + 1, pl.ds(pl.multiple_of(j * blk, blk), blk)]
                s = lax.dot_general(qa, kj, nt, preferred_element_type=F32) + (fq - fk)
                if masked:
                    s = jnp.where(tri, s, NEG)
                m_new = jnp.maximum(m, jnp.max(s, axis=1, keepdims=True))
                alpha = jnp.exp(m - m_new)
                pr = jnp.exp(s - m_new)
                l = alpha * l + jnp.sum(pr, axis=1, keepdims=True)
                pr_hi = pr.astype(CDT)
                pr_lo = (pr - pr_hi.astype(F32)).astype(CDT)
                acc = alpha * acc + jnp.dot(pr_hi, vj, preferred_element_type=F32)
                acc_lo = alpha * acc_lo + jnp.dot(pr_lo, vj, preferred_element_type=F32)
                return m_new, l, acc, acc_lo

            zero = jnp.zeros((blk, LANES), F32)
            init = (jnp.full((blk, 1), NEG, F32), jnp.zeros((blk, 1), F32), zero, zero)
            carry = lax.fori_loop(0, i, lambda j, c: tile(j, c, False), init)
            m, l, acc, acc_lo = tile(i, carry, True)
            outs.append(acc / l)
            los.append(acc_lo / l)
            lses.append(m + jnp.log(l))
        lo = lane < HEAD_DIM
        o_ref[...] = jnp.where(lo, outs[0], outs[1])
        olo_ref[...] = jnp.where(lo, los[0], los[1])
        lse_t = jnp.transpose(jnp.where(lo, lses[0], lses[1]))
        lse_ref[...] = jnp.concatenate([lse_t[0:1, :], lse_t[HEAD_DIM:HEAD_DIM + 1, :]], axis=0)

    return pl.pallas_call(
        body, grid=(N_PAIRS, nq),
        in_specs=[pl.BlockSpec((blk, LANES), lambda p, i: (i, p)),
                  pl.BlockSpec((T, LANES), lambda p, i: (0, N_PAIRS + p)),
                  pl.BlockSpec((T, LANES), lambda p, i: (0, 2 * N_PAIRS + p)),
                  pl.BlockSpec((blk, LANES), lambda p, i: (i, 0)),
                  pl.BlockSpec((None, 2, T), lambda p, i: (p, 0, 0))],
        out_specs=[pl.BlockSpec((blk, LANES), lambda p, i: (i, p)), pl.BlockSpec((blk, LANES), lambda p, i: (i, p)),
                   pl.BlockSpec((None, 2, blk), lambda p, i: (p, 0, i))],
        out_shape=[_sds((T, N_PAIRS * LANES), F32), _sds((T, N_PAIRS * LANES), F32), _sds((N_PAIRS, 2, T), F32)],
        name=name, compiler_params=_cp(("parallel", "arbitrary")))(qkv, qkv, qkv, fcol, frow)


def _attn_delta(dcat, o, o_lo, name, blk=512):
    T = o.shape[0]
    blk = _tile(T, blk)

    def body(d_ref, o_ref, olo_ref, out_ref):
        prod = d_ref[:, 256:768].astype(CDT).astype(F32) * (o_ref[...] + olo_ref[...])
        pt = jnp.transpose(prod)
        out_ref[...] = jnp.sum(pt.reshape(N_HEADS, HEAD_DIM, blk), axis=1)

    return pl.pallas_call(
        body, grid=(T // blk,),
        in_specs=[pl.BlockSpec((blk, 1024), lambda i: (i, 0)), pl.BlockSpec((blk, 512), lambda i: (i, 0)),
                  pl.BlockSpec((blk, 512), lambda i: (i, 0))],
        out_specs=pl.BlockSpec((N_HEADS, blk), lambda i: (0, i)),
        out_shape=_sds((N_HEADS, T), F32), name=name, compiler_params=_cp(("parallel",)))(dcat, o, o_lo)


def _attn_bwd(qkv, dcat, fcol, frow, lse, delta, name, blk=512):
    T = qkv.shape[0]
    blk = _tile(T, blk)
    nq = T // blk
    nt = (((1,), (1,)), ((), ()))
    tn = (((0,), (0,)), ((), ()))

    def body(q_ref, do_ref, k_ref, v_ref, fc_ref, fr_ref, lse_ref, dl_ref, dq_ref, dk_ref, dv_ref, df_ref):
        p_id = pl.program_id(0)
        j = pl.program_id(1)

        @pl.when(j == 0)
        def _():
            dq_ref[...] = jnp.zeros_like(dq_ref)

        k2 = k_ref[...]
        v2 = v_ref[...]
        fc = fc_ref[...]
        lane = lax.broadcasted_iota(jnp.int32, (blk, LANES), 1)
        tri = lax.broadcasted_iota(jnp.int32, (blk, blk), 0) <= lax.broadcasted_iota(jnp.int32, (blk, blk), 1)
        dks, dvs, dfs = [], [], []
        for a in range(2):
            hm = _half_mask(k2.shape, a)
            ka = jnp.where(hm, k2, jnp.zeros_like(k2)) * ATT_SCALE
            va = jnp.where(hm, v2, jnp.zeros_like(v2))
            fk = jnp.sum(jnp.where(lane == 2 * p_id + a, fc, 0.0), axis=1, keepdims=True)

            def tile(i, carry, masked):
                dk_acc, dv_acc, df_acc = carry
                rows = pl.ds(pl.multiple_of(i * blk, blk), blk)
                qi = q_ref[rows, :]
                doi = do_ref[rows, :].astype(CDT)
                st = lax.dot_general(ka, qi, nt, preferred_element_type=F32)
                e = st + (fr_ref[a:a + 1, rows] - fk) - lse_ref[a:a + 1, rows]
                if masked:
                    e = jnp.where(tri, e, NEG)
                pt = jnp.exp(e)
                dpt = lax.dot_general(va, doi, nt, preferred_element_type=F32)
                ds32 = pt * (dpt - dl_ref[a:a + 1, rows])
                dst = ds32.astype(CDT)
                df_acc = df_acc + jnp.sum(ds32, axis=1, keepdims=True)
                dv_acc = dv_acc + jnp.dot(pt.astype(CDT), doi, preferred_element_type=F32)
                dk_acc = dk_acc + jnp.dot(dst, qi, preferred_element_type=F32)
                dq_ref[rows, :] += lax.dot_general(dst, ka, tn, preferred_element_type=F32)
                return dk_acc, dv_acc, df_acc

            init = (jnp.zeros((blk, LANES), F32), jnp.zeros((blk, LANES), F32), jnp.zeros((blk, 1), F32))
            carry = tile(j, init, True)
            dk_acc, dv_acc, df_acc = lax.fori_loop(j + 1, nq, lambda i, c: tile(i, c, False), carry)
            dks.append(dk_acc)
            dvs.append(dv_acc)
            dfs.append(df_acc)
        lo = lane < HEAD_DIM
        dk_ref[...] = jnp.where(lo, dks[0], dks[1]) * ATT_SCALE
        dv_ref[...] = jnp.where(lo, dvs[0], dvs[1])
        df_ref[...] = -jnp.where(lo, dfs[0], dfs[1])

    res = pl.BlockSpec((T, LANES), lambda p, j: (0, p))
    rows = pl.BlockSpec((None, 2, T), lambda p, j: (p, 0, 0))
    kv_out = pl.BlockSpec((blk, LANES), lambda p, j: (j, p))
    return pl.pallas_call(
        body, grid=(N_PAIRS, nq),
        in_specs=[res, pl.BlockSpec((T, LANES), lambda p, j: (0, 2 + p)),
                  pl.BlockSpec((blk, LANES), lambda p, j: (j, N_PAIRS + p)),
                  pl.BlockSpec((blk, LANES), lambda p, j: (j, 2 * N_PAIRS + p)),
                  pl.BlockSpec((blk, LANES), lambda p, j: (j, 0)), rows, rows, rows],
        out_specs=[res, kv_out, kv_out, kv_out],
        out_shape=[_sds((T, N_PAIRS * LANES), F32)] * 4,
        name=name, compiler_params=_cp(("parallel", "arbitrary")))(qkv, dcat, qkv, qkv, fcol, frow, lse, delta)


def _mixer_fwd(x, wts, tag, dep=None):
    T = x.shape[0]
    h = _rms_fwd(x, wts["mix_norm"], f"{tag}_norm", dep)
    qkv = _mm([(h, wts["win_qkv"], False)], out_dtype=CDT, tm=1024, tn=768, name=f"{tag}_in_qkv")
    rest = _mm([(h, wts["win_rest"], False)], tm=1024, name=f"{tag}_in_rest")
    fcol, frow8 = _fgate_fwd(rest, wts["fbias"], f"{tag}_fgate")
    frow = frow8.reshape(N_PAIRS, 2, T)
    ya = _pool_fwd(rest, wts["pool_wbd"], wts["pool_scale"], f"{tag}_pool")
    o, o_lo, lse = _attn_fwd(qkv, fcol, frow, f"{tag}_attn")
    yc, yconv = _conv_fwd(rest, wts["conv_w"], wts["conv_b"], wts["conv_ln_g"], wts["conv_ln_b"], f"{tag}_conv")
    cat = jnp.concatenate([ya, o.astype(CDT), yc], axis=1)
    y = _mm([(cat, wts["w_out"], False)], res=x, tn=512, name=f"{tag}_out")
    return y, (x, h, qkv, rest, fcol, frow, o, o_lo, lse, yconv, cat)


def _mixer_bwd(saved, wts, gout, tag, dep=None):
    x, h, qkv, rest, fcol, frow, o, o_lo, lse, yconv, cat = saved
    T = x.shape[0]
    dcat = _mm([(gout, wts["w_out"], True)], tn=512, name=f"{tag}_dcat", dep=dep)
    dwout = _mm_tn(cat, gout, name=f"{tag}_dwout")
    du, dpw, dpsc = _pool_bwd(dcat, rest, wts["pool_wbd"], wts["pool_scale"], f"{tag}_pool_bwd")
    delta = _attn_delta(dcat, o, o_lo, f"{tag}_attn_delta").reshape(N_PAIRS, 2, T)
    dq, dk, dv, dfk = _attn_bwd(qkv, dcat, fcol, frow, lse, delta, f"{tag}_attn_bwd")
    dz, dfb = _fgate_bwd(dfk, rest, wts["fbias"], f"{tag}_fgate_bwd")
    da, dg, dcw, dcb, dlg, dlb = _conv_bwd(dcat, yconv, rest, wts["conv_w"], wts["conv_ln_g"], wts["conv_ln_b"],
                                           f"{tag}_conv_bwd")
    dp_qkv = jnp.concatenate([dq, dk, dv], axis=1).astype(CDT)
    dp_rest = jnp.concatenate([du, da, dg, dz], axis=1)
    dwin_qkv = _mm_tn(h, dp_qkv, name=f"{tag}_dwin_qkv")
    dwin_rest = _mm_tn(h, dp_rest, name=f"{tag}_dwin_rest")
    dh = _mm([(dp_qkv, wts["win_qkv"], True), (dp_rest, wts["win_rest"], True)], tn=512, name=f"{tag}_dh")
    gin, dgamma = _rms_bwd(x, wts["mix_norm"], dh, gout, f"{tag}_norm_bwd")
    dwin = _split_win(dwin_qkv, dwin_rest, f"{tag}_dwin_split")
    dpool_w = jnp.stack([dpw[64 * g:64 * g + 64, 64 * g:64 * g + 64] for g in range(4)])
    grads = dict(mix_norm=dgamma[0], w_in=dwin, pool_w=dpool_w, pool_scale=dpsc[0], forget_bias=dfb[0, 0:N_HEADS],
                 conv_w=dcw[0:CONV_K], conv_b=dcb[0], conv_ln_g=dlg[0], conv_ln_b=dlb[0], w_out=dwout)
    return gin, grads


def _rep_layer(rep, l):
    pw = rep["pool_w"][l].astype(CDT)
    wbd = jnp.zeros((256, 256), CDT)
    for g in range(4):
        wbd = lax.dynamic_update_slice(wbd, pw[g], (64 * g, 64 * g))
    return dict(
        ffn1_norm=rep["ffn1_norm"][l][None], ffn2_norm=rep["ffn2_norm"][l][None], mix_norm=rep["mix_norm"][l][None],
        fbias=jnp.pad(rep["forget_bias"][l], (0, LANES - N_HEADS))[None],
        pool_wbd=wbd, pool_scale=rep["pool_scale"][l][None], conv_b=rep["conv_b"][l][None],
        conv_ln_g=rep["conv_ln_g"][l][None], conv_ln_b=rep["conv_ln_b"][l][None])


def _local_step(x, target, rep, weights_for, grads_ready):
    depth = rep["ffn1_norm"].shape[0]
    kept = []
    for l in range(depth):
        r = _rep_layer(rep, l)
        w1, dep = weights_for(l, "ffn1", x)
        x, s1 = _ffn_fwd(x, r["ffn1_norm"], w1["w_gate"], w1["w_up"], w1["w_down"], f"l{l}_ffn1", dep)
        wm, dep = weights_for(l, "mix", x)
        wm = dict(r, win_qkv=wm["win_qkv"], win_rest=wm["win_rest"], w_out=wm["w_out"],
                  conv_w=jnp.pad(wm["conv_w"], ((0, CONV_HALO - CONV_K), (0, 0))))
        x, s2 = _mixer_fwd(x, wm, f"l{l}_mix", dep)
        w2, dep = weights_for(l, "ffn2", x)
        x, s3 = _ffn_fwd(x, r["ffn2_norm"], w2["w_gate"], w2["w_up"], w2["w_down"], f"l{l}_ffn2", dep)
        kept.append((r, w1, wm, w2, s1, s2, s3))
    loss, g, dfinal = _loss_bwd(x, rep["final_norm"][None], target, "loss_head")
    dep = grads_ready(None, "final", dict(final_norm=dfinal[0]), g)
    for l in reversed(range(depth)):
        r, w1, wm, w2, s1, s2, s3 = kept[l]
        g, dn, dwg, dwu, dwd = _ffn_bwd(s3, r["ffn2_norm"], w2["w_gate"], w2["w_up"], w2["w_down"], g, f"l{l}_ffn2", dep)
        dep = grads_ready(l, "ffn2", dict(ffn2_norm=dn[0], ffn2_w_gate=dwg, ffn2_w_up=dwu, ffn2_w_down=dwd), g)
        g, gm = _mixer_bwd(s2, wm, g, f"l{l}_mix", dep)
        dep = grads_ready(l, "mix", gm, g)
        g, dn, dwg, dwu, dwd = _ffn_bwd(s1, r["ffn1_norm"], w1["w_gate"], w1["w_up"], w1["w_down"], g, f"l{l}_ffn1", dep)
        dep = grads_ready(l, "ffn1", dict(ffn1_norm=dn[0], ffn1_w_gate=dwg, ffn1_w_up=dwu, ffn1_w_down=dwd), g)
    return loss, g


def _mesh_pos():
    return lax.axis_index("x"), lax.axis_index("y"), lax.axis_index("c")


def _dev_block(ref, dev, by_rows):
    if by_rows:
        r = ref.shape[1] // N_DEV
        return ref.at[:, pl.ds(dev * r, r), :]
    return ref.at[dev]


def _all_gather(shards, by_rows, name):
    n_arr = len(shards)
    out_shape = [_sds((s.shape[0], N_DEV * s.shape[1], s.shape[2]) if br else (N_DEV,) + s.shape, s.dtype)
                 for s, br in zip(shards, by_rows)]

    def body(*refs):
        xs, outs = refs[:n_arr], refs[n_arr:2 * n_arr]
        send_sems, recv_sems, local_sems = refs[2 * n_arr:]
        x, y, c = _mesh_pos()
        me, sibling = (x, y, c), (x, y, 1 - c)
        chips = [(1 - x, y), (x, 1 - y), (1 - x, 1 - y)]

        def rows(a, px, py, pc):
            return _dev_block(outs[a], 4 * px + 2 * py + pc, by_rows[a])

        def copy(k, a, block, to, src=None):
            return pltpu.make_async_remote_copy(
                src_ref=rows(a, *block) if src is None else src, dst_ref=rows(a, *block),
                send_sem=send_sems.at[k, a], recv_sem=recv_sems.at[k, a],
                device_id=to, device_id_type=pl.DeviceIdType.MESH)

        arrs = range(n_arr)
        mine = [pltpu.make_async_copy(xs[a], rows(a, *me), local_sems.at[a]) for a in arrs]
        for cp in mine:
            cp.start()
        first = [copy(0, a, me, sibling, src=xs[a]) for a in arrs]
        first += [copy(1 + j, a, me, (*chip, c), src=xs[a]) for j, chip in enumerate(chips) for a in arrs]
        for cp in first:
            cp.start()
        passed = []
        for j, chip in enumerate(chips):
            for a in arrs:
                copy(1 + j, a, (*chip, c), me).wait_recv()
                passed.append(copy(4 + j, a, (*chip, c), sibling))
                passed[-1].start()
        for a in arrs:
            copy(0, a, sibling, me).wait_recv()
        for j, chip in enumerate(chips):
            for a in arrs:
                copy(4 + j, a, (*chip, 1 - c), me).wait_recv()
        for cp in first + passed:
            cp.wait_send()
        for cp in mine:
            cp.wait()

    hbm = pl.BlockSpec(memory_space=pl.ANY)
    return pl.pallas_call(
        body, out_shape=out_shape, in_specs=[hbm] * n_arr, out_specs=[hbm] * n_arr,
        scratch_shapes=[pltpu.SemaphoreType.DMA((7, n_arr)), pltpu.SemaphoreType.DMA((7, n_arr)),
                        pltpu.SemaphoreType.DMA((n_arr,))],
        name=name)(*shards)


def _exchange(parts, by_rows, name):
    n_arr = len(parts)
    out_shape = [_sds((N_DEV, p.shape[0], p.shape[1] // N_DEV, p.shape[2]) if br else p.shape, p.dtype)
                 for p, br in zip(parts, by_rows)]

    def body(*refs):
        ps, outs = refs[:n_arr], refs[n_arr:2 * n_arr]
        send_sems, recv_sems, local_sems = refs[2 * n_arr:]
        x, y, c = _mesh_pos()
        my = 4 * x + 2 * y + c
        arrs = range(n_arr)
        mine = [pltpu.make_async_copy(_dev_block(ps[a], my, by_rows[a]), outs[a].at[my], local_sems.at[a]) for a in arrs]
        for cp in mine:
            cp.start()
        copies = []
        for k in range(1, N_DEV):
            px, py, pc = x ^ (k >> 2), y ^ ((k >> 1) & 1), c ^ (k & 1)
            for a in arrs:
                copies.append(pltpu.make_async_remote_copy(
                    src_ref=_dev_block(ps[a], 4 * px + 2 * py + pc, by_rows[a]), dst_ref=outs[a].at[my],
                    send_sem=send_sems.at[k - 1, a], recv_sem=recv_sems.at[k - 1, a],
                    device_id=(px, py, pc), device_id_type=pl.DeviceIdType.MESH))
        for cp in copies:
            cp.start()
        for cp in copies:
            cp.wait()
        for cp in mine:
            cp.wait()

    hbm = pl.BlockSpec(memory_space=pl.ANY)
    return pl.pallas_call(
        body, out_shape=out_shape, in_specs=[hbm] * n_arr, out_specs=[hbm] * n_arr,
        scratch_shapes=[pltpu.SemaphoreType.DMA((7, n_arr)), pltpu.SemaphoreType.DMA((7, n_arr)),
                        pltpu.SemaphoreType.DMA((n_arr,))],
        name=name)(*parts)


def _peer_copies(srcs, lands, send_sems, recv_sems, gather, by_rows):
    n_arr = len(srcs)
    x, y, c = _mesh_pos()
    my = 4 * x + 2 * y + c
    out = []
    for k in range(1, N_DEV):
        px, py, pc = x ^ (k >> 2), y ^ ((k >> 1) & 1), c ^ (k & 1)
        peer = 4 * px + 2 * py + pc
        for a in range(n_arr):
            src = srcs[a] if gather else _dev_block(srcs[a], peer, by_rows[a])
            dst = _dev_block(lands[a], my, by_rows[a]) if gather else lands[a].at[my]
            out.append(pltpu.make_async_remote_copy(
                src_ref=src, dst_ref=dst, send_sem=send_sems.at[(k - 1) * n_arr + a],
                recv_sem=recv_sems.at[(k - 1) * n_arr + a], device_id=(px, py, pc), device_id_type=pl.DeviceIdType.MESH))
    return out


def _land_shape(s, gather, by_rows):
    if gather:
        return (s.shape[0], N_DEV * s.shape[1], s.shape[2]) if by_rows else (N_DEV,) + s.shape
    return (N_DEV, s.shape[0], s.shape[1] // N_DEV, s.shape[2]) if by_rows else s.shape


_HBM = pl.BlockSpec(memory_space=pltpu.HBM)
_SEM = pl.BlockSpec(memory_space=pltpu.SEMAPHORE)


def _xfer_start(srcs, gather, by_rows, name):
    n = len(srcs)
    lands = [lax.empty(_land_shape(s, gather, br), s.dtype) for s, br in zip(srcs, by_rows)]
    ins = [pltpu.with_memory_space_constraint(a, pltpu.HBM) for a in list(srcs) + lands]

    def body(*refs):
        for cp in _peer_copies(refs[:n], refs[n:2 * n], refs[2 * n], refs[2 * n + 1], gather, by_rows):
            cp.start()
        refs[-1][...] = jnp.zeros_like(refs[-1])

    sems = pltpu.SemaphoreType.DMA(((N_DEV - 1) * n,))
    outs = pl.pallas_call(
        body, name=name,
        out_shape=(sems, sems, *[pltpu.HBM(a.shape, a.dtype) for a in ins], _sds((8, LANES), F32)),
        in_specs=[_HBM] * (2 * n), out_specs=(_SEM, _SEM, *[_HBM] * (2 * n), pl.BlockSpec(memory_space=pltpu.VMEM)),
        input_output_aliases={i: 2 + i for i in range(2 * n)},
        compiler_params=pltpu.CompilerParams(has_side_effects=pltpu.SideEffectType.DATAFLOW_SIDE_EFFECTING))(*ins)
    return outs[0], outs[1], list(outs[2:-1]), outs[-1]


def _xfer_wait(started, after, gather, by_rows, name):
    send_sems, recv_sems, bufs, _ = started
    n = len(bufs) // 2

    def body(*refs):
        for cp in _peer_copies(refs[:n], refs[n:2 * n], refs[2 * n], refs[2 * n + 1], gather, by_rows):
            cp.wait_send()
            cp.wait_recv()

    outs = pl.pallas_call(
        body, name=name, out_shape=tuple(pltpu.HBM(a.shape, a.dtype) for a in bufs),
        in_specs=[_HBM] * (2 * n) + [_SEM, _SEM, pl.BlockSpec(memory_space=pl.ANY)], out_specs=tuple([_HBM] * (2 * n)),
        input_output_aliases={i: i for i in range(2 * n)},
        compiler_params=pltpu.CompilerParams(has_side_effects=pltpu.SideEffectType.DATAFLOW_SIDE_EFFECTING))(
            *bufs, send_sems, recv_sems, after)
    x, y, c = _mesh_pos()
    my = 4 * x + 2 * y + c
    res = []
    for src, land, br in zip(outs[:n], outs[n:], by_rows):
        zeros = (0,) * (land.ndim - 1)
        if gather and br:
            res.append(lax.dynamic_update_slice(land, src, (0, my * src.shape[1], 0)))
        elif gather:
            res.append(lax.dynamic_update_slice(land, src[None], (my,) + zeros))
        elif br:
            r = src.shape[1] // N_DEV
            own = lax.dynamic_slice(src, (0, my * r, 0), (src.shape[0], r, src.shape[2]))
            res.append(lax.dynamic_update_slice(land, own[None], (my,) + zeros))
        else:
            res.append(lax.dynamic_update_slice(land, lax.dynamic_index_in_dim(src, my, 0, keepdims=True), (my,) + zeros))
    return res


def _adam_update(g, w, m, v):
    c1 = 1.0 - ADAM_B1 ** ADAM_STEP
    c2 = 1.0 - ADAM_B2 ** ADAM_STEP
    nm = ADAM_B1 * m + (1.0 - ADAM_B1) * g
    nv = ADAM_B2 * v + (1.0 - ADAM_B2) * (g * g)
    return -ADAM_LR * ((nm / c1) / (jnp.sqrt(nv / c2) + ADAM_EPS) + ADAM_WD * w), nm, nv


def _adamw_body(p_ref, w_ref, m_ref, v_ref, g_ref, d_ref, nm_ref, nv_ref):
    g = p_ref[0]
    for i in range(1, N_DEV):
        g = g + p_ref[i]
    g_ref[...] = g
    d_ref[...], nm_ref[...], nv_ref[...] = _adam_update(g, w_ref[...], m_ref[...], v_ref[...])


def _adamw(parts, w, m, v, name, tr=1536):
    R = w.shape[0]
    tr = max(t for t in range(8, tr + 1, 8) if R % t == 0)

    def body(*refs):
        _adamw_body(*refs)

    row = pl.BlockSpec((tr, LANES), lambda i: (i, 0))
    return pl.pallas_call(
        body, grid=(R // tr,),
        in_specs=[pl.BlockSpec((N_DEV, tr, LANES), lambda i: (0, i, 0)), row, row, row],
        out_specs=[row, row, row, row], out_shape=[_sds((R, LANES), F32)] * 4,
        name=name, compiler_params=_cp(("parallel",)))(parts, w, m, v)


def _adamw_split(recvs, w, m, v, name, tr):
    depth, r, c = w.shape
    assert depth == len(recvs)
    tr = _tile(r, tr)

    def body(*refs):
        layer = pl.program_id(0)
        for ll in range(depth):
            @pl.when(layer == ll)
            def _(ll=ll):
                _adamw_body(refs[ll], *refs[depth:])

    wspec = pl.BlockSpec((None, tr, c), lambda l, i: (l, i, 0))
    rspecs = [pl.BlockSpec((N_DEV, None, tr, c), lambda l, i, ll=ll, t=t: (0, t, jnp.where(l == ll, i, 0), 0))
              for ll, (_, t) in enumerate(recvs)]
    return pl.pallas_call(
        body, grid=(depth, r // tr), in_specs=rspecs + [wspec, wspec, wspec],
        out_specs=[wspec] * 4, out_shape=[_sds(w.shape, F32)] * 4,
        name=name, compiler_params=_cp(("arbitrary", "arbitrary")))(*[a for a, _ in recvs], w, m, v)


def _merge_cols(g, name, tr=256):
    _, nt, K, n = g.shape
    tr = _tile(K, tr)

    def body(g_ref, o_ref):
        o_ref[...] = jnp.concatenate([g_ref[j] for j in range(N_DEV)], axis=1)

    return pl.pallas_call(
        body, grid=(nt, K // tr),
        in_specs=[pl.BlockSpec((N_DEV, None, tr, n), lambda t, i: (0, t, i, 0))],
        out_specs=pl.BlockSpec((None, tr, N_DEV * n), lambda t, i: (t, i, 0)),
        out_shape=_sds((nt, K, N_DEV * n), g.dtype), name=name, compiler_params=_cp(("parallel", "parallel")))(g)


def _merge_win(g, name, tr=256):
    _, nt, K, n = g.shape
    tr = _tile(K, tr)

    def body(g_ref, q_ref, r_ref):
        full = jnp.concatenate([g_ref[j] for j in range(N_DEV)], axis=1)
        q_ref[...] = full[:, 256:1792]
        zpad = jnp.zeros((tr, REST_W - 776), full.dtype)
        r_ref[...] = jnp.concatenate([full[:, 0:256], full[:, 1800:2312], full[:, 1792:1800], zpad], axis=1)

    return pl.pallas_call(
        body, grid=(nt, K // tr),
        in_specs=[pl.BlockSpec((N_DEV, None, tr, n), lambda t, i: (0, t, i, 0))],
        out_specs=[pl.BlockSpec((None, tr, 1536), lambda t, i: (t, i, 0)), pl.BlockSpec((None, tr, REST_W), lambda t, i: (t, i, 0))],
        out_shape=[_sds((nt, K, 1536), g.dtype), _sds((nt, K, REST_W), g.dtype)],
        name=name, compiler_params=_cp(("parallel", "parallel")))(g)


def _split_win(dq, dr, name, tr=256):
    K = dq.shape[0]
    tr = _tile(K, tr)
    n = (dq.shape[1] + 776) // N_DEV

    def body(q_ref, r_ref, o_ref):
        r = r_ref[...]
        full = jnp.concatenate([r[:, 0:256], q_ref[...], r[:, 768:776], r[:, 256:768]], axis=1)
        for j in range(N_DEV):
            o_ref[j] = full[:, n * j:n * (j + 1)]

    return pl.pallas_call(
        body, grid=(K // tr,),
        in_specs=[pl.BlockSpec((tr, dq.shape[1]), lambda i: (i, 0)), pl.BlockSpec((tr, REST_W), lambda i: (i, 0))],
        out_specs=pl.BlockSpec((N_DEV, tr, n), lambda i: (0, i, 0)),
        out_shape=_sds((N_DEV, K, n), F32), name=name, compiler_params=_cp(("parallel",)))(dq, dr)


WEIGHTS = ["ffn1_norm", "ffn1_w_gate", "ffn1_w_up", "ffn1_w_down", "mix_norm", "w_in", "pool_w", "pool_scale",
           "forget_bias", "conv_w", "conv_b", "conv_ln_g", "conv_ln_b", "w_out", "ffn2_norm", "ffn2_w_gate",
           "ffn2_w_up", "ffn2_w_down", "final_norm"]
FFN_COL = ["ffn1_w_gate", "ffn1_w_up", "ffn2_w_gate", "ffn2_w_up"]
FFN_ROW = ["ffn1_w_down", "ffn2_w_down"]
BIG = FFN_COL + FFN_ROW + ["w_in", "w_out"]
SMALL = [n for n in WEIGHTS if n not in BIG]


def _padded(n):
    return -(-n // PACK_ALIGN) * PACK_ALIGN


def _flat_pad(a):
    f = a.reshape(-1)
    return jnp.pad(f, (0, _padded(f.shape[0]) - f.shape[0]))


def _split8(a, axis):
    shp = a.shape
    a = a.reshape(shp[:axis] + (N_DEV, shp[axis] // N_DEV) + shp[axis + 1:])
    return jnp.moveaxis(a, axis, 0)


def _merge8(a, axis):
    a = jnp.moveaxis(a, 0, axis)
    shp = a.shape
    return a.reshape(shp[:axis] + (shp[axis] * shp[axis + 1],) + shp[axis + 2:])


def _pack_small(arrs):
    return jnp.concatenate([_flat_pad(arrs[n]) for n in SMALL]).reshape(-1, LANES)


def _pack_small_parts(grads):
    cols = []
    for n in SMALL:
        g = grads[n]
        if n == "conv_w":
            s = _split8(g, 2).reshape(N_DEV, -1)
        else:
            s = jnp.broadcast_to(g.reshape(1, -1), (N_DEV, g.size))
        cols.append(jnp.pad(s, ((0, 0), (0, _padded(s.shape[1]) - s.shape[1]))))
    return jnp.concatenate(cols, axis=1).reshape(N_DEV, -1, LANES)


def _unpack_small(buf, like):
    flat = buf.reshape(-1)
    out, off = {}, 0
    for n in SMALL:
        size = like[n].size
        out[n] = flat[off:off + size].reshape(like[n].shape)
        off += _padded(size)
    return out


class _Comm:
    def __init__(self, w):
        self.w = w
        self.bf = {n: w[n].astype(CDT) for n in BIG}
        self.ready = {}
        self.grads = {}
        self.recv = {}

    def _ffn_shards(self, l, which):
        cols = jnp.stack([self.bf[f"{which}_w_gate"][l], self.bf[f"{which}_w_up"][l]])
        return cols, self.bf[f"{which}_w_down"][l][None]

    def _put_ffn(self, l, which, cols, rows, t):
        self.ready[(l, which)] = dict(w_gate=cols[t], w_up=cols[t + 1], w_down=rows[t // 2])

    def weights_for(self, l, stage, x):
        bf = self.bf
        dep = None
        if (l, stage) == (0, "ffn1"):
            ga, gd = _all_gather(list(self._ffn_shards(0, "ffn1")), [False, True], "gather_l0_ffn1")
            self._put_ffn(0, "ffn1", _merge_cols(ga, "merge_l0_ffn1"), gd, 0)
            cols, rows = self._ffn_shards(0, "ffn2")
            self.started = _xfer_start([cols, rows, bf["w_in"][0:1], bf["w_out"][0:1], self.w["conv_w"]], True,
                                       [False, True, False, True, False], "gather_l0_start")
            dep = self.started[3]
        elif (l, stage) == (0, "mix"):
            ga, gd, gi, go, gc = _xfer_wait(self.started, x, True, [False, True, False, True, False], "gather_l0_wait")
            self._put_ffn(0, "ffn2", _merge_cols(ga, "merge_l0_ffn2"), gd, 0)
            q, r = _merge_win(gi, "merge_l0_w_in")
            self.conv_w = _merge8(gc, 2)
            self.ready[(0, "mix")] = dict(win_qkv=q[0], win_rest=r[0], w_out=go[0], conv_w=self.conv_w[0])
            c1, r1 = self._ffn_shards(1, "ffn1")
            c2, r2 = self._ffn_shards(1, "ffn2")
            self.started = _xfer_start([jnp.concatenate([c1, c2]), jnp.concatenate([r1, r2]), bf["w_in"][1:2], bf["w_out"][1:2]],
                                       True, [False, True, False, True], "gather_l1_start")
            dep = self.started[3]
        elif (l, stage) == (1, "ffn1"):
            ga, gd, gi, go = _xfer_wait(self.started, x, True, [False, True, False, True], "gather_l1_wait")
            cols = _merge_cols(ga, "merge_l1_ffn")
            self._put_ffn(1, "ffn1", cols, gd, 0)
            self._put_ffn(1, "ffn2", cols, gd, 2)
            q, r = _merge_win(gi, "merge_l1_w_in")
            self.ready[(1, "mix")] = dict(win_qkv=q[0], win_rest=r[0], w_out=go[0], conv_w=self.conv_w[1])
        return self.ready[(l, stage)], dep

    def grads_ready(self, l, stage, grads, g):
        for n, v in grads.items():
            self.grads[(l, n)] = v
        gr = self.grads
        if (l, stage) == (1, "ffn1"):
            self.sent1 = _xfer_start(
                [jnp.stack([gr[(1, n)] for n in FFN_COL], axis=1), jnp.stack([gr[(1, n)] for n in FFN_ROW]),
                 gr[(1, "w_in")][:, None], gr[(1, "w_out")][None]], False, [False, True, False, True], "grads_l1_start")
            return self.sent1[3]
        if (l, stage) == (0, "mix"):
            self.sent0 = _xfer_start(
                [jnp.stack([gr[(0, "ffn2_w_gate")], gr[(0, "ffn2_w_up")]], axis=1), gr[(0, "ffn2_w_down")][None],
                 gr[(0, "w_in")][:, None], gr[(0, "w_out")][None]], False, [False, True, False, True], "grads_l0_start")
            return self.sent0[3]
        return None

    def finish(self, m, v, after):
        w, gr = self.w, self.grads
        depth = range(w["w_in"].shape[0])
        small = {n: (gr[(None, n)] if n == "final_norm" else jnp.stack([gr[(l, n)] for l in depth])) for n in SMALL}
        c0, r0, rs = _exchange(
            [jnp.stack([gr[(0, "ffn1_w_gate")], gr[(0, "ffn1_w_up")]], axis=1), gr[(0, "ffn1_w_down")][None],
             _pack_small_parts(small)], [False, True, False], "exchange_l0_ffn1")
        c1, r1, i1, o1 = _xfer_wait(self.sent1, after, False, [False, True, False, True], "grads_l1_wait")
        c2, r2, i0, o0 = _xfer_wait(self.sent0, after, False, [False, True, False, True], "grads_l0_wait")
        res = {}
        for n, src0, t0, t1 in (("ffn1_w_gate", c0, 0, 0), ("ffn1_w_up", c0, 1, 1), ("ffn2_w_gate", c2, 0, 2),
                                ("ffn2_w_up", c2, 1, 3)):
            res[n] = _adamw_split([(src0, t0), (c1, t1)], w[n], m[n], v[n], f"adamw_{n}", 256)
        res["ffn1_w_down"] = _adamw_split([(r0, 0), (r1, 0)], w["ffn1_w_down"], m["ffn1_w_down"], v["ffn1_w_down"],
                                          "adamw_ffn1_w_down", 176)
        res["ffn2_w_down"] = _adamw_split([(r2, 0), (r1, 1)], w["ffn2_w_down"], m["ffn2_w_down"], v["ffn2_w_down"],
                                          "adamw_ffn2_w_down", 176)
        res["w_in"] = _adamw_split([(i0, 0), (i1, 0)], w["w_in"], m["w_in"], v["w_in"], "adamw_w_in", 256)
        res["w_out"] = _adamw_split([(o0, 0), (o1, 0)], w["w_out"], m["w_out"], v["w_out"], "adamw_w_out", 128)
        packed = _adamw(rs, _pack_small(w), _pack_small(m), _pack_small(v), "adamw_small")
        unpacked = [_unpack_small(b, w) for b in packed]
        for n in SMALL:
            res[n] = [u[n] for u in unpacked]
        return res


def kernel(x, ffn1_norm, ffn1_w_gate, ffn1_w_up, ffn1_w_down, mix_norm, w_in, pool_w, pool_scale, forget_bias, conv_w, conv_b, conv_ln_g, conv_ln_b, w_out, ffn2_norm, ffn2_w_gate, ffn2_w_up, ffn2_w_down, final_norm, loss_target, m_ffn1_norm, m_ffn1_w_gate, m_ffn1_w_up, m_ffn1_w_down, m_mix_norm, m_w_in, m_pool_w, m_pool_scale, m_forget_bias, m_conv_w, m_conv_b, m_conv_ln_g, m_conv_ln_b, m_w_out, m_ffn2_norm, m_ffn2_w_gate, m_ffn2_w_up, m_ffn2_w_down, m_final_norm, v_ffn1_norm, v_ffn1_w_gate, v_ffn1_w_up, v_ffn1_w_down, v_mix_norm, v_w_in, v_pool_w, v_pool_scale, v_forget_bias, v_conv_w, v_conv_b, v_conv_ln_g, v_conv_ln_b, v_w_out, v_ffn2_norm, v_ffn2_w_gate, v_ffn2_w_up, v_ffn2_w_down, v_final_norm):
    w = dict(zip(WEIGHTS, (ffn1_norm, ffn1_w_gate, ffn1_w_up, ffn1_w_down, mix_norm, w_in, pool_w, pool_scale, forget_bias,
                           conv_w, conv_b, conv_ln_g, conv_ln_b, w_out, ffn2_norm, ffn2_w_gate, ffn2_w_up, ffn2_w_down,
                           final_norm)))
    m = dict(zip(WEIGHTS, (m_ffn1_norm, m_ffn1_w_gate, m_ffn1_w_up, m_ffn1_w_down, m_mix_norm, m_w_in, m_pool_w, m_pool_scale,
                           m_forget_bias, m_conv_w, m_conv_b, m_conv_ln_g, m_conv_ln_b, m_w_out, m_ffn2_norm, m_ffn2_w_gate,
                           m_ffn2_w_up, m_ffn2_w_down, m_final_norm)))
    v = dict(zip(WEIGHTS, (v_ffn1_norm, v_ffn1_w_gate, v_ffn1_w_up, v_ffn1_w_down, v_mix_norm, v_w_in, v_pool_w, v_pool_scale,
                           v_forget_bias, v_conv_w, v_conv_b, v_conv_ln_g, v_conv_ln_b, v_w_out, v_ffn2_norm, v_ffn2_w_gate,
                           v_ffn2_w_up, v_ffn2_w_down, v_final_norm)))
    comm = _Comm(w)
    loss_row, gx = _local_step(x[0], loss_target[0], w, comm.weights_for, comm.grads_ready)
    loss = lax.psum(loss_row[0, 0], ("x", "y", "c"))
    res = comm.finish(m, v, gx)
    return (loss, gx[None], *[res[n][i] for i in range(4) for n in WEIGHTS])
```

```python
import math

import numpy as np
import jax
import jax.numpy as jnp
from jax import lax
from jax.experimental import pallas as pl
from jax.experimental.pallas import tpu as pltpu

F32 = jnp.float32
CDT = jnp.bfloat16
NORM_EPS = 1e-6
N_DEV = 8
LANES = 128
PACK_ALIGN = 8 * LANES
VMEM_LIMIT = 48 * 1024 * 1024

POOL_WINDOWS = (2, 4, 8, 16)
POOL_HALO = 16
CONV_K = 31
CONV_HALO = 32
HEAD_DIM = 64
N_HEADS = 8
N_PAIRS = N_HEADS // 2
ATT_SCALE = 1.0 / math.sqrt(HEAD_DIM)
NEG = -1e30

ADAM_LR, ADAM_B1, ADAM_B2, ADAM_EPS, ADAM_WD, ADAM_STEP = 0.001, 0.9, 0.999, 1e-08, 0.01, 10

REST_W = 896
REST_Z_BLK = 6


def _cp(sem):
    return pltpu.CompilerParams(dimension_semantics=sem, vmem_limit_bytes=VMEM_LIMIT)


def _tile(n, pref):
    t = min(n, pref)
    assert n % t == 0, (n, pref)
    return t


def _sigmoid(x):
    return 1.0 / (1.0 + jnp.exp(-x))


def _sds(shape, dtype):
    return jax.ShapeDtypeStruct(shape, dtype)


_ANY = pl.BlockSpec(memory_space=pl.ANY)


def _dep(dep):
    return ([], []) if dep is None else ([_ANY], [dep])


def _rms_fwd(x, g, name, dep=None):
    T, D = x.shape
    tm = _tile(T, 1024)

    def body(x_ref, g_ref, *rest):
        o_ref = rest[-1]
        xv = x_ref[...]
        r = lax.rsqrt(jnp.mean(xv * xv, axis=-1, keepdims=True) + NORM_EPS)
        o_ref[...] = (xv * r * g_ref[...]).astype(o_ref.dtype)

    dspec, darg = _dep(dep)
    return pl.pallas_call(
        body, grid=(T // tm,),
        in_specs=[pl.BlockSpec((tm, D), lambda i: (i, 0)), pl.BlockSpec((1, D), lambda i: (0, 0))] + dspec,
        out_specs=pl.BlockSpec((tm, D), lambda i: (i, 0)),
        out_shape=_sds((T, D), CDT), name=name, compiler_params=_cp(("parallel",)))(x, g, *darg)


def _rms_bwd(x, g, dh, gres, name):
    T, D = x.shape
    tm = _tile(T, 512)

    def body(x_ref, g_ref, dh_ref, gres_ref, gin_ref, dg_ref):
        i = pl.program_id(0)
        xv = x_ref[...]
        d = dh_ref[...]
        r = lax.rsqrt(jnp.mean(xv * xv, axis=-1, keepdims=True) + NORM_EPS)
        xh = xv * r
        dxh = d * g_ref[...]
        c = jnp.mean(dxh * xh, axis=-1, keepdims=True)
        gin_ref[...] = gres_ref[...] + r * (dxh - xh * c)
        part = jnp.sum(d * xh, axis=0, keepdims=True)

        @pl.when(i == 0)
        def _():
            dg_ref[...] = part

        @pl.when(i > 0)
        def _():
            dg_ref[...] += part

    row = pl.BlockSpec((tm, D), lambda i: (i, 0))
    vec = pl.BlockSpec((1, D), lambda i: (0, 0))
    return pl.pallas_call(
        body, grid=(T // tm,), in_specs=[row, vec, row, row], out_specs=[row, vec],
        out_shape=[_sds((T, D), F32), _sds((1, D), F32)], name=name, compiler_params=_cp(("arbitrary",)))(x, g, dh, gres)


def _loss_bwd(x, g, target, name):
    T, D = x.shape
    tm = _tile(T, 512)

    def body(x_ref, g_ref, t_ref, loss_ref, dx_ref, dg_ref):
        i = pl.program_id(0)
        xv = x_ref[...]
        gv = g_ref[...]
        r = lax.rsqrt(jnp.mean(xv * xv, axis=-1, keepdims=True) + NORM_EPS)
        xh = xv * r
        err = xh * gv - t_ref[...]
        lpart = 0.5 * jnp.sum(jnp.mean(err * err, axis=-1, keepdims=True), axis=0, keepdims=True)
        dy = err * (1.0 / D)
        dxh = dy * gv
        c = jnp.mean(dxh * xh, axis=-1, keepdims=True)
        dx_ref[...] = r * (dxh - xh * c)
        part = jnp.sum(dy * xh, axis=0, keepdims=True)
        lrow = jnp.broadcast_to(lpart, (1, LANES))

        @pl.when(i == 0)
        def _():
            dg_ref[...] = part
            loss_ref[...] = lrow

        @pl.when(i > 0)
        def _():
            dg_ref[...] += part
            loss_ref[...] += lrow

    row = pl.BlockSpec((tm, D), lambda i: (i, 0))
    vec = pl.BlockSpec((1, D), lambda i: (0, 0))
    return pl.pallas_call(
        body, grid=(T // tm,), in_specs=[row, vec, row],
        out_specs=[pl.BlockSpec((1, LANES), lambda i: (0, 0)), row, vec],
        out_shape=[_sds((1, LANES), F32), _sds((T, D), F32), _sds((1, D), F32)],
        name=name, compiler_params=_cp(("arbitrary",)))(x, g, target)


def _mm(pairs, *, name, res=None, alpha=1.0, out_dtype=F32, tm=512, tn=None, dep=None):
    T = pairs[0][0].shape[0]
    N = pairs[0][1].shape[0] if pairs[0][2] else pairs[0][1].shape[1]
    tm = _tile(T, tm)
    tn = N if tn is None else _tile(N, tn)
    flags = [p[2] for p in pairs]
    n_in = 2 * len(pairs)

    def body(*refs):
        o_ref = refs[-1]
        acc = None
        for p, bt in enumerate(flags):
            a = refs[2 * p][...].astype(CDT)
            b = refs[2 * p + 1][...]
            dims = (((1,), (1,)), ((), ())) if bt else (((1,), (0,)), ((), ()))
            d = lax.dot_general(a, b, dims, preferred_element_type=F32)
            acc = d if acc is None else acc + d
        if alpha != 1.0:
            acc = acc * alpha
        if res is not None:
            acc = refs[n_in][...] + acc
        o_ref[...] = acc.astype(o_ref.dtype)

    in_specs, args = [], []
    for a, b, bt in pairs:
        K = a.shape[1]
        in_specs.append(pl.BlockSpec((tm, K), lambda i, j: (i, 0)))
        in_specs.append(pl.BlockSpec((tn, K), lambda i, j: (j, 0)) if bt else pl.BlockSpec((K, tn), lambda i, j: (0, j)))
        args += [a, b]
    if res is not None:
        in_specs.append(pl.BlockSpec((tm, tn), lambda i, j: (i, j)))
        args.append(res)
    dspec, darg = _dep(dep)
    in_specs += dspec
    args += darg
    return pl.pallas_call(
        body, grid=(T // tm, N // tn), in_specs=in_specs,
        out_specs=pl.BlockSpec((tm, tn), lambda i, j: (i, j)),
        out_shape=_sds((T, N), out_dtype), name=name, compiler_params=_cp(("parallel", "arbitrary")))(*args)


def _mm_tn(a, b, *, name, alpha=1.0, tk=512):
    T, M = a.shape
    N = b.shape[1]
    tm = M if M <= 1024 else M // 2
    tn = N if N <= 1536 else N // 2
    assert M % tm == 0 and N % tn == 0 and tm % LANES == 0 and tn % LANES == 0
    tk = _tile(T, tk)
    nk = T // tk

    def body(a_ref, b_ref, o_ref):
        k = pl.program_id(2)
        d = lax.dot_general(a_ref[...].astype(CDT), b_ref[...].astype(CDT), (((0,), (0,)), ((), ())),
                            preferred_element_type=F32)

        @pl.when(k == 0)
        def _():
            o_ref[...] = d

        @pl.when(k > 0)
        def _():
            o_ref[...] += d

        if alpha != 1.0:
            @pl.when(k == nk - 1)
            def _():
                o_ref[...] *= alpha

    return pl.pallas_call(
        body, grid=(M // tm, N // tn, nk),
        in_specs=[pl.BlockSpec((tk, tm), lambda i, j, k: (k, i)), pl.BlockSpec((tk, tn), lambda i, j, k: (k, j))],
        out_specs=pl.BlockSpec((tm, tn), lambda i, j, k: (i, j)),
        out_shape=_sds((M, N), F32), name=name, compiler_params=_cp(("parallel", "parallel", "arbitrary")))(a, b)


def _mm_tn_split(a, b, *, name, tk=512):
    T, M = a.shape
    N = b.shape[1]
    n = N // N_DEV
    parts = N_DEV // 2
    tn = parts * n
    assert M % LANES == 0 and N == N_DEV * n
    tk = _tile(T, tk)
    nk = T // tk

    def body(a_ref, b_ref, o_ref, acc):
        k = pl.program_id(1)
        d = lax.dot_general(a_ref[...].astype(CDT), b_ref[...].astype(CDT), (((0,), (0,)), ((), ())),
                            preferred_element_type=F32)

        @pl.when(k == 0)
        def _():
            acc[...] = d

        @pl.when(k > 0)
        def _():
            acc[...] += d

        @pl.when(k == nk - 1)
        def _():
            full = acc[...]
            for s in range(parts):
                o_ref[s] = full[:, s * n:(s + 1) * n]

    return pl.pallas_call(
        body, grid=(N // tn, nk),
        in_specs=[pl.BlockSpec((tk, M), lambda j, k: (k, 0)), pl.BlockSpec((tk, tn), lambda j, k: (k, j))],
        out_specs=pl.BlockSpec((parts, M, n), lambda j, k: (j, 0, 0)),
        out_shape=_sds((N_DEV, M, n), F32), scratch_shapes=[pltpu.VMEM((M, tn), F32)],
        name=name, compiler_params=_cp(("parallel", "arbitrary")))(a, b)


def _ffn_up(h, wg, wu, name):
    T, D = h.shape
    Fh = wg.shape[1]
    tm = _tile(T, 2048)
    tn = _tile(Fh, 256)

    def body(h_ref, wg_ref, wu_ref, a_ref, b_ref, s_ref):
        hv = h_ref[...]
        a = jnp.dot(hv, wg_ref[...], preferred_element_type=F32)
        b = jnp.dot(hv, wu_ref[...], preferred_element_type=F32)
        a_ref[...] = a.astype(a_ref.dtype)
        b_ref[...] = b.astype(b_ref.dtype)
        s_ref[...] = (a * _sigmoid(a) * b).astype(s_ref.dtype)

    wspec = pl.BlockSpec((D, tn), lambda i, j: (0, j))
    ospec = pl.BlockSpec((tm, tn), lambda i, j: (i, j))
    return pl.pallas_call(
        body, grid=(T // tm, Fh // tn),
        in_specs=[pl.BlockSpec((tm, D), lambda i, j: (i, 0)), wspec, wspec],
        out_specs=[ospec, ospec, ospec],
        out_shape=[_sds((T, Fh), CDT), _sds((T, Fh), CDT), _sds((T, Fh), CDT)],
        name=name, compiler_params=_cp(("parallel", "arbitrary")))(h, wg, wu)


def _ffn_bwd_ds(gout, wd, a, b, name, dep=None):
    T, D = gout.shape
    Fh = wd.shape[0]
    tm = _tile(T, 1024)
    tn = _tile(Fh, 256)

    def body(g_ref, wd_ref, a_ref, b_ref, *rest):
        da_ref, db_ref = rest[-2:]
        dy = (0.5 * g_ref[...]).astype(CDT)
        ds = lax.dot_general(dy, wd_ref[...], (((1,), (1,)), ((), ())), preferred_element_type=F32)
        av = a_ref[...].astype(F32)
        sg = _sigmoid(av)
        da_ref[...] = (ds * b_ref[...].astype(F32) * (sg * (1.0 + av * (1.0 - sg)))).astype(da_ref.dtype)
        db_ref[...] = (ds * (av * sg)).astype(db_ref.dtype)

    ospec = pl.BlockSpec((tm, tn), lambda i, j: (i, j))
    dspec, darg = _dep(dep)
    return pl.pallas_call(
        body, grid=(T // tm, Fh // tn),
        in_specs=[pl.BlockSpec((tm, D), lambda i, j: (i, 0)), pl.BlockSpec((tn, D), lambda i, j: (j, 0)), ospec, ospec] + dspec,
        out_specs=[ospec, ospec],
        out_shape=[_sds((T, Fh), CDT), _sds((T, Fh), CDT)],
        name=name, compiler_params=_cp(("parallel", "arbitrary")))(gout, wd, a, b, *darg)


def _ffn_fwd(x, gamma, wg, wu, wd, tag, dep=None):
    h = _rms_fwd(x, gamma, f"{tag}_norm", dep)
    a, b, s = _ffn_up(h, wg, wu, f"{tag}_up")
    y = _mm([(s, wd, False)], res=x, alpha=0.5, tn=512, name=f"{tag}_down")
    return y, (x, h, a, b, s)


def _ffn_bwd(saved, gamma, wg, wu, wd, gout, tag, dep, on_grads):
    x, h, a, b, s = saved
    da, db = _ffn_bwd_ds(gout, wd, a, b, f"{tag}_bwd_ds", dep)
    dwd = _mm_tn(s, gout, alpha=0.5, name=f"{tag}_dwd")
    dwg = _mm_tn_split(h, da, name=f"{tag}_dwg")
    dwu = _mm_tn_split(h, db, name=f"{tag}_dwu")
    dep = on_grads(dict(w_gate=dwg, w_up=dwu, w_down=dwd))
    dh = _mm([(da, wg, True), (db, wu, True)], tn=512, name=f"{tag}_dh", dep=dep)
    gin, dgamma = _rms_bwd(x, gamma, dh, gout, f"{tag}_norm_bwd")
    return gin, dgamma


def _fgate_fwd(rest, bias, name, bt=512):
    T = rest.shape[0]
    bt = _tile(T, bt)

    def body(z_ref, b_ref, fc_ref, ft_ref, carry):
        i = pl.program_id(0)

        @pl.when(i == 0)
        def _():
            carry[...] = jnp.zeros_like(carry)

        zb = z_ref[...] + b_ref[...]
        e = jnp.exp(-jnp.abs(zb))
        u = 1.0 + e
        log1p_e = jnp.where(u == 1.0, e, jnp.log(u) * (e / (u - 1.0)))
        x = jnp.minimum(zb, 0.0) - log1p_e
        row = lax.broadcasted_iota(jnp.int32, x.shape, 0)
        sh = 1
        while sh < bt:
            x = x + jnp.where(row >= sh, pltpu.roll(x, sh, 0), 0.0)
            sh *= 2
        f = x + carry[...]
        carry[...] = f[bt - 1:bt, :]
        fc_ref[...] = f
        ft_ref[...] = jnp.transpose(f)[0:N_HEADS, :]

    return pl.pallas_call(
        body, grid=(T // bt,),
        in_specs=[pl.BlockSpec((bt, LANES), lambda i: (i, REST_Z_BLK)), pl.BlockSpec((1, LANES), lambda i: (0, 0))],
        out_specs=[pl.BlockSpec((bt, LANES), lambda i: (i, 0)), pl.BlockSpec((N_HEADS, bt), lambda i: (0, i))],
        out_shape=[_sds((T, LANES), F32), _sds((N_HEADS, T), F32)],
        scratch_shapes=[pltpu.VMEM((1, LANES), F32)],
        name=name, compiler_params=_cp(("arbitrary",)))(rest, bias)


def _fgate_bwd(dfk, rest, bias, name, bt=512):
    T = rest.shape[0]
    bt = _tile(T, bt)
    nb = T // bt

    def body(df_ref, z_ref, b_ref, dz_ref, db_ref, carry):
        i = pl.program_id(0)

        @pl.when(i == 0)
        def _():
            carry[...] = jnp.zeros_like(carry)

        dfv = df_ref[...]
        lane = lax.broadcasted_iota(jnp.int32, (bt, LANES), 1)
        x = jnp.zeros((bt, LANES), F32)
        for h in range(N_HEADS):
            x = jnp.where(lane == h, dfv[:, HEAD_DIM * h:HEAD_DIM * h + 1], x)
        row = lax.broadcasted_iota(jnp.int32, x.shape, 0)
        sh = 1
        while sh < bt:
            x = x + jnp.where(row + sh < bt, pltpu.roll(x, bt - sh, 0), 0.0)
            sh *= 2
        dlf = x + carry[...]
        carry[...] = dlf[0:1, :]
        zb = z_ref[...] + b_ref[...]
        dz = jnp.where(lane < N_HEADS, dlf * _sigmoid(-zb), 0.0)
        dz_ref[...] = dz.astype(dz_ref.dtype)
        part = jnp.sum(dz, axis=0, keepdims=True)

        @pl.when(i == 0)
        def _():
            db_ref[...] = part

        @pl.when(i > 0)
        def _():
            db_ref[...] += part

    return pl.pallas_call(
        body, grid=(nb,),
        in_specs=[pl.BlockSpec((bt, 4 * LANES), lambda i: (nb - 1 - i, 0)),
                  pl.BlockSpec((bt, LANES), lambda i: (nb - 1 - i, REST_Z_BLK)),
                  pl.BlockSpec((1, LANES), lambda i: (0, 0))],
        out_specs=[pl.BlockSpec((bt, LANES), lambda i: (nb - 1 - i, 0)), pl.BlockSpec((1, LANES), lambda i: (0, 0))],
        out_shape=[_sds((T, LANES), CDT), _sds((1, LANES), F32)],
        scratch_shapes=[pltpu.VMEM((1, LANES), F32)],
        name=name, compiler_params=_cp(("arbitrary",)))(dfk, rest, bias)


def _by_group(vals, lane):
    out = vals[-1]
    for g in range(len(vals) - 2, -1, -1):
        out = jnp.where(lane // 64 == g, vals[g], out)
    return out


def _pool_counts(t0, n, lane):
    t = t0 + lax.broadcasted_iota(jnp.int32, (n, 256), 0)
    return _by_group([jnp.minimum(t + 1, w) for w in POOL_WINDOWS], lane).astype(F32)


def _pooled(u, halo, i, bt):
    lane = lax.broadcasted_iota(jnp.int32, (bt, 256), 1)
    ext = jnp.concatenate([jnp.where(i > 0, halo, 0.0), u], axis=0)
    sums, s, sh = [], ext, 1
    for _ in POOL_WINDOWS:
        s = s + pltpu.roll(s, sh, 0)
        sums.append(s[POOL_HALO:, :])
        sh *= 2
    return _by_group(sums, lane) / _pool_counts(i * bt, bt, lane) - u


def _pool_fwd(rest, wbd, scale, name, bt=512):
    T = rest.shape[0]
    bt = _tile(T, bt)
    hb = bt // POOL_HALO

    def body(u_ref, halo_ref, w_ref, sc_ref, o_ref):
        i = pl.program_id(0)
        pooled = _pooled(u_ref[...], halo_ref[...], i, bt)
        mixed = jnp.dot(pooled.astype(CDT), w_ref[...], preferred_element_type=F32)
        o_ref[...] = (mixed * sc_ref[...]).astype(o_ref.dtype)

    return pl.pallas_call(
        body, grid=(T // bt,),
        in_specs=[pl.BlockSpec((bt, 256), lambda i: (i, 0)),
                  pl.BlockSpec((POOL_HALO, 256), lambda i: (jnp.maximum(i * hb - 1, 0), 0)),
                  pl.BlockSpec((256, 256), lambda i: (0, 0)), pl.BlockSpec((1, 256), lambda i: (0, 0))],
        out_specs=pl.BlockSpec((bt, 256), lambda i: (i, 0)),
        out_shape=_sds((T, 256), CDT), name=name, compiler_params=_cp(("parallel",)))(rest, rest, wbd, scale)


def _pool_bwd(dcat, rest, wbd, scale, name, bt=512):
    T = rest.shape[0]
    bt = _tile(T, bt)
    hb = bt // POOL_HALO
    nb = T // bt
    n = bt + POOL_HALO

    def body(dy_ref, dyn_ref, u_ref, halo_ref, w_ref, sc_ref, du_ref, dw_ref, dsc_ref):
        i = pl.program_id(0)
        lane = lax.broadcasted_iota(jnp.int32, (bt, 256), 1)
        w = w_ref[...]
        sc = sc_ref[...]
        pooled = _pooled(u_ref[...], halo_ref[...], i, bt)
        pooled_c = pooled.astype(CDT)
        mixed = jnp.dot(pooled_c, w, preferred_element_type=F32)
        dy = dy_ref[...]
        dm = (dy * sc).astype(CDT)
        dsc = jnp.sum(dy * mixed, axis=0, keepdims=True)
        dw = lax.dot_general(pooled_c, dm, (((0,), (0,)), ((), ())), preferred_element_type=F32)
        nt = (((1,), (1,)), ((), ()))
        dpl = lax.dot_general(dm, w, nt, preferred_element_type=F32)
        dmn = (jnp.where(i < nb - 1, dyn_ref[...], 0.0) * sc).astype(CDT)
        dpln = lax.dot_general(dmn, w, nt, preferred_element_type=F32)
        lane_h = lax.broadcasted_iota(jnp.int32, (POOL_HALO, 256), 1)
        ext = jnp.concatenate([dpl / _pool_counts(i * bt, bt, lane),
                               dpln / _pool_counts((i + 1) * bt, POOL_HALO, lane_h)], axis=0)
        sums, s, sh = [], ext, 1
        for _ in POOL_WINDOWS:
            s = s + pltpu.roll(s, n - sh, 0)
            sums.append(s[0:bt, :])
            sh *= 2
        du_ref[...] = (_by_group(sums, lane) - dpl).astype(du_ref.dtype)

        @pl.when(i == 0)
        def _():
            dw_ref[...] = dw
            dsc_ref[...] = dsc

        @pl.when(i > 0)
        def _():
            dw_ref[...] += dw
            dsc_ref[...] += dsc

    full = pl.BlockSpec((256, 256), lambda i: (0, 0))
    vec = pl.BlockSpec((1, 256), lambda i: (0, 0))
    return pl.pallas_call(
        body, grid=(nb,),
        in_specs=[pl.BlockSpec((bt, 256), lambda i: (i, 0)),
                  pl.BlockSpec((POOL_HALO, 256), lambda i: (jnp.minimum((i + 1) * hb, nb * hb - 1), 0)),
                  pl.BlockSpec((bt, 256), lambda i: (i, 0)),
                  pl.BlockSpec((POOL_HALO, 256), lambda i: (jnp.maximum(i * hb - 1, 0), 0)),
                  full, vec],
        out_specs=[pl.BlockSpec((bt, 256), lambda i: (i, 0)), full, vec],
        out_shape=[_sds((T, 256), CDT), _sds((256, 256), F32), _sds((1, 256), F32)],
        name=name, compiler_params=_cp(("arbitrary",)))(dcat, dcat, rest, rest, wbd, scale)


def _glu_ext(a_ref, g_ref, ah_ref, gh_ref, i):
    u = a_ref[...] * _sigmoid(g_ref[...])
    uh = jnp.where(i > 0, ah_ref[...] * _sigmoid(gh_ref[...]), 0.0)
    return jnp.concatenate([uh, u], axis=0)


def _conv_fwd(rest, cw, cb, lg, lb, name, bt=512):
    T = rest.shape[0]
    bt = _tile(T, bt)
    hb = bt // CONV_HALO

    def body(a_ref, g_ref, ah_ref, gh_ref, cw_ref, cb_ref, lg_ref, lb_ref, o_ref, y_ref):
        i = pl.program_id(0)
        ext = _glu_ext(a_ref, g_ref, ah_ref, gh_ref, i)
        w = cw_ref[...]
        acc = w[CONV_K - 1:CONV_K, :] * ext
        for k in range(CONV_K - 1):
            acc = acc + w[k:k + 1, :] * pltpu.roll(ext, CONV_K - 1 - k, 0)
        y = acc[CONV_HALO:, :] + cb_ref[...]
        y_ref[...] = y
        yc = y - jnp.mean(y, axis=-1, keepdims=True)
        yn = yc * lax.rsqrt(jnp.mean(yc * yc, axis=-1, keepdims=True) + NORM_EPS)
        z = yn * lg_ref[...] + lb_ref[...]
        o_ref[...] = (z * _sigmoid(z)).astype(o_ref.dtype)

    def cur(c):
        return pl.BlockSpec((bt, 256), lambda i: (i, c))

    def prev(c):
        return pl.BlockSpec((CONV_HALO, 256), lambda i: (jnp.maximum(i * hb - 1, 0), c))

    vec = pl.BlockSpec((1, 256), lambda i: (0, 0))
    return pl.pallas_call(
        body, grid=(T // bt,),
        in_specs=[cur(1), cur(2), prev(1), prev(2), pl.BlockSpec((CONV_HALO, 256), lambda i: (0, 0)), vec, vec, vec],
        out_specs=[pl.BlockSpec((bt, 256), lambda i: (i, 0)), pl.BlockSpec((bt, 256), lambda i: (i, 0))],
        out_shape=[_sds((T, 256), CDT), _sds((T, 256), F32)],
        name=name, compiler_params=_cp(("parallel",)))(rest, rest, rest, rest, cw, cb, lg, lb)


def _conv_bwd(dcat, yconv, rest, cw, lg, lb, name, bt=512):
    T = rest.shape[0]
    bt = _tile(T, bt)
    hb = bt // CONV_HALO
    nb = T // bt
    n = bt + CONV_HALO

    def body(dy_ref, dyn_ref, y_ref, yn_ref, a_ref, g_ref, ah_ref, gh_ref, cw_ref, lg_ref, lb_ref,
             da_ref, dg_ref, dcw_ref, dcb_ref, dlg_ref, dlb_ref):
        i = pl.program_id(0)
        lgv = lg_ref[...]
        lbv = lb_ref[...]

        def ln_swish_bwd(dout, y):
            yc = y - jnp.mean(y, axis=-1, keepdims=True)
            rs = lax.rsqrt(jnp.mean(yc * yc, axis=-1, keepdims=True) + NORM_EPS)
            yn = yc * rs
            z = yn * lgv + lbv
            sg = _sigmoid(z)
            dz = dout * (sg * (1.0 + z * (1.0 - sg)))
            dyn = dz * lgv
            dyc = rs * (dyn - jnp.mean(dyn, axis=-1, keepdims=True) - yn * jnp.mean(dyn * yn, axis=-1, keepdims=True))
            return dyc, dz, yn

        dyc, dz, yn = ln_swish_bwd(dy_ref[...], y_ref[...])
        dyc_next, _, _ = ln_swish_bwd(dyn_ref[...], yn_ref[...])
        dyc_next = jnp.where(i < nb - 1, dyc_next, 0.0)
        ext_u = _glu_ext(a_ref, g_ref, ah_ref, gh_ref, i)
        ext_d = jnp.concatenate([dyc, dyc_next], axis=0)
        w = cw_ref[...]
        du = w[CONV_K - 1:CONV_K, :] * ext_d
        rows = []
        for k in range(CONV_K):
            s = CONV_K - 1 - k
            if s > 0:
                du = du + w[k:k + 1, :] * pltpu.roll(ext_d, n - s, 0)
                us = pltpu.roll(ext_u, s, 0)[CONV_HALO:, :]
            else:
                us = ext_u[CONV_HALO:, :]
            rows.append(jnp.sum(dyc * us, axis=0, keepdims=True))
        rows.append(jnp.zeros((1, 256), F32))
        dcw = jnp.concatenate(rows, axis=0)
        du = du[0:bt, :]
        av = a_ref[...]
        sg = _sigmoid(g_ref[...])
        da_ref[...] = (du * sg).astype(da_ref.dtype)
        dg_ref[...] = (du * av * (sg * (1.0 - sg))).astype(dg_ref.dtype)
        dcb = jnp.sum(dyc, axis=0, keepdims=True)
        dlg = jnp.sum(dz * yn, axis=0, keepdims=True)
        dlb = jnp.sum(dz, axis=0, keepdims=True)

        @pl.when(i == 0)
        def _():
            dcw_ref[...] = dcw
            dcb_ref[...] = dcb
            dlg_ref[...] = dlg
            dlb_ref[...] = dlb

        @pl.when(i > 0)
        def _():
            dcw_ref[...] += dcw
            dcb_ref[...] += dcb
            dlg_ref[...] += dlg
            dlb_ref[...] += dlb

    def cur(c):
        return pl.BlockSpec((bt, 256), lambda i: (i, c))

    def prev(c):
        return pl.BlockSpec((CONV_HALO, 256), lambda i: (jnp.maximum(i * hb - 1, 0), c))

    def nxt(c):
        return pl.BlockSpec((CONV_HALO, 256), lambda i: (jnp.minimum((i + 1) * hb, nb * hb - 1), c))

    vec = pl.BlockSpec((1, 256), lambda i: (0, 0))
    wfull = pl.BlockSpec((CONV_HALO, 256), lambda i: (0, 0))
    return pl.pallas_call(
        body, grid=(nb,),
        in_specs=[cur(3), nxt(3), cur(0), nxt(0), cur(1), cur(2), prev(1), prev(2), wfull, vec, vec],
        out_specs=[cur(0), cur(0), wfull, vec, vec, vec],
        out_shape=[_sds((T, 256), CDT), _sds((T, 256), CDT), _sds((CONV_HALO, 256), F32),
                   _sds((1, 256), F32), _sds((1, 256), F32), _sds((1, 256), F32)],
        name=name, compiler_params=_cp(("arbitrary",)))(dcat, dcat, yconv, yconv, rest, rest, rest, rest, cw, lg, lb)


def _half_mask(shape, a):
    lane = lax.broadcasted_iota(jnp.int32, shape, 1)
    return (lane // HEAD_DIM) == a


def _attn_fwd(qkv, fcol, frow, name, blk=512):
    T = qkv.shape[0]
    blk = _tile(T, blk)
    nq = T // blk
    nt = (((1,), (1,)), ((), ()))

    def body(q_ref, k_ref, v_ref, fc_ref, fr_ref, o_ref, olo_ref, lse_ref):
        p_id = pl.program_id(0)
        i = pl.program_id(1)
        q2 = q_ref[...]
        fc = fc_ref[...]
        lane = lax.broadcasted_iota(jnp.int32, (blk, LANES), 1)
        tri = lax.broadcasted_iota(jnp.int32, (blk, blk), 1) <= lax.broadcasted_iota(jnp.int32, (blk, blk), 0)
        outs, los, lses = [], [], []
        for a in range(2):
            qa = jnp.where(_half_mask(q2.shape, a), q2, jnp.zeros_like(q2)) * ATT_SCALE
            fq = jnp.sum(jnp.where(lane == 2 * p_id + a, fc, 0.0), axis=1, keepdims=True)

            def tile(j, carry, masked):
                m, l, acc, acc_lo = carry
                kj = k_ref[pl.ds(pl.multiple_of(j * blk, blk), blk), :]
                vj = v_ref[pl.ds(pl.multiple_of(j * blk, blk), blk), :]
                fk = fr_ref[a:a + 1, pl.ds(pl.multiple_of(j * blk, blk), blk)]
                s = lax.dot_general(qa, kj, nt, preferred_element_type=F32) + (fq - fk)
                if masked:
                    s = jnp.where(tri, s, NEG)
                m_new = jnp.maximum(m, jnp.max(s, axis=1, keepdims=True))
                alpha = jnp.exp(m - m_new)
                pr = jnp.exp(s - m_new)
                l = alpha * l + jnp.sum(pr, axis=1, keepdims=True)
                pr_hi = pr.astype(CDT)
                pr_lo = (pr - pr_hi.astype(F32)).astype(CDT)
                acc = alpha * acc + jnp.dot(pr_hi, vj, preferred_element_type=F32)
                acc_lo = alpha * acc_lo + jnp.dot(pr_lo, vj, preferred_element_type=F32)
                return m_new, l, acc, acc_lo

            zero = jnp.zeros((blk, LANES), F32)
            init = (jnp.full((blk, 1), NEG, F32), jnp.zeros((blk, 1), F32), zero, zero)
            carry = lax.fori_loop(0, i, lambda j, c: tile(j, c, False), init)
            m, l, acc, acc_lo = tile(i, carry, True)
            outs.append(acc / l)
            los.append(acc_lo / l)
            lses.append(m + jnp.log(l))
        lo = lane < HEAD_DIM
        o_ref[...] = jnp.where(lo, outs[0], outs[1])
        olo_ref[...] = jnp.where(lo, los[0], los[1])
        lse_t = jnp.transpose(jnp.where(lo, lses[0], lses[1]))
        lse_ref[...] = jnp.concatenate([lse_t[0:1, :], lse_t[HEAD_DIM:HEAD_DIM + 1, :]], axis=0)

    return pl.pallas_call(
        body, grid=(N_PAIRS, nq),
        in_specs=[pl.BlockSpec((blk, LANES), lambda p, i: (i, p)),
                  pl.BlockSpec((T, LANES), lambda p, i: (0, N_PAIRS + p)),
                  pl.BlockSpec((T, LANES), lambda p, i: (0, 2 * N_PAIRS + p)),
                  pl.BlockSpec((blk, LANES), lambda p, i: (i, 0)),
                  pl.BlockSpec((None, 2, T), lambda p, i: (p, 0, 0))],
        out_specs=[pl.BlockSpec((blk, LANES), lambda p, i: (i, p)), pl.BlockSpec((blk, LANES), lambda p, i: (i, p)),
                   pl.BlockSpec((None, 2, blk), lambda p, i: (p, 0, i))],
        out_shape=[_sds((T, N_PAIRS * LANES), F32), _sds((T, N_PAIRS * LANES), F32), _sds((N_PAIRS, 2, T), F32)],
        name=name, compiler_params=_cp(("parallel", "arbitrary")))(qkv, qkv, qkv, fcol, frow)


def _attn_delta(dcat, o, o_lo, name, blk=512):
    T = o.shape[0]
    blk = _tile(T, blk)

    def body(d_ref, o_ref, olo_ref, out_ref):
        prod = d_ref[:, 256:768].astype(CDT).astype(F32) * (o_ref[...] + olo_ref[...])
        pt = jnp.transpose(prod)
        out_ref[...] = jnp.sum(pt.reshape(N_HEADS, HEAD_DIM, blk), axis=1)

    return pl.pallas_call(
        body, grid=(T // blk,),
        in_specs=[pl.BlockSpec((blk, 1024), lambda i: (i, 0)), pl.BlockSpec((blk, 512), lambda i: (i, 0)),
                  pl.BlockSpec((blk, 512), lambda i: (i, 0))],
        out_specs=pl.BlockSpec((N_HEADS, blk), lambda i: (0, i)),
        out_shape=_sds((N_HEADS, T), F32), name=name, compiler_params=_cp(("parallel",)))(dcat, o, o_lo)


def _attn_bwd(qkv, dcat, fcol, frow, lse, delta, name, blk=512):
    T = qkv.shape[0]
    blk = _tile(T, blk)
    nq = T // blk
    nt = (((1,), (1,)), ((), ()))
    tn = (((0,), (0,)), ((), ()))

    def body(q_ref, do_ref, k_ref, v_ref, fc_ref, fr_ref, lse_ref, dl_ref, dq_ref, dk_ref, dv_ref, df_ref):
        p_id = pl.program_id(0)
        j = pl.program_id(1)

        @pl.when(j == 0)
        def _():
            dq_ref[...] = jnp.zeros_like(dq_ref)

        k2 = k_ref[...]
        v2 = v_ref[...]
        fc = fc_ref[...]
        lane = lax.broadcasted_iota(jnp.int32, (blk, LANES), 1)
        tri = lax.broadcasted_iota(jnp.int32, (blk, blk), 0) <= lax.broadcasted_iota(jnp.int32, (blk, blk), 1)
        dks, dvs, dfs = [], [], []
        for a in range(2):
            hm = _half_mask(k2.shape, a)
            ka = jnp.where(hm, k2, jnp.zeros_like(k2)) * ATT_SCALE
            va = jnp.where(hm, v2, jnp.zeros_like(v2))
            fk = jnp.sum(jnp.where(lane == 2 * p_id + a, fc, 0.0), axis=1, keepdims=True)

            def tile(i, carry, masked):
                dk_acc, dv_acc, df_acc = carry
                rows = pl.ds(pl.multiple_of(i * blk, blk), blk)
                qi = q_ref[rows, :]
                doi = do_ref[rows, :].astype(CDT)
                st = lax.dot_general(ka, qi, nt, preferred_element_type=F32)
                e = st + (fr_ref[a:a + 1, rows] - fk) - lse_ref[a:a + 1, rows]
                if masked:
                    e = jnp.where(tri, e, NEG)
                pt = jnp.exp(e)
                dpt = lax.dot_general(va, doi, nt, preferred_element_type=F32)
                ds32 = pt * (dpt - dl_ref[a:a + 1, rows])
                dst = ds32.astype(CDT)
                df_acc = df_acc + jnp.sum(ds32, axis=1, keepdims=True)
                dv_acc = dv_acc + jnp.dot(pt.astype(CDT), doi, preferred_element_type=F32)
                dk_acc = dk_acc + jnp.dot(dst, qi, preferred_element_type=F32)
                dq_ref[rows, :] += lax.dot_general(dst, ka, tn, preferred_element_type=F32)
                return dk_acc, dv_acc, df_acc

            init = (jnp.zeros((blk, LANES), F32), jnp.zeros((blk, LANES), F32), jnp.zeros((blk, 1), F32))
            carry = tile(j, init, True)
            dk_acc, dv_acc, df_acc = lax.fori_loop(j + 1, nq, lambda i, c: tile(i, c, False), carry)
            dks.append(dk_acc)
            dvs.append(dv_acc)
            dfs.append(df_acc)
        lo = lane < HEAD_DIM
        dk_ref[...] = jnp.where(lo, dks[0], dks[1]) * ATT_SCALE
        dv_ref[...] = jnp.where(lo, dvs[0], dvs[1])
        df_ref[...] = -jnp.where(lo, dfs[0], dfs[1])

    res = pl.BlockSpec((T, LANES), lambda p, j: (0, p))
    rows = pl.BlockSpec((None, 2, T), lambda p, j: (p, 0, 0))
    kv_out = pl.BlockSpec((blk, LANES), lambda p, j: (j, p))
    return pl.pallas_call(
        body, grid=(N_PAIRS, nq),
        in_specs=[res, pl.BlockSpec((T, LANES), lambda p, j: (0, 2 + p)),
                  pl.BlockSpec((blk, LANES), lambda p, j: (j, N_PAIRS + p)),
                  pl.BlockSpec((blk, LANES), lambda p, j: (j, 2 * N_PAIRS + p)),
                  pl.BlockSpec((blk, LANES), lambda p, j: (j, 0)), rows, rows, rows],
        out_specs=[res, kv_out, kv_out, kv_out],
        out_shape=[_sds((T, N_PAIRS * LANES), F32)] * 4,
        name=name, compiler_params=_cp(("parallel", "arbitrary")))(qkv, dcat, qkv, qkv, fcol, frow, lse, delta)


def _mixer_fwd(x, wts, tag, dep=None):
    T = x.shape[0]
    h = _rms_fwd(x, wts["mix_norm"], f"{tag}_norm", dep)
    qkv = _mm([(h, wts["win_qkv"], False)], out_dtype=CDT, tm=1024, tn=768, name=f"{tag}_in_qkv")
    rest = _mm([(h, wts["win_rest"], False)], tm=1024, name=f"{tag}_in_rest")
    fcol, frow8 = _fgate_fwd(rest, wts["fbias"], f"{tag}_fgate")
    frow = frow8.reshape(N_PAIRS, 2, T)
    ya = _pool_fwd(rest, wts["pool_wbd"], wts["pool_scale"], f"{tag}_pool")
    o, o_lo, lse = _attn_fwd(qkv, fcol, frow, f"{tag}_attn")
    yc, yconv = _conv_fwd(rest, wts["conv_w"], wts["conv_b"], wts["conv_ln_g"], wts["conv_ln_b"], f"{tag}_conv")
    cat = jnp.concatenate([ya, o.astype(CDT), yc], axis=1)
    y = _mm([(cat, wts["w_out"], False)], res=x, tn=512, name=f"{tag}_out")
    return y, (x, h, qkv, rest, fcol, frow, o, o_lo, lse, yconv, cat)


def _mixer_bwd(saved, wts, gout, tag, dep=None):
    x, h, qkv, rest, fcol, frow, o, o_lo, lse, yconv, cat = saved
    T = x.shape[0]
    dcat = _mm([(gout, wts["w_out"], True)], tn=512, name=f"{tag}_dcat", dep=dep)
    dwout = _mm_tn(cat, gout, name=f"{tag}_dwout")
    du, dpw, dpsc = _pool_bwd(dcat, rest, wts["pool_wbd"], wts["pool_scale"], f"{tag}_pool_bwd")
    delta = _attn_delta(dcat, o, o_lo, f"{tag}_attn_delta").reshape(N_PAIRS, 2, T)
    dq, dk, dv, dfk = _attn_bwd(qkv, dcat, fcol, frow, lse, delta, f"{tag}_attn_bwd")
    dz, dfb = _fgate_bwd(dfk, rest, wts["fbias"], f"{tag}_fgate_bwd")
    da, dg, dcw, dcb, dlg, dlb = _conv_bwd(dcat, yconv, rest, wts["conv_w"], wts["conv_ln_g"], wts["conv_ln_b"],
                                           f"{tag}_conv_bwd")
    dp_qkv = jnp.concatenate([dq, dk, dv], axis=1).astype(CDT)
    dp_rest = jnp.concatenate([du, da, dg, dz], axis=1)
    dwin_qkv = _mm_tn(h, dp_qkv, name=f"{tag}_dwin_qkv")
    dwin_rest = _mm_tn(h, dp_rest, name=f"{tag}_dwin_rest")
    dh = _mm([(dp_qkv, wts["win_qkv"], True), (dp_rest, wts["win_rest"], True)], tn=512, name=f"{tag}_dh")
    gin, dgamma = _rms_bwd(x, wts["mix_norm"], dh, gout, f"{tag}_norm_bwd")
    dwin = _split_win(dwin_qkv, dwin_rest, f"{tag}_dwin_split")
    dpool_w = jnp.stack([dpw[64 * g:64 * g + 64, 64 * g:64 * g + 64] for g in range(4)])
    grads = dict(mix_norm=dgamma[0], w_in=dwin, pool_w=dpool_w, pool_scale=dpsc[0], forget_bias=dfb[0, 0:N_HEADS],
                 conv_w=dcw[0:CONV_K], conv_b=dcb[0], conv_ln_g=dlg[0], conv_ln_b=dlb[0], w_out=dwout)
    return gin, grads


def _rep_layer(rep, l):
    pw = rep["pool_w"][l].astype(CDT)
    wbd = jnp.zeros((256, 256), CDT)
    for g in range(4):
        wbd = lax.dynamic_update_slice(wbd, pw[g], (64 * g, 64 * g))
    return dict(
        ffn1_norm=rep["ffn1_norm"][l][None], ffn2_norm=rep["ffn2_norm"][l][None], mix_norm=rep["mix_norm"][l][None],
        fbias=jnp.pad(rep["forget_bias"][l], (0, LANES - N_HEADS))[None],
        pool_wbd=wbd, pool_scale=rep["pool_scale"][l][None], conv_b=rep["conv_b"][l][None],
        conv_ln_g=rep["conv_ln_g"][l][None], conv_ln_b=rep["conv_ln_b"][l][None])


def _local_step(x, target, rep, weights_for, grads_ready):
    depth = rep["ffn1_norm"].shape[0]
    kept = []
    for l in range(depth):
        r = _rep_layer(rep, l)
        w1, dep = weights_for(l, "ffn1", x)
        x, s1 = _ffn_fwd(x, r["ffn1_norm"], w1["w_gate"], w1["w_up"], w1["w_down"], f"l{l}_ffn1", dep)
        wm, dep = weights_for(l, "mix", x)
        wm = dict(r, win_qkv=wm["win_qkv"], win_rest=wm["win_rest"], w_out=wm["w_out"],
                  conv_w=jnp.pad(wm["conv_w"], ((0, CONV_HALO - CONV_K), (0, 0))))
        x, s2 = _mixer_fwd(x, wm, f"l{l}_mix", dep)
        w2, dep = weights_for(l, "ffn2", x)
        x, s3 = _ffn_fwd(x, r["ffn2_norm"], w2["w_gate"], w2["w_up"], w2["w_down"], f"l{l}_ffn2", dep)
        kept.append((r, w1, wm, w2, s1, s2, s3))
    loss, g, dfinal = _loss_bwd(x, rep["final_norm"][None], target, "loss_head")
    dep = grads_ready(None, "final", dict(final_norm=dfinal[0]))
    for l in reversed(range(depth)):
        r, w1, wm, w2, s1, s2, s3 = kept[l]

        def ffn_grads(which, l=l):
            return lambda gr: grads_ready(l, which, {f"{which}_{k}": v for k, v in gr.items()})

        g, dn = _ffn_bwd(s3, r["ffn2_norm"], w2["w_gate"], w2["w_up"], w2["w_down"], g, f"l{l}_ffn2", dep, ffn_grads("ffn2"))
        grads_ready(l, "norm", dict(ffn2_norm=dn[0]))
        g, gm = _mixer_bwd(s2, wm, g, f"l{l}_mix")
        dep = grads_ready(l, "mix", gm)
        g, dn = _ffn_bwd(s1, r["ffn1_norm"], w1["w_gate"], w1["w_up"], w1["w_down"], g, f"l{l}_ffn1", dep, ffn_grads("ffn1"))
        dep = grads_ready(l, "norm", dict(ffn1_norm=dn[0]))
    return loss, g


def _mesh_pos():
    return lax.axis_index("x"), lax.axis_index("y"), lax.axis_index("c")


def _dev_block(ref, dev, by_rows):
    if by_rows:
        r = ref.shape[1] // N_DEV
        return ref.at[:, pl.ds(dev * r, r), :]
    return ref.at[dev]


def _all_gather(shards, by_rows, name):
    n_arr = len(shards)
    out_shape = [_sds((s.shape[0], N_DEV * s.shape[1], s.shape[2]) if br else (N_DEV,) + s.shape, s.dtype)
                 for s, br in zip(shards, by_rows)]

    def body(*refs):
        xs, outs = refs[:n_arr], refs[n_arr:2 * n_arr]
        send_sems, recv_sems, local_sems = refs[2 * n_arr:]
        x, y, c = _mesh_pos()
        me, sibling = (x, y, c), (x, y, 1 - c)
        chips = [(1 - x, y), (x, 1 - y), (1 - x, 1 - y)]

        def rows(a, px, py, pc):
            return _dev_block(outs[a], 4 * px + 2 * py + pc, by_rows[a])

        def copy(k, a, block, to, src=None):
            return pltpu.make_async_remote_copy(
                src_ref=rows(a, *block) if src is None else src, dst_ref=rows(a, *block),
                send_sem=send_sems.at[k, a], recv_sem=recv_sems.at[k, a],
                device_id=to, device_id_type=pl.DeviceIdType.MESH)

        arrs = range(n_arr)
        mine = [pltpu.make_async_copy(xs[a], rows(a, *me), local_sems.at[a]) for a in arrs]
        for cp in mine:
            cp.start()
        first = [copy(0, a, me, sibling, src=xs[a]) for a in arrs]
        first += [copy(1 + j, a, me, (*chip, c), src=xs[a]) for j, chip in enumerate(chips) for a in arrs]
        for cp in first:
            cp.start()
        passed = []
        for j, chip in enumerate(chips):
            for a in arrs:
                copy(1 + j, a, (*chip, c), me).wait_recv()
                passed.append(copy(4 + j, a, (*chip, c), sibling))
                passed[-1].start()
        for a in arrs:
            copy(0, a, sibling, me).wait_recv()
        for j, chip in enumerate(chips):
            for a in arrs:
                copy(4 + j, a, (*chip, 1 - c), me).wait_recv()
        for cp in first + passed:
            cp.wait_send()
        for cp in mine:
            cp.wait()

    hbm = pl.BlockSpec(memory_space=pl.ANY)
    return pl.pallas_call(
        body, out_shape=out_shape, in_specs=[hbm] * n_arr, out_specs=[hbm] * n_arr,
        scratch_shapes=[pltpu.SemaphoreType.DMA((7, n_arr)), pltpu.SemaphoreType.DMA((7, n_arr)),
                        pltpu.SemaphoreType.DMA((n_arr,))],
        name=name)(*shards)


def _exchange(parts, by_rows, name):
    n_arr = len(parts)
    out_shape = [_sds((N_DEV, p.shape[0], p.shape[1] // N_DEV, p.shape[2]) if br else p.shape, p.dtype)
                 for p, br in zip(parts, by_rows)]

    def body(*refs):
        ps, outs = refs[:n_arr], refs[n_arr:2 * n_arr]
        send_sems, recv_sems, local_sems = refs[2 * n_arr:]
        x, y, c = _mesh_pos()
        my = 4 * x + 2 * y + c
        arrs = range(n_arr)
        mine = [pltpu.make_async_copy(_dev_block(ps[a], my, by_rows[a]), outs[a].at[my], local_sems.at[a]) for a in arrs]
        for cp in mine:
            cp.start()
        copies = []
        for k in range(1, N_DEV):
            px, py, pc = x ^ (k >> 2), y ^ ((k >> 1) & 1), c ^ (k & 1)
            for a in arrs:
                copies.append(pltpu.make_async_remote_copy(
                    src_ref=_dev_block(ps[a], 4 * px + 2 * py + pc, by_rows[a]), dst_ref=outs[a].at[my],
                    send_sem=send_sems.at[k - 1, a], recv_sem=recv_sems.at[k - 1, a],
                    device_id=(px, py, pc), device_id_type=pl.DeviceIdType.MESH))
        for cp in copies:
            cp.start()
        for cp in copies:
            cp.wait()
        for cp in mine:
            cp.wait()

    hbm = pl.BlockSpec(memory_space=pl.ANY)
    return pl.pallas_call(
        body, out_shape=out_shape, in_specs=[hbm] * n_arr, out_specs=[hbm] * n_arr,
        scratch_shapes=[pltpu.SemaphoreType.DMA((7, n_arr)), pltpu.SemaphoreType.DMA((7, n_arr)),
                        pltpu.SemaphoreType.DMA((n_arr,))],
        name=name)(*parts)


def _peer_copies(srcs, lands, send_sems, recv_sems, gather, by_rows):
    n_arr = len(srcs)
    x, y, c = _mesh_pos()
    my = 4 * x + 2 * y + c
    out = []
    for k in range(1, N_DEV):
        px, py, pc = x ^ (k >> 2), y ^ ((k >> 1) & 1), c ^ (k & 1)
        peer = 4 * px + 2 * py + pc
        for a in range(n_arr):
            src = srcs[a] if gather else _dev_block(srcs[a], peer, by_rows[a])
            dst = _dev_block(lands[a], my, by_rows[a]) if gather else lands[a].at[my]
            out.append(pltpu.make_async_remote_copy(
                src_ref=src, dst_ref=dst, send_sem=send_sems.at[(k - 1) * n_arr + a],
                recv_sem=recv_sems.at[(k - 1) * n_arr + a], device_id=(px, py, pc), device_id_type=pl.DeviceIdType.MESH))
    return out


def _land_shape(s, gather, by_rows):
    if gather:
        return (s.shape[0], N_DEV * s.shape[1], s.shape[2]) if by_rows else (N_DEV,) + s.shape
    return (N_DEV, s.shape[0], s.shape[1] // N_DEV, s.shape[2]) if by_rows else s.shape


_HBM = pl.BlockSpec(memory_space=pltpu.HBM)
_SEM = pl.BlockSpec(memory_space=pltpu.SEMAPHORE)


def _xfer_start(srcs, gather, by_rows, name, dep=None):
    n = len(srcs)
    lands = [lax.empty(_land_shape(s, gather, br), s.dtype) for s, br in zip(srcs, by_rows)]
    ins = [pltpu.with_memory_space_constraint(a, pltpu.HBM) for a in list(srcs) + lands]
    dspec, darg = _dep(dep)

    def body(*refs):
        s = 2 * n + len(darg)
        for cp in _peer_copies(refs[:n], refs[n:2 * n], refs[s], refs[s + 1], gather, by_rows):
            cp.start()
        refs[-1][...] = jnp.zeros_like(refs[-1])

    sems = pltpu.SemaphoreType.DMA(((N_DEV - 1) * n,))
    outs = pl.pallas_call(
        body, name=name,
        out_shape=(sems, sems, *[pltpu.HBM(a.shape, a.dtype) for a in ins], _sds((8, LANES), F32)),
        in_specs=[_HBM] * (2 * n) + dspec,
        out_specs=(_SEM, _SEM, *[_HBM] * (2 * n), pl.BlockSpec(memory_space=pltpu.VMEM)),
        input_output_aliases={i: 2 + i for i in range(2 * n)},
        compiler_params=pltpu.CompilerParams(has_side_effects=pltpu.SideEffectType.DATAFLOW_SIDE_EFFECTING))(*ins, *darg)
    return outs[0], outs[1], list(outs[2:-1]), outs[-1]


def _xfer_wait(started, after, gather, by_rows, name):
    send_sems, recv_sems, bufs, _ = started
    n = len(bufs) // 2

    def body(*refs):
        for cp in _peer_copies(refs[:n], refs[n:2 * n], refs[2 * n], refs[2 * n + 1], gather, by_rows):
            cp.wait_send()
            cp.wait_recv()

    outs = pl.pallas_call(
        body, name=name, out_shape=tuple(pltpu.HBM(a.shape, a.dtype) for a in bufs),
        in_specs=[_HBM] * (2 * n) + [_SEM, _SEM, pl.BlockSpec(memory_space=pl.ANY)], out_specs=tuple([_HBM] * (2 * n)),
        input_output_aliases={i: i for i in range(2 * n)},
        compiler_params=pltpu.CompilerParams(has_side_effects=pltpu.SideEffectType.DATAFLOW_SIDE_EFFECTING))(
            *bufs, send_sems, recv_sems, after)
    x, y, c = _mesh_pos()
    my = 4 * x + 2 * y + c
    res = []
    for src, land, br in zip(outs[:n], outs[n:], by_rows):
        zeros = (0,) * (land.ndim - 1)
        if gather and br:
            res.append(lax.dynamic_update_slice(land, src, (0, my * src.shape[1], 0)))
        elif gather:
            res.append(lax.dynamic_update_slice(land, src[None], (my,) + zeros))
        elif br:
            r = src.shape[1] // N_DEV
            own = lax.dynamic_slice(src, (0, my * r, 0), (src.shape[0], r, src.shape[2]))
            res.append(lax.dynamic_update_slice(land, own[None], (my,) + zeros))
        else:
            res.append(lax.dynamic_update_slice(land, lax.dynamic_index_in_dim(src, my, 0, keepdims=True), (my,) + zeros))
    return res


def _adam_update(g, w, m, v):
    c1 = 1.0 - ADAM_B1 ** ADAM_STEP
    c2 = 1.0 - ADAM_B2 ** ADAM_STEP
    nm = ADAM_B1 * m + (1.0 - ADAM_B1) * g
    nv = ADAM_B2 * v + (1.0 - ADAM_B2) * (g * g)
    return -ADAM_LR * ((nm / c1) / (jnp.sqrt(nv / c2) + ADAM_EPS) + ADAM_WD * w), nm, nv


def _adamw_body(p_ref, w_ref, m_ref, v_ref, g_ref, d_ref, nm_ref, nv_ref):
    g = p_ref[0]
    for i in range(1, N_DEV):
        g = g + p_ref[i]
    g_ref[...] = g
    d_ref[...], nm_ref[...], nv_ref[...] = _adam_update(g, w_ref[...], m_ref[...], v_ref[...])


def _adamw(parts, w, m, v, name, tr=1536):
    R = w.shape[0]
    tr = max(t for t in range(8, tr + 1, 8) if R % t == 0)

    def body(*refs):
        _adamw_body(*refs)

    row = pl.BlockSpec((tr, LANES), lambda i: (i, 0))
    return pl.pallas_call(
        body, grid=(R // tr,),
        in_specs=[pl.BlockSpec((N_DEV, tr, LANES), lambda i: (0, i, 0)), row, row, row],
        out_specs=[row, row, row, row], out_shape=[_sds((R, LANES), F32)] * 4,
        name=name, compiler_params=_cp(("parallel",)))(parts, w, m, v)


def _adamw_split(recvs, w, m, v, name, tr):
    depth, r, c = w.shape
    assert depth == len(recvs)
    tr = _tile(r, tr)

    def body(*refs):
        layer = pl.program_id(0)
        for ll in range(depth):
            @pl.when(layer == ll)
            def _(ll=ll):
                _adamw_body(refs[ll], *refs[depth:])

    wspec = pl.BlockSpec((None, tr, c), lambda l, i: (l, i, 0))
    rspecs = [pl.BlockSpec((N_DEV, None, tr, c), lambda l, i, ll=ll, t=t: (0, t, jnp.where(l == ll, i, 0), 0))
              for ll, (_, t) in enumerate(recvs)]
    return pl.pallas_call(
        body, grid=(depth, r // tr), in_specs=rspecs + [wspec, wspec, wspec],
        out_specs=[wspec] * 4, out_shape=[_sds(w.shape, F32)] * 4,
        name=name, compiler_params=_cp(("arbitrary", "arbitrary")))(*[a for a, _ in recvs], w, m, v)


def _merge_cols(g, name, tr=256):
    _, nt, K, n = g.shape
    tr = _tile(K, tr)

    def body(g_ref, o_ref):
        o_ref[...] = jnp.concatenate([g_ref[j] for j in range(N_DEV)], axis=1)

    return pl.pallas_call(
        body, grid=(nt, K // tr),
        in_specs=[pl.BlockSpec((N_DEV, None, tr, n), lambda t, i: (0, t, i, 0))],
        out_specs=pl.BlockSpec((None, tr, N_DEV * n), lambda t, i: (t, i, 0)),
        out_shape=_sds((nt, K, N_DEV * n), g.dtype), name=name, compiler_params=_cp(("parallel", "parallel")))(g)


def _merge_win(g, name, tr=256):
    _, nt, K, n = g.shape
    tr = _tile(K, tr)

    def body(g_ref, q_ref, r_ref):
        full = jnp.concatenate([g_ref[j] for j in range(N_DEV)], axis=1)
        q_ref[...] = full[:, 256:1792]
        zpad = jnp.zeros((tr, REST_W - 776), full.dtype)
        r_ref[...] = jnp.concatenate([full[:, 0:256], full[:, 1800:2312], full[:, 1792:1800], zpad], axis=1)

    return pl.pallas_call(
        body, grid=(nt, K // tr),
        in_specs=[pl.BlockSpec((N_DEV, None, tr, n), lambda t, i: (0, t, i, 0))],
        out_specs=[pl.BlockSpec((None, tr, 1536), lambda t, i: (t, i, 0)), pl.BlockSpec((None, tr, REST_W), lambda t, i: (t, i, 0))],
        out_shape=[_sds((nt, K, 1536), g.dtype), _sds((nt, K, REST_W), g.dtype)],
        name=name, compiler_params=_cp(("parallel", "parallel")))(g)


def _split_win(dq, dr, name, tr=256):
    K = dq.shape[0]
    tr = _tile(K, tr)
    n = (dq.shape[1] + 776) // N_DEV

    def body(q_ref, r_ref, o_ref):
        r = r_ref[...]
        full = jnp.concatenate([r[:, 0:256], q_ref[...], r[:, 768:776], r[:, 256:768]], axis=1)
        for j in range(N_DEV):
            o_ref[j] = full[:, n * j:n * (j + 1)]

    return pl.pallas_call(
        body, grid=(K // tr,),
        in_specs=[pl.BlockSpec((tr, dq.shape[1]), lambda i: (i, 0)), pl.BlockSpec((tr, REST_W), lambda i: (i, 0))],
        out_specs=pl.BlockSpec((N_DEV, tr, n), lambda i: (0, i, 0)),
        out_shape=_sds((N_DEV, K, n), F32), name=name, compiler_params=_cp(("parallel",)))(dq, dr)


WEIGHTS = ["ffn1_norm", "ffn1_w_gate", "ffn1_w_up", "ffn1_w_down", "mix_norm", "w_in", "pool_w", "pool_scale",
           "forget_bias", "conv_w", "conv_b", "conv_ln_g", "conv_ln_b", "w_out", "ffn2_norm", "ffn2_w_gate",
           "ffn2_w_up", "ffn2_w_down", "final_norm"]
FFN_COL = ["ffn1_w_gate", "ffn1_w_up", "ffn2_w_gate", "ffn2_w_up"]
FFN_ROW = ["ffn1_w_down", "ffn2_w_down"]
BIG = FFN_COL + FFN_ROW + ["w_in", "w_out"]
SMALL = [n for n in WEIGHTS if n not in BIG]


def _padded(n):
    return -(-n // PACK_ALIGN) * PACK_ALIGN


def _flat_pad(a):
    f = a.reshape(-1)
    return jnp.pad(f, (0, _padded(f.shape[0]) - f.shape[0]))


def _split8(a, axis):
    shp = a.shape
    a = a.reshape(shp[:axis] + (N_DEV, shp[axis] // N_DEV) + shp[axis + 1:])
    return jnp.moveaxis(a, axis, 0)


def _merge8(a, axis):
    a = jnp.moveaxis(a, 0, axis)
    shp = a.shape
    return a.reshape(shp[:axis] + (shp[axis] * shp[axis + 1],) + shp[axis + 2:])


def _pack_small(arrs):
    return jnp.concatenate([_flat_pad(arrs[n]) for n in SMALL]).reshape(-1, LANES)


def _pack_small_parts(grads):
    cols = []
    for n in SMALL:
        g = grads[n]
        if n == "conv_w":
            s = _split8(g, 2).reshape(N_DEV, -1)
        else:
            s = jnp.broadcast_to(g.reshape(1, -1), (N_DEV, g.size))
        cols.append(jnp.pad(s, ((0, 0), (0, _padded(s.shape[1]) - s.shape[1]))))
    return jnp.concatenate(cols, axis=1).reshape(N_DEV, -1, LANES)


def _unpack_small(buf, like):
    flat = buf.reshape(-1)
    out, off = {}, 0
    for n in SMALL:
        size = like[n].size
        out[n] = flat[off:off + size].reshape(like[n].shape)
        off += _padded(size)
    return out


class _Comm:
    def __init__(self, w):
        self.w = w
        self.bf = {n: w[n].astype(CDT) for n in BIG}
        self.ready = {}
        self.grads = {}
        self.recv = {}

    def _ffn_shards(self, l, which):
        cols = jnp.stack([self.bf[f"{which}_w_gate"][l], self.bf[f"{which}_w_up"][l]])
        return cols, self.bf[f"{which}_w_down"][l][None]

    def _put_ffn(self, l, which, cols, rows, t):
        self.ready[(l, which)] = dict(w_gate=cols[t], w_up=cols[t + 1], w_down=rows[t // 2])

    def weights_for(self, l, stage, x):
        bf = self.bf
        dep = None
        if (l, stage) == (0, "ffn1"):
            ga, gd = _all_gather(list(self._ffn_shards(0, "ffn1")), [False, True], "gather_l0_ffn1")
            self._put_ffn(0, "ffn1", _merge_cols(ga, "merge_l0_ffn1"), gd, 0)
            cols, rows = self._ffn_shards(0, "ffn2")
            self.started = _xfer_start([cols, rows, bf["w_in"][0:1], bf["w_out"][0:1], self.w["conv_w"]], True,
                                       [False, True, False, True, False], "gather_l0_start", dep=gd)
            dep = self.started[3]
        elif (l, stage) == (0, "mix"):
            ga, gd, gi, go, gc = _xfer_wait(self.started, x, True, [False, True, False, True, False], "gather_l0_wait")
            self._put_ffn(0, "ffn2", _merge_cols(ga, "merge_l0_ffn2"), gd, 0)
            q, r = _merge_win(gi, "merge_l0_w_in")
            self.conv_w = _merge8(gc, 2)
            self.ready[(0, "mix")] = dict(win_qkv=q[0], win_rest=r[0], w_out=go[0], conv_w=self.conv_w[0])
            c1, r1 = self._ffn_shards(1, "ffn1")
            c2, r2 = self._ffn_shards(1, "ffn2")
            self.started = _xfer_start([jnp.concatenate([c1, c2]), jnp.concatenate([r1, r2]), bf["w_in"][1:2], bf["w_out"][1:2]],
                                       True, [False, True, False, True], "gather_l1_start")
            dep = self.started[3]
        elif (l, stage) == (1, "ffn1"):
            ga, gd, gi, go = _xfer_wait(self.started, x, True, [False, True, False, True], "gather_l1_wait")
            cols = _merge_cols(ga, "merge_l1_ffn")
            self._put_ffn(1, "ffn1", cols, gd, 0)
            self._put_ffn(1, "ffn2", cols, gd, 2)
            q, r = _merge_win(gi, "merge_l1_w_in")
            self.ready[(1, "mix")] = dict(win_qkv=q[0], win_rest=r[0], w_out=go[0], conv_w=self.conv_w[1])
        return self.ready[(l, stage)], dep

    def grads_ready(self, l, stage, grads):
        for n, v in grads.items():
            self.grads[(l, n)] = v
        gr = self.grads
        four = [False, True, False, True]
        if (l, stage) == (1, "ffn1"):
            self.sent1 = _xfer_start(
                [jnp.stack([gr[(1, n)] for n in FFN_COL], axis=1), jnp.stack([gr[(1, n)] for n in FFN_ROW]),
                 gr[(1, "w_in")][:, None], gr[(1, "w_out")][None]], False, four, "grads_l1_start")
            return self.sent1[3]
        if (l, stage) == (0, "mix"):
            self.sent0 = _xfer_start(
                [jnp.stack([gr[(0, "ffn2_w_gate")], gr[(0, "ffn2_w_up")]], axis=1), gr[(0, "ffn2_w_down")][None],
                 gr[(0, "w_in")][:, None], gr[(0, "w_out")][None]], False, four, "grads_l0_start")
            return self.sent0[3]
        if (l, stage) == (0, "ffn1"):
            self.sent_last = _xfer_start(
                [jnp.stack([gr[(0, "ffn1_w_gate")], gr[(0, "ffn1_w_up")]], axis=1), gr[(0, "ffn1_w_down")][None]],
                False, [False, True], "grads_l0_ffn1_start")
            return self.sent_last[3]
        return None

    def finish(self, m, v, after):
        w, gr = self.w, self.grads
        depth = range(w["w_in"].shape[0])
        small = {n: (gr[(None, n)] if n == "final_norm" else jnp.stack([gr[(l, n)] for l in depth])) for n in SMALL}
        four = [False, True, False, True]
        c1, r1, i1, o1 = _xfer_wait(self.sent1, after, False, four, "grads_l1_wait")
        c2, r2, i0, o0 = _xfer_wait(self.sent0, after, False, four, "grads_l0_wait")

        def adam(n, recvs, tr):
            return _adamw_split(recvs, w[n], m[n], v[n], f"adamw_{n}", tr)

        res = {}
        res["ffn2_w_gate"] = adam("ffn2_w_gate", [(c2, 0), (c1, 2)], 256)
        res["ffn2_w_up"] = adam("ffn2_w_up", [(c2, 1), (c1, 3)], 256)
        res["ffn2_w_down"] = adam("ffn2_w_down", [(r2, 0), (r1, 1)], 176)
        res["w_in"] = adam("w_in", [(i0, 0), (i1, 0)], 256)
        res["w_out"] = adam("w_out", [(o0, 0), (o1, 0)], 128)
        rs, = _exchange([_pack_small_parts(small)], [False], "exchange_small")
        c0, r0 = _xfer_wait(self.sent_last, res["w_out"][0], False, [False, True], "grads_l0_ffn1_wait")
        res["ffn1_w_gate"] = adam("ffn1_w_gate", [(c0, 0), (c1, 0)], 256)
        res["ffn1_w_up"] = adam("ffn1_w_up", [(c0, 1), (c1, 1)], 256)
        res["ffn1_w_down"] = adam("ffn1_w_down", [(r0, 0), (r1, 0)], 176)
        packed = _adamw(rs, _pack_small(w), _pack_small(m), _pack_small(v), "adamw_small")
        unpacked = [_unpack_small(b, w) for b in packed]
        for n in SMALL:
            res[n] = [u[n] for u in unpacked]
        return res


def kernel(x, ffn1_norm, ffn1_w_gate, ffn1_w_up, ffn1_w_down, mix_norm, w_in, pool_w, pool_scale, forget_bias, conv_w, conv_b, conv_ln_g, conv_ln_b, w_out, ffn2_norm, ffn2_w_gate, ffn2_w_up, ffn2_w_down, final_norm, loss_target, m_ffn1_norm, m_ffn1_w_gate, m_ffn1_w_up, m_ffn1_w_down, m_mix_norm, m_w_in, m_pool_w, m_pool_scale, m_forget_bias, m_conv_w, m_conv_b, m_conv_ln_g, m_conv_ln_b, m_w_out, m_ffn2_norm, m_ffn2_w_gate, m_ffn2_w_up, m_ffn2_w_down, m_final_norm, v_ffn1_norm, v_ffn1_w_gate, v_ffn1_w_up, v_ffn1_w_down, v_mix_norm, v_w_in, v_pool_w, v_pool_scale, v_forget_bias, v_conv_w, v_conv_b, v_conv_ln_g, v_conv_ln_b, v_w_out, v_ffn2_norm, v_ffn2_w_gate, v_ffn2_w_up, v_ffn2_w_down, v_final_norm):
    w = dict(zip(WEIGHTS, (ffn1_norm, ffn1_w_gate, ffn1_w_up, ffn1_w_down, mix_norm, w_in, pool_w, pool_scale, forget_bias,
                           conv_w, conv_b, conv_ln_g, conv_ln_b, w_out, ffn2_norm, ffn2_w_gate, ffn2_w_up, ffn2_w_down,
                           final_norm)))
    m = dict(zip(WEIGHTS, (m_ffn1_norm, m_ffn1_w_gate, m_ffn1_w_up, m_ffn1_w_down, m_mix_norm, m_w_in, m_pool_w, m_pool_scale,
                           m_forget_bias, m_conv_w, m_conv_b, m_conv_ln_g, m_conv_ln_b, m_w_out, m_ffn2_norm, m_ffn2_w_gate,
                           m_ffn2_w_up, m_ffn2_w_down, m_final_norm)))
    v = dict(zip(WEIGHTS, (v_ffn1_norm, v_ffn1_w_gate, v_ffn1_w_up, v_ffn1_w_down, v_mix_norm, v_w_in, v_pool_w, v_pool_scale,
                           v_forget_bias, v_conv_w, v_conv_b, v_conv_ln_g, v_conv_ln_b, v_w_out, v_ffn2_norm, v_ffn2_w_gate,
                           v_ffn2_w_up, v_ffn2_w_down, v_final_norm)))
    comm = _Comm(w)
    loss_row, gx = _local_step(x[0], loss_target[0], w, comm.weights_for, comm.grads_ready)
    loss = lax.psum(loss_row[0, 0], ("x", "y", "c"))
    res = comm.finish(m, v, gx)
    return (loss, gx[None], *[res[n][i] for i in range(4) for n in WEIGHTS])
```

```python
import math

import numpy as np
import jax
import jax.numpy as jnp
from jax import lax
from jax.experimental import pallas as pl
from jax.experimental.pallas import tpu as pltpu

F32 = jnp.float32
CDT = jnp.bfloat16
NORM_EPS = 1e-6
N_DEV = 8
LANES = 128
PACK_ALIGN = 8 * LANES
VMEM_LIMIT = 48 * 1024 * 1024

POOL_WINDOWS = (2, 4, 8, 16)
POOL_HALO = 16
CONV_K = 31
CONV_HALO = 32
HEAD_DIM = 64
N_HEADS = 8
N_PAIRS = N_HEADS // 2
ATT_SCALE = 1.0 / math.sqrt(HEAD_DIM)
NEG = -1e30

ADAM_LR, ADAM_B1, ADAM_B2, ADAM_EPS, ADAM_WD, ADAM_STEP = 0.001, 0.9, 0.999, 1e-08, 0.01, 10

REST_W = 896
REST_Z_BLK = 6


def _cp(sem):
    return pltpu.CompilerParams(dimension_semantics=sem, vmem_limit_bytes=VMEM_LIMIT)


def _tile(n, pref):
    t = min(n, pref)
    assert n % t == 0, (n, pref)
    return t


def _sigmoid(x):
    return 1.0 / (1.0 + jnp.exp(-x))


def _sds(shape, dtype):
    return jax.ShapeDtypeStruct(shape, dtype)


_ANY = pl.BlockSpec(memory_space=pl.ANY)


def _dep(dep):
    return ([], []) if dep is None else ([_ANY], [dep])


def _rms_fwd(x, g, name, dep=None):
    T, D = x.shape
    tm = _tile(T, 1024)

    def body(x_ref, g_ref, *rest):
        o_ref = rest[-1]
        xv = x_ref[...]
        r = lax.rsqrt(jnp.mean(xv * xv, axis=-1, keepdims=True) + NORM_EPS)
        o_ref[...] = (xv * r * g_ref[...]).astype(o_ref.dtype)

    dspec, darg = _dep(dep)
    return pl.pallas_call(
        body, grid=(T // tm,),
        in_specs=[pl.BlockSpec((tm, D), lambda i: (i, 0)), pl.BlockSpec((1, D), lambda i: (0, 0))] + dspec,
        out_specs=pl.BlockSpec((tm, D), lambda i: (i, 0)),
        out_shape=_sds((T, D), CDT), name=name, compiler_params=_cp(("parallel",)))(x, g, *darg)


def _rms_bwd(x, g, dh, gres, name):
    T, D = x.shape
    tm = _tile(T, 512)

    def body(x_ref, g_ref, dh_ref, gres_ref, gin_ref, dg_ref):
        i = pl.program_id(0)
        xv = x_ref[...]
        d = dh_ref[...]
        r = lax.rsqrt(jnp.mean(xv * xv, axis=-1, keepdims=True) + NORM_EPS)
        xh = xv * r
        dxh = d * g_ref[...]
        c = jnp.mean(dxh * xh, axis=-1, keepdims=True)
        gin_ref[...] = gres_ref[...] + r * (dxh - xh * c)
        part = jnp.sum(d * xh, axis=0, keepdims=True)

        @pl.when(i == 0)
        def _():
            dg_ref[...] = part

        @pl.when(i > 0)
        def _():
            dg_ref[...] += part

    row = pl.BlockSpec((tm, D), lambda i: (i, 0))
    vec = pl.BlockSpec((1, D), lambda i: (0, 0))
    return pl.pallas_call(
        body, grid=(T // tm,), in_specs=[row, vec, row, row], out_specs=[row, vec],
        out_shape=[_sds((T, D), F32), _sds((1, D), F32)], name=name, compiler_params=_cp(("arbitrary",)))(x, g, dh, gres)


def _loss_bwd(x, g, target, name):
    T, D = x.shape
    tm = _tile(T, 512)

    def body(x_ref, g_ref, t_ref, loss_ref, dx_ref, dg_ref):
        i = pl.program_id(0)
        xv = x_ref[...]
        gv = g_ref[...]
        r = lax.rsqrt(jnp.mean(xv * xv, axis=-1, keepdims=True) + NORM_EPS)
        xh = xv * r
        err = xh * gv - t_ref[...]
        lpart = 0.5 * jnp.sum(jnp.mean(err * err, axis=-1, keepdims=True), axis=0, keepdims=True)
        dy = err * (1.0 / D)
        dxh = dy * gv
        c = jnp.mean(dxh * xh, axis=-1, keepdims=True)
        dx_ref[...] = r * (dxh - xh * c)
        part = jnp.sum(dy * xh, axis=0, keepdims=True)
        lrow = jnp.broadcast_to(lpart, (1, LANES))

        @pl.when(i == 0)
        def _():
            dg_ref[...] = part
            loss_ref[...] = lrow

        @pl.when(i > 0)
        def _():
            dg_ref[...] += part
            loss_ref[...] += lrow

    row = pl.BlockSpec((tm, D), lambda i: (i, 0))
    vec = pl.BlockSpec((1, D), lambda i: (0, 0))
    return pl.pallas_call(
        body, grid=(T // tm,), in_specs=[row, vec, row],
        out_specs=[pl.BlockSpec((1, LANES), lambda i: (0, 0)), row, vec],
        out_shape=[_sds((1, LANES), F32), _sds((T, D), F32), _sds((1, D), F32)],
        name=name, compiler_params=_cp(("arbitrary",)))(x, g, target)


def _mm(pairs, *, name, res=None, alpha=1.0, out_dtype=F32, tm=512, tn=None, dep=None):
    T = pairs[0][0].shape[0]
    N = pairs[0][1].shape[0] if pairs[0][2] else pairs[0][1].shape[1]
    tm = _tile(T, tm)
    tn = N if tn is None else _tile(N, tn)
    flags = [p[2] for p in pairs]
    n_in = 2 * len(pairs)

    def body(*refs):
        o_ref = refs[-1]
        acc = None
        for p, bt in enumerate(flags):
            a = refs[2 * p][...].astype(CDT)
            b = refs[2 * p + 1][...]
            dims = (((1,), (1,)), ((), ())) if bt else (((1,), (0,)), ((), ()))
            d = lax.dot_general(a, b, dims, preferred_element_type=F32)
            acc = d if acc is None else acc + d
        if alpha != 1.0:
            acc = acc * alpha
        if res is not None:
            acc = refs[n_in][...] + acc
        o_ref[...] = acc.astype(o_ref.dtype)

    in_specs, args = [], []
    for a, b, bt in pairs:
        K = a.shape[1]
        in_specs.append(pl.BlockSpec((tm, K), lambda i, j: (i, 0)))
        in_specs.append(pl.BlockSpec((tn, K), lambda i, j: (j, 0)) if bt else pl.BlockSpec((K, tn), lambda i, j: (0, j)))
        args += [a, b]
    if res is not None:
        in_specs.append(pl.BlockSpec((tm, tn), lambda i, j: (i, j)))
        args.append(res)
    dspec, darg = _dep(dep)
    in_specs += dspec
    args += darg
    return pl.pallas_call(
        body, grid=(T // tm, N // tn), in_specs=in_specs,
        out_specs=pl.BlockSpec((tm, tn), lambda i, j: (i, j)),
        out_shape=_sds((T, N), out_dtype), name=name, compiler_params=_cp(("parallel", "arbitrary")))(*args)


def _mm_tn(a, b, *, name, alpha=1.0, tk=512):
    T, M = a.shape
    N = b.shape[1]
    tm = M if M <= 1024 else M // 2
    tn = N if N <= 1536 else N // 2
    assert M % tm == 0 and N % tn == 0 and tm % LANES == 0 and tn % LANES == 0
    tk = _tile(T, tk)
    nk = T // tk

    def body(a_ref, b_ref, o_ref):
        k = pl.program_id(2)
        d = lax.dot_general(a_ref[...].astype(CDT), b_ref[...].astype(CDT), (((0,), (0,)), ((), ())),
                            preferred_element_type=F32)

        @pl.when(k == 0)
        def _():
            o_ref[...] = d

        @pl.when(k > 0)
        def _():
            o_ref[...] += d

        if alpha != 1.0:
            @pl.when(k == nk - 1)
            def _():
                o_ref[...] *= alpha

    return pl.pallas_call(
        body, grid=(M // tm, N // tn, nk),
        in_specs=[pl.BlockSpec((tk, tm), lambda i, j, k: (k, i)), pl.BlockSpec((tk, tn), lambda i, j, k: (k, j))],
        out_specs=pl.BlockSpec((tm, tn), lambda i, j, k: (i, j)),
        out_shape=_sds((M, N), F32), name=name, compiler_params=_cp(("parallel", "parallel", "arbitrary")))(a, b)


def _mm_tn_split(a, b, *, name, tk=512):
    T, M = a.shape
    N = b.shape[1]
    n = N // N_DEV
    parts = N_DEV // 2
    tn = parts * n
    assert M % LANES == 0 and N == N_DEV * n
    tk = _tile(T, tk)
    nk = T // tk

    def body(a_ref, b_ref, o_ref, acc):
        k = pl.program_id(1)
        d = lax.dot_general(a_ref[...].astype(CDT), b_ref[...].astype(CDT), (((0,), (0,)), ((), ())),
                            preferred_element_type=F32)

        @pl.when(k == 0)
        def _():
            acc[...] = d

        @pl.when(k > 0)
        def _():
            acc[...] += d

        @pl.when(k == nk - 1)
        def _():
            full = acc[...]
            for s in range(parts):
                o_ref[s] = full[:, s * n:(s + 1) * n]

    return pl.pallas_call(
        body, grid=(N // tn, nk),
        in_specs=[pl.BlockSpec((tk, M), lambda j, k: (k, 0)), pl.BlockSpec((tk, tn), lambda j, k: (k, j))],
        out_specs=pl.BlockSpec((parts, M, n), lambda j, k: (j, 0, 0)),
        out_shape=_sds((N_DEV, M, n), F32), scratch_shapes=[pltpu.VMEM((M, tn), F32)],
        name=name, compiler_params=_cp(("parallel", "arbitrary")))(a, b)


def _ffn_up(h, wg, wu, name):
    T, D = h.shape
    Fh = wg.shape[1]
    tm = _tile(T, 2048)
    tn = _tile(Fh, 256)

    def body(h_ref, wg_ref, wu_ref, a_ref, b_ref, s_ref):
        hv = h_ref[...]
        a = jnp.dot(hv, wg_ref[...], preferred_element_type=F32)
        b = jnp.dot(hv, wu_ref[...], preferred_element_type=F32)
        a_ref[...] = a.astype(a_ref.dtype)
        b_ref[...] = b.astype(b_ref.dtype)
        s_ref[...] = (a * _sigmoid(a) * b).astype(s_ref.dtype)

    wspec = pl.BlockSpec((D, tn), lambda i, j: (0, j))
    ospec = pl.BlockSpec((tm, tn), lambda i, j: (i, j))
    return pl.pallas_call(
        body, grid=(T // tm, Fh // tn),
        in_specs=[pl.BlockSpec((tm, D), lambda i, j: (i, 0)), wspec, wspec],
        out_specs=[ospec, ospec, ospec],
        out_shape=[_sds((T, Fh), CDT), _sds((T, Fh), CDT), _sds((T, Fh), CDT)],
        name=name, compiler_params=_cp(("parallel", "arbitrary")))(h, wg, wu)


def _ffn_bwd_ds(gout, wd, a, b, name, dep=None):
    T, D = gout.shape
    Fh = wd.shape[0]
    tm = _tile(T, 1024)
    tn = _tile(Fh, 256)

    def body(g_ref, wd_ref, a_ref, b_ref, *rest):
        da_ref, db_ref = rest[-2:]
        dy = (0.5 * g_ref[...]).astype(CDT)
        ds = lax.dot_general(dy, wd_ref[...], (((1,), (1,)), ((), ())), preferred_element_type=F32)
        av = a_ref[...].astype(F32)
        sg = _sigmoid(av)
        da_ref[...] = (ds * b_ref[...].astype(F32) * (sg * (1.0 + av * (1.0 - sg)))).astype(da_ref.dtype)
        db_ref[...] = (ds * (av * sg)).astype(db_ref.dtype)

    ospec = pl.BlockSpec((tm, tn), lambda i, j: (i, j))
    dspec, darg = _dep(dep)
    return pl.pallas_call(
        body, grid=(T // tm, Fh // tn),
        in_specs=[pl.BlockSpec((tm, D), lambda i, j: (i, 0)), pl.BlockSpec((tn, D), lambda i, j: (j, 0)), ospec, ospec] + dspec,
        out_specs=[ospec, ospec],
        out_shape=[_sds((T, Fh), CDT), _sds((T, Fh), CDT)],
        name=name, compiler_params=_cp(("parallel", "arbitrary")))(gout, wd, a, b, *darg)


def _ffn_fwd(x, gamma, wg, wu, wd, tag, dep=None):
    h = _rms_fwd(x, gamma, f"{tag}_norm", dep)
    a, b, s = _ffn_up(h, wg, wu, f"{tag}_up")
    y = _mm([(s, wd, False)], res=x, alpha=0.5, tn=512, name=f"{tag}_down")
    return y, (x, h, a, b, s)


def _ffn_bwd(saved, gamma, wg, wu, wd, gout, tag, dep, on_grads):
    x, h, a, b, s = saved
    da, db = _ffn_bwd_ds(gout, wd, a, b, f"{tag}_bwd_ds", dep)
    dwd = _mm_tn(s, gout, alpha=0.5, name=f"{tag}_dwd")
    dwg = _mm_tn_split(h, da, name=f"{tag}_dwg")
    dwu = _mm_tn_split(h, db, name=f"{tag}_dwu")
    dep = on_grads(dict(w_gate=dwg, w_up=dwu, w_down=dwd))
    dh = _mm([(da, wg, True), (db, wu, True)], tn=512, name=f"{tag}_dh", dep=dep)
    gin, dgamma = _rms_bwd(x, gamma, dh, gout, f"{tag}_norm_bwd")
    return gin, dgamma


def _fgate_fwd(rest, bias, name, bt=512):
    T = rest.shape[0]
    bt = _tile(T, bt)

    def body(z_ref, b_ref, fc_ref, ft_ref, carry):
        i = pl.program_id(0)

        @pl.when(i == 0)
        def _():
            carry[...] = jnp.zeros_like(carry)

        zb = z_ref[...] + b_ref[...]
        e = jnp.exp(-jnp.abs(zb))
        u = 1.0 + e
        log1p_e = jnp.where(u == 1.0, e, jnp.log(u) * (e / (u - 1.0)))
        x = jnp.minimum(zb, 0.0) - log1p_e
        row = lax.broadcasted_iota(jnp.int32, x.shape, 0)
        sh = 1
        while sh < bt:
            x = x + jnp.where(row >= sh, pltpu.roll(x, sh, 0), 0.0)
            sh *= 2
        f = x + carry[...]
        carry[...] = f[bt - 1:bt, :]
        fc_ref[...] = f
        ft_ref[...] = jnp.transpose(f)[0:N_HEADS, :]

    return pl.pallas_call(
        body, grid=(T // bt,),
        in_specs=[pl.BlockSpec((bt, LANES), lambda i: (i, REST_Z_BLK)), pl.BlockSpec((1, LANES), lambda i: (0, 0))],
        out_specs=[pl.BlockSpec((bt, LANES), lambda i: (i, 0)), pl.BlockSpec((N_HEADS, bt), lambda i: (0, i))],
        out_shape=[_sds((T, LANES), F32), _sds((N_HEADS, T), F32)],
        scratch_shapes=[pltpu.VMEM((1, LANES), F32)],
        name=name, compiler_params=_cp(("arbitrary",)))(rest, bias)


def _fgate_bwd(dfk, rest, bias, name, bt=512):
    T = rest.shape[0]
    bt = _tile(T, bt)
    nb = T // bt

    def body(df_ref, z_ref, b_ref, dz_ref, db_ref, carry):
        i = pl.program_id(0)

        @pl.when(i == 0)
        def _():
            carry[...] = jnp.zeros_like(carry)

        dfv = df_ref[...]
        lane = lax.broadcasted_iota(jnp.int32, (bt, LANES), 1)
        x = jnp.zeros((bt, LANES), F32)
        for h in range(N_HEADS):
            x = jnp.where(lane == h, dfv[:, HEAD_DIM * h:HEAD_DIM * h + 1], x)
        row = lax.broadcasted_iota(jnp.int32, x.shape, 0)
        sh = 1
        while sh < bt:
            x = x + jnp.where(row + sh < bt, pltpu.roll(x, bt - sh, 0), 0.0)
            sh *= 2
        dlf = x + carry[...]
        carry[...] = dlf[0:1, :]
        zb = z_ref[...] + b_ref[...]
        dz = jnp.where(lane < N_HEADS, dlf * _sigmoid(-zb), 0.0)
        dz_ref[...] = dz.astype(dz_ref.dtype)
        part = jnp.sum(dz, axis=0, keepdims=True)

        @pl.when(i == 0)
        def _():
            db_ref[...] = part

        @pl.when(i > 0)
        def _():
            db_ref[...] += part

    return pl.pallas_call(
        body, grid=(nb,),
        in_specs=[pl.BlockSpec((bt, 4 * LANES), lambda i: (nb - 1 - i, 0)),
                  pl.BlockSpec((bt, LANES), lambda i: (nb - 1 - i, REST_Z_BLK)),
                  pl.BlockSpec((1, LANES), lambda i: (0, 0))],
        out_specs=[pl.BlockSpec((bt, LANES), lambda i: (nb - 1 - i, 0)), pl.BlockSpec((1, LANES), lambda i: (0, 0))],
        out_shape=[_sds((T, LANES), CDT), _sds((1, LANES), F32)],
        scratch_shapes=[pltpu.VMEM((1, LANES), F32)],
        name=name, compiler_params=_cp(("arbitrary",)))(dfk, rest, bias)


def _by_group(vals, lane):
    out = vals[-1]
    for g in range(len(vals) - 2, -1, -1):
        out = jnp.where(lane // 64 == g, vals[g], out)
    return out


def _pool_counts(t0, n, lane):
    t = t0 + lax.broadcasted_iota(jnp.int32, (n, 256), 0)
    return _by_group([jnp.minimum(t + 1, w) for w in POOL_WINDOWS], lane).astype(F32)


def _pooled(u, halo, i, bt):
    lane = lax.broadcasted_iota(jnp.int32, (bt, 256), 1)
    ext = jnp.concatenate([jnp.where(i > 0, halo, 0.0), u], axis=0)
    sums, s, sh = [], ext, 1
    for _ in POOL_WINDOWS:
        s = s + pltpu.roll(s, sh, 0)
        sums.append(s[POOL_HALO:, :])
        sh *= 2
    return _by_group(sums, lane) / _pool_counts(i * bt, bt, lane) - u


def _pool_fwd(rest, wbd, scale, name, bt=512):
    T = rest.shape[0]
    bt = _tile(T, bt)
    hb = bt // POOL_HALO

    def body(u_ref, halo_ref, w_ref, sc_ref, o_ref):
        i = pl.program_id(0)
        pooled = _pooled(u_ref[...], halo_ref[...], i, bt)
        mixed = jnp.dot(pooled.astype(CDT), w_ref[...], preferred_element_type=F32)
        o_ref[...] = (mixed * sc_ref[...]).astype(o_ref.dtype)

    return pl.pallas_call(
        body, grid=(T // bt,),
        in_specs=[pl.BlockSpec((bt, 256), lambda i: (i, 0)),
                  pl.BlockSpec((POOL_HALO, 256), lambda i: (jnp.maximum(i * hb - 1, 0), 0)),
                  pl.BlockSpec((256, 256), lambda i: (0, 0)), pl.BlockSpec((1, 256), lambda i: (0, 0))],
        out_specs=pl.BlockSpec((bt, 256), lambda i: (i, 0)),
        out_shape=_sds((T, 256), CDT), name=name, compiler_params=_cp(("parallel",)))(rest, rest, wbd, scale)


def _pool_bwd(dcat, rest, wbd, scale, name, bt=512):
    T = rest.shape[0]
    bt = _tile(T, bt)
    hb = bt // POOL_HALO
    nb = T // bt
    n = bt + POOL_HALO

    def body(dy_ref, dyn_ref, u_ref, halo_ref, w_ref, sc_ref, du_ref, dw_ref, dsc_ref):
        i = pl.program_id(0)
        lane = lax.broadcasted_iota(jnp.int32, (bt, 256), 1)
        w = w_ref[...]
        sc = sc_ref[...]
        pooled = _pooled(u_ref[...], halo_ref[...], i, bt)
        pooled_c = pooled.astype(CDT)
        mixed = jnp.dot(pooled_c, w, preferred_element_type=F32)
        dy = dy_ref[...]
        dm = (dy * sc).astype(CDT)
        dsc = jnp.sum(dy * mixed, axis=0, keepdims=True)
        dw = lax.dot_general(pooled_c, dm, (((0,), (0,)), ((), ())), preferred_element_type=F32)
        nt = (((1,), (1,)), ((), ()))
        dpl = lax.dot_general(dm, w, nt, preferred_element_type=F32)
        dmn = (jnp.where(i < nb - 1, dyn_ref[...], 0.0) * sc).astype(CDT)
        dpln = lax.dot_general(dmn, w, nt, preferred_element_type=F32)
        lane_h = lax.broadcasted_iota(jnp.int32, (POOL_HALO, 256), 1)
        ext = jnp.concatenate([dpl / _pool_counts(i * bt, bt, lane),
                               dpln / _pool_counts((i + 1) * bt, POOL_HALO, lane_h)], axis=0)
        sums, s, sh = [], ext, 1
        for _ in POOL_WINDOWS:
            s = s + pltpu.roll(s, n - sh, 0)
            sums.append(s[0:bt, :])
            sh *= 2
        du_ref[...] = (_by_group(sums, lane) - dpl).astype(du_ref.dtype)

        @pl.when(i == 0)
        def _():
            dw_ref[...] = dw
            dsc_ref[...] = dsc

        @pl.when(i > 0)
        def _():
            dw_ref[...] += dw
            dsc_ref[...] += dsc

    full = pl.BlockSpec((256, 256), lambda i: (0, 0))
    vec = pl.BlockSpec((1, 256), lambda i: (0, 0))
    return pl.pallas_call(
        body, grid=(nb,),
        in_specs=[pl.BlockSpec((bt, 256), lambda i: (i, 0)),
                  pl.BlockSpec((POOL_HALO, 256), lambda i: (jnp.minimum((i + 1) * hb, nb * hb - 1), 0)),
                  pl.BlockSpec((bt, 256), lambda i: (i, 0)),
                  pl.BlockSpec((POOL_HALO, 256), lambda i: (jnp.maximum(i * hb - 1, 0), 0)),
                  full, vec],
        out_specs=[pl.BlockSpec((bt, 256), lambda i: (i, 0)), full, vec],
        out_shape=[_sds((T, 256), CDT), _sds((256, 256), F32), _sds((1, 256), F32)],
        name=name, compiler_params=_cp(("arbitrary",)))(dcat, dcat, rest, rest, wbd, scale)


def _glu_ext(a_ref, g_ref, ah_ref, gh_ref, i):
    u = a_ref[...] * _sigmoid(g_ref[...])
    uh = jnp.where(i > 0, ah_ref[...] * _sigmoid(gh_ref[...]), 0.0)
    return jnp.concatenate([uh, u], axis=0)


def _conv_fwd(rest, cw, cb, lg, lb, name, bt=512):
    T = rest.shape[0]
    bt = _tile(T, bt)
    hb = bt // CONV_HALO

    def body(a_ref, g_ref, ah_ref, gh_ref, cw_ref, cb_ref, lg_ref, lb_ref, o_ref, y_ref):
        i = pl.program_id(0)
        ext = _glu_ext(a_ref, g_ref, ah_ref, gh_ref, i)
        w = cw_ref[...]
        acc = w[CONV_K - 1:CONV_K, :] * ext
        for k in range(CONV_K - 1):
            acc = acc + w[k:k + 1, :] * pltpu.roll(ext, CONV_K - 1 - k, 0)
        y = acc[CONV_HALO:, :] + cb_ref[...]
        y_ref[...] = y
        yc = y - jnp.mean(y, axis=-1, keepdims=True)
        yn = yc * lax.rsqrt(jnp.mean(yc * yc, axis=-1, keepdims=True) + NORM_EPS)
        z = yn * lg_ref[...] + lb_ref[...]
        o_ref[...] = (z * _sigmoid(z)).astype(o_ref.dtype)

    def cur(c):
        return pl.BlockSpec((bt, 256), lambda i: (i, c))

    def prev(c):
        return pl.BlockSpec((CONV_HALO, 256), lambda i: (jnp.maximum(i * hb - 1, 0), c))

    vec = pl.BlockSpec((1, 256), lambda i: (0, 0))
    return pl.pallas_call(
        body, grid=(T // bt,),
        in_specs=[cur(1), cur(2), prev(1), prev(2), pl.BlockSpec((CONV_HALO, 256), lambda i: (0, 0)), vec, vec, vec],
        out_specs=[pl.BlockSpec((bt, 256), lambda i: (i, 0)), pl.BlockSpec((bt, 256), lambda i: (i, 0))],
        out_shape=[_sds((T, 256), CDT), _sds((T, 256), F32)],
        name=name, compiler_params=_cp(("parallel",)))(rest, rest, rest, rest, cw, cb, lg, lb)


def _conv_bwd(dcat, yconv, rest, cw, lg, lb, name, bt=512):
    T = rest.shape[0]
    bt = _tile(T, bt)
    hb = bt // CONV_HALO
    nb = T // bt
    n = bt + CONV_HALO

    def body(dy_ref, dyn_ref, y_ref, yn_ref, a_ref, g_ref, ah_ref, gh_ref, cw_ref, lg_ref, lb_ref,
             da_ref, dg_ref, dcw_ref, dcb_ref, dlg_ref, dlb_ref):
        i = pl.program_id(0)
        lgv = lg_ref[...]
        lbv = lb_ref[...]

        def ln_swish_bwd(dout, y):
            yc = y - jnp.mean(y, axis=-1, keepdims=True)
            rs = lax.rsqrt(jnp.mean(yc * yc, axis=-1, keepdims=True) + NORM_EPS)
            yn = yc * rs
            z = yn * lgv + lbv
            sg = _sigmoid(z)
            dz = dout * (sg * (1.0 + z * (1.0 - sg)))
            dyn = dz * lgv
            dyc = rs * (dyn - jnp.mean(dyn, axis=-1, keepdims=True) - yn * jnp.mean(dyn * yn, axis=-1, keepdims=True))
            return dyc, dz, yn

        dyc, dz, yn = ln_swish_bwd(dy_ref[...], y_ref[...])
        dyc_next, _, _ = ln_swish_bwd(dyn_ref[...], yn_ref[...])
        dyc_next = jnp.where(i < nb - 1, dyc_next, 0.0)
        ext_u = _glu_ext(a_ref, g_ref, ah_ref, gh_ref, i)
        ext_d = jnp.concatenate([dyc, dyc_next], axis=0)
        w = cw_ref[...]
        du = w[CONV_K - 1:CONV_K, :] * ext_d
        rows = []
        for k in range(CONV_K):
            s = CONV_K - 1 - k
            if s > 0:
                du = du + w[k:k + 1, :] * pltpu.roll(ext_d, n - s, 0)
                us = pltpu.roll(ext_u, s, 0)[CONV_HALO:, :]
            else:
                us = ext_u[CONV_HALO:, :]
            rows.append(jnp.sum(dyc * us, axis=0, keepdims=True))
        rows.append(jnp.zeros((1, 256), F32))
        dcw = jnp.concatenate(rows, axis=0)
        du = du[0:bt, :]
        av = a_ref[...]
        sg = _sigmoid(g_ref[...])
        da_ref[...] = (du * sg).astype(da_ref.dtype)
        dg_ref[...] = (du * av * (sg * (1.0 - sg))).astype(dg_ref.dtype)
        dcb = jnp.sum(dyc, axis=0, keepdims=True)
        dlg = jnp.sum(dz * yn, axis=0, keepdims=True)
        dlb = jnp.sum(dz, axis=0, keepdims=True)

        @pl.when(i == 0)
        def _():
            dcw_ref[...] = dcw
            dcb_ref[...] = dcb
            dlg_ref[...] = dlg
            dlb_ref[...] = dlb

        @pl.when(i > 0)
        def _():
            dcw_ref[...] += dcw
            dcb_ref[...] += dcb
            dlg_ref[...] += dlg
            dlb_ref[...] += dlb

    def cur(c):
        return pl.BlockSpec((bt, 256), lambda i: (i, c))

    def prev(c):
        return pl.BlockSpec((CONV_HALO, 256), lambda i: (jnp.maximum(i * hb - 1, 0), c))

    def nxt(c):
        return pl.BlockSpec((CONV_HALO, 256), lambda i: (jnp.minimum((i + 1) * hb, nb * hb - 1), c))

    vec = pl.BlockSpec((1, 256), lambda i: (0, 0))
    wfull = pl.BlockSpec((CONV_HALO, 256), lambda i: (0, 0))
    return pl.pallas_call(
        body, grid=(nb,),
        in_specs=[cur(3), nxt(3), cur(0), nxt(0), cur(1), cur(2), prev(1), prev(2), wfull, vec, vec],
        out_specs=[cur(0), cur(0), wfull, vec, vec, vec],
        out_shape=[_sds((T, 256), CDT), _sds((T, 256), CDT), _sds((CONV_HALO, 256), F32),
                   _sds((1, 256), F32), _sds((1, 256), F32), _sds((1, 256), F32)],
        name=name, compiler_params=_cp(("arbitrary",)))(dcat, dcat, yconv, yconv, rest, rest, rest, rest, cw, lg, lb)


def _half_mask(shape, a):
    lane = lax.broadcasted_iota(jnp.int32, shape, 1)
    return (lane // HEAD_DIM) == a


def _attn_fwd(qkv, fcol, frow, name, blk=512):
    T = qkv.shape[0]
    blk = _tile(T, blk)
    nq = T // blk
    nt = (((1,), (1,)), ((), ()))

    def body(q_ref, k_ref, v_ref, fc_ref, fr_ref, o_ref, lse_ref):
        p_id = pl.program_id(0)
        i = pl.program_id(1)
        q2 = q_ref[...]
        fc = fc_ref[...]
        lane = lax.broadcasted_iota(jnp.int32, (blk, LANES), 1)
        tri = lax.broadcasted_iota(jnp.int32, (blk, blk), 1) <= lax.broadcasted_iota(jnp.int32, (blk, blk), 0)
        masks = [_half_mask(q2.shape, a) for a in range(2)]
        qs = [jnp.where(hm, q2, jnp.zeros_like(q2)) * ATT_SCALE for hm in masks]
        fqs = [jnp.sum(jnp.where(lane == 2 * p_id + a, fc, 0.0), axis=1, keepdims=True) for a in range(2)]

        def tile(j, carry, masked):
            cols = pl.ds(pl.multiple_of(j * blk, blk), blk)
            kj = k_ref[cols, :]
            vj = v_ref[cols, :]
            out = []
            for a in range(2):
                m, acc = carry[2 * a:2 * a + 2]
                va = jnp.where(masks[a], vj, jnp.ones_like(vj))
                s = lax.dot_general(qs[a], kj, nt, preferred_element_type=F32) + (fqs[a] - fr_ref[a:a + 1, cols])
                if masked:
                    s = jnp.where(tri, s, NEG)
                m_new = jnp.maximum(m, jnp.max(s, axis=1, keepdims=True))
                alpha = jnp.exp(m - m_new)
                pr = jnp.exp(s - m_new)
                hi = lax.bitcast_convert_type(lax.bitcast_convert_type(pr, jnp.uint32) & jnp.uint32(0xFFFF0000), F32)
                pv = (jnp.dot(hi.astype(CDT), va, preferred_element_type=F32)
                      + jnp.dot((pr - hi).astype(CDT), va, preferred_element_type=F32))
                out += [m_new, alpha * acc + pv]
            return tuple(out)

        init = (jnp.full((blk, 1), NEG, F32), jnp.zeros((blk, LANES), F32)) * 2
        carry = lax.fori_loop(0, i, lambda j, c: tile(j, c, False), init)
        carry = tile(i, carry, True)
        ls = [carry[1][:, HEAD_DIM:HEAD_DIM + 1], carry[3][:, 0:1]]
        lo = lane < HEAD_DIM
        o_ref[...] = jnp.where(lo, carry[1] / ls[0], carry[3] / ls[1])
        lse_t = jnp.transpose(jnp.where(lo, carry[0] + jnp.log(ls[0]), carry[2] + jnp.log(ls[1])))
        lse_ref[...] = jnp.concatenate([lse_t[0:1, :], lse_t[HEAD_DIM:HEAD_DIM + 1, :]], axis=0)

    return pl.pallas_call(
        body, grid=(N_PAIRS, nq),
        in_specs=[pl.BlockSpec((blk, LANES), lambda p, i: (i, p)),
                  pl.BlockSpec((T, LANES), lambda p, i: (0, N_PAIRS + p)),
                  pl.BlockSpec((T, LANES), lambda p, i: (0, 2 * N_PAIRS + p)),
                  pl.BlockSpec((blk, LANES), lambda p, i: (i, 0)),
                  pl.BlockSpec((None, 2, T), lambda p, i: (p, 0, 0))],
        out_specs=[pl.BlockSpec((blk, LANES), lambda p, i: (i, p)), pl.BlockSpec((None, 2, blk), lambda p, i: (p, 0, i))],
        out_shape=[_sds((T, N_PAIRS * LANES), F32), _sds((N_PAIRS, 2, T), F32)],
        name=name, compiler_params=_cp(("parallel", "arbitrary")))(qkv, qkv, qkv, fcol, frow)


def _attn_delta(dcat, o, name, blk=512):
    T = o.shape[0]
    blk = _tile(T, blk)

    def body(d_ref, o_ref, out_ref):
        prod = d_ref[:, 256:768].astype(CDT).astype(F32) * o_ref[...]
        pt = jnp.transpose(prod)
        out_ref[...] = jnp.sum(pt.reshape(N_HEADS, HEAD_DIM, blk), axis=1)

    return pl.pallas_call(
        body, grid=(T // blk,),
        in_specs=[pl.BlockSpec((blk, 1024), lambda i: (i, 0)), pl.BlockSpec((blk, 512), lambda i: (i, 0))],
        out_specs=pl.BlockSpec((N_HEADS, blk), lambda i: (0, i)),
        out_shape=_sds((N_HEADS, T), F32), name=name, compiler_params=_cp(("parallel",)))(dcat, o)


def _attn_bwd(qkv, dcat, fcol, frow, lse, delta, name, blk=512):
    T = qkv.shape[0]
    blk = _tile(T, blk)
    nq = T // blk
    nt = (((1,), (1,)), ((), ()))

    def body(q_ref, do_ref, k_ref, v_ref, fc_ref, fr_ref, lse_ref, dl_ref, dqt_ref, dk_ref, dv_ref, df_ref):
        p_id = pl.program_id(0)
        j = pl.program_id(1)

        @pl.when(j == 0)
        def _():
            dqt_ref[...] = jnp.zeros_like(dqt_ref)

        k2 = k_ref[...]
        v2 = v_ref[...]
        fc = fc_ref[...]
        lane = lax.broadcasted_iota(jnp.int32, (blk, LANES), 1)
        tri = lax.broadcasted_iota(jnp.int32, (blk, blk), 0) <= lax.broadcasted_iota(jnp.int32, (blk, blk), 1)
        masks = [_half_mask(k2.shape, a) for a in range(2)]
        kas = [jnp.where(hm, k2, jnp.zeros_like(k2)) * ATT_SCALE for hm in masks]
        kats = [jnp.transpose(ka) for ka in kas]
        vas = [jnp.where(hm, v2, jnp.zeros_like(v2)) for hm in masks]
        fks = [jnp.sum(jnp.where(lane == 2 * p_id + a, fc, 0.0), axis=1, keepdims=True) for a in range(2)]

        def tile(i, carry, masked):
            rows = pl.ds(pl.multiple_of(i * blk, blk), blk)
            qi = q_ref[rows, :]
            doi = do_ref[rows, :].astype(CDT)
            out = []
            dqt = None
            for a in range(2):
                dk_acc, dv_acc, df_acc = carry[3 * a:3 * a + 3]
                st = lax.dot_general(kas[a], qi, nt, preferred_element_type=F32)
                e = (st + (fr_ref[a:a + 1, rows] - fks[a])) - lse_ref[a:a + 1, rows]
                if masked:
                    e = jnp.where(tri, e, NEG)
                pt = jnp.exp(e)
                dpt = lax.dot_general(vas[a], doi, nt, preferred_element_type=F32)
                ds32 = pt * (dpt - dl_ref[a:a + 1, rows])
                dst = ds32.astype(CDT)
                df_acc = df_acc + jnp.sum(ds32, axis=1, keepdims=True)
                dv_acc = dv_acc + jnp.dot(pt.astype(CDT), doi, preferred_element_type=F32)
                dk_acc = dk_acc + jnp.dot(dst, qi, preferred_element_type=F32)
                part = jnp.dot(kats[a], dst, preferred_element_type=F32)
                dqt = part if dqt is None else dqt + part
                out += [dk_acc, dv_acc, df_acc]
            dqt_ref[:, rows] += dqt
            return tuple(out)

        init = (jnp.zeros((blk, LANES), F32), jnp.zeros((blk, LANES), F32), jnp.zeros((blk, 1), F32)) * 2
        carry = tile(j, init, True)
        carry = lax.fori_loop(j + 1, nq, lambda i, c: tile(i, c, False), carry)
        lo = lane < HEAD_DIM
        dk_ref[...] = jnp.where(lo, carry[0], carry[3]) * ATT_SCALE
        dv_ref[...] = jnp.where(lo, carry[1], carry[4])
        df_ref[...] = -jnp.where(lo, carry[2], carry[5])

    res = pl.BlockSpec((T, LANES), lambda p, j: (0, p))
    rows = pl.BlockSpec((None, 2, T), lambda p, j: (p, 0, 0))
    kv_out = pl.BlockSpec((blk, LANES), lambda p, j: (j, p))
    return pl.pallas_call(
        body, grid=(N_PAIRS, nq),
        in_specs=[res, pl.BlockSpec((T, LANES), lambda p, j: (0, 2 + p)),
                  pl.BlockSpec((blk, LANES), lambda p, j: (j, N_PAIRS + p)),
                  pl.BlockSpec((blk, LANES), lambda p, j: (j, 2 * N_PAIRS + p)),
                  pl.BlockSpec((blk, LANES), lambda p, j: (j, 0)), rows, rows, rows],
        out_specs=[pl.BlockSpec((LANES, T), lambda p, j: (p, 0)), kv_out, kv_out, kv_out],
        out_shape=[_sds((N_PAIRS * LANES, T), F32)] + [_sds((T, N_PAIRS * LANES), F32)] * 3,
        name=name, compiler_params=_cp(("parallel", "arbitrary")))(qkv, dcat, qkv, qkv, fcol, frow, lse, delta)


def _mixer_fwd(x, wts, tag, dep=None):
    T = x.shape[0]
    h = _rms_fwd(x, wts["mix_norm"], f"{tag}_norm", dep)
    qkv = _mm([(h, wts["win_qkv"], False)], out_dtype=CDT, tm=1024, tn=768, name=f"{tag}_in_qkv")
    rest = _mm([(h, wts["win_rest"], False)], tm=1024, name=f"{tag}_in_rest")
    fcol, frow8 = _fgate_fwd(rest, wts["fbias"], f"{tag}_fgate")
    frow = frow8.reshape(N_PAIRS, 2, T)
    ya = _pool_fwd(rest, wts["pool_wbd"], wts["pool_scale"], f"{tag}_pool")
    o, lse = _attn_fwd(qkv, fcol, frow, f"{tag}_attn")
    yc, yconv = _conv_fwd(rest, wts["conv_w"], wts["conv_b"], wts["conv_ln_g"], wts["conv_ln_b"], f"{tag}_conv")
    cat = jnp.concatenate([ya, o.astype(CDT), yc], axis=1)
    y = _mm([(cat, wts["w_out"], False)], res=x, tn=512, name=f"{tag}_out")
    return y, (x, h, qkv, rest, fcol, frow, o, lse, yconv, cat)


def _mixer_bwd(saved, wts, gout, tag, dep=None):
    x, h, qkv, rest, fcol, frow, o, lse, yconv, cat = saved
    T = x.shape[0]
    dcat = _mm([(gout, wts["w_out"], True)], tn=512, name=f"{tag}_dcat", dep=dep)
    dwout = _mm_tn(cat, gout, name=f"{tag}_dwout")
    du, dpw, dpsc = _pool_bwd(dcat, rest, wts["pool_wbd"], wts["pool_scale"], f"{tag}_pool_bwd")
    delta = _attn_delta(dcat, o, f"{tag}_attn_delta").reshape(N_PAIRS, 2, T)
    dqt, dk, dv, dfk = _attn_bwd(qkv, dcat, fcol, frow, lse, delta, f"{tag}_attn_bwd")
    dq = dqt.T
    dz, dfb = _fgate_bwd(dfk, rest, wts["fbias"], f"{tag}_fgate_bwd")
    da, dg, dcw, dcb, dlg, dlb = _conv_bwd(dcat, yconv, rest, wts["conv_w"], wts["conv_ln_g"], wts["conv_ln_b"],
                                           f"{tag}_conv_bwd")
    dp_qkv = jnp.concatenate([dq, dk, dv], axis=1).astype(CDT)
    dp_rest = jnp.concatenate([du, da, dg, dz], axis=1)
    dwin_qkv = _mm_tn(h, dp_qkv, name=f"{tag}_dwin_qkv")
    dwin_rest = _mm_tn(h, dp_rest, name=f"{tag}_dwin_rest")
    dh = _mm([(dp_qkv, wts["win_qkv"], True), (dp_rest, wts["win_rest"], True)], tn=512, name=f"{tag}_dh")
    gin, dgamma = _rms_bwd(x, wts["mix_norm"], dh, gout, f"{tag}_norm_bwd")
    dwin = _split_win(dwin_qkv, dwin_rest, f"{tag}_dwin_split")
    dpool_w = jnp.stack([dpw[64 * g:64 * g + 64, 64 * g:64 * g + 64] for g in range(4)])
    grads = dict(mix_norm=dgamma[0], w_in=dwin, pool_w=dpool_w, pool_scale=dpsc[0], forget_bias=dfb[0, 0:N_HEADS],
                 conv_w=dcw[0:CONV_K], conv_b=dcb[0], conv_ln_g=dlg[0], conv_ln_b=dlb[0], w_out=dwout)
    return gin, grads


def _rep_layer(rep, l):
    pw = rep["pool_w"][l].astype(CDT)
    wbd = jnp.zeros((256, 256), CDT)
    for g in range(4):
        wbd = lax.dynamic_update_slice(wbd, pw[g], (64 * g, 64 * g))
    return dict(
        ffn1_norm=rep["ffn1_norm"][l][None], ffn2_norm=rep["ffn2_norm"][l][None], mix_norm=rep["mix_norm"][l][None],
        fbias=jnp.pad(rep["forget_bias"][l], (0, LANES - N_HEADS))[None],
        pool_wbd=wbd, pool_scale=rep["pool_scale"][l][None], conv_b=rep["conv_b"][l][None],
        conv_ln_g=rep["conv_ln_g"][l][None], conv_ln_b=rep["conv_ln_b"][l][None])


def _local_step(x, target, rep, weights_for, grads_ready):
    depth = rep["ffn1_norm"].shape[0]
    kept = []
    for l in range(depth):
        r = _rep_layer(rep, l)
        w1, dep = weights_for(l, "ffn1", x)
        x, s1 = _ffn_fwd(x, r["ffn1_norm"], w1["w_gate"], w1["w_up"], w1["w_down"], f"l{l}_ffn1", dep)
        wm, dep = weights_for(l, "mix", x)
        wm = dict(r, win_qkv=wm["win_qkv"], win_rest=wm["win_rest"], w_out=wm["w_out"],
                  conv_w=jnp.pad(wm["conv_w"], ((0, CONV_HALO - CONV_K), (0, 0))))
        x, s2 = _mixer_fwd(x, wm, f"l{l}_mix", dep)
        w2, dep = weights_for(l, "ffn2", x)
        x, s3 = _ffn_fwd(x, r["ffn2_norm"], w2["w_gate"], w2["w_up"], w2["w_down"], f"l{l}_ffn2", dep)
        kept.append((r, w1, wm, w2, s1, s2, s3))
    loss, g, dfinal = _loss_bwd(x, rep["final_norm"][None], target, "loss_head")
    dep = grads_ready(None, "final", dict(final_norm=dfinal[0]))
    for l in reversed(range(depth)):
        r, w1, wm, w2, s1, s2, s3 = kept[l]

        def ffn_grads(which, l=l):
            return lambda gr: grads_ready(l, which, {f"{which}_{k}": v for k, v in gr.items()})

        g, dn = _ffn_bwd(s3, r["ffn2_norm"], w2["w_gate"], w2["w_up"], w2["w_down"], g, f"l{l}_ffn2", dep, ffn_grads("ffn2"))
        grads_ready(l, "norm", dict(ffn2_norm=dn[0]))
        g, gm = _mixer_bwd(s2, wm, g, f"l{l}_mix")
        dep = grads_ready(l, "mix", gm)
        g, dn = _ffn_bwd(s1, r["ffn1_norm"], w1["w_gate"], w1["w_up"], w1["w_down"], g, f"l{l}_ffn1", dep, ffn_grads("ffn1"))
        dep = grads_ready(l, "norm", dict(ffn1_norm=dn[0]))
    return loss, g


def _mesh_pos():
    return lax.axis_index("x"), lax.axis_index("y"), lax.axis_index("c")


def _dev_block(ref, dev, by_rows):
    if by_rows:
        r = ref.shape[1] // N_DEV
        return ref.at[:, pl.ds(dev * r, r), :]
    return ref.at[dev]


def _all_gather(shards, by_rows, name):
    n_arr = len(shards)
    out_shape = [_sds((s.shape[0], N_DEV * s.shape[1], s.shape[2]) if br else (N_DEV,) + s.shape, s.dtype)
                 for s, br in zip(shards, by_rows)]

    def body(*refs):
        xs, outs = refs[:n_arr], refs[n_arr:2 * n_arr]
        send_sems, recv_sems, local_sems = refs[2 * n_arr:]
        x, y, c = _mesh_pos()
        me, sibling = (x, y, c), (x, y, 1 - c)
        chips = [(1 - x, y), (x, 1 - y), (1 - x, 1 - y)]

        def rows(a, px, py, pc):
            return _dev_block(outs[a], 4 * px + 2 * py + pc, by_rows[a])

        def copy(k, a, block, to, src=None):
            return pltpu.make_async_remote_copy(
                src_ref=rows(a, *block) if src is None else src, dst_ref=rows(a, *block),
                send_sem=send_sems.at[k, a], recv_sem=recv_sems.at[k, a],
                device_id=to, device_id_type=pl.DeviceIdType.MESH)

        arrs = range(n_arr)
        mine = [pltpu.make_async_copy(xs[a], rows(a, *me), local_sems.at[a]) for a in arrs]
        for cp in mine:
            cp.start()
        first = [copy(0, a, me, sibling, src=xs[a]) for a in arrs]
        first += [copy(1 + j, a, me, (*chip, c), src=xs[a]) for j, chip in enumerate(chips) for a in arrs]
        for cp in first:
            cp.start()
        passed = []
        for j, chip in enumerate(chips):
            for a in arrs:
                copy(1 + j, a, (*chip, c), me).wait_recv()
                passed.append(copy(4 + j, a, (*chip, c), sibling))
                passed[-1].start()
        for a in arrs:
            copy(0, a, sibling, me).wait_recv()
        for j, chip in enumerate(chips):
            for a in arrs:
                copy(4 + j, a, (*chip, 1 - c), me).wait_recv()
        for cp in first + passed:
            cp.wait_send()
        for cp in mine:
            cp.wait()

    hbm = pl.BlockSpec(memory_space=pl.ANY)
    return pl.pallas_call(
        body, out_shape=out_shape, in_specs=[hbm] * n_arr, out_specs=[hbm] * n_arr,
        scratch_shapes=[pltpu.SemaphoreType.DMA((7, n_arr)), pltpu.SemaphoreType.DMA((7, n_arr)),
                        pltpu.SemaphoreType.DMA((n_arr,))],
        name=name)(*shards)


def _exchange(parts, by_rows, name):
    n_arr = len(parts)
    out_shape = [_sds((N_DEV, p.shape[0], p.shape[1] // N_DEV, p.shape[2]) if br else p.shape, p.dtype)
                 for p, br in zip(parts, by_rows)]

    def body(*refs):
        ps, outs = refs[:n_arr], refs[n_arr:2 * n_arr]
        send_sems, recv_sems, local_sems = refs[2 * n_arr:]
        x, y, c = _mesh_pos()
        my = 4 * x + 2 * y + c
        arrs = range(n_arr)
        mine = [pltpu.make_async_copy(_dev_block(ps[a], my, by_rows[a]), outs[a].at[my], local_sems.at[a]) for a in arrs]
        for cp in mine:
            cp.start()
        copies = []
        for k in range(1, N_DEV):
            px, py, pc = x ^ (k >> 2), y ^ ((k >> 1) & 1), c ^ (k & 1)
            for a in arrs:
                copies.append(pltpu.make_async_remote_copy(
                    src_ref=_dev_block(ps[a], 4 * px + 2 * py + pc, by_rows[a]), dst_ref=outs[a].at[my],
                    send_sem=send_sems.at[k - 1, a], recv_sem=recv_sems.at[k - 1, a],
                    device_id=(px, py, pc), device_id_type=pl.DeviceIdType.MESH))
        for cp in copies:
            cp.start()
        for cp in copies:
            cp.wait()
        for cp in mine:
            cp.wait()

    hbm = pl.BlockSpec(memory_space=pl.ANY)
    return pl.pallas_call(
        body, out_shape=out_shape, in_specs=[hbm] * n_arr, out_specs=[hbm] * n_arr,
        scratch_shapes=[pltpu.SemaphoreType.DMA((7, n_arr)), pltpu.SemaphoreType.DMA((7, n_arr)),
                        pltpu.SemaphoreType.DMA((n_arr,))],
        name=name)(*parts)


def _peer_copies(srcs, lands, send_sems, recv_sems, gather, by_rows):
    n_arr = len(srcs)
    x, y, c = _mesh_pos()
    my = 4 * x + 2 * y + c
    out = []
    for k in range(1, N_DEV):
        px, py, pc = x ^ (k >> 2), y ^ ((k >> 1) & 1), c ^ (k & 1)
        peer = 4 * px + 2 * py + pc
        for a in range(n_arr):
            src = srcs[a] if gather else _dev_block(srcs[a], peer, by_rows[a])
            dst = _dev_block(lands[a], my, by_rows[a]) if gather else lands[a].at[my]
            out.append(pltpu.make_async_remote_copy(
                src_ref=src, dst_ref=dst, send_sem=send_sems.at[(k - 1) * n_arr + a],
                recv_sem=recv_sems.at[(k - 1) * n_arr + a], device_id=(px, py, pc), device_id_type=pl.DeviceIdType.MESH))
    return out


def _land_shape(s, gather, by_rows):
    if gather:
        return (s.shape[0], N_DEV * s.shape[1], s.shape[2]) if by_rows else (N_DEV,) + s.shape
    return (N_DEV, s.shape[0], s.shape[1] // N_DEV, s.shape[2]) if by_rows else s.shape


_HBM = pl.BlockSpec(memory_space=pltpu.HBM)
_SEM = pl.BlockSpec(memory_space=pltpu.SEMAPHORE)


def _xfer_start(srcs, gather, by_rows, name, dep=None):
    n = len(srcs)
    lands = [lax.empty(_land_shape(s, gather, br), s.dtype) for s, br in zip(srcs, by_rows)]
    ins = [pltpu.with_memory_space_constraint(a, pltpu.HBM) for a in list(srcs) + lands]
    dspec, darg = _dep(dep)

    def body(*refs):
        s = 2 * n + len(darg)
        for cp in _peer_copies(refs[:n], refs[n:2 * n], refs[s], refs[s + 1], gather, by_rows):
            cp.start()
        refs[-1][...] = jnp.zeros_like(refs[-1])

    sems = pltpu.SemaphoreType.DMA(((N_DEV - 1) * n,))
    outs = pl.pallas_call(
        body, name=name,
        out_shape=(sems, sems, *[pltpu.HBM(a.shape, a.dtype) for a in ins], _sds((8, LANES), F32)),
        in_specs=[_HBM] * (2 * n) + dspec,
        out_specs=(_SEM, _SEM, *[_HBM] * (2 * n), pl.BlockSpec(memory_space=pltpu.VMEM)),
        input_output_aliases={i: 2 + i for i in range(2 * n)},
        compiler_params=pltpu.CompilerParams(has_side_effects=pltpu.SideEffectType.DATAFLOW_SIDE_EFFECTING))(*ins, *darg)
    return outs[0], outs[1], list(outs[2:-1]), outs[-1]


def _xfer_wait(started, after, gather, by_rows, name):
    send_sems, recv_sems, bufs, _ = started
    n = len(bufs) // 2

    def body(*refs):
        for cp in _peer_copies(refs[:n], refs[n:2 * n], refs[2 * n], refs[2 * n + 1], gather, by_rows):
            cp.wait_send()
            cp.wait_recv()

    outs = pl.pallas_call(
        body, name=name, out_shape=tuple(pltpu.HBM(a.shape, a.dtype) for a in bufs),
        in_specs=[_HBM] * (2 * n) + [_SEM, _SEM, pl.BlockSpec(memory_space=pl.ANY)], out_specs=tuple([_HBM] * (2 * n)),
        input_output_aliases={i: i for i in range(2 * n)},
        compiler_params=pltpu.CompilerParams(has_side_effects=pltpu.SideEffectType.DATAFLOW_SIDE_EFFECTING))(
            *bufs, send_sems, recv_sems, after)
    x, y, c = _mesh_pos()
    my = 4 * x + 2 * y + c
    res = []
    for src, land, br in zip(outs[:n], outs[n:], by_rows):
        zeros = (0,) * (land.ndim - 1)
        if gather and br:
            res.append(lax.dynamic_update_slice(land, src, (0, my * src.shape[1], 0)))
        elif gather:
            res.append(lax.dynamic_update_slice(land, src[None], (my,) + zeros))
        elif br:
            r = src.shape[1] // N_DEV
            own = lax.dynamic_slice(src, (0, my * r, 0), (src.shape[0], r, src.shape[2]))
            res.append(lax.dynamic_update_slice(land, own[None], (my,) + zeros))
        else:
            res.append(lax.dynamic_update_slice(land, lax.dynamic_index_in_dim(src, my, 0, keepdims=True), (my,) + zeros))
    return res


def _adam_update(g, w, m, v):
    c1 = 1.0 - ADAM_B1 ** ADAM_STEP
    c2 = 1.0 - ADAM_B2 ** ADAM_STEP
    nm = ADAM_B1 * m + (1.0 - ADAM_B1) * g
    nv = ADAM_B2 * v + (1.0 - ADAM_B2) * (g * g)
    return -ADAM_LR * ((nm / c1) / (jnp.sqrt(nv / c2) + ADAM_EPS) + ADAM_WD * w), nm, nv


def _adamw_body(p_ref, w_ref, m_ref, v_ref, g_ref, d_ref, nm_ref, nv_ref):
    g = p_ref[0]
    for i in range(1, N_DEV):
        g = g + p_ref[i]
    g_ref[...] = g
    d_ref[...], nm_ref[...], nv_ref[...] = _adam_update(g, w_ref[...], m_ref[...], v_ref[...])


def _adamw(parts, w, m, v, name, tr=1536):
    R = w.shape[0]
    tr = max(t for t in range(8, tr + 1, 8) if R % t == 0)

    def body(*refs):
        _adamw_body(*refs)

    row = pl.BlockSpec((tr, LANES), lambda i: (i, 0))
    return pl.pallas_call(
        body, grid=(R // tr,),
        in_specs=[pl.BlockSpec((N_DEV, tr, LANES), lambda i: (0, i, 0)), row, row, row],
        out_specs=[row, row, row, row], out_shape=[_sds((R, LANES), F32)] * 4,
        name=name, compiler_params=_cp(("parallel",)))(parts, w, m, v)


def _adamw_split(recvs, w, m, v, name, tr):
    depth, r, c = w.shape
    assert depth == len(recvs)
    tr = _tile(r, tr)

    def body(*refs):
        layer = pl.program_id(0)
        for ll in range(depth):
            @pl.when(layer == ll)
            def _(ll=ll):
                _adamw_body(refs[ll], *refs[depth:])

    wspec = pl.BlockSpec((None, tr, c), lambda l, i: (l, i, 0))
    rspecs = [pl.BlockSpec((N_DEV, None, tr, c), lambda l, i, ll=ll, t=t: (0, t, jnp.where(l == ll, i, 0), 0))
              for ll, (_, t) in enumerate(recvs)]
    return pl.pallas_call(
        body, grid=(depth, r // tr), in_specs=rspecs + [wspec, wspec, wspec],
        out_specs=[wspec] * 4, out_shape=[_sds(w.shape, F32)] * 4,
        name=name, compiler_params=_cp(("arbitrary", "arbitrary")))(*[a for a, _ in recvs], w, m, v)


def _merge_cols(g, name, tr=256):
    _, nt, K, n = g.shape
    tr = _tile(K, tr)

    def body(g_ref, o_ref):
        o_ref[...] = jnp.concatenate([g_ref[j] for j in range(N_DEV)], axis=1)

    return pl.pallas_call(
        body, grid=(nt, K // tr),
        in_specs=[pl.BlockSpec((N_DEV, None, tr, n), lambda t, i: (0, t, i, 0))],
        out_specs=pl.BlockSpec((None, tr, N_DEV * n), lambda t, i: (t, i, 0)),
        out_shape=_sds((nt, K, N_DEV * n), g.dtype), name=name, compiler_params=_cp(("parallel", "parallel")))(g)


def _merge_win(g, name, tr=256):
    _, nt, K, n = g.shape
    tr = _tile(K, tr)

    def body(g_ref, q_ref, r_ref):
        full = jnp.concatenate([g_ref[j] for j in range(N_DEV)], axis=1)
        q_ref[...] = full[:, 256:1792]
        zpad = jnp.zeros((tr, REST_W - 776), full.dtype)
        r_ref[...] = jnp.concatenate([full[:, 0:256], full[:, 1800:2312], full[:, 1792:1800], zpad], axis=1)

    return pl.pallas_call(
        body, grid=(nt, K // tr),
        in_specs=[pl.BlockSpec((N_DEV, None, tr, n), lambda t, i: (0, t, i, 0))],
        out_specs=[pl.BlockSpec((None, tr, 1536), lambda t, i: (t, i, 0)), pl.BlockSpec((None, tr, REST_W), lambda t, i: (t, i, 0))],
        out_shape=[_sds((nt, K, 1536), g.dtype), _sds((nt, K, REST_W), g.dtype)],
        name=name, compiler_params=_cp(("parallel", "parallel")))(g)


def _split_win(dq, dr, name, tr=256):
    K = dq.shape[0]
    tr = _tile(K, tr)
    n = (dq.shape[1] + 776) // N_DEV

    def body(q_ref, r_ref, o_ref):
        r = r_ref[...]
        full = jnp.concatenate([r[:, 0:256], q_ref[...], r[:, 768:776], r[:, 256:768]], axis=1)
        for j in range(N_DEV):
            o_ref[j] = full[:, n * j:n * (j + 1)]

    return pl.pallas_call(
        body, grid=(K // tr,),
        in_specs=[pl.BlockSpec((tr, dq.shape[1]), lambda i: (i, 0)), pl.BlockSpec((tr, REST_W), lambda i: (i, 0))],
        out_specs=pl.BlockSpec((N_DEV, tr, n), lambda i: (0, i, 0)),
        out_shape=_sds((N_DEV, K, n), F32), name=name, compiler_params=_cp(("parallel",)))(dq, dr)


WEIGHTS = ["ffn1_norm", "ffn1_w_gate", "ffn1_w_up", "ffn1_w_down", "mix_norm", "w_in", "pool_w", "pool_scale",
           "forget_bias", "conv_w", "conv_b", "conv_ln_g", "conv_ln_b", "w_out", "ffn2_norm", "ffn2_w_gate",
           "ffn2_w_up", "ffn2_w_down", "final_norm"]
FFN_COL = ["ffn1_w_gate", "ffn1_w_up", "ffn2_w_gate", "ffn2_w_up"]
FFN_ROW = ["ffn1_w_down", "ffn2_w_down"]
BIG = FFN_COL + FFN_ROW + ["w_in", "w_out"]
SMALL = [n for n in WEIGHTS if n not in BIG]


def _padded(n):
    return -(-n // PACK_ALIGN) * PACK_ALIGN


def _flat_pad(a):
    f = a.reshape(-1)
    return jnp.pad(f, (0, _padded(f.shape[0]) - f.shape[0]))


def _split8(a, axis):
    shp = a.shape
    a = a.reshape(shp[:axis] + (N_DEV, shp[axis] // N_DEV) + shp[axis + 1:])
    return jnp.moveaxis(a, axis, 0)


def _merge8(a, axis):
    a = jnp.moveaxis(a, 0, axis)
    shp = a.shape
    return a.reshape(shp[:axis] + (shp[axis] * shp[axis + 1],) + shp[axis + 2:])


def _pack_small(arrs):
    return jnp.concatenate([_flat_pad(arrs[n]) for n in SMALL]).reshape(-1, LANES)


def _pack_small_parts(grads):
    cols = []
    for n in SMALL:
        g = grads[n]
        if n == "conv_w":
            s = _split8(g, 2).reshape(N_DEV, -1)
        else:
            s = jnp.broadcast_to(g.reshape(1, -1), (N_DEV, g.size))
        cols.append(jnp.pad(s, ((0, 0), (0, _padded(s.shape[1]) - s.shape[1]))))
    return jnp.concatenate(cols, axis=1).reshape(N_DEV, -1, LANES)


def _unpack_small(buf, like):
    flat = buf.reshape(-1)
    out, off = {}, 0
    for n in SMALL:
        size = like[n].size
        out[n] = flat[off:off + size].reshape(like[n].shape)
        off += _padded(size)
    return out


class _Comm:
    def __init__(self, w):
        self.w = w
        self.bf = {n: w[n].astype(CDT) for n in BIG}
        self.ready = {}
        self.grads = {}
        self.recv = {}

    def _ffn_shards(self, l, which):
        cols = jnp.stack([self.bf[f"{which}_w_gate"][l], self.bf[f"{which}_w_up"][l]])
        return cols, self.bf[f"{which}_w_down"][l][None]

    def _put_ffn(self, l, which, cols, rows, t):
        self.ready[(l, which)] = dict(w_gate=cols[t], w_up=cols[t + 1], w_down=rows[t // 2])

    def weights_for(self, l, stage, x):
        bf = self.bf
        dep = None
        if (l, stage) == (0, "ffn1"):
            ga, gd = _all_gather(list(self._ffn_shards(0, "ffn1")), [False, True], "gather_l0_ffn1")
            self._put_ffn(0, "ffn1", _merge_cols(ga, "merge_l0_ffn1"), gd, 0)
            cols, rows = self._ffn_shards(0, "ffn2")
            self.started = _xfer_start([cols, rows, bf["w_in"][0:1], bf["w_out"][0:1], self.w["conv_w"]], True,
                                       [False, True, False, True, False], "gather_l0_start", dep=gd)
            dep = self.started[3]
        elif (l, stage) == (0, "mix"):
            ga, gd, gi, go, gc = _xfer_wait(self.started, x, True, [False, True, False, True, False], "gather_l0_wait")
            self._put_ffn(0, "ffn2", _merge_cols(ga, "merge_l0_ffn2"), gd, 0)
            q, r = _merge_win(gi, "merge_l0_w_in")
            self.conv_w = _merge8(gc, 2)
            self.ready[(0, "mix")] = dict(win_qkv=q[0], win_rest=r[0], w_out=go[0], conv_w=self.conv_w[0])
            c1, r1 = self._ffn_shards(1, "ffn1")
            c2, r2 = self._ffn_shards(1, "ffn2")
            self.started = _xfer_start([jnp.concatenate([c1, c2]), jnp.concatenate([r1, r2]), bf["w_in"][1:2], bf["w_out"][1:2]],
                                       True, [False, True, False, True], "gather_l1_start")
            dep = self.started[3]
        elif (l, stage) == (1, "ffn1"):
            ga, gd, gi, go = _xfer_wait(self.started, x, True, [False, True, False, True], "gather_l1_wait")
            cols = _merge_cols(ga, "merge_l1_ffn")
            self._put_ffn(1, "ffn1", cols, gd, 0)
            self._put_ffn(1, "ffn2", cols, gd, 2)
            q, r = _merge_win(gi, "merge_l1_w_in")
            self.ready[(1, "mix")] = dict(win_qkv=q[0], win_rest=r[0], w_out=go[0], conv_w=self.conv_w[1])
        return self.ready[(l, stage)], dep

    def grads_ready(self, l, stage, grads):
        for n, v in grads.items():
            self.grads[(l, n)] = v
        gr = self.grads
        four = [False, True, False, True]
        if (l, stage) == (1, "ffn1"):
            self.sent1 = _xfer_start(
                [jnp.stack([gr[(1, n)] for n in FFN_COL], axis=1), jnp.stack([gr[(1, n)] for n in FFN_ROW]),
                 gr[(1, "w_in")][:, None], gr[(1, "w_out")][None]], False, four, "grads_l1_start")
            return self.sent1[3]
        if (l, stage) == (0, "mix"):
            self.sent0 = _xfer_start(
                [jnp.stack([gr[(0, "ffn2_w_gate")], gr[(0, "ffn2_w_up")]], axis=1), gr[(0, "ffn2_w_down")][None],
                 gr[(0, "w_in")][:, None], gr[(0, "w_out")][None]], False, four, "grads_l0_start")
            return self.sent0[3]
        if (l, stage) == (0, "ffn1"):
            self.sent_last = _xfer_start(
                [jnp.stack([gr[(0, "ffn1_w_gate")], gr[(0, "ffn1_w_up")]], axis=1), gr[(0, "ffn1_w_down")][None]],
                False, [False, True], "grads_l0_ffn1_start")
            return self.sent_last[3]
        return None

    def finish(self, m, v, after):
        w, gr = self.w, self.grads
        depth = range(w["w_in"].shape[0])
        small = {n: (gr[(None, n)] if n == "final_norm" else jnp.stack([gr[(l, n)] for l in depth])) for n in SMALL}
        four = [False, True, False, True]
        c1, r1, i1, o1 = _xfer_wait(self.sent1, after, False, four, "grads_l1_wait")
        c2, r2, i0, o0 = _xfer_wait(self.sent0, after, False, four, "grads_l0_wait")

        def adam(n, recvs, tr):
            return _adamw_split(recvs, w[n], m[n], v[n], f"adamw_{n}", tr)

        res = {}
        res["ffn2_w_gate"] = adam("ffn2_w_gate", [(c2, 0), (c1, 2)], 256)
        res["ffn2_w_up"] = adam("ffn2_w_up", [(c2, 1), (c1, 3)], 256)
        res["ffn2_w_down"] = adam("ffn2_w_down", [(r2, 0), (r1, 1)], 176)
        res["w_in"] = adam("w_in", [(i0, 0), (i1, 0)], 256)
        res["w_out"] = adam("w_out", [(o0, 0), (o1, 0)], 128)
        rs, = _exchange([_pack_small_parts(small)], [False], "exchange_small")
        c0, r0 = _xfer_wait(self.sent_last, res["w_out"][0], False, [False, True], "grads_l0_ffn1_wait")
        res["ffn1_w_gate"] = adam("ffn1_w_gate", [(c0, 0), (c1, 0)], 256)
        res["ffn1_w_up"] = adam("ffn1_w_up", [(c0, 1), (c1, 1)], 256)
        res["ffn1_w_down"] = adam("ffn1_w_down", [(r0, 0), (r1, 0)], 176)
        packed = _adamw(rs, _pack_small(w), _pack_small(m), _pack_small(v), "adamw_small")
        unpacked = [_unpack_small(b, w) for b in packed]
        for n in SMALL:
            res[n] = [u[n] for u in unpacked]
        return res


def kernel(x, ffn1_norm, ffn1_w_gate, ffn1_w_up, ffn1_w_down, mix_norm, w_in, pool_w, pool_scale, forget_bias, conv_w, conv_b, conv_ln_g, conv_ln_b, w_out, ffn2_norm, ffn2_w_gate, ffn2_w_up, ffn2_w_down, final_norm, loss_target, m_ffn1_norm, m_ffn1_w_gate, m_ffn1_w_up, m_ffn1_w_down, m_mix_norm, m_w_in, m_pool_w, m_pool_scale, m_forget_bias, m_conv_w, m_conv_b, m_conv_ln_g, m_conv_ln_b, m_w_out, m_ffn2_norm, m_ffn2_w_gate, m_ffn2_w_up, m_ffn2_w_down, m_final_norm, v_ffn1_norm, v_ffn1_w_gate, v_ffn1_w_up, v_ffn1_w_down, v_mix_norm, v_w_in, v_pool_w, v_pool_scale, v_forget_bias, v_conv_w, v_conv_b, v_conv_ln_g, v_conv_ln_b, v_w_out, v_ffn2_norm, v_ffn2_w_gate, v_ffn2_w_up, v_ffn2_w_down, v_final_norm):
    w = dict(zip(WEIGHTS, (ffn1_norm, ffn1_w_gate, ffn1_w_up, ffn1_w_down, mix_norm, w_in, pool_w, pool_scale, forget_bias,
                           conv_w, conv_b, conv_ln_g, conv_ln_b, w_out, ffn2_norm, ffn2_w_gate, ffn2_w_up, ffn2_w_down,
                           final_norm)))
    m = dict(zip(WEIGHTS, (m_ffn1_norm, m_ffn1_w_gate, m_ffn1_w_up, m_ffn1_w_down, m_mix_norm, m_w_in, m_pool_w, m_pool_scale,
                           m_forget_bias, m_conv_w, m_conv_b, m_conv_ln_g, m_conv_ln_b, m_w_out, m_ffn2_norm, m_ffn2_w_gate,
                           m_ffn2_w_up, m_ffn2_w_down, m_final_norm)))
    v = dict(zip(WEIGHTS, (v_ffn1_norm, v_ffn1_w_gate, v_ffn1_w_up, v_ffn1_w_down, v_mix_norm, v_w_in, v_pool_w, v_pool_scale,
                           v_forget_bias, v_conv_w, v_conv_b, v_conv_ln_g, v_conv_ln_b, v_w_out, v_ffn2_norm, v_ffn2_w_gate,
                           v_ffn2_w_up, v_ffn2_w_down, v_final_norm)))
    comm = _Comm(w)
    loss_row, gx = _local_step(x[0], loss_target[0], w, comm.weights_for, comm.grads_ready)
    loss = lax.psum(loss_row[0, 0], ("x", "y", "c"))
    res = comm.finish(m, v, gx)
    return (loss, gx[None], *[res[n][i] for i in range(4) for n in WEIGHTS])
```

```python
import math

import numpy as np
import jax
import jax.numpy as jnp
from jax import lax
from jax.experimental import pallas as pl
from jax.experimental.pallas import tpu as pltpu

F32 = jnp.float32
CDT = jnp.bfloat16
NORM_EPS = 1e-6
N_DEV = 8
LANES = 128
PACK_ALIGN = 8 * LANES
VMEM_LIMIT = 48 * 1024 * 1024

POOL_WINDOWS = (2, 4, 8, 16)
POOL_HALO = 16
CONV_K = 31
CONV_HALO = 32
HEAD_DIM = 64
N_HEADS = 8
N_PAIRS = N_HEADS // 2
ATT_SCALE = 1.0 / math.sqrt(HEAD_DIM)
NEG = -1e30

ADAM_LR, ADAM_B1, ADAM_B2, ADAM_EPS, ADAM_WD, ADAM_STEP = 0.001, 0.9, 0.999, 1e-08, 0.01, 10

REST_W = 896
REST_Z_BLK = 6


def _cp(sem):
    return pltpu.CompilerParams(dimension_semantics=sem, vmem_limit_bytes=VMEM_LIMIT)


def _tile(n, pref):
    t = min(n, pref)
    assert n % t == 0, (n, pref)
    return t


def _sigmoid(x):
    return 1.0 / (1.0 + jnp.exp(-x))


def _sds(shape, dtype):
    return jax.ShapeDtypeStruct(shape, dtype)


_ANY = pl.BlockSpec(memory_space=pl.ANY)


def _dep(dep):
    return ([], []) if dep is None else ([_ANY], [dep])


def _rms_fwd(x, g, name, dep=None):
    T, D = x.shape
    tm = _tile(T, 1024)

    def body(x_ref, g_ref, *rest):
        o_ref = rest[-1]
        xv = x_ref[...]
        r = lax.rsqrt(jnp.mean(xv * xv, axis=-1, keepdims=True) + NORM_EPS)
        o_ref[...] = (xv * r * g_ref[...]).astype(o_ref.dtype)

    dspec, darg = _dep(dep)
    return pl.pallas_call(
        body, grid=(T // tm,),
        in_specs=[pl.BlockSpec((tm, D), lambda i: (i, 0)), pl.BlockSpec((1, D), lambda i: (0, 0))] + dspec,
        out_specs=pl.BlockSpec((tm, D), lambda i: (i, 0)),
        out_shape=_sds((T, D), CDT), name=name, compiler_params=_cp(("parallel",)))(x, g, *darg)


def _rms_bwd(x, g, dh, gres, name):
    T, D = x.shape
    tm = _tile(T, 512)

    def body(x_ref, g_ref, dh_ref, gres_ref, gin_ref, dg_ref):
        i = pl.program_id(0)
        xv = x_ref[...]
        d = dh_ref[...]
        r = lax.rsqrt(jnp.mean(xv * xv, axis=-1, keepdims=True) + NORM_EPS)
        xh = xv * r
        dxh = d * g_ref[...]
        c = jnp.mean(dxh * xh, axis=-1, keepdims=True)
        gin_ref[...] = gres_ref[...] + r * (dxh - xh * c)
        part = jnp.sum(d * xh, axis=0, keepdims=True)

        @pl.when(i == 0)
        def _():
            dg_ref[...] = part

        @pl.when(i > 0)
        def _():
            dg_ref[...] += part

    row = pl.BlockSpec((tm, D), lambda i: (i, 0))
    vec = pl.BlockSpec((1, D), lambda i: (0, 0))
    return pl.pallas_call(
        body, grid=(T // tm,), in_specs=[row, vec, row, row], out_specs=[row, vec],
        out_shape=[_sds((T, D), F32), _sds((1, D), F32)], name=name, compiler_params=_cp(("arbitrary",)))(x, g, dh, gres)


def _loss_bwd(x, g, target, name):
    T, D = x.shape
    tm = _tile(T, 512)

    def body(x_ref, g_ref, t_ref, loss_ref, dx_ref, dg_ref):
        i = pl.program_id(0)
        xv = x_ref[...]
        gv = g_ref[...]
        r = lax.rsqrt(jnp.mean(xv * xv, axis=-1, keepdims=True) + NORM_EPS)
        xh = xv * r
        err = xh * gv - t_ref[...]
        lpart = 0.5 * jnp.sum(jnp.mean(err * err, axis=-1, keepdims=True), axis=0, keepdims=True)
        dy = err * (1.0 / D)
        dxh = dy * gv
        c = jnp.mean(dxh * xh, axis=-1, keepdims=True)
        dx_ref[...] = r * (dxh - xh * c)
        part = jnp.sum(dy * xh, axis=0, keepdims=True)
        lrow = jnp.broadcast_to(lpart, (1, LANES))

        @pl.when(i == 0)
        def _():
            dg_ref[...] = part
            loss_ref[...] = lrow

        @pl.when(i > 0)
        def _():
            dg_ref[...] += part
            loss_ref[...] += lrow

    row = pl.BlockSpec((tm, D), lambda i: (i, 0))
    vec = pl.BlockSpec((1, D), lambda i: (0, 0))
    return pl.pallas_call(
        body, grid=(T // tm,), in_specs=[row, vec, row],
        out_specs=[pl.BlockSpec((1, LANES), lambda i: (0, 0)), row, vec],
        out_shape=[_sds((1, LANES), F32), _sds((T, D), F32), _sds((1, D), F32)],
        name=name, compiler_params=_cp(("arbitrary",)))(x, g, target)


def _mm(pairs, *, name, res=None, alpha=1.0, out_dtype=F32, tm=512, tn=None, dep=None):
    T = pairs[0][0].shape[0]
    N = pairs[0][1].shape[0] if pairs[0][2] else pairs[0][1].shape[1]
    tm = _tile(T, tm)
    tn = N if tn is None else _tile(N, tn)
    flags = [p[2] for p in pairs]
    n_in = 2 * len(pairs)

    def body(*refs):
        o_ref = refs[-1]
        acc = None
        for p, bt in enumerate(flags):
            a = refs[2 * p][...].astype(CDT)
            b = refs[2 * p + 1][...]
            dims = (((1,), (1,)), ((), ())) if bt else (((1,), (0,)), ((), ()))
            d = lax.dot_general(a, b, dims, preferred_element_type=F32)
            acc = d if acc is None else acc + d
        if alpha != 1.0:
            acc = acc * alpha
        if res is not None:
            acc = refs[n_in][...] + acc
        o_ref[...] = acc.astype(o_ref.dtype)

    in_specs, args = [], []
    for a, b, bt in pairs:
        K = a.shape[1]
        in_specs.append(pl.BlockSpec((tm, K), lambda i, j: (i, 0)))
        in_specs.append(pl.BlockSpec((tn, K), lambda i, j: (j, 0)) if bt else pl.BlockSpec((K, tn), lambda i, j: (0, j)))
        args += [a, b]
    if res is not None:
        in_specs.append(pl.BlockSpec((tm, tn), lambda i, j: (i, j)))
        args.append(res)
    dspec, darg = _dep(dep)
    in_specs += dspec
    args += darg
    return pl.pallas_call(
        body, grid=(T // tm, N // tn), in_specs=in_specs,
        out_specs=pl.BlockSpec((tm, tn), lambda i, j: (i, j)),
        out_shape=_sds((T, N), out_dtype), name=name, compiler_params=_cp(("parallel", "arbitrary")))(*args)


def _mm_norm_bwd(pairs, x, g, gres, *, name, tm=256, dep=None):
    T, D = x.shape
    tm = _tile(T, tm)
    n_in = 2 * len(pairs)

    def body(*refs):
        x_ref, g_ref, gres_ref = refs[n_in:n_in + 3]
        gin_ref, dg_ref = refs[-2:]
        i = pl.program_id(0)
        d = None
        for p in range(len(pairs)):
            part = lax.dot_general(refs[2 * p][...].astype(CDT), refs[2 * p + 1][...], (((1,), (1,)), ((), ())),
                                   preferred_element_type=F32)
            d = part if d is None else d + part
        xv = x_ref[...]
        r = lax.rsqrt(jnp.mean(xv * xv, axis=-1, keepdims=True) + NORM_EPS)
        xh = xv * r
        dxh = d * g_ref[...]
        c = jnp.mean(dxh * xh, axis=-1, keepdims=True)
        gin_ref[...] = gres_ref[...] + r * (dxh - xh * c)
        part = jnp.sum(d * xh, axis=0, keepdims=True)

        @pl.when(i == 0)
        def _():
            dg_ref[...] = part

        @pl.when(i > 0)
        def _():
            dg_ref[...] += part

    in_specs, args = [], []
    for a, b in pairs:
        K = a.shape[1]
        in_specs += [pl.BlockSpec((tm, K), lambda i: (i, 0)), pl.BlockSpec((D, K), lambda i: (0, 0))]
        args += [a, b]
    row = pl.BlockSpec((tm, D), lambda i: (i, 0))
    vec = pl.BlockSpec((1, D), lambda i: (0, 0))
    dspec, darg = _dep(dep)
    return pl.pallas_call(
        body, grid=(T // tm,), in_specs=in_specs + [row, vec, row] + dspec, out_specs=[row, vec],
        out_shape=[_sds((T, D), F32), _sds((1, D), F32)], name=name,
        compiler_params=_cp(("arbitrary",)))(*args, x, g, gres, *darg)


def _mm_tn(a, b, *, name, alpha=1.0, tk=512, dep=None):
    T, M = a.shape
    N = b.shape[1]
    tm = M if M <= 1024 else M // 2
    tn = N if N <= 1536 else N // 2
    assert M % tm == 0 and N % tn == 0 and tm % LANES == 0 and tn % LANES == 0
    tk = _tile(T, tk)
    nk = T // tk

    def body(a_ref, b_ref, *rest):
        o_ref = rest[-1]
        k = pl.program_id(2)
        d = lax.dot_general(a_ref[...].astype(CDT), b_ref[...].astype(CDT), (((0,), (0,)), ((), ())),
                            preferred_element_type=F32)

        @pl.when(k == 0)
        def _():
            o_ref[...] = d

        @pl.when(k > 0)
        def _():
            o_ref[...] += d

        if alpha != 1.0:
            @pl.when(k == nk - 1)
            def _():
                o_ref[...] *= alpha

    dspec, darg = _dep(dep)
    return pl.pallas_call(
        body, grid=(M // tm, N // tn, nk),
        in_specs=[pl.BlockSpec((tk, tm), lambda i, j, k: (k, i)), pl.BlockSpec((tk, tn), lambda i, j, k: (k, j))] + dspec,
        out_specs=pl.BlockSpec((tm, tn), lambda i, j, k: (i, j)),
        out_shape=_sds((M, N), F32), name=name, compiler_params=_cp(("parallel", "parallel", "arbitrary")))(a, b, *darg)


def _mm_tn_split(a, b, *, name, tk=512):
    T, M = a.shape
    N = b.shape[1]
    n = N // N_DEV
    parts = N_DEV // 2
    tn = parts * n
    assert M % LANES == 0 and N == N_DEV * n
    tk = _tile(T, tk)
    nk = T // tk

    def body(a_ref, b_ref, o_ref, acc):
        k = pl.program_id(1)
        d = lax.dot_general(a_ref[...].astype(CDT), b_ref[...].astype(CDT), (((0,), (0,)), ((), ())),
                            preferred_element_type=F32)

        @pl.when(k == 0)
        def _():
            acc[...] = d

        @pl.when(k > 0)
        def _():
            acc[...] += d

        @pl.when(k == nk - 1)
        def _():
            full = acc[...]
            for s in range(parts):
                o_ref[s] = full[:, s * n:(s + 1) * n]

    return pl.pallas_call(
        body, grid=(N // tn, nk),
        in_specs=[pl.BlockSpec((tk, M), lambda j, k: (k, 0)), pl.BlockSpec((tk, tn), lambda j, k: (k, j))],
        out_specs=pl.BlockSpec((parts, M, n), lambda j, k: (j, 0, 0)),
        out_shape=_sds((N_DEV, M, n), F32), scratch_shapes=[pltpu.VMEM((M, tn), F32)],
        name=name, compiler_params=_cp(("parallel", "arbitrary")))(a, b)


def _ffn_up(h, wg, wu, name):
    T, D = h.shape
    Fh = wg.shape[1]
    tm = _tile(T, 2048)
    tn = _tile(Fh, 256)

    def body(h_ref, wg_ref, wu_ref, a_ref, b_ref, s_ref):
        hv = h_ref[...]
        a = jnp.dot(hv, wg_ref[...], preferred_element_type=F32)
        b = jnp.dot(hv, wu_ref[...], preferred_element_type=F32)
        a_ref[...] = a.astype(a_ref.dtype)
        b_ref[...] = b.astype(b_ref.dtype)
        s_ref[...] = (a * _sigmoid(a) * b).astype(s_ref.dtype)

    wspec = pl.BlockSpec((D, tn), lambda i, j: (0, j))
    ospec = pl.BlockSpec((tm, tn), lambda i, j: (i, j))
    return pl.pallas_call(
        body, grid=(T // tm, Fh // tn),
        in_specs=[pl.BlockSpec((tm, D), lambda i, j: (i, 0)), wspec, wspec],
        out_specs=[ospec, ospec, ospec],
        out_shape=[_sds((T, Fh), CDT), _sds((T, Fh), CDT), _sds((T, Fh), CDT)],
        name=name, compiler_params=_cp(("parallel", "arbitrary")))(h, wg, wu)


def _ffn_bwd_ds(gout, wd, a, b, name, dep=None):
    T, D = gout.shape
    Fh = wd.shape[0]
    tm = _tile(T, 1024)
    tn = _tile(Fh, 256)

    def body(g_ref, wd_ref, a_ref, b_ref, *rest):
        da_ref, db_ref = rest[-2:]
        dy = (0.5 * g_ref[...]).astype(CDT)
        ds = lax.dot_general(dy, wd_ref[...], (((1,), (1,)), ((), ())), preferred_element_type=F32)
        av = a_ref[...].astype(F32)
        sg = _sigmoid(av)
        da_ref[...] = (ds * b_ref[...].astype(F32) * (sg * (1.0 + av * (1.0 - sg)))).astype(da_ref.dtype)
        db_ref[...] = (ds * (av * sg)).astype(db_ref.dtype)

    ospec = pl.BlockSpec((tm, tn), lambda i, j: (i, j))
    dspec, darg = _dep(dep)
    return pl.pallas_call(
        body, grid=(T // tm, Fh // tn),
        in_specs=[pl.BlockSpec((tm, D), lambda i, j: (i, 0)), pl.BlockSpec((tn, D), lambda i, j: (j, 0)), ospec, ospec] + dspec,
        out_specs=[ospec, ospec],
        out_shape=[_sds((T, Fh), CDT), _sds((T, Fh), CDT)],
        name=name, compiler_params=_cp(("parallel", "arbitrary")))(gout, wd, a, b, *darg)


def _ffn_fwd(x, gamma, wg, wu, wd, tag, dep=None):
    h = _rms_fwd(x, gamma, f"{tag}_norm", dep)
    a, b, s = _ffn_up(h, wg, wu, f"{tag}_up")
    y = _mm([(s, wd, False)], res=x, alpha=0.5, tn=512, name=f"{tag}_down")
    return y, (x, h, a, b, s)


def _ffn_bwd(saved, gamma, wg, wu, wd, gout, tag, dep, on_grads):
    x, h, a, b, s = saved
    dwd = _mm_tn(s, gout, alpha=0.5, name=f"{tag}_dwd", dep=dep)
    da, db = _ffn_bwd_ds(gout, wd, a, b, f"{tag}_bwd_ds", on_grads(dict(w_down=dwd)))
    dwg = _mm_tn_split(h, da, name=f"{tag}_dwg")
    dwu = _mm_tn_split(h, db, name=f"{tag}_dwu")
    dep = on_grads(dict(w_gate=dwg, w_up=dwu))
    return _mm_norm_bwd([(da, wg), (db, wu)], x, gamma, gout, name=f"{tag}_dh_norm_bwd", dep=dep)


def _fgate_fwd(rest, bias, name, bt=512):
    T = rest.shape[0]
    bt = _tile(T, bt)

    def body(z_ref, b_ref, fc_ref, ft_ref, carry):
        i = pl.program_id(0)

        @pl.when(i == 0)
        def _():
            carry[...] = jnp.zeros_like(carry)

        zb = z_ref[...] + b_ref[...]
        e = jnp.exp(-jnp.abs(zb))
        u = 1.0 + e
        log1p_e = jnp.where(u == 1.0, e, jnp.log(u) * (e / (u - 1.0)))
        x = jnp.minimum(zb, 0.0) - log1p_e
        row = lax.broadcasted_iota(jnp.int32, x.shape, 0)
        sh = 1
        while sh < bt:
            x = x + jnp.where(row >= sh, pltpu.roll(x, sh, 0), 0.0)
            sh *= 2
        f = x + carry[...]
        carry[...] = f[bt - 1:bt, :]
        fc_ref[...] = f
        ft_ref[...] = jnp.transpose(f)[0:N_HEADS, :]

    return pl.pallas_call(
        body, grid=(T // bt,),
        in_specs=[pl.BlockSpec((bt, LANES), lambda i: (i, REST_Z_BLK)), pl.BlockSpec((1, LANES), lambda i: (0, 0))],
        out_specs=[pl.BlockSpec((bt, LANES), lambda i: (i, 0)), pl.BlockSpec((N_HEADS, bt), lambda i: (0, i))],
        out_shape=[_sds((T, LANES), F32), _sds((N_HEADS, T), F32)],
        scratch_shapes=[pltpu.VMEM((1, LANES), F32)],
        name=name, compiler_params=_cp(("arbitrary",)))(rest, bias)


def _fgate_bwd(dfk, rest, bias, name, bt=512):
    T = rest.shape[0]
    bt = _tile(T, bt)
    nb = T // bt

    def body(df_ref, z_ref, b_ref, dz_ref, db_ref, carry):
        i = pl.program_id(0)

        @pl.when(i == 0)
        def _():
            carry[...] = jnp.zeros_like(carry)

        dfv = df_ref[...]
        lane = lax.broadcasted_iota(jnp.int32, (bt, LANES), 1)
        x = jnp.zeros((bt, LANES), F32)
        for h in range(N_HEADS):
            x = jnp.where(lane == h, dfv[:, HEAD_DIM * h:HEAD_DIM * h + 1], x)
        row = lax.broadcasted_iota(jnp.int32, x.shape, 0)
        sh = 1
        while sh < bt:
            x = x + jnp.where(row + sh < bt, pltpu.roll(x, bt - sh, 0), 0.0)
            sh *= 2
        dlf = x + carry[...]
        carry[...] = dlf[0:1, :]
        zb = z_ref[...] + b_ref[...]
        dz = jnp.where(lane < N_HEADS, dlf * _sigmoid(-zb), 0.0)
        dz_ref[...] = dz.astype(dz_ref.dtype)
        part = jnp.sum(dz, axis=0, keepdims=True)

        @pl.when(i == 0)
        def _():
            db_ref[...] = part

        @pl.when(i > 0)
        def _():
            db_ref[...] += part

    return pl.pallas_call(
        body, grid=(nb,),
        in_specs=[pl.BlockSpec((bt, 4 * LANES), lambda i: (nb - 1 - i, 0)),
                  pl.BlockSpec((bt, LANES), lambda i: (nb - 1 - i, REST_Z_BLK)),
                  pl.BlockSpec((1, LANES), lambda i: (0, 0))],
        out_specs=[pl.BlockSpec((bt, LANES), lambda i: (nb - 1 - i, 0)), pl.BlockSpec((1, LANES), lambda i: (0, 0))],
        out_shape=[_sds((T, LANES), CDT), _sds((1, LANES), F32)],
        scratch_shapes=[pltpu.VMEM((1, LANES), F32)],
        name=name, compiler_params=_cp(("arbitrary",)))(dfk, rest, bias)


def _by_group(vals, lane):
    out = vals[-1]
    for g in range(len(vals) - 2, -1, -1):
        out = jnp.where(lane // 64 == g, vals[g], out)
    return out


def _pool_counts(t0, n, lane):
    t = t0 + lax.broadcasted_iota(jnp.int32, (n, 256), 0)
    return _by_group([jnp.minimum(t + 1, w) for w in POOL_WINDOWS], lane).astype(F32)


def _pooled(u, halo, i, bt):
    lane = lax.broadcasted_iota(jnp.int32, (bt, 256), 1)
    ext = jnp.concatenate([jnp.where(i > 0, halo, 0.0), u], axis=0)
    sums, s, sh = [], ext, 1
    for _ in POOL_WINDOWS:
        s = s + pltpu.roll(s, sh, 0)
        sums.append(s[POOL_HALO:, :])
        sh *= 2
    return _by_group(sums, lane) / _pool_counts(i * bt, bt, lane) - u


def _pool_fwd(rest, wbd, scale, name, bt=512):
    T = rest.shape[0]
    bt = _tile(T, bt)
    hb = bt // POOL_HALO

    def body(u_ref, halo_ref, w_ref, sc_ref, o_ref):
        i = pl.program_id(0)
        pooled = _pooled(u_ref[...], halo_ref[...], i, bt)
        mixed = jnp.dot(pooled.astype(CDT), w_ref[...], preferred_element_type=F32)
        o_ref[...] = (mixed * sc_ref[...]).astype(o_ref.dtype)

    return pl.pallas_call(
        body, grid=(T // bt,),
        in_specs=[pl.BlockSpec((bt, 256), lambda i: (i, 0)),
                  pl.BlockSpec((POOL_HALO, 256), lambda i: (jnp.maximum(i * hb - 1, 0), 0)),
                  pl.BlockSpec((256, 256), lambda i: (0, 0)), pl.BlockSpec((1, 256), lambda i: (0, 0))],
        out_specs=pl.BlockSpec((bt, 256), lambda i: (i, 0)),
        out_shape=_sds((T, 256), CDT), name=name, compiler_params=_cp(("parallel",)))(rest, rest, wbd, scale)


def _pool_bwd(dcat, rest, wbd, scale, name, bt=512):
    T = rest.shape[0]
    bt = _tile(T, bt)
    hb = bt // POOL_HALO
    nb = T // bt
    n = bt + POOL_HALO

    def body(dy_ref, dyn_ref, u_ref, halo_ref, w_ref, sc_ref, du_ref, dw_ref, dsc_ref):
        i = pl.program_id(0)
        lane = lax.broadcasted_iota(jnp.int32, (bt, 256), 1)
        w = w_ref[...]
        sc = sc_ref[...]
        pooled = _pooled(u_ref[...], halo_ref[...], i, bt)
        pooled_c = pooled.astype(CDT)
        mixed = jnp.dot(pooled_c, w, preferred_element_type=F32)
        dy = dy_ref[...]
        dm = (dy * sc).astype(CDT)
        dsc = jnp.sum(dy * mixed, axis=0, keepdims=True)
        dw = lax.dot_general(pooled_c, dm, (((0,), (0,)), ((), ())), preferred_element_type=F32)
        nt = (((1,), (1,)), ((), ()))
        dpl = lax.dot_general(dm, w, nt, preferred_element_type=F32)
        dmn = (jnp.where(i < nb - 1, dyn_ref[...], 0.0) * sc).astype(CDT)
        dpln = lax.dot_general(dmn, w, nt, preferred_element_type=F32)
        lane_h = lax.broadcasted_iota(jnp.int32, (POOL_HALO, 256), 1)
        ext = jnp.concatenate([dpl / _pool_counts(i * bt, bt, lane),
                               dpln / _pool_counts((i + 1) * bt, POOL_HALO, lane_h)], axis=0)
        sums, s, sh = [], ext, 1
        for _ in POOL_WINDOWS:
            s = s + pltpu.roll(s, n - sh, 0)
            sums.append(s[0:bt, :])
            sh *= 2
        du_ref[...] = (_by_group(sums, lane) - dpl).astype(du_ref.dtype)

        @pl.when(i == 0)
        def _():
            dw_ref[...] = dw
            dsc_ref[...] = dsc

        @pl.when(i > 0)
        def _():
            dw_ref[...] += dw
            dsc_ref[...] += dsc

    full = pl.BlockSpec((256, 256), lambda i: (0, 0))
    vec = pl.BlockSpec((1, 256), lambda i: (0, 0))
    return pl.pallas_call(
        body, grid=(nb,),
        in_specs=[pl.BlockSpec((bt, 256), lambda i: (i, 0)),
                  pl.BlockSpec((POOL_HALO, 256), lambda i: (jnp.minimum((i + 1) * hb, nb * hb - 1), 0)),
                  pl.BlockSpec((bt, 256), lambda i: (i, 0)),
                  pl.BlockSpec((POOL_HALO, 256), lambda i: (jnp.maximum(i * hb - 1, 0), 0)),
                  full, vec],
        out_specs=[pl.BlockSpec((bt, 256), lambda i: (i, 0)), full, vec],
        out_shape=[_sds((T, 256), CDT), _sds((256, 256), F32), _sds((1, 256), F32)],
        name=name, compiler_params=_cp(("arbitrary",)))(dcat, dcat, rest, rest, wbd, scale)


def _glu_ext(a_ref, g_ref, ah_ref, gh_ref, i):
    u = a_ref[...] * _sigmoid(g_ref[...])
    uh = jnp.where(i > 0, ah_ref[...] * _sigmoid(gh_ref[...]), 0.0)
    return jnp.concatenate([uh, u], axis=0)


def _conv_fwd(rest, cw, cb, lg, lb, name, bt=512):
    T = rest.shape[0]
    bt = _tile(T, bt)
    hb = bt // CONV_HALO

    def body(a_ref, g_ref, ah_ref, gh_ref, cw_ref, cb_ref, lg_ref, lb_ref, o_ref, y_ref):
        i = pl.program_id(0)
        ext = _glu_ext(a_ref, g_ref, ah_ref, gh_ref, i)
        w = cw_ref[...]
        acc = w[CONV_K - 1:CONV_K, :] * ext
        for k in range(CONV_K - 1):
            acc = acc + w[k:k + 1, :] * pltpu.roll(ext, CONV_K - 1 - k, 0)
        y = acc[CONV_HALO:, :] + cb_ref[...]
        y_ref[...] = y
        yc = y - jnp.mean(y, axis=-1, keepdims=True)
        yn = yc * lax.rsqrt(jnp.mean(yc * yc, axis=-1, keepdims=True) + NORM_EPS)
        z = yn * lg_ref[...] + lb_ref[...]
        o_ref[...] = (z * _sigmoid(z)).astype(o_ref.dtype)

    def cur(c):
        return pl.BlockSpec((bt, 256), lambda i: (i, c))

    def prev(c):
        return pl.BlockSpec((CONV_HALO, 256), lambda i: (jnp.maximum(i * hb - 1, 0), c))

    vec = pl.BlockSpec((1, 256), lambda i: (0, 0))
    return pl.pallas_call(
        body, grid=(T // bt,),
        in_specs=[cur(1), cur(2), prev(1), prev(2), pl.BlockSpec((CONV_HALO, 256), lambda i: (0, 0)), vec, vec, vec],
        out_specs=[pl.BlockSpec((bt, 256), lambda i: (i, 0)), pl.BlockSpec((bt, 256), lambda i: (i, 0))],
        out_shape=[_sds((T, 256), CDT), _sds((T, 256), F32)],
        name=name, compiler_params=_cp(("parallel",)))(rest, rest, rest, rest, cw, cb, lg, lb)


def _conv_bwd(dcat, yconv, rest, cw, lg, lb, name, bt=512):
    T = rest.shape[0]
    bt = _tile(T, bt)
    hb = bt // CONV_HALO
    nb = T // bt
    n = bt + CONV_HALO

    def body(dy_ref, dyn_ref, y_ref, yn_ref, a_ref, g_ref, ah_ref, gh_ref, cw_ref, lg_ref, lb_ref,
             da_ref, dg_ref, dcw_ref, dcb_ref, dlg_ref, dlb_ref):
        i = pl.program_id(0)
        lgv = lg_ref[...]
        lbv = lb_ref[...]

        def ln_swish_bwd(dout, y):
            yc = y - jnp.mean(y, axis=-1, keepdims=True)
            rs = lax.rsqrt(jnp.mean(yc * yc, axis=-1, keepdims=True) + NORM_EPS)
            yn = yc * rs
            z = yn * lgv + lbv
            sg = _sigmoid(z)
            dz = dout * (sg * (1.0 + z * (1.0 - sg)))
            dyn = dz * lgv
            dyc = rs * (dyn - jnp.mean(dyn, axis=-1, keepdims=True) - yn * jnp.mean(dyn * yn, axis=-1, keepdims=True))
            return dyc, dz, yn

        dyc, dz, yn = ln_swish_bwd(dy_ref[...], y_ref[...])
        dyc_next, _, _ = ln_swish_bwd(dyn_ref[...], yn_ref[...])
        dyc_next = jnp.where(i < nb - 1, dyc_next, 0.0)
        ext_u = _glu_ext(a_ref, g_ref, ah_ref, gh_ref, i)
        ext_d = jnp.concatenate([dyc, dyc_next], axis=0)
        w = cw_ref[...]
        du = w[CONV_K - 1:CONV_K, :] * ext_d
        rows = []
        for k in range(CONV_K):
            s = CONV_K - 1 - k
            if s > 0:
                du = du + w[k:k + 1, :] * pltpu.roll(ext_d, n - s, 0)
                us = pltpu.roll(ext_u, s, 0)[CONV_HALO:, :]
            else:
                us = ext_u[CONV_HALO:, :]
            rows.append(jnp.sum(dyc * us, axis=0, keepdims=True))
        rows.append(jnp.zeros((1, 256), F32))
        dcw = jnp.concatenate(rows, axis=0)
        du = du[0:bt, :]
        av = a_ref[...]
        sg = _sigmoid(g_ref[...])
        da_ref[...] = (du * sg).astype(da_ref.dtype)
        dg_ref[...] = (du * av * (sg * (1.0 - sg))).astype(dg_ref.dtype)
        dcb = jnp.sum(dyc, axis=0, keepdims=True)
        dlg = jnp.sum(dz * yn, axis=0, keepdims=True)
        dlb = jnp.sum(dz, axis=0, keepdims=True)

        @pl.when(i == 0)
        def _():
            dcw_ref[...] = dcw
            dcb_ref[...] = dcb
            dlg_ref[...] = dlg
            dlb_ref[...] = dlb

        @pl.when(i > 0)
        def _():
            dcw_ref[...] += dcw
            dcb_ref[...] += dcb
            dlg_ref[...] += dlg
            dlb_ref[...] += dlb

    def cur(c):
        return pl.BlockSpec((bt, 256), lambda i: (i, c))

    def prev(c):
        return pl.BlockSpec((CONV_HALO, 256), lambda i: (jnp.maximum(i * hb - 1, 0), c))

    def nxt(c):
        return pl.BlockSpec((CONV_HALO, 256), lambda i: (jnp.minimum((i + 1) * hb, nb * hb - 1), c))

    vec = pl.BlockSpec((1, 256), lambda i: (0, 0))
    wfull = pl.BlockSpec((CONV_HALO, 256), lambda i: (0, 0))
    return pl.pallas_call(
        body, grid=(nb,),
        in_specs=[cur(3), nxt(3), cur(0), nxt(0), cur(1), cur(2), prev(1), prev(2), wfull, vec, vec],
        out_specs=[cur(0), cur(0), wfull, vec, vec, vec],
        out_shape=[_sds((T, 256), CDT), _sds((T, 256), CDT), _sds((CONV_HALO, 256), F32),
                   _sds((1, 256), F32), _sds((1, 256), F32), _sds((1, 256), F32)],
        name=name, compiler_params=_cp(("arbitrary",)))(dcat, dcat, yconv, yconv, rest, rest, rest, rest, cw, lg, lb)


def _half_mask(shape, a):
    lane = lax.broadcasted_iota(jnp.int32, shape, 1)
    return (lane // HEAD_DIM) == a


def _attn_fwd(qkv, fcol, frow, name, blk=512):
    T = qkv.shape[0]
    blk = _tile(T, blk)
    nq = T // blk
    nt = (((1,), (1,)), ((), ()))

    def body(q_ref, k_ref, v_ref, fc_ref, fr_ref, o_ref, lse_ref):
        p_id = pl.program_id(0)
        i = pl.program_id(1)
        q2 = q_ref[...]
        fc = fc_ref[...]
        lane = lax.broadcasted_iota(jnp.int32, (blk, LANES), 1)
        tri = lax.broadcasted_iota(jnp.int32, (blk, blk), 1) <= lax.broadcasted_iota(jnp.int32, (blk, blk), 0)
        masks = [_half_mask(q2.shape, a) for a in range(2)]
        qs = [jnp.where(hm, q2, jnp.zeros_like(q2)) * ATT_SCALE for hm in masks]
        fqs = [jnp.sum(jnp.where(lane == 2 * p_id + a, fc, 0.0), axis=1, keepdims=True) for a in range(2)]

        def tile(j, carry, masked):
            cols = pl.ds(pl.multiple_of(j * blk, blk), blk)
            kj = k_ref[cols, :]
            vj = v_ref[cols, :]
            out = []
            for a in range(2):
                m, acc = carry[2 * a:2 * a + 2]
                va = jnp.where(masks[a], vj, jnp.ones_like(vj))
                s = lax.dot_general(qs[a], kj, nt, preferred_element_type=F32) + (fqs[a] - fr_ref[a:a + 1, cols])
                if masked:
                    s = jnp.where(tri, s, NEG)
                m_new = jnp.maximum(m, jnp.max(s, axis=1, keepdims=True))
                alpha = jnp.exp(m - m_new)
                pr = jnp.exp(s - m_new)
                hi = lax.bitcast_convert_type(lax.bitcast_convert_type(pr, jnp.uint32) & jnp.uint32(0xFFFF0000), F32)
                pv = (jnp.dot(hi.astype(CDT), va, preferred_element_type=F32)
                      + jnp.dot((pr - hi).astype(CDT), va, preferred_element_type=F32))
                out += [m_new, alpha * acc + pv]
            return tuple(out)

        init = (jnp.full((blk, 1), NEG, F32), jnp.zeros((blk, LANES), F32)) * 2
        carry = lax.fori_loop(0, i, lambda j, c: tile(j, c, False), init)
        carry = tile(i, carry, True)
        ls = [carry[1][:, HEAD_DIM:HEAD_DIM + 1], carry[3][:, 0:1]]
        lo = lane < HEAD_DIM
        o_ref[...] = jnp.where(lo, carry[1] / ls[0], carry[3] / ls[1])
        lse_t = jnp.transpose(jnp.where(lo, carry[0] + jnp.log(ls[0]), carry[2] + jnp.log(ls[1])))
        lse_ref[...] = jnp.concatenate([lse_t[0:1, :], lse_t[HEAD_DIM:HEAD_DIM + 1, :]], axis=0)

    return pl.pallas_call(
        body, grid=(N_PAIRS, nq),
        in_specs=[pl.BlockSpec((blk, LANES), lambda p, i: (i, p)),
                  pl.BlockSpec((T, LANES), lambda p, i: (0, N_PAIRS + p)),
                  pl.BlockSpec((T, LANES), lambda p, i: (0, 2 * N_PAIRS + p)),
                  pl.BlockSpec((blk, LANES), lambda p, i: (i, 0)),
                  pl.BlockSpec((None, 2, T), lambda p, i: (p, 0, 0))],
        out_specs=[pl.BlockSpec((blk, LANES), lambda p, i: (i, p)), pl.BlockSpec((None, 2, blk), lambda p, i: (p, 0, i))],
        out_shape=[_sds((T, N_PAIRS * LANES), F32), _sds((N_PAIRS, 2, T), F32)],
        name=name, compiler_params=_cp(("parallel", "arbitrary")))(qkv, qkv, qkv, fcol, frow)


def _attn_delta(dcat, o, name, blk=512):
    T = o.shape[0]
    blk = _tile(T, blk)

    def body(d_ref, o_ref, out_ref):
        prod = d_ref[:, 256:768].astype(CDT).astype(F32) * o_ref[...]
        pt = jnp.transpose(prod)
        out_ref[...] = jnp.sum(pt.reshape(N_HEADS, HEAD_DIM, blk), axis=1)

    return pl.pallas_call(
        body, grid=(T // blk,),
        in_specs=[pl.BlockSpec((blk, 1024), lambda i: (i, 0)), pl.BlockSpec((blk, 512), lambda i: (i, 0))],
        out_specs=pl.BlockSpec((N_HEADS, blk), lambda i: (0, i)),
        out_shape=_sds((N_HEADS, T), F32), name=name, compiler_params=_cp(("parallel",)))(dcat, o)


def _attn_bwd(qkv, dcat, fcol, frow, lse, delta, name, blk=512):
    T = qkv.shape[0]
    blk = _tile(T, blk)
    nq = T // blk
    nt = (((1,), (1,)), ((), ()))

    def body(q_ref, do_ref, k_ref, v_ref, fc_ref, fr_ref, lse_ref, dl_ref, dqt_ref, dk_ref, dv_ref, df_ref):
        p_id = pl.program_id(0)
        j = pl.program_id(1)

        @pl.when(j == 0)
        def _():
            dqt_ref[...] = jnp.zeros_like(dqt_ref)

        k2 = k_ref[...]
        v2 = v_ref[...]
        fc = fc_ref[...]
        lane = lax.broadcasted_iota(jnp.int32, (blk, LANES), 1)
        tri = lax.broadcasted_iota(jnp.int32, (blk, blk), 0) <= lax.broadcasted_iota(jnp.int32, (blk, blk), 1)
        masks = [_half_mask(k2.shape, a) for a in range(2)]
        kas = [jnp.where(hm, k2, jnp.zeros_like(k2)) * ATT_SCALE for hm in masks]
        kats = [jnp.transpose(ka) for ka in kas]
        vas = [jnp.where(hm, v2, jnp.zeros_like(v2)) for hm in masks]
        fks = [jnp.sum(jnp.where(lane == 2 * p_id + a, fc, 0.0), axis=1, keepdims=True) for a in range(2)]

        def tile(i, carry, masked):
            rows = pl.ds(pl.multiple_of(i * blk, blk), blk)
            qi = q_ref[rows, :]
            doi = do_ref[rows, :].astype(CDT)
            out = []
            dqt = None
            for a in range(2):
                dk_acc, dv_acc, df_acc = carry[3 * a:3 * a + 3]
                st = lax.dot_general(kas[a], qi, nt, preferred_element_type=F32)
                e = (st + (fr_ref[a:a + 1, rows] - fks[a])) - lse_ref[a:a + 1, rows]
                if masked:
                    e = jnp.where(tri, e, NEG)
                pt = jnp.exp(e)
                dpt = lax.dot_general(vas[a], doi, nt, preferred_element_type=F32)
                ds32 = pt * (dpt - dl_ref[a:a + 1, rows])
                dst = ds32.astype(CDT)
                df_acc = df_acc + jnp.sum(ds32, axis=1, keepdims=True)
                dv_acc = dv_acc + jnp.dot(pt.astype(CDT), doi, preferred_element_type=F32)
                dk_acc = dk_acc + jnp.dot(dst, qi, preferred_element_type=F32)
                part = jnp.dot(kats[a], dst, preferred_element_type=F32)
                dqt = part if dqt is None else dqt + part
                out += [dk_acc, dv_acc, df_acc]
            dqt_ref[:, rows] += dqt
            return tuple(out)

        init = (jnp.zeros((blk, LANES), F32), jnp.zeros((blk, LANES), F32), jnp.zeros((blk, 1), F32)) * 2
        carry = tile(j, init, True)
        carry = lax.fori_loop(j + 1, nq, lambda i, c: tile(i, c, False), carry)
        lo = lane < HEAD_DIM
        dk_ref[...] = (jnp.where(lo, carry[0], carry[3]) * ATT_SCALE).astype(dk_ref.dtype)
        dv_ref[...] = jnp.where(lo, carry[1], carry[4]).astype(dv_ref.dtype)
        df_ref[...] = -jnp.where(lo, carry[2], carry[5])

    res = pl.BlockSpec((T, LANES), lambda p, j: (0, p))
    rows = pl.BlockSpec((None, 2, T), lambda p, j: (p, 0, 0))
    kv_out = pl.BlockSpec((blk, LANES), lambda p, j: (j, p))
    return pl.pallas_call(
        body, grid=(N_PAIRS, nq),
        in_specs=[res, pl.BlockSpec((T, LANES), lambda p, j: (0, 2 + p)),
                  pl.BlockSpec((blk, LANES), lambda p, j: (j, N_PAIRS + p)),
                  pl.BlockSpec((blk, LANES), lambda p, j: (j, 2 * N_PAIRS + p)),
                  pl.BlockSpec((blk, LANES), lambda p, j: (j, 0)), rows, rows, rows],
        out_specs=[pl.BlockSpec((LANES, T), lambda p, j: (p, 0)), kv_out, kv_out, kv_out],
        out_shape=[_sds((N_PAIRS * LANES, T), F32), _sds((T, N_PAIRS * LANES), CDT), _sds((T, N_PAIRS * LANES), CDT),
                   _sds((T, N_PAIRS * LANES), F32)],
        name=name, compiler_params=_cp(("parallel", "arbitrary")))(qkv, dcat, qkv, qkv, fcol, frow, lse, delta)


def _mixer_fwd(x, wts, tag, dep=None):
    T = x.shape[0]
    h = _rms_fwd(x, wts["mix_norm"], f"{tag}_norm", dep)
    qkv = _mm([(h, wts["win_qkv"], False)], out_dtype=CDT, tm=1024, tn=768, name=f"{tag}_in_qkv")
    rest = _mm([(h, wts["win_rest"], False)], tm=1024, name=f"{tag}_in_rest")
    fcol, frow8 = _fgate_fwd(rest, wts["fbias"], f"{tag}_fgate")
    frow = frow8.reshape(N_PAIRS, 2, T)
    ya = _pool_fwd(rest, wts["pool_wbd"], wts["pool_scale"], f"{tag}_pool")
    o, lse = _attn_fwd(qkv, fcol, frow, f"{tag}_attn")
    yc, yconv = _conv_fwd(rest, wts["conv_w"], wts["conv_b"], wts["conv_ln_g"], wts["conv_ln_b"], f"{tag}_conv")
    cat = jnp.concatenate([ya, o.astype(CDT), yc], axis=1)
    y = _mm([(cat, wts["w_out"], False)], res=x, tn=512, name=f"{tag}_out")
    return y, (x, h, qkv, rest, fcol, frow, o, lse, yconv, cat)


def _mixer_bwd(saved, wts, gout, tag, dep=None):
    x, h, qkv, rest, fcol, frow, o, lse, yconv, cat = saved
    T = x.shape[0]
    dcat = _mm([(gout, wts["w_out"], True)], tn=512, name=f"{tag}_dcat", dep=dep)
    dwout = _mm_tn(cat, gout, name=f"{tag}_dwout")
    du, dpw, dpsc = _pool_bwd(dcat, rest, wts["pool_wbd"], wts["pool_scale"], f"{tag}_pool_bwd")
    delta = _attn_delta(dcat, o, f"{tag}_attn_delta").reshape(N_PAIRS, 2, T)
    dqt, dk, dv, dfk = _attn_bwd(qkv, dcat, fcol, frow, lse, delta, f"{tag}_attn_bwd")
    dq = dqt.T.astype(CDT)
    dz, dfb = _fgate_bwd(dfk, rest, wts["fbias"], f"{tag}_fgate_bwd")
    da, dg, dcw, dcb, dlg, dlb = _conv_bwd(dcat, yconv, rest, wts["conv_w"], wts["conv_ln_g"], wts["conv_ln_b"],
                                           f"{tag}_conv_bwd")
    dp_qkv = jnp.concatenate([dq, dk, dv], axis=1).astype(CDT)
    dp_rest = jnp.concatenate([du, da, dg, dz], axis=1)
    dwin_qkv = _mm_tn(h, dp_qkv, name=f"{tag}_dwin_qkv")
    dwin_rest = _mm_tn(h, dp_rest, name=f"{tag}_dwin_rest")
    gin, dgamma = _mm_norm_bwd([(dp_qkv, wts["win_qkv"]), (dp_rest, wts["win_rest"])], x, wts["mix_norm"], gout,
                               name=f"{tag}_dh_norm_bwd")
    dwin = _split_win(dwin_qkv, dwin_rest, f"{tag}_dwin_split")
    dpool_w = jnp.stack([dpw[64 * g:64 * g + 64, 64 * g:64 * g + 64] for g in range(4)])
    grads = dict(mix_norm=dgamma[0], w_in=dwin, pool_w=dpool_w, pool_scale=dpsc[0], forget_bias=dfb[0, 0:N_HEADS],
                 conv_w=dcw[0:CONV_K], conv_b=dcb[0], conv_ln_g=dlg[0], conv_ln_b=dlb[0], w_out=dwout)
    return gin, grads


def _rep_layer(rep, l):
    pw = rep["pool_w"][l].astype(CDT)
    wbd = jnp.zeros((256, 256), CDT)
    for g in range(4):
        wbd = lax.dynamic_update_slice(wbd, pw[g], (64 * g, 64 * g))
    return dict(
        ffn1_norm=rep["ffn1_norm"][l][None], ffn2_norm=rep["ffn2_norm"][l][None], mix_norm=rep["mix_norm"][l][None],
        fbias=jnp.pad(rep["forget_bias"][l], (0, LANES - N_HEADS))[None],
        pool_wbd=wbd, pool_scale=rep["pool_scale"][l][None], conv_b=rep["conv_b"][l][None],
        conv_ln_g=rep["conv_ln_g"][l][None], conv_ln_b=rep["conv_ln_b"][l][None])


def _local_step(x, target, rep, weights_for, grads_ready):
    depth = rep["ffn1_norm"].shape[0]
    kept = []
    for l in range(depth):
        r = _rep_layer(rep, l)
        w1, dep = weights_for(l, "ffn1", x)
        x, s1 = _ffn_fwd(x, r["ffn1_norm"], w1["w_gate"], w1["w_up"], w1["w_down"], f"l{l}_ffn1", dep)
        wm, dep = weights_for(l, "mix", x)
        wm = dict(r, win_qkv=wm["win_qkv"], win_rest=wm["win_rest"], w_out=wm["w_out"],
                  conv_w=jnp.pad(wm["conv_w"], ((0, CONV_HALO - CONV_K), (0, 0))))
        x, s2 = _mixer_fwd(x, wm, f"l{l}_mix", dep)
        w2, dep = weights_for(l, "ffn2", x)
        x, s3 = _ffn_fwd(x, r["ffn2_norm"], w2["w_gate"], w2["w_up"], w2["w_down"], f"l{l}_ffn2", dep)
        kept.append((r, w1, wm, w2, s1, s2, s3))
    loss, g, dfinal = _loss_bwd(x, rep["final_norm"][None], target, "loss_head")
    dep = grads_ready(None, "final", dict(final_norm=dfinal[0]))
    for l in reversed(range(depth)):
        r, w1, wm, w2, s1, s2, s3 = kept[l]

        def ffn_grads(which, l=l):
            return lambda gr: grads_ready(l, which, {f"{which}_{k}": v for k, v in gr.items()})

        g, dn = _ffn_bwd(s3, r["ffn2_norm"], w2["w_gate"], w2["w_up"], w2["w_down"], g, f"l{l}_ffn2", dep, ffn_grads("ffn2"))
        grads_ready(l, "norm", dict(ffn2_norm=dn[0]))
        g, gm = _mixer_bwd(s2, wm, g, f"l{l}_mix")
        dep = grads_ready(l, "mix", gm)
        g, dn = _ffn_bwd(s1, r["ffn1_norm"], w1["w_gate"], w1["w_up"], w1["w_down"], g, f"l{l}_ffn1", dep, ffn_grads("ffn1"))
        dep = grads_ready(l, "norm", dict(ffn1_norm=dn[0]))
    return loss, g


def _mesh_pos():
    return lax.axis_index("x"), lax.axis_index("y"), lax.axis_index("c")


def _dev_block(ref, dev, by_rows):
    if by_rows:
        r = ref.shape[1] // N_DEV
        return ref.at[:, pl.ds(dev * r, r), :]
    return ref.at[dev]


def _all_gather(shards, by_rows, name):
    n_arr = len(shards)
    out_shape = [_sds((s.shape[0], N_DEV * s.shape[1], s.shape[2]) if br else (N_DEV,) + s.shape, s.dtype)
                 for s, br in zip(shards, by_rows)]

    def body(*refs):
        xs, outs = refs[:n_arr], refs[n_arr:2 * n_arr]
        send_sems, recv_sems, local_sems = refs[2 * n_arr:]
        x, y, c = _mesh_pos()
        me, sibling = (x, y, c), (x, y, 1 - c)
        chips = [(1 - x, y), (x, 1 - y), (1 - x, 1 - y)]

        def rows(a, px, py, pc):
            return _dev_block(outs[a], 4 * px + 2 * py + pc, by_rows[a])

        def copy(k, a, block, to, src=None):
            return pltpu.make_async_remote_copy(
                src_ref=rows(a, *block) if src is None else src, dst_ref=rows(a, *block),
                send_sem=send_sems.at[k, a], recv_sem=recv_sems.at[k, a],
                device_id=to, device_id_type=pl.DeviceIdType.MESH)

        arrs = range(n_arr)
        mine = [pltpu.make_async_copy(xs[a], rows(a, *me), local_sems.at[a]) for a in arrs]
        for cp in mine:
            cp.start()
        first = [copy(0, a, me, sibling, src=xs[a]) for a in arrs]
        first += [copy(1 + j, a, me, (*chip, c), src=xs[a]) for j, chip in enumerate(chips) for a in arrs]
        for cp in first:
            cp.start()
        passed = []
        for j, chip in enumerate(chips):
            for a in arrs:
                copy(1 + j, a, (*chip, c), me).wait_recv()
                passed.append(copy(4 + j, a, (*chip, c), sibling))
                passed[-1].start()
        for a in arrs:
            copy(0, a, sibling, me).wait_recv()
        for j, chip in enumerate(chips):
            for a in arrs:
                copy(4 + j, a, (*chip, 1 - c), me).wait_recv()
        for cp in first + passed:
            cp.wait_send()
        for cp in mine:
            cp.wait()

    hbm = pl.BlockSpec(memory_space=pl.ANY)
    return pl.pallas_call(
        body, out_shape=out_shape, in_specs=[hbm] * n_arr, out_specs=[hbm] * n_arr,
        scratch_shapes=[pltpu.SemaphoreType.DMA((7, n_arr)), pltpu.SemaphoreType.DMA((7, n_arr)),
                        pltpu.SemaphoreType.DMA((n_arr,))],
        name=name)(*shards)


def _exchange(parts, by_rows, name):
    n_arr = len(parts)
    out_shape = [_sds((N_DEV, p.shape[0], p.shape[1] // N_DEV, p.shape[2]) if br else p.shape, p.dtype)
                 for p, br in zip(parts, by_rows)]

    def body(*refs):
        ps, outs = refs[:n_arr], refs[n_arr:2 * n_arr]
        send_sems, recv_sems, local_sems = refs[2 * n_arr:]
        x, y, c = _mesh_pos()
        my = 4 * x + 2 * y + c
        arrs = range(n_arr)
        mine = [pltpu.make_async_copy(_dev_block(ps[a], my, by_rows[a]), outs[a].at[my], local_sems.at[a]) for a in arrs]
        for cp in mine:
            cp.start()
        copies = []
        for k in range(1, N_DEV):
            px, py, pc = x ^ (k >> 2), y ^ ((k >> 1) & 1), c ^ (k & 1)
            for a in arrs:
                copies.append(pltpu.make_async_remote_copy(
                    src_ref=_dev_block(ps[a], 4 * px + 2 * py + pc, by_rows[a]), dst_ref=outs[a].at[my],
                    send_sem=send_sems.at[k - 1, a], recv_sem=recv_sems.at[k - 1, a],
                    device_id=(px, py, pc), device_id_type=pl.DeviceIdType.MESH))
        for cp in copies:
            cp.start()
        for cp in copies:
            cp.wait()
        for cp in mine:
            cp.wait()

    hbm = pl.BlockSpec(memory_space=pl.ANY)
    return pl.pallas_call(
        body, out_shape=out_shape, in_specs=[hbm] * n_arr, out_specs=[hbm] * n_arr,
        scratch_shapes=[pltpu.SemaphoreType.DMA((7, n_arr)), pltpu.SemaphoreType.DMA((7, n_arr)),
                        pltpu.SemaphoreType.DMA((n_arr,))],
        name=name)(*parts)


def _peer_copies(srcs, lands, send_sems, recv_sems, gather, by_rows):
    n_arr = len(srcs)
    x, y, c = _mesh_pos()
    my = 4 * x + 2 * y + c
    out = []
    for k in range(1, N_DEV):
        px, py, pc = x ^ (k >> 2), y ^ ((k >> 1) & 1), c ^ (k & 1)
        peer = 4 * px + 2 * py + pc
        for a in range(n_arr):
            src = srcs[a] if gather else _dev_block(srcs[a], peer, by_rows[a])
            dst = _dev_block(lands[a], my, by_rows[a]) if gather else lands[a].at[my]
            out.append(pltpu.make_async_remote_copy(
                src_ref=src, dst_ref=dst, send_sem=send_sems.at[(k - 1) * n_arr + a],
                recv_sem=recv_sems.at[(k - 1) * n_arr + a], device_id=(px, py, pc), device_id_type=pl.DeviceIdType.MESH))
    return out


def _land_shape(s, gather, by_rows):
    if gather:
        return (s.shape[0], N_DEV * s.shape[1], s.shape[2]) if by_rows else (N_DEV,) + s.shape
    return (N_DEV, s.shape[0], s.shape[1] // N_DEV, s.shape[2]) if by_rows else s.shape


_HBM = pl.BlockSpec(memory_space=pltpu.HBM)
_SEM = pl.BlockSpec(memory_space=pltpu.SEMAPHORE)


def _xfer_start(srcs, gather, by_rows, name, dep=None):
    n = len(srcs)
    lands = [lax.empty(_land_shape(s, gather, br), s.dtype) for s, br in zip(srcs, by_rows)]
    ins = [pltpu.with_memory_space_constraint(a, pltpu.HBM) for a in list(srcs) + lands]
    dspec, darg = _dep(dep)

    def body(*refs):
        s = 2 * n + len(darg)
        for cp in _peer_copies(refs[:n], refs[n:2 * n], refs[s], refs[s + 1], gather, by_rows):
            cp.start()
        refs[-1][...] = jnp.zeros_like(refs[-1])

    sems = pltpu.SemaphoreType.DMA(((N_DEV - 1) * n,))
    outs = pl.pallas_call(
        body, name=name,
        out_shape=(sems, sems, *[pltpu.HBM(a.shape, a.dtype) for a in ins], _sds((8, LANES), F32)),
        in_specs=[_HBM] * (2 * n) + dspec,
        out_specs=(_SEM, _SEM, *[_HBM] * (2 * n), pl.BlockSpec(memory_space=pltpu.VMEM)),
        input_output_aliases={i: 2 + i for i in range(2 * n)},
        compiler_params=pltpu.CompilerParams(has_side_effects=pltpu.SideEffectType.DATAFLOW_SIDE_EFFECTING))(*ins, *darg)
    return outs[0], outs[1], list(outs[2:-1]), outs[-1]


def _xfer_wait(started, after, gather, by_rows, name):
    send_sems, recv_sems, bufs, _ = started
    n = len(bufs) // 2

    def body(*refs):
        for cp in _peer_copies(refs[:n], refs[n:2 * n], refs[2 * n], refs[2 * n + 1], gather, by_rows):
            cp.wait_send()
            cp.wait_recv()

    outs = pl.pallas_call(
        body, name=name, out_shape=tuple(pltpu.HBM(a.shape, a.dtype) for a in bufs),
        in_specs=[_HBM] * (2 * n) + [_SEM, _SEM, pl.BlockSpec(memory_space=pl.ANY)], out_specs=tuple([_HBM] * (2 * n)),
        input_output_aliases={i: i for i in range(2 * n)},
        compiler_params=pltpu.CompilerParams(has_side_effects=pltpu.SideEffectType.DATAFLOW_SIDE_EFFECTING))(
            *bufs, send_sems, recv_sems, after)
    x, y, c = _mesh_pos()
    my = 4 * x + 2 * y + c
    res = []
    for src, land, br in zip(outs[:n], outs[n:], by_rows):
        zeros = (0,) * (land.ndim - 1)
        if gather and br:
            res.append(lax.dynamic_update_slice(land, src, (0, my * src.shape[1], 0)))
        elif gather:
            res.append(lax.dynamic_update_slice(land, src[None], (my,) + zeros))
        elif br:
            r = src.shape[1] // N_DEV
            own = lax.dynamic_slice(src, (0, my * r, 0), (src.shape[0], r, src.shape[2]))
            res.append(lax.dynamic_update_slice(land, own[None], (my,) + zeros))
        else:
            res.append(lax.dynamic_update_slice(land, lax.dynamic_index_in_dim(src, my, 0, keepdims=True), (my,) + zeros))
    return res


def _adam_update(g, w, m, v):
    c1 = 1.0 - ADAM_B1 ** ADAM_STEP
    c2 = 1.0 - ADAM_B2 ** ADAM_STEP
    nm = ADAM_B1 * m + (1.0 - ADAM_B1) * g
    nv = ADAM_B2 * v + (1.0 - ADAM_B2) * (g * g)
    return -ADAM_LR * ((nm / c1) / (jnp.sqrt(nv / c2) + ADAM_EPS) + ADAM_WD * w), nm, nv


def _adamw_body(p_ref, w_ref, m_ref, v_ref, g_ref, d_ref, nm_ref, nv_ref):
    g = p_ref[0]
    for i in range(1, N_DEV):
        g = g + p_ref[i]
    g_ref[...] = g
    d_ref[...], nm_ref[...], nv_ref[...] = _adam_update(g, w_ref[...], m_ref[...], v_ref[...])


def _adamw(parts, w, m, v, name, tr=1536):
    R = w.shape[0]
    tr = max(t for t in range(8, tr + 1, 8) if R % t == 0)

    def body(*refs):
        _adamw_body(*refs)

    row = pl.BlockSpec((tr, LANES), lambda i: (i, 0))
    return pl.pallas_call(
        body, grid=(R // tr,),
        in_specs=[pl.BlockSpec((N_DEV, tr, LANES), lambda i: (0, i, 0)), row, row, row],
        out_specs=[row, row, row, row], out_shape=[_sds((R, LANES), F32)] * 4,
        name=name, compiler_params=_cp(("parallel",)))(parts, w, m, v)


def _adamw_split(recvs, w, m, v, name, tr):
    depth, r, c = w.shape
    assert depth == len(recvs)
    tr = _tile(r, tr)

    def body(*refs):
        layer = pl.program_id(0)
        for ll in range(depth):
            @pl.when(layer == ll)
            def _(ll=ll):
                _adamw_body(refs[ll], *refs[depth:])

    wspec = pl.BlockSpec((None, tr, c), lambda l, i: (l, i, 0))
    rspecs = [pl.BlockSpec((N_DEV, None, tr, c), lambda l, i, ll=ll, t=t: (0, t, jnp.where(l == ll, i, 0), 0))
              for ll, (_, t) in enumerate(recvs)]
    return pl.pallas_call(
        body, grid=(depth, r // tr), in_specs=rspecs + [wspec, wspec, wspec],
        out_specs=[wspec] * 4, out_shape=[_sds(w.shape, F32)] * 4,
        name=name, compiler_params=_cp(("arbitrary", "arbitrary")))(*[a for a, _ in recvs], w, m, v)


def _merge_cols(g, name, tr=256):
    _, nt, K, n = g.shape
    tr = _tile(K, tr)

    def body(g_ref, o_ref):
        o_ref[...] = jnp.concatenate([g_ref[j] for j in range(N_DEV)], axis=1)

    return pl.pallas_call(
        body, grid=(nt, K // tr),
        in_specs=[pl.BlockSpec((N_DEV, None, tr, n), lambda t, i: (0, t, i, 0))],
        out_specs=pl.BlockSpec((None, tr, N_DEV * n), lambda t, i: (t, i, 0)),
        out_shape=_sds((nt, K, N_DEV * n), g.dtype), name=name, compiler_params=_cp(("parallel", "parallel")))(g)


def _merge_win(g, name, tr=256):
    _, nt, K, n = g.shape
    tr = _tile(K, tr)

    def body(g_ref, q_ref, r_ref):
        full = jnp.concatenate([g_ref[j] for j in range(N_DEV)], axis=1)
        q_ref[...] = full[:, 256:1792]
        zpad = jnp.zeros((tr, REST_W - 776), full.dtype)
        r_ref[...] = jnp.concatenate([full[:, 0:256], full[:, 1800:2312], full[:, 1792:1800], zpad], axis=1)

    return pl.pallas_call(
        body, grid=(nt, K // tr),
        in_specs=[pl.BlockSpec((N_DEV, None, tr, n), lambda t, i: (0, t, i, 0))],
        out_specs=[pl.BlockSpec((None, tr, 1536), lambda t, i: (t, i, 0)), pl.BlockSpec((None, tr, REST_W), lambda t, i: (t, i, 0))],
        out_shape=[_sds((nt, K, 1536), g.dtype), _sds((nt, K, REST_W), g.dtype)],
        name=name, compiler_params=_cp(("parallel", "parallel")))(g)


def _split_win(dq, dr, name, tr=256):
    K = dq.shape[0]
    tr = _tile(K, tr)
    n = (dq.shape[1] + 776) // N_DEV

    def body(q_ref, r_ref, o_ref):
        r = r_ref[...]
        full = jnp.concatenate([r[:, 0:256], q_ref[...], r[:, 768:776], r[:, 256:768]], axis=1)
        for j in range(N_DEV):
            o_ref[j] = full[:, n * j:n * (j + 1)]

    return pl.pallas_call(
        body, grid=(K // tr,),
        in_specs=[pl.BlockSpec((tr, dq.shape[1]), lambda i: (i, 0)), pl.BlockSpec((tr, REST_W), lambda i: (i, 0))],
        out_specs=pl.BlockSpec((N_DEV, tr, n), lambda i: (0, i, 0)),
        out_shape=_sds((N_DEV, K, n), F32), name=name, compiler_params=_cp(("parallel",)))(dq, dr)


WEIGHTS = ["ffn1_norm", "ffn1_w_gate", "ffn1_w_up", "ffn1_w_down", "mix_norm", "w_in", "pool_w", "pool_scale",
           "forget_bias", "conv_w", "conv_b", "conv_ln_g", "conv_ln_b", "w_out", "ffn2_norm", "ffn2_w_gate",
           "ffn2_w_up", "ffn2_w_down", "final_norm"]
FFN_COL = ["ffn1_w_gate", "ffn1_w_up", "ffn2_w_gate", "ffn2_w_up"]
FFN_ROW = ["ffn1_w_down", "ffn2_w_down"]
BIG = FFN_COL + FFN_ROW + ["w_in", "w_out"]
SMALL = [n for n in WEIGHTS if n not in BIG]


def _padded(n):
    return -(-n // PACK_ALIGN) * PACK_ALIGN


def _flat_pad(a):
    f = a.reshape(-1)
    return jnp.pad(f, (0, _padded(f.shape[0]) - f.shape[0]))


def _split8(a, axis):
    shp = a.shape
    a = a.reshape(shp[:axis] + (N_DEV, shp[axis] // N_DEV) + shp[axis + 1:])
    return jnp.moveaxis(a, axis, 0)


def _merge8(a, axis):
    a = jnp.moveaxis(a, 0, axis)
    shp = a.shape
    return a.reshape(shp[:axis] + (shp[axis] * shp[axis + 1],) + shp[axis + 2:])


def _pack_small(arrs):
    return jnp.concatenate([_flat_pad(arrs[n]) for n in SMALL]).reshape(-1, LANES)


def _pack_small_parts(grads):
    cols = []
    for n in SMALL:
        g = grads[n]
        if n == "conv_w":
            s = _split8(g, 2).reshape(N_DEV, -1)
        else:
            s = jnp.broadcast_to(g.reshape(1, -1), (N_DEV, g.size))
        cols.append(jnp.pad(s, ((0, 0), (0, _padded(s.shape[1]) - s.shape[1]))))
    return jnp.concatenate(cols, axis=1).reshape(N_DEV, -1, LANES)


def _unpack_small(buf, like):
    flat = buf.reshape(-1)
    out, off = {}, 0
    for n in SMALL:
        size = like[n].size
        out[n] = flat[off:off + size].reshape(like[n].shape)
        off += _padded(size)
    return out


class _Comm:
    def __init__(self, w):
        self.w = w
        self.bf = {n: w[n].astype(CDT) for n in BIG}
        self.ready = {}
        self.grads = {}
        self.recv = {}

    def _ffn_shards(self, l, which):
        cols = jnp.stack([self.bf[f"{which}_w_gate"][l], self.bf[f"{which}_w_up"][l]])
        return cols, self.bf[f"{which}_w_down"][l][None]

    def _put_ffn(self, l, which, cols, rows, t):
        self.ready[(l, which)] = dict(w_gate=cols[t], w_up=cols[t + 1], w_down=rows[t // 2])

    def weights_for(self, l, stage, x):
        bf = self.bf
        dep = None
        if (l, stage) == (0, "ffn1"):
            ga, gd = _all_gather(list(self._ffn_shards(0, "ffn1")), [False, True], "gather_l0_ffn1")
            self._put_ffn(0, "ffn1", _merge_cols(ga, "merge_l0_ffn1"), gd, 0)
            self.started = _xfer_start([bf["w_in"][0:1], bf["w_out"][0:1], self.w["conv_w"]], True,
                                       [False, True, False], "gather_mix0_start", dep=gd)
            dep = self.started[3]
        elif (l, stage) == (0, "mix"):
            gi, go, gc = _xfer_wait(self.started, x, True, [False, True, False], "gather_mix0_wait")
            q, r = _merge_win(gi, "merge_l0_w_in")
            self.conv_w = _merge8(gc, 2)
            self.ready[(0, "mix")] = dict(win_qkv=q[0], win_rest=r[0], w_out=go[0], conv_w=self.conv_w[0])
            shards = [self._ffn_shards(0, "ffn2"), self._ffn_shards(1, "ffn1"), self._ffn_shards(1, "ffn2")]
            self.started = _xfer_start(
                [jnp.concatenate([c for c, _ in shards]), jnp.concatenate([r for _, r in shards]), bf["w_in"][1:2],
                 bf["w_out"][1:2]], True, [False, True, False, True], "gather_rest_start")
            dep = self.started[3]
        elif (l, stage) == (0, "ffn2"):
            ga, gd, gi, go = _xfer_wait(self.started, x, True, [False, True, False, True], "gather_rest_wait")
            cols = _merge_cols(ga, "merge_rest_ffn")
            self._put_ffn(0, "ffn2", cols, gd, 0)
            self._put_ffn(1, "ffn1", cols, gd, 2)
            self._put_ffn(1, "ffn2", cols, gd, 4)
            q, r = _merge_win(gi, "merge_l1_w_in")
            self.ready[(1, "mix")] = dict(win_qkv=q[0], win_rest=r[0], w_out=go[0], conv_w=self.conv_w[1])
        return self.ready[(l, stage)], dep

    def grads_ready(self, l, stage, grads):
        for n, v in grads.items():
            self.grads[(l, n)] = v
        gr = self.grads
        four = [False, True, False, True]
        if l == 1 and "ffn1_w_gate" in grads:
            self.sent1 = _xfer_start(
                [jnp.stack([gr[(1, n)] for n in FFN_COL], axis=1), jnp.stack([gr[(1, n)] for n in FFN_ROW]),
                 gr[(1, "w_in")][:, None], gr[(1, "w_out")][None]], False, four, "grads_l1_start")
            return self.sent1[3]
        if (l, stage) == (0, "mix"):
            self.sent0 = _xfer_start(
                [jnp.stack([gr[(0, "ffn2_w_gate")], gr[(0, "ffn2_w_up")]], axis=1), gr[(0, "ffn2_w_down")][None],
                 gr[(0, "w_in")][:, None], gr[(0, "w_out")][None]], False, four, "grads_l0_start")
            return self.sent0[3]
        if l == 0 and "ffn1_w_down" in grads:
            self.sent_down = _xfer_start([gr[(0, "ffn1_w_down")][None]], False, [True], "grads_l0_ffn1_down_start")
            return self.sent_down[3]
        if l == 0 and "ffn1_w_gate" in grads:
            self.sent_cols = _xfer_start([jnp.stack([gr[(0, "ffn1_w_gate")], gr[(0, "ffn1_w_up")]], axis=1)],
                                         False, [False], "grads_l0_ffn1_cols_start")
            return self.sent_cols[3]
        return None

    def finish(self, m, v, after):
        w, gr = self.w, self.grads
        depth = range(w["w_in"].shape[0])
        small = {n: (gr[(None, n)] if n == "final_norm" else jnp.stack([gr[(l, n)] for l in depth])) for n in SMALL}
        four = [False, True, False, True]
        c1, r1, i1, o1 = _xfer_wait(self.sent1, after, False, four, "grads_l1_wait")
        c2, r2, i0, o0 = _xfer_wait(self.sent0, after, False, four, "grads_l0_wait")

        def adam(n, recvs, tr):
            return _adamw_split(recvs, w[n], m[n], v[n], f"adamw_{n}", tr)

        res = {}
        res["ffn2_w_gate"] = adam("ffn2_w_gate", [(c2, 0), (c1, 2)], 256)
        res["ffn2_w_up"] = adam("ffn2_w_up", [(c2, 1), (c1, 3)], 256)
        res["ffn2_w_down"] = adam("ffn2_w_down", [(r2, 0), (r1, 1)], 176)
        res["w_in"] = adam("w_in", [(i0, 0), (i1, 0)], 256)
        res["w_out"] = adam("w_out", [(o0, 0), (o1, 0)], 128)
        rs, = _exchange([_pack_small_parts(small)], [False], "exchange_small")
        r0, = _xfer_wait(self.sent_down, res["w_out"][0], False, [True], "grads_l0_ffn1_down_wait")
        c0, = _xfer_wait(self.sent_cols, res["w_out"][0], False, [False], "grads_l0_ffn1_cols_wait")
        res["ffn1_w_gate"] = adam("ffn1_w_gate", [(c0, 0), (c1, 0)], 256)
        res["ffn1_w_up"] = adam("ffn1_w_up", [(c0, 1), (c1, 1)], 256)
        res["ffn1_w_down"] = adam("ffn1_w_down", [(r0, 0), (r1, 0)], 176)
        packed = _adamw(rs, _pack_small(w), _pack_small(m), _pack_small(v), "adamw_small")
        unpacked = [_unpack_small(b, w) for b in packed]
        for n in SMALL:
            res[n] = [u[n] for u in unpacked]
        return res


def kernel(x, ffn1_norm, ffn1_w_gate, ffn1_w_up, ffn1_w_down, mix_norm, w_in, pool_w, pool_scale, forget_bias, conv_w, conv_b, conv_ln_g, conv_ln_b, w_out, ffn2_norm, ffn2_w_gate, ffn2_w_up, ffn2_w_down, final_norm, loss_target, m_ffn1_norm, m_ffn1_w_gate, m_ffn1_w_up, m_ffn1_w_down, m_mix_norm, m_w_in, m_pool_w, m_pool_scale, m_forget_bias, m_conv_w, m_conv_b, m_conv_ln_g, m_conv_ln_b, m_w_out, m_ffn2_norm, m_ffn2_w_gate, m_ffn2_w_up, m_ffn2_w_down, m_final_norm, v_ffn1_norm, v_ffn1_w_gate, v_ffn1_w_up, v_ffn1_w_down, v_mix_norm, v_w_in, v_pool_w, v_pool_scale, v_forget_bias, v_conv_w, v_conv_b, v_conv_ln_g, v_conv_ln_b, v_w_out, v_ffn2_norm, v_ffn2_w_gate, v_ffn2_w_up, v_ffn2_w_down, v_final_norm):
    w = dict(zip(WEIGHTS, (ffn1_norm, ffn1_w_gate, ffn1_w_up, ffn1_w_down, mix_norm, w_in, pool_w, pool_scale, forget_bias,
                           conv_w, conv_b, conv_ln_g, conv_ln_b, w_out, ffn2_norm, ffn2_w_gate, ffn2_w_up, ffn2_w_down,
                           final_norm)))
    m = dict(zip(WEIGHTS, (m_ffn1_norm, m_ffn1_w_gate, m_ffn1_w_up, m_ffn1_w_down, m_mix_norm, m_w_in, m_pool_w, m_pool_scale,
                           m_forget_bias, m_conv_w, m_conv_b, m_conv_ln_g, m_conv_ln_b, m_w_out, m_ffn2_norm, m_ffn2_w_gate,
                           m_ffn2_w_up, m_ffn2_w_down, m_final_norm)))
    v = dict(zip(WEIGHTS, (v_ffn1_norm, v_ffn1_w_gate, v_ffn1_w_up, v_ffn1_w_down, v_mix_norm, v_w_in, v_pool_w, v_pool_scale,
                           v_forget_bias, v_conv_w, v_conv_b, v_conv_ln_g, v_conv_ln_b, v_w_out, v_ffn2_norm, v_ffn2_w_gate,
                           v_ffn2_w_up, v_ffn2_w_down, v_final_norm)))
    comm = _Comm(w)
    loss_row, gx = _local_step(x[0], loss_target[0], w, comm.weights_for, comm.grads_ready)
    loss = lax.psum(loss_row[0, 0], ("x", "y", "c"))
    res = comm.finish(m, v, gx)
    return (loss, gx[None], *[res[n][i] for i in range(4) for n in WEIGHTS])
```

```python
import math

import numpy as np
import jax
import jax.numpy as jnp
from jax import lax
from jax.experimental import pallas as pl
from jax.experimental.pallas import tpu as pltpu

F32 = jnp.float32
CDT = jnp.bfloat16
NORM_EPS = 1e-6
N_DEV = 8
LANES = 128
PACK_ALIGN = 8 * LANES
VMEM_LIMIT = 48 * 1024 * 1024

POOL_WINDOWS = (2, 4, 8, 16)
POOL_HALO = 16
CONV_K = 31
CONV_HALO = 32
HEAD_DIM = 64
N_HEADS = 8
N_PAIRS = N_HEADS // 2
ATT_SCALE = 1.0 / math.sqrt(HEAD_DIM)
NEG = -1e30

ADAM_LR, ADAM_B1, ADAM_B2, ADAM_EPS, ADAM_WD, ADAM_STEP = 0.001, 0.9, 0.999, 1e-08, 0.01, 10

REST_W = 896
REST_Z_BLK = 6


def _cp(sem):
    return pltpu.CompilerParams(dimension_semantics=sem, vmem_limit_bytes=VMEM_LIMIT)


def _tile(n, pref):
    t = min(n, pref)
    assert n % t == 0, (n, pref)
    return t


def _sigmoid(x):
    return 1.0 / (1.0 + jnp.exp(-x))


def _sds(shape, dtype):
    return jax.ShapeDtypeStruct(shape, dtype)


_ANY = pl.BlockSpec(memory_space=pl.ANY)


def _dep(dep):
    return ([], []) if dep is None else ([_ANY], [dep])


def _rms_fwd(x, g, name, dep=None):
    T, D = x.shape
    tm = _tile(T, 1024)

    def body(x_ref, g_ref, *rest):
        o_ref = rest[-1]
        xv = x_ref[...]
        r = lax.rsqrt(jnp.mean(xv * xv, axis=-1, keepdims=True) + NORM_EPS)
        o_ref[...] = (xv * r * g_ref[...]).astype(o_ref.dtype)

    dspec, darg = _dep(dep)
    return pl.pallas_call(
        body, grid=(T // tm,),
        in_specs=[pl.BlockSpec((tm, D), lambda i: (i, 0)), pl.BlockSpec((1, D), lambda i: (0, 0))] + dspec,
        out_specs=pl.BlockSpec((tm, D), lambda i: (i, 0)),
        out_shape=_sds((T, D), CDT), name=name, compiler_params=_cp(("parallel",)))(x, g, *darg)


def _rms_bwd(x, g, dh, gres, name):
    T, D = x.shape
    tm = _tile(T, 512)

    def body(x_ref, g_ref, dh_ref, gres_ref, gin_ref, dg_ref):
        i = pl.program_id(0)
        xv = x_ref[...]
        d = dh_ref[...]
        r = lax.rsqrt(jnp.mean(xv * xv, axis=-1, keepdims=True) + NORM_EPS)
        xh = xv * r
        dxh = d * g_ref[...]
        c = jnp.mean(dxh * xh, axis=-1, keepdims=True)
        gin_ref[...] = gres_ref[...] + r * (dxh - xh * c)
        part = jnp.sum(d * xh, axis=0, keepdims=True)

        @pl.when(i == 0)
        def _():
            dg_ref[...] = part

        @pl.when(i > 0)
        def _():
            dg_ref[...] += part

    row = pl.BlockSpec((tm, D), lambda i: (i, 0))
    vec = pl.BlockSpec((1, D), lambda i: (0, 0))
    return pl.pallas_call(
        body, grid=(T // tm,), in_specs=[row, vec, row, row], out_specs=[row, vec],
        out_shape=[_sds((T, D), F32), _sds((1, D), F32)], name=name, compiler_params=_cp(("arbitrary",)))(x, g, dh, gres)


def _loss_bwd(x, g, target, name):
    T, D = x.shape
    tm = _tile(T, 512)

    def body(x_ref, g_ref, t_ref, loss_ref, dx_ref, dg_ref):
        i = pl.program_id(0)
        xv = x_ref[...]
        gv = g_ref[...]
        r = lax.rsqrt(jnp.mean(xv * xv, axis=-1, keepdims=True) + NORM_EPS)
        xh = xv * r
        err = xh * gv - t_ref[...]
        lpart = 0.5 * jnp.sum(jnp.mean(err * err, axis=-1, keepdims=True), axis=0, keepdims=True)
        dy = err * (1.0 / D)
        dxh = dy * gv
        c = jnp.mean(dxh * xh, axis=-1, keepdims=True)
        dx_ref[...] = r * (dxh - xh * c)
        part = jnp.sum(dy * xh, axis=0, keepdims=True)
        lrow = jnp.broadcast_to(lpart, (1, LANES))

        @pl.when(i == 0)
        def _():
            dg_ref[...] = part
            loss_ref[...] = lrow

        @pl.when(i > 0)
        def _():
            dg_ref[...] += part
            loss_ref[...] += lrow

    row = pl.BlockSpec((tm, D), lambda i: (i, 0))
    vec = pl.BlockSpec((1, D), lambda i: (0, 0))
    return pl.pallas_call(
        body, grid=(T // tm,), in_specs=[row, vec, row],
        out_specs=[pl.BlockSpec((1, LANES), lambda i: (0, 0)), row, vec],
        out_shape=[_sds((1, LANES), F32), _sds((T, D), F32), _sds((1, D), F32)],
        name=name, compiler_params=_cp(("arbitrary",)))(x, g, target)


def _mm(pairs, *, name, res=None, alpha=1.0, out_dtype=F32, tm=512, tn=None, dep=None):
    T = pairs[0][0].shape[0]
    N = pairs[0][1].shape[0] if pairs[0][2] else pairs[0][1].shape[1]
    tm = _tile(T, tm)
    tn = N if tn is None else _tile(N, tn)
    flags = [p[2] for p in pairs]
    n_in = 2 * len(pairs)

    def body(*refs):
        o_ref = refs[-1]
        acc = None
        for p, bt in enumerate(flags):
            a = refs[2 * p][...].astype(CDT)
            b = refs[2 * p + 1][...]
            dims = (((1,), (1,)), ((), ())) if bt else (((1,), (0,)), ((), ()))
            d = lax.dot_general(a, b, dims, preferred_element_type=F32)
            acc = d if acc is None else acc + d
        if alpha != 1.0:
            acc = acc * alpha
        if res is not None:
            acc = refs[n_in][...] + acc
        o_ref[...] = acc.astype(o_ref.dtype)

    in_specs, args = [], []
    for a, b, bt in pairs:
        K = a.shape[1]
        in_specs.append(pl.BlockSpec((tm, K), lambda i, j: (i, 0)))
        in_specs.append(pl.BlockSpec((tn, K), lambda i, j: (j, 0)) if bt else pl.BlockSpec((K, tn), lambda i, j: (0, j)))
        args += [a, b]
    if res is not None:
        in_specs.append(pl.BlockSpec((tm, tn), lambda i, j: (i, j)))
        args.append(res)
    dspec, darg = _dep(dep)
    in_specs += dspec
    args += darg
    return pl.pallas_call(
        body, grid=(T // tm, N // tn), in_specs=in_specs,
        out_specs=pl.BlockSpec((tm, tn), lambda i, j: (i, j)),
        out_shape=_sds((T, N), out_dtype), name=name, compiler_params=_cp(("parallel", "arbitrary")))(*args)


def _mm_norm_bwd(pairs, x, g, gres, *, name, tm=256, dep=None):
    T, D = x.shape
    tm = _tile(T, tm)
    n_in = 2 * len(pairs)
    flags = [p[2] for p in pairs]

    def body(*refs):
        x_ref, g_ref, gres_ref = refs[n_in:n_in + 3]
        gin_ref, dg_ref = refs[-2:]
        i = pl.program_id(0)
        d = None
        for p, bt in enumerate(flags):
            dims = (((1,), (1,)), ((), ())) if bt else (((1,), (0,)), ((), ()))
            part = lax.dot_general(refs[2 * p][...].astype(CDT), refs[2 * p + 1][...], dims, preferred_element_type=F32)
            d = part if d is None else d + part
        xv = x_ref[...]
        r = lax.rsqrt(jnp.mean(xv * xv, axis=-1, keepdims=True) + NORM_EPS)
        xh = xv * r
        dxh = d * g_ref[...]
        c = jnp.mean(dxh * xh, axis=-1, keepdims=True)
        gin_ref[...] = gres_ref[...] + r * (dxh - xh * c)
        part = jnp.sum(d * xh, axis=0, keepdims=True)

        @pl.when(i == 0)
        def _():
            dg_ref[...] = part

        @pl.when(i > 0)
        def _():
            dg_ref[...] += part

    in_specs, args = [], []
    for a, b, bt in pairs:
        K = a.shape[1]
        in_specs += [pl.BlockSpec((tm, K), lambda i: (i, 0)), pl.BlockSpec(b.shape, lambda i: (0, 0))]
        args += [a, b]
    row = pl.BlockSpec((tm, D), lambda i: (i, 0))
    vec = pl.BlockSpec((1, D), lambda i: (0, 0))
    dspec, darg = _dep(dep)
    return pl.pallas_call(
        body, grid=(T // tm,), in_specs=in_specs + [row, vec, row] + dspec, out_specs=[row, vec],
        out_shape=[_sds((T, D), F32), _sds((1, D), F32)], name=name,
        compiler_params=_cp(("arbitrary",)))(*args, x, g, gres, *darg)


def _mm_tn(a, b, *, name, alpha=1.0, tk=512, dep=None):
    T, M = a.shape
    N = b.shape[1]
    tm = M if M <= 1024 else M // 2
    tn = N if N <= 1536 else N // 2
    assert M % tm == 0 and N % tn == 0 and tm % LANES == 0 and tn % LANES == 0
    tk = _tile(T, tk)
    nk = T // tk

    def body(a_ref, b_ref, *rest):
        o_ref = rest[-1]
        k = pl.program_id(2)
        d = lax.dot_general(a_ref[...].astype(CDT), b_ref[...].astype(CDT), (((0,), (0,)), ((), ())),
                            preferred_element_type=F32)

        @pl.when(k == 0)
        def _():
            o_ref[...] = d

        @pl.when(k > 0)
        def _():
            o_ref[...] += d

        if alpha != 1.0:
            @pl.when(k == nk - 1)
            def _():
                o_ref[...] *= alpha

    dspec, darg = _dep(dep)
    return pl.pallas_call(
        body, grid=(M // tm, N // tn, nk),
        in_specs=[pl.BlockSpec((tk, tm), lambda i, j, k: (k, i)), pl.BlockSpec((tk, tn), lambda i, j, k: (k, j))] + dspec,
        out_specs=pl.BlockSpec((tm, tn), lambda i, j, k: (i, j)),
        out_shape=_sds((M, N), F32), name=name, compiler_params=_cp(("parallel", "parallel", "arbitrary")))(a, b, *darg)


def _ffn_up(h, wgt, wut, name):
    T, D = h.shape
    Fh = wgt.shape[0]
    tm = _tile(T, 2048)
    tn = _tile(Fh, 256)
    nt = (((1,), (1,)), ((), ()))

    def body(h_ref, wg_ref, wu_ref, a_ref, b_ref, s_ref):
        hv = h_ref[...]
        a = lax.dot_general(hv, wg_ref[...], nt, preferred_element_type=F32)
        b = lax.dot_general(hv, wu_ref[...], nt, preferred_element_type=F32)
        a_ref[...] = a.astype(a_ref.dtype)
        b_ref[...] = b.astype(b_ref.dtype)
        s_ref[...] = (a * _sigmoid(a) * b).astype(s_ref.dtype)

    wspec = pl.BlockSpec((tn, D), lambda i, j: (j, 0))
    ospec = pl.BlockSpec((tm, tn), lambda i, j: (i, j))
    return pl.pallas_call(
        body, grid=(T // tm, Fh // tn),
        in_specs=[pl.BlockSpec((tm, D), lambda i, j: (i, 0)), wspec, wspec],
        out_specs=[ospec, ospec, ospec],
        out_shape=[_sds((T, Fh), CDT), _sds((T, Fh), CDT), _sds((T, Fh), CDT)],
        name=name, compiler_params=_cp(("parallel", "arbitrary")))(h, wgt, wut)


def _ffn_bwd_ds(gout, wd, a, b, name, dep=None):
    T, D = gout.shape
    Fh = wd.shape[0]
    tm = _tile(T, 1024)
    tn = _tile(Fh, 256)

    def body(g_ref, wd_ref, a_ref, b_ref, *rest):
        da_ref, db_ref = rest[-2:]
        dy = (0.5 * g_ref[...]).astype(CDT)
        ds = lax.dot_general(dy, wd_ref[...], (((1,), (1,)), ((), ())), preferred_element_type=F32)
        av = a_ref[...].astype(F32)
        sg = _sigmoid(av)
        da_ref[...] = (ds * b_ref[...].astype(F32) * (sg * (1.0 + av * (1.0 - sg)))).astype(da_ref.dtype)
        db_ref[...] = (ds * (av * sg)).astype(db_ref.dtype)

    ospec = pl.BlockSpec((tm, tn), lambda i, j: (i, j))
    dspec, darg = _dep(dep)
    return pl.pallas_call(
        body, grid=(T // tm, Fh // tn),
        in_specs=[pl.BlockSpec((tm, D), lambda i, j: (i, 0)), pl.BlockSpec((tn, D), lambda i, j: (j, 0)), ospec, ospec] + dspec,
        out_specs=[ospec, ospec],
        out_shape=[_sds((T, Fh), CDT), _sds((T, Fh), CDT)],
        name=name, compiler_params=_cp(("parallel", "arbitrary")))(gout, wd, a, b, *darg)


def _ffn_fwd(x, gamma, wgt, wut, wd, tag, dep=None):
    h = _rms_fwd(x, gamma, f"{tag}_norm", dep)
    a, b, s = _ffn_up(h, wgt, wut, f"{tag}_up")
    y = _mm([(s, wd, False)], res=x, alpha=0.5, tn=512, name=f"{tag}_down")
    return y, (x, h, a, b, s)


def _ffn_bwd(saved, gamma, wgt, wut, wd, gout, tag, dep, on_grads):
    x, h, a, b, s = saved
    dwd = _mm_tn(s, gout, alpha=0.5, name=f"{tag}_dwd", dep=dep)
    da, db = _ffn_bwd_ds(gout, wd, a, b, f"{tag}_bwd_ds", on_grads(dict(w_down=dwd)))
    dwgt = _mm_tn(da, h, name=f"{tag}_dwg")
    dwut = _mm_tn(db, h, name=f"{tag}_dwu")
    dep = on_grads(dict(w_gate=dwgt, w_up=dwut))
    return _mm_norm_bwd([(da, wgt, False), (db, wut, False)], x, gamma, gout, name=f"{tag}_dh_norm_bwd", dep=dep)


def _fgate_fwd(rest, bias, name, bt=512):
    T = rest.shape[0]
    bt = _tile(T, bt)

    def body(z_ref, b_ref, fc_ref, ft_ref, carry):
        i = pl.program_id(0)

        @pl.when(i == 0)
        def _():
            carry[...] = jnp.zeros_like(carry)

        zb = z_ref[...] + b_ref[...]
        e = jnp.exp(-jnp.abs(zb))
        u = 1.0 + e
        log1p_e = jnp.where(u == 1.0, e, jnp.log(u) * (e / (u - 1.0)))
        x = jnp.minimum(zb, 0.0) - log1p_e
        row = lax.broadcasted_iota(jnp.int32, x.shape, 0)
        sh = 1
        while sh < bt:
            x = x + jnp.where(row >= sh, pltpu.roll(x, sh, 0), 0.0)
            sh *= 2
        f = x + carry[...]
        carry[...] = f[bt - 1:bt, :]
        fc_ref[...] = f
        ft_ref[...] = jnp.transpose(f)[0:N_HEADS, :]

    return pl.pallas_call(
        body, grid=(T // bt,),
        in_specs=[pl.BlockSpec((bt, LANES), lambda i: (i, REST_Z_BLK)), pl.BlockSpec((1, LANES), lambda i: (0, 0))],
        out_specs=[pl.BlockSpec((bt, LANES), lambda i: (i, 0)), pl.BlockSpec((N_HEADS, bt), lambda i: (0, i))],
        out_shape=[_sds((T, LANES), F32), _sds((N_HEADS, T), F32)],
        scratch_shapes=[pltpu.VMEM((1, LANES), F32)],
        name=name, compiler_params=_cp(("arbitrary",)))(rest, bias)


def _fgate_bwd(dfk, rest, bias, name, bt=512):
    T = rest.shape[0]
    bt = _tile(T, bt)
    nb = T // bt

    def body(df_ref, z_ref, b_ref, dz_ref, db_ref, carry):
        i = pl.program_id(0)

        @pl.when(i == 0)
        def _():
            carry[...] = jnp.zeros_like(carry)

        dfv = df_ref[...]
        lane = lax.broadcasted_iota(jnp.int32, (bt, LANES), 1)
        x = jnp.zeros((bt, LANES), F32)
        for h in range(N_HEADS):
            x = jnp.where(lane == h, dfv[:, HEAD_DIM * h:HEAD_DIM * h + 1], x)
        row = lax.broadcasted_iota(jnp.int32, x.shape, 0)
        sh = 1
        while sh < bt:
            x = x + jnp.where(row + sh < bt, pltpu.roll(x, bt - sh, 0), 0.0)
            sh *= 2
        dlf = x + carry[...]
        carry[...] = dlf[0:1, :]
        zb = z_ref[...] + b_ref[...]
        dz = jnp.where(lane < N_HEADS, dlf * _sigmoid(-zb), 0.0)
        dz_ref[...] = dz.astype(dz_ref.dtype)
        part = jnp.sum(dz, axis=0, keepdims=True)

        @pl.when(i == 0)
        def _():
            db_ref[...] = part

        @pl.when(i > 0)
        def _():
            db_ref[...] += part

    return pl.pallas_call(
        body, grid=(nb,),
        in_specs=[pl.BlockSpec((bt, 4 * LANES), lambda i: (nb - 1 - i, 0)),
                  pl.BlockSpec((bt, LANES), lambda i: (nb - 1 - i, REST_Z_BLK)),
                  pl.BlockSpec((1, LANES), lambda i: (0, 0))],
        out_specs=[pl.BlockSpec((bt, LANES), lambda i: (nb - 1 - i, 0)), pl.BlockSpec((1, LANES), lambda i: (0, 0))],
        out_shape=[_sds((T, LANES), CDT), _sds((1, LANES), F32)],
        scratch_shapes=[pltpu.VMEM((1, LANES), F32)],
        name=name, compiler_params=_cp(("arbitrary",)))(dfk, rest, bias)


def _by_group(vals, lane):
    out = vals[-1]
    for g in range(len(vals) - 2, -1, -1):
        out = jnp.where(lane // 64 == g, vals[g], out)
    return out


def _pool_counts(t0, n, lane):
    t = t0 + lax.broadcasted_iota(jnp.int32, (n, 256), 0)
    return _by_group([jnp.minimum(t + 1, w) for w in POOL_WINDOWS], lane).astype(F32)


def _pooled(u, halo, i, bt):
    lane = lax.broadcasted_iota(jnp.int32, (bt, 256), 1)
    ext = jnp.concatenate([jnp.where(i > 0, halo, 0.0), u], axis=0)
    sums, s, sh = [], ext, 1
    for _ in POOL_WINDOWS:
        s = s + pltpu.roll(s, sh, 0)
        sums.append(s[POOL_HALO:, :])
        sh *= 2
    return _by_group(sums, lane) / _pool_counts(i * bt, bt, lane) - u


def _pool_fwd(rest, wbd, scale, name, bt=512):
    T = rest.shape[0]
    bt = _tile(T, bt)
    hb = bt // POOL_HALO

    def body(u_ref, halo_ref, w_ref, sc_ref, o_ref):
        i = pl.program_id(0)
        pooled = _pooled(u_ref[...], halo_ref[...], i, bt)
        mixed = jnp.dot(pooled.astype(CDT), w_ref[...], preferred_element_type=F32)
        o_ref[...] = (mixed * sc_ref[...]).astype(o_ref.dtype)

    return pl.pallas_call(
        body, grid=(T // bt,),
        in_specs=[pl.BlockSpec((bt, 256), lambda i: (i, 0)),
                  pl.BlockSpec((POOL_HALO, 256), lambda i: (jnp.maximum(i * hb - 1, 0), 0)),
                  pl.BlockSpec((256, 256), lambda i: (0, 0)), pl.BlockSpec((1, 256), lambda i: (0, 0))],
        out_specs=pl.BlockSpec((bt, 256), lambda i: (i, 0)),
        out_shape=_sds((T, 256), CDT), name=name, compiler_params=_cp(("parallel",)))(rest, rest, wbd, scale)


def _pool_bwd(dcat, rest, wbd, scale, name, bt=512):
    T = rest.shape[0]
    bt = _tile(T, bt)
    hb = bt // POOL_HALO
    nb = T // bt
    n = bt + POOL_HALO

    def body(dy_ref, dyn_ref, u_ref, halo_ref, w_ref, sc_ref, du_ref, dw_ref, dsc_ref):
        i = pl.program_id(0)
        lane = lax.broadcasted_iota(jnp.int32, (bt, 256), 1)
        w = w_ref[...]
        sc = sc_ref[...]
        pooled = _pooled(u_ref[...], halo_ref[...], i, bt)
        pooled_c = pooled.astype(CDT)
        mixed = jnp.dot(pooled_c, w, preferred_element_type=F32)
        dy = dy_ref[...]
        dm = (dy * sc).astype(CDT)
        dsc = jnp.sum(dy * mixed, axis=0, keepdims=True)
        dw = lax.dot_general(pooled_c, dm, (((0,), (0,)), ((), ())), preferred_element_type=F32)
        nt = (((1,), (1,)), ((), ()))
        dpl = lax.dot_general(dm, w, nt, preferred_element_type=F32)
        dmn = (jnp.where(i < nb - 1, dyn_ref[...], 0.0) * sc).astype(CDT)
        dpln = lax.dot_general(dmn, w, nt, preferred_element_type=F32)
        lane_h = lax.broadcasted_iota(jnp.int32, (POOL_HALO, 256), 1)
        ext = jnp.concatenate([dpl / _pool_counts(i * bt, bt, lane),
                               dpln / _pool_counts((i + 1) * bt, POOL_HALO, lane_h)], axis=0)
        sums, s, sh = [], ext, 1
        for _ in POOL_WINDOWS:
            s = s + pltpu.roll(s, n - sh, 0)
            sums.append(s[0:bt, :])
            sh *= 2
        du_ref[...] = (_by_group(sums, lane) - dpl).astype(du_ref.dtype)

        @pl.when(i == 0)
        def _():
            dw_ref[...] = dw
            dsc_ref[...] = dsc

        @pl.when(i > 0)
        def _():
            dw_ref[...] += dw
            dsc_ref[...] += dsc

    full = pl.BlockSpec((256, 256), lambda i: (0, 0))
    vec = pl.BlockSpec((1, 256), lambda i: (0, 0))
    return pl.pallas_call(
        body, grid=(nb,),
        in_specs=[pl.BlockSpec((bt, 256), lambda i: (i, 0)),
                  pl.BlockSpec((POOL_HALO, 256), lambda i: (jnp.minimum((i + 1) * hb, nb * hb - 1), 0)),
                  pl.BlockSpec((bt, 256), lambda i: (i, 0)),
                  pl.BlockSpec((POOL_HALO, 256), lambda i: (jnp.maximum(i * hb - 1, 0), 0)),
                  full, vec],
        out_specs=[pl.BlockSpec((bt, 256), lambda i: (i, 0)), full, vec],
        out_shape=[_sds((T, 256), CDT), _sds((256, 256), F32), _sds((1, 256), F32)],
        name=name, compiler_params=_cp(("arbitrary",)))(dcat, dcat, rest, rest, wbd, scale)


def _glu_ext(a_ref, g_ref, ah_ref, gh_ref, i):
    u = a_ref[...] * _sigmoid(g_ref[...])
    uh = jnp.where(i > 0, ah_ref[...] * _sigmoid(gh_ref[...]), 0.0)
    return jnp.concatenate([uh, u], axis=0)


def _conv_fwd(rest, cw, cb, lg, lb, name, bt=512):
    T = rest.shape[0]
    bt = _tile(T, bt)
    hb = bt // CONV_HALO

    def body(a_ref, g_ref, ah_ref, gh_ref, cw_ref, cb_ref, lg_ref, lb_ref, o_ref, y_ref):
        i = pl.program_id(0)
        ext = _glu_ext(a_ref, g_ref, ah_ref, gh_ref, i)
        w = cw_ref[...]
        acc = w[CONV_K - 1:CONV_K, :] * ext
        for k in range(CONV_K - 1):
            acc = acc + w[k:k + 1, :] * pltpu.roll(ext, CONV_K - 1 - k, 0)
        y = acc[CONV_HALO:, :] + cb_ref[...]
        y_ref[...] = y
        yc = y - jnp.mean(y, axis=-1, keepdims=True)
        yn = yc * lax.rsqrt(jnp.mean(yc * yc, axis=-1, keepdims=True) + NORM_EPS)
        z = yn * lg_ref[...] + lb_ref[...]
        o_ref[...] = (z * _sigmoid(z)).astype(o_ref.dtype)

    def cur(c):
        return pl.BlockSpec((bt, 256), lambda i: (i, c))

    def prev(c):
        return pl.BlockSpec((CONV_HALO, 256), lambda i: (jnp.maximum(i * hb - 1, 0), c))

    vec = pl.BlockSpec((1, 256), lambda i: (0, 0))
    return pl.pallas_call(
        body, grid=(T // bt,),
        in_specs=[cur(1), cur(2), prev(1), prev(2), pl.BlockSpec((CONV_HALO, 256), lambda i: (0, 0)), vec, vec, vec],
        out_specs=[pl.BlockSpec((bt, 256), lambda i: (i, 0)), pl.BlockSpec((bt, 256), lambda i: (i, 0))],
        out_shape=[_sds((T, 256), CDT), _sds((T, 256), F32)],
        name=name, compiler_params=_cp(("parallel",)))(rest, rest, rest, rest, cw, cb, lg, lb)


def _conv_bwd(dcat, yconv, rest, cw, lg, lb, name, bt=512):
    T = rest.shape[0]
    bt = _tile(T, bt)
    hb = bt // CONV_HALO
    nb = T // bt
    n = bt + CONV_HALO

    def body(dy_ref, dyn_ref, y_ref, yn_ref, a_ref, g_ref, ah_ref, gh_ref, cw_ref, lg_ref, lb_ref,
             da_ref, dg_ref, dcw_ref, dcb_ref, dlg_ref, dlb_ref):
        i = pl.program_id(0)
        lgv = lg_ref[...]
        lbv = lb_ref[...]

        def ln_swish_bwd(dout, y):
            yc = y - jnp.mean(y, axis=-1, keepdims=True)
            rs = lax.rsqrt(jnp.mean(yc * yc, axis=-1, keepdims=True) + NORM_EPS)
            yn = yc * rs
            z = yn * lgv + lbv
            sg = _sigmoid(z)
            dz = dout * (sg * (1.0 + z * (1.0 - sg)))
            dyn = dz * lgv
            dyc = rs * (dyn - jnp.mean(dyn, axis=-1, keepdims=True) - yn * jnp.mean(dyn * yn, axis=-1, keepdims=True))
            return dyc, dz, yn

        dyc, dz, yn = ln_swish_bwd(dy_ref[...], y_ref[...])
        dyc_next, _, _ = ln_swish_bwd(dyn_ref[...], yn_ref[...])
        dyc_next = jnp.where(i < nb - 1, dyc_next, 0.0)
        ext_u = _glu_ext(a_ref, g_ref, ah_ref, gh_ref, i)
        ext_d = jnp.concatenate([dyc, dyc_next], axis=0)
        w = cw_ref[...]
        du = w[CONV_K - 1:CONV_K, :] * ext_d
        rows = []
        for k in range(CONV_K):
            s = CONV_K - 1 - k
            if s > 0:
                du = du + w[k:k + 1, :] * pltpu.roll(ext_d, n - s, 0)
                us = pltpu.roll(ext_u, s, 0)[CONV_HALO:, :]
            else:
                us = ext_u[CONV_HALO:, :]
            rows.append(jnp.sum(dyc * us, axis=0, keepdims=True))
        rows.append(jnp.zeros((1, 256), F32))
        dcw = jnp.concatenate(rows, axis=0)
        du = du[0:bt, :]
        av = a_ref[...]
        sg = _sigmoid(g_ref[...])
        da_ref[...] = (du * sg).astype(da_ref.dtype)
        dg_ref[...] = (du * av * (sg * (1.0 - sg))).astype(dg_ref.dtype)
        dcb = jnp.sum(dyc, axis=0, keepdims=True)
        dlg = jnp.sum(dz * yn, axis=0, keepdims=True)
        dlb = jnp.sum(dz, axis=0, keepdims=True)

        @pl.when(i == 0)
        def _():
            dcw_ref[...] = dcw
            dcb_ref[...] = dcb
            dlg_ref[...] = dlg
            dlb_ref[...] = dlb

        @pl.when(i > 0)
        def _():
            dcw_ref[...] += dcw
            dcb_ref[...] += dcb
            dlg_ref[...] += dlg
            dlb_ref[...] += dlb

    def cur(c):
        return pl.BlockSpec((bt, 256), lambda i: (i, c))

    def prev(c):
        return pl.BlockSpec((CONV_HALO, 256), lambda i: (jnp.maximum(i * hb - 1, 0), c))

    def nxt(c):
        return pl.BlockSpec((CONV_HALO, 256), lambda i: (jnp.minimum((i + 1) * hb, nb * hb - 1), c))

    vec = pl.BlockSpec((1, 256), lambda i: (0, 0))
    wfull = pl.BlockSpec((CONV_HALO, 256), lambda i: (0, 0))
    return pl.pallas_call(
        body, grid=(nb,),
        in_specs=[cur(3), nxt(3), cur(0), nxt(0), cur(1), cur(2), prev(1), prev(2), wfull, vec, vec],
        out_specs=[cur(0), cur(0), wfull, vec, vec, vec],
        out_shape=[_sds((T, 256), CDT), _sds((T, 256), CDT), _sds((CONV_HALO, 256), F32),
                   _sds((1, 256), F32), _sds((1, 256), F32), _sds((1, 256), F32)],
        name=name, compiler_params=_cp(("arbitrary",)))(dcat, dcat, yconv, yconv, rest, rest, rest, rest, cw, lg, lb)


def _half_mask(shape, a):
    lane = lax.broadcasted_iota(jnp.int32, shape, 1)
    return (lane // HEAD_DIM) == a


def _attn_fwd(qkv, fcol, frow, name, blk=512):
    T = qkv.shape[0]
    blk = _tile(T, blk)
    nq = T // blk
    nt = (((1,), (1,)), ((), ()))

    def body(q_ref, k_ref, v_ref, fc_ref, fr_ref, o_ref, lse_ref):
        p_id = pl.program_id(0)
        i = pl.program_id(1)
        q2 = q_ref[...]
        fc = fc_ref[...]
        lane = lax.broadcasted_iota(jnp.int32, (blk, LANES), 1)
        tri = lax.broadcasted_iota(jnp.int32, (blk, blk), 1) <= lax.broadcasted_iota(jnp.int32, (blk, blk), 0)
        masks = [_half_mask(q2.shape, a) for a in range(2)]
        qs = [jnp.where(hm, q2, jnp.zeros_like(q2)) * ATT_SCALE for hm in masks]
        fqs = [jnp.sum(jnp.where(lane == 2 * p_id + a, fc, 0.0), axis=1, keepdims=True) for a in range(2)]

        def tile(j, carry, masked):
            cols = pl.ds(pl.multiple_of(j * blk, blk), blk)
            kj = k_ref[cols, :]
            vj = v_ref[cols, :]
            out = []
            for a in range(2):
                m, acc = carry[2 * a:2 * a + 2]
                va = jnp.where(masks[a], vj, jnp.ones_like(vj))
                s = lax.dot_general(qs[a], kj, nt, preferred_element_type=F32) + (fqs[a] - fr_ref[a:a + 1, cols])
                if masked:
                    s = jnp.where(tri, s, NEG)
                m_new = jnp.maximum(m, jnp.max(s, axis=1, keepdims=True))
                alpha = jnp.exp(m - m_new)
                pr = jnp.exp(s - m_new)
                hi = lax.bitcast_convert_type(lax.bitcast_convert_type(pr, jnp.uint32) & jnp.uint32(0xFFFF0000), F32)
                pv = (jnp.dot(hi.astype(CDT), va, preferred_element_type=F32)
                      + jnp.dot((pr - hi).astype(CDT), va, preferred_element_type=F32))
                out += [m_new, alpha * acc + pv]
            return tuple(out)

        init = (jnp.full((blk, 1), NEG, F32), jnp.zeros((blk, LANES), F32)) * 2
        carry = lax.fori_loop(0, i, lambda j, c: tile(j, c, False), init)
        carry = tile(i, carry, True)
        ls = [carry[1][:, HEAD_DIM:HEAD_DIM + 1], carry[3][:, 0:1]]
        lo = lane < HEAD_DIM
        o_ref[...] = jnp.where(lo, carry[1] / ls[0], carry[3] / ls[1])
        lse_t = jnp.transpose(jnp.where(lo, carry[0] + jnp.log(ls[0]), carry[2] + jnp.log(ls[1])))
        lse_ref[...] = jnp.concatenate([lse_t[0:1, :], lse_t[HEAD_DIM:HEAD_DIM + 1, :]], axis=0)

    return pl.pallas_call(
        body, grid=(N_PAIRS, nq),
        in_specs=[pl.BlockSpec((blk, LANES), lambda p, i: (i, p)),
                  pl.BlockSpec((T, LANES), lambda p, i: (0, N_PAIRS + p)),
                  pl.BlockSpec((T, LANES), lambda p, i: (0, 2 * N_PAIRS + p)),
                  pl.BlockSpec((blk, LANES), lambda p, i: (i, 0)),
                  pl.BlockSpec((None, 2, T), lambda p, i: (p, 0, 0))],
        out_specs=[pl.BlockSpec((blk, LANES), lambda p, i: (i, p)), pl.BlockSpec((None, 2, blk), lambda p, i: (p, 0, i))],
        out_shape=[_sds((T, N_PAIRS * LANES), F32), _sds((N_PAIRS, 2, T), F32)],
        name=name, compiler_params=_cp(("parallel", "arbitrary")))(qkv, qkv, qkv, fcol, frow)


def _attn_delta(dcat, o, name, blk=512):
    T = o.shape[0]
    blk = _tile(T, blk)

    def body(d_ref, o_ref, out_ref):
        prod = d_ref[:, 256:768].astype(CDT).astype(F32) * o_ref[...]
        pt = jnp.transpose(prod)
        out_ref[...] = jnp.sum(pt.reshape(N_HEADS, HEAD_DIM, blk), axis=1)

    return pl.pallas_call(
        body, grid=(T // blk,),
        in_specs=[pl.BlockSpec((blk, 1024), lambda i: (i, 0)), pl.BlockSpec((blk, 512), lambda i: (i, 0))],
        out_specs=pl.BlockSpec((N_HEADS, blk), lambda i: (0, i)),
        out_shape=_sds((N_HEADS, T), F32), name=name, compiler_params=_cp(("parallel",)))(dcat, o)


def _attn_bwd(qkv, dcat, fcol, frow, lse, delta, name, blk=512):
    T = qkv.shape[0]
    blk = _tile(T, blk)
    nq = T // blk
    nt = (((1,), (1,)), ((), ()))

    def body(q_ref, do_ref, k_ref, v_ref, fc_ref, fr_ref, lse_ref, dl_ref, dqt_ref, dk_ref, dv_ref, df_ref):
        p_id = pl.program_id(0)
        j = pl.program_id(1)

        @pl.when(j == 0)
        def _():
            dqt_ref[...] = jnp.zeros_like(dqt_ref)

        k2 = k_ref[...]
        v2 = v_ref[...]
        fc = fc_ref[...]
        lane = lax.broadcasted_iota(jnp.int32, (blk, LANES), 1)
        tri = lax.broadcasted_iota(jnp.int32, (blk, blk), 0) <= lax.broadcasted_iota(jnp.int32, (blk, blk), 1)
        masks = [_half_mask(k2.shape, a) for a in range(2)]
        kas = [jnp.where(hm, k2, jnp.zeros_like(k2)) * ATT_SCALE for hm in masks]
        kats = [jnp.transpose(ka) for ka in kas]
        vas = [jnp.where(hm, v2, jnp.zeros_like(v2)) for hm in masks]
        fks = [jnp.sum(jnp.where(lane == 2 * p_id + a, fc, 0.0), axis=1, keepdims=True) for a in range(2)]

        def tile(i, carry, masked):
            rows = pl.ds(pl.multiple_of(i * blk, blk), blk)
            qi = q_ref[rows, :]
            doi = do_ref[rows, :].astype(CDT)
            out = []
            dqt = None
            for a in range(2):
                dk_acc, dv_acc, df_acc = carry[3 * a:3 * a + 3]
                st = lax.dot_general(kas[a], qi, nt, preferred_element_type=F32)
                e = (st + (fr_ref[a:a + 1, rows] - fks[a])) - lse_ref[a:a + 1, rows]
                if masked:
                    e = jnp.where(tri, e, NEG)
                pt = jnp.exp(e)
                dpt = lax.dot_general(vas[a], doi, nt, preferred_element_type=F32)
                ds32 = pt * (dpt - dl_ref[a:a + 1, rows])
                dst = ds32.astype(CDT)
                df_acc = df_acc + jnp.sum(ds32, axis=1, keepdims=True)
                dv_acc = dv_acc + jnp.dot(pt.astype(CDT), doi, preferred_element_type=F32)
                dk_acc = dk_acc + jnp.dot(dst, qi, preferred_element_type=F32)
                part = jnp.dot(kats[a], dst, preferred_element_type=F32)
                dqt = part if dqt is None else dqt + part
                out += [dk_acc, dv_acc, df_acc]
            dqt_ref[:, rows] += dqt
            return tuple(out)

        init = (jnp.zeros((blk, LANES), F32), jnp.zeros((blk, LANES), F32), jnp.zeros((blk, 1), F32)) * 2
        carry = tile(j, init, True)
        carry = lax.fori_loop(j + 1, nq, lambda i, c: tile(i, c, False), carry)
        lo = lane < HEAD_DIM
        dk_ref[...] = (jnp.where(lo, carry[0], carry[3]) * ATT_SCALE).astype(dk_ref.dtype)
        dv_ref[...] = jnp.where(lo, carry[1], carry[4]).astype(dv_ref.dtype)
        df_ref[...] = -jnp.where(lo, carry[2], carry[5])

    res = pl.BlockSpec((T, LANES), lambda p, j: (0, p))
    rows = pl.BlockSpec((None, 2, T), lambda p, j: (p, 0, 0))
    kv_out = pl.BlockSpec((blk, LANES), lambda p, j: (j, p))
    return pl.pallas_call(
        body, grid=(N_PAIRS, nq),
        in_specs=[res, pl.BlockSpec((T, LANES), lambda p, j: (0, 2 + p)),
                  pl.BlockSpec((blk, LANES), lambda p, j: (j, N_PAIRS + p)),
                  pl.BlockSpec((blk, LANES), lambda p, j: (j, 2 * N_PAIRS + p)),
                  pl.BlockSpec((blk, LANES), lambda p, j: (j, 0)), rows, rows, rows],
        out_specs=[pl.BlockSpec((LANES, T), lambda p, j: (p, 0)), kv_out, kv_out, kv_out],
        out_shape=[_sds((N_PAIRS * LANES, T), F32), _sds((T, N_PAIRS * LANES), CDT), _sds((T, N_PAIRS * LANES), CDT),
                   _sds((T, N_PAIRS * LANES), F32)],
        name=name, compiler_params=_cp(("parallel", "arbitrary")))(qkv, dcat, qkv, qkv, fcol, frow, lse, delta)


def _mixer_fwd(x, wts, tag, dep=None):
    T = x.shape[0]
    h = _rms_fwd(x, wts["mix_norm"], f"{tag}_norm", dep)
    qkv = _mm([(h, wts["win_qkv"], False)], out_dtype=CDT, tm=1024, tn=768, name=f"{tag}_in_qkv")
    rest = _mm([(h, wts["win_rest"], False)], tm=1024, name=f"{tag}_in_rest")
    fcol, frow8 = _fgate_fwd(rest, wts["fbias"], f"{tag}_fgate")
    frow = frow8.reshape(N_PAIRS, 2, T)
    ya = _pool_fwd(rest, wts["pool_wbd"], wts["pool_scale"], f"{tag}_pool")
    o, lse = _attn_fwd(qkv, fcol, frow, f"{tag}_attn")
    yc, yconv = _conv_fwd(rest, wts["conv_w"], wts["conv_b"], wts["conv_ln_g"], wts["conv_ln_b"], f"{tag}_conv")
    cat = jnp.concatenate([ya, o.astype(CDT), yc], axis=1)
    y = _mm([(cat, wts["w_out"], False)], res=x, tn=512, name=f"{tag}_out")
    return y, (x, h, qkv, rest, fcol, frow, o, lse, yconv, cat)


def _mixer_bwd(saved, wts, gout, tag, dep=None):
    x, h, qkv, rest, fcol, frow, o, lse, yconv, cat = saved
    T = x.shape[0]
    dcat = _mm([(gout, wts["w_out"], True)], tn=512, name=f"{tag}_dcat", dep=dep)
    dwout = _mm_tn(cat, gout, name=f"{tag}_dwout")
    du, dpw, dpsc = _pool_bwd(dcat, rest, wts["pool_wbd"], wts["pool_scale"], f"{tag}_pool_bwd")
    delta = _attn_delta(dcat, o, f"{tag}_attn_delta").reshape(N_PAIRS, 2, T)
    dqt, dk, dv, dfk = _attn_bwd(qkv, dcat, fcol, frow, lse, delta, f"{tag}_attn_bwd")
    dq = dqt.T.astype(CDT)
    dz, dfb = _fgate_bwd(dfk, rest, wts["fbias"], f"{tag}_fgate_bwd")
    da, dg, dcw, dcb, dlg, dlb = _conv_bwd(dcat, yconv, rest, wts["conv_w"], wts["conv_ln_g"], wts["conv_ln_b"],
                                           f"{tag}_conv_bwd")
    dp_qkv = jnp.concatenate([dq, dk, dv], axis=1).astype(CDT)
    dp_rest = jnp.concatenate([du, da, dg, dz], axis=1)
    dwin_qkv = _mm_tn(h, dp_qkv, name=f"{tag}_dwin_qkv")
    dwin_rest = _mm_tn(h, dp_rest, name=f"{tag}_dwin_rest")
    gin, dgamma = _mm_norm_bwd([(dp_qkv, wts["win_qkv"], True), (dp_rest, wts["win_rest"], True)], x, wts["mix_norm"],
                               gout, name=f"{tag}_dh_norm_bwd")
    dwin = _split_win(dwin_qkv, dwin_rest, f"{tag}_dwin_split")
    dpool_w = jnp.stack([dpw[64 * g:64 * g + 64, 64 * g:64 * g + 64] for g in range(4)])
    grads = dict(mix_norm=dgamma[0], w_in=dwin, pool_w=dpool_w, pool_scale=dpsc[0], forget_bias=dfb[0, 0:N_HEADS],
                 conv_w=dcw[0:CONV_K], conv_b=dcb[0], conv_ln_g=dlg[0], conv_ln_b=dlb[0], w_out=dwout)
    return gin, grads


def _rep_layer(rep, l):
    pw = rep["pool_w"][l].astype(CDT)
    wbd = jnp.zeros((256, 256), CDT)
    for g in range(4):
        wbd = lax.dynamic_update_slice(wbd, pw[g], (64 * g, 64 * g))
    return dict(
        ffn1_norm=rep["ffn1_norm"][l][None], ffn2_norm=rep["ffn2_norm"][l][None], mix_norm=rep["mix_norm"][l][None],
        fbias=jnp.pad(rep["forget_bias"][l], (0, LANES - N_HEADS))[None],
        pool_wbd=wbd, pool_scale=rep["pool_scale"][l][None], conv_b=rep["conv_b"][l][None],
        conv_ln_g=rep["conv_ln_g"][l][None], conv_ln_b=rep["conv_ln_b"][l][None])


def _local_step(x, target, rep, weights_for, grads_ready):
    depth = rep["ffn1_norm"].shape[0]
    kept = []
    for l in range(depth):
        r = _rep_layer(rep, l)
        w1, dep = weights_for(l, "ffn1", x)
        x, s1 = _ffn_fwd(x, r["ffn1_norm"], w1["w_gate"], w1["w_up"], w1["w_down"], f"l{l}_ffn1", dep)
        wm, dep = weights_for(l, "mix", x)
        wm = dict(r, win_qkv=wm["win_qkv"], win_rest=wm["win_rest"], w_out=wm["w_out"],
                  conv_w=jnp.pad(wm["conv_w"], ((0, CONV_HALO - CONV_K), (0, 0))))
        x, s2 = _mixer_fwd(x, wm, f"l{l}_mix", dep)
        w2, dep = weights_for(l, "ffn2", x)
        x, s3 = _ffn_fwd(x, r["ffn2_norm"], w2["w_gate"], w2["w_up"], w2["w_down"], f"l{l}_ffn2", dep)
        kept.append((r, w1, wm, w2, s1, s2, s3))
    loss, g, dfinal = _loss_bwd(x, rep["final_norm"][None], target, "loss_head")
    dep = grads_ready(None, "final", dict(final_norm=dfinal[0]))
    for l in reversed(range(depth)):
        r, w1, wm, w2, s1, s2, s3 = kept[l]

        def ffn_grads(which, l=l):
            return lambda gr: grads_ready(l, which, {f"{which}_{k}": v for k, v in gr.items()})

        g, dn = _ffn_bwd(s3, r["ffn2_norm"], w2["w_gate"], w2["w_up"], w2["w_down"], g, f"l{l}_ffn2", dep, ffn_grads("ffn2"))
        grads_ready(l, "norm", dict(ffn2_norm=dn[0]))
        g, gm = _mixer_bwd(s2, wm, g, f"l{l}_mix")
        dep = grads_ready(l, "mix", gm)
        g, dn = _ffn_bwd(s1, r["ffn1_norm"], w1["w_gate"], w1["w_up"], w1["w_down"], g, f"l{l}_ffn1", dep, ffn_grads("ffn1"))
        dep = grads_ready(l, "norm", dict(ffn1_norm=dn[0]))
    return loss, g


def _mesh_pos():
    return lax.axis_index("x"), lax.axis_index("y"), lax.axis_index("c")


def _dev_block(ref, dev, by_rows):
    if by_rows:
        r = ref.shape[1] // N_DEV
        return ref.at[:, pl.ds(dev * r, r), :]
    return ref.at[dev]


def _all_gather(shards, by_rows, name):
    n_arr = len(shards)
    out_shape = [_sds((s.shape[0], N_DEV * s.shape[1], s.shape[2]) if br else (N_DEV,) + s.shape, s.dtype)
                 for s, br in zip(shards, by_rows)]

    def body(*refs):
        xs, outs = refs[:n_arr], refs[n_arr:2 * n_arr]
        send_sems, recv_sems, local_sems = refs[2 * n_arr:]
        x, y, c = _mesh_pos()
        me, sibling = (x, y, c), (x, y, 1 - c)
        chips = [(1 - x, y), (x, 1 - y), (1 - x, 1 - y)]

        def rows(a, px, py, pc):
            return _dev_block(outs[a], 4 * px + 2 * py + pc, by_rows[a])

        def copy(k, a, block, to, src=None):
            return pltpu.make_async_remote_copy(
                src_ref=rows(a, *block) if src is None else src, dst_ref=rows(a, *block),
                send_sem=send_sems.at[k, a], recv_sem=recv_sems.at[k, a],
                device_id=to, device_id_type=pl.DeviceIdType.MESH)

        arrs = range(n_arr)
        mine = [pltpu.make_async_copy(xs[a], rows(a, *me), local_sems.at[a]) for a in arrs]
        for cp in mine:
            cp.start()
        first = [copy(0, a, me, sibling, src=xs[a]) for a in arrs]
        first += [copy(1 + j, a, me, (*chip, c), src=xs[a]) for j, chip in enumerate(chips) for a in arrs]
        for cp in first:
            cp.start()
        passed = []
        for j, chip in enumerate(chips):
            for a in arrs:
                copy(1 + j, a, (*chip, c), me).wait_recv()
                passed.append(copy(4 + j, a, (*chip, c), sibling))
                passed[-1].start()
        for a in arrs:
            copy(0, a, sibling, me).wait_recv()
        for j, chip in enumerate(chips):
            for a in arrs:
                copy(4 + j, a, (*chip, 1 - c), me).wait_recv()
        for cp in first + passed:
            cp.wait_send()
        for cp in mine:
            cp.wait()

    hbm = pl.BlockSpec(memory_space=pl.ANY)
    return pl.pallas_call(
        body, out_shape=out_shape, in_specs=[hbm] * n_arr, out_specs=[hbm] * n_arr,
        scratch_shapes=[pltpu.SemaphoreType.DMA((7, n_arr)), pltpu.SemaphoreType.DMA((7, n_arr)),
                        pltpu.SemaphoreType.DMA((n_arr,))],
        name=name)(*shards)


def _exchange(parts, by_rows, name):
    n_arr = len(parts)
    out_shape = [_sds((N_DEV, p.shape[0], p.shape[1] // N_DEV, p.shape[2]) if br else p.shape, p.dtype)
                 for p, br in zip(parts, by_rows)]

    def body(*refs):
        ps, outs = refs[:n_arr], refs[n_arr:2 * n_arr]
        send_sems, recv_sems, local_sems = refs[2 * n_arr:]
        x, y, c = _mesh_pos()
        my = 4 * x + 2 * y + c
        arrs = range(n_arr)
        mine = [pltpu.make_async_copy(_dev_block(ps[a], my, by_rows[a]), outs[a].at[my], local_sems.at[a]) for a in arrs]
        for cp in mine:
            cp.start()
        copies = []
        for k in range(1, N_DEV):
            px, py, pc = x ^ (k >> 2), y ^ ((k >> 1) & 1), c ^ (k & 1)
            for a in arrs:
                copies.append(pltpu.make_async_remote_copy(
                    src_ref=_dev_block(ps[a], 4 * px + 2 * py + pc, by_rows[a]), dst_ref=outs[a].at[my],
                    send_sem=send_sems.at[k - 1, a], recv_sem=recv_sems.at[k - 1, a],
                    device_id=(px, py, pc), device_id_type=pl.DeviceIdType.MESH))
        for cp in copies:
            cp.start()
        for cp in copies:
            cp.wait()
        for cp in mine:
            cp.wait()

    hbm = pl.BlockSpec(memory_space=pl.ANY)
    return pl.pallas_call(
        body, out_shape=out_shape, in_specs=[hbm] * n_arr, out_specs=[hbm] * n_arr,
        scratch_shapes=[pltpu.SemaphoreType.DMA((7, n_arr)), pltpu.SemaphoreType.DMA((7, n_arr)),
                        pltpu.SemaphoreType.DMA((n_arr,))],
        name=name)(*parts)


def _peer_copies(srcs, lands, send_sems, recv_sems, gather, by_rows):
    n_arr = len(srcs)
    x, y, c = _mesh_pos()
    my = 4 * x + 2 * y + c
    out = []
    for k in range(1, N_DEV):
        px, py, pc = x ^ (k >> 2), y ^ ((k >> 1) & 1), c ^ (k & 1)
        peer = 4 * px + 2 * py + pc
        for a in range(n_arr):
            src = srcs[a] if gather else _dev_block(srcs[a], peer, by_rows[a])
            dst = _dev_block(lands[a], my, by_rows[a]) if gather else lands[a].at[my]
            out.append(pltpu.make_async_remote_copy(
                src_ref=src, dst_ref=dst, send_sem=send_sems.at[(k - 1) * n_arr + a],
                recv_sem=recv_sems.at[(k - 1) * n_arr + a], device_id=(px, py, pc), device_id_type=pl.DeviceIdType.MESH))
    return out


def _land_shape(s, gather, by_rows):
    if gather:
        return (s.shape[0], N_DEV * s.shape[1], s.shape[2]) if by_rows else (N_DEV,) + s.shape
    return (N_DEV, s.shape[0], s.shape[1] // N_DEV, s.shape[2]) if by_rows else s.shape


_HBM = pl.BlockSpec(memory_space=pltpu.HBM)
_SEM = pl.BlockSpec(memory_space=pltpu.SEMAPHORE)


def _xfer_start(srcs, gather, by_rows, name, dep=None):
    n = len(srcs)
    lands = [lax.empty(_land_shape(s, gather, br), s.dtype) for s, br in zip(srcs, by_rows)]
    ins = [pltpu.with_memory_space_constraint(a, pltpu.HBM) for a in list(srcs) + lands]
    dspec, darg = _dep(dep)

    def body(*refs):
        s = 2 * n + len(darg)
        for cp in _peer_copies(refs[:n], refs[n:2 * n], refs[s], refs[s + 1], gather, by_rows):
            cp.start()
        refs[-1][...] = jnp.zeros_like(refs[-1])

    sems = pltpu.SemaphoreType.DMA(((N_DEV - 1) * n,))
    outs = pl.pallas_call(
        body, name=name,
        out_shape=(sems, sems, *[pltpu.HBM(a.shape, a.dtype) for a in ins], _sds((8, LANES), F32)),
        in_specs=[_HBM] * (2 * n) + dspec,
        out_specs=(_SEM, _SEM, *[_HBM] * (2 * n), pl.BlockSpec(memory_space=pltpu.VMEM)),
        input_output_aliases={i: 2 + i for i in range(2 * n)},
        compiler_params=pltpu.CompilerParams(has_side_effects=pltpu.SideEffectType.DATAFLOW_SIDE_EFFECTING))(*ins, *darg)
    return outs[0], outs[1], list(outs[2:-1]), outs[-1]


def _xfer_wait(started, after, gather, by_rows, name):
    send_sems, recv_sems, bufs, _ = started
    n = len(bufs) // 2

    def body(*refs):
        for cp in _peer_copies(refs[:n], refs[n:2 * n], refs[2 * n], refs[2 * n + 1], gather, by_rows):
            cp.wait_send()
            cp.wait_recv()

    outs = pl.pallas_call(
        body, name=name, out_shape=tuple(pltpu.HBM(a.shape, a.dtype) for a in bufs),
        in_specs=[_HBM] * (2 * n) + [_SEM, _SEM, pl.BlockSpec(memory_space=pl.ANY)], out_specs=tuple([_HBM] * (2 * n)),
        input_output_aliases={i: i for i in range(2 * n)},
        compiler_params=pltpu.CompilerParams(has_side_effects=pltpu.SideEffectType.DATAFLOW_SIDE_EFFECTING))(
            *bufs, send_sems, recv_sems, after)
    x, y, c = _mesh_pos()
    my = 4 * x + 2 * y + c
    res = []
    for src, land, br in zip(outs[:n], outs[n:], by_rows):
        zeros = (0,) * (land.ndim - 1)
        if gather and br:
            res.append(lax.dynamic_update_slice(land, src, (0, my * src.shape[1], 0)))
        elif gather:
            res.append(lax.dynamic_update_slice(land, src[None], (my,) + zeros))
        elif br:
            r = src.shape[1] // N_DEV
            own = lax.dynamic_slice(src, (0, my * r, 0), (src.shape[0], r, src.shape[2]))
            res.append(lax.dynamic_update_slice(land, own[None], (my,) + zeros))
        else:
            res.append(lax.dynamic_update_slice(land, lax.dynamic_index_in_dim(src, my, 0, keepdims=True), (my,) + zeros))
    return res


def _adam_update(g, w, m, v):
    c1 = 1.0 - ADAM_B1 ** ADAM_STEP
    c2 = 1.0 - ADAM_B2 ** ADAM_STEP
    nm = ADAM_B1 * m + (1.0 - ADAM_B1) * g
    nv = ADAM_B2 * v + (1.0 - ADAM_B2) * (g * g)
    return -ADAM_LR * ((nm / c1) / (jnp.sqrt(nv / c2) + ADAM_EPS) + ADAM_WD * w), nm, nv


def _adamw_body(p_ref, w_ref, m_ref, v_ref, g_ref, d_ref, nm_ref, nv_ref):
    g = p_ref[0]
    for i in range(1, N_DEV):
        g = g + p_ref[i]
    g_ref[...] = g
    d_ref[...], nm_ref[...], nv_ref[...] = _adam_update(g, w_ref[...], m_ref[...], v_ref[...])


def _adamw(parts, w, m, v, name, tr=1536):
    R = w.shape[0]
    tr = max(t for t in range(8, tr + 1, 8) if R % t == 0)

    def body(*refs):
        _adamw_body(*refs)

    row = pl.BlockSpec((tr, LANES), lambda i: (i, 0))
    return pl.pallas_call(
        body, grid=(R // tr,),
        in_specs=[pl.BlockSpec((N_DEV, tr, LANES), lambda i: (0, i, 0)), row, row, row],
        out_specs=[row, row, row, row], out_shape=[_sds((R, LANES), F32)] * 4,
        name=name, compiler_params=_cp(("parallel",)))(parts, w, m, v)


def _adamw_split(recvs, w, m, v, name, tr):
    depth, r, c = w.shape
    assert depth == len(recvs)
    tr = _tile(r, tr)

    def body(*refs):
        layer = pl.program_id(0)
        for ll in range(depth):
            @pl.when(layer == ll)
            def _(ll=ll):
                _adamw_body(refs[ll], *refs[depth:])

    wspec = pl.BlockSpec((None, tr, c), lambda l, i: (l, i, 0))
    rspecs = [pl.BlockSpec((N_DEV, None, tr, c), lambda l, i, ll=ll, t=t: (0, t, jnp.where(l == ll, i, 0), 0))
              for ll, (_, t) in enumerate(recvs)]
    return pl.pallas_call(
        body, grid=(depth, r // tr), in_specs=rspecs + [wspec, wspec, wspec],
        out_specs=[wspec] * 4, out_shape=[_sds(w.shape, F32)] * 4,
        name=name, compiler_params=_cp(("arbitrary", "arbitrary")))(*[a for a, _ in recvs], w, m, v)


def _merge_win(g, name, tr=256):
    _, nt, K, n = g.shape
    tr = _tile(K, tr)

    def body(g_ref, q_ref, r_ref):
        full = jnp.concatenate([g_ref[j] for j in range(N_DEV)], axis=1)
        q_ref[...] = full[:, 256:1792]
        zpad = jnp.zeros((tr, REST_W - 776), full.dtype)
        r_ref[...] = jnp.concatenate([full[:, 0:256], full[:, 1800:2312], full[:, 1792:1800], zpad], axis=1)

    return pl.pallas_call(
        body, grid=(nt, K // tr),
        in_specs=[pl.BlockSpec((N_DEV, None, tr, n), lambda t, i: (0, t, i, 0))],
        out_specs=[pl.BlockSpec((None, tr, 1536), lambda t, i: (t, i, 0)), pl.BlockSpec((None, tr, REST_W), lambda t, i: (t, i, 0))],
        out_shape=[_sds((nt, K, 1536), g.dtype), _sds((nt, K, REST_W), g.dtype)],
        name=name, compiler_params=_cp(("parallel", "parallel")))(g)


def _split_win(dq, dr, name, tr=256):
    K = dq.shape[0]
    tr = _tile(K, tr)
    n = (dq.shape[1] + 776) // N_DEV

    def body(q_ref, r_ref, o_ref):
        r = r_ref[...]
        full = jnp.concatenate([r[:, 0:256], q_ref[...], r[:, 768:776], r[:, 256:768]], axis=1)
        for j in range(N_DEV):
            o_ref[j] = full[:, n * j:n * (j + 1)]

    return pl.pallas_call(
        body, grid=(K // tr,),
        in_specs=[pl.BlockSpec((tr, dq.shape[1]), lambda i: (i, 0)), pl.BlockSpec((tr, REST_W), lambda i: (i, 0))],
        out_specs=pl.BlockSpec((N_DEV, tr, n), lambda i: (0, i, 0)),
        out_shape=_sds((N_DEV, K, n), F32), name=name, compiler_params=_cp(("parallel",)))(dq, dr)


WEIGHTS = ["ffn1_norm", "ffn1_w_gate", "ffn1_w_up", "ffn1_w_down", "mix_norm", "w_in", "pool_w", "pool_scale",
           "forget_bias", "conv_w", "conv_b", "conv_ln_g", "conv_ln_b", "w_out", "ffn2_norm", "ffn2_w_gate",
           "ffn2_w_up", "ffn2_w_down", "final_norm"]
FFN_PARTS = ("w_gate", "w_up", "w_down")
FFN_T = ["ffn1_w_gate", "ffn1_w_up", "ffn2_w_gate", "ffn2_w_up"]
BIG = FFN_T + ["ffn1_w_down", "ffn2_w_down", "w_in", "w_out"]
SMALL = [n for n in WEIGHTS if n not in BIG]


def _padded(n):
    return -(-n // PACK_ALIGN) * PACK_ALIGN


def _flat_pad(a):
    f = a.reshape(-1)
    return jnp.pad(f, (0, _padded(f.shape[0]) - f.shape[0]))


def _split8(a, axis):
    shp = a.shape
    a = a.reshape(shp[:axis] + (N_DEV, shp[axis] // N_DEV) + shp[axis + 1:])
    return jnp.moveaxis(a, axis, 0)


def _merge8(a, axis):
    a = jnp.moveaxis(a, 0, axis)
    shp = a.shape
    return a.reshape(shp[:axis] + (shp[axis] * shp[axis + 1],) + shp[axis + 2:])


def _pack_small(arrs):
    return jnp.concatenate([_flat_pad(arrs[n]) for n in SMALL]).reshape(-1, LANES)


def _pack_small_parts(grads):
    cols = []
    for n in SMALL:
        g = grads[n]
        if n == "conv_w":
            s = _split8(g, 2).reshape(N_DEV, -1)
        else:
            s = jnp.broadcast_to(g.reshape(1, -1), (N_DEV, g.size))
        cols.append(jnp.pad(s, ((0, 0), (0, _padded(s.shape[1]) - s.shape[1]))))
    return jnp.concatenate(cols, axis=1).reshape(N_DEV, -1, LANES)


def _unpack_small(buf, like):
    flat = buf.reshape(-1)
    out, off = {}, 0
    for n in SMALL:
        size = like[n].size
        out[n] = flat[off:off + size].reshape(like[n].shape)
        off += _padded(size)
    return out


class _Comm:
    def __init__(self, w):
        self.w = w
        self.bf = {n: (jnp.swapaxes(w[n], 1, 2) if n in FFN_T else w[n]).astype(CDT) for n in BIG}
        self.ready = {}
        self.grads = {}

    def _ffn_shards(self, l, which):
        return jnp.stack([self.bf[f"{which}_{k}"][l] for k in FFN_PARTS])

    def _put_ffn(self, l, which, rows, t):
        self.ready[(l, which)] = dict(w_gate=rows[t], w_up=rows[t + 1], w_down=rows[t + 2])

    def weights_for(self, l, stage, x):
        bf = self.bf
        dep = None
        if (l, stage) == (0, "ffn1"):
            gd, = _all_gather([self._ffn_shards(0, "ffn1")], [True], "gather_l0_ffn1")
            self._put_ffn(0, "ffn1", gd, 0)
            self.started = _xfer_start([bf["w_in"][0:1], bf["w_out"][0:1], self.w["conv_w"]], True,
                                       [False, True, False], "gather_mix0_start", dep=gd)
            dep = self.started[3]
        elif (l, stage) == (0, "mix"):
            gi, go, gc = _xfer_wait(self.started, x, True, [False, True, False], "gather_mix0_wait")
            q, r = _merge_win(gi, "merge_l0_w_in")
            self.conv_w = _merge8(gc, 2)
            self.ready[(0, "mix")] = dict(win_qkv=q[0], win_rest=r[0], w_out=go[0], conv_w=self.conv_w[0])
            rows = jnp.concatenate([self._ffn_shards(0, "ffn2"), self._ffn_shards(1, "ffn1"), self._ffn_shards(1, "ffn2")])
            self.started = _xfer_start([rows, bf["w_in"][1:2], bf["w_out"][1:2]], True, [True, False, True],
                                       "gather_rest_start")
            dep = self.started[3]
        elif (l, stage) == (0, "ffn2"):
            gd, gi, go = _xfer_wait(self.started, x, True, [True, False, True], "gather_rest_wait")
            self._put_ffn(0, "ffn2", gd, 0)
            self._put_ffn(1, "ffn1", gd, 3)
            self._put_ffn(1, "ffn2", gd, 6)
            q, r = _merge_win(gi, "merge_l1_w_in")
            self.ready[(1, "mix")] = dict(win_qkv=q[0], win_rest=r[0], w_out=go[0], conv_w=self.conv_w[1])
        return self.ready[(l, stage)], dep

    def grads_ready(self, l, stage, grads):
        for n, v in grads.items():
            self.grads[(l, n)] = v
        gr = self.grads

        def ffn_rows(layer, which, parts=FFN_PARTS):
            return [gr[(layer, f"{which}_{k}")] for k in parts]

        if l == 1 and "ffn1_w_gate" in grads:
            self.sent1 = _xfer_start(
                [jnp.stack(ffn_rows(1, "ffn1") + ffn_rows(1, "ffn2")), gr[(1, "w_in")][:, None], gr[(1, "w_out")][None]],
                False, [True, False, True], "grads_l1_start")
            return self.sent1[3]
        if l == 0 and "ffn2_w_gate" in grads:
            self.sent_ffn2 = _xfer_start([jnp.stack(ffn_rows(0, "ffn2"))], False, [True], "grads_l0_ffn2_start")
            return self.sent_ffn2[3]
        if (l, stage) == (0, "mix"):
            self.sent_mix = _xfer_start([gr[(0, "w_in")][:, None], gr[(0, "w_out")][None]], False, [False, True],
                                        "grads_l0_mix_start")
            return self.sent_mix[3]
        if l == 0 and "ffn1_w_down" in grads:
            self.sent_down = _xfer_start([gr[(0, "ffn1_w_down")][None]], False, [True], "grads_l0_ffn1_down_start")
            return self.sent_down[3]
        if l == 0 and "ffn1_w_gate" in grads:
            self.sent_gu = _xfer_start([jnp.stack(ffn_rows(0, "ffn1", FFN_PARTS[:2]))], False, [True],
                                       "grads_l0_ffn1_gate_up_start")
            return self.sent_gu[3]
        return None

    def finish(self, m, v, after):
        w, gr = self.w, self.grads
        depth = range(w["w_in"].shape[0])
        small = {n: (gr[(None, n)] if n == "final_norm" else jnp.stack([gr[(l, n)] for l in depth])) for n in SMALL}
        r1, i1, o1 = _xfer_wait(self.sent1, after, False, [True, False, True], "grads_l1_wait")
        r2, = _xfer_wait(self.sent_ffn2, after, False, [True], "grads_l0_ffn2_wait")
        i0, o0 = _xfer_wait(self.sent_mix, after, False, [False, True], "grads_l0_mix_wait")

        def adam(n, recvs, tr):
            if n in FFN_T:
                out = _adamw_split(recvs, *[jnp.swapaxes(t[n], 1, 2) for t in (w, m, v)], f"adamw_{n}", tr)
                return [jnp.swapaxes(o, 1, 2) for o in out]
            return _adamw_split(recvs, w[n], m[n], v[n], f"adamw_{n}", tr)

        res = {}
        for t, k in enumerate(FFN_PARTS):
            res[f"ffn2_{k}"] = adam(f"ffn2_{k}", [(r2, t), (r1, 3 + t)], 176)
        res["w_in"] = adam("w_in", [(i0, 0), (i1, 0)], 256)
        res["w_out"] = adam("w_out", [(o0, 0), (o1, 0)], 128)
        rs, = _exchange([_pack_small_parts(small)], [False], "exchange_small")
        r0, = _xfer_wait(self.sent_down, res["w_out"][0], False, [True], "grads_l0_ffn1_down_wait")
        res["ffn1_w_down"] = adam("ffn1_w_down", [(r0, 0), (r1, 2)], 176)
        g0, = _xfer_wait(self.sent_gu, res["ffn1_w_down"][0], False, [True], "grads_l0_ffn1_gate_up_wait")
        res["ffn1_w_gate"] = adam("ffn1_w_gate", [(g0, 0), (r1, 0)], 176)
        res["ffn1_w_up"] = adam("ffn1_w_up", [(g0, 1), (r1, 1)], 176)
        packed = _adamw(rs, _pack_small(w), _pack_small(m), _pack_small(v), "adamw_small")
        unpacked = [_unpack_small(b, w) for b in packed]
        for n in SMALL:
            res[n] = [u[n] for u in unpacked]
        return res


def kernel(x, ffn1_norm, ffn1_w_gate, ffn1_w_up, ffn1_w_down, mix_norm, w_in, pool_w, pool_scale, forget_bias, conv_w, conv_b, conv_ln_g, conv_ln_b, w_out, ffn2_norm, ffn2_w_gate, ffn2_w_up, ffn2_w_down, final_norm, loss_target, m_ffn1_norm, m_ffn1_w_gate, m_ffn1_w_up, m_ffn1_w_down, m_mix_norm, m_w_in, m_pool_w, m_pool_scale, m_forget_bias, m_conv_w, m_conv_b, m_conv_ln_g, m_conv_ln_b, m_w_out, m_ffn2_norm, m_ffn2_w_gate, m_ffn2_w_up, m_ffn2_w_down, m_final_norm, v_ffn1_norm, v_ffn1_w_gate, v_ffn1_w_up, v_ffn1_w_down, v_mix_norm, v_w_in, v_pool_w, v_pool_scale, v_forget_bias, v_conv_w, v_conv_b, v_conv_ln_g, v_conv_ln_b, v_w_out, v_ffn2_norm, v_ffn2_w_gate, v_ffn2_w_up, v_ffn2_w_down, v_final_norm):
    w = dict(zip(WEIGHTS, (ffn1_norm, ffn1_w_gate, ffn1_w_up, ffn1_w_down, mix_norm, w_in, pool_w, pool_scale, forget_bias,
                           conv_w, conv_b, conv_ln_g, conv_ln_b, w_out, ffn2_norm, ffn2_w_gate, ffn2_w_up, ffn2_w_down,
                           final_norm)))
    m = dict(zip(WEIGHTS, (m_ffn1_norm, m_ffn1_w_gate, m_ffn1_w_up, m_ffn1_w_down, m_mix_norm, m_w_in, m_pool_w, m_pool_scale,
                           m_forget_bias, m_conv_w, m_conv_b, m_conv_ln_g, m_conv_ln_b, m_w_out, m_ffn2_norm, m_ffn2_w_gate,
                           m_ffn2_w_up, m_ffn2_w_down, m_final_norm)))
    v = dict(zip(WEIGHTS, (v_ffn1_norm, v_ffn1_w_gate, v_ffn1_w_up, v_ffn1_w_down, v_mix_norm, v_w_in, v_pool_w, v_pool_scale,
                           v_forget_bias, v_conv_w, v_conv_b, v_conv_ln_g, v_conv_ln_b, v_w_out, v_ffn2_norm, v_ffn2_w_gate,
                           v_ffn2_w_up, v_ffn2_w_down, v_final_norm)))
    comm = _Comm(w)
    loss_row, gx = _local_step(x[0], loss_target[0], w, comm.weights_for, comm.grads_ready)
    loss = lax.psum(loss_row[0, 0], ("x", "y", "c"))
    res = comm.finish(m, v, gx)
    return (loss, gx[None], *[res[n][i] for i in range(4) for n in WEIGHTS])
```

```python
import math

import numpy as np
import jax
import jax.numpy as jnp
from jax import lax
from jax.experimental import pallas as pl
from jax.experimental.pallas import tpu as pltpu

F32 = jnp.float32
CDT = jnp.bfloat16
NORM_EPS = 1e-6
N_DEV = 8
LANES = 128
PACK_ALIGN = 8 * LANES
VMEM_LIMIT = 48 * 1024 * 1024

POOL_WINDOWS = (2, 4, 8, 16)
POOL_HALO = 16
CONV_K = 31
CONV_HALO = 32
HEAD_DIM = 64
N_HEADS = 8
N_PAIRS = N_HEADS // 2
ATT_SCALE = 1.0 / math.sqrt(HEAD_DIM)
NEG = -1e30

ADAM_LR, ADAM_B1, ADAM_B2, ADAM_EPS, ADAM_WD, ADAM_STEP = 0.001, 0.9, 0.999, 1e-08, 0.01, 10

REST_W = 896
REST_Z_BLK = 6


def _cp(sem):
    return pltpu.CompilerParams(dimension_semantics=sem, vmem_limit_bytes=VMEM_LIMIT)


def _tile(n, pref):
    t = min(n, pref)
    assert n % t == 0, (n, pref)
    return t


def _sigmoid(x):
    return 1.0 / (1.0 + jnp.exp(-x))


def _sds(shape, dtype):
    return jax.ShapeDtypeStruct(shape, dtype)


_ANY = pl.BlockSpec(memory_space=pl.ANY)


def _dep(dep):
    return ([], []) if dep is None else ([_ANY], [dep])


def _wshape(w):
    return w[0].shape[1:] if isinstance(w, tuple) else w.shape


def _wspec(w, block, index_map):
    if not isinstance(w, tuple):
        return w, pl.BlockSpec(block, index_map)
    arr, t = w
    return arr, pl.BlockSpec((None,) + block, lambda *g: (t,) + index_map(*g))


def _rms_fwd(x, g, name, dep=None):
    T, D = x.shape
    tm = _tile(T, 1024)

    def body(x_ref, g_ref, *rest):
        o_ref = rest[-1]
        xv = x_ref[...]
        r = lax.rsqrt(jnp.mean(xv * xv, axis=-1, keepdims=True) + NORM_EPS)
        o_ref[...] = (xv * r * g_ref[...]).astype(o_ref.dtype)

    dspec, darg = _dep(dep)
    return pl.pallas_call(
        body, grid=(T // tm,),
        in_specs=[pl.BlockSpec((tm, D), lambda i: (i, 0)), pl.BlockSpec((1, D), lambda i: (0, 0))] + dspec,
        out_specs=pl.BlockSpec((tm, D), lambda i: (i, 0)),
        out_shape=_sds((T, D), CDT), name=name, compiler_params=_cp(("parallel",)))(x, g, *darg)


def _rms_bwd(x, g, dh, gres, name):
    T, D = x.shape
    tm = _tile(T, 512)

    def body(x_ref, g_ref, dh_ref, gres_ref, gin_ref, dg_ref):
        i = pl.program_id(0)
        xv = x_ref[...]
        d = dh_ref[...]
        r = lax.rsqrt(jnp.mean(xv * xv, axis=-1, keepdims=True) + NORM_EPS)
        xh = xv * r
        dxh = d * g_ref[...]
        c = jnp.mean(dxh * xh, axis=-1, keepdims=True)
        gin_ref[...] = gres_ref[...] + r * (dxh - xh * c)
        part = jnp.sum(d * xh, axis=0, keepdims=True)

        @pl.when(i == 0)
        def _():
            dg_ref[...] = part

        @pl.when(i > 0)
        def _():
            dg_ref[...] += part

    row = pl.BlockSpec((tm, D), lambda i: (i, 0))
    vec = pl.BlockSpec((1, D), lambda i: (0, 0))
    return pl.pallas_call(
        body, grid=(T // tm,), in_specs=[row, vec, row, row], out_specs=[row, vec],
        out_shape=[_sds((T, D), F32), _sds((1, D), F32)], name=name, compiler_params=_cp(("arbitrary",)))(x, g, dh, gres)


def _loss_bwd(x, g, target, name):
    T, D = x.shape
    tm = _tile(T, 512)

    def body(x_ref, g_ref, t_ref, loss_ref, dx_ref, dg_ref):
        i = pl.program_id(0)
        xv = x_ref[...]
        gv = g_ref[...]
        r = lax.rsqrt(jnp.mean(xv * xv, axis=-1, keepdims=True) + NORM_EPS)
        xh = xv * r
        err = xh * gv - t_ref[...]
        lpart = 0.5 * jnp.sum(jnp.mean(err * err, axis=-1, keepdims=True), axis=0, keepdims=True)
        dy = err * (1.0 / D)
        dxh = dy * gv
        c = jnp.mean(dxh * xh, axis=-1, keepdims=True)
        dx_ref[...] = r * (dxh - xh * c)
        part = jnp.sum(dy * xh, axis=0, keepdims=True)
        lrow = jnp.broadcast_to(lpart, (1, LANES))

        @pl.when(i == 0)
        def _():
            dg_ref[...] = part
            loss_ref[...] = lrow

        @pl.when(i > 0)
        def _():
            dg_ref[...] += part
            loss_ref[...] += lrow

    row = pl.BlockSpec((tm, D), lambda i: (i, 0))
    vec = pl.BlockSpec((1, D), lambda i: (0, 0))
    return pl.pallas_call(
        body, grid=(T // tm,), in_specs=[row, vec, row],
        out_specs=[pl.BlockSpec((1, LANES), lambda i: (0, 0)), row, vec],
        out_shape=[_sds((1, LANES), F32), _sds((T, D), F32), _sds((1, D), F32)],
        name=name, compiler_params=_cp(("arbitrary",)))(x, g, target)


def _mm(pairs, *, name, res=None, alpha=1.0, out_dtype=F32, tm=512, tn=None, dep=None):
    T = pairs[0][0].shape[0]
    N = _wshape(pairs[0][1])[0 if pairs[0][2] else 1]
    tm = _tile(T, tm)
    tn = N if tn is None else _tile(N, tn)
    flags = [p[2] for p in pairs]
    n_in = 2 * len(pairs)

    def body(*refs):
        o_ref = refs[-1]
        acc = None
        for p, bt in enumerate(flags):
            a = refs[2 * p][...].astype(CDT)
            b = refs[2 * p + 1][...]
            dims = (((1,), (1,)), ((), ())) if bt else (((1,), (0,)), ((), ()))
            d = lax.dot_general(a, b, dims, preferred_element_type=F32)
            acc = d if acc is None else acc + d
        if alpha != 1.0:
            acc = acc * alpha
        if res is not None:
            acc = refs[n_in][...] + acc
        o_ref[...] = acc.astype(o_ref.dtype)

    in_specs, args = [], []
    for a, b, bt in pairs:
        K = a.shape[1]
        in_specs.append(pl.BlockSpec((tm, K), lambda i, j: (i, 0)))
        b, bspec = _wspec(b, (tn, K), lambda i, j: (j, 0)) if bt else _wspec(b, (K, tn), lambda i, j: (0, j))
        in_specs.append(bspec)
        args += [a, b]
    if res is not None:
        in_specs.append(pl.BlockSpec((tm, tn), lambda i, j: (i, j)))
        args.append(res)
    dspec, darg = _dep(dep)
    in_specs += dspec
    args += darg
    return pl.pallas_call(
        body, grid=(T // tm, N // tn), in_specs=in_specs,
        out_specs=pl.BlockSpec((tm, tn), lambda i, j: (i, j)),
        out_shape=_sds((T, N), out_dtype), name=name, compiler_params=_cp(("parallel", "arbitrary")))(*args)


def _mm_norm_bwd(pairs, x, g, gres, *, name, tm=256, dep=None):
    T, D = x.shape
    tm = _tile(T, tm)
    n_in = 2 * len(pairs)
    flags = [p[2] for p in pairs]

    def body(*refs):
        x_ref, g_ref, gres_ref = refs[n_in:n_in + 3]
        gin_ref, dg_ref = refs[-2:]
        i = pl.program_id(0)
        d = None
        for p, bt in enumerate(flags):
            dims = (((1,), (1,)), ((), ())) if bt else (((1,), (0,)), ((), ()))
            part = lax.dot_general(refs[2 * p][...].astype(CDT), refs[2 * p + 1][...], dims, preferred_element_type=F32)
            d = part if d is None else d + part
        xv = x_ref[...]
        r = lax.rsqrt(jnp.mean(xv * xv, axis=-1, keepdims=True) + NORM_EPS)
        xh = xv * r
        dxh = d * g_ref[...]
        c = jnp.mean(dxh * xh, axis=-1, keepdims=True)
        gin_ref[...] = gres_ref[...] + r * (dxh - xh * c)
        part = jnp.sum(d * xh, axis=0, keepdims=True)

        @pl.when(i == 0)
        def _():
            dg_ref[...] = part

        @pl.when(i > 0)
        def _():
            dg_ref[...] += part

    in_specs, args = [], []
    for a, b, bt in pairs:
        K = a.shape[1]
        b, bspec = _wspec(b, tuple(_wshape(b)), lambda i: (0, 0))
        in_specs += [pl.BlockSpec((tm, K), lambda i: (i, 0)), bspec]
        args += [a, b]
    row = pl.BlockSpec((tm, D), lambda i: (i, 0))
    vec = pl.BlockSpec((1, D), lambda i: (0, 0))
    dspec, darg = _dep(dep)
    return pl.pallas_call(
        body, grid=(T // tm,), in_specs=in_specs + [row, vec, row] + dspec, out_specs=[row, vec],
        out_shape=[_sds((T, D), F32), _sds((1, D), F32)], name=name,
        compiler_params=_cp(("arbitrary",)))(*args, x, g, gres, *darg)


def _mm_tn(a, b, *, name, alpha=1.0, tk=512, dep=None):
    T, M = a.shape
    N = b.shape[1]
    tm = M if M <= 1024 else M // 2
    tn = N if N <= 1536 else N // 2
    assert M % tm == 0 and N % tn == 0 and tm % LANES == 0 and tn % LANES == 0
    tk = _tile(T, tk)
    nk = T // tk

    def body(a_ref, b_ref, *rest):
        o_ref = rest[-1]
        k = pl.program_id(2)
        d = lax.dot_general(a_ref[...].astype(CDT), b_ref[...].astype(CDT), (((0,), (0,)), ((), ())),
                            preferred_element_type=F32)

        @pl.when(k == 0)
        def _():
            o_ref[...] = d

        @pl.when(k > 0)
        def _():
            o_ref[...] += d

        if alpha != 1.0:
            @pl.when(k == nk - 1)
            def _():
                o_ref[...] *= alpha

    dspec, darg = _dep(dep)
    return pl.pallas_call(
        body, grid=(M // tm, N // tn, nk),
        in_specs=[pl.BlockSpec((tk, tm), lambda i, j, k: (k, i)), pl.BlockSpec((tk, tn), lambda i, j, k: (k, j))] + dspec,
        out_specs=pl.BlockSpec((tm, tn), lambda i, j, k: (i, j)),
        out_shape=_sds((M, N), F32), name=name, compiler_params=_cp(("parallel", "parallel", "arbitrary")))(a, b, *darg)


def _ffn_up(h, wgt, wut, name):
    T, D = h.shape
    Fh = _wshape(wgt)[0]
    tm = _tile(T, 2048)
    tn = _tile(Fh, 256)
    nt = (((1,), (1,)), ((), ()))

    def body(h_ref, wg_ref, wu_ref, a_ref, b_ref, s_ref):
        hv = h_ref[...]
        a = lax.dot_general(hv, wg_ref[...], nt, preferred_element_type=F32)
        b = lax.dot_general(hv, wu_ref[...], nt, preferred_element_type=F32)
        a_ref[...] = a.astype(a_ref.dtype)
        b_ref[...] = b.astype(b_ref.dtype)
        s_ref[...] = (a * _sigmoid(a) * b).astype(s_ref.dtype)

    wgt, gspec = _wspec(wgt, (tn, D), lambda i, j: (j, 0))
    wut, uspec = _wspec(wut, (tn, D), lambda i, j: (j, 0))
    ospec = pl.BlockSpec((tm, tn), lambda i, j: (i, j))
    return pl.pallas_call(
        body, grid=(T // tm, Fh // tn),
        in_specs=[pl.BlockSpec((tm, D), lambda i, j: (i, 0)), gspec, uspec],
        out_specs=[ospec, ospec, ospec],
        out_shape=[_sds((T, Fh), CDT), _sds((T, Fh), CDT), _sds((T, Fh), CDT)],
        name=name, compiler_params=_cp(("parallel", "arbitrary")))(h, wgt, wut)


def _ffn_bwd_ds(gout, wd, a, b, name, dep=None):
    T, D = gout.shape
    Fh = _wshape(wd)[0]
    tm = _tile(T, 1024)
    tn = _tile(Fh, 256)

    def body(g_ref, wd_ref, a_ref, b_ref, *rest):
        da_ref, db_ref = rest[-2:]
        dy = (0.5 * g_ref[...]).astype(CDT)
        ds = lax.dot_general(dy, wd_ref[...], (((1,), (1,)), ((), ())), preferred_element_type=F32)
        av = a_ref[...].astype(F32)
        sg = _sigmoid(av)
        da_ref[...] = (ds * b_ref[...].astype(F32) * (sg * (1.0 + av * (1.0 - sg)))).astype(da_ref.dtype)
        db_ref[...] = (ds * (av * sg)).astype(db_ref.dtype)

    ospec = pl.BlockSpec((tm, tn), lambda i, j: (i, j))
    dspec, darg = _dep(dep)
    wd, wspec = _wspec(wd, (tn, D), lambda i, j: (j, 0))
    return pl.pallas_call(
        body, grid=(T // tm, Fh // tn),
        in_specs=[pl.BlockSpec((tm, D), lambda i, j: (i, 0)), wspec, ospec, ospec] + dspec,
        out_specs=[ospec, ospec],
        out_shape=[_sds((T, Fh), CDT), _sds((T, Fh), CDT)],
        name=name, compiler_params=_cp(("parallel", "arbitrary")))(gout, wd, a, b, *darg)


def _ffn_fwd(x, gamma, wgt, wut, wd, tag, dep=None):
    h = _rms_fwd(x, gamma, f"{tag}_norm", dep)
    a, b, s = _ffn_up(h, wgt, wut, f"{tag}_up")
    y = _mm([(s, wd, False)], res=x, alpha=0.5, tn=512, name=f"{tag}_down")
    return y, (x, h, a, b, s)


def _ffn_bwd(saved, gamma, wgt, wut, wd, gout, tag, dep, on_grads):
    x, h, a, b, s = saved
    dwd = _mm_tn(s, gout, alpha=0.5, name=f"{tag}_dwd", dep=dep)
    da, db = _ffn_bwd_ds(gout, wd, a, b, f"{tag}_bwd_ds", on_grads(dict(w_down=dwd)))
    dwgt = _mm_tn(da, h, name=f"{tag}_dwg")
    dwut = _mm_tn(db, h, name=f"{tag}_dwu")
    dep = on_grads(dict(w_gate=dwgt, w_up=dwut))
    return _mm_norm_bwd([(da, wgt, False), (db, wut, False)], x, gamma, gout, name=f"{tag}_dh_norm_bwd", dep=dep)


def _fgate_fwd(rest, bias, name, bt=512):
    T = rest.shape[0]
    bt = _tile(T, bt)

    def body(z_ref, b_ref, fc_ref, ft_ref, carry):
        i = pl.program_id(0)

        @pl.when(i == 0)
        def _():
            carry[...] = jnp.zeros_like(carry)

        zb = z_ref[...] + b_ref[...]
        e = jnp.exp(-jnp.abs(zb))
        u = 1.0 + e
        log1p_e = jnp.where(u == 1.0, e, jnp.log(u) * (e / (u - 1.0)))
        x = jnp.minimum(zb, 0.0) - log1p_e
        row = lax.broadcasted_iota(jnp.int32, x.shape, 0)
        sh = 1
        while sh < bt:
            x = x + jnp.where(row >= sh, pltpu.roll(x, sh, 0), 0.0)
            sh *= 2
        f = x + carry[...]
        carry[...] = f[bt - 1:bt, :]
        fc_ref[...] = f
        ft_ref[...] = jnp.transpose(f)[0:N_HEADS, :]

    return pl.pallas_call(
        body, grid=(T // bt,),
        in_specs=[pl.BlockSpec((bt, LANES), lambda i: (i, REST_Z_BLK)), pl.BlockSpec((1, LANES), lambda i: (0, 0))],
        out_specs=[pl.BlockSpec((bt, LANES), lambda i: (i, 0)), pl.BlockSpec((N_HEADS, bt), lambda i: (0, i))],
        out_shape=[_sds((T, LANES), F32), _sds((N_HEADS, T), F32)],
        scratch_shapes=[pltpu.VMEM((1, LANES), F32)],
        name=name, compiler_params=_cp(("arbitrary",)))(rest, bias)


def _fgate_bwd(dfk, rest, bias, name, bt=512):
    T = rest.shape[0]
    bt = _tile(T, bt)
    nb = T // bt

    def body(df_ref, z_ref, b_ref, dz_ref, db_ref, carry):
        i = pl.program_id(0)

        @pl.when(i == 0)
        def _():
            carry[...] = jnp.zeros_like(carry)

        dfv = df_ref[...]
        lane = lax.broadcasted_iota(jnp.int32, (bt, LANES), 1)
        x = jnp.zeros((bt, LANES), F32)
        for h in range(N_HEADS):
            x = jnp.where(lane == h, dfv[:, HEAD_DIM * h:HEAD_DIM * h + 1], x)
        row = lax.broadcasted_iota(jnp.int32, x.shape, 0)
        sh = 1
        while sh < bt:
            x = x + jnp.where(row + sh < bt, pltpu.roll(x, bt - sh, 0), 0.0)
            sh *= 2
        dlf = x + carry[...]
        carry[...] = dlf[0:1, :]
        zb = z_ref[...] + b_ref[...]
        dz = jnp.where(lane < N_HEADS, dlf * _sigmoid(-zb), 0.0)
        dz_ref[...] = dz.astype(dz_ref.dtype)
        part = jnp.sum(dz, axis=0, keepdims=True)

        @pl.when(i == 0)
        def _():
            db_ref[...] = part

        @pl.when(i > 0)
        def _():
            db_ref[...] += part

    return pl.pallas_call(
        body, grid=(nb,),
        in_specs=[pl.BlockSpec((bt, 4 * LANES), lambda i: (nb - 1 - i, 0)),
                  pl.BlockSpec((bt, LANES), lambda i: (nb - 1 - i, REST_Z_BLK)),
                  pl.BlockSpec((1, LANES), lambda i: (0, 0))],
        out_specs=[pl.BlockSpec((bt, LANES), lambda i: (nb - 1 - i, 0)), pl.BlockSpec((1, LANES), lambda i: (0, 0))],
        out_shape=[_sds((T, LANES), CDT), _sds((1, LANES), F32)],
        scratch_shapes=[pltpu.VMEM((1, LANES), F32)],
        name=name, compiler_params=_cp(("arbitrary",)))(dfk, rest, bias)


def _by_group(vals, lane):
    out = vals[-1]
    for g in range(len(vals) - 2, -1, -1):
        out = jnp.where(lane // 64 == g, vals[g], out)
    return out


def _pool_counts(t0, n, lane):
    t = t0 + lax.broadcasted_iota(jnp.int32, (n, 256), 0)
    return _by_group([jnp.minimum(t + 1, w) for w in POOL_WINDOWS], lane).astype(F32)


def _pooled(u, halo, i, bt):
    lane = lax.broadcasted_iota(jnp.int32, (bt, 256), 1)
    ext = jnp.concatenate([jnp.where(i > 0, halo, 0.0), u], axis=0)
    sums, s, sh = [], ext, 1
    for _ in POOL_WINDOWS:
        s = s + pltpu.roll(s, sh, 0)
        sums.append(s[POOL_HALO:, :])
        sh *= 2
    return _by_group(sums, lane) / _pool_counts(i * bt, bt, lane) - u


def _pool_fwd(rest, wbd, scale, name, bt=512):
    T = rest.shape[0]
    bt = _tile(T, bt)
    hb = bt // POOL_HALO

    def body(u_ref, halo_ref, w_ref, sc_ref, o_ref):
        i = pl.program_id(0)
        pooled = _pooled(u_ref[...], halo_ref[...], i, bt)
        mixed = jnp.dot(pooled.astype(CDT), w_ref[...], preferred_element_type=F32)
        o_ref[...] = (mixed * sc_ref[...]).astype(o_ref.dtype)

    return pl.pallas_call(
        body, grid=(T // bt,),
        in_specs=[pl.BlockSpec((bt, 256), lambda i: (i, 0)),
                  pl.BlockSpec((POOL_HALO, 256), lambda i: (jnp.maximum(i * hb - 1, 0), 0)),
                  pl.BlockSpec((256, 256), lambda i: (0, 0)), pl.BlockSpec((1, 256), lambda i: (0, 0))],
        out_specs=pl.BlockSpec((bt, 256), lambda i: (i, 0)),
        out_shape=_sds((T, 256), CDT), name=name, compiler_params=_cp(("parallel",)))(rest, rest, wbd, scale)


def _pool_bwd(dcat, rest, wbd, scale, name, bt=512):
    T = rest.shape[0]
    bt = _tile(T, bt)
    hb = bt // POOL_HALO
    nb = T // bt
    n = bt + POOL_HALO

    def body(dy_ref, dyn_ref, u_ref, halo_ref, w_ref, sc_ref, du_ref, dw_ref, dsc_ref):
        i = pl.program_id(0)
        lane = lax.broadcasted_iota(jnp.int32, (bt, 256), 1)
        w = w_ref[...]
        sc = sc_ref[...]
        pooled = _pooled(u_ref[...], halo_ref[...], i, bt)
        pooled_c = pooled.astype(CDT)
        mixed = jnp.dot(pooled_c, w, preferred_element_type=F32)
        dy = dy_ref[...]
        dm = (dy * sc).astype(CDT)
        dsc = jnp.sum(dy * mixed, axis=0, keepdims=True)
        dw = lax.dot_general(pooled_c, dm, (((0,), (0,)), ((), ())), preferred_element_type=F32)
        nt = (((1,), (1,)), ((), ()))
        dpl = lax.dot_general(dm, w, nt, preferred_element_type=F32)
        dmn = (jnp.where(i < nb - 1, dyn_ref[...], 0.0) * sc).astype(CDT)
        dpln = lax.dot_general(dmn, w, nt, preferred_element_type=F32)
        lane_h = lax.broadcasted_iota(jnp.int32, (POOL_HALO, 256), 1)
        ext = jnp.concatenate([dpl / _pool_counts(i * bt, bt, lane),
                               dpln / _pool_counts((i + 1) * bt, POOL_HALO, lane_h)], axis=0)
        sums, s, sh = [], ext, 1
        for _ in POOL_WINDOWS:
            s = s + pltpu.roll(s, n - sh, 0)
            sums.append(s[0:bt, :])
            sh *= 2
        du_ref[...] = (_by_group(sums, lane) - dpl).astype(du_ref.dtype)

        @pl.when(i == 0)
        def _():
            dw_ref[...] = dw
            dsc_ref[...] = dsc

        @pl.when(i > 0)
        def _():
            dw_ref[...] += dw
            dsc_ref[...] += dsc

    full = pl.BlockSpec((256, 256), lambda i: (0, 0))
    vec = pl.BlockSpec((1, 256), lambda i: (0, 0))
    return pl.pallas_call(
        body, grid=(nb,),
        in_specs=[pl.BlockSpec((bt, 256), lambda i: (i, 0)),
                  pl.BlockSpec((POOL_HALO, 256), lambda i: (jnp.minimum((i + 1) * hb, nb * hb - 1), 0)),
                  pl.BlockSpec((bt, 256), lambda i: (i, 0)),
                  pl.BlockSpec((POOL_HALO, 256), lambda i: (jnp.maximum(i * hb - 1, 0), 0)),
                  full, vec],
        out_specs=[pl.BlockSpec((bt, 256), lambda i: (i, 0)), full, vec],
        out_shape=[_sds((T, 256), CDT), _sds((256, 256), F32), _sds((1, 256), F32)],
        name=name, compiler_params=_cp(("arbitrary",)))(dcat, dcat, rest, rest, wbd, scale)


def _glu_ext(a_ref, g_ref, ah_ref, gh_ref, i):
    u = a_ref[...] * _sigmoid(g_ref[...])
    uh = jnp.where(i > 0, ah_ref[...] * _sigmoid(gh_ref[...]), 0.0)
    return jnp.concatenate([uh, u], axis=0)


def _conv_fwd(rest, cw, cb, lg, lb, name, bt=512):
    T = rest.shape[0]
    bt = _tile(T, bt)
    hb = bt // CONV_HALO

    def body(a_ref, g_ref, ah_ref, gh_ref, cw_ref, cb_ref, lg_ref, lb_ref, o_ref, y_ref):
        i = pl.program_id(0)
        ext = _glu_ext(a_ref, g_ref, ah_ref, gh_ref, i)
        w = cw_ref[...]
        acc = w[CONV_K - 1:CONV_K, :] * ext
        for k in range(CONV_K - 1):
            acc = acc + w[k:k + 1, :] * pltpu.roll(ext, CONV_K - 1 - k, 0)
        y = acc[CONV_HALO:, :] + cb_ref[...]
        y_ref[...] = y
        yc = y - jnp.mean(y, axis=-1, keepdims=True)
        yn = yc * lax.rsqrt(jnp.mean(yc * yc, axis=-1, keepdims=True) + NORM_EPS)
        z = yn * lg_ref[...] + lb_ref[...]
        o_ref[...] = (z * _sigmoid(z)).astype(o_ref.dtype)

    def cur(c):
        return pl.BlockSpec((bt, 256), lambda i: (i, c))

    def prev(c):
        return pl.BlockSpec((CONV_HALO, 256), lambda i: (jnp.maximum(i * hb - 1, 0), c))

    vec = pl.BlockSpec((1, 256), lambda i: (0, 0))
    return pl.pallas_call(
        body, grid=(T // bt,),
        in_specs=[cur(1), cur(2), prev(1), prev(2), pl.BlockSpec((CONV_HALO, 256), lambda i: (0, 0)), vec, vec, vec],
        out_specs=[pl.BlockSpec((bt, 256), lambda i: (i, 0)), pl.BlockSpec((bt, 256), lambda i: (i, 0))],
        out_shape=[_sds((T, 256), CDT), _sds((T, 256), F32)],
        name=name, compiler_params=_cp(("parallel",)))(rest, rest, rest, rest, cw, cb, lg, lb)


def _conv_bwd(dcat, yconv, rest, cw, lg, lb, name, bt=512):
    T = rest.shape[0]
    bt = _tile(T, bt)
    hb = bt // CONV_HALO
    nb = T // bt
    n = bt + CONV_HALO

    def body(dy_ref, dyn_ref, y_ref, yn_ref, a_ref, g_ref, ah_ref, gh_ref, cw_ref, lg_ref, lb_ref,
             da_ref, dg_ref, dcw_ref, dcb_ref, dlg_ref, dlb_ref):
        i = pl.program_id(0)
        lgv = lg_ref[...]
        lbv = lb_ref[...]

        def ln_swish_bwd(dout, y):
            yc = y - jnp.mean(y, axis=-1, keepdims=True)
            rs = lax.rsqrt(jnp.mean(yc * yc, axis=-1, keepdims=True) + NORM_EPS)
            yn = yc * rs
            z = yn * lgv + lbv
            sg = _sigmoid(z)
            dz = dout * (sg * (1.0 + z * (1.0 - sg)))
            dyn = dz * lgv
            dyc = rs * (dyn - jnp.mean(dyn, axis=-1, keepdims=True) - yn * jnp.mean(dyn * yn, axis=-1, keepdims=True))
            return dyc, dz, yn

        dyc, dz, yn = ln_swish_bwd(dy_ref[...], y_ref[...])
        dyc_next, _, _ = ln_swish_bwd(dyn_ref[...], yn_ref[...])
        dyc_next = jnp.where(i < nb - 1, dyc_next, 0.0)
        ext_u = _glu_ext(a_ref, g_ref, ah_ref, gh_ref, i)
        ext_d = jnp.concatenate([dyc, dyc_next], axis=0)
        w = cw_ref[...]
        du = w[CONV_K - 1:CONV_K, :] * ext_d
        rows = []
        for k in range(CONV_K):
            s = CONV_K - 1 - k
            if s > 0:
                du = du + w[k:k + 1, :] * pltpu.roll(ext_d, n - s, 0)
                us = pltpu.roll(ext_u, s, 0)[CONV_HALO:, :]
            else:
                us = ext_u[CONV_HALO:, :]
            rows.append(jnp.sum(dyc * us, axis=0, keepdims=True))
        rows.append(jnp.zeros((1, 256), F32))
        dcw = jnp.concatenate(rows, axis=0)
        du = du[0:bt, :]
        av = a_ref[...]
        sg = _sigmoid(g_ref[...])
        da_ref[...] = (du * sg).astype(da_ref.dtype)
        dg_ref[...] = (du * av * (sg * (1.0 - sg))).astype(dg_ref.dtype)
        dcb = jnp.sum(dyc, axis=0, keepdims=True)
        dlg = jnp.sum(dz * yn, axis=0, keepdims=True)
        dlb = jnp.sum(dz, axis=0, keepdims=True)

        @pl.when(i == 0)
        def _():
            dcw_ref[...] = dcw
            dcb_ref[...] = dcb
            dlg_ref[...] = dlg
            dlb_ref[...] = dlb

        @pl.when(i > 0)
        def _():
            dcw_ref[...] += dcw
            dcb_ref[...] += dcb
            dlg_ref[...] += dlg
            dlb_ref[...] += dlb

    def cur(c):
        return pl.BlockSpec((bt, 256), lambda i: (i, c))

    def prev(c):
        return pl.BlockSpec((CONV_HALO, 256), lambda i: (jnp.maximum(i * hb - 1, 0), c))

    def nxt(c):
        return pl.BlockSpec((CONV_HALO, 256), lambda i: (jnp.minimum((i + 1) * hb, nb * hb - 1), c))

    vec = pl.BlockSpec((1, 256), lambda i: (0, 0))
    wfull = pl.BlockSpec((CONV_HALO, 256), lambda i: (0, 0))
    return pl.pallas_call(
        body, grid=(nb,),
        in_specs=[cur(3), nxt(3), cur(0), nxt(0), cur(1), cur(2), prev(1), prev(2), wfull, vec, vec],
        out_specs=[cur(0), cur(0), wfull, vec, vec, vec],
        out_shape=[_sds((T, 256), CDT), _sds((T, 256), CDT), _sds((CONV_HALO, 256), F32),
                   _sds((1, 256), F32), _sds((1, 256), F32), _sds((1, 256), F32)],
        name=name, compiler_params=_cp(("arbitrary",)))(dcat, dcat, yconv, yconv, rest, rest, rest, rest, cw, lg, lb)


def _half_mask(shape, a):
    lane = lax.broadcasted_iota(jnp.int32, shape, 1)
    return (lane // HEAD_DIM) == a


def _attn_fwd(qkv, fcol, frow, name, blk=512):
    T = qkv.shape[0]
    blk = _tile(T, blk)
    nq = T // blk
    nt = (((1,), (1,)), ((), ()))

    def body(q_ref, k_ref, v_ref, fc_ref, fr_ref, o_ref, lse_ref):
        p_id = pl.program_id(0)
        i = pl.program_id(1)
        q2 = q_ref[...]
        fc = fc_ref[...]
        lane = lax.broadcasted_iota(jnp.int32, (blk, LANES), 1)
        tri = lax.broadcasted_iota(jnp.int32, (blk, blk), 1) <= lax.broadcasted_iota(jnp.int32, (blk, blk), 0)
        masks = [_half_mask(q2.shape, a) for a in range(2)]
        qs = [jnp.where(hm, q2, jnp.zeros_like(q2)) * ATT_SCALE for hm in masks]
        fqs = [jnp.sum(jnp.where(lane == 2 * p_id + a, fc, 0.0), axis=1, keepdims=True) for a in range(2)]

        def tile(j, carry, masked):
            cols = pl.ds(pl.multiple_of(j * blk, blk), blk)
            kj = k_ref[cols, :]
            vj = v_ref[cols, :]
            out = []
            for a in range(2):
                m, acc = carry[2 * a:2 * a + 2]
                va = jnp.where(masks[a], vj, jnp.ones_like(vj))
                s = lax.dot_general(qs[a], kj, nt, preferred_element_type=F32) + (fqs[a] - fr_ref[a:a + 1, cols])
                if masked:
                    s = jnp.where(tri, s, NEG)
                m_new = jnp.maximum(m, jnp.max(s, axis=1, keepdims=True))
                alpha = jnp.exp(m - m_new)
                pr = jnp.exp(s - m_new)
                hi = lax.bitcast_convert_type(lax.bitcast_convert_type(pr, jnp.uint32) & jnp.uint32(0xFFFF0000), F32)
                pv = (jnp.dot(hi.astype(CDT), va, preferred_element_type=F32)
                      + jnp.dot((pr - hi).astype(CDT), va, preferred_element_type=F32))
                out += [m_new, alpha * acc + pv]
            return tuple(out)

        init = (jnp.full((blk, 1), NEG, F32), jnp.zeros((blk, LANES), F32)) * 2
        carry = lax.fori_loop(0, i, lambda j, c: tile(j, c, False), init)
        carry = tile(i, carry, True)
        ls = [carry[1][:, HEAD_DIM:HEAD_DIM + 1], carry[3][:, 0:1]]
        lo = lane < HEAD_DIM
        o_ref[...] = jnp.where(lo, carry[1] / ls[0], carry[3] / ls[1])
        lse_t = jnp.transpose(jnp.where(lo, carry[0] + jnp.log(ls[0]), carry[2] + jnp.log(ls[1])))
        lse_ref[...] = jnp.concatenate([lse_t[0:1, :], lse_t[HEAD_DIM:HEAD_DIM + 1, :]], axis=0)

    return pl.pallas_call(
        body, grid=(N_PAIRS, nq),
        in_specs=[pl.BlockSpec((blk, LANES), lambda p, i: (i, p)),
                  pl.BlockSpec((T, LANES), lambda p, i: (0, N_PAIRS + p)),
                  pl.BlockSpec((T, LANES), lambda p, i: (0, 2 * N_PAIRS + p)),
                  pl.BlockSpec((blk, LANES), lambda p, i: (i, 0)),
                  pl.BlockSpec((None, 2, T), lambda p, i: (p, 0, 0))],
        out_specs=[pl.BlockSpec((blk, LANES), lambda p, i: (i, p)), pl.BlockSpec((None, 2, blk), lambda p, i: (p, 0, i))],
        out_shape=[_sds((T, N_PAIRS * LANES), F32), _sds((N_PAIRS, 2, T), F32)],
        name=name, compiler_params=_cp(("parallel", "arbitrary")))(qkv, qkv, qkv, fcol, frow)


def _attn_delta(dcat, o, name, blk=512):
    T = o.shape[0]
    blk = _tile(T, blk)

    def body(d_ref, o_ref, out_ref):
        prod = d_ref[:, 256:768].astype(CDT).astype(F32) * o_ref[...]
        pt = jnp.transpose(prod)
        out_ref[...] = jnp.sum(pt.reshape(N_HEADS, HEAD_DIM, blk), axis=1)

    return pl.pallas_call(
        body, grid=(T // blk,),
        in_specs=[pl.BlockSpec((blk, 1024), lambda i: (i, 0)), pl.BlockSpec((blk, 512), lambda i: (i, 0))],
        out_specs=pl.BlockSpec((N_HEADS, blk), lambda i: (0, i)),
        out_shape=_sds((N_HEADS, T), F32), name=name, compiler_params=_cp(("parallel",)))(dcat, o)


def _attn_bwd(qkv, dcat, fcol, frow, lse, delta, name, blk=512):
    T = qkv.shape[0]
    blk = _tile(T, blk)
    nq = T // blk
    nt = (((1,), (1,)), ((), ()))

    def body(q_ref, do_ref, k_ref, v_ref, fc_ref, fr_ref, lse_ref, dl_ref, dqt_ref, dk_ref, dv_ref, df_ref):
        p_id = pl.program_id(0)
        j = pl.program_id(1)

        @pl.when(j == 0)
        def _():
            dqt_ref[...] = jnp.zeros_like(dqt_ref)

        k2 = k_ref[...]
        v2 = v_ref[...]
        fc = fc_ref[...]
        lane = lax.broadcasted_iota(jnp.int32, (blk, LANES), 1)
        tri = lax.broadcasted_iota(jnp.int32, (blk, blk), 0) <= lax.broadcasted_iota(jnp.int32, (blk, blk), 1)
        masks = [_half_mask(k2.shape, a) for a in range(2)]
        kas = [jnp.where(hm, k2, jnp.zeros_like(k2)) * ATT_SCALE for hm in masks]
        kats = [jnp.transpose(ka) for ka in kas]
        vas = [jnp.where(hm, v2, jnp.zeros_like(v2)) for hm in masks]
        fks = [jnp.sum(jnp.where(lane == 2 * p_id + a, fc, 0.0), axis=1, keepdims=True) for a in range(2)]

        def tile(i, carry, masked):
            rows = pl.ds(pl.multiple_of(i * blk, blk), blk)
            qi = q_ref[rows, :]
            doi = do_ref[rows, :].astype(CDT)
            out = []
            dqt = None
            for a in range(2):
                dk_acc, dv_acc, df_acc = carry[3 * a:3 * a + 3]
                st = lax.dot_general(kas[a], qi, nt, preferred_element_type=F32)
                e = (st + (fr_ref[a:a + 1, rows] - fks[a])) - lse_ref[a:a + 1, rows]
                if masked:
                    e = jnp.where(tri, e, NEG)
                pt = jnp.exp(e)
                dpt = lax.dot_general(vas[a], doi, nt, preferred_element_type=F32)
                ds32 = pt * (dpt - dl_ref[a:a + 1, rows])
                dst = ds32.astype(CDT)
                df_acc = df_acc + jnp.sum(ds32, axis=1, keepdims=True)
                dv_acc = dv_acc + jnp.dot(pt.astype(CDT), doi, preferred_element_type=F32)
                dk_acc = dk_acc + jnp.dot(dst, qi, preferred_element_type=F32)
                part = jnp.dot(kats[a], dst, preferred_element_type=F32)
                dqt = part if dqt is None else dqt + part
                out += [dk_acc, dv_acc, df_acc]
            dqt_ref[:, rows] += dqt
            return tuple(out)

        init = (jnp.zeros((blk, LANES), F32), jnp.zeros((blk, LANES), F32), jnp.zeros((blk, 1), F32)) * 2
        carry = tile(j, init, True)
        carry = lax.fori_loop(j + 1, nq, lambda i, c: tile(i, c, False), carry)
        lo = lane < HEAD_DIM
        dk_ref[...] = (jnp.where(lo, carry[0], carry[3]) * ATT_SCALE).astype(dk_ref.dtype)
        dv_ref[...] = jnp.where(lo, carry[1], carry[4]).astype(dv_ref.dtype)
        df_ref[...] = -jnp.where(lo, carry[2], carry[5])

    res = pl.BlockSpec((T, LANES), lambda p, j: (0, p))
    rows = pl.BlockSpec((None, 2, T), lambda p, j: (p, 0, 0))
    kv_out = pl.BlockSpec((blk, LANES), lambda p, j: (j, p))
    return pl.pallas_call(
        body, grid=(N_PAIRS, nq),
        in_specs=[res, pl.BlockSpec((T, LANES), lambda p, j: (0, 2 + p)),
                  pl.BlockSpec((blk, LANES), lambda p, j: (j, N_PAIRS + p)),
                  pl.BlockSpec((blk, LANES), lambda p, j: (j, 2 * N_PAIRS + p)),
                  pl.BlockSpec((blk, LANES), lambda p, j: (j, 0)), rows, rows, rows],
        out_specs=[pl.BlockSpec((LANES, T), lambda p, j: (p, 0)), kv_out, kv_out, kv_out],
        out_shape=[_sds((N_PAIRS * LANES, T), F32), _sds((T, N_PAIRS * LANES), CDT), _sds((T, N_PAIRS * LANES), CDT),
                   _sds((T, N_PAIRS * LANES), F32)],
        name=name, compiler_params=_cp(("parallel", "arbitrary")))(qkv, dcat, qkv, qkv, fcol, frow, lse, delta)


def _mixer_fwd(x, wts, tag, dep=None):
    T = x.shape[0]
    h = _rms_fwd(x, wts["mix_norm"], f"{tag}_norm", dep)
    qkv = _mm([(h, wts["win_qkv"], False)], out_dtype=CDT, tm=1024, tn=768, name=f"{tag}_in_qkv")
    rest = _mm([(h, wts["win_rest"], False)], tm=1024, name=f"{tag}_in_rest")
    fcol, frow8 = _fgate_fwd(rest, wts["fbias"], f"{tag}_fgate")
    frow = frow8.reshape(N_PAIRS, 2, T)
    ya = _pool_fwd(rest, wts["pool_wbd"], wts["pool_scale"], f"{tag}_pool")
    o, lse = _attn_fwd(qkv, fcol, frow, f"{tag}_attn")
    yc, yconv = _conv_fwd(rest, wts["conv_w"], wts["conv_b"], wts["conv_ln_g"], wts["conv_ln_b"], f"{tag}_conv")
    cat = jnp.concatenate([ya, o.astype(CDT), yc], axis=1)
    y = _mm([(cat, wts["w_out"], False)], res=x, tn=512, name=f"{tag}_out")
    return y, (x, h, qkv, rest, fcol, frow, o, lse, yconv, cat)


def _mixer_bwd(saved, wts, gout, tag, dep=None):
    x, h, qkv, rest, fcol, frow, o, lse, yconv, cat = saved
    T = x.shape[0]
    dcat = _mm([(gout, wts["w_out"], True)], tn=512, name=f"{tag}_dcat", dep=dep)
    dwout = _mm_tn(cat, gout, name=f"{tag}_dwout")
    du, dpw, dpsc = _pool_bwd(dcat, rest, wts["pool_wbd"], wts["pool_scale"], f"{tag}_pool_bwd")
    delta = _attn_delta(dcat, o, f"{tag}_attn_delta").reshape(N_PAIRS, 2, T)
    dqt, dk, dv, dfk = _attn_bwd(qkv, dcat, fcol, frow, lse, delta, f"{tag}_attn_bwd")
    dq = dqt.T.astype(CDT)
    dz, dfb = _fgate_bwd(dfk, rest, wts["fbias"], f"{tag}_fgate_bwd")
    da, dg, dcw, dcb, dlg, dlb = _conv_bwd(dcat, yconv, rest, wts["conv_w"], wts["conv_ln_g"], wts["conv_ln_b"],
                                           f"{tag}_conv_bwd")
    dp_qkv = jnp.concatenate([dq, dk, dv], axis=1).astype(CDT)
    dp_rest = jnp.concatenate([du, da, dg, dz], axis=1)
    dwin_qkv = _mm_tn(h, dp_qkv, name=f"{tag}_dwin_qkv")
    dwin_rest = _mm_tn(h, dp_rest, name=f"{tag}_dwin_rest")
    gin, dgamma = _mm_norm_bwd([(dp_qkv, wts["win_qkv"], True), (dp_rest, wts["win_rest"], True)], x, wts["mix_norm"],
                               gout, name=f"{tag}_dh_norm_bwd")
    dwin = _split_win(dwin_qkv, dwin_rest, f"{tag}_dwin_split")
    dpool_w = jnp.stack([dpw[64 * g:64 * g + 64, 64 * g:64 * g + 64] for g in range(4)])
    grads = dict(mix_norm=dgamma[0], w_in=dwin, pool_w=dpool_w, pool_scale=dpsc[0], forget_bias=dfb[0, 0:N_HEADS],
                 conv_w=dcw[0:CONV_K], conv_b=dcb[0], conv_ln_g=dlg[0], conv_ln_b=dlb[0], w_out=dwout)
    return gin, grads


def _rep_layer(rep, l):
    pw = rep["pool_w"][l].astype(CDT)
    wbd = jnp.zeros((256, 256), CDT)
    for g in range(4):
        wbd = lax.dynamic_update_slice(wbd, pw[g], (64 * g, 64 * g))
    return dict(
        ffn1_norm=rep["ffn1_norm"][l][None], ffn2_norm=rep["ffn2_norm"][l][None], mix_norm=rep["mix_norm"][l][None],
        fbias=jnp.pad(rep["forget_bias"][l], (0, LANES - N_HEADS))[None],
        pool_wbd=wbd, pool_scale=rep["pool_scale"][l][None], conv_b=rep["conv_b"][l][None],
        conv_ln_g=rep["conv_ln_g"][l][None], conv_ln_b=rep["conv_ln_b"][l][None])


def _local_step(x, target, rep, weights_for, grads_ready):
    depth = rep["ffn1_norm"].shape[0]
    kept = []
    for l in range(depth):
        r = _rep_layer(rep, l)
        w1, dep = weights_for(l, "ffn1", x)
        x, s1 = _ffn_fwd(x, r["ffn1_norm"], w1["w_gate"], w1["w_up"], w1["w_down"], f"l{l}_ffn1", dep)
        wm, dep = weights_for(l, "mix", x)
        wm = dict(r, win_qkv=wm["win_qkv"], win_rest=wm["win_rest"], w_out=wm["w_out"],
                  conv_w=jnp.pad(wm["conv_w"], ((0, CONV_HALO - CONV_K), (0, 0))))
        x, s2 = _mixer_fwd(x, wm, f"l{l}_mix", dep)
        w2, dep = weights_for(l, "ffn2", x)
        x, s3 = _ffn_fwd(x, r["ffn2_norm"], w2["w_gate"], w2["w_up"], w2["w_down"], f"l{l}_ffn2", dep)
        kept.append((r, w1, wm, w2, s1, s2, s3))
    loss, g, dfinal = _loss_bwd(x, rep["final_norm"][None], target, "loss_head")
    dep = grads_ready(None, "final", dict(final_norm=dfinal[0]))
    for l in reversed(range(depth)):
        r, w1, wm, w2, s1, s2, s3 = kept[l]

        def ffn_grads(which, l=l):
            return lambda gr: grads_ready(l, which, {f"{which}_{k}": v for k, v in gr.items()})

        g, dn = _ffn_bwd(s3, r["ffn2_norm"], w2["w_gate"], w2["w_up"], w2["w_down"], g, f"l{l}_ffn2", dep, ffn_grads("ffn2"))
        grads_ready(l, "norm", dict(ffn2_norm=dn[0]))
        g, gm = _mixer_bwd(s2, wm, g, f"l{l}_mix")
        dep = grads_ready(l, "mix", gm)
        g, dn = _ffn_bwd(s1, r["ffn1_norm"], w1["w_gate"], w1["w_up"], w1["w_down"], g, f"l{l}_ffn1", dep, ffn_grads("ffn1"))
        dep = grads_ready(l, "norm", dict(ffn1_norm=dn[0]))
    return loss, g


def _mesh_pos():
    return lax.axis_index("x"), lax.axis_index("y"), lax.axis_index("c")


def _dev_block(ref, dev, by_rows):
    if by_rows:
        r = ref.shape[1] // N_DEV
        return ref.at[:, pl.ds(dev * r, r), :]
    return ref.at[dev]


def _all_gather(shards, by_rows, name):
    n_arr = len(shards)
    out_shape = [_sds((s.shape[0], N_DEV * s.shape[1], s.shape[2]) if br else (N_DEV,) + s.shape, s.dtype)
                 for s, br in zip(shards, by_rows)]

    def body(*refs):
        xs, outs = refs[:n_arr], refs[n_arr:2 * n_arr]
        send_sems, recv_sems, local_sems = refs[2 * n_arr:]
        x, y, c = _mesh_pos()
        me, sibling = (x, y, c), (x, y, 1 - c)
        chips = [(1 - x, y), (x, 1 - y), (1 - x, 1 - y)]

        def rows(a, px, py, pc):
            return _dev_block(outs[a], 4 * px + 2 * py + pc, by_rows[a])

        def copy(k, a, block, to, src=None):
            return pltpu.make_async_remote_copy(
                src_ref=rows(a, *block) if src is None else src, dst_ref=rows(a, *block),
                send_sem=send_sems.at[k, a], recv_sem=recv_sems.at[k, a],
                device_id=to, device_id_type=pl.DeviceIdType.MESH)

        arrs = range(n_arr)
        mine = [pltpu.make_async_copy(xs[a], rows(a, *me), local_sems.at[a]) for a in arrs]
        for cp in mine:
            cp.start()
        first = [copy(0, a, me, sibling, src=xs[a]) for a in arrs]
        first += [copy(1 + j, a, me, (*chip, c), src=xs[a]) for j, chip in enumerate(chips) for a in arrs]
        for cp in first:
            cp.start()
        passed = []
        for j, chip in enumerate(chips):
            for a in arrs:
                copy(1 + j, a, (*chip, c), me).wait_recv()
                passed.append(copy(4 + j, a, (*chip, c), sibling))
                passed[-1].start()
        for a in arrs:
            copy(0, a, sibling, me).wait_recv()
        for j, chip in enumerate(chips):
            for a in arrs:
                copy(4 + j, a, (*chip, 1 - c), me).wait_recv()
        for cp in first + passed:
            cp.wait_send()
        for cp in mine:
            cp.wait()

    hbm = pl.BlockSpec(memory_space=pl.ANY)
    return pl.pallas_call(
        body, out_shape=out_shape, in_specs=[hbm] * n_arr, out_specs=[hbm] * n_arr,
        scratch_shapes=[pltpu.SemaphoreType.DMA((7, n_arr)), pltpu.SemaphoreType.DMA((7, n_arr)),
                        pltpu.SemaphoreType.DMA((n_arr,))],
        name=name)(*shards)


def _exchange(parts, by_rows, name):
    n_arr = len(parts)
    out_shape = [_sds((N_DEV, p.shape[0], p.shape[1] // N_DEV, p.shape[2]) if br else p.shape, p.dtype)
                 for p, br in zip(parts, by_rows)]

    def body(*refs):
        ps, outs = refs[:n_arr], refs[n_arr:2 * n_arr]
        send_sems, recv_sems, local_sems = refs[2 * n_arr:]
        x, y, c = _mesh_pos()
        my = 4 * x + 2 * y + c
        arrs = range(n_arr)
        mine = [pltpu.make_async_copy(_dev_block(ps[a], my, by_rows[a]), outs[a].at[my], local_sems.at[a]) for a in arrs]
        for cp in mine:
            cp.start()
        copies = []
        for k in range(1, N_DEV):
            px, py, pc = x ^ (k >> 2), y ^ ((k >> 1) & 1), c ^ (k & 1)
            for a in arrs:
                copies.append(pltpu.make_async_remote_copy(
                    src_ref=_dev_block(ps[a], 4 * px + 2 * py + pc, by_rows[a]), dst_ref=outs[a].at[my],
                    send_sem=send_sems.at[k - 1, a], recv_sem=recv_sems.at[k - 1, a],
                    device_id=(px, py, pc), device_id_type=pl.DeviceIdType.MESH))
        for cp in copies:
            cp.start()
        for cp in copies:
            cp.wait()
        for cp in mine:
            cp.wait()

    hbm = pl.BlockSpec(memory_space=pl.ANY)
    return pl.pallas_call(
        body, out_shape=out_shape, in_specs=[hbm] * n_arr, out_specs=[hbm] * n_arr,
        scratch_shapes=[pltpu.SemaphoreType.DMA((7, n_arr)), pltpu.SemaphoreType.DMA((7, n_arr)),
                        pltpu.SemaphoreType.DMA((n_arr,))],
        name=name)(*parts)


def _peer_copies(srcs, lands, send_sems, recv_sems, gather, by_rows):
    n_arr = len(srcs)
    x, y, c = _mesh_pos()
    my = 4 * x + 2 * y + c
    out = []
    for k in range(1, N_DEV):
        px, py, pc = x ^ (k >> 2), y ^ ((k >> 1) & 1), c ^ (k & 1)
        peer = 4 * px + 2 * py + pc
        for a in range(n_arr):
            src = srcs[a] if gather else _dev_block(srcs[a], peer, by_rows[a])
            dst = _dev_block(lands[a], my, by_rows[a]) if gather else lands[a].at[my]
            out.append(pltpu.make_async_remote_copy(
                src_ref=src, dst_ref=dst, send_sem=send_sems.at[(k - 1) * n_arr + a],
                recv_sem=recv_sems.at[(k - 1) * n_arr + a], device_id=(px, py, pc), device_id_type=pl.DeviceIdType.MESH))
    return out


def _land_shape(s, gather, by_rows):
    if gather:
        return (s.shape[0], N_DEV * s.shape[1], s.shape[2]) if by_rows else (N_DEV,) + s.shape
    return (N_DEV, s.shape[0], s.shape[1] // N_DEV, s.shape[2]) if by_rows else s.shape


_HBM = pl.BlockSpec(memory_space=pltpu.HBM)
_SEM = pl.BlockSpec(memory_space=pltpu.SEMAPHORE)


def _xfer_start(srcs, gather, by_rows, name, dep=None):
    n = len(srcs)
    x, y, c = _mesh_pos()
    my = 4 * x + 2 * y + c
    lands = []
    for src, br in zip(srcs, by_rows):
        land = lax.empty(_land_shape(src, gather, br), src.dtype)
        zeros = (0,) * (land.ndim - 1)
        if gather and br:
            own, at = src, (0, my * src.shape[1], 0)
        elif gather:
            own, at = src[None], (my,) + zeros
        elif br:
            r = src.shape[1] // N_DEV
            own = lax.dynamic_slice(src, (0, my * r, 0), (src.shape[0], r, src.shape[2]))[None]
            at = (my,) + zeros
        else:
            own, at = lax.dynamic_index_in_dim(src, my, 0, keepdims=True), (my,) + zeros
        lands.append(lax.dynamic_update_slice(land, own, at))
    ins = [pltpu.with_memory_space_constraint(a, pltpu.HBM) for a in list(srcs) + lands]
    dspec, darg = _dep(dep)

    def body(*refs):
        s = 2 * n + len(darg)
        for cp in _peer_copies(refs[:n], refs[n:2 * n], refs[s], refs[s + 1], gather, by_rows):
            cp.start()
        refs[-1][...] = jnp.zeros_like(refs[-1])

    sems = pltpu.SemaphoreType.DMA(((N_DEV - 1) * n,))
    outs = pl.pallas_call(
        body, name=name,
        out_shape=(sems, sems, *[pltpu.HBM(a.shape, a.dtype) for a in ins], _sds((8, LANES), F32)),
        in_specs=[_HBM] * (2 * n) + dspec,
        out_specs=(_SEM, _SEM, *[_HBM] * (2 * n), pl.BlockSpec(memory_space=pltpu.VMEM)),
        input_output_aliases={i: 2 + i for i in range(2 * n)},
        compiler_params=pltpu.CompilerParams(has_side_effects=pltpu.SideEffectType.DATAFLOW_SIDE_EFFECTING))(*ins, *darg)
    return outs[0], outs[1], list(outs[2:-1]), outs[-1]


def _xfer_wait(started, after, gather, by_rows, name):
    send_sems, recv_sems, bufs, _ = started
    n = len(bufs) // 2

    def body(*refs):
        for cp in _peer_copies(refs[:n], refs[n:2 * n], refs[2 * n], refs[2 * n + 1], gather, by_rows):
            cp.wait_send()
            cp.wait_recv()

    outs = pl.pallas_call(
        body, name=name, out_shape=tuple(pltpu.HBM(a.shape, a.dtype) for a in bufs),
        in_specs=[_HBM] * (2 * n) + [_SEM, _SEM, pl.BlockSpec(memory_space=pl.ANY)], out_specs=tuple([_HBM] * (2 * n)),
        input_output_aliases={i: i for i in range(2 * n)},
        compiler_params=pltpu.CompilerParams(has_side_effects=pltpu.SideEffectType.DATAFLOW_SIDE_EFFECTING))(
            *bufs, send_sems, recv_sems, after)
    return list(outs[n:])


def _adam_update(g, w, m, v):
    c1 = 1.0 - ADAM_B1 ** ADAM_STEP
    c2 = 1.0 - ADAM_B2 ** ADAM_STEP
    nm = ADAM_B1 * m + (1.0 - ADAM_B1) * g
    nv = ADAM_B2 * v + (1.0 - ADAM_B2) * (g * g)
    return -ADAM_LR * ((nm / c1) / (jnp.sqrt(nv / c2) + ADAM_EPS) + ADAM_WD * w), nm, nv


def _adamw_body(p_ref, w_ref, m_ref, v_ref, g_ref, d_ref, nm_ref, nv_ref):
    g = p_ref[0]
    for i in range(1, N_DEV):
        g = g + p_ref[i]
    g_ref[...] = g
    d_ref[...], nm_ref[...], nv_ref[...] = _adam_update(g, w_ref[...], m_ref[...], v_ref[...])


def _adamw(parts, w, m, v, name, tr=1536):
    R = w.shape[0]
    tr = max(t for t in range(8, tr + 1, 8) if R % t == 0)

    def body(*refs):
        _adamw_body(*refs)

    row = pl.BlockSpec((tr, LANES), lambda i: (i, 0))
    return pl.pallas_call(
        body, grid=(R // tr,),
        in_specs=[pl.BlockSpec((N_DEV, tr, LANES), lambda i: (0, i, 0)), row, row, row],
        out_specs=[row, row, row, row], out_shape=[_sds((R, LANES), F32)] * 4,
        name=name, compiler_params=_cp(("parallel",)))(parts, w, m, v)


def _adamw_split(recvs, w, m, v, name, tr):
    depth, r, c = w.shape
    assert depth == len(recvs)
    tr = _tile(r, tr)

    def body(*refs):
        layer = pl.program_id(0)
        for ll in range(depth):
            @pl.when(layer == ll)
            def _(ll=ll):
                _adamw_body(refs[ll], *refs[depth:])

    wspec = pl.BlockSpec((None, tr, c), lambda l, i: (l, i, 0))
    rspecs = [pl.BlockSpec((N_DEV, None, tr, c), lambda l, i, ll=ll, t=t: (0, t, jnp.where(l == ll, i, 0), 0))
              for ll, (_, t) in enumerate(recvs)]
    return pl.pallas_call(
        body, grid=(depth, r // tr), in_specs=rspecs + [wspec, wspec, wspec],
        out_specs=[wspec] * 4, out_shape=[_sds(w.shape, F32)] * 4,
        name=name, compiler_params=_cp(("arbitrary", "arbitrary")))(*[a for a, _ in recvs], w, m, v)


def _merge_win(g, name, tr=256):
    _, nt, K, n = g.shape
    tr = _tile(K, tr)

    def body(g_ref, q_ref, r_ref):
        full = jnp.concatenate([g_ref[j] for j in range(N_DEV)], axis=1)
        q_ref[...] = full[:, 256:1792]
        zpad = jnp.zeros((tr, REST_W - 776), full.dtype)
        r_ref[...] = jnp.concatenate([full[:, 0:256], full[:, 1800:2312], full[:, 1792:1800], zpad], axis=1)

    return pl.pallas_call(
        body, grid=(nt, K // tr),
        in_specs=[pl.BlockSpec((N_DEV, None, tr, n), lambda t, i: (0, t, i, 0))],
        out_specs=[pl.BlockSpec((None, tr, 1536), lambda t, i: (t, i, 0)), pl.BlockSpec((None, tr, REST_W), lambda t, i: (t, i, 0))],
        out_shape=[_sds((nt, K, 1536), g.dtype), _sds((nt, K, REST_W), g.dtype)],
        name=name, compiler_params=_cp(("parallel", "parallel")))(g)


def _split_win(dq, dr, name, tr=256):
    K = dq.shape[0]
    tr = _tile(K, tr)
    n = (dq.shape[1] + 776) // N_DEV

    def body(q_ref, r_ref, o_ref):
        r = r_ref[...]
        full = jnp.concatenate([r[:, 0:256], q_ref[...], r[:, 768:776], r[:, 256:768]], axis=1)
        for j in range(N_DEV):
            o_ref[j] = full[:, n * j:n * (j + 1)]

    return pl.pallas_call(
        body, grid=(K // tr,),
        in_specs=[pl.BlockSpec((tr, dq.shape[1]), lambda i: (i, 0)), pl.BlockSpec((tr, REST_W), lambda i: (i, 0))],
        out_specs=pl.BlockSpec((N_DEV, tr, n), lambda i: (0, i, 0)),
        out_shape=_sds((N_DEV, K, n), F32), name=name, compiler_params=_cp(("parallel",)))(dq, dr)


WEIGHTS = ["ffn1_norm", "ffn1_w_gate", "ffn1_w_up", "ffn1_w_down", "mix_norm", "w_in", "pool_w", "pool_scale",
           "forget_bias", "conv_w", "conv_b", "conv_ln_g", "conv_ln_b", "w_out", "ffn2_norm", "ffn2_w_gate",
           "ffn2_w_up", "ffn2_w_down", "final_norm"]
FFN_PARTS = ("w_gate", "w_up", "w_down")
FFN_T = ["ffn1_w_gate", "ffn1_w_up", "ffn2_w_gate", "ffn2_w_up"]
BIG = FFN_T + ["ffn1_w_down", "ffn2_w_down", "w_in", "w_out"]
SMALL = [n for n in WEIGHTS if n not in BIG]


def _padded(n):
    return -(-n // PACK_ALIGN) * PACK_ALIGN


def _flat_pad(a):
    f = a.reshape(-1)
    return jnp.pad(f, (0, _padded(f.shape[0]) - f.shape[0]))


def _split8(a, axis):
    shp = a.shape
    a = a.reshape(shp[:axis] + (N_DEV, shp[axis] // N_DEV) + shp[axis + 1:])
    return jnp.moveaxis(a, axis, 0)


def _merge8(a, axis):
    a = jnp.moveaxis(a, 0, axis)
    shp = a.shape
    return a.reshape(shp[:axis] + (shp[axis] * shp[axis + 1],) + shp[axis + 2:])


def _pack_small(arrs):
    return jnp.concatenate([_flat_pad(arrs[n]) for n in SMALL]).reshape(-1, LANES)


def _pack_small_parts(grads):
    cols = []
    for n in SMALL:
        g = grads[n]
        if n == "conv_w":
            s = _split8(g, 2).reshape(N_DEV, -1)
        else:
            s = jnp.broadcast_to(g.reshape(1, -1), (N_DEV, g.size))
        cols.append(jnp.pad(s, ((0, 0), (0, _padded(s.shape[1]) - s.shape[1]))))
    return jnp.concatenate(cols, axis=1).reshape(N_DEV, -1, LANES)


def _unpack_small(buf, like):
    flat = buf.reshape(-1)
    out, off = {}, 0
    for n in SMALL:
        size = like[n].size
        out[n] = flat[off:off + size].reshape(like[n].shape)
        off += _padded(size)
    return out


class _Comm:
    def __init__(self, w):
        self.w = w
        self.bf = {n: (jnp.swapaxes(w[n], 1, 2) if n in FFN_T else w[n]).astype(CDT) for n in BIG}
        self.ready = {}
        self.grads = {}

    def _ffn_shards(self, l, which):
        return jnp.stack([self.bf[f"{which}_{k}"][l] for k in FFN_PARTS])

    def _put_ffn(self, l, which, rows, t):
        self.ready[(l, which)] = dict(w_gate=(rows, t), w_up=(rows, t + 1), w_down=(rows, t + 2))

    def weights_for(self, l, stage, x):
        bf = self.bf
        dep = None
        if (l, stage) == (0, "ffn1"):
            gd, = _all_gather([self._ffn_shards(0, "ffn1")], [True], "gather_l0_ffn1")
            self._put_ffn(0, "ffn1", gd, 0)
            self.started = _xfer_start([bf["w_in"][0:1], bf["w_out"][0:1], self.w["conv_w"]], True,
                                       [False, True, False], "gather_mix0_start", dep=gd)
            dep = self.started[3]
        elif (l, stage) == (0, "mix"):
            gi, go, gc = _xfer_wait(self.started, x, True, [False, True, False], "gather_mix0_wait")
            q, r = _merge_win(gi, "merge_l0_w_in")
            self.conv_w = _merge8(gc, 2)
            self.ready[(0, "mix")] = dict(win_qkv=q[0], win_rest=r[0], w_out=go[0], conv_w=self.conv_w[0])
            rows = jnp.concatenate([self._ffn_shards(0, "ffn2"), self._ffn_shards(1, "ffn1"), self._ffn_shards(1, "ffn2")])
            self.started = _xfer_start([rows, bf["w_in"][1:2], bf["w_out"][1:2]], True, [True, False, True],
                                       "gather_rest_start")
            dep = self.started[3]
        elif (l, stage) == (0, "ffn2"):
            gd, gi, go = _xfer_wait(self.started, x, True, [True, False, True], "gather_rest_wait")
            self._put_ffn(0, "ffn2", gd, 0)
            self._put_ffn(1, "ffn1", gd, 3)
            self._put_ffn(1, "ffn2", gd, 6)
            q, r = _merge_win(gi, "merge_l1_w_in")
            self.ready[(1, "mix")] = dict(win_qkv=q[0], win_rest=r[0], w_out=go[0], conv_w=self.conv_w[1])
        return self.ready[(l, stage)], dep

    def grads_ready(self, l, stage, grads):
        for n, v in grads.items():
            self.grads[(l, n)] = v
        gr = self.grads

        def ffn_rows(layer, which, parts=FFN_PARTS):
            return [gr[(layer, f"{which}_{k}")] for k in parts]

        if l == 1 and "ffn1_w_gate" in grads:
            self.sent1 = _xfer_start(
                [jnp.stack(ffn_rows(1, "ffn1") + ffn_rows(1, "ffn2")), gr[(1, "w_in")][:, None], gr[(1, "w_out")][None]],
                False, [True, False, True], "grads_l1_start")
            return self.sent1[3]
        if l == 0 and "ffn2_w_gate" in grads:
            self.sent_ffn2 = _xfer_start([jnp.stack(ffn_rows(0, "ffn2"))], False, [True], "grads_l0_ffn2_start")
            return self.sent_ffn2[3]
        if (l, stage) == (0, "mix"):
            self.sent_mix = _xfer_start([gr[(0, "w_in")][:, None], gr[(0, "w_out")][None]], False, [False, True],
                                        "grads_l0_mix_start")
            return self.sent_mix[3]
        if l == 0 and "ffn1_w_down" in grads:
            self.sent_down = _xfer_start([gr[(0, "ffn1_w_down")][None]], False, [True], "grads_l0_ffn1_down_start")
            return self.sent_down[3]
        if l == 0 and "ffn1_w_gate" in grads:
            self.sent_gu = _xfer_start([jnp.stack(ffn_rows(0, "ffn1", FFN_PARTS[:2]))], False, [True],
                                       "grads_l0_ffn1_gate_up_start")
            return self.sent_gu[3]
        return None

    def finish(self, m, v, after):
        w, gr = self.w, self.grads
        depth = range(w["w_in"].shape[0])
        small = {n: (gr[(None, n)] if n == "final_norm" else jnp.stack([gr[(l, n)] for l in depth])) for n in SMALL}
        r1, i1, o1 = _xfer_wait(self.sent1, after, False, [True, False, True], "grads_l1_wait")
        r2, = _xfer_wait(self.sent_ffn2, after, False, [True], "grads_l0_ffn2_wait")
        i0, o0 = _xfer_wait(self.sent_mix, after, False, [False, True], "grads_l0_mix_wait")

        def adam(n, recvs, tr):
            if n in FFN_T:
                out = _adamw_split(recvs, *[jnp.swapaxes(t[n], 1, 2) for t in (w, m, v)], f"adamw_{n}", tr)
                return [jnp.swapaxes(o, 1, 2) for o in out]
            return _adamw_split(recvs, w[n], m[n], v[n], f"adamw_{n}", tr)

        res = {}
        for t, k in enumerate(FFN_PARTS):
            res[f"ffn2_{k}"] = adam(f"ffn2_{k}", [(r2, t), (r1, 3 + t)], 176)
        res["w_in"] = adam("w_in", [(i0, 0), (i1, 0)], 256)
        res["w_out"] = adam("w_out", [(o0, 0), (o1, 0)], 128)
        rs, = _exchange([_pack_small_parts(small)], [False], "exchange_small")
        r0, = _xfer_wait(self.sent_down, res["w_out"][0], False, [True], "grads_l0_ffn1_down_wait")
        res["ffn1_w_down"] = adam("ffn1_w_down", [(r0, 0), (r1, 2)], 176)
        g0, = _xfer_wait(self.sent_gu, res["ffn1_w_down"][0], False, [True], "grads_l0_ffn1_gate_up_wait")
        res["ffn1_w_gate"] = adam("ffn1_w_gate", [(g0, 0), (r1, 0)], 176)
        res["ffn1_w_up"] = adam("ffn1_w_up", [(g0, 1), (r1, 1)], 176)
        packed = _adamw(rs, _pack_small(w), _pack_small(m), _pack_small(v), "adamw_small")
        unpacked = [_unpack_small(b, w) for b in packed]
        for n in SMALL:
            res[n] = [u[n] for u in unpacked]
        return res


def kernel(x, ffn1_norm, ffn1_w_gate, ffn1_w_up, ffn1_w_down, mix_norm, w_in, pool_w, pool_scale, forget_bias, conv_w, conv_b, conv_ln_g, conv_ln_b, w_out, ffn2_norm, ffn2_w_gate, ffn2_w_up, ffn2_w_down, final_norm, loss_target, m_ffn1_norm, m_ffn1_w_gate, m_ffn1_w_up, m_ffn1_w_down, m_mix_norm, m_w_in, m_pool_w, m_pool_scale, m_forget_bias, m_conv_w, m_conv_b, m_conv_ln_g, m_conv_ln_b, m_w_out, m_ffn2_norm, m_ffn2_w_gate, m_ffn2_w_up, m_ffn2_w_down, m_final_norm, v_ffn1_norm, v_ffn1_w_gate, v_ffn1_w_up, v_ffn1_w_down, v_mix_norm, v_w_in, v_pool_w, v_pool_scale, v_forget_bias, v_conv_w, v_conv_b, v_conv_ln_g, v_conv_ln_b, v_w_out, v_ffn2_norm, v_ffn2_w_gate, v_ffn2_w_up, v_ffn2_w_down, v_final_norm):
    w = dict(zip(WEIGHTS, (ffn1_norm, ffn1_w_gate, ffn1_w_up, ffn1_w_down, mix_norm, w_in, pool_w, pool_scale, forget_bias,
                           conv_w, conv_b, conv_ln_g, conv_ln_b, w_out, ffn2_norm, ffn2_w_gate, ffn2_w_up, ffn2_w_down,
                           final_norm)))
    m = dict(zip(WEIGHTS, (m_ffn1_norm, m_ffn1_w_gate, m_ffn1_w_up, m_ffn1_w_down, m_mix_norm, m_w_in, m_pool_w, m_pool_scale,
                           m_forget_bias, m_conv_w, m_conv_b, m_conv_ln_g, m_conv_ln_b, m_w_out, m_ffn2_norm, m_ffn2_w_gate,
                           m_ffn2_w_up, m_ffn2_w_down, m_final_norm)))
    v = dict(zip(WEIGHTS, (v_ffn1_norm, v_ffn1_w_gate, v_ffn1_w_up, v_ffn1_w_down, v_mix_norm, v_w_in, v_pool_w, v_pool_scale,
                           v_forget_bias, v_conv_w, v_conv_b, v_conv_ln_g, v_conv_ln_b, v_w_out, v_ffn2_norm, v_ffn2_w_gate,
                           v_ffn2_w_up, v_ffn2_w_down, v_final_norm)))
    comm = _Comm(w)
    loss_row, gx = _local_step(x[0], loss_target[0], w, comm.weights_for, comm.grads_ready)
    loss = lax.psum(loss_row[0, 0], ("x", "y", "c"))
    res = comm.finish(m, v, gx)
    return (loss, gx[None], *[res[n][i] for i in range(4) for n in WEIGHTS])
```

```python
import math

import numpy as np
import jax
import jax.numpy as jnp
from jax import lax
from jax.experimental import pallas as pl
from jax.experimental.pallas import tpu as pltpu

F32 = jnp.float32
CDT = jnp.bfloat16
NORM_EPS = 1e-6
N_DEV = 8
LANES = 128
PACK_ALIGN = 8 * LANES
VMEM_LIMIT = 48 * 1024 * 1024

POOL_WINDOWS = (2, 4, 8, 16)
POOL_HALO = 16
CONV_K = 31
CONV_HALO = 32
HEAD_DIM = 64
N_HEADS = 8
N_PAIRS = N_HEADS // 2
ATT_SCALE = 1.0 / math.sqrt(HEAD_DIM)
NEG = -1e30

ADAM_LR, ADAM_B1, ADAM_B2, ADAM_EPS, ADAM_WD, ADAM_STEP = 0.001, 0.9, 0.999, 1e-08, 0.01, 10

REST_W = 896
REST_Z_BLK = 6


def _cp(sem):
    return pltpu.CompilerParams(dimension_semantics=sem, vmem_limit_bytes=VMEM_LIMIT)


def _tile(n, pref):
    t = min(n, pref)
    assert n % t == 0, (n, pref)
    return t


def _sigmoid(x):
    return 1.0 / (1.0 + jnp.exp(-x))


def _sds(shape, dtype):
    return jax.ShapeDtypeStruct(shape, dtype)


_ANY = pl.BlockSpec(memory_space=pl.ANY)


def _dep(dep):
    return ([], []) if dep is None else ([_ANY], [dep])


def _wshape(w):
    return w[0].shape[1:] if isinstance(w, tuple) else w.shape


def _wspec(w, block, index_map):
    if not isinstance(w, tuple):
        return w, pl.BlockSpec(block, index_map)
    arr, t = w
    return arr, pl.BlockSpec((None,) + block, lambda *g: (t,) + index_map(*g))


def _rms_fwd(x, g, name, dep=None):
    T, D = x.shape
    tm = _tile(T, 1024)

    def body(x_ref, g_ref, *rest):
        o_ref = rest[-1]
        xv = x_ref[...]
        r = lax.rsqrt(jnp.mean(xv * xv, axis=-1, keepdims=True) + NORM_EPS)
        o_ref[...] = (xv * r * g_ref[...]).astype(o_ref.dtype)

    dspec, darg = _dep(dep)
    return pl.pallas_call(
        body, grid=(T // tm,),
        in_specs=[pl.BlockSpec((tm, D), lambda i: (i, 0)), pl.BlockSpec((1, D), lambda i: (0, 0))] + dspec,
        out_specs=pl.BlockSpec((tm, D), lambda i: (i, 0)),
        out_shape=_sds((T, D), CDT), name=name, compiler_params=_cp(("parallel",)))(x, g, *darg)


def _rms_bwd(x, g, dh, gres, name):
    T, D = x.shape
    tm = _tile(T, 512)

    def body(x_ref, g_ref, dh_ref, gres_ref, gin_ref, dg_ref):
        i = pl.program_id(0)
        xv = x_ref[...]
        d = dh_ref[...]
        r = lax.rsqrt(jnp.mean(xv * xv, axis=-1, keepdims=True) + NORM_EPS)
        xh = xv * r
        dxh = d * g_ref[...]
        c = jnp.mean(dxh * xh, axis=-1, keepdims=True)
        gin_ref[...] = gres_ref[...] + r * (dxh - xh * c)
        part = jnp.sum(d * xh, axis=0, keepdims=True)

        @pl.when(i == 0)
        def _():
            dg_ref[...] = part

        @pl.when(i > 0)
        def _():
            dg_ref[...] += part

    row = pl.BlockSpec((tm, D), lambda i: (i, 0))
    vec = pl.BlockSpec((1, D), lambda i: (0, 0))
    return pl.pallas_call(
        body, grid=(T // tm,), in_specs=[row, vec, row, row], out_specs=[row, vec],
        out_shape=[_sds((T, D), F32), _sds((1, D), F32)], name=name, compiler_params=_cp(("arbitrary",)))(x, g, dh, gres)


def _loss_bwd(x, g, target, name):
    T, D = x.shape
    tm = _tile(T, 512)

    def body(x_ref, g_ref, t_ref, loss_ref, dx_ref, dg_ref):
        i = pl.program_id(0)
        xv = x_ref[...]
        gv = g_ref[...]
        r = lax.rsqrt(jnp.mean(xv * xv, axis=-1, keepdims=True) + NORM_EPS)
        xh = xv * r
        err = xh * gv - t_ref[...]
        lpart = 0.5 * jnp.sum(jnp.mean(err * err, axis=-1, keepdims=True), axis=0, keepdims=True)
        dy = err * (1.0 / D)
        dxh = dy * gv
        c = jnp.mean(dxh * xh, axis=-1, keepdims=True)
        dx_ref[...] = r * (dxh - xh * c)
        part = jnp.sum(dy * xh, axis=0, keepdims=True)
        lrow = jnp.broadcast_to(lpart, (1, LANES))

        @pl.when(i == 0)
        def _():
            dg_ref[...] = part
            loss_ref[...] = lrow

        @pl.when(i > 0)
        def _():
            dg_ref[...] += part
            loss_ref[...] += lrow

    row = pl.BlockSpec((tm, D), lambda i: (i, 0))
    vec = pl.BlockSpec((1, D), lambda i: (0, 0))
    return pl.pallas_call(
        body, grid=(T // tm,), in_specs=[row, vec, row],
        out_specs=[pl.BlockSpec((1, LANES), lambda i: (0, 0)), row, vec],
        out_shape=[_sds((1, LANES), F32), _sds((T, D), F32), _sds((1, D), F32)],
        name=name, compiler_params=_cp(("arbitrary",)))(x, g, target)


def _mm(pairs, *, name, res=None, alpha=1.0, out_dtype=F32, tm=512, tn=None, dep=None):
    T = pairs[0][0].shape[0]
    N = _wshape(pairs[0][1])[0 if pairs[0][2] else 1]
    tm = _tile(T, tm)
    tn = N if tn is None else _tile(N, tn)
    flags = [p[2] for p in pairs]
    n_in = 2 * len(pairs)

    def body(*refs):
        o_ref = refs[-1]
        acc = None
        for p, bt in enumerate(flags):
            a = refs[2 * p][...].astype(CDT)
            b = refs[2 * p + 1][...]
            dims = (((1,), (1,)), ((), ())) if bt else (((1,), (0,)), ((), ()))
            d = lax.dot_general(a, b, dims, preferred_element_type=F32)
            acc = d if acc is None else acc + d
        if alpha != 1.0:
            acc = acc * alpha
        if res is not None:
            acc = refs[n_in][...] + acc
        o_ref[...] = acc.astype(o_ref.dtype)

    in_specs, args = [], []
    for a, b, bt in pairs:
        K = a.shape[1]
        in_specs.append(pl.BlockSpec((tm, K), lambda i, j: (i, 0)))
        b, bspec = _wspec(b, (tn, K), lambda i, j: (j, 0)) if bt else _wspec(b, (K, tn), lambda i, j: (0, j))
        in_specs.append(bspec)
        args += [a, b]
    if res is not None:
        in_specs.append(pl.BlockSpec((tm, tn), lambda i, j: (i, j)))
        args.append(res)
    dspec, darg = _dep(dep)
    in_specs += dspec
    args += darg
    return pl.pallas_call(
        body, grid=(T // tm, N // tn), in_specs=in_specs,
        out_specs=pl.BlockSpec((tm, tn), lambda i, j: (i, j)),
        out_shape=_sds((T, N), out_dtype), name=name, compiler_params=_cp(("parallel", "arbitrary")))(*args)


def _mm_norm_bwd(pairs, x, g, gres, *, name, tm=256, dep=None):
    T, D = x.shape
    tm = _tile(T, tm)
    n_in = 2 * len(pairs)
    flags = [p[2] for p in pairs]

    def body(*refs):
        x_ref, g_ref, gres_ref = refs[n_in:n_in + 3]
        gin_ref, dg_ref = refs[-2:]
        i = pl.program_id(0)
        d = None
        for p, bt in enumerate(flags):
            dims = (((1,), (1,)), ((), ())) if bt else (((1,), (0,)), ((), ()))
            part = lax.dot_general(refs[2 * p][...].astype(CDT), refs[2 * p + 1][...], dims, preferred_element_type=F32)
            d = part if d is None else d + part
        xv = x_ref[...]
        r = lax.rsqrt(jnp.mean(xv * xv, axis=-1, keepdims=True) + NORM_EPS)
        xh = xv * r
        dxh = d * g_ref[...]
        c = jnp.mean(dxh * xh, axis=-1, keepdims=True)
        gin_ref[...] = gres_ref[...] + r * (dxh - xh * c)
        part = jnp.sum(d * xh, axis=0, keepdims=True)

        @pl.when(i == 0)
        def _():
            dg_ref[...] = part

        @pl.when(i > 0)
        def _():
            dg_ref[...] += part

    in_specs, args = [], []
    for a, b, bt in pairs:
        K = a.shape[1]
        b, bspec = _wspec(b, tuple(_wshape(b)), lambda i: (0, 0))
        in_specs += [pl.BlockSpec((tm, K), lambda i: (i, 0)), bspec]
        args += [a, b]
    row = pl.BlockSpec((tm, D), lambda i: (i, 0))
    vec = pl.BlockSpec((1, D), lambda i: (0, 0))
    dspec, darg = _dep(dep)
    return pl.pallas_call(
        body, grid=(T // tm,), in_specs=in_specs + [row, vec, row] + dspec, out_specs=[row, vec],
        out_shape=[_sds((T, D), F32), _sds((1, D), F32)], name=name,
        compiler_params=_cp(("arbitrary",)))(*args, x, g, gres, *darg)


def _mm_tn(a, b, *, name, alpha=1.0, tk=512, dep=None):
    T, M = a.shape
    N = b.shape[1]
    tm = M if M <= 1024 else M // 2
    tn = N if N <= 1536 else N // 2
    assert M % tm == 0 and N % tn == 0 and tm % LANES == 0 and tn % LANES == 0
    tk = _tile(T, tk)
    nk = T // tk

    def body(a_ref, b_ref, *rest):
        o_ref = rest[-1]
        k = pl.program_id(2)
        d = lax.dot_general(a_ref[...].astype(CDT), b_ref[...].astype(CDT), (((0,), (0,)), ((), ())),
                            preferred_element_type=F32)

        @pl.when(k == 0)
        def _():
            o_ref[...] = d

        @pl.when(k > 0)
        def _():
            o_ref[...] += d

        if alpha != 1.0:
            @pl.when(k == nk - 1)
            def _():
                o_ref[...] *= alpha

    dspec, darg = _dep(dep)
    return pl.pallas_call(
        body, grid=(M // tm, N // tn, nk),
        in_specs=[pl.BlockSpec((tk, tm), lambda i, j, k: (k, i)), pl.BlockSpec((tk, tn), lambda i, j, k: (k, j))] + dspec,
        out_specs=pl.BlockSpec((tm, tn), lambda i, j, k: (i, j)),
        out_shape=_sds((M, N), F32), name=name, compiler_params=_cp(("parallel", "parallel", "arbitrary")))(a, b, *darg)


def _ffn_up(h, wgt, wut, name):
    T, D = h.shape
    Fh = _wshape(wgt)[0]
    tm = _tile(T, 2048)
    tn = _tile(Fh, 256)
    nt = (((1,), (1,)), ((), ()))

    def body(h_ref, wg_ref, wu_ref, a_ref, b_ref, s_ref):
        hv = h_ref[...]
        a = lax.dot_general(hv, wg_ref[...], nt, preferred_element_type=F32)
        b = lax.dot_general(hv, wu_ref[...], nt, preferred_element_type=F32)
        a_ref[...] = a.astype(a_ref.dtype)
        b_ref[...] = b.astype(b_ref.dtype)
        s_ref[...] = (a * _sigmoid(a) * b).astype(s_ref.dtype)

    wgt, gspec = _wspec(wgt, (tn, D), lambda i, j: (j, 0))
    wut, uspec = _wspec(wut, (tn, D), lambda i, j: (j, 0))
    ospec = pl.BlockSpec((tm, tn), lambda i, j: (i, j))
    return pl.pallas_call(
        body, grid=(T // tm, Fh // tn),
        in_specs=[pl.BlockSpec((tm, D), lambda i, j: (i, 0)), gspec, uspec],
        out_specs=[ospec, ospec, ospec],
        out_shape=[_sds((T, Fh), CDT), _sds((T, Fh), CDT), _sds((T, Fh), CDT)],
        name=name, compiler_params=_cp(("parallel", "arbitrary")))(h, wgt, wut)


def _ffn_bwd_ds(gout, wd, a, b, name, dep=None):
    T, D = gout.shape
    Fh = _wshape(wd)[0]
    tm = _tile(T, 1024)
    tn = _tile(Fh, 256)

    def body(g_ref, wd_ref, a_ref, b_ref, *rest):
        da_ref, db_ref = rest[-2:]
        dy = (0.5 * g_ref[...]).astype(CDT)
        ds = lax.dot_general(dy, wd_ref[...], (((1,), (1,)), ((), ())), preferred_element_type=F32)
        av = a_ref[...].astype(F32)
        sg = _sigmoid(av)
        da_ref[...] = (ds * b_ref[...].astype(F32) * (sg * (1.0 + av * (1.0 - sg)))).astype(da_ref.dtype)
        db_ref[...] = (ds * (av * sg)).astype(db_ref.dtype)

    ospec = pl.BlockSpec((tm, tn), lambda i, j: (i, j))
    dspec, darg = _dep(dep)
    wd, wspec = _wspec(wd, (tn, D), lambda i, j: (j, 0))
    return pl.pallas_call(
        body, grid=(T // tm, Fh // tn),
        in_specs=[pl.BlockSpec((tm, D), lambda i, j: (i, 0)), wspec, ospec, ospec] + dspec,
        out_specs=[ospec, ospec],
        out_shape=[_sds((T, Fh), CDT), _sds((T, Fh), CDT)],
        name=name, compiler_params=_cp(("parallel", "arbitrary")))(gout, wd, a, b, *darg)


def _ffn_fwd(x, gamma, wgt, wut, wd, tag, dep=None):
    h = _rms_fwd(x, gamma, f"{tag}_norm", dep)
    a, b, s = _ffn_up(h, wgt, wut, f"{tag}_up")
    y = _mm([(s, wd, False)], res=x, alpha=0.5, name=f"{tag}_down")
    return y, (x, h, a, b, s)


def _ffn_bwd(saved, gamma, wgt, wut, wd, gout, tag, dep, on_grads):
    x, h, a, b, s = saved
    dwd = _mm_tn(s, gout, alpha=0.5, name=f"{tag}_dwd", dep=dep)
    da, db = _ffn_bwd_ds(gout, wd, a, b, f"{tag}_bwd_ds", on_grads(dict(w_down=dwd)))
    dwgt = _mm_tn(da, h, name=f"{tag}_dwg")
    dwut = _mm_tn(db, h, name=f"{tag}_dwu")
    dep = on_grads(dict(w_gate=dwgt, w_up=dwut))
    return _mm_norm_bwd([(da, wgt, False), (db, wut, False)], x, gamma, gout, name=f"{tag}_dh_norm_bwd", dep=dep)


def _fgate_fwd(rest, bias, name, bt=512):
    T = rest.shape[0]
    bt = _tile(T, bt)

    def body(z_ref, b_ref, fc_ref, ft_ref, carry):
        i = pl.program_id(0)

        @pl.when(i == 0)
        def _():
            carry[...] = jnp.zeros_like(carry)

        zb = z_ref[...] + b_ref[...]
        e = jnp.exp(-jnp.abs(zb))
        u = 1.0 + e
        log1p_e = jnp.where(u == 1.0, e, jnp.log(u) * (e / (u - 1.0)))
        x = jnp.minimum(zb, 0.0) - log1p_e
        row = lax.broadcasted_iota(jnp.int32, x.shape, 0)
        sh = 1
        while sh < bt:
            x = x + jnp.where(row >= sh, pltpu.roll(x, sh, 0), 0.0)
            sh *= 2
        f = x + carry[...]
        carry[...] = f[bt - 1:bt, :]
        fc_ref[...] = f
        ft_ref[...] = jnp.transpose(f)[0:N_HEADS, :]

    return pl.pallas_call(
        body, grid=(T // bt,),
        in_specs=[pl.BlockSpec((bt, LANES), lambda i: (i, REST_Z_BLK)), pl.BlockSpec((1, LANES), lambda i: (0, 0))],
        out_specs=[pl.BlockSpec((bt, LANES), lambda i: (i, 0)), pl.BlockSpec((N_HEADS, bt), lambda i: (0, i))],
        out_shape=[_sds((T, LANES), F32), _sds((N_HEADS, T), F32)],
        scratch_shapes=[pltpu.VMEM((1, LANES), F32)],
        name=name, compiler_params=_cp(("arbitrary",)))(rest, bias)


def _fgate_bwd(dfk, rest, bias, name, bt=512):
    T = rest.shape[0]
    bt = _tile(T, bt)
    nb = T // bt

    def body(df_ref, z_ref, b_ref, dz_ref, db_ref, carry):
        i = pl.program_id(0)

        @pl.when(i == 0)
        def _():
            carry[...] = jnp.zeros_like(carry)

        dfv = df_ref[...]
        lane = lax.broadcasted_iota(jnp.int32, (bt, LANES), 1)
        x = jnp.zeros((bt, LANES), F32)
        for h in range(N_HEADS):
            x = jnp.where(lane == h, dfv[:, HEAD_DIM * h:HEAD_DIM * h + 1], x)
        row = lax.broadcasted_iota(jnp.int32, x.shape, 0)
        sh = 1
        while sh < bt:
            x = x + jnp.where(row + sh < bt, pltpu.roll(x, bt - sh, 0), 0.0)
            sh *= 2
        dlf = x + carry[...]
        carry[...] = dlf[0:1, :]
        zb = z_ref[...] + b_ref[...]
        dz = jnp.where(lane < N_HEADS, dlf * _sigmoid(-zb), 0.0)
        dz_ref[...] = dz.astype(dz_ref.dtype)
        part = jnp.sum(dz, axis=0, keepdims=True)

        @pl.when(i == 0)
        def _():
            db_ref[...] = part

        @pl.when(i > 0)
        def _():
            db_ref[...] += part

    return pl.pallas_call(
        body, grid=(nb,),
        in_specs=[pl.BlockSpec((bt, 4 * LANES), lambda i: (nb - 1 - i, 0)),
                  pl.BlockSpec((bt, LANES), lambda i: (nb - 1 - i, REST_Z_BLK)),
                  pl.BlockSpec((1, LANES), lambda i: (0, 0))],
        out_specs=[pl.BlockSpec((bt, LANES), lambda i: (nb - 1 - i, 0)), pl.BlockSpec((1, LANES), lambda i: (0, 0))],
        out_shape=[_sds((T, LANES), CDT), _sds((1, LANES), F32)],
        scratch_shapes=[pltpu.VMEM((1, LANES), F32)],
        name=name, compiler_params=_cp(("arbitrary",)))(dfk, rest, bias)


def _by_group(vals, lane):
    out = vals[-1]
    for g in range(len(vals) - 2, -1, -1):
        out = jnp.where(lane // 64 == g, vals[g], out)
    return out


def _pool_counts(t0, n, lane):
    t = t0 + lax.broadcasted_iota(jnp.int32, (n, 256), 0)
    return _by_group([jnp.minimum(t + 1, w) for w in POOL_WINDOWS], lane).astype(F32)


def _pooled(u, halo, i, bt):
    lane = lax.broadcasted_iota(jnp.int32, (bt, 256), 1)
    ext = jnp.concatenate([jnp.where(i > 0, halo, 0.0), u], axis=0)
    sums, s, sh = [], ext, 1
    for _ in POOL_WINDOWS:
        s = s + pltpu.roll(s, sh, 0)
        sums.append(s[POOL_HALO:, :])
        sh *= 2
    return _by_group(sums, lane) / _pool_counts(i * bt, bt, lane) - u


def _pool_fwd(rest, wbd, scale, name, bt=512):
    T = rest.shape[0]
    bt = _tile(T, bt)
    hb = bt // POOL_HALO

    def body(u_ref, halo_ref, w_ref, sc_ref, o_ref):
        i = pl.program_id(0)
        pooled = _pooled(u_ref[...], halo_ref[...], i, bt)
        mixed = jnp.dot(pooled.astype(CDT), w_ref[...], preferred_element_type=F32)
        o_ref[...] = (mixed * sc_ref[...]).astype(o_ref.dtype)

    return pl.pallas_call(
        body, grid=(T // bt,),
        in_specs=[pl.BlockSpec((bt, 256), lambda i: (i, 0)),
                  pl.BlockSpec((POOL_HALO, 256), lambda i: (jnp.maximum(i * hb - 1, 0), 0)),
                  pl.BlockSpec((256, 256), lambda i: (0, 0)), pl.BlockSpec((1, 256), lambda i: (0, 0))],
        out_specs=pl.BlockSpec((bt, 256), lambda i: (i, 0)),
        out_shape=_sds((T, 256), CDT), name=name, compiler_params=_cp(("parallel",)))(rest, rest, wbd, scale)


def _pool_bwd(dcat, rest, wbd, scale, name, bt=512):
    T = rest.shape[0]
    bt = _tile(T, bt)
    hb = bt // POOL_HALO
    nb = T // bt
    n = bt + POOL_HALO

    def body(dy_ref, dyn_ref, u_ref, halo_ref, w_ref, sc_ref, du_ref, dw_ref, dsc_ref):
        i = pl.program_id(0)
        lane = lax.broadcasted_iota(jnp.int32, (bt, 256), 1)
        w = w_ref[...]
        sc = sc_ref[...]
        pooled = _pooled(u_ref[...], halo_ref[...], i, bt)
        pooled_c = pooled.astype(CDT)
        mixed = jnp.dot(pooled_c, w, preferred_element_type=F32)
        dy = dy_ref[...]
        dm = (dy * sc).astype(CDT)
        dsc = jnp.sum(dy * mixed, axis=0, keepdims=True)
        dw = lax.dot_general(pooled_c, dm, (((0,), (0,)), ((), ())), preferred_element_type=F32)
        nt = (((1,), (1,)), ((), ()))
        dpl = lax.dot_general(dm, w, nt, preferred_element_type=F32)
        dmn = (jnp.where(i < nb - 1, dyn_ref[...], 0.0) * sc).astype(CDT)
        dpln = lax.dot_general(dmn, w, nt, preferred_element_type=F32)
        lane_h = lax.broadcasted_iota(jnp.int32, (POOL_HALO, 256), 1)
        ext = jnp.concatenate([dpl / _pool_counts(i * bt, bt, lane),
                               dpln / _pool_counts((i + 1) * bt, POOL_HALO, lane_h)], axis=0)
        sums, s, sh = [], ext, 1
        for _ in POOL_WINDOWS:
            s = s + pltpu.roll(s, n - sh, 0)
            sums.append(s[0:bt, :])
            sh *= 2
        du_ref[...] = (_by_group(sums, lane) - dpl).astype(du_ref.dtype)

        @pl.when(i == 0)
        def _():
            dw_ref[...] = dw
            dsc_ref[...] = dsc

        @pl.when(i > 0)
        def _():
            dw_ref[...] += dw
            dsc_ref[...] += dsc

    full = pl.BlockSpec((256, 256), lambda i: (0, 0))
    vec = pl.BlockSpec((1, 256), lambda i: (0, 0))
    return pl.pallas_call(
        body, grid=(nb,),
        in_specs=[pl.BlockSpec((bt, 256), lambda i: (i, 0)),
                  pl.BlockSpec((POOL_HALO, 256), lambda i: (jnp.minimum((i + 1) * hb, nb * hb - 1), 0)),
                  pl.BlockSpec((bt, 256), lambda i: (i, 0)),
                  pl.BlockSpec((POOL_HALO, 256), lambda i: (jnp.maximum(i * hb - 1, 0), 0)),
                  full, vec],
        out_specs=[pl.BlockSpec((bt, 256), lambda i: (i, 0)), full, vec],
        out_shape=[_sds((T, 256), CDT), _sds((256, 256), F32), _sds((1, 256), F32)],
        name=name, compiler_params=_cp(("arbitrary",)))(dcat, dcat, rest, rest, wbd, scale)


def _glu_ext(a_ref, g_ref, ah_ref, gh_ref, i):
    u = a_ref[...] * _sigmoid(g_ref[...])
    uh = jnp.where(i > 0, ah_ref[...] * _sigmoid(gh_ref[...]), 0.0)
    return jnp.concatenate([uh, u], axis=0)


def _conv_fwd(rest, cw, cb, lg, lb, name, bt=512):
    T = rest.shape[0]
    bt = _tile(T, bt)
    hb = bt // CONV_HALO

    def body(a_ref, g_ref, ah_ref, gh_ref, cw_ref, cb_ref, lg_ref, lb_ref, o_ref, y_ref):
        i = pl.program_id(0)
        ext = _glu_ext(a_ref, g_ref, ah_ref, gh_ref, i)
        w = cw_ref[...]
        acc = w[CONV_K - 1:CONV_K, :] * ext
        for k in range(CONV_K - 1):
            acc = acc + w[k:k + 1, :] * pltpu.roll(ext, CONV_K - 1 - k, 0)
        y = acc[CONV_HALO:, :] + cb_ref[...]
        y_ref[...] = y
        yc = y - jnp.mean(y, axis=-1, keepdims=True)
        yn = yc * lax.rsqrt(jnp.mean(yc * yc, axis=-1, keepdims=True) + NORM_EPS)
        z = yn * lg_ref[...] + lb_ref[...]
        o_ref[...] = (z * _sigmoid(z)).astype(o_ref.dtype)

    def cur(c):
        return pl.BlockSpec((bt, 256), lambda i: (i, c))

    def prev(c):
        return pl.BlockSpec((CONV_HALO, 256), lambda i: (jnp.maximum(i * hb - 1, 0), c))

    vec = pl.BlockSpec((1, 256), lambda i: (0, 0))
    return pl.pallas_call(
        body, grid=(T // bt,),
        in_specs=[cur(1), cur(2), prev(1), prev(2), pl.BlockSpec((CONV_HALO, 256), lambda i: (0, 0)), vec, vec, vec],
        out_specs=[pl.BlockSpec((bt, 256), lambda i: (i, 0)), pl.BlockSpec((bt, 256), lambda i: (i, 0))],
        out_shape=[_sds((T, 256), CDT), _sds((T, 256), F32)],
        name=name, compiler_params=_cp(("parallel",)))(rest, rest, rest, rest, cw, cb, lg, lb)


def _conv_bwd(dcat, yconv, rest, cw, lg, lb, name, bt=512):
    T = rest.shape[0]
    bt = _tile(T, bt)
    hb = bt // CONV_HALO
    nb = T // bt
    n = bt + CONV_HALO

    def body(dy_ref, dyn_ref, y_ref, yn_ref, a_ref, g_ref, ah_ref, gh_ref, cw_ref, lg_ref, lb_ref,
             da_ref, dg_ref, dcw_ref, dcb_ref, dlg_ref, dlb_ref):
        i = pl.program_id(0)
        lgv = lg_ref[...]
        lbv = lb_ref[...]

        def ln_swish_bwd(dout, y):
            yc = y - jnp.mean(y, axis=-1, keepdims=True)
            rs = lax.rsqrt(jnp.mean(yc * yc, axis=-1, keepdims=True) + NORM_EPS)
            yn = yc * rs
            z = yn * lgv + lbv
            sg = _sigmoid(z)
            dz = dout * (sg * (1.0 + z * (1.0 - sg)))
            dyn = dz * lgv
            dyc = rs * (dyn - jnp.mean(dyn, axis=-1, keepdims=True) - yn * jnp.mean(dyn * yn, axis=-1, keepdims=True))
            return dyc, dz, yn

        dyc, dz, yn = ln_swish_bwd(dy_ref[...], y_ref[...])
        dyc_next, _, _ = ln_swish_bwd(dyn_ref[...], yn_ref[...])
        dyc_next = jnp.where(i < nb - 1, dyc_next, 0.0)
        ext_u = _glu_ext(a_ref, g_ref, ah_ref, gh_ref, i)
        ext_d = jnp.concatenate([dyc, dyc_next], axis=0)
        w = cw_ref[...]
        du = w[CONV_K - 1:CONV_K, :] * ext_d
        rows = []
        for k in range(CONV_K):
            s = CONV_K - 1 - k
            if s > 0:
                du = du + w[k:k + 1, :] * pltpu.roll(ext_d, n - s, 0)
                us = pltpu.roll(ext_u, s, 0)[CONV_HALO:, :]
            else:
                us = ext_u[CONV_HALO:, :]
            rows.append(jnp.sum(dyc * us, axis=0, keepdims=True))
        rows.append(jnp.zeros((1, 256), F32))
        dcw = jnp.concatenate(rows, axis=0)
        du = du[0:bt, :]
        av = a_ref[...]
        sg = _sigmoid(g_ref[...])
        da_ref[...] = (du * sg).astype(da_ref.dtype)
        dg_ref[...] = (du * av * (sg * (1.0 - sg))).astype(dg_ref.dtype)
        dcb = jnp.sum(dyc, axis=0, keepdims=True)
        dlg = jnp.sum(dz * yn, axis=0, keepdims=True)
        dlb = jnp.sum(dz, axis=0, keepdims=True)

        @pl.when(i == 0)
        def _():
            dcw_ref[...] = dcw
            dcb_ref[...] = dcb
            dlg_ref[...] = dlg
            dlb_ref[...] = dlb

        @pl.when(i > 0)
        def _():
            dcw_ref[...] += dcw
            dcb_ref[...] += dcb
            dlg_ref[...] += dlg
            dlb_ref[...] += dlb

    def cur(c):
        return pl.BlockSpec((bt, 256), lambda i: (i, c))

    def prev(c):
        return pl.BlockSpec((CONV_HALO, 256), lambda i: (jnp.maximum(i * hb - 1, 0), c))

    def nxt(c):
        return pl.BlockSpec((CONV_HALO, 256), lambda i: (jnp.minimum((i + 1) * hb, nb * hb - 1), c))

    vec = pl.BlockSpec((1, 256), lambda i: (0, 0))
    wfull = pl.BlockSpec((CONV_HALO, 256), lambda i: (0, 0))
    return pl.pallas_call(
        body, grid=(nb,),
        in_specs=[cur(3), nxt(3), cur(0), nxt(0), cur(1), cur(2), prev(1), prev(2), wfull, vec, vec],
        out_specs=[cur(0), cur(0), wfull, vec, vec, vec],
        out_shape=[_sds((T, 256), CDT), _sds((T, 256), CDT), _sds((CONV_HALO, 256), F32),
                   _sds((1, 256), F32), _sds((1, 256), F32), _sds((1, 256), F32)],
        name=name, compiler_params=_cp(("arbitrary",)))(dcat, dcat, yconv, yconv, rest, rest, rest, rest, cw, lg, lb)


def _half_mask(shape, a):
    lane = lax.broadcasted_iota(jnp.int32, shape, 1)
    return (lane // HEAD_DIM) == a


def _attn_fwd(qkv, fcol, frow, name, blk=512):
    T = qkv.shape[0]
    blk = _tile(T, blk)
    nq = T // blk
    nt = (((1,), (1,)), ((), ()))

    def body(q_ref, k_ref, v_ref, fc_ref, fr_ref, o_ref, lse_ref):
        p_id = pl.program_id(0)
        i = pl.program_id(1)
        q2 = q_ref[...]
        fc = fc_ref[...]
        lane = lax.broadcasted_iota(jnp.int32, (blk, LANES), 1)
        tri = lax.broadcasted_iota(jnp.int32, (blk, blk), 1) <= lax.broadcasted_iota(jnp.int32, (blk, blk), 0)
        masks = [_half_mask(q2.shape, a) for a in range(2)]
        qs = [jnp.where(hm, q2, jnp.zeros_like(q2)) * ATT_SCALE for hm in masks]
        fqs = [jnp.sum(jnp.where(lane == 2 * p_id + a, fc, 0.0), axis=1, keepdims=True) for a in range(2)]

        def tile(j, carry, masked):
            cols = pl.ds(pl.multiple_of(j * blk, blk), blk)
            kj = k_ref[cols, :]
            vj = v_ref[cols, :]
            out = []
            for a in range(2):
                m, acc = carry[2 * a:2 * a + 2]
                va = jnp.where(masks[a], vj, jnp.ones_like(vj))
                s = lax.dot_general(qs[a], kj, nt, preferred_element_type=F32) + (fqs[a] - fr_ref[a:a + 1, cols])
                if masked:
                    s = jnp.where(tri, s, NEG)
                m_new = jnp.maximum(m, jnp.max(s, axis=1, keepdims=True))
                alpha = jnp.exp(m - m_new)
                pr = jnp.exp(s - m_new)
                hi = lax.bitcast_convert_type(lax.bitcast_convert_type(pr, jnp.uint32) & jnp.uint32(0xFFFF0000), F32)
                pv = (jnp.dot(hi.astype(CDT), va, preferred_element_type=F32)
                      + jnp.dot((pr - hi).astype(CDT), va, preferred_element_type=F32))
                out += [m_new, alpha * acc + pv]
            return tuple(out)

        init = (jnp.full((blk, 1), NEG, F32), jnp.zeros((blk, LANES), F32)) * 2
        carry = lax.fori_loop(0, i, lambda j, c: tile(j, c, False), init)
        carry = tile(i, carry, True)
        ls = [carry[1][:, HEAD_DIM:HEAD_DIM + 1], carry[3][:, 0:1]]
        lo = lane < HEAD_DIM
        o_ref[...] = jnp.where(lo, carry[1] / ls[0], carry[3] / ls[1])
        lse_t = jnp.transpose(jnp.where(lo, carry[0] + jnp.log(ls[0]), carry[2] + jnp.log(ls[1])))
        lse_ref[...] = jnp.concatenate([lse_t[0:1, :], lse_t[HEAD_DIM:HEAD_DIM + 1, :]], axis=0)

    return pl.pallas_call(
        body, grid=(N_PAIRS, nq),
        in_specs=[pl.BlockSpec((blk, LANES), lambda p, i: (i, p)),
                  pl.BlockSpec((T, LANES), lambda p, i: (0, N_PAIRS + p)),
                  pl.BlockSpec((T, LANES), lambda p, i: (0, 2 * N_PAIRS + p)),
                  pl.BlockSpec((blk, LANES), lambda p, i: (i, 0)),
                  pl.BlockSpec((None, 2, T), lambda p, i: (p, 0, 0))],
        out_specs=[pl.BlockSpec((blk, LANES), lambda p, i: (i, p)), pl.BlockSpec((None, 2, blk), lambda p, i: (p, 0, i))],
        out_shape=[_sds((T, N_PAIRS * LANES), F32), _sds((N_PAIRS, 2, T), F32)],
        name=name, compiler_params=_cp(("parallel", "arbitrary")))(qkv, qkv, qkv, fcol, frow)


def _attn_delta(dcat, o, name, blk=512):
    T = o.shape[0]
    blk = _tile(T, blk)

    def body(d_ref, o_ref, out_ref):
        prod = d_ref[:, 256:768].astype(CDT).astype(F32) * o_ref[...]
        pt = jnp.transpose(prod)
        out_ref[...] = jnp.sum(pt.reshape(N_HEADS, HEAD_DIM, blk), axis=1)

    return pl.pallas_call(
        body, grid=(T // blk,),
        in_specs=[pl.BlockSpec((blk, 1024), lambda i: (i, 0)), pl.BlockSpec((blk, 512), lambda i: (i, 0))],
        out_specs=pl.BlockSpec((N_HEADS, blk), lambda i: (0, i)),
        out_shape=_sds((N_HEADS, T), F32), name=name, compiler_params=_cp(("parallel",)))(dcat, o)


def _attn_bwd(qkv, dcat, fcol, frow, lse, delta, name, blk=512):
    T = qkv.shape[0]
    blk = _tile(T, blk)
    nq = T // blk
    nt = (((1,), (1,)), ((), ()))

    def body(q_ref, do_ref, k_ref, v_ref, fc_ref, fr_ref, lse_ref, dl_ref, dqt_ref, dk_ref, dv_ref, df_ref):
        p_id = pl.program_id(0)
        j = pl.program_id(1)

        @pl.when(j == 0)
        def _():
            dqt_ref[...] = jnp.zeros_like(dqt_ref)

        k2 = k_ref[...]
        v2 = v_ref[...]
        fc = fc_ref[...]
        lane = lax.broadcasted_iota(jnp.int32, (blk, LANES), 1)
        tri = lax.broadcasted_iota(jnp.int32, (blk, blk), 0) <= lax.broadcasted_iota(jnp.int32, (blk, blk), 1)
        masks = [_half_mask(k2.shape, a) for a in range(2)]
        kas = [jnp.where(hm, k2, jnp.zeros_like(k2)) * ATT_SCALE for hm in masks]
        kats = [jnp.transpose(ka) for ka in kas]
        vas = [jnp.where(hm, v2, jnp.zeros_like(v2)) for hm in masks]
        fks = [jnp.sum(jnp.where(lane == 2 * p_id + a, fc, 0.0), axis=1, keepdims=True) for a in range(2)]

        def tile(i, carry, masked):
            rows = pl.ds(pl.multiple_of(i * blk, blk), blk)
            qi = q_ref[rows, :]
            doi = do_ref[rows, :].astype(CDT)
            out = []
            dqt = None
            for a in range(2):
                dk_acc, dv_acc, df_acc = carry[3 * a:3 * a + 3]
                st = lax.dot_general(kas[a], qi, nt, preferred_element_type=F32)
                e = (st + (fr_ref[a:a + 1, rows] - fks[a])) - lse_ref[a:a + 1, rows]
                if masked:
                    e = jnp.where(tri, e, NEG)
                pt = jnp.exp(e)
                dpt = lax.dot_general(vas[a], doi, nt, preferred_element_type=F32)
                ds32 = pt * (dpt - dl_ref[a:a + 1, rows])
                dst = ds32.astype(CDT)
                df_acc = df_acc + jnp.sum(ds32, axis=1, keepdims=True)
                dv_acc = dv_acc + jnp.dot(pt.astype(CDT), doi, preferred_element_type=F32)
                dk_acc = dk_acc + jnp.dot(dst, qi, preferred_element_type=F32)
                part = jnp.dot(kats[a], dst, preferred_element_type=F32)
                dqt = part if dqt is None else dqt + part
                out += [dk_acc, dv_acc, df_acc]
            dqt_ref[:, rows] += dqt
            return tuple(out)

        init = (jnp.zeros((blk, LANES), F32), jnp.zeros((blk, LANES), F32), jnp.zeros((blk, 1), F32)) * 2
        carry = tile(j, init, True)
        carry = lax.fori_loop(j + 1, nq, lambda i, c: tile(i, c, False), carry)
        lo = lane < HEAD_DIM
        dk_ref[...] = (jnp.where(lo, carry[0], carry[3]) * ATT_SCALE).astype(dk_ref.dtype)
        dv_ref[...] = jnp.where(lo, carry[1], carry[4]).astype(dv_ref.dtype)
        df_ref[...] = -jnp.where(lo, carry[2], carry[5])

    res = pl.BlockSpec((T, LANES), lambda p, j: (0, p))
    rows = pl.BlockSpec((None, 2, T), lambda p, j: (p, 0, 0))
    kv_out = pl.BlockSpec((blk, LANES), lambda p, j: (j, p))
    return pl.pallas_call(
        body, grid=(N_PAIRS, nq),
        in_specs=[res, pl.BlockSpec((T, LANES), lambda p, j: (0, 2 + p)),
                  pl.BlockSpec((blk, LANES), lambda p, j: (j, N_PAIRS + p)),
                  pl.BlockSpec((blk, LANES), lambda p, j: (j, 2 * N_PAIRS + p)),
                  pl.BlockSpec((blk, LANES), lambda p, j: (j, 0)), rows, rows, rows],
        out_specs=[pl.BlockSpec((LANES, T), lambda p, j: (p, 0)), kv_out, kv_out, kv_out],
        out_shape=[_sds((N_PAIRS * LANES, T), F32), _sds((T, N_PAIRS * LANES), CDT), _sds((T, N_PAIRS * LANES), CDT),
                   _sds((T, N_PAIRS * LANES), F32)],
        name=name, compiler_params=_cp(("parallel", "arbitrary")))(qkv, dcat, qkv, qkv, fcol, frow, lse, delta)


def _mixer_fwd(x, wts, tag, dep=None):
    T = x.shape[0]
    h = _rms_fwd(x, wts["mix_norm"], f"{tag}_norm", dep)
    qkv = _mm([(h, wts["win_qkv"], False)], out_dtype=CDT, tm=1024, tn=768, name=f"{tag}_in_qkv")
    rest = _mm([(h, wts["win_rest"], False)], tm=1024, name=f"{tag}_in_rest")
    fcol, frow8 = _fgate_fwd(rest, wts["fbias"], f"{tag}_fgate")
    frow = frow8.reshape(N_PAIRS, 2, T)
    ya = _pool_fwd(rest, wts["pool_wbd"], wts["pool_scale"], f"{tag}_pool")
    o, lse = _attn_fwd(qkv, fcol, frow, f"{tag}_attn")
    yc, yconv = _conv_fwd(rest, wts["conv_w"], wts["conv_b"], wts["conv_ln_g"], wts["conv_ln_b"], f"{tag}_conv")
    cat = jnp.concatenate([ya, o.astype(CDT), yc], axis=1)
    y = _mm([(cat, wts["w_out"], False)], res=x, name=f"{tag}_out")
    return y, (x, h, qkv, rest, fcol, frow, o, lse, yconv, cat)


def _mixer_bwd(saved, wts, gout, tag, dep=None):
    x, h, qkv, rest, fcol, frow, o, lse, yconv, cat = saved
    T = x.shape[0]
    dcat = _mm([(gout, wts["w_out"], True)], name=f"{tag}_dcat", dep=dep)
    dwout = _mm_tn(cat, gout, name=f"{tag}_dwout")
    du, dpw, dpsc = _pool_bwd(dcat, rest, wts["pool_wbd"], wts["pool_scale"], f"{tag}_pool_bwd")
    delta = _attn_delta(dcat, o, f"{tag}_attn_delta").reshape(N_PAIRS, 2, T)
    dqt, dk, dv, dfk = _attn_bwd(qkv, dcat, fcol, frow, lse, delta, f"{tag}_attn_bwd")
    dq = dqt.T.astype(CDT)
    dz, dfb = _fgate_bwd(dfk, rest, wts["fbias"], f"{tag}_fgate_bwd")
    da, dg, dcw, dcb, dlg, dlb = _conv_bwd(dcat, yconv, rest, wts["conv_w"], wts["conv_ln_g"], wts["conv_ln_b"],
                                           f"{tag}_conv_bwd")
    dp_qkv = jnp.concatenate([dq, dk, dv], axis=1).astype(CDT)
    dp_rest = jnp.concatenate([du, da, dg, dz], axis=1)
    dwin_qkv = _mm_tn(h, dp_qkv, name=f"{tag}_dwin_qkv")
    dwin_rest = _mm_tn(h, dp_rest, name=f"{tag}_dwin_rest")
    gin, dgamma = _mm_norm_bwd([(dp_qkv, wts["win_qkv"], True), (dp_rest, wts["win_rest"], True)], x, wts["mix_norm"],
                               gout, name=f"{tag}_dh_norm_bwd")
    dwin = _split_win(dwin_qkv, dwin_rest, f"{tag}_dwin_split")
    dpool_w = jnp.stack([dpw[64 * g:64 * g + 64, 64 * g:64 * g + 64] for g in range(4)])
    grads = dict(mix_norm=dgamma[0], w_in=dwin, pool_w=dpool_w, pool_scale=dpsc[0], forget_bias=dfb[0, 0:N_HEADS],
                 conv_w=dcw[0:CONV_K], conv_b=dcb[0], conv_ln_g=dlg[0], conv_ln_b=dlb[0], w_out=dwout)
    return gin, grads


def _rep_layer(rep, l):
    pw = rep["pool_w"][l].astype(CDT)
    wbd = jnp.zeros((256, 256), CDT)
    for g in range(4):
        wbd = lax.dynamic_update_slice(wbd, pw[g], (64 * g, 64 * g))
    return dict(
        ffn1_norm=rep["ffn1_norm"][l][None], ffn2_norm=rep["ffn2_norm"][l][None], mix_norm=rep["mix_norm"][l][None],
        fbias=jnp.pad(rep["forget_bias"][l], (0, LANES - N_HEADS))[None],
        pool_wbd=wbd, pool_scale=rep["pool_scale"][l][None], conv_b=rep["conv_b"][l][None],
        conv_ln_g=rep["conv_ln_g"][l][None], conv_ln_b=rep["conv_ln_b"][l][None])


def _local_step(x, target, rep, weights_for, grads_ready):
    depth = rep["ffn1_norm"].shape[0]
    kept = []
    for l in range(depth):
        r = _rep_layer(rep, l)
        w1, dep = weights_for(l, "ffn1", x)
        x, s1 = _ffn_fwd(x, r["ffn1_norm"], w1["w_gate"], w1["w_up"], w1["w_down"], f"l{l}_ffn1", dep)
        wm, dep = weights_for(l, "mix", x)
        wm = dict(r, win_qkv=wm["win_qkv"], win_rest=wm["win_rest"], w_out=wm["w_out"],
                  conv_w=jnp.pad(wm["conv_w"], ((0, CONV_HALO - CONV_K), (0, 0))))
        x, s2 = _mixer_fwd(x, wm, f"l{l}_mix", dep)
        w2, dep = weights_for(l, "ffn2", x)
        x, s3 = _ffn_fwd(x, r["ffn2_norm"], w2["w_gate"], w2["w_up"], w2["w_down"], f"l{l}_ffn2", dep)
        kept.append((r, w1, wm, w2, s1, s2, s3))
    loss, g, dfinal = _loss_bwd(x, rep["final_norm"][None], target, "loss_head")
    dep = grads_ready(None, "final", dict(final_norm=dfinal[0]))
    for l in reversed(range(depth)):
        r, w1, wm, w2, s1, s2, s3 = kept[l]

        def ffn_grads(which, l=l):
            return lambda gr: grads_ready(l, which, {f"{which}_{k}": v for k, v in gr.items()})

        g, dn = _ffn_bwd(s3, r["ffn2_norm"], w2["w_gate"], w2["w_up"], w2["w_down"], g, f"l{l}_ffn2", dep, ffn_grads("ffn2"))
        grads_ready(l, "norm", dict(ffn2_norm=dn[0]))
        g, gm = _mixer_bwd(s2, wm, g, f"l{l}_mix")
        dep = grads_ready(l, "mix", gm)
        g, dn = _ffn_bwd(s1, r["ffn1_norm"], w1["w_gate"], w1["w_up"], w1["w_down"], g, f"l{l}_ffn1", dep, ffn_grads("ffn1"))
        dep = grads_ready(l, "norm", dict(ffn1_norm=dn[0]))
    return loss, g


def _mesh_pos():
    return lax.axis_index("x"), lax.axis_index("y"), lax.axis_index("c")


def _dev_block(ref, dev, by_rows):
    if by_rows:
        r = ref.shape[1] // N_DEV
        return ref.at[:, pl.ds(dev * r, r), :]
    return ref.at[dev]


def _all_gather(shards, by_rows, name):
    n_arr = len(shards)
    out_shape = [_sds((s.shape[0], N_DEV * s.shape[1], s.shape[2]) if br else (N_DEV,) + s.shape, s.dtype)
                 for s, br in zip(shards, by_rows)]

    def body(*refs):
        xs, outs = refs[:n_arr], refs[n_arr:2 * n_arr]
        send_sems, recv_sems, local_sems = refs[2 * n_arr:]
        x, y, c = _mesh_pos()
        me, sibling = (x, y, c), (x, y, 1 - c)
        chips = [(1 - x, y), (x, 1 - y), (1 - x, 1 - y)]

        def rows(a, px, py, pc):
            return _dev_block(outs[a], 4 * px + 2 * py + pc, by_rows[a])

        def copy(k, a, block, to, src=None):
            return pltpu.make_async_remote_copy(
                src_ref=rows(a, *block) if src is None else src, dst_ref=rows(a, *block),
                send_sem=send_sems.at[k, a], recv_sem=recv_sems.at[k, a],
                device_id=to, device_id_type=pl.DeviceIdType.MESH)

        arrs = range(n_arr)
        mine = [pltpu.make_async_copy(xs[a], rows(a, *me), local_sems.at[a]) for a in arrs]
        for cp in mine:
            cp.start()
        first = [copy(0, a, me, sibling, src=xs[a]) for a in arrs]
        first += [copy(1 + j, a, me, (*chip, c), src=xs[a]) for j, chip in enumerate(chips) for a in arrs]
        for cp in first:
            cp.start()
        passed = []
        for j, chip in enumerate(chips):
            for a in arrs:
                copy(1 + j, a, (*chip, c), me).wait_recv()
                passed.append(copy(4 + j, a, (*chip, c), sibling))
                passed[-1].start()
        for a in arrs:
            copy(0, a, sibling, me).wait_recv()
        for j, chip in enumerate(chips):
            for a in arrs:
                copy(4 + j, a, (*chip, 1 - c), me).wait_recv()
        for cp in first + passed:
            cp.wait_send()
        for cp in mine:
            cp.wait()

    hbm = pl.BlockSpec(memory_space=pl.ANY)
    return pl.pallas_call(
        body, out_shape=out_shape, in_specs=[hbm] * n_arr, out_specs=[hbm] * n_arr,
        scratch_shapes=[pltpu.SemaphoreType.DMA((7, n_arr)), pltpu.SemaphoreType.DMA((7, n_arr)),
                        pltpu.SemaphoreType.DMA((n_arr,))],
        name=name)(*shards)


def _exchange(parts, by_rows, name):
    n_arr = len(parts)
    out_shape = [_sds((N_DEV, p.shape[0], p.shape[1] // N_DEV, p.shape[2]) if br else p.shape, p.dtype)
                 for p, br in zip(parts, by_rows)]

    def body(*refs):
        ps, outs = refs[:n_arr], refs[n_arr:2 * n_arr]
        send_sems, recv_sems, local_sems = refs[2 * n_arr:]
        x, y, c = _mesh_pos()
        my = 4 * x + 2 * y + c
        arrs = range(n_arr)
        mine = [pltpu.make_async_copy(_dev_block(ps[a], my, by_rows[a]), outs[a].at[my], local_sems.at[a]) for a in arrs]
        for cp in mine:
            cp.start()
        copies = []
        for k in range(1, N_DEV):
            px, py, pc = x ^ (k >> 2), y ^ ((k >> 1) & 1), c ^ (k & 1)
            for a in arrs:
                copies.append(pltpu.make_async_remote_copy(
                    src_ref=_dev_block(ps[a], 4 * px + 2 * py + pc, by_rows[a]), dst_ref=outs[a].at[my],
                    send_sem=send_sems.at[k - 1, a], recv_sem=recv_sems.at[k - 1, a],
                    device_id=(px, py, pc), device_id_type=pl.DeviceIdType.MESH))
        for cp in copies:
            cp.start()
        for cp in copies:
            cp.wait()
        for cp in mine:
            cp.wait()

    hbm = pl.BlockSpec(memory_space=pl.ANY)
    return pl.pallas_call(
        body, out_shape=out_shape, in_specs=[hbm] * n_arr, out_specs=[hbm] * n_arr,
        scratch_shapes=[pltpu.SemaphoreType.DMA((7, n_arr)), pltpu.SemaphoreType.DMA((7, n_arr)),
                        pltpu.SemaphoreType.DMA((n_arr,))],
        name=name)(*parts)


def _peer_copies(srcs, lands, send_sems, recv_sems, gather, by_rows):
    n_arr = len(srcs)
    x, y, c = _mesh_pos()
    my = 4 * x + 2 * y + c
    out = []
    for k in range(1, N_DEV):
        px, py, pc = x ^ (k >> 2), y ^ ((k >> 1) & 1), c ^ (k & 1)
        peer = 4 * px + 2 * py + pc
        for a in range(n_arr):
            src = srcs[a] if gather else _dev_block(srcs[a], peer, by_rows[a])
            dst = _dev_block(lands[a], my, by_rows[a]) if gather else lands[a].at[my]
            out.append(pltpu.make_async_remote_copy(
                src_ref=src, dst_ref=dst, send_sem=send_sems.at[(k - 1) * n_arr + a],
                recv_sem=recv_sems.at[(k - 1) * n_arr + a], device_id=(px, py, pc), device_id_type=pl.DeviceIdType.MESH))
    return out


def _land_shape(s, gather, by_rows):
    if gather:
        return (s.shape[0], N_DEV * s.shape[1], s.shape[2]) if by_rows else (N_DEV,) + s.shape
    return (N_DEV, s.shape[0], s.shape[1] // N_DEV, s.shape[2]) if by_rows else s.shape


_HBM = pl.BlockSpec(memory_space=pltpu.HBM)
_SEM = pl.BlockSpec(memory_space=pltpu.SEMAPHORE)


def _xfer_start(srcs, gather, by_rows, name, dep=None):
    n = len(srcs)
    x, y, c = _mesh_pos()
    my = 4 * x + 2 * y + c
    lands = []
    for src, br in zip(srcs, by_rows):
        land = lax.empty(_land_shape(src, gather, br), src.dtype)
        zeros = (0,) * (land.ndim - 1)
        if gather and br:
            own, at = src, (0, my * src.shape[1], 0)
        elif gather:
            own, at = src[None], (my,) + zeros
        elif br:
            r = src.shape[1] // N_DEV
            own = lax.dynamic_slice(src, (0, my * r, 0), (src.shape[0], r, src.shape[2]))[None]
            at = (my,) + zeros
        else:
            own, at = lax.dynamic_index_in_dim(src, my, 0, keepdims=True), (my,) + zeros
        lands.append(lax.dynamic_update_slice(land, own, at))
    ins = [pltpu.with_memory_space_constraint(a, pltpu.HBM) for a in list(srcs) + lands]
    dspec, darg = _dep(dep)

    def body(*refs):
        s = 2 * n + len(darg)
        for cp in _peer_copies(refs[:n], refs[n:2 * n], refs[s], refs[s + 1], gather, by_rows):
            cp.start()
        refs[-1][...] = jnp.zeros_like(refs[-1])

    sems = pltpu.SemaphoreType.DMA(((N_DEV - 1) * n,))
    outs = pl.pallas_call(
        body, name=name,
        out_shape=(sems, sems, *[pltpu.HBM(a.shape, a.dtype) for a in ins], _sds((8, LANES), F32)),
        in_specs=[_HBM] * (2 * n) + dspec,
        out_specs=(_SEM, _SEM, *[_HBM] * (2 * n), pl.BlockSpec(memory_space=pltpu.VMEM)),
        input_output_aliases={i: 2 + i for i in range(2 * n)},
        compiler_params=pltpu.CompilerParams(has_side_effects=pltpu.SideEffectType.DATAFLOW_SIDE_EFFECTING))(*ins, *darg)
    return outs[0], outs[1], list(outs[2:-1]), outs[-1]


def _xfer_wait(started, after, gather, by_rows, name):
    send_sems, recv_sems, bufs, _ = started
    n = len(bufs) // 2

    def body(*refs):
        for cp in _peer_copies(refs[:n], refs[n:2 * n], refs[2 * n], refs[2 * n + 1], gather, by_rows):
            cp.wait_send()
            cp.wait_recv()

    outs = pl.pallas_call(
        body, name=name, out_shape=tuple(pltpu.HBM(a.shape, a.dtype) for a in bufs),
        in_specs=[_HBM] * (2 * n) + [_SEM, _SEM, pl.BlockSpec(memory_space=pl.ANY)], out_specs=tuple([_HBM] * (2 * n)),
        input_output_aliases={i: i for i in range(2 * n)},
        compiler_params=pltpu.CompilerParams(has_side_effects=pltpu.SideEffectType.DATAFLOW_SIDE_EFFECTING))(
            *bufs, send_sems, recv_sems, after)
    return list(outs[n:])


def _adam_update(g, w, m, v):
    c1 = 1.0 - ADAM_B1 ** ADAM_STEP
    c2 = 1.0 - ADAM_B2 ** ADAM_STEP
    nm = ADAM_B1 * m + (1.0 - ADAM_B1) * g
    nv = ADAM_B2 * v + (1.0 - ADAM_B2) * (g * g)
    return -ADAM_LR * ((nm / c1) / (jnp.sqrt(nv / c2) + ADAM_EPS) + ADAM_WD * w), nm, nv


def _adamw_body(p_ref, w_ref, m_ref, v_ref, g_ref, d_ref, nm_ref, nv_ref):
    g = p_ref[0]
    for i in range(1, N_DEV):
        g = g + p_ref[i]
    g_ref[...] = g
    d_ref[...], nm_ref[...], nv_ref[...] = _adam_update(g, w_ref[...], m_ref[...], v_ref[...])


def _adamw(parts, w, m, v, name, tr=1536):
    R = w.shape[0]
    tr = max(t for t in range(8, tr + 1, 8) if R % t == 0)

    def body(*refs):
        _adamw_body(*refs)

    row = pl.BlockSpec((tr, LANES), lambda i: (i, 0))
    return pl.pallas_call(
        body, grid=(R // tr,),
        in_specs=[pl.BlockSpec((N_DEV, tr, LANES), lambda i: (0, i, 0)), row, row, row],
        out_specs=[row, row, row, row], out_shape=[_sds((R, LANES), F32)] * 4,
        name=name, compiler_params=_cp(("parallel",)))(parts, w, m, v)


def _adamw_split(recvs, w, m, v, name, tr):
    depth, r, c = w.shape
    assert depth == len(recvs)
    tr = _tile(r, tr)

    def body(*refs):
        layer = pl.program_id(0)
        for ll in range(depth):
            @pl.when(layer == ll)
            def _(ll=ll):
                _adamw_body(refs[ll], *refs[depth:])

    wspec = pl.BlockSpec((None, tr, c), lambda l, i: (l, i, 0))
    rspecs = [pl.BlockSpec((N_DEV, None, tr, c), lambda l, i, ll=ll, t=t: (0, t, jnp.where(l == ll, i, 0), 0))
              for ll, (_, t) in enumerate(recvs)]
    return pl.pallas_call(
        body, grid=(depth, r // tr), in_specs=rspecs + [wspec, wspec, wspec],
        out_specs=[wspec] * 4, out_shape=[_sds(w.shape, F32)] * 4,
        name=name, compiler_params=_cp(("arbitrary", "arbitrary")))(*[a for a, _ in recvs], w, m, v)


def _merge_win(g, name, tr=256):
    _, nt, K, n = g.shape
    tr = _tile(K, tr)

    def body(g_ref, q_ref, r_ref):
        full = jnp.concatenate([g_ref[j] for j in range(N_DEV)], axis=1)
        q_ref[...] = full[:, 256:1792]
        zpad = jnp.zeros((tr, REST_W - 776), full.dtype)
        r_ref[...] = jnp.concatenate([full[:, 0:256], full[:, 1800:2312], full[:, 1792:1800], zpad], axis=1)

    return pl.pallas_call(
        body, grid=(nt, K // tr),
        in_specs=[pl.BlockSpec((N_DEV, None, tr, n), lambda t, i: (0, t, i, 0))],
        out_specs=[pl.BlockSpec((None, tr, 1536), lambda t, i: (t, i, 0)), pl.BlockSpec((None, tr, REST_W), lambda t, i: (t, i, 0))],
        out_shape=[_sds((nt, K, 1536), g.dtype), _sds((nt, K, REST_W), g.dtype)],
        name=name, compiler_params=_cp(("parallel", "parallel")))(g)


def _split_win(dq, dr, name, tr=256):
    K = dq.shape[0]
    tr = _tile(K, tr)
    n = (dq.shape[1] + 776) // N_DEV

    def body(q_ref, r_ref, o_ref):
        r = r_ref[...]
        full = jnp.concatenate([r[:, 0:256], q_ref[...], r[:, 768:776], r[:, 256:768]], axis=1)
        for j in range(N_DEV):
            o_ref[j] = full[:, n * j:n * (j + 1)]

    return pl.pallas_call(
        body, grid=(K // tr,),
        in_specs=[pl.BlockSpec((tr, dq.shape[1]), lambda i: (i, 0)), pl.BlockSpec((tr, REST_W), lambda i: (i, 0))],
        out_specs=pl.BlockSpec((N_DEV, tr, n), lambda i: (0, i, 0)),
        out_shape=_sds((N_DEV, K, n), F32), name=name, compiler_params=_cp(("parallel",)))(dq, dr)


WEIGHTS = ["ffn1_norm", "ffn1_w_gate", "ffn1_w_up", "ffn1_w_down", "mix_norm", "w_in", "pool_w", "pool_scale",
           "forget_bias", "conv_w", "conv_b", "conv_ln_g", "conv_ln_b", "w_out", "ffn2_norm", "ffn2_w_gate",
           "ffn2_w_up", "ffn2_w_down", "final_norm"]
FFN_PARTS = ("w_gate", "w_up", "w_down")
FFN_T = ["ffn1_w_gate", "ffn1_w_up", "ffn2_w_gate", "ffn2_w_up"]
BIG = FFN_T + ["ffn1_w_down", "ffn2_w_down", "w_in", "w_out"]
SMALL = [n for n in WEIGHTS if n not in BIG]


def _padded(n):
    return -(-n // PACK_ALIGN) * PACK_ALIGN


def _flat_pad(a):
    f = a.reshape(-1)
    return jnp.pad(f, (0, _padded(f.shape[0]) - f.shape[0]))


def _split8(a, axis):
    shp = a.shape
    a = a.reshape(shp[:axis] + (N_DEV, shp[axis] // N_DEV) + shp[axis + 1:])
    return jnp.moveaxis(a, axis, 0)


def _merge8(a, axis):
    a = jnp.moveaxis(a, 0, axis)
    shp = a.shape
    return a.reshape(shp[:axis] + (shp[axis] * shp[axis + 1],) + shp[axis + 2:])


def _pack_small(arrs):
    return jnp.concatenate([_flat_pad(arrs[n]) for n in SMALL]).reshape(-1, LANES)


def _pack_small_parts(grads):
    cols = []
    for n in SMALL:
        g = grads[n]
        if n == "conv_w":
            s = _split8(g, 2).reshape(N_DEV, -1)
        else:
            s = jnp.broadcast_to(g.reshape(1, -1), (N_DEV, g.size))
        cols.append(jnp.pad(s, ((0, 0), (0, _padded(s.shape[1]) - s.shape[1]))))
    return jnp.concatenate(cols, axis=1).reshape(N_DEV, -1, LANES)


def _unpack_small(buf, like):
    flat = buf.reshape(-1)
    out, off = {}, 0
    for n in SMALL:
        size = like[n].size
        out[n] = flat[off:off + size].reshape(like[n].shape)
        off += _padded(size)
    return out


class _Comm:
    def __init__(self, w):
        self.w = w
        self.bf = {n: (jnp.swapaxes(w[n], 1, 2) if n in FFN_T else w[n]).astype(CDT) for n in BIG}
        self.ready = {}
        self.grads = {}

    def _ffn_shards(self, l, which):
        return jnp.stack([self.bf[f"{which}_{k}"][l] for k in FFN_PARTS])

    def _put_ffn(self, l, which, rows, t):
        self.ready[(l, which)] = dict(w_gate=rows[t], w_up=rows[t + 1], w_down=rows[t + 2])

    def weights_for(self, l, stage, x):
        bf = self.bf
        dep = None
        if (l, stage) == (0, "ffn1"):
            gd, = _all_gather([self._ffn_shards(0, "ffn1")], [True], "gather_l0_ffn1")
            self._put_ffn(0, "ffn1", gd, 0)
            self.started = _xfer_start([bf["w_in"][0:1], bf["w_out"][0:1], self.w["conv_w"]], True,
                                       [False, True, False], "gather_mix0_start", dep=gd)
            dep = self.started[3]
        elif (l, stage) == (0, "mix"):
            gi, go, gc = _xfer_wait(self.started, x, True, [False, True, False], "gather_mix0_wait")
            q, r = _merge_win(gi, "merge_l0_w_in")
            self.conv_w = _merge8(gc, 2)
            self.ready[(0, "mix")] = dict(win_qkv=q[0], win_rest=r[0], w_out=go[0], conv_w=self.conv_w[0])
            rows = jnp.concatenate([self._ffn_shards(0, "ffn2"), self._ffn_shards(1, "ffn1"), self._ffn_shards(1, "ffn2")])
            self.started = _xfer_start([rows, bf["w_in"][1:2], bf["w_out"][1:2]], True, [True, False, True],
                                       "gather_rest_start")
            dep = self.started[3]
        elif (l, stage) == (0, "ffn2"):
            gd, gi, go = _xfer_wait(self.started, x, True, [True, False, True], "gather_rest_wait")
            self._put_ffn(0, "ffn2", gd, 0)
            self._put_ffn(1, "ffn1", gd, 3)
            self._put_ffn(1, "ffn2", gd, 6)
            q, r = _merge_win(gi, "merge_l1_w_in")
            self.ready[(1, "mix")] = dict(win_qkv=q[0], win_rest=r[0], w_out=go[0], conv_w=self.conv_w[1])
        return self.ready[(l, stage)], dep

    def grads_ready(self, l, stage, grads):
        for n, v in grads.items():
            self.grads[(l, n)] = v
        gr = self.grads

        def ffn_rows(layer, which, parts=FFN_PARTS):
            return [gr[(layer, f"{which}_{k}")] for k in parts]

        if l == 1 and "ffn1_w_gate" in grads:
            self.sent1 = _xfer_start(
                [jnp.stack(ffn_rows(1, "ffn1") + ffn_rows(1, "ffn2")), gr[(1, "w_in")][:, None], gr[(1, "w_out")][None]],
                False, [True, False, True], "grads_l1_start")
            return self.sent1[3]
        if l == 0 and "ffn2_w_gate" in grads:
            self.sent_ffn2 = _xfer_start([jnp.stack(ffn_rows(0, "ffn2"))], False, [True], "grads_l0_ffn2_start")
            return self.sent_ffn2[3]
        if (l, stage) == (0, "mix"):
            self.sent_mix = _xfer_start([gr[(0, "w_in")][:, None], gr[(0, "w_out")][None]], False, [False, True],
                                        "grads_l0_mix_start")
            return self.sent_mix[3]
        if l == 0 and "ffn1_w_down" in grads:
            self.sent_down = _xfer_start([gr[(0, "ffn1_w_down")][None]], False, [True], "grads_l0_ffn1_down_start")
            return self.sent_down[3]
        if l == 0 and "ffn1_w_gate" in grads:
            self.sent_gu = _xfer_start([jnp.stack(ffn_rows(0, "ffn1", FFN_PARTS[:2]))], False, [True],
                                       "grads_l0_ffn1_gate_up_start")
            return self.sent_gu[3]
        return None

    def finish(self, m, v, after):
        w, gr = self.w, self.grads
        depth = range(w["w_in"].shape[0])
        small = {n: (gr[(None, n)] if n == "final_norm" else jnp.stack([gr[(l, n)] for l in depth])) for n in SMALL}
        r1, i1, o1 = _xfer_wait(self.sent1, after, False, [True, False, True], "grads_l1_wait")
        r2, = _xfer_wait(self.sent_ffn2, after, False, [True], "grads_l0_ffn2_wait")
        i0, o0 = _xfer_wait(self.sent_mix, after, False, [False, True], "grads_l0_mix_wait")

        def adam(n, recvs, tr):
            if n in FFN_T:
                out = _adamw_split(recvs, *[jnp.swapaxes(t[n], 1, 2) for t in (w, m, v)], f"adamw_{n}", tr)
                return [jnp.swapaxes(o, 1, 2) for o in out]
            return _adamw_split(recvs, w[n], m[n], v[n], f"adamw_{n}", tr)

        res = {}
        for t, k in enumerate(FFN_PARTS):
            res[f"ffn2_{k}"] = adam(f"ffn2_{k}", [(r2, t), (r1, 3 + t)], 176)
        res["w_in"] = adam("w_in", [(i0, 0), (i1, 0)], 256)
        res["w_out"] = adam("w_out", [(o0, 0), (o1, 0)], 128)
        rs, = _exchange([_pack_small_parts(small)], [False], "exchange_small")
        r0, = _xfer_wait(self.sent_down, res["w_out"][0], False, [True], "grads_l0_ffn1_down_wait")
        res["ffn1_w_down"] = adam("ffn1_w_down", [(r0, 0), (r1, 2)], 176)
        g0, = _xfer_wait(self.sent_gu, res["ffn1_w_down"][0], False, [True], "grads_l0_ffn1_gate_up_wait")
        res["ffn1_w_gate"] = adam("ffn1_w_gate", [(g0, 0), (r1, 0)], 176)
        res["ffn1_w_up"] = adam("ffn1_w_up", [(g0, 1), (r1, 1)], 176)
        packed = _adamw(rs, _pack_small(w), _pack_small(m), _pack_small(v), "adamw_small")
        unpacked = [_unpack_small(b, w) for b in packed]
        for n in SMALL:
            res[n] = [u[n] for u in unpacked]
        return res


def kernel(x, ffn1_norm, ffn1_w_gate, ffn1_w_up, ffn1_w_down, mix_norm, w_in, pool_w, pool_scale, forget_bias, conv_w, conv_b, conv_ln_g, conv_ln_b, w_out, ffn2_norm, ffn2_w_gate, ffn2_w_up, ffn2_w_down, final_norm, loss_target, m_ffn1_norm, m_ffn1_w_gate, m_ffn1_w_up, m_ffn1_w_down, m_mix_norm, m_w_in, m_pool_w, m_pool_scale, m_forget_bias, m_conv_w, m_conv_b, m_conv_ln_g, m_conv_ln_b, m_w_out, m_ffn2_norm, m_ffn2_w_gate, m_ffn2_w_up, m_ffn2_w_down, m_final_norm, v_ffn1_norm, v_ffn1_w_gate, v_ffn1_w_up, v_ffn1_w_down, v_mix_norm, v_w_in, v_pool_w, v_pool_scale, v_forget_bias, v_conv_w, v_conv_b, v_conv_ln_g, v_conv_ln_b, v_w_out, v_ffn2_norm, v_ffn2_w_gate, v_ffn2_w_up, v_ffn2_w_down, v_final_norm):
    w = dict(zip(WEIGHTS, (ffn1_norm, ffn1_w_gate, ffn1_w_up, ffn1_w_down, mix_norm, w_in, pool_w, pool_scale, forget_bias,
                           conv_w, conv_b, conv_ln_g, conv_ln_b, w_out, ffn2_norm, ffn2_w_gate, ffn2_w_up, ffn2_w_down,
                           final_norm)))
    m = dict(zip(WEIGHTS, (m_ffn1_norm, m_ffn1_w_gate, m_ffn1_w_up, m_ffn1_w_down, m_mix_norm, m_w_in, m_pool_w, m_pool_scale,
                           m_forget_bias, m_conv_w, m_conv_b, m_conv_ln_g, m_conv_ln_b, m_w_out, m_ffn2_norm, m_ffn2_w_gate,
                           m_ffn2_w_up, m_ffn2_w_down, m_final_norm)))
    v = dict(zip(WEIGHTS, (v_ffn1_norm, v_ffn1_w_gate, v_ffn1_w_up, v_ffn1_w_down, v_mix_norm, v_w_in, v_pool_w, v_pool_scale,
                           v_forget_bias, v_conv_w, v_conv_b, v_conv_ln_g, v_conv_ln_b, v_w_out, v_ffn2_norm, v_ffn2_w_gate,
                           v_ffn2_w_up, v_ffn2_w_down, v_final_norm)))
    comm = _Comm(w)
    loss_row, gx = _local_step(x[0], loss_target[0], w, comm.weights_for, comm.grads_ready)
    loss = lax.psum(loss_row[0, 0], ("x", "y", "c"))
    res = comm.finish(m, v, gx)
    return (loss, gx[None], *[res[n][i] for i in range(4) for n in WEIGHTS])
```

```python
import math

import numpy as np
import jax
import jax.numpy as jnp
from jax import lax
from jax.experimental import pallas as pl
from jax.experimental.pallas import tpu as pltpu

F32 = jnp.float32
CDT = jnp.bfloat16
NORM_EPS = 1e-6
N_DEV = 8
LANES = 128
PACK_ALIGN = 8 * LANES
VMEM_LIMIT = 48 * 1024 * 1024

POOL_WINDOWS = (2, 4, 8, 16)
POOL_HALO = 16
CONV_K = 31
CONV_HALO = 32
HEAD_DIM = 64
N_HEADS = 8
N_PAIRS = N_HEADS // 2
ATT_SCALE = 1.0 / math.sqrt(HEAD_DIM)
NEG = -1e30

ADAM_LR, ADAM_B1, ADAM_B2, ADAM_EPS, ADAM_WD, ADAM_STEP = 0.001, 0.9, 0.999, 1e-08, 0.01, 10

REST_W = 896
REST_Z_BLK = 6


def _cp(sem):
    return pltpu.CompilerParams(dimension_semantics=sem, vmem_limit_bytes=VMEM_LIMIT)


def _tile(n, pref):
    t = min(n, pref)
    assert n % t == 0, (n, pref)
    return t


def _sigmoid(x):
    return 1.0 / (1.0 + jnp.exp(-x))


def _sds(shape, dtype):
    return jax.ShapeDtypeStruct(shape, dtype)


_ANY = pl.BlockSpec(memory_space=pl.ANY)


def _dep(dep):
    return ([], []) if dep is None else ([_ANY], [dep])


def _wshape(w):
    return w[0].shape[1:] if isinstance(w, tuple) else w.shape


def _wspec(w, block, index_map):
    if not isinstance(w, tuple):
        return w, pl.BlockSpec(block, index_map)
    arr, t = w
    return arr, pl.BlockSpec((None,) + block, lambda *g: (t,) + index_map(*g))


def _rms_fwd(x, g, name, dep=None):
    T, D = x.shape
    tm = _tile(T, 1024)

    def body(x_ref, g_ref, *rest):
        o_ref = rest[-1]
        xv = x_ref[...]
        r = lax.rsqrt(jnp.mean(xv * xv, axis=-1, keepdims=True) + NORM_EPS)
        o_ref[...] = (xv * r * g_ref[...]).astype(o_ref.dtype)

    dspec, darg = _dep(dep)
    return pl.pallas_call(
        body, grid=(T // tm,),
        in_specs=[pl.BlockSpec((tm, D), lambda i: (i, 0)), pl.BlockSpec((1, D), lambda i: (0, 0))] + dspec,
        out_specs=pl.BlockSpec((tm, D), lambda i: (i, 0)),
        out_shape=_sds((T, D), CDT), name=name, compiler_params=_cp(("parallel",)))(x, g, *darg)


def _rms_bwd(x, g, dh, gres, name):
    T, D = x.shape
    tm = _tile(T, 512)

    def body(x_ref, g_ref, dh_ref, gres_ref, gin_ref, dg_ref):
        i = pl.program_id(0)
        xv = x_ref[...]
        d = dh_ref[...]
        r = lax.rsqrt(jnp.mean(xv * xv, axis=-1, keepdims=True) + NORM_EPS)
        xh = xv * r
        dxh = d * g_ref[...]
        c = jnp.mean(dxh * xh, axis=-1, keepdims=True)
        gin_ref[...] = gres_ref[...] + r * (dxh - xh * c)
        part = jnp.sum(d * xh, axis=0, keepdims=True)

        @pl.when(i == 0)
        def _():
            dg_ref[...] = part

        @pl.when(i > 0)
        def _():
            dg_ref[...] += part

    row = pl.BlockSpec((tm, D), lambda i: (i, 0))
    vec = pl.BlockSpec((1, D), lambda i: (0, 0))
    return pl.pallas_call(
        body, grid=(T // tm,), in_specs=[row, vec, row, row], out_specs=[row, vec],
        out_shape=[_sds((T, D), F32), _sds((1, D), F32)], name=name, compiler_params=_cp(("arbitrary",)))(x, g, dh, gres)


def _loss_bwd(x, g, target, name):
    T, D = x.shape
    tm = _tile(T, 512)

    def body(x_ref, g_ref, t_ref, loss_ref, dx_ref, dg_ref):
        i = pl.program_id(0)
        xv = x_ref[...]
        gv = g_ref[...]
        r = lax.rsqrt(jnp.mean(xv * xv, axis=-1, keepdims=True) + NORM_EPS)
        xh = xv * r
        err = xh * gv - t_ref[...]
        lpart = 0.5 * jnp.sum(jnp.mean(err * err, axis=-1, keepdims=True), axis=0, keepdims=True)
        dy = err * (1.0 / D)
        dxh = dy * gv
        c = jnp.mean(dxh * xh, axis=-1, keepdims=True)
        dx_ref[...] = r * (dxh - xh * c)
        part = jnp.sum(dy * xh, axis=0, keepdims=True)
        lrow = jnp.broadcast_to(lpart, (1, LANES))

        @pl.when(i == 0)
        def _():
            dg_ref[...] = part
            loss_ref[...] = lrow

        @pl.when(i > 0)
        def _():
            dg_ref[...] += part
            loss_ref[...] += lrow

    row = pl.BlockSpec((tm, D), lambda i: (i, 0))
    vec = pl.BlockSpec((1, D), lambda i: (0, 0))
    return pl.pallas_call(
        body, grid=(T // tm,), in_specs=[row, vec, row],
        out_specs=[pl.BlockSpec((1, LANES), lambda i: (0, 0)), row, vec],
        out_shape=[_sds((1, LANES), F32), _sds((T, D), F32), _sds((1, D), F32)],
        name=name, compiler_params=_cp(("arbitrary",)))(x, g, target)


def _mm(pairs, *, name, res=None, alpha=1.0, out_dtype=F32, tm=512, tn=None, dep=None):
    T = pairs[0][0].shape[0]
    N = _wshape(pairs[0][1])[0 if pairs[0][2] else 1]
    tm = _tile(T, tm)
    tn = N if tn is None else _tile(N, tn)
    flags = [p[2] for p in pairs]
    n_in = 2 * len(pairs)

    def body(*refs):
        o_ref = refs[-1]
        acc = None
        for p, bt in enumerate(flags):
            a = refs[2 * p][...].astype(CDT)
            b = refs[2 * p + 1][...]
            dims = (((1,), (1,)), ((), ())) if bt else (((1,), (0,)), ((), ()))
            d = lax.dot_general(a, b, dims, preferred_element_type=F32)
            acc = d if acc is None else acc + d
        if alpha != 1.0:
            acc = acc * alpha
        if res is not None:
            acc = refs[n_in][...] + acc
        o_ref[...] = acc.astype(o_ref.dtype)

    in_specs, args = [], []
    for a, b, bt in pairs:
        K = a.shape[1]
        in_specs.append(pl.BlockSpec((tm, K), lambda i, j: (i, 0)))
        b, bspec = _wspec(b, (tn, K), lambda i, j: (j, 0)) if bt else _wspec(b, (K, tn), lambda i, j: (0, j))
        in_specs.append(bspec)
        args += [a, b]
    if res is not None:
        in_specs.append(pl.BlockSpec((tm, tn), lambda i, j: (i, j)))
        args.append(res)
    dspec, darg = _dep(dep)
    in_specs += dspec
    args += darg
    return pl.pallas_call(
        body, grid=(T // tm, N // tn), in_specs=in_specs,
        out_specs=pl.BlockSpec((tm, tn), lambda i, j: (i, j)),
        out_shape=_sds((T, N), out_dtype), name=name, compiler_params=_cp(("parallel", "arbitrary")))(*args)


def _mm_norm_bwd(pairs, x, g, gres, *, name, tm=256, dep=None):
    T, D = x.shape
    tm = _tile(T, tm)
    n_in = 2 * len(pairs)
    flags = [p[2] for p in pairs]

    def body(*refs):
        x_ref, g_ref, gres_ref = refs[n_in:n_in + 3]
        gin_ref, dg_ref = refs[-2:]
        i = pl.program_id(0)
        d = None
        for p, bt in enumerate(flags):
            dims = (((1,), (1,)), ((), ())) if bt else (((1,), (0,)), ((), ()))
            part = lax.dot_general(refs[2 * p][...].astype(CDT), refs[2 * p + 1][...], dims, preferred_element_type=F32)
            d = part if d is None else d + part
        xv = x_ref[...]
        r = lax.rsqrt(jnp.mean(xv * xv, axis=-1, keepdims=True) + NORM_EPS)
        xh = xv * r
        dxh = d * g_ref[...]
        c = jnp.mean(dxh * xh, axis=-1, keepdims=True)
        gin_ref[...] = gres_ref[...] + r * (dxh - xh * c)
        part = jnp.sum(d * xh, axis=0, keepdims=True)

        @pl.when(i == 0)
        def _():
            dg_ref[...] = part

        @pl.when(i > 0)
        def _():
            dg_ref[...] += part

    in_specs, args = [], []
    for a, b, bt in pairs:
        K = a.shape[1]
        b, bspec = _wspec(b, tuple(_wshape(b)), lambda i: (0, 0))
        in_specs += [pl.BlockSpec((tm, K), lambda i: (i, 0)), bspec]
        args += [a, b]
    row = pl.BlockSpec((tm, D), lambda i: (i, 0))
    vec = pl.BlockSpec((1, D), lambda i: (0, 0))
    dspec, darg = _dep(dep)
    return pl.pallas_call(
        body, grid=(T // tm,), in_specs=in_specs + [row, vec, row] + dspec, out_specs=[row, vec],
        out_shape=[_sds((T, D), F32), _sds((1, D), F32)], name=name,
        compiler_params=_cp(("arbitrary",)))(*args, x, g, gres, *darg)


def _mm_tn(a, b, *, name, alpha=1.0, tk=2048, dep=None):
    T, M = a.shape
    N = b.shape[1]
    tm = M if M <= 1024 else M // 2
    tn = N if N <= 1536 else N // 2
    assert M % tm == 0 and N % tn == 0 and tm % LANES == 0 and tn % LANES == 0
    tk = _tile(T, tk)
    nk = T // tk

    def body(a_ref, b_ref, *rest):
        o_ref = rest[-1]
        k = pl.program_id(2)
        d = lax.dot_general(a_ref[...].astype(CDT), b_ref[...].astype(CDT), (((0,), (0,)), ((), ())),
                            preferred_element_type=F32)

        @pl.when(k == 0)
        def _():
            o_ref[...] = d

        @pl.when(k > 0)
        def _():
            o_ref[...] += d

        if alpha != 1.0:
            @pl.when(k == nk - 1)
            def _():
                o_ref[...] *= alpha

    dspec, darg = _dep(dep)
    return pl.pallas_call(
        body, grid=(M // tm, N // tn, nk),
        in_specs=[pl.BlockSpec((tk, tm), lambda i, j, k: (k, i)), pl.BlockSpec((tk, tn), lambda i, j, k: (k, j))] + dspec,
        out_specs=pl.BlockSpec((tm, tn), lambda i, j, k: (i, j)),
        out_shape=_sds((M, N), F32), name=name, compiler_params=_cp(("parallel", "parallel", "arbitrary")))(a, b, *darg)


def _ffn_up(h, wgt, wut, name):
    T, D = h.shape
    Fh = _wshape(wgt)[0]
    tm = _tile(T, 2048)
    tn = _tile(Fh, 256)
    nt = (((1,), (1,)), ((), ()))

    def body(h_ref, wg_ref, wu_ref, a_ref, b_ref, s_ref):
        hv = h_ref[...]
        a = lax.dot_general(hv, wg_ref[...], nt, preferred_element_type=F32)
        b = lax.dot_general(hv, wu_ref[...], nt, preferred_element_type=F32)
        a_ref[...] = a.astype(a_ref.dtype)
        b_ref[...] = b.astype(b_ref.dtype)
        s_ref[...] = (a * _sigmoid(a) * b).astype(s_ref.dtype)

    wgt, gspec = _wspec(wgt, (tn, D), lambda i, j: (j, 0))
    wut, uspec = _wspec(wut, (tn, D), lambda i, j: (j, 0))
    ospec = pl.BlockSpec((tm, tn), lambda i, j: (i, j))
    return pl.pallas_call(
        body, grid=(T // tm, Fh // tn),
        in_specs=[pl.BlockSpec((tm, D), lambda i, j: (i, 0)), gspec, uspec],
        out_specs=[ospec, ospec, ospec],
        out_shape=[_sds((T, Fh), CDT), _sds((T, Fh), CDT), _sds((T, Fh), CDT)],
        name=name, compiler_params=_cp(("parallel", "arbitrary")))(h, wgt, wut)


def _ffn_bwd_ds(gout, wd, a, b, name, dep=None):
    T, D = gout.shape
    Fh = _wshape(wd)[0]
    tm = _tile(T, 2048)
    tn = _tile(Fh, 256)

    def body(g_ref, wd_ref, a_ref, b_ref, *rest):
        da_ref, db_ref = rest[-2:]
        dy = (0.5 * g_ref[...]).astype(CDT)
        ds = lax.dot_general(dy, wd_ref[...], (((1,), (1,)), ((), ())), preferred_element_type=F32)
        av = a_ref[...].astype(F32)
        sg = _sigmoid(av)
        da_ref[...] = (ds * b_ref[...].astype(F32) * (sg * (1.0 + av * (1.0 - sg)))).astype(da_ref.dtype)
        db_ref[...] = (ds * (av * sg)).astype(db_ref.dtype)

    ospec = pl.BlockSpec((tm, tn), lambda i, j: (i, j))
    dspec, darg = _dep(dep)
    wd, wspec = _wspec(wd, (tn, D), lambda i, j: (j, 0))
    return pl.pallas_call(
        body, grid=(T // tm, Fh // tn),
        in_specs=[pl.BlockSpec((tm, D), lambda i, j: (i, 0)), wspec, ospec, ospec] + dspec,
        out_specs=[ospec, ospec],
        out_shape=[_sds((T, Fh), CDT), _sds((T, Fh), CDT)],
        name=name, compiler_params=_cp(("parallel", "arbitrary")))(gout, wd, a, b, *darg)


def _ffn_fwd(x, gamma, wgt, wut, wd, tag, dep=None):
    h = _rms_fwd(x, gamma, f"{tag}_norm", dep)
    a, b, s = _ffn_up(h, wgt, wut, f"{tag}_up")
    y = _mm([(s, wd, False)], res=x, alpha=0.5, name=f"{tag}_down")
    return y, (x, h, a, b, s)


def _ffn_bwd(saved, gamma, wgt, wut, wd, gout, tag, dep, on_grads):
    x, h, a, b, s = saved
    dwd = _mm_tn(s, gout, alpha=0.5, name=f"{tag}_dwd", dep=dep)
    da, db = _ffn_bwd_ds(gout, wd, a, b, f"{tag}_bwd_ds", on_grads(dict(w_down=dwd)))
    dwgt = _mm_tn(da, h, name=f"{tag}_dwg")
    dwut = _mm_tn(db, h, name=f"{tag}_dwu")
    dep = on_grads(dict(w_gate=dwgt, w_up=dwut))
    return _mm_norm_bwd([(da, wgt, False), (db, wut, False)], x, gamma, gout, name=f"{tag}_dh_norm_bwd", dep=dep)


def _fgate_fwd(rest, bias, name, bt=512):
    T = rest.shape[0]
    bt = _tile(T, bt)

    def body(z_ref, b_ref, fc_ref, ft_ref, carry):
        i = pl.program_id(0)

        @pl.when(i == 0)
        def _():
            carry[...] = jnp.zeros_like(carry)

        zb = z_ref[...] + b_ref[...]
        e = jnp.exp(-jnp.abs(zb))
        u = 1.0 + e
        log1p_e = jnp.where(u == 1.0, e, jnp.log(u) * (e / (u - 1.0)))
        x = jnp.minimum(zb, 0.0) - log1p_e
        row = lax.broadcasted_iota(jnp.int32, x.shape, 0)
        sh = 1
        while sh < bt:
            x = x + jnp.where(row >= sh, pltpu.roll(x, sh, 0), 0.0)
            sh *= 2
        f = x + carry[...]
        carry[...] = f[bt - 1:bt, :]
        fc_ref[...] = f
        ft_ref[...] = jnp.transpose(f)[0:N_HEADS, :]

    return pl.pallas_call(
        body, grid=(T // bt,),
        in_specs=[pl.BlockSpec((bt, LANES), lambda i: (i, REST_Z_BLK)), pl.BlockSpec((1, LANES), lambda i: (0, 0))],
        out_specs=[pl.BlockSpec((bt, LANES), lambda i: (i, 0)), pl.BlockSpec((N_HEADS, bt), lambda i: (0, i))],
        out_shape=[_sds((T, LANES), F32), _sds((N_HEADS, T), F32)],
        scratch_shapes=[pltpu.VMEM((1, LANES), F32)],
        name=name, compiler_params=_cp(("arbitrary",)))(rest, bias)


def _fgate_bwd(dfk, rest, bias, name, bt=512):
    T = rest.shape[0]
    bt = _tile(T, bt)
    nb = T // bt

    def body(df_ref, z_ref, b_ref, dz_ref, db_ref, carry):
        i = pl.program_id(0)

        @pl.when(i == 0)
        def _():
            carry[...] = jnp.zeros_like(carry)

        dfv = df_ref[...]
        lane = lax.broadcasted_iota(jnp.int32, (bt, LANES), 1)
        x = jnp.zeros((bt, LANES), F32)
        for h in range(N_HEADS):
            x = jnp.where(lane == h, dfv[:, HEAD_DIM * h:HEAD_DIM * h + 1], x)
        row = lax.broadcasted_iota(jnp.int32, x.shape, 0)
        sh = 1
        while sh < bt:
            x = x + jnp.where(row + sh < bt, pltpu.roll(x, bt - sh, 0), 0.0)
            sh *= 2
        dlf = x + carry[...]
        carry[...] = dlf[0:1, :]
        zb = z_ref[...] + b_ref[...]
        dz = jnp.where(lane < N_HEADS, dlf * _sigmoid(-zb), 0.0)
        dz_ref[...] = dz.astype(dz_ref.dtype)
        part = jnp.sum(dz, axis=0, keepdims=True)

        @pl.when(i == 0)
        def _():
            db_ref[...] = part

        @pl.when(i > 0)
        def _():
            db_ref[...] += part

    return pl.pallas_call(
        body, grid=(nb,),
        in_specs=[pl.BlockSpec((bt, 4 * LANES), lambda i: (nb - 1 - i, 0)),
                  pl.BlockSpec((bt, LANES), lambda i: (nb - 1 - i, REST_Z_BLK)),
                  pl.BlockSpec((1, LANES), lambda i: (0, 0))],
        out_specs=[pl.BlockSpec((bt, LANES), lambda i: (nb - 1 - i, 0)), pl.BlockSpec((1, LANES), lambda i: (0, 0))],
        out_shape=[_sds((T, LANES), CDT), _sds((1, LANES), F32)],
        scratch_shapes=[pltpu.VMEM((1, LANES), F32)],
        name=name, compiler_params=_cp(("arbitrary",)))(dfk, rest, bias)


def _by_group(vals, lane):
    out = vals[-1]
    for g in range(len(vals) - 2, -1, -1):
        out = jnp.where(lane // 64 == g, vals[g], out)
    return out


def _pool_counts(t0, n, lane):
    t = t0 + lax.broadcasted_iota(jnp.int32, (n, 256), 0)
    return _by_group([jnp.minimum(t + 1, w) for w in POOL_WINDOWS], lane).astype(F32)


def _pooled(u, halo, i, bt):
    lane = lax.broadcasted_iota(jnp.int32, (bt, 256), 1)
    ext = jnp.concatenate([jnp.where(i > 0, halo, 0.0), u], axis=0)
    sums, s, sh = [], ext, 1
    for _ in POOL_WINDOWS:
        s = s + pltpu.roll(s, sh, 0)
        sums.append(s[POOL_HALO:, :])
        sh *= 2
    return _by_group(sums, lane) / _pool_counts(i * bt, bt, lane) - u


def _pool_fwd(rest, wbd, scale, name, bt=512):
    T = rest.shape[0]
    bt = _tile(T, bt)
    hb = bt // POOL_HALO

    def body(u_ref, halo_ref, w_ref, sc_ref, o_ref):
        i = pl.program_id(0)
        pooled = _pooled(u_ref[...], halo_ref[...], i, bt)
        mixed = jnp.dot(pooled.astype(CDT), w_ref[...], preferred_element_type=F32)
        o_ref[...] = (mixed * sc_ref[...]).astype(o_ref.dtype)

    return pl.pallas_call(
        body, grid=(T // bt,),
        in_specs=[pl.BlockSpec((bt, 256), lambda i: (i, 0)),
                  pl.BlockSpec((POOL_HALO, 256), lambda i: (jnp.maximum(i * hb - 1, 0), 0)),
                  pl.BlockSpec((256, 256), lambda i: (0, 0)), pl.BlockSpec((1, 256), lambda i: (0, 0))],
        out_specs=pl.BlockSpec((bt, 256), lambda i: (i, 0)),
        out_shape=_sds((T, 256), CDT), name=name, compiler_params=_cp(("parallel",)))(rest, rest, wbd, scale)


def _pool_bwd(dcat, rest, wbd, scale, name, bt=512):
    T = rest.shape[0]
    bt = _tile(T, bt)
    hb = bt // POOL_HALO
    nb = T // bt
    n = bt + POOL_HALO

    def body(dy_ref, dyn_ref, u_ref, halo_ref, w_ref, sc_ref, du_ref, dw_ref, dsc_ref):
        i = pl.program_id(0)
        lane = lax.broadcasted_iota(jnp.int32, (bt, 256), 1)
        w = w_ref[...]
        sc = sc_ref[...]
        pooled = _pooled(u_ref[...], halo_ref[...], i, bt)
        pooled_c = pooled.astype(CDT)
        mixed = jnp.dot(pooled_c, w, preferred_element_type=F32)
        dy = dy_ref[...]
        dm = (dy * sc).astype(CDT)
        dsc = jnp.sum(dy * mixed, axis=0, keepdims=True)
        dw = lax.dot_general(pooled_c, dm, (((0,), (0,)), ((), ())), preferred_element_type=F32)
        nt = (((1,), (1,)), ((), ()))
        dpl = lax.dot_general(dm, w, nt, preferred_element_type=F32)
        dmn = (jnp.where(i < nb - 1, dyn_ref[...], 0.0) * sc).astype(CDT)
        dpln = lax.dot_general(dmn, w, nt, preferred_element_type=F32)
        lane_h = lax.broadcasted_iota(jnp.int32, (POOL_HALO, 256), 1)
        ext = jnp.concatenate([dpl / _pool_counts(i * bt, bt, lane),
                               dpln / _pool_counts((i + 1) * bt, POOL_HALO, lane_h)], axis=0)
        sums, s, sh = [], ext, 1
        for _ in POOL_WINDOWS:
            s = s + pltpu.roll(s, n - sh, 0)
            sums.append(s[0:bt, :])
            sh *= 2
        du_ref[...] = (_by_group(sums, lane) - dpl).astype(du_ref.dtype)

        @pl.when(i == 0)
        def _():
            dw_ref[...] = dw
            dsc_ref[...] = dsc

        @pl.when(i > 0)
        def _():
            dw_ref[...] += dw
            dsc_ref[...] += dsc

    full = pl.BlockSpec((256, 256), lambda i: (0, 0))
    vec = pl.BlockSpec((1, 256), lambda i: (0, 0))
    return pl.pallas_call(
        body, grid=(nb,),
        in_specs=[pl.BlockSpec((bt, 256), lambda i: (i, 0)),
                  pl.BlockSpec((POOL_HALO, 256), lambda i: (jnp.minimum((i + 1) * hb, nb * hb - 1), 0)),
                  pl.BlockSpec((bt, 256), lambda i: (i, 0)),
                  pl.BlockSpec((POOL_HALO, 256), lambda i: (jnp.maximum(i * hb - 1, 0), 0)),
                  full, vec],
        out_specs=[pl.BlockSpec((bt, 256), lambda i: (i, 0)), full, vec],
        out_shape=[_sds((T, 256), CDT), _sds((256, 256), F32), _sds((1, 256), F32)],
        name=name, compiler_params=_cp(("arbitrary",)))(dcat, dcat, rest, rest, wbd, scale)


def _glu_ext(a_ref, g_ref, ah_ref, gh_ref, i):
    u = a_ref[...] * _sigmoid(g_ref[...])
    uh = jnp.where(i > 0, ah_ref[...] * _sigmoid(gh_ref[...]), 0.0)
    return jnp.concatenate([uh, u], axis=0)


def _conv_fwd(rest, cw, cb, lg, lb, name, bt=512):
    T = rest.shape[0]
    bt = _tile(T, bt)
    hb = bt // CONV_HALO

    def body(a_ref, g_ref, ah_ref, gh_ref, cw_ref, cb_ref, lg_ref, lb_ref, o_ref, y_ref):
        i = pl.program_id(0)
        ext = _glu_ext(a_ref, g_ref, ah_ref, gh_ref, i)
        w = cw_ref[...]
        acc = w[CONV_K - 1:CONV_K, :] * ext
        for k in range(CONV_K - 1):
            acc = acc + w[k:k + 1, :] * pltpu.roll(ext, CONV_K - 1 - k, 0)
        y = acc[CONV_HALO:, :] + cb_ref[...]
        y_ref[...] = y
        yc = y - jnp.mean(y, axis=-1, keepdims=True)
        yn = yc * lax.rsqrt(jnp.mean(yc * yc, axis=-1, keepdims=True) + NORM_EPS)
        z = yn * lg_ref[...] + lb_ref[...]
        o_ref[...] = (z * _sigmoid(z)).astype(o_ref.dtype)

    def cur(c):
        return pl.BlockSpec((bt, 256), lambda i: (i, c))

    def prev(c):
        return pl.BlockSpec((CONV_HALO, 256), lambda i: (jnp.maximum(i * hb - 1, 0), c))

    vec = pl.BlockSpec((1, 256), lambda i: (0, 0))
    return pl.pallas_call(
        body, grid=(T // bt,),
        in_specs=[cur(1), cur(2), prev(1), prev(2), pl.BlockSpec((CONV_HALO, 256), lambda i: (0, 0)), vec, vec, vec],
        out_specs=[pl.BlockSpec((bt, 256), lambda i: (i, 0)), pl.BlockSpec((bt, 256), lambda i: (i, 0))],
        out_shape=[_sds((T, 256), CDT), _sds((T, 256), F32)],
        name=name, compiler_params=_cp(("parallel",)))(rest, rest, rest, rest, cw, cb, lg, lb)


def _conv_bwd(dcat, yconv, rest, cw, lg, lb, name, bt=512):
    T = rest.shape[0]
    bt = _tile(T, bt)
    hb = bt // CONV_HALO
    nb = T // bt
    n = bt + CONV_HALO

    def body(dy_ref, dyn_ref, y_ref, yn_ref, a_ref, g_ref, ah_ref, gh_ref, cw_ref, lg_ref, lb_ref,
             da_ref, dg_ref, dcw_ref, dcb_ref, dlg_ref, dlb_ref):
        i = pl.program_id(0)
        lgv = lg_ref[...]
        lbv = lb_ref[...]

        def ln_swish_bwd(dout, y):
            yc = y - jnp.mean(y, axis=-1, keepdims=True)
            rs = lax.rsqrt(jnp.mean(yc * yc, axis=-1, keepdims=True) + NORM_EPS)
            yn = yc * rs
            z = yn * lgv + lbv
            sg = _sigmoid(z)
            dz = dout * (sg * (1.0 + z * (1.0 - sg)))
            dyn = dz * lgv
            dyc = rs * (dyn - jnp.mean(dyn, axis=-1, keepdims=True) - yn * jnp.mean(dyn * yn, axis=-1, keepdims=True))
            return dyc, dz, yn

        dyc, dz, yn = ln_swish_bwd(dy_ref[...], y_ref[...])
        dyc_next, _, _ = ln_swish_bwd(dyn_ref[...], yn_ref[...])
        dyc_next = jnp.where(i < nb - 1, dyc_next, 0.0)
        ext_u = _glu_ext(a_ref, g_ref, ah_ref, gh_ref, i)
        ext_d = jnp.concatenate([dyc, dyc_next], axis=0)
        w = cw_ref[...]
        du = w[CONV_K - 1:CONV_K, :] * ext_d
        rows = []
        for k in range(CONV_K):
            s = CONV_K - 1 - k
            if s > 0:
                du = du + w[k:k + 1, :] * pltpu.roll(ext_d, n - s, 0)
                us = pltpu.roll(ext_u, s, 0)[CONV_HALO:, :]
            else:
                us = ext_u[CONV_HALO:, :]
            rows.append(jnp.sum(dyc * us, axis=0, keepdims=True))
        rows.append(jnp.zeros((1, 256), F32))
        dcw = jnp.concatenate(rows, axis=0)
        du = du[0:bt, :]
        av = a_ref[...]
        sg = _sigmoid(g_ref[...])
        da_ref[...] = (du * sg).astype(da_ref.dtype)
        dg_ref[...] = (du * av * (sg * (1.0 - sg))).astype(dg_ref.dtype)
        dcb = jnp.sum(dyc, axis=0, keepdims=True)
        dlg = jnp.sum(dz * yn, axis=0, keepdims=True)
        dlb = jnp.sum(dz, axis=0, keepdims=True)

        @pl.when(i == 0)
        def _():
            dcw_ref[...] = dcw
            dcb_ref[...] = dcb
            dlg_ref[...] = dlg
            dlb_ref[...] = dlb

        @pl.when(i > 0)
        def _():
            dcw_ref[...] += dcw
            dcb_ref[...] += dcb
            dlg_ref[...] += dlg
            dlb_ref[...] += dlb

    def cur(c):
        return pl.BlockSpec((bt, 256), lambda i: (i, c))

    def prev(c):
        return pl.BlockSpec((CONV_HALO, 256), lambda i: (jnp.maximum(i * hb - 1, 0), c))

    def nxt(c):
        return pl.BlockSpec((CONV_HALO, 256), lambda i: (jnp.minimum((i + 1) * hb, nb * hb - 1), c))

    vec = pl.BlockSpec((1, 256), lambda i: (0, 0))
    wfull = pl.BlockSpec((CONV_HALO, 256), lambda i: (0, 0))
    return pl.pallas_call(
        body, grid=(nb,),
        in_specs=[cur(3), nxt(3), cur(0), nxt(0), cur(1), cur(2), prev(1), prev(2), wfull, vec, vec],
        out_specs=[cur(0), cur(0), wfull, vec, vec, vec],
        out_shape=[_sds((T, 256), CDT), _sds((T, 256), CDT), _sds((CONV_HALO, 256), F32),
                   _sds((1, 256), F32), _sds((1, 256), F32), _sds((1, 256), F32)],
        name=name, compiler_params=_cp(("arbitrary",)))(dcat, dcat, yconv, yconv, rest, rest, rest, rest, cw, lg, lb)


def _half_mask(shape, a):
    lane = lax.broadcasted_iota(jnp.int32, shape, 1)
    return (lane // HEAD_DIM) == a


def _attn_fwd(qkv, fcol, frow, name, blk=512):
    T = qkv.shape[0]
    blk = _tile(T, blk)
    nq = T // blk
    nt = (((1,), (1,)), ((), ()))

    def body(q_ref, k_ref, v_ref, fc_ref, fr_ref, o_ref, lse_ref):
        p_id = pl.program_id(0)
        i = pl.program_id(1)
        q2 = q_ref[...]
        fc = fc_ref[...]
        lane = lax.broadcasted_iota(jnp.int32, (blk, LANES), 1)
        tri = lax.broadcasted_iota(jnp.int32, (blk, blk), 1) <= lax.broadcasted_iota(jnp.int32, (blk, blk), 0)
        masks = [_half_mask(q2.shape, a) for a in range(2)]
        qs = [jnp.where(hm, q2, jnp.zeros_like(q2)) * ATT_SCALE for hm in masks]
        fqs = [jnp.sum(jnp.where(lane == 2 * p_id + a, fc, 0.0), axis=1, keepdims=True) for a in range(2)]

        def tile(j, carry, masked):
            cols = pl.ds(pl.multiple_of(j * blk, blk), blk)
            kj = k_ref[cols, :]
            vj = v_ref[cols, :]
            out = []
            for a in range(2):
                m, acc = carry[2 * a:2 * a + 2]
                va = jnp.where(masks[a], vj, jnp.ones_like(vj))
                s = lax.dot_general(qs[a], kj, nt, preferred_element_type=F32) + (fqs[a] - fr_ref[a:a + 1, cols])
                if masked:
                    s = jnp.where(tri, s, NEG)
                m_new = jnp.maximum(m, jnp.max(s, axis=1, keepdims=True))
                alpha = jnp.exp(m - m_new)
                pr = jnp.exp(s - m_new)
                hi = lax.bitcast_convert_type(lax.bitcast_convert_type(pr, jnp.uint32) & jnp.uint32(0xFFFF0000), F32)
                pv = (jnp.dot(hi.astype(CDT), va, preferred_element_type=F32)
                      + jnp.dot((pr - hi).astype(CDT), va, preferred_element_type=F32))
                out += [m_new, alpha * acc + pv]
            return tuple(out)

        init = (jnp.full((blk, 1), NEG, F32), jnp.zeros((blk, LANES), F32)) * 2
        carry = lax.fori_loop(0, i, lambda j, c: tile(j, c, False), init)
        carry = tile(i, carry, True)
        ls = [carry[1][:, HEAD_DIM:HEAD_DIM + 1], carry[3][:, 0:1]]
        lo = lane < HEAD_DIM
        o_ref[...] = jnp.where(lo, carry[1] / ls[0], carry[3] / ls[1])
        lse_t = jnp.transpose(jnp.where(lo, carry[0] + jnp.log(ls[0]), carry[2] + jnp.log(ls[1])))
        lse_ref[...] = jnp.concatenate([lse_t[0:1, :], lse_t[HEAD_DIM:HEAD_DIM + 1, :]], axis=0)

    return pl.pallas_call(
        body, grid=(N_PAIRS, nq),
        in_specs=[pl.BlockSpec((blk, LANES), lambda p, i: (i, p)),
                  pl.BlockSpec((T, LANES), lambda p, i: (0, N_PAIRS + p)),
                  pl.BlockSpec((T, LANES), lambda p, i: (0, 2 * N_PAIRS + p)),
                  pl.BlockSpec((blk, LANES), lambda p, i: (i, 0)),
                  pl.BlockSpec((None, 2, T), lambda p, i: (p, 0, 0))],
        out_specs=[pl.BlockSpec((blk, LANES), lambda p, i: (i, p)), pl.BlockSpec((None, 2, blk), lambda p, i: (p, 0, i))],
        out_shape=[_sds((T, N_PAIRS * LANES), F32), _sds((N_PAIRS, 2, T), F32)],
        name=name, compiler_params=_cp(("parallel", "arbitrary")))(qkv, qkv, qkv, fcol, frow)


def _attn_delta(dcat, o, name, blk=512):
    T = o.shape[0]
    blk = _tile(T, blk)

    def body(d_ref, o_ref, out_ref):
        prod = d_ref[:, 256:768].astype(CDT).astype(F32) * o_ref[...]
        pt = jnp.transpose(prod)
        out_ref[...] = jnp.sum(pt.reshape(N_HEADS, HEAD_DIM, blk), axis=1)

    return pl.pallas_call(
        body, grid=(T // blk,),
        in_specs=[pl.BlockSpec((blk, 1024), lambda i: (i, 0)), pl.BlockSpec((blk, 512), lambda i: (i, 0))],
        out_specs=pl.BlockSpec((N_HEADS, blk), lambda i: (0, i)),
        out_shape=_sds((N_HEADS, T), F32), name=name, compiler_params=_cp(("parallel",)))(dcat, o)


def _attn_bwd(qkv, dcat, fcol, frow, lse, delta, name, blk=512):
    T = qkv.shape[0]
    blk = _tile(T, blk)
    nq = T // blk
    nt = (((1,), (1,)), ((), ()))

    def body(q_ref, do_ref, k_ref, v_ref, fc_ref, fr_ref, lse_ref, dl_ref, dqt_ref, dk_ref, dv_ref, df_ref):
        p_id = pl.program_id(0)
        j = pl.program_id(1)

        @pl.when(j == 0)
        def _():
            dqt_ref[...] = jnp.zeros_like(dqt_ref)

        k2 = k_ref[...]
        v2 = v_ref[...]
        fc = fc_ref[...]
        lane = lax.broadcasted_iota(jnp.int32, (blk, LANES), 1)
        tri = lax.broadcasted_iota(jnp.int32, (blk, blk), 0) <= lax.broadcasted_iota(jnp.int32, (blk, blk), 1)
        masks = [_half_mask(k2.shape, a) for a in range(2)]
        kas = [jnp.where(hm, k2, jnp.zeros_like(k2)) * ATT_SCALE for hm in masks]
        kats = [jnp.transpose(ka) for ka in kas]
        vas = [jnp.where(hm, v2, jnp.zeros_like(v2)) for hm in masks]
        fks = [jnp.sum(jnp.where(lane == 2 * p_id + a, fc, 0.0), axis=1, keepdims=True) for a in range(2)]

        def tile(i, carry, masked):
            rows = pl.ds(pl.multiple_of(i * blk, blk), blk)
            qi = q_ref[rows, :]
            doi = do_ref[rows, :].astype(CDT)
            out = []
            dqt = None
            for a in range(2):
                dk_acc, dv_acc, df_acc = carry[3 * a:3 * a + 3]
                st = lax.dot_general(kas[a], qi, nt, preferred_element_type=F32)
                e = (st + (fr_ref[a:a + 1, rows] - fks[a])) - lse_ref[a:a + 1, rows]
                if masked:
                    e = jnp.where(tri, e, NEG)
                pt = jnp.exp(e)
                dpt = lax.dot_general(vas[a], doi, nt, preferred_element_type=F32)
                ds32 = pt * (dpt - dl_ref[a:a + 1, rows])
                dst = ds32.astype(CDT)
                df_acc = df_acc + jnp.sum(ds32, axis=1, keepdims=True)
                dv_acc = dv_acc + jnp.dot(pt.astype(CDT), doi, preferred_element_type=F32)
                dk_acc = dk_acc + jnp.dot(dst, qi, preferred_element_type=F32)
                part = jnp.dot(kats[a], dst, preferred_element_type=F32)
                dqt = part if dqt is None else dqt + part
                out += [dk_acc, dv_acc, df_acc]
            dqt_ref[:, rows] += dqt
            return tuple(out)

        init = (jnp.zeros((blk, LANES), F32), jnp.zeros((blk, LANES), F32), jnp.zeros((blk, 1), F32)) * 2
        carry = tile(j, init, True)
        carry = lax.fori_loop(j + 1, nq, lambda i, c: tile(i, c, False), carry)
        lo = lane < HEAD_DIM
        dk_ref[...] = (jnp.where(lo, carry[0], carry[3]) * ATT_SCALE).astype(dk_ref.dtype)
        dv_ref[...] = jnp.where(lo, carry[1], carry[4]).astype(dv_ref.dtype)
        df_ref[...] = -jnp.where(lo, carry[2], carry[5])

    res = pl.BlockSpec((T, LANES), lambda p, j: (0, p))
    rows = pl.BlockSpec((None, 2, T), lambda p, j: (p, 0, 0))
    kv_out = pl.BlockSpec((blk, LANES), lambda p, j: (j, p))
    return pl.pallas_call(
        body, grid=(N_PAIRS, nq),
        in_specs=[res, pl.BlockSpec((T, LANES), lambda p, j: (0, 2 + p)),
                  pl.BlockSpec((blk, LANES), lambda p, j: (j, N_PAIRS + p)),
                  pl.BlockSpec((blk, LANES), lambda p, j: (j, 2 * N_PAIRS + p)),
                  pl.BlockSpec((blk, LANES), lambda p, j: (j, 0)), rows, rows, rows],
        out_specs=[pl.BlockSpec((LANES, T), lambda p, j: (p, 0)), kv_out, kv_out, kv_out],
        out_shape=[_sds((N_PAIRS * LANES, T), F32), _sds((T, N_PAIRS * LANES), CDT), _sds((T, N_PAIRS * LANES), CDT),
                   _sds((T, N_PAIRS * LANES), F32)],
        name=name, compiler_params=_cp(("parallel", "arbitrary")))(qkv, dcat, qkv, qkv, fcol, frow, lse, delta)


def _mixer_fwd(x, wts, tag, dep=None):
    T = x.shape[0]
    h = _rms_fwd(x, wts["mix_norm"], f"{tag}_norm", dep)
    qkv = _mm([(h, wts["win_qkv"], False)], out_dtype=CDT, tm=1024, tn=768, name=f"{tag}_in_qkv")
    rest = _mm([(h, wts["win_rest"], False)], tm=1024, name=f"{tag}_in_rest")
    fcol, frow8 = _fgate_fwd(rest, wts["fbias"], f"{tag}_fgate")
    frow = frow8.reshape(N_PAIRS, 2, T)
    ya = _pool_fwd(rest, wts["pool_wbd"], wts["pool_scale"], f"{tag}_pool")
    o, lse = _attn_fwd(qkv, fcol, frow, f"{tag}_attn")
    yc, yconv = _conv_fwd(rest, wts["conv_w"], wts["conv_b"], wts["conv_ln_g"], wts["conv_ln_b"], f"{tag}_conv")
    cat = jnp.concatenate([ya, o.astype(CDT), yc], axis=1)
    y = _mm([(cat, wts["w_out"], False)], res=x, name=f"{tag}_out")
    return y, (x, h, qkv, rest, fcol, frow, o, lse, yconv, cat)


def _mixer_bwd(saved, wts, gout, tag, dep=None):
    x, h, qkv, rest, fcol, frow, o, lse, yconv, cat = saved
    T = x.shape[0]
    dcat = _mm([(gout, wts["w_out"], True)], name=f"{tag}_dcat", dep=dep)
    dwout = _mm_tn(cat, gout, name=f"{tag}_dwout")
    du, dpw, dpsc = _pool_bwd(dcat, rest, wts["pool_wbd"], wts["pool_scale"], f"{tag}_pool_bwd")
    delta = _attn_delta(dcat, o, f"{tag}_attn_delta").reshape(N_PAIRS, 2, T)
    dqt, dk, dv, dfk = _attn_bwd(qkv, dcat, fcol, frow, lse, delta, f"{tag}_attn_bwd")
    dq = dqt.T.astype(CDT)
    dz, dfb = _fgate_bwd(dfk, rest, wts["fbias"], f"{tag}_fgate_bwd")
    da, dg, dcw, dcb, dlg, dlb = _conv_bwd(dcat, yconv, rest, wts["conv_w"], wts["conv_ln_g"], wts["conv_ln_b"],
                                           f"{tag}_conv_bwd")
    dp_qkv = jnp.concatenate([dq, dk, dv], axis=1).astype(CDT)
    dp_rest = jnp.concatenate([du, da, dg, dz], axis=1)
    dwin_qkv = _mm_tn(h, dp_qkv, name=f"{tag}_dwin_qkv")
    dwin_rest = _mm_tn(h, dp_rest, name=f"{tag}_dwin_rest")
    gin, dgamma = _mm_norm_bwd([(dp_qkv, wts["win_qkv"], True), (dp_rest, wts["win_rest"], True)], x, wts["mix_norm"],
                               gout, name=f"{tag}_dh_norm_bwd")
    dwin = _split_win(dwin_qkv, dwin_rest, f"{tag}_dwin_split")
    dpool_w = jnp.stack([dpw[64 * g:64 * g + 64, 64 * g:64 * g + 64] for g in range(4)])
    grads = dict(mix_norm=dgamma[0], w_in=dwin, pool_w=dpool_w, pool_scale=dpsc[0], forget_bias=dfb[0, 0:N_HEADS],
                 conv_w=dcw[0:CONV_K], conv_b=dcb[0], conv_ln_g=dlg[0], conv_ln_b=dlb[0], w_out=dwout)
    return gin, grads


def _rep_layer(rep, l):
    pw = rep["pool_w"][l].astype(CDT)
    wbd = jnp.zeros((256, 256), CDT)
    for g in range(4):
        wbd = lax.dynamic_update_slice(wbd, pw[g], (64 * g, 64 * g))
    return dict(
        ffn1_norm=rep["ffn1_norm"][l][None], ffn2_norm=rep["ffn2_norm"][l][None], mix_norm=rep["mix_norm"][l][None],
        fbias=jnp.pad(rep["forget_bias"][l], (0, LANES - N_HEADS))[None],
        pool_wbd=wbd, pool_scale=rep["pool_scale"][l][None], conv_b=rep["conv_b"][l][None],
        conv_ln_g=rep["conv_ln_g"][l][None], conv_ln_b=rep["conv_ln_b"][l][None])


def _local_step(x, target, rep, weights_for, grads_ready):
    depth = rep["ffn1_norm"].shape[0]
    kept = []
    for l in range(depth):
        r = _rep_layer(rep, l)
        w1, dep = weights_for(l, "ffn1", x)
        x, s1 = _ffn_fwd(x, r["ffn1_norm"], w1["w_gate"], w1["w_up"], w1["w_down"], f"l{l}_ffn1", dep)
        wm, dep = weights_for(l, "mix", x)
        wm = dict(r, win_qkv=wm["win_qkv"], win_rest=wm["win_rest"], w_out=wm["w_out"],
                  conv_w=jnp.pad(wm["conv_w"], ((0, CONV_HALO - CONV_K), (0, 0))))
        x, s2 = _mixer_fwd(x, wm, f"l{l}_mix", dep)
        w2, dep = weights_for(l, "ffn2", x)
        x, s3 = _ffn_fwd(x, r["ffn2_norm"], w2["w_gate"], w2["w_up"], w2["w_down"], f"l{l}_ffn2", dep)
        kept.append((r, w1, wm, w2, s1, s2, s3))
    loss, g, dfinal = _loss_bwd(x, rep["final_norm"][None], target, "loss_head")
    dep = grads_ready(None, "final", dict(final_norm=dfinal[0]))
    for l in reversed(range(depth)):
        r, w1, wm, w2, s1, s2, s3 = kept[l]

        def ffn_grads(which, l=l):
            return lambda gr: grads_ready(l, which, {f"{which}_{k}": v for k, v in gr.items()})

        g, dn = _ffn_bwd(s3, r["ffn2_norm"], w2["w_gate"], w2["w_up"], w2["w_down"], g, f"l{l}_ffn2", dep, ffn_grads("ffn2"))
        grads_ready(l, "norm", dict(ffn2_norm=dn[0]))
        g, gm = _mixer_bwd(s2, wm, g, f"l{l}_mix")
        dep = grads_ready(l, "mix", gm)
        g, dn = _ffn_bwd(s1, r["ffn1_norm"], w1["w_gate"], w1["w_up"], w1["w_down"], g, f"l{l}_ffn1", dep, ffn_grads("ffn1"))
        dep = grads_ready(l, "norm", dict(ffn1_norm=dn[0]))
    return loss, g


def _mesh_pos():
    return lax.axis_index("x"), lax.axis_index("y"), lax.axis_index("c")


def _dev_block(ref, dev, by_rows):
    if by_rows:
        r = ref.shape[1] // N_DEV
        return ref.at[:, pl.ds(dev * r, r), :]
    return ref.at[dev]


def _all_gather(shards, by_rows, name):
    n_arr = len(shards)
    out_shape = [_sds((s.shape[0], N_DEV * s.shape[1], s.shape[2]) if br else (N_DEV,) + s.shape, s.dtype)
                 for s, br in zip(shards, by_rows)]

    def body(*refs):
        xs, outs = refs[:n_arr], refs[n_arr:2 * n_arr]
        send_sems, recv_sems, local_sems = refs[2 * n_arr:]
        x, y, c = _mesh_pos()
        me, sibling = (x, y, c), (x, y, 1 - c)
        chips = [(1 - x, y), (x, 1 - y), (1 - x, 1 - y)]

        def rows(a, px, py, pc):
            return _dev_block(outs[a], 4 * px + 2 * py + pc, by_rows[a])

        def copy(k, a, block, to, src=None):
            return pltpu.make_async_remote_copy(
                src_ref=rows(a, *block) if src is None else src, dst_ref=rows(a, *block),
                send_sem=send_sems.at[k, a], recv_sem=recv_sems.at[k, a],
                device_id=to, device_id_type=pl.DeviceIdType.MESH)

        arrs = range(n_arr)
        mine = [pltpu.make_async_copy(xs[a], rows(a, *me), local_sems.at[a]) for a in arrs]
        for cp in mine:
            cp.start()
        first = [copy(0, a, me, sibling, src=xs[a]) for a in arrs]
        first += [copy(1 + j, a, me, (*chip, c), src=xs[a]) for j, chip in enumerate(chips) for a in arrs]
        for cp in first:
            cp.start()
        passed = []
        for j, chip in enumerate(chips):
            for a in arrs:
                copy(1 + j, a, (*chip, c), me).wait_recv()
                passed.append(copy(4 + j, a, (*chip, c), sibling))
                passed[-1].start()
        for a in arrs:
            copy(0, a, sibling, me).wait_recv()
        for j, chip in enumerate(chips):
            for a in arrs:
                copy(4 + j, a, (*chip, 1 - c), me).wait_recv()
        for cp in first + passed:
            cp.wait_send()
        for cp in mine:
            cp.wait()

    hbm = pl.BlockSpec(memory_space=pl.ANY)
    return pl.pallas_call(
        body, out_shape=out_shape, in_specs=[hbm] * n_arr, out_specs=[hbm] * n_arr,
        scratch_shapes=[pltpu.SemaphoreType.DMA((7, n_arr)), pltpu.SemaphoreType.DMA((7, n_arr)),
                        pltpu.SemaphoreType.DMA((n_arr,))],
        name=name)(*shards)


def _exchange(parts, by_rows, name):
    n_arr = len(parts)
    out_shape = [_sds((N_DEV, p.shape[0], p.shape[1] // N_DEV, p.shape[2]) if br else p.shape, p.dtype)
                 for p, br in zip(parts, by_rows)]

    def body(*refs):
        ps, outs = refs[:n_arr], refs[n_arr:2 * n_arr]
        send_sems, recv_sems, local_sems = refs[2 * n_arr:]
        x, y, c = _mesh_pos()
        my = 4 * x + 2 * y + c
        arrs = range(n_arr)
        mine = [pltpu.make_async_copy(_dev_block(ps[a], my, by_rows[a]), outs[a].at[my], local_sems.at[a]) for a in arrs]
        for cp in mine:
            cp.start()
        copies = []
        for k in range(1, N_DEV):
            px, py, pc = x ^ (k >> 2), y ^ ((k >> 1) & 1), c ^ (k & 1)
            for a in arrs:
                copies.append(pltpu.make_async_remote_copy(
                    src_ref=_dev_block(ps[a], 4 * px + 2 * py + pc, by_rows[a]), dst_ref=outs[a].at[my],
                    send_sem=send_sems.at[k - 1, a], recv_sem=recv_sems.at[k - 1, a],
                    device_id=(px, py, pc), device_id_type=pl.DeviceIdType.MESH))
        for cp in copies:
            cp.start()
        for cp in copies:
            cp.wait()
        for cp in mine:
            cp.wait()

    hbm = pl.BlockSpec(memory_space=pl.ANY)
    return pl.pallas_call(
        body, out_shape=out_shape, in_specs=[hbm] * n_arr, out_specs=[hbm] * n_arr,
        scratch_shapes=[pltpu.SemaphoreType.DMA((7, n_arr)), pltpu.SemaphoreType.DMA((7, n_arr)),
                        pltpu.SemaphoreType.DMA((n_arr,))],
        name=name)(*parts)


def _peer_copies(srcs, lands, send_sems, recv_sems, gather, by_rows):
    n_arr = len(srcs)
    x, y, c = _mesh_pos()
    my = 4 * x + 2 * y + c
    out = []
    for k in range(1, N_DEV):
        px, py, pc = x ^ (k >> 2), y ^ ((k >> 1) & 1), c ^ (k & 1)
        peer = 4 * px + 2 * py + pc
        for a in range(n_arr):
            src = srcs[a] if gather else _dev_block(srcs[a], peer, by_rows[a])
            dst = _dev_block(lands[a], my, by_rows[a]) if gather else lands[a].at[my]
            out.append(pltpu.make_async_remote_copy(
                src_ref=src, dst_ref=dst, send_sem=send_sems.at[(k - 1) * n_arr + a],
                recv_sem=recv_sems.at[(k - 1) * n_arr + a], device_id=(px, py, pc), device_id_type=pl.DeviceIdType.MESH))
    return out


def _land_shape(s, gather, by_rows):
    if gather:
        return (s.shape[0], N_DEV * s.shape[1], s.shape[2]) if by_rows else (N_DEV,) + s.shape
    return (N_DEV, s.shape[0], s.shape[1] // N_DEV, s.shape[2]) if by_rows else s.shape


_HBM = pl.BlockSpec(memory_space=pltpu.HBM)
_SEM = pl.BlockSpec(memory_space=pltpu.SEMAPHORE)


def _xfer_start(srcs, gather, by_rows, name, dep=None):
    n = len(srcs)
    x, y, c = _mesh_pos()
    my = 4 * x + 2 * y + c
    lands = []
    for src, br in zip(srcs, by_rows):
        land = lax.empty(_land_shape(src, gather, br), src.dtype)
        zeros = (0,) * (land.ndim - 1)
        if gather and br:
            own, at = src, (0, my * src.shape[1], 0)
        elif gather:
            own, at = src[None], (my,) + zeros
        elif br:
            r = src.shape[1] // N_DEV
            own = lax.dynamic_slice(src, (0, my * r, 0), (src.shape[0], r, src.shape[2]))[None]
            at = (my,) + zeros
        else:
            own, at = lax.dynamic_index_in_dim(src, my, 0, keepdims=True), (my,) + zeros
        lands.append(lax.dynamic_update_slice(land, own, at))
    ins = [pltpu.with_memory_space_constraint(a, pltpu.HBM) for a in list(srcs) + lands]
    dspec, darg = _dep(dep)

    def body(*refs):
        s = 2 * n + len(darg)
        for cp in _peer_copies(refs[:n], refs[n:2 * n], refs[s], refs[s + 1], gather, by_rows):
            cp.start()
        refs[-1][...] = jnp.zeros_like(refs[-1])

    sems = pltpu.SemaphoreType.DMA(((N_DEV - 1) * n,))
    outs = pl.pallas_call(
        body, name=name,
        out_shape=(sems, sems, *[pltpu.HBM(a.shape, a.dtype) for a in ins], _sds((8, LANES), F32)),
        in_specs=[_HBM] * (2 * n) + dspec,
        out_specs=(_SEM, _SEM, *[_HBM] * (2 * n), pl.BlockSpec(memory_space=pltpu.VMEM)),
        input_output_aliases={i: 2 + i for i in range(2 * n)},
        compiler_params=pltpu.CompilerParams(has_side_effects=pltpu.SideEffectType.DATAFLOW_SIDE_EFFECTING))(*ins, *darg)
    return outs[0], outs[1], list(outs[2:-1]), outs[-1]


def _xfer_wait(started, after, gather, by_rows, name):
    send_sems, recv_sems, bufs, _ = started
    n = len(bufs) // 2

    def body(*refs):
        for cp in _peer_copies(refs[:n], refs[n:2 * n], refs[2 * n], refs[2 * n + 1], gather, by_rows):
            cp.wait_send()
            cp.wait_recv()

    outs = pl.pallas_call(
        body, name=name, out_shape=tuple(pltpu.HBM(a.shape, a.dtype) for a in bufs),
        in_specs=[_HBM] * (2 * n) + [_SEM, _SEM, pl.BlockSpec(memory_space=pl.ANY)], out_specs=tuple([_HBM] * (2 * n)),
        input_output_aliases={i: i for i in range(2 * n)},
        compiler_params=pltpu.CompilerParams(has_side_effects=pltpu.SideEffectType.DATAFLOW_SIDE_EFFECTING))(
            *bufs, send_sems, recv_sems, after)
    return list(outs[n:])


def _adam_update(g, w, m, v):
    c1 = 1.0 - ADAM_B1 ** ADAM_STEP
    c2 = 1.0 - ADAM_B2 ** ADAM_STEP
    nm = ADAM_B1 * m + (1.0 - ADAM_B1) * g
    nv = ADAM_B2 * v + (1.0 - ADAM_B2) * (g * g)
    return -ADAM_LR * ((nm / c1) / (jnp.sqrt(nv / c2) + ADAM_EPS) + ADAM_WD * w), nm, nv


def _adamw_body(p_ref, w_ref, m_ref, v_ref, g_ref, d_ref, nm_ref, nv_ref):
    g = p_ref[0]
    for i in range(1, N_DEV):
        g = g + p_ref[i]
    g_ref[...] = g
    d_ref[...], nm_ref[...], nv_ref[...] = _adam_update(g, w_ref[...], m_ref[...], v_ref[...])


def _adamw(parts, w, m, v, name, tr=1536):
    R = w.shape[0]
    tr = max(t for t in range(8, tr + 1, 8) if R % t == 0)

    def body(*refs):
        _adamw_body(*refs)

    row = pl.BlockSpec((tr, LANES), lambda i: (i, 0))
    return pl.pallas_call(
        body, grid=(R // tr,),
        in_specs=[pl.BlockSpec((N_DEV, tr, LANES), lambda i: (0, i, 0)), row, row, row],
        out_specs=[row, row, row, row], out_shape=[_sds((R, LANES), F32)] * 4,
        name=name, compiler_params=_cp(("parallel",)))(parts, w, m, v)


def _adamw_split(recvs, w, m, v, name, tr):
    depth, r, c = w.shape
    assert depth == len(recvs)
    tr = _tile(r, tr)

    def body(*refs):
        layer = pl.program_id(0)
        for ll in range(depth):
            @pl.when(layer == ll)
            def _(ll=ll):
                _adamw_body(refs[ll], *refs[depth:])

    wspec = pl.BlockSpec((None, tr, c), lambda l, i: (l, i, 0))
    rspecs = [pl.BlockSpec((N_DEV, None, tr, c), lambda l, i, ll=ll, t=t: (0, t, jnp.where(l == ll, i, 0), 0))
              for ll, (_, t) in enumerate(recvs)]
    return pl.pallas_call(
        body, grid=(depth, r // tr), in_specs=rspecs + [wspec, wspec, wspec],
        out_specs=[wspec] * 4, out_shape=[_sds(w.shape, F32)] * 4,
        name=name, compiler_params=_cp(("arbitrary", "arbitrary")))(*[a for a, _ in recvs], w, m, v)


def _merge_win(g, name, tr=256):
    _, nt, K, n = g.shape
    tr = _tile(K, tr)

    def body(g_ref, q_ref, r_ref):
        full = jnp.concatenate([g_ref[j] for j in range(N_DEV)], axis=1)
        q_ref[...] = full[:, 256:1792]
        zpad = jnp.zeros((tr, REST_W - 776), full.dtype)
        r_ref[...] = jnp.concatenate([full[:, 0:256], full[:, 1800:2312], full[:, 1792:1800], zpad], axis=1)

    return pl.pallas_call(
        body, grid=(nt, K // tr),
        in_specs=[pl.BlockSpec((N_DEV, None, tr, n), lambda t, i: (0, t, i, 0))],
        out_specs=[pl.BlockSpec((None, tr, 1536), lambda t, i: (t, i, 0)), pl.BlockSpec((None, tr, REST_W), lambda t, i: (t, i, 0))],
        out_shape=[_sds((nt, K, 1536), g.dtype), _sds((nt, K, REST_W), g.dtype)],
        name=name, compiler_params=_cp(("parallel", "parallel")))(g)


def _split_win(dq, dr, name, tr=256):
    K = dq.shape[0]
    tr = _tile(K, tr)
    n = (dq.shape[1] + 776) // N_DEV

    def body(q_ref, r_ref, o_ref):
        r = r_ref[...]
        full = jnp.concatenate([r[:, 0:256], q_ref[...], r[:, 768:776], r[:, 256:768]], axis=1)
        for j in range(N_DEV):
            o_ref[j] = full[:, n * j:n * (j + 1)]

    return pl.pallas_call(
        body, grid=(K // tr,),
        in_specs=[pl.BlockSpec((tr, dq.shape[1]), lambda i: (i, 0)), pl.BlockSpec((tr, REST_W), lambda i: (i, 0))],
        out_specs=pl.BlockSpec((N_DEV, tr, n), lambda i: (0, i, 0)),
        out_shape=_sds((N_DEV, K, n), F32), name=name, compiler_params=_cp(("parallel",)))(dq, dr)


WEIGHTS = ["ffn1_norm", "ffn1_w_gate", "ffn1_w_up", "ffn1_w_down", "mix_norm", "w_in", "pool_w", "pool_scale",
           "forget_bias", "conv_w", "conv_b", "conv_ln_g", "conv_ln_b", "w_out", "ffn2_norm", "ffn2_w_gate",
           "ffn2_w_up", "ffn2_w_down", "final_norm"]
FFN_PARTS = ("w_gate", "w_up", "w_down")
FFN_T = ["ffn1_w_gate", "ffn1_w_up", "ffn2_w_gate", "ffn2_w_up"]
BIG = FFN_T + ["ffn1_w_down", "ffn2_w_down", "w_in", "w_out"]
SMALL = [n for n in WEIGHTS if n not in BIG]


def _padded(n):
    return -(-n // PACK_ALIGN) * PACK_ALIGN


def _flat_pad(a):
    f = a.reshape(-1)
    return jnp.pad(f, (0, _padded(f.shape[0]) - f.shape[0]))


def _split8(a, axis):
    shp = a.shape
    a = a.reshape(shp[:axis] + (N_DEV, shp[axis] // N_DEV) + shp[axis + 1:])
    return jnp.moveaxis(a, axis, 0)


def _merge8(a, axis):
    a = jnp.moveaxis(a, 0, axis)
    shp = a.shape
    return a.reshape(shp[:axis] + (shp[axis] * shp[axis + 1],) + shp[axis + 2:])


def _pack_small(arrs):
    return jnp.concatenate([_flat_pad(arrs[n]) for n in SMALL]).reshape(-1, LANES)


def _pack_small_parts(grads):
    cols = []
    for n in SMALL:
        g = grads[n]
        if n == "conv_w":
            s = _split8(g, 2).reshape(N_DEV, -1)
        else:
            s = jnp.broadcast_to(g.reshape(1, -1), (N_DEV, g.size))
        cols.append(jnp.pad(s, ((0, 0), (0, _padded(s.shape[1]) - s.shape[1]))))
    return jnp.concatenate(cols, axis=1).reshape(N_DEV, -1, LANES)


def _unpack_small(buf, like):
    flat = buf.reshape(-1)
    out, off = {}, 0
    for n in SMALL:
        size = like[n].size
        out[n] = flat[off:off + size].reshape(like[n].shape)
        off += _padded(size)
    return out


class _Comm:
    def __init__(self, w):
        self.w = w
        self.bf = {n: (jnp.swapaxes(w[n], 1, 2) if n in FFN_T else w[n]).astype(CDT) for n in BIG}
        self.ready = {}
        self.grads = {}

    def _ffn_shards(self, l, which):
        return jnp.stack([self.bf[f"{which}_{k}"][l] for k in FFN_PARTS])

    def _put_ffn(self, l, which, rows, t):
        self.ready[(l, which)] = dict(w_gate=rows[t], w_up=rows[t + 1], w_down=rows[t + 2])

    def weights_for(self, l, stage, x):
        bf = self.bf
        dep = None
        if (l, stage) == (0, "ffn1"):
            gd, = _all_gather([self._ffn_shards(0, "ffn1")], [True], "gather_l0_ffn1")
            self._put_ffn(0, "ffn1", gd, 0)
            self.started = _xfer_start([bf["w_in"][0:1], bf["w_out"][0:1], self.w["conv_w"]], True,
                                       [False, True, False], "gather_mix0_start", dep=gd)
            dep = self.started[3]
        elif (l, stage) == (0, "mix"):
            gi, go, gc = _xfer_wait(self.started, x, True, [False, True, False], "gather_mix0_wait")
            q, r = _merge_win(gi, "merge_l0_w_in")
            self.conv_w = _merge8(gc, 2)
            self.ready[(0, "mix")] = dict(win_qkv=q[0], win_rest=r[0], w_out=go[0], conv_w=self.conv_w[0])
            rows = jnp.concatenate([self._ffn_shards(0, "ffn2"), self._ffn_shards(1, "ffn1"), self._ffn_shards(1, "ffn2")])
            self.started = _xfer_start([rows, bf["w_in"][1:2], bf["w_out"][1:2]], True, [True, False, True],
                                       "gather_rest_start")
            dep = self.started[3]
        elif (l, stage) == (0, "ffn2"):
            gd, gi, go = _xfer_wait(self.started, x, True, [True, False, True], "gather_rest_wait")
            self._put_ffn(0, "ffn2", gd, 0)
            self._put_ffn(1, "ffn1", gd, 3)
            self._put_ffn(1, "ffn2", gd, 6)
            q, r = _merge_win(gi, "merge_l1_w_in")
            self.ready[(1, "mix")] = dict(win_qkv=q[0], win_rest=r[0], w_out=go[0], conv_w=self.conv_w[1])
        return self.ready[(l, stage)], dep

    def grads_ready(self, l, stage, grads):
        for n, v in grads.items():
            self.grads[(l, n)] = v
        gr = self.grads

        def ffn_rows(layer, which, parts=FFN_PARTS):
            return [gr[(layer, f"{which}_{k}")] for k in parts]

        if l == 1 and "ffn1_w_gate" in grads:
            self.sent1 = _xfer_start(
                [jnp.stack(ffn_rows(1, "ffn1") + ffn_rows(1, "ffn2")), gr[(1, "w_in")][:, None], gr[(1, "w_out")][None]],
                False, [True, False, True], "grads_l1_start")
            return self.sent1[3]
        if l == 0 and "ffn2_w_gate" in grads:
            self.sent_ffn2 = _xfer_start([jnp.stack(ffn_rows(0, "ffn2"))], False, [True], "grads_l0_ffn2_start")
            return self.sent_ffn2[3]
        if (l, stage) == (0, "mix"):
            self.sent_mix = _xfer_start([gr[(0, "w_in")][:, None], gr[(0, "w_out")][None]], False, [False, True],
                                        "grads_l0_mix_start")
            return self.sent_mix[3]
        if l == 0 and "ffn1_w_down" in grads:
            self.sent_down = _xfer_start([gr[(0, "ffn1_w_down")][None]], False, [True], "grads_l0_ffn1_down_start")
            return self.sent_down[3]
        if l == 0 and "ffn1_w_gate" in grads:
            self.sent_gu = _xfer_start([jnp.stack(ffn_rows(0, "ffn1", FFN_PARTS[:2]))], False, [True],
                                       "grads_l0_ffn1_gate_up_start")
            return self.sent_gu[3]
        return None

    def finish(self, m, v, after):
        w, gr = self.w, self.grads
        depth = range(w["w_in"].shape[0])
        small = {n: (gr[(None, n)] if n == "final_norm" else jnp.stack([gr[(l, n)] for l in depth])) for n in SMALL}
        r1, i1, o1 = _xfer_wait(self.sent1, after, False, [True, False, True], "grads_l1_wait")
        r2, = _xfer_wait(self.sent_ffn2, after, False, [True], "grads_l0_ffn2_wait")
        i0, o0 = _xfer_wait(self.sent_mix, after, False, [False, True], "grads_l0_mix_wait")

        def adam(n, recvs, tr):
            if n in FFN_T:
                out = _adamw_split(recvs, *[jnp.swapaxes(t[n], 1, 2) for t in (w, m, v)], f"adamw_{n}", tr)
                return [jnp.swapaxes(o, 1, 2) for o in out]
            return _adamw_split(recvs, w[n], m[n], v[n], f"adamw_{n}", tr)

        res = {}
        for t, k in enumerate(FFN_PARTS):
            res[f"ffn2_{k}"] = adam(f"ffn2_{k}", [(r2, t), (r1, 3 + t)], 176)
        res["w_in"] = adam("w_in", [(i0, 0), (i1, 0)], 256)
        res["w_out"] = adam("w_out", [(o0, 0), (o1, 0)], 128)
        rs, = _exchange([_pack_small_parts(small)], [False], "exchange_small")
        r0, = _xfer_wait(self.sent_down, res["w_out"][0], False, [True], "grads_l0_ffn1_down_wait")
        res["ffn1_w_down"] = adam("ffn1_w_down", [(r0, 0), (r1, 2)], 176)
        g0, = _xfer_wait(self.sent_gu, res["ffn1_w_down"][0], False, [True], "grads_l0_ffn1_gate_up_wait")
        res["ffn1_w_gate"] = adam("ffn1_w_gate", [(g0, 0), (r1, 0)], 176)
        res["ffn1_w_up"] = adam("ffn1_w_up", [(g0, 1), (r1, 1)], 176)
        packed = _adamw(rs, _pack_small(w), _pack_small(m), _pack_small(v), "adamw_small")
        unpacked = [_unpack_small(b, w) for b in packed]
        for n in SMALL:
            res[n] = [u[n] for u in unpacked]
        return res


def kernel(x, ffn1_norm, ffn1_w_gate, ffn1_w_up, ffn1_w_down, mix_norm, w_in, pool_w, pool_scale, forget_bias, conv_w, conv_b, conv_ln_g, conv_ln_b, w_out, ffn2_norm, ffn2_w_gate, ffn2_w_up, ffn2_w_down, final_norm, loss_target, m_ffn1_norm, m_ffn1_w_gate, m_ffn1_w_up, m_ffn1_w_down, m_mix_norm, m_w_in, m_pool_w, m_pool_scale, m_forget_bias, m_conv_w, m_conv_b, m_conv_ln_g, m_conv_ln_b, m_w_out, m_ffn2_norm, m_ffn2_w_gate, m_ffn2_w_up, m_ffn2_w_down, m_final_norm, v_ffn1_norm, v_ffn1_w_gate, v_ffn1_w_up, v_ffn1_w_down, v_mix_norm, v_w_in, v_pool_w, v_pool_scale, v_forget_bias, v_conv_w, v_conv_b, v_conv_ln_g, v_conv_ln_b, v_w_out, v_ffn2_norm, v_ffn2_w_gate, v_ffn2_w_up, v_ffn2_w_down, v_final_norm):
    w = dict(zip(WEIGHTS, (ffn1_norm, ffn1_w_gate, ffn1_w_up, ffn1_w_down, mix_norm, w_in, pool_w, pool_scale, forget_bias,
                           conv_w, conv_b, conv_ln_g, conv_ln_b, w_out, ffn2_norm, ffn2_w_gate, ffn2_w_up, ffn2_w_down,
                           final_norm)))
    m = dict(zip(WEIGHTS, (m_ffn1_norm, m_ffn1_w_gate, m_ffn1_w_up, m_ffn1_w_down, m_mix_norm, m_w_in, m_pool_w, m_pool_scale,
                           m_forget_bias, m_conv_w, m_conv_b, m_conv_ln_g, m_conv_ln_b, m_w_out, m_ffn2_norm, m_ffn2_w_gate,
                           m_ffn2_w_up, m_ffn2_w_down, m_final_norm)))
    v = dict(zip(WEIGHTS, (v_ffn1_norm, v_ffn1_w_gate, v_ffn1_w_up, v_ffn1_w_down, v_mix_norm, v_w_in, v_pool_w, v_pool_scale,
                           v_forget_bias, v_conv_w, v_conv_b, v_conv_ln_g, v_conv_ln_b, v_w_out, v_ffn2_norm, v_ffn2_w_gate,
                           v_ffn2_w_up, v_ffn2_w_down, v_final_norm)))
    comm = _Comm(w)
    loss_row, gx = _local_step(x[0], loss_target[0], w, comm.weights_for, comm.grads_ready)
    loss = lax.psum(loss_row[0, 0], ("x", "y", "c"))
    res = comm.finish(m, v, gx)
    return (loss, gx[None], *[res[n][i] for i in range(4) for n in WEIGHTS])
```

```python
import math

import numpy as np
import jax
import jax.numpy as jnp
from jax import lax
from jax.experimental import pallas as pl
from jax.experimental.pallas import tpu as pltpu

F32 = jnp.float32
CDT = jnp.bfloat16
NORM_EPS = 1e-6
N_DEV = 8
LANES = 128
PACK_ALIGN = 8 * LANES
VMEM_LIMIT = 48 * 1024 * 1024

POOL_WINDOWS = (2, 4, 8, 16)
POOL_HALO = 16
CONV_K = 31
CONV_HALO = 32
HEAD_DIM = 64
N_HEADS = 8
N_PAIRS = N_HEADS // 2
ATT_SCALE = 1.0 / math.sqrt(HEAD_DIM)
NEG = -1e30

ADAM_LR, ADAM_B1, ADAM_B2, ADAM_EPS, ADAM_WD, ADAM_STEP = 0.001, 0.9, 0.999, 1e-08, 0.01, 10

REST_W = 896
REST_Z_BLK = 6


def _cp(sem):
    return pltpu.CompilerParams(dimension_semantics=sem, vmem_limit_bytes=VMEM_LIMIT)


def _tile(n, pref):
    t = min(n, pref)
    assert n % t == 0, (n, pref)
    return t


def _sigmoid(x):
    return 1.0 / (1.0 + jnp.exp(-x))


def _sds(shape, dtype):
    return jax.ShapeDtypeStruct(shape, dtype)


_ANY = pl.BlockSpec(memory_space=pl.ANY)


def _dep(dep):
    return ([], []) if dep is None else ([_ANY], [dep])


def _wshape(w):
    return w[0].shape[1:] if isinstance(w, tuple) else w.shape


def _wspec(w, block, index_map):
    if not isinstance(w, tuple):
        return w, pl.BlockSpec(block, index_map)
    arr, t = w
    return arr, pl.BlockSpec((None,) + block, lambda *g: (t,) + index_map(*g))


def _rms_fwd(x, g, name, dep=None):
    T, D = x.shape
    tm = _tile(T, 1024)

    def body(x_ref, g_ref, *rest):
        o_ref = rest[-1]
        xv = x_ref[...]
        r = lax.rsqrt(jnp.mean(xv * xv, axis=-1, keepdims=True) + NORM_EPS)
        o_ref[...] = (xv * r * g_ref[...]).astype(o_ref.dtype)

    dspec, darg = _dep(dep)
    return pl.pallas_call(
        body, grid=(T // tm,),
        in_specs=[pl.BlockSpec((tm, D), lambda i: (i, 0)), pl.BlockSpec((1, D), lambda i: (0, 0))] + dspec,
        out_specs=pl.BlockSpec((tm, D), lambda i: (i, 0)),
        out_shape=_sds((T, D), CDT), name=name, compiler_params=_cp(("parallel",)))(x, g, *darg)


def _rms_bwd(x, g, dh, gres, name):
    T, D = x.shape
    tm = _tile(T, 512)

    def body(x_ref, g_ref, dh_ref, gres_ref, gin_ref, dg_ref):
        i = pl.program_id(0)
        xv = x_ref[...]
        d = dh_ref[...]
        r = lax.rsqrt(jnp.mean(xv * xv, axis=-1, keepdims=True) + NORM_EPS)
        xh = xv * r
        dxh = d * g_ref[...]
        c = jnp.mean(dxh * xh, axis=-1, keepdims=True)
        gin_ref[...] = gres_ref[...] + r * (dxh - xh * c)
        part = jnp.sum(d * xh, axis=0, keepdims=True)

        @pl.when(i == 0)
        def _():
            dg_ref[...] = part

        @pl.when(i > 0)
        def _():
            dg_ref[...] += part

    row = pl.BlockSpec((tm, D), lambda i: (i, 0))
    vec = pl.BlockSpec((1, D), lambda i: (0, 0))
    return pl.pallas_call(
        body, grid=(T // tm,), in_specs=[row, vec, row, row], out_specs=[row, vec],
        out_shape=[_sds((T, D), F32), _sds((1, D), F32)], name=name, compiler_params=_cp(("arbitrary",)))(x, g, dh, gres)


def _loss_bwd(x, g, target, name):
    T, D = x.shape
    tm = _tile(T, 512)

    def body(x_ref, g_ref, t_ref, loss_ref, dx_ref, dg_ref):
        i = pl.program_id(0)
        xv = x_ref[...]
        gv = g_ref[...]
        r = lax.rsqrt(jnp.mean(xv * xv, axis=-1, keepdims=True) + NORM_EPS)
        xh = xv * r
        err = xh * gv - t_ref[...]
        lpart = 0.5 * jnp.sum(jnp.mean(err * err, axis=-1, keepdims=True), axis=0, keepdims=True)
        dy = err * (1.0 / D)
        dxh = dy * gv
        c = jnp.mean(dxh * xh, axis=-1, keepdims=True)
        dx_ref[...] = r * (dxh - xh * c)
        part = jnp.sum(dy * xh, axis=0, keepdims=True)
        lrow = jnp.broadcast_to(lpart, (1, LANES))

        @pl.when(i == 0)
        def _():
            dg_ref[...] = part
            loss_ref[...] = lrow

        @pl.when(i > 0)
        def _():
            dg_ref[...] += part
            loss_ref[...] += lrow

    row = pl.BlockSpec((tm, D), lambda i: (i, 0))
    vec = pl.BlockSpec((1, D), lambda i: (0, 0))
    return pl.pallas_call(
        body, grid=(T // tm,), in_specs=[row, vec, row],
        out_specs=[pl.BlockSpec((1, LANES), lambda i: (0, 0)), row, vec],
        out_shape=[_sds((1, LANES), F32), _sds((T, D), F32), _sds((1, D), F32)],
        name=name, compiler_params=_cp(("arbitrary",)))(x, g, target)


def _mm(pairs, *, name, res=None, alpha=1.0, out_dtype=F32, tm=512, tn=None, dep=None):
    T = pairs[0][0].shape[0]
    N = _wshape(pairs[0][1])[0 if pairs[0][2] else 1]
    tm = _tile(T, tm)
    tn = N if tn is None else _tile(N, tn)
    flags = [p[2] for p in pairs]
    n_in = 2 * len(pairs)

    def body(*refs):
        o_ref = refs[-1]
        acc = None
        for p, bt in enumerate(flags):
            a = refs[2 * p][...].astype(CDT)
            b = refs[2 * p + 1][...]
            dims = (((1,), (1,)), ((), ())) if bt else (((1,), (0,)), ((), ()))
            d = lax.dot_general(a, b, dims, preferred_element_type=F32)
            acc = d if acc is None else acc + d
        if alpha != 1.0:
            acc = acc * alpha
        if res is not None:
            acc = refs[n_in][...] + acc
        o_ref[...] = acc.astype(o_ref.dtype)

    in_specs, args = [], []
    for a, b, bt in pairs:
        K = a.shape[1]
        in_specs.append(pl.BlockSpec((tm, K), lambda i, j: (i, 0)))
        b, bspec = _wspec(b, (tn, K), lambda i, j: (j, 0)) if bt else _wspec(b, (K, tn), lambda i, j: (0, j))
        in_specs.append(bspec)
        args += [a, b]
    if res is not None:
        in_specs.append(pl.BlockSpec((tm, tn), lambda i, j: (i, j)))
        args.append(res)
    dspec, darg = _dep(dep)
    in_specs += dspec
    args += darg
    return pl.pallas_call(
        body, grid=(T // tm, N // tn), in_specs=in_specs,
        out_specs=pl.BlockSpec((tm, tn), lambda i, j: (i, j)),
        out_shape=_sds((T, N), out_dtype), name=name, compiler_params=_cp(("parallel", "arbitrary")))(*args)


def _mm_norm_bwd(pairs, x, g, gres, *, name, tm=256, dep=None):
    T, D = x.shape
    tm = _tile(T, tm)
    n_in = 2 * len(pairs)
    flags = [p[2] for p in pairs]

    def body(*refs):
        x_ref, g_ref, gres_ref = refs[n_in:n_in + 3]
        gin_ref, dg_ref = refs[-2:]
        i = pl.program_id(0)
        d = None
        for p, bt in enumerate(flags):
            dims = (((1,), (1,)), ((), ())) if bt else (((1,), (0,)), ((), ()))
            part = lax.dot_general(refs[2 * p][...].astype(CDT), refs[2 * p + 1][...], dims, preferred_element_type=F32)
            d = part if d is None else d + part
        xv = x_ref[...]
        r = lax.rsqrt(jnp.mean(xv * xv, axis=-1, keepdims=True) + NORM_EPS)
        xh = xv * r
        dxh = d * g_ref[...]
        c = jnp.mean(dxh * xh, axis=-1, keepdims=True)
        gin_ref[...] = gres_ref[...] + r * (dxh - xh * c)
        part = jnp.sum(d * xh, axis=0, keepdims=True)

        @pl.when(i == 0)
        def _():
            dg_ref[...] = part

        @pl.when(i > 0)
        def _():
            dg_ref[...] += part

    in_specs, args = [], []
    for a, b, bt in pairs:
        K = a.shape[1]
        b, bspec = _wspec(b, tuple(_wshape(b)), lambda i: (0, 0))
        in_specs += [pl.BlockSpec((tm, K), lambda i: (i, 0)), bspec]
        args += [a, b]
    row = pl.BlockSpec((tm, D), lambda i: (i, 0))
    vec = pl.BlockSpec((1, D), lambda i: (0, 0))
    dspec, darg = _dep(dep)
    return pl.pallas_call(
        body, grid=(T // tm,), in_specs=in_specs + [row, vec, row] + dspec, out_specs=[row, vec],
        out_shape=[_sds((T, D), F32), _sds((1, D), F32)], name=name,
        compiler_params=_cp(("arbitrary",)))(*args, x, g, gres, *darg)


def _mm_tn(a, b, *, name, alpha=1.0, tk=2048, dep=None):
    T, M = a.shape
    N = b.shape[1]
    tm = M if M <= 1024 else M // 2
    tn = N if N <= 1536 else N // 2
    assert M % tm == 0 and N % tn == 0 and tm % LANES == 0 and tn % LANES == 0
    tk = _tile(T, tk)
    nk = T // tk

    def body(a_ref, b_ref, *rest):
        o_ref = rest[-1]
        k = pl.program_id(2)
        d = lax.dot_general(a_ref[...].astype(CDT), b_ref[...].astype(CDT), (((0,), (0,)), ((), ())),
                            preferred_element_type=F32)

        @pl.when(k == 0)
        def _():
            o_ref[...] = d

        @pl.when(k > 0)
        def _():
            o_ref[...] += d

        if alpha != 1.0:
            @pl.when(k == nk - 1)
            def _():
                o_ref[...] *= alpha

    dspec, darg = _dep(dep)
    return pl.pallas_call(
        body, grid=(M // tm, N // tn, nk),
        in_specs=[pl.BlockSpec((tk, tm), lambda i, j, k: (k, i)), pl.BlockSpec((tk, tn), lambda i, j, k: (k, j))] + dspec,
        out_specs=pl.BlockSpec((tm, tn), lambda i, j, k: (i, j)),
        out_shape=_sds((M, N), F32), name=name, compiler_params=_cp(("parallel", "parallel", "arbitrary")))(a, b, *darg)


def _ffn_up(h, wgt, wut, name):
    T, D = h.shape
    Fh = _wshape(wgt)[0]
    tm = _tile(T, 2048)
    tn = _tile(Fh, 256)
    nt = (((1,), (1,)), ((), ()))

    def body(h_ref, wg_ref, wu_ref, a_ref, b_ref, s_ref):
        hv = h_ref[...]
        a = lax.dot_general(hv, wg_ref[...], nt, preferred_element_type=F32)
        b = lax.dot_general(hv, wu_ref[...], nt, preferred_element_type=F32)
        a_ref[...] = a.astype(a_ref.dtype)
        b_ref[...] = b.astype(b_ref.dtype)
        s_ref[...] = (a * _sigmoid(a) * b).astype(s_ref.dtype)

    wgt, gspec = _wspec(wgt, (tn, D), lambda i, j: (j, 0))
    wut, uspec = _wspec(wut, (tn, D), lambda i, j: (j, 0))
    ospec = pl.BlockSpec((tm, tn), lambda i, j: (i, j))
    return pl.pallas_call(
        body, grid=(T // tm, Fh // tn),
        in_specs=[pl.BlockSpec((tm, D), lambda i, j: (i, 0)), gspec, uspec],
        out_specs=[ospec, ospec, ospec],
        out_shape=[_sds((T, Fh), CDT), _sds((T, Fh), CDT), _sds((T, Fh), CDT)],
        name=name, compiler_params=_cp(("parallel", "arbitrary")))(h, wgt, wut)


def _ffn_bwd_ds(gout, wd, a, b, name, dep=None):
    T, D = gout.shape
    Fh = _wshape(wd)[0]
    tm = _tile(T, 2048)
    tn = _tile(Fh, 256)

    def body(g_ref, wd_ref, a_ref, b_ref, *rest):
        da_ref, db_ref = rest[-2:]
        dy = (0.5 * g_ref[...]).astype(CDT)
        ds = lax.dot_general(dy, wd_ref[...], (((1,), (1,)), ((), ())), preferred_element_type=F32)
        av = a_ref[...].astype(F32)
        sg = _sigmoid(av)
        da_ref[...] = (ds * b_ref[...].astype(F32) * (sg * (1.0 + av * (1.0 - sg)))).astype(da_ref.dtype)
        db_ref[...] = (ds * (av * sg)).astype(db_ref.dtype)

    ospec = pl.BlockSpec((tm, tn), lambda i, j: (i, j))
    dspec, darg = _dep(dep)
    wd, wspec = _wspec(wd, (tn, D), lambda i, j: (j, 0))
    return pl.pallas_call(
        body, grid=(T // tm, Fh // tn),
        in_specs=[pl.BlockSpec((tm, D), lambda i, j: (i, 0)), wspec, ospec, ospec] + dspec,
        out_specs=[ospec, ospec],
        out_shape=[_sds((T, Fh), CDT), _sds((T, Fh), CDT)],
        name=name, compiler_params=_cp(("parallel", "arbitrary")))(gout, wd, a, b, *darg)


def _ffn_fwd(x, gamma, wgt, wut, wd, tag, dep=None):
    h = _rms_fwd(x, gamma, f"{tag}_norm", dep)
    a, b, s = _ffn_up(h, wgt, wut, f"{tag}_up")
    y = _mm([(s, wd, False)], res=x, alpha=0.5, name=f"{tag}_down")
    return y, (x, h, a, b, s)


def _ffn_bwd(saved, gamma, wgt, wut, wd, gout, tag, dep, on_grads):
    x, h, a, b, s = saved
    dwd = _mm_tn(s, gout, alpha=0.5, name=f"{tag}_dwd", dep=dep)
    da, db = _ffn_bwd_ds(gout, wd, a, b, f"{tag}_bwd_ds", on_grads(dict(w_down=dwd)))
    dwgt = _mm_tn(da, h, name=f"{tag}_dwg")
    dwut = _mm_tn(db, h, name=f"{tag}_dwu")
    dep = on_grads(dict(w_gate=dwgt, w_up=dwut))
    return _mm_norm_bwd([(da, wgt, False), (db, wut, False)], x, gamma, gout, name=f"{tag}_dh_norm_bwd", dep=dep)


def _fgate_fwd(rest, bias, name, bt=512):
    T = rest.shape[0]
    bt = _tile(T, bt)

    def body(z_ref, b_ref, fc_ref, ft_ref, carry):
        i = pl.program_id(0)

        @pl.when(i == 0)
        def _():
            carry[...] = jnp.zeros_like(carry)

        zb = z_ref[...] + b_ref[...]
        e = jnp.exp(-jnp.abs(zb))
        u = 1.0 + e
        log1p_e = jnp.where(u == 1.0, e, jnp.log(u) * (e / (u - 1.0)))
        x = jnp.minimum(zb, 0.0) - log1p_e
        row = lax.broadcasted_iota(jnp.int32, x.shape, 0)
        sh = 1
        while sh < bt:
            x = x + jnp.where(row >= sh, pltpu.roll(x, sh, 0), 0.0)
            sh *= 2
        f = x + carry[...]
        carry[...] = f[bt - 1:bt, :]
        fc_ref[...] = f
        ft_ref[...] = jnp.transpose(f)[0:N_HEADS, :]

    return pl.pallas_call(
        body, grid=(T // bt,),
        in_specs=[pl.BlockSpec((bt, LANES), lambda i: (i, REST_Z_BLK)), pl.BlockSpec((1, LANES), lambda i: (0, 0))],
        out_specs=[pl.BlockSpec((bt, LANES), lambda i: (i, 0)), pl.BlockSpec((N_HEADS, bt), lambda i: (0, i))],
        out_shape=[_sds((T, LANES), F32), _sds((N_HEADS, T), F32)],
        scratch_shapes=[pltpu.VMEM((1, LANES), F32)],
        name=name, compiler_params=_cp(("arbitrary",)))(rest, bias)


def _fgate_bwd(dfk, rest, bias, name, bt=512):
    T = rest.shape[0]
    bt = _tile(T, bt)
    nb = T // bt

    def body(df_ref, z_ref, b_ref, dz_ref, db_ref, carry):
        i = pl.program_id(0)

        @pl.when(i == 0)
        def _():
            carry[...] = jnp.zeros_like(carry)

        dfv = df_ref[...]
        lane = lax.broadcasted_iota(jnp.int32, (bt, LANES), 1)
        x = jnp.zeros((bt, LANES), F32)
        for h in range(N_HEADS):
            x = jnp.where(lane == h, dfv[:, HEAD_DIM * h:HEAD_DIM * h + 1], x)
        row = lax.broadcasted_iota(jnp.int32, x.shape, 0)
        sh = 1
        while sh < bt:
            x = x + jnp.where(row + sh < bt, pltpu.roll(x, bt - sh, 0), 0.0)
            sh *= 2
        dlf = x + carry[...]
        carry[...] = dlf[0:1, :]
        zb = z_ref[...] + b_ref[...]
        dz = jnp.where(lane < N_HEADS, dlf * _sigmoid(-zb), 0.0)
        dz_ref[...] = dz.astype(dz_ref.dtype)
        part = jnp.sum(dz, axis=0, keepdims=True)

        @pl.when(i == 0)
        def _():
            db_ref[...] = part

        @pl.when(i > 0)
        def _():
            db_ref[...] += part

    return pl.pallas_call(
        body, grid=(nb,),
        in_specs=[pl.BlockSpec((bt, 4 * LANES), lambda i: (nb - 1 - i, 0)),
                  pl.BlockSpec((bt, LANES), lambda i: (nb - 1 - i, REST_Z_BLK)),
                  pl.BlockSpec((1, LANES), lambda i: (0, 0))],
        out_specs=[pl.BlockSpec((bt, LANES), lambda i: (nb - 1 - i, 0)), pl.BlockSpec((1, LANES), lambda i: (0, 0))],
        out_shape=[_sds((T, LANES), CDT), _sds((1, LANES), F32)],
        scratch_shapes=[pltpu.VMEM((1, LANES), F32)],
        name=name, compiler_params=_cp(("arbitrary",)))(dfk, rest, bias)


def _by_group(vals, lane):
    out = vals[-1]
    for g in range(len(vals) - 2, -1, -1):
        out = jnp.where(lane // 64 == g, vals[g], out)
    return out


def _pool_counts(t0, n, lane):
    t = t0 + lax.broadcasted_iota(jnp.int32, (n, 256), 0)
    return _by_group([jnp.minimum(t + 1, w) for w in POOL_WINDOWS], lane).astype(F32)


def _pooled(u, halo, i, bt):
    lane = lax.broadcasted_iota(jnp.int32, (bt, 256), 1)
    ext = jnp.concatenate([jnp.where(i > 0, halo, 0.0), u], axis=0)
    sums, s, sh = [], ext, 1
    for _ in POOL_WINDOWS:
        s = s + pltpu.roll(s, sh, 0)
        sums.append(s[POOL_HALO:, :])
        sh *= 2
    return _by_group(sums, lane) / _pool_counts(i * bt, bt, lane) - u


def _pool_fwd(rest, wbd, scale, name, bt=512):
    T = rest.shape[0]
    bt = _tile(T, bt)
    hb = bt // POOL_HALO

    def body(u_ref, halo_ref, w_ref, sc_ref, o_ref):
        i = pl.program_id(0)
        pooled = _pooled(u_ref[...], halo_ref[...], i, bt)
        mixed = jnp.dot(pooled.astype(CDT), w_ref[...], preferred_element_type=F32)
        o_ref[...] = (mixed * sc_ref[...]).astype(o_ref.dtype)

    return pl.pallas_call(
        body, grid=(T // bt,),
        in_specs=[pl.BlockSpec((bt, 256), lambda i: (i, 0)),
                  pl.BlockSpec((POOL_HALO, 256), lambda i: (jnp.maximum(i * hb - 1, 0), 0)),
                  pl.BlockSpec((256, 256), lambda i: (0, 0)), pl.BlockSpec((1, 256), lambda i: (0, 0))],
        out_specs=pl.BlockSpec((bt, 256), lambda i: (i, 0)),
        out_shape=_sds((T, 256), CDT), name=name, compiler_params=_cp(("parallel",)))(rest, rest, wbd, scale)


def _pool_bwd(dcat, rest, wbd, scale, name, bt=512):
    T = rest.shape[0]
    bt = _tile(T, bt)
    hb = bt // POOL_HALO
    nb = T // bt
    n = bt + POOL_HALO

    def body(dy_ref, dyn_ref, u_ref, halo_ref, w_ref, sc_ref, du_ref, dw_ref, dsc_ref):
        i = pl.program_id(0)
        lane = lax.broadcasted_iota(jnp.int32, (bt, 256), 1)
        w = w_ref[...]
        sc = sc_ref[...]
        pooled = _pooled(u_ref[...], halo_ref[...], i, bt)
        pooled_c = pooled.astype(CDT)
        mixed = jnp.dot(pooled_c, w, preferred_element_type=F32)
        dy = dy_ref[...]
        dm = (dy * sc).astype(CDT)
        dsc = jnp.sum(dy * mixed, axis=0, keepdims=True)
        dw = lax.dot_general(pooled_c, dm, (((0,), (0,)), ((), ())), preferred_element_type=F32)
        nt = (((1,), (1,)), ((), ()))
        dpl = lax.dot_general(dm, w, nt, preferred_element_type=F32)
        dmn = (jnp.where(i < nb - 1, dyn_ref[...], 0.0) * sc).astype(CDT)
        dpln = lax.dot_general(dmn, w, nt, preferred_element_type=F32)
        lane_h = lax.broadcasted_iota(jnp.int32, (POOL_HALO, 256), 1)
        ext = jnp.concatenate([dpl / _pool_counts(i * bt, bt, lane),
                               dpln / _pool_counts((i + 1) * bt, POOL_HALO, lane_h)], axis=0)
        sums, s, sh = [], ext, 1
        for _ in POOL_WINDOWS:
            s = s + pltpu.roll(s, n - sh, 0)
            sums.append(s[0:bt, :])
            sh *= 2
        du_ref[...] = (_by_group(sums, lane) - dpl).astype(du_ref.dtype)

        @pl.when(i == 0)
        def _():
            dw_ref[...] = dw
            dsc_ref[...] = dsc

        @pl.when(i > 0)
        def _():
            dw_ref[...] += dw
            dsc_ref[...] += dsc

    full = pl.BlockSpec((256, 256), lambda i: (0, 0))
    vec = pl.BlockSpec((1, 256), lambda i: (0, 0))
    return pl.pallas_call(
        body, grid=(nb,),
        in_specs=[pl.BlockSpec((bt, 256), lambda i: (i, 0)),
                  pl.BlockSpec((POOL_HALO, 256), lambda i: (jnp.minimum((i + 1) * hb, nb * hb - 1), 0)),
                  pl.BlockSpec((bt, 256), lambda i: (i, 0)),
                  pl.BlockSpec((POOL_HALO, 256), lambda i: (jnp.maximum(i * hb - 1, 0), 0)),
                  full, vec],
        out_specs=[pl.BlockSpec((bt, 256), lambda i: (i, 0)), full, vec],
        out_shape=[_sds((T, 256), CDT), _sds((256, 256), F32), _sds((1, 256), F32)],
        name=name, compiler_params=_cp(("arbitrary",)))(dcat, dcat, rest, rest, wbd, scale)


def _glu_ext(a_ref, g_ref, ah_ref, gh_ref, i):
    u = a_ref[...] * _sigmoid(g_ref[...])
    uh = jnp.where(i > 0, ah_ref[...] * _sigmoid(gh_ref[...]), 0.0)
    return jnp.concatenate([uh, u], axis=0)


def _conv_fwd(rest, cw, cb, lg, lb, name, bt=512):
    T = rest.shape[0]
    bt = _tile(T, bt)
    hb = bt // CONV_HALO

    def body(a_ref, g_ref, ah_ref, gh_ref, cw_ref, cb_ref, lg_ref, lb_ref, o_ref, y_ref):
        i = pl.program_id(0)
        ext = _glu_ext(a_ref, g_ref, ah_ref, gh_ref, i)
        w = cw_ref[...]
        acc = w[CONV_K - 1:CONV_K, :] * ext
        for k in range(CONV_K - 1):
            acc = acc + w[k:k + 1, :] * pltpu.roll(ext, CONV_K - 1 - k, 0)
        y = acc[CONV_HALO:, :] + cb_ref[...]
        y_ref[...] = y
        yc = y - jnp.mean(y, axis=-1, keepdims=True)
        yn = yc * lax.rsqrt(jnp.mean(yc * yc, axis=-1, keepdims=True) + NORM_EPS)
        z = yn * lg_ref[...] + lb_ref[...]
        o_ref[...] = (z * _sigmoid(z)).astype(o_ref.dtype)

    def cur(c):
        return pl.BlockSpec((bt, 256), lambda i: (i, c))

    def prev(c):
        return pl.BlockSpec((CONV_HALO, 256), lambda i: (jnp.maximum(i * hb - 1, 0), c))

    vec = pl.BlockSpec((1, 256), lambda i: (0, 0))
    return pl.pallas_call(
        body, grid=(T // bt,),
        in_specs=[cur(1), cur(2), prev(1), prev(2), pl.BlockSpec((CONV_HALO, 256), lambda i: (0, 0)), vec, vec, vec],
        out_specs=[pl.BlockSpec((bt, 256), lambda i: (i, 0)), pl.BlockSpec((bt, 256), lambda i: (i, 0))],
        out_shape=[_sds((T, 256), CDT), _sds((T, 256), F32)],
        name=name, compiler_params=_cp(("parallel",)))(rest, rest, rest, rest, cw, cb, lg, lb)


def _conv_bwd(dcat, yconv, rest, cw, lg, lb, name, bt=512):
    T = rest.shape[0]
    bt = _tile(T, bt)
    hb = bt // CONV_HALO
    nb = T // bt
    n = bt + CONV_HALO

    def body(dy_ref, dyn_ref, y_ref, yn_ref, a_ref, g_ref, ah_ref, gh_ref, cw_ref, lg_ref, lb_ref,
             da_ref, dg_ref, dcw_ref, dcb_ref, dlg_ref, dlb_ref):
        i = pl.program_id(0)
        lgv = lg_ref[...]
        lbv = lb_ref[...]

        def ln_swish_bwd(dout, y):
            yc = y - jnp.mean(y, axis=-1, keepdims=True)
            rs = lax.rsqrt(jnp.mean(yc * yc, axis=-1, keepdims=True) + NORM_EPS)
            yn = yc * rs
            z = yn * lgv + lbv
            sg = _sigmoid(z)
            dz = dout * (sg * (1.0 + z * (1.0 - sg)))
            dyn = dz * lgv
            dyc = rs * (dyn - jnp.mean(dyn, axis=-1, keepdims=True) - yn * jnp.mean(dyn * yn, axis=-1, keepdims=True))
            return dyc, dz, yn

        dyc, dz, yn = ln_swish_bwd(dy_ref[...], y_ref[...])
        dyc_next, _, _ = ln_swish_bwd(dyn_ref[...], yn_ref[...])
        dyc_next = jnp.where(i < nb - 1, dyc_next, 0.0)
        ext_u = _glu_ext(a_ref, g_ref, ah_ref, gh_ref, i)
        ext_d = jnp.concatenate([dyc, dyc_next], axis=0)
        w = cw_ref[...]
        du = w[CONV_K - 1:CONV_K, :] * ext_d
        rows = []
        for k in range(CONV_K):
            s = CONV_K - 1 - k
            if s > 0:
                du = du + w[k:k + 1, :] * pltpu.roll(ext_d, n - s, 0)
                us = pltpu.roll(ext_u, s, 0)[CONV_HALO:, :]
            else:
                us = ext_u[CONV_HALO:, :]
            rows.append(jnp.sum(dyc * us, axis=0, keepdims=True))
        rows.append(jnp.zeros((1, 256), F32))
        dcw = jnp.concatenate(rows, axis=0)
        du = du[0:bt, :]
        av = a_ref[...]
        sg = _sigmoid(g_ref[...])
        da_ref[...] = (du * sg).astype(da_ref.dtype)
        dg_ref[...] = (du * av * (sg * (1.0 - sg))).astype(dg_ref.dtype)
        dcb = jnp.sum(dyc, axis=0, keepdims=True)
        dlg = jnp.sum(dz * yn, axis=0, keepdims=True)
        dlb = jnp.sum(dz, axis=0, keepdims=True)

        @pl.when(i == 0)
        def _():
            dcw_ref[...] = dcw
            dcb_ref[...] = dcb
            dlg_ref[...] = dlg
            dlb_ref[...] = dlb

        @pl.when(i > 0)
        def _():
            dcw_ref[...] += dcw
            dcb_ref[...] += dcb
            dlg_ref[...] += dlg
            dlb_ref[...] += dlb

    def cur(c):
        return pl.BlockSpec((bt, 256), lambda i: (i, c))

    def prev(c):
        return pl.BlockSpec((CONV_HALO, 256), lambda i: (jnp.maximum(i * hb - 1, 0), c))

    def nxt(c):
        return pl.BlockSpec((CONV_HALO, 256), lambda i: (jnp.minimum((i + 1) * hb, nb * hb - 1), c))

    vec = pl.BlockSpec((1, 256), lambda i: (0, 0))
    wfull = pl.BlockSpec((CONV_HALO, 256), lambda i: (0, 0))
    return pl.pallas_call(
        body, grid=(nb,),
        in_specs=[cur(3), nxt(3), cur(0), nxt(0), cur(1), cur(2), prev(1), prev(2), wfull, vec, vec],
        out_specs=[cur(0), cur(0), wfull, vec, vec, vec],
        out_shape=[_sds((T, 256), CDT), _sds((T, 256), CDT), _sds((CONV_HALO, 256), F32),
                   _sds((1, 256), F32), _sds((1, 256), F32), _sds((1, 256), F32)],
        name=name, compiler_params=_cp(("arbitrary",)))(dcat, dcat, yconv, yconv, rest, rest, rest, rest, cw, lg, lb)


def _half_mask(shape, a):
    lane = lax.broadcasted_iota(jnp.int32, shape, 1)
    return (lane // HEAD_DIM) == a


def _attn_fwd(qkv, fcol, frow, name, blk=1024):
    T = qkv.shape[0]
    blk = _tile(T, blk)
    nq = T // blk
    nt = (((1,), (1,)), ((), ()))

    def body(q_ref, k_ref, v_ref, fc_ref, fr_ref, o_ref, lse_ref):
        p_id = pl.program_id(0)
        i = pl.program_id(1)
        q2 = q_ref[...]
        fc = fc_ref[...]
        lane = lax.broadcasted_iota(jnp.int32, (blk, LANES), 1)
        tri = lax.broadcasted_iota(jnp.int32, (blk, blk), 1) <= lax.broadcasted_iota(jnp.int32, (blk, blk), 0)
        masks = [_half_mask(q2.shape, a) for a in range(2)]
        qs = [jnp.where(hm, q2, jnp.zeros_like(q2)) * ATT_SCALE for hm in masks]
        fqs = [jnp.sum(jnp.where(lane == 2 * p_id + a, fc, 0.0), axis=1, keepdims=True) for a in range(2)]

        def tile(j, carry, masked):
            cols = pl.ds(pl.multiple_of(j * blk, blk), blk)
            kj = k_ref[cols, :]
            vj = v_ref[cols, :]
            out = []
            for a in range(2):
                m, acc = carry[2 * a:2 * a + 2]
                va = jnp.where(masks[a], vj, jnp.ones_like(vj))
                s = lax.dot_general(qs[a], kj, nt, preferred_element_type=F32) + (fqs[a] - fr_ref[a:a + 1, cols])
                if masked:
                    s = jnp.where(tri, s, NEG)
                m_new = jnp.maximum(m, jnp.max(s, axis=1, keepdims=True))
                alpha = jnp.exp(m - m_new)
                pr = jnp.exp(s - m_new)
                hi = lax.bitcast_convert_type(lax.bitcast_convert_type(pr, jnp.uint32) & jnp.uint32(0xFFFF0000), F32)
                pv = (jnp.dot(hi.astype(CDT), va, preferred_element_type=F32)
                      + jnp.dot((pr - hi).astype(CDT), va, preferred_element_type=F32))
                out += [m_new, alpha * acc + pv]
            return tuple(out)

        init = (jnp.full((blk, 1), NEG, F32), jnp.zeros((blk, LANES), F32)) * 2
        carry = lax.fori_loop(0, i, lambda j, c: tile(j, c, False), init)
        carry = tile(i, carry, True)
        ls = [carry[1][:, HEAD_DIM:HEAD_DIM + 1], carry[3][:, 0:1]]
        lo = lane < HEAD_DIM
        o_ref[...] = jnp.where(lo, carry[1] / ls[0], carry[3] / ls[1])
        lse_t = jnp.transpose(jnp.where(lo, carry[0] + jnp.log(ls[0]), carry[2] + jnp.log(ls[1])))
        lse_ref[...] = jnp.concatenate([lse_t[0:1, :], lse_t[HEAD_DIM:HEAD_DIM + 1, :]], axis=0)

    return pl.pallas_call(
        body, grid=(N_PAIRS, nq),
        in_specs=[pl.BlockSpec((blk, LANES), lambda p, i: (i, p)),
                  pl.BlockSpec((T, LANES), lambda p, i: (0, N_PAIRS + p)),
                  pl.BlockSpec((T, LANES), lambda p, i: (0, 2 * N_PAIRS + p)),
                  pl.BlockSpec((blk, LANES), lambda p, i: (i, 0)),
                  pl.BlockSpec((None, 2, T), lambda p, i: (p, 0, 0))],
        out_specs=[pl.BlockSpec((blk, LANES), lambda p, i: (i, p)), pl.BlockSpec((None, 2, blk), lambda p, i: (p, 0, i))],
        out_shape=[_sds((T, N_PAIRS * LANES), F32), _sds((N_PAIRS, 2, T), F32)],
        name=name, compiler_params=_cp(("parallel", "arbitrary")))(qkv, qkv, qkv, fcol, frow)


def _attn_delta(dcat, o, name, blk=512):
    T = o.shape[0]
    blk = _tile(T, blk)

    def body(d_ref, o_ref, out_ref):
        prod = d_ref[:, 256:768].astype(CDT).astype(F32) * o_ref[...]
        pt = jnp.transpose(prod)
        out_ref[...] = jnp.sum(pt.reshape(N_HEADS, HEAD_DIM, blk), axis=1)

    return pl.pallas_call(
        body, grid=(T // blk,),
        in_specs=[pl.BlockSpec((blk, 1024), lambda i: (i, 0)), pl.BlockSpec((blk, 512), lambda i: (i, 0))],
        out_specs=pl.BlockSpec((N_HEADS, blk), lambda i: (0, i)),
        out_shape=_sds((N_HEADS, T), F32), name=name, compiler_params=_cp(("parallel",)))(dcat, o)


def _attn_bwd(qkv, dcat, fcol, frow, lse, delta, name, blk=512):
    T = qkv.shape[0]
    blk = _tile(T, blk)
    nq = T // blk
    nt = (((1,), (1,)), ((), ()))

    def body(q_ref, do_ref, k_ref, v_ref, fc_ref, fr_ref, lse_ref, dl_ref, dqt_ref, dk_ref, dv_ref, df_ref):
        p_id = pl.program_id(0)
        j = pl.program_id(1)

        @pl.when(j == 0)
        def _():
            dqt_ref[...] = jnp.zeros_like(dqt_ref)

        k2 = k_ref[...]
        v2 = v_ref[...]
        fc = fc_ref[...]
        lane = lax.broadcasted_iota(jnp.int32, (blk, LANES), 1)
        tri = lax.broadcasted_iota(jnp.int32, (blk, blk), 0) <= lax.broadcasted_iota(jnp.int32, (blk, blk), 1)
        masks = [_half_mask(k2.shape, a) for a in range(2)]
        kas = [jnp.where(hm, k2, jnp.zeros_like(k2)) * ATT_SCALE for hm in masks]
        kats = [jnp.transpose(ka) for ka in kas]
        vas = [jnp.where(hm, v2, jnp.zeros_like(v2)) for hm in masks]
        fks = [jnp.sum(jnp.where(lane == 2 * p_id + a, fc, 0.0), axis=1, keepdims=True) for a in range(2)]

        def tile(i, carry, masked):
            rows = pl.ds(pl.multiple_of(i * blk, blk), blk)
            qi = q_ref[rows, :]
            doi = do_ref[rows, :].astype(CDT)
            out = []
            dqt = None
            for a in range(2):
                dk_acc, dv_acc, df_acc = carry[3 * a:3 * a + 3]
                st = lax.dot_general(kas[a], qi, nt, preferred_element_type=F32)
                e = (st + (fr_ref[a:a + 1, rows] - fks[a])) - lse_ref[a:a + 1, rows]
                if masked:
                    e = jnp.where(tri, e, NEG)
                pt = jnp.exp(e)
                dpt = lax.dot_general(vas[a], doi, nt, preferred_element_type=F32)
                ds32 = pt * (dpt - dl_ref[a:a + 1, rows])
                dst = ds32.astype(CDT)
                df_acc = df_acc + jnp.sum(ds32, axis=1, keepdims=True)
                dv_acc = dv_acc + jnp.dot(pt.astype(CDT), doi, preferred_element_type=F32)
                dk_acc = dk_acc + jnp.dot(dst, qi, preferred_element_type=F32)
                part = jnp.dot(kats[a], dst, preferred_element_type=F32)
                dqt = part if dqt is None else dqt + part
                out += [dk_acc, dv_acc, df_acc]
            dqt_ref[:, rows] += dqt
            return tuple(out)

        init = (jnp.zeros((blk, LANES), F32), jnp.zeros((blk, LANES), F32), jnp.zeros((blk, 1), F32)) * 2
        carry = tile(j, init, True)
        carry = lax.fori_loop(j + 1, nq, lambda i, c: tile(i, c, False), carry)
        lo = lane < HEAD_DIM
        dk_ref[...] = (jnp.where(lo, carry[0], carry[3]) * ATT_SCALE).astype(dk_ref.dtype)
        dv_ref[...] = jnp.where(lo, carry[1], carry[4]).astype(dv_ref.dtype)
        df_ref[...] = -jnp.where(lo, carry[2], carry[5])

    res = pl.BlockSpec((T, LANES), lambda p, j: (0, p))
    rows = pl.BlockSpec((None, 2, T), lambda p, j: (p, 0, 0))
    kv_out = pl.BlockSpec((blk, LANES), lambda p, j: (j, p))
    return pl.pallas_call(
        body, grid=(N_PAIRS, nq),
        in_specs=[res, pl.BlockSpec((T, LANES), lambda p, j: (0, 2 + p)),
                  pl.BlockSpec((blk, LANES), lambda p, j: (j, N_PAIRS + p)),
                  pl.BlockSpec((blk, LANES), lambda p, j: (j, 2 * N_PAIRS + p)),
                  pl.BlockSpec((blk, LANES), lambda p, j: (j, 0)), rows, rows, rows],
        out_specs=[pl.BlockSpec((LANES, T), lambda p, j: (p, 0)), kv_out, kv_out, kv_out],
        out_shape=[_sds((N_PAIRS * LANES, T), F32), _sds((T, N_PAIRS * LANES), CDT), _sds((T, N_PAIRS * LANES), CDT),
                   _sds((T, N_PAIRS * LANES), F32)],
        name=name, compiler_params=_cp(("parallel", "arbitrary")))(qkv, dcat, qkv, qkv, fcol, frow, lse, delta)


def _mixer_fwd(x, wts, tag, dep=None):
    T = x.shape[0]
    h = _rms_fwd(x, wts["mix_norm"], f"{tag}_norm", dep)
    qkv = _mm([(h, wts["win_qkv"], False)], out_dtype=CDT, tm=1024, tn=768, name=f"{tag}_in_qkv")
    rest = _mm([(h, wts["win_rest"], False)], tm=1024, name=f"{tag}_in_rest")
    fcol, frow8 = _fgate_fwd(rest, wts["fbias"], f"{tag}_fgate")
    frow = frow8.reshape(N_PAIRS, 2, T)
    ya = _pool_fwd(rest, wts["pool_wbd"], wts["pool_scale"], f"{tag}_pool")
    o, lse = _attn_fwd(qkv, fcol, frow, f"{tag}_attn")
    yc, yconv = _conv_fwd(rest, wts["conv_w"], wts["conv_b"], wts["conv_ln_g"], wts["conv_ln_b"], f"{tag}_conv")
    cat = jnp.concatenate([ya, o.astype(CDT), yc], axis=1)
    y = _mm([(cat, wts["w_out"], False)], res=x, name=f"{tag}_out")
    return y, (x, h, qkv, rest, fcol, frow, o, lse, yconv, cat)


def _mixer_bwd(saved, wts, gout, tag, dep=None):
    x, h, qkv, rest, fcol, frow, o, lse, yconv, cat = saved
    T = x.shape[0]
    dcat = _mm([(gout, wts["w_out"], True)], name=f"{tag}_dcat", dep=dep)
    dwout = _mm_tn(cat, gout, name=f"{tag}_dwout")
    du, dpw, dpsc = _pool_bwd(dcat, rest, wts["pool_wbd"], wts["pool_scale"], f"{tag}_pool_bwd")
    delta = _attn_delta(dcat, o, f"{tag}_attn_delta").reshape(N_PAIRS, 2, T)
    dqt, dk, dv, dfk = _attn_bwd(qkv, dcat, fcol, frow, lse, delta, f"{tag}_attn_bwd")
    dq = dqt.T.astype(CDT)
    dz, dfb = _fgate_bwd(dfk, rest, wts["fbias"], f"{tag}_fgate_bwd")
    da, dg, dcw, dcb, dlg, dlb = _conv_bwd(dcat, yconv, rest, wts["conv_w"], wts["conv_ln_g"], wts["conv_ln_b"],
                                           f"{tag}_conv_bwd")
    dp_qkv = jnp.concatenate([dq, dk, dv], axis=1).astype(CDT)
    dp_rest = jnp.concatenate([du, da, dg, dz], axis=1)
    dwin_qkv = _mm_tn(h, dp_qkv, name=f"{tag}_dwin_qkv")
    dwin_rest = _mm_tn(h, dp_rest, name=f"{tag}_dwin_rest")
    gin, dgamma = _mm_norm_bwd([(dp_qkv, wts["win_qkv"], True), (dp_rest, wts["win_rest"], True)], x, wts["mix_norm"],
                               gout, name=f"{tag}_dh_norm_bwd")
    dwin = _split_win(dwin_qkv, dwin_rest, f"{tag}_dwin_split")
    dpool_w = jnp.stack([dpw[64 * g:64 * g + 64, 64 * g:64 * g + 64] for g in range(4)])
    grads = dict(mix_norm=dgamma[0], w_in=dwin, pool_w=dpool_w, pool_scale=dpsc[0], forget_bias=dfb[0, 0:N_HEADS],
                 conv_w=dcw[0:CONV_K], conv_b=dcb[0], conv_ln_g=dlg[0], conv_ln_b=dlb[0], w_out=dwout)
    return gin, grads


def _rep_layer(rep, l):
    pw = rep["pool_w"][l].astype(CDT)
    wbd = jnp.zeros((256, 256), CDT)
    for g in range(4):
        wbd = lax.dynamic_update_slice(wbd, pw[g], (64 * g, 64 * g))
    return dict(
        ffn1_norm=rep["ffn1_norm"][l][None], ffn2_norm=rep["ffn2_norm"][l][None], mix_norm=rep["mix_norm"][l][None],
        fbias=jnp.pad(rep["forget_bias"][l], (0, LANES - N_HEADS))[None],
        pool_wbd=wbd, pool_scale=rep["pool_scale"][l][None], conv_b=rep["conv_b"][l][None],
        conv_ln_g=rep["conv_ln_g"][l][None], conv_ln_b=rep["conv_ln_b"][l][None])


def _local_step(x, target, rep, weights_for, grads_ready):
    depth = rep["ffn1_norm"].shape[0]
    kept = []
    for l in range(depth):
        r = _rep_layer(rep, l)
        w1, dep = weights_for(l, "ffn1", x)
        x, s1 = _ffn_fwd(x, r["ffn1_norm"], w1["w_gate"], w1["w_up"], w1["w_down"], f"l{l}_ffn1", dep)
        wm, dep = weights_for(l, "mix", x)
        wm = dict(r, win_qkv=wm["win_qkv"], win_rest=wm["win_rest"], w_out=wm["w_out"],
                  conv_w=jnp.pad(wm["conv_w"], ((0, CONV_HALO - CONV_K), (0, 0))))
        x, s2 = _mixer_fwd(x, wm, f"l{l}_mix", dep)
        w2, dep = weights_for(l, "ffn2", x)
        x, s3 = _ffn_fwd(x, r["ffn2_norm"], w2["w_gate"], w2["w_up"], w2["w_down"], f"l{l}_ffn2", dep)
        kept.append((r, w1, wm, w2, s1, s2, s3))
    loss, g, dfinal = _loss_bwd(x, rep["final_norm"][None], target, "loss_head")
    dep = grads_ready(None, "final", dict(final_norm=dfinal[0]))
    for l in reversed(range(depth)):
        r, w1, wm, w2, s1, s2, s3 = kept[l]

        def ffn_grads(which, l=l):
            return lambda gr: grads_ready(l, which, {f"{which}_{k}": v for k, v in gr.items()})

        g, dn = _ffn_bwd(s3, r["ffn2_norm"], w2["w_gate"], w2["w_up"], w2["w_down"], g, f"l{l}_ffn2", dep, ffn_grads("ffn2"))
        grads_ready(l, "norm", dict(ffn2_norm=dn[0]))
        g, gm = _mixer_bwd(s2, wm, g, f"l{l}_mix")
        dep = grads_ready(l, "mix", gm)
        g, dn = _ffn_bwd(s1, r["ffn1_norm"], w1["w_gate"], w1["w_up"], w1["w_down"], g, f"l{l}_ffn1", dep, ffn_grads("ffn1"))
        dep = grads_ready(l, "norm", dict(ffn1_norm=dn[0]))
    return loss, g


def _mesh_pos():
    return lax.axis_index("x"), lax.axis_index("y"), lax.axis_index("c")


def _dev_block(ref, dev, by_rows):
    if by_rows:
        r = ref.shape[1] // N_DEV
        return ref.at[:, pl.ds(dev * r, r), :]
    return ref.at[dev]


def _all_gather(shards, by_rows, name):
    n_arr = len(shards)
    out_shape = [_sds((s.shape[0], N_DEV * s.shape[1], s.shape[2]) if br else (N_DEV,) + s.shape, s.dtype)
                 for s, br in zip(shards, by_rows)]

    def body(*refs):
        xs, outs = refs[:n_arr], refs[n_arr:2 * n_arr]
        send_sems, recv_sems, local_sems = refs[2 * n_arr:]
        x, y, c = _mesh_pos()
        me, sibling = (x, y, c), (x, y, 1 - c)
        chips = [(1 - x, y), (x, 1 - y), (1 - x, 1 - y)]

        def rows(a, px, py, pc):
            return _dev_block(outs[a], 4 * px + 2 * py + pc, by_rows[a])

        def copy(k, a, block, to, src=None):
            return pltpu.make_async_remote_copy(
                src_ref=rows(a, *block) if src is None else src, dst_ref=rows(a, *block),
                send_sem=send_sems.at[k, a], recv_sem=recv_sems.at[k, a],
                device_id=to, device_id_type=pl.DeviceIdType.MESH)

        arrs = range(n_arr)
        mine = [pltpu.make_async_copy(xs[a], rows(a, *me), local_sems.at[a]) for a in arrs]
        for cp in mine:
            cp.start()
        first = [copy(0, a, me, sibling, src=xs[a]) for a in arrs]
        first += [copy(1 + j, a, me, (*chip, c), src=xs[a]) for j, chip in enumerate(chips) for a in arrs]
        for cp in first:
            cp.start()
        passed = []
        for j, chip in enumerate(chips):
            for a in arrs:
                copy(1 + j, a, (*chip, c), me).wait_recv()
                passed.append(copy(4 + j, a, (*chip, c), sibling))
                passed[-1].start()
        for a in arrs:
            copy(0, a, sibling, me).wait_recv()
        for j, chip in enumerate(chips):
            for a in arrs:
                copy(4 + j, a, (*chip, 1 - c), me).wait_recv()
        for cp in first + passed:
            cp.wait_send()
        for cp in mine:
            cp.wait()

    hbm = pl.BlockSpec(memory_space=pl.ANY)
    return pl.pallas_call(
        body, out_shape=out_shape, in_specs=[hbm] * n_arr, out_specs=[hbm] * n_arr,
        scratch_shapes=[pltpu.SemaphoreType.DMA((7, n_arr)), pltpu.SemaphoreType.DMA((7, n_arr)),
                        pltpu.SemaphoreType.DMA((n_arr,))],
        name=name)(*shards)


def _exchange(parts, by_rows, name):
    n_arr = len(parts)
    out_shape = [_sds((N_DEV, p.shape[0], p.shape[1] // N_DEV, p.shape[2]) if br else p.shape, p.dtype)
                 for p, br in zip(parts, by_rows)]

    def body(*refs):
        ps, outs = refs[:n_arr], refs[n_arr:2 * n_arr]
        send_sems, recv_sems, local_sems = refs[2 * n_arr:]
        x, y, c = _mesh_pos()
        my = 4 * x + 2 * y + c
        arrs = range(n_arr)
        mine = [pltpu.make_async_copy(_dev_block(ps[a], my, by_rows[a]), outs[a].at[my], local_sems.at[a]) for a in arrs]
        for cp in mine:
            cp.start()
        copies = []
        for k in range(1, N_DEV):
            px, py, pc = x ^ (k >> 2), y ^ ((k >> 1) & 1), c ^ (k & 1)
            for a in arrs:
                copies.append(pltpu.make_async_remote_copy(
                    src_ref=_dev_block(ps[a], 4 * px + 2 * py + pc, by_rows[a]), dst_ref=outs[a].at[my],
                    send_sem=send_sems.at[k - 1, a], recv_sem=recv_sems.at[k - 1, a],
                    device_id=(px, py, pc), device_id_type=pl.DeviceIdType.MESH))
        for cp in copies:
            cp.start()
        for cp in copies:
            cp.wait()
        for cp in mine:
            cp.wait()

    hbm = pl.BlockSpec(memory_space=pl.ANY)
    return pl.pallas_call(
        body, out_shape=out_shape, in_specs=[hbm] * n_arr, out_specs=[hbm] * n_arr,
        scratch_shapes=[pltpu.SemaphoreType.DMA((7, n_arr)), pltpu.SemaphoreType.DMA((7, n_arr)),
                        pltpu.SemaphoreType.DMA((n_arr,))],
        name=name)(*parts)


def _peer_copies(srcs, lands, send_sems, recv_sems, gather, by_rows):
    n_arr = len(srcs)
    x, y, c = _mesh_pos()
    my = 4 * x + 2 * y + c
    out = []
    for k in range(1, N_DEV):
        px, py, pc = x ^ (k >> 2), y ^ ((k >> 1) & 1), c ^ (k & 1)
        peer = 4 * px + 2 * py + pc
        for a in range(n_arr):
            src = srcs[a] if gather else _dev_block(srcs[a], peer, by_rows[a])
            dst = _dev_block(lands[a], my, by_rows[a]) if gather else lands[a].at[my]
            out.append(pltpu.make_async_remote_copy(
                src_ref=src, dst_ref=dst, send_sem=send_sems.at[(k - 1) * n_arr + a],
                recv_sem=recv_sems.at[(k - 1) * n_arr + a], device_id=(px, py, pc), device_id_type=pl.DeviceIdType.MESH))
    return out


def _land_shape(s, gather, by_rows):
    if gather:
        return (s.shape[0], N_DEV * s.shape[1], s.shape[2]) if by_rows else (N_DEV,) + s.shape
    return (N_DEV, s.shape[0], s.shape[1] // N_DEV, s.shape[2]) if by_rows else s.shape


_HBM = pl.BlockSpec(memory_space=pltpu.HBM)
_SEM = pl.BlockSpec(memory_space=pltpu.SEMAPHORE)


def _xfer_start(srcs, gather, by_rows, name, dep=None):
    n = len(srcs)
    x, y, c = _mesh_pos()
    my = 4 * x + 2 * y + c
    lands = []
    for src, br in zip(srcs, by_rows):
        land = lax.empty(_land_shape(src, gather, br), src.dtype)
        zeros = (0,) * (land.ndim - 1)
        if gather and br:
            own, at = src, (0, my * src.shape[1], 0)
        elif gather:
            own, at = src[None], (my,) + zeros
        elif br:
            r = src.shape[1] // N_DEV
            own = lax.dynamic_slice(src, (0, my * r, 0), (src.shape[0], r, src.shape[2]))[None]
            at = (my,) + zeros
        else:
            own, at = lax.dynamic_index_in_dim(src, my, 0, keepdims=True), (my,) + zeros
        lands.append(lax.dynamic_update_slice(land, own, at))
    ins = [pltpu.with_memory_space_constraint(a, pltpu.HBM) for a in list(srcs) + lands]
    dspec, darg = _dep(dep)

    def body(*refs):
        s = 2 * n + len(darg)
        for cp in _peer_copies(refs[:n], refs[n:2 * n], refs[s], refs[s + 1], gather, by_rows):
            cp.start()
        refs[-1][...] = jnp.zeros_like(refs[-1])

    sems = pltpu.SemaphoreType.DMA(((N_DEV - 1) * n,))
    outs = pl.pallas_call(
        body, name=name,
        out_shape=(sems, sems, *[pltpu.HBM(a.shape, a.dtype) for a in ins], _sds((8, LANES), F32)),
        in_specs=[_HBM] * (2 * n) + dspec,
        out_specs=(_SEM, _SEM, *[_HBM] * (2 * n), pl.BlockSpec(memory_space=pltpu.VMEM)),
        input_output_aliases={i: 2 + i for i in range(2 * n)},
        compiler_params=pltpu.CompilerParams(has_side_effects=pltpu.SideEffectType.DATAFLOW_SIDE_EFFECTING))(*ins, *darg)
    return outs[0], outs[1], list(outs[2:-1]), outs[-1]


def _xfer_wait(started, after, gather, by_rows, name):
    send_sems, recv_sems, bufs, _ = started
    n = len(bufs) // 2

    def body(*refs):
        for cp in _peer_copies(refs[:n], refs[n:2 * n], refs[2 * n], refs[2 * n + 1], gather, by_rows):
            cp.wait_send()
            cp.wait_recv()

    outs = pl.pallas_call(
        body, name=name, out_shape=tuple(pltpu.HBM(a.shape, a.dtype) for a in bufs),
        in_specs=[_HBM] * (2 * n) + [_SEM, _SEM, pl.BlockSpec(memory_space=pl.ANY)], out_specs=tuple([_HBM] * (2 * n)),
        input_output_aliases={i: i for i in range(2 * n)},
        compiler_params=pltpu.CompilerParams(has_side_effects=pltpu.SideEffectType.DATAFLOW_SIDE_EFFECTING))(
            *bufs, send_sems, recv_sems, after)
    return list(outs[n:])


def _adam_update(g, w, m, v):
    c1 = 1.0 - ADAM_B1 ** ADAM_STEP
    c2 = 1.0 - ADAM_B2 ** ADAM_STEP
    nm = ADAM_B1 * m + (1.0 - ADAM_B1) * g
    nv = ADAM_B2 * v + (1.0 - ADAM_B2) * (g * g)
    return -ADAM_LR * ((nm / c1) / (jnp.sqrt(nv / c2) + ADAM_EPS) + ADAM_WD * w), nm, nv


def _adamw_body(p_ref, w_ref, m_ref, v_ref, g_ref, d_ref, nm_ref, nv_ref):
    g = p_ref[0]
    for i in range(1, N_DEV):
        g = g + p_ref[i]
    g_ref[...] = g
    d_ref[...], nm_ref[...], nv_ref[...] = _adam_update(g, w_ref[...], m_ref[...], v_ref[...])


def _adamw(parts, w, m, v, name, tr=1536):
    R = w.shape[0]
    tr = max(t for t in range(8, tr + 1, 8) if R % t == 0)

    def body(*refs):
        _adamw_body(*refs)

    row = pl.BlockSpec((tr, LANES), lambda i: (i, 0))
    return pl.pallas_call(
        body, grid=(R // tr,),
        in_specs=[pl.BlockSpec((N_DEV, tr, LANES), lambda i: (0, i, 0)), row, row, row],
        out_specs=[row, row, row, row], out_shape=[_sds((R, LANES), F32)] * 4,
        name=name, compiler_params=_cp(("parallel",)))(parts, w, m, v)


def _adamw_split(recvs, w, m, v, name, tr):
    depth, r, c = w.shape
    assert depth == len(recvs)
    tr = _tile(r, tr)

    def body(*refs):
        layer = pl.program_id(0)
        for ll in range(depth):
            @pl.when(layer == ll)
            def _(ll=ll):
                _adamw_body(refs[ll], *refs[depth:])

    wspec = pl.BlockSpec((None, tr, c), lambda l, i: (l, i, 0))
    rspecs = [pl.BlockSpec((N_DEV, None, tr, c), lambda l, i, ll=ll, t=t: (0, t, jnp.where(l == ll, i, 0), 0))
              for ll, (_, t) in enumerate(recvs)]
    return pl.pallas_call(
        body, grid=(depth, r // tr), in_specs=rspecs + [wspec, wspec, wspec],
        out_specs=[wspec] * 4, out_shape=[_sds(w.shape, F32)] * 4,
        name=name, compiler_params=_cp(("arbitrary", "arbitrary")))(*[a for a, _ in recvs], w, m, v)


def _merge_win(g, name, tr=256):
    _, nt, K, n = g.shape
    tr = _tile(K, tr)

    def body(g_ref, q_ref, r_ref):
        full = jnp.concatenate([g_ref[j] for j in range(N_DEV)], axis=1)
        q_ref[...] = full[:, 256:1792]
        zpad = jnp.zeros((tr, REST_W - 776), full.dtype)
        r_ref[...] = jnp.concatenate([full[:, 0:256], full[:, 1800:2312], full[:, 1792:1800], zpad], axis=1)

    return pl.pallas_call(
        body, grid=(nt, K // tr),
        in_specs=[pl.BlockSpec((N_DEV, None, tr, n), lambda t, i: (0, t, i, 0))],
        out_specs=[pl.BlockSpec((None, tr, 1536), lambda t, i: (t, i, 0)), pl.BlockSpec((None, tr, REST_W), lambda t, i: (t, i, 0))],
        out_shape=[_sds((nt, K, 1536), g.dtype), _sds((nt, K, REST_W), g.dtype)],
        name=name, compiler_params=_cp(("parallel", "parallel")))(g)


def _split_win(dq, dr, name, tr=256):
    K = dq.shape[0]
    tr = _tile(K, tr)
    n = (dq.shape[1] + 776) // N_DEV

    def body(q_ref, r_ref, o_ref):
        r = r_ref[...]
        full = jnp.concatenate([r[:, 0:256], q_ref[...], r[:, 768:776], r[:, 256:768]], axis=1)
        for j in range(N_DEV):
            o_ref[j] = full[:, n * j:n * (j + 1)]

    return pl.pallas_call(
        body, grid=(K // tr,),
        in_specs=[pl.BlockSpec((tr, dq.shape[1]), lambda i: (i, 0)), pl.BlockSpec((tr, REST_W), lambda i: (i, 0))],
        out_specs=pl.BlockSpec((N_DEV, tr, n), lambda i: (0, i, 0)),
        out_shape=_sds((N_DEV, K, n), F32), name=name, compiler_params=_cp(("parallel",)))(dq, dr)


WEIGHTS = ["ffn1_norm", "ffn1_w_gate", "ffn1_w_up", "ffn1_w_down", "mix_norm", "w_in", "pool_w", "pool_scale",
           "forget_bias", "conv_w", "conv_b", "conv_ln_g", "conv_ln_b", "w_out", "ffn2_norm", "ffn2_w_gate",
           "ffn2_w_up", "ffn2_w_down", "final_norm"]
FFN_PARTS = ("w_gate", "w_up", "w_down")
FFN_T = ["ffn1_w_gate", "ffn1_w_up", "ffn2_w_gate", "ffn2_w_up"]
BIG = FFN_T + ["ffn1_w_down", "ffn2_w_down", "w_in", "w_out"]
SMALL = [n for n in WEIGHTS if n not in BIG]


def _padded(n):
    return -(-n // PACK_ALIGN) * PACK_ALIGN


def _flat_pad(a):
    f = a.reshape(-1)
    return jnp.pad(f, (0, _padded(f.shape[0]) - f.shape[0]))


def _split8(a, axis):
    shp = a.shape
    a = a.reshape(shp[:axis] + (N_DEV, shp[axis] // N_DEV) + shp[axis + 1:])
    return jnp.moveaxis(a, axis, 0)


def _merge8(a, axis):
    a = jnp.moveaxis(a, 0, axis)
    shp = a.shape
    return a.reshape(shp[:axis] + (shp[axis] * shp[axis + 1],) + shp[axis + 2:])


def _pack_small(arrs):
    return jnp.concatenate([_flat_pad(arrs[n]) for n in SMALL]).reshape(-1, LANES)


def _pack_small_parts(grads):
    cols = []
    for n in SMALL:
        g = grads[n]
        if n == "conv_w":
            s = _split8(g, 2).reshape(N_DEV, -1)
        else:
            s = jnp.broadcast_to(g.reshape(1, -1), (N_DEV, g.size))
        cols.append(jnp.pad(s, ((0, 0), (0, _padded(s.shape[1]) - s.shape[1]))))
    return jnp.concatenate(cols, axis=1).reshape(N_DEV, -1, LANES)


def _unpack_small(buf, like):
    flat = buf.reshape(-1)
    out, off = {}, 0
    for n in SMALL:
        size = like[n].size
        out[n] = flat[off:off + size].reshape(like[n].shape)
        off += _padded(size)
    return out


class _Comm:
    def __init__(self, w):
        self.w = w
        self.bf = {n: (jnp.swapaxes(w[n], 1, 2) if n in FFN_T else w[n]).astype(CDT) for n in BIG}
        self.ready = {}
        self.grads = {}

    def _ffn_shards(self, l, which):
        return jnp.stack([self.bf[f"{which}_{k}"][l] for k in FFN_PARTS])

    def _put_ffn(self, l, which, rows, t):
        self.ready[(l, which)] = dict(w_gate=rows[t], w_up=rows[t + 1], w_down=rows[t + 2])

    def weights_for(self, l, stage, x):
        bf = self.bf
        dep = None
        if (l, stage) == (0, "ffn1"):
            gd, = _all_gather([self._ffn_shards(0, "ffn1")], [True], "gather_l0_ffn1")
            self._put_ffn(0, "ffn1", gd, 0)
            self.started = _xfer_start([bf["w_in"][0:1], bf["w_out"][0:1], self.w["conv_w"]], True,
                                       [False, True, False], "gather_mix0_start", dep=gd)
            dep = self.started[3]
        elif (l, stage) == (0, "mix"):
            gi, go, gc = _xfer_wait(self.started, x, True, [False, True, False], "gather_mix0_wait")
            q, r = _merge_win(gi, "merge_l0_w_in")
            self.conv_w = _merge8(gc, 2)
            self.ready[(0, "mix")] = dict(win_qkv=q[0], win_rest=r[0], w_out=go[0], conv_w=self.conv_w[0])
            rows = jnp.concatenate([self._ffn_shards(0, "ffn2"), self._ffn_shards(1, "ffn1"), self._ffn_shards(1, "ffn2")])
            self.started = _xfer_start([rows, bf["w_in"][1:2], bf["w_out"][1:2]], True, [True, False, True],
                                       "gather_rest_start")
            dep = self.started[3]
        elif (l, stage) == (0, "ffn2"):
            gd, gi, go = _xfer_wait(self.started, x, True, [True, False, True], "gather_rest_wait")
            self._put_ffn(0, "ffn2", gd, 0)
            self._put_ffn(1, "ffn1", gd, 3)
            self._put_ffn(1, "ffn2", gd, 6)
            q, r = _merge_win(gi, "merge_l1_w_in")
            self.ready[(1, "mix")] = dict(win_qkv=q[0], win_rest=r[0], w_out=go[0], conv_w=self.conv_w[1])
        return self.ready[(l, stage)], dep

    def grads_ready(self, l, stage, grads):
        for n, v in grads.items():
            self.grads[(l, n)] = v
        gr = self.grads

        def ffn_rows(layer, which, parts=FFN_PARTS):
            return [gr[(layer, f"{which}_{k}")] for k in parts]

        if l == 1 and "ffn1_w_gate" in grads:
            self.sent1 = _xfer_start(
                [jnp.stack(ffn_rows(1, "ffn1") + ffn_rows(1, "ffn2")), gr[(1, "w_in")][:, None], gr[(1, "w_out")][None]],
                False, [True, False, True], "grads_l1_start")
            return self.sent1[3]
        if l == 0 and "ffn2_w_gate" in grads:
            self.sent_ffn2 = _xfer_start([jnp.stack(ffn_rows(0, "ffn2"))], False, [True], "grads_l0_ffn2_start")
            return self.sent_ffn2[3]
        if (l, stage) == (0, "mix"):
            self.sent_mix = _xfer_start([gr[(0, "w_in")][:, None], gr[(0, "w_out")][None]], False, [False, True],
                                        "grads_l0_mix_start")
            return self.sent_mix[3]
        if l == 0 and "ffn1_w_down" in grads:
            self.sent_down = _xfer_start([gr[(0, "ffn1_w_down")][None]], False, [True], "grads_l0_ffn1_down_start")
            return self.sent_down[3]
        if l == 0 and "ffn1_w_gate" in grads:
            self.sent_gu = _xfer_start([jnp.stack(ffn_rows(0, "ffn1", FFN_PARTS[:2]))], False, [True],
                                       "grads_l0_ffn1_gate_up_start")
            return self.sent_gu[3]
        return None

    def finish(self, m, v, after):
        w, gr = self.w, self.grads
        depth = range(w["w_in"].shape[0])
        small = {n: (gr[(None, n)] if n == "final_norm" else jnp.stack([gr[(l, n)] for l in depth])) for n in SMALL}
        r1, i1, o1 = _xfer_wait(self.sent1, after, False, [True, False, True], "grads_l1_wait")
        r2, = _xfer_wait(self.sent_ffn2, after, False, [True], "grads_l0_ffn2_wait")
        i0, o0 = _xfer_wait(self.sent_mix, after, False, [False, True], "grads_l0_mix_wait")

        def adam(n, recvs, tr):
            if n in FFN_T:
                out = _adamw_split(recvs, *[jnp.swapaxes(t[n], 1, 2) for t in (w, m, v)], f"adamw_{n}", tr)
                return [jnp.swapaxes(o, 1, 2) for o in out]
            return _adamw_split(recvs, w[n], m[n], v[n], f"adamw_{n}", tr)

        res = {}
        for t, k in enumerate(FFN_PARTS):
            res[f"ffn2_{k}"] = adam(f"ffn2_{k}", [(r2, t), (r1, 3 + t)], 176)
        res["w_in"] = adam("w_in", [(i0, 0), (i1, 0)], 256)
        res["w_out"] = adam("w_out", [(o0, 0), (o1, 0)], 128)
        rs, = _exchange([_pack_small_parts(small)], [False], "exchange_small")
        r0, = _xfer_wait(self.sent_down, res["w_out"][0], False, [True], "grads_l0_ffn1_down_wait")
        res["ffn1_w_down"] = adam("ffn1_w_down", [(r0, 0), (r1, 2)], 176)
        g0, = _xfer_wait(self.sent_gu, res["ffn1_w_down"][0], False, [True], "grads_l0_ffn1_gate_up_wait")
        res["ffn1_w_gate"] = adam("ffn1_w_gate", [(g0, 0), (r1, 0)], 176)
        res["ffn1_w_up"] = adam("ffn1_w_up", [(g0, 1), (r1, 1)], 176)
        packed = _adamw(rs, _pack_small(w), _pack_small(m), _pack_small(v), "adamw_small")
        unpacked = [_unpack_small(b, w) for b in packed]
        for n in SMALL:
            res[n] = [u[n] for u in unpacked]
        return res


def kernel(x, ffn1_norm, ffn1_w_gate, ffn1_w_up, ffn1_w_down, mix_norm, w_in, pool_w, pool_scale, forget_bias, conv_w, conv_b, conv_ln_g, conv_ln_b, w_out, ffn2_norm, ffn2_w_gate, ffn2_w_up, ffn2_w_down, final_norm, loss_target, m_ffn1_norm, m_ffn1_w_gate, m_ffn1_w_up, m_ffn1_w_down, m_mix_norm, m_w_in, m_pool_w, m_pool_scale, m_forget_bias, m_conv_w, m_conv_b, m_conv_ln_g, m_conv_ln_b, m_w_out, m_ffn2_norm, m_ffn2_w_gate, m_ffn2_w_up, m_ffn2_w_down, m_final_norm, v_ffn1_norm, v_ffn1_w_gate, v_ffn1_w_up, v_ffn1_w_down, v_mix_norm, v_w_in, v_pool_w, v_pool_scale, v_forget_bias, v_conv_w, v_conv_b, v_conv_ln_g, v_conv_ln_b, v_w_out, v_ffn2_norm, v_ffn2_w_gate, v_ffn2_w_up, v_ffn2_w_down, v_final_norm):
    w = dict(zip(WEIGHTS, (ffn1_norm, ffn1_w_gate, ffn1_w_up, ffn1_w_down, mix_norm, w_in, pool_w, pool_scale, forget_bias,
                           conv_w, conv_b, conv_ln_g, conv_ln_b, w_out, ffn2_norm, ffn2_w_gate, ffn2_w_up, ffn2_w_down,
                           final_norm)))
    m = dict(zip(WEIGHTS, (m_ffn1_norm, m_ffn1_w_gate, m_ffn1_w_up, m_ffn1_w_down, m_mix_norm, m_w_in, m_pool_w, m_pool_scale,
                           m_forget_bias, m_conv_w, m_conv_b, m_conv_ln_g, m_conv_ln_b, m_w_out, m_ffn2_norm, m_ffn2_w_gate,
                           m_ffn2_w_up, m_ffn2_w_down, m_final_norm)))
    v = dict(zip(WEIGHTS, (v_ffn1_norm, v_ffn1_w_gate, v_ffn1_w_up, v_ffn1_w_down, v_mix_norm, v_w_in, v_pool_w, v_pool_scale,
                           v_forget_bias, v_conv_w, v_conv_b, v_conv_ln_g, v_conv_ln_b, v_w_out, v_ffn2_norm, v_ffn2_w_gate,
                           v_ffn2_w_up, v_ffn2_w_down, v_final_norm)))
    comm = _Comm(w)
    loss_row, gx = _local_step(x[0], loss_target[0], w, comm.weights_for, comm.grads_ready)
    loss = lax.psum(loss_row[0, 0], ("x", "y", "c"))
    res = comm.finish(m, v, gx)
    return (loss, gx[None], *[res[n][i] for i in range(4) for n in WEIGHTS])
```

```python
import math

import numpy as np
import jax
import jax.numpy as jnp
from jax import lax
from jax.experimental import pallas as pl
from jax.experimental.pallas import tpu as pltpu

F32 = jnp.float32
CDT = jnp.bfloat16
NORM_EPS = 1e-6
N_DEV = 8
LANES = 128
PACK_ALIGN = 8 * LANES
VMEM_LIMIT = 48 * 1024 * 1024

POOL_WINDOWS = (2, 4, 8, 16)
POOL_HALO = 16
CONV_K = 31
CONV_HALO = 32
HEAD_DIM = 64
N_HEADS = 8
N_PAIRS = N_HEADS // 2
ATT_SCALE = 1.0 / math.sqrt(HEAD_DIM)
NEG = -1e30

ADAM_LR, ADAM_B1, ADAM_B2, ADAM_EPS, ADAM_WD, ADAM_STEP = 0.001, 0.9, 0.999, 1e-08, 0.01, 10

REST_W = 896
REST_Z_BLK = 6


def _cp(sem):
    return pltpu.CompilerParams(dimension_semantics=sem, vmem_limit_bytes=VMEM_LIMIT)


def _tile(n, pref):
    t = min(n, pref)
    assert n % t == 0, (n, pref)
    return t


def _sigmoid(x):
    return 1.0 / (1.0 + jnp.exp(-x))


def _sds(shape, dtype):
    return jax.ShapeDtypeStruct(shape, dtype)


_ANY = pl.BlockSpec(memory_space=pl.ANY)


def _dep(dep):
    return ([], []) if dep is None else ([_ANY], [dep])


def _wshape(w):
    return w[0].shape[1:] if isinstance(w, tuple) else w.shape


def _wspec(w, block, index_map):
    if not isinstance(w, tuple):
        return w, pl.BlockSpec(block, index_map)
    arr, t = w
    return arr, pl.BlockSpec((None,) + block, lambda *g: (t,) + index_map(*g))


def _rms_fwd(x, g, name, dep=None):
    T, D = x.shape
    tm = _tile(T, 1024)

    def body(x_ref, g_ref, *rest):
        o_ref = rest[-1]
        xv = x_ref[...]
        r = lax.rsqrt(jnp.mean(xv * xv, axis=-1, keepdims=True) + NORM_EPS)
        o_ref[...] = (xv * r * g_ref[...]).astype(o_ref.dtype)

    dspec, darg = _dep(dep)
    return pl.pallas_call(
        body, grid=(T // tm,),
        in_specs=[pl.BlockSpec((tm, D), lambda i: (i, 0)), pl.BlockSpec((1, D), lambda i: (0, 0))] + dspec,
        out_specs=pl.BlockSpec((tm, D), lambda i: (i, 0)),
        out_shape=_sds((T, D), CDT), name=name, compiler_params=_cp(("parallel",)))(x, g, *darg)


def _rms_bwd(x, g, dh, gres, name):
    T, D = x.shape
    tm = _tile(T, 512)

    def body(x_ref, g_ref, dh_ref, gres_ref, gin_ref, dg_ref):
        i = pl.program_id(0)
        xv = x_ref[...]
        d = dh_ref[...]
        r = lax.rsqrt(jnp.mean(xv * xv, axis=-1, keepdims=True) + NORM_EPS)
        xh = xv * r
        dxh = d * g_ref[...]
        c = jnp.mean(dxh * xh, axis=-1, keepdims=True)
        gin_ref[...] = gres_ref[...] + r * (dxh - xh * c)
        part = jnp.sum(d * xh, axis=0, keepdims=True)

        @pl.when(i == 0)
        def _():
            dg_ref[...] = part

        @pl.when(i > 0)
        def _():
            dg_ref[...] += part

    row = pl.BlockSpec((tm, D), lambda i: (i, 0))
    vec = pl.BlockSpec((1, D), lambda i: (0, 0))
    return pl.pallas_call(
        body, grid=(T // tm,), in_specs=[row, vec, row, row], out_specs=[row, vec],
        out_shape=[_sds((T, D), F32), _sds((1, D), F32)], name=name, compiler_params=_cp(("arbitrary",)))(x, g, dh, gres)


def _loss_bwd(x, g, target, name):
    T, D = x.shape
    tm = _tile(T, 512)

    def body(x_ref, g_ref, t_ref, loss_ref, dx_ref, dg_ref):
        i = pl.program_id(0)
        xv = x_ref[...]
        gv = g_ref[...]
        r = lax.rsqrt(jnp.mean(xv * xv, axis=-1, keepdims=True) + NORM_EPS)
        xh = xv * r
        err = xh * gv - t_ref[...]
        lpart = 0.5 * jnp.sum(jnp.mean(err * err, axis=-1, keepdims=True), axis=0, keepdims=True)
        dy = err * (1.0 / D)
        dxh = dy * gv
        c = jnp.mean(dxh * xh, axis=-1, keepdims=True)
        dx_ref[...] = r * (dxh - xh * c)
        part = jnp.sum(dy * xh, axis=0, keepdims=True)
        lrow = jnp.broadcast_to(lpart, (1, LANES))

        @pl.when(i == 0)
        def _():
            dg_ref[...] = part
            loss_ref[...] = lrow

        @pl.when(i > 0)
        def _():
            dg_ref[...] += part
            loss_ref[...] += lrow

    row = pl.BlockSpec((tm, D), lambda i: (i, 0))
    vec = pl.BlockSpec((1, D), lambda i: (0, 0))
    return pl.pallas_call(
        body, grid=(T // tm,), in_specs=[row, vec, row],
        out_specs=[pl.BlockSpec((1, LANES), lambda i: (0, 0)), row, vec],
        out_shape=[_sds((1, LANES), F32), _sds((T, D), F32), _sds((1, D), F32)],
        name=name, compiler_params=_cp(("arbitrary",)))(x, g, target)


def _mm(pairs, *, name, res=None, alpha=1.0, out_dtype=F32, tm=512, tn=None, dep=None):
    T = pairs[0][0].shape[0]
    N = _wshape(pairs[0][1])[0 if pairs[0][2] else 1]
    tm = _tile(T, tm)
    tn = N if tn is None else _tile(N, tn)
    flags = [p[2] for p in pairs]
    n_in = 2 * len(pairs)

    def body(*refs):
        o_ref = refs[-1]
        acc = None
        for p, bt in enumerate(flags):
            a = refs[2 * p][...].astype(CDT)
            b = refs[2 * p + 1][...]
            dims = (((1,), (1,)), ((), ())) if bt else (((1,), (0,)), ((), ()))
            d = lax.dot_general(a, b, dims, preferred_element_type=F32)
            acc = d if acc is None else acc + d
        if alpha != 1.0:
            acc = acc * alpha
        if res is not None:
            acc = refs[n_in][...] + acc
        o_ref[...] = acc.astype(o_ref.dtype)

    in_specs, args = [], []
    for a, b, bt in pairs:
        K = a.shape[1]
        in_specs.append(pl.BlockSpec((tm, K), lambda i, j: (i, 0)))
        b, bspec = _wspec(b, (tn, K), lambda i, j: (j, 0)) if bt else _wspec(b, (K, tn), lambda i, j: (0, j))
        in_specs.append(bspec)
        args += [a, b]
    if res is not None:
        in_specs.append(pl.BlockSpec((tm, tn), lambda i, j: (i, j)))
        args.append(res)
    dspec, darg = _dep(dep)
    in_specs += dspec
    args += darg
    return pl.pallas_call(
        body, grid=(T // tm, N // tn), in_specs=in_specs,
        out_specs=pl.BlockSpec((tm, tn), lambda i, j: (i, j)),
        out_shape=_sds((T, N), out_dtype), name=name, compiler_params=_cp(("parallel", "arbitrary")))(*args)


def _mm_norm_bwd(pairs, x, g, gres, *, name, tm=256, dep=None):
    T, D = x.shape
    tm = _tile(T, tm)
    n_in = 2 * len(pairs)
    flags = [p[2] for p in pairs]

    def body(*refs):
        x_ref, g_ref, gres_ref = refs[n_in:n_in + 3]
        gin_ref, dg_ref = refs[-2:]
        i = pl.program_id(0)
        d = None
        for p, bt in enumerate(flags):
            dims = (((1,), (1,)), ((), ())) if bt else (((1,), (0,)), ((), ()))
            part = lax.dot_general(refs[2 * p][...].astype(CDT), refs[2 * p + 1][...], dims, preferred_element_type=F32)
            d = part if d is None else d + part
        xv = x_ref[...]
        r = lax.rsqrt(jnp.mean(xv * xv, axis=-1, keepdims=True) + NORM_EPS)
        xh = xv * r
        dxh = d * g_ref[...]
        c = jnp.mean(dxh * xh, axis=-1, keepdims=True)
        gin_ref[...] = gres_ref[...] + r * (dxh - xh * c)
        part = jnp.sum(d * xh, axis=0, keepdims=True)

        @pl.when(i == 0)
        def _():
            dg_ref[...] = part

        @pl.when(i > 0)
        def _():
            dg_ref[...] += part

    in_specs, args = [], []
    for a, b, bt in pairs:
        K = a.shape[1]
        b, bspec = _wspec(b, tuple(_wshape(b)), lambda i: (0, 0))
        in_specs += [pl.BlockSpec((tm, K), lambda i: (i, 0)), bspec]
        args += [a, b]
    row = pl.BlockSpec((tm, D), lambda i: (i, 0))
    vec = pl.BlockSpec((1, D), lambda i: (0, 0))
    dspec, darg = _dep(dep)
    return pl.pallas_call(
        body, grid=(T // tm,), in_specs=in_specs + [row, vec, row] + dspec, out_specs=[row, vec],
        out_shape=[_sds((T, D), F32), _sds((1, D), F32)], name=name,
        compiler_params=_cp(("arbitrary",)))(*args, x, g, gres, *darg)


def _mm_tn(a, b, *, name, alpha=1.0, tk=2048, dep=None):
    T, M = a.shape
    N = b.shape[1]
    tm = M if M <= 1024 else M // 2
    tn = N if N <= 1536 else N // 2
    assert M % tm == 0 and N % tn == 0 and tm % LANES == 0 and tn % LANES == 0
    tk = _tile(T, tk)
    nk = T // tk

    def body(a_ref, b_ref, *rest):
        o_ref = rest[-1]
        k = pl.program_id(2)
        d = lax.dot_general(a_ref[...].astype(CDT), b_ref[...].astype(CDT), (((0,), (0,)), ((), ())),
                            preferred_element_type=F32)

        @pl.when(k == 0)
        def _():
            o_ref[...] = d

        @pl.when(k > 0)
        def _():
            o_ref[...] += d

        if alpha != 1.0:
            @pl.when(k == nk - 1)
            def _():
                o_ref[...] *= alpha

    dspec, darg = _dep(dep)
    return pl.pallas_call(
        body, grid=(M // tm, N // tn, nk),
        in_specs=[pl.BlockSpec((tk, tm), lambda i, j, k: (k, i)), pl.BlockSpec((tk, tn), lambda i, j, k: (k, j))] + dspec,
        out_specs=pl.BlockSpec((tm, tn), lambda i, j, k: (i, j)),
        out_shape=_sds((M, N), F32), name=name, compiler_params=_cp(("parallel", "parallel", "arbitrary")))(a, b, *darg)


def _ffn_up(h, wgt, wut, name):
    T, D = h.shape
    Fh = _wshape(wgt)[0]
    tm = _tile(T, 2048)
    tn = _tile(Fh, 256)
    nt = (((1,), (1,)), ((), ()))

    def body(h_ref, wg_ref, wu_ref, a_ref, b_ref, s_ref):
        hv = h_ref[...]
        a = lax.dot_general(hv, wg_ref[...], nt, preferred_element_type=F32)
        b = lax.dot_general(hv, wu_ref[...], nt, preferred_element_type=F32)
        a_ref[...] = a.astype(a_ref.dtype)
        b_ref[...] = b.astype(b_ref.dtype)
        s_ref[...] = (a * _sigmoid(a) * b).astype(s_ref.dtype)

    wgt, gspec = _wspec(wgt, (tn, D), lambda i, j: (j, 0))
    wut, uspec = _wspec(wut, (tn, D), lambda i, j: (j, 0))
    ospec = pl.BlockSpec((tm, tn), lambda i, j: (i, j))
    return pl.pallas_call(
        body, grid=(T // tm, Fh // tn),
        in_specs=[pl.BlockSpec((tm, D), lambda i, j: (i, 0)), gspec, uspec],
        out_specs=[ospec, ospec, ospec],
        out_shape=[_sds((T, Fh), CDT), _sds((T, Fh), CDT), _sds((T, Fh), CDT)],
        name=name, compiler_params=_cp(("parallel", "arbitrary")))(h, wgt, wut)


def _ffn_bwd_ds(gout, wd, a, b, name, dep=None):
    T, D = gout.shape
    Fh = _wshape(wd)[0]
    tm = _tile(T, 2048)
    tn = _tile(Fh, 256)

    def body(g_ref, wd_ref, a_ref, b_ref, *rest):
        da_ref, db_ref = rest[-2:]
        dy = (0.5 * g_ref[...]).astype(CDT)
        ds = lax.dot_general(dy, wd_ref[...], (((1,), (1,)), ((), ())), preferred_element_type=F32)
        av = a_ref[...].astype(F32)
        sg = _sigmoid(av)
        da_ref[...] = (ds * b_ref[...].astype(F32) * (sg * (1.0 + av * (1.0 - sg)))).astype(da_ref.dtype)
        db_ref[...] = (ds * (av * sg)).astype(db_ref.dtype)

    ospec = pl.BlockSpec((tm, tn), lambda i, j: (i, j))
    dspec, darg = _dep(dep)
    wd, wspec = _wspec(wd, (tn, D), lambda i, j: (j, 0))
    return pl.pallas_call(
        body, grid=(T // tm, Fh // tn),
        in_specs=[pl.BlockSpec((tm, D), lambda i, j: (i, 0)), wspec, ospec, ospec] + dspec,
        out_specs=[ospec, ospec],
        out_shape=[_sds((T, Fh), CDT), _sds((T, Fh), CDT)],
        name=name, compiler_params=_cp(("parallel", "arbitrary")))(gout, wd, a, b, *darg)


def _ffn_fwd(x, gamma, wgt, wut, wd, tag, dep=None):
    h = _rms_fwd(x, gamma, f"{tag}_norm", dep)
    a, b, s = _ffn_up(h, wgt, wut, f"{tag}_up")
    y = _mm([(s, wd, False)], res=x, alpha=0.5, name=f"{tag}_down")
    return y, (x, h, a, b, s)


def _ffn_bwd(saved, gamma, wgt, wut, wd, gout, tag, dep, on_grads):
    x, h, a, b, s = saved
    dwd = _mm_tn(s, gout, alpha=0.5, name=f"{tag}_dwd", dep=dep)
    da, db = _ffn_bwd_ds(gout, wd, a, b, f"{tag}_bwd_ds", on_grads(dict(w_down=dwd)))
    dwgt = _mm_tn(da, h, name=f"{tag}_dwg")
    dwut = _mm_tn(db, h, name=f"{tag}_dwu")
    dep = on_grads(dict(w_gate=dwgt, w_up=dwut))
    return _mm_norm_bwd([(da, wgt, False), (db, wut, False)], x, gamma, gout, name=f"{tag}_dh_norm_bwd", dep=dep)


def _fgate_fwd(rest, bias, name, bt=512):
    T = rest.shape[0]
    bt = _tile(T, bt)

    def body(z_ref, b_ref, fc_ref, ft_ref, carry):
        i = pl.program_id(0)

        @pl.when(i == 0)
        def _():
            carry[...] = jnp.zeros_like(carry)

        zb = z_ref[...] + b_ref[...]
        e = jnp.exp(-jnp.abs(zb))
        u = 1.0 + e
        log1p_e = jnp.where(u == 1.0, e, jnp.log(u) * (e / (u - 1.0)))
        x = jnp.minimum(zb, 0.0) - log1p_e
        row = lax.broadcasted_iota(jnp.int32, x.shape, 0)
        sh = 1
        while sh < bt:
            x = x + jnp.where(row >= sh, pltpu.roll(x, sh, 0), 0.0)
            sh *= 2
        f = x + carry[...]
        carry[...] = f[bt - 1:bt, :]
        fc_ref[...] = f
        ft_ref[...] = jnp.transpose(f)[0:N_HEADS, :]

    return pl.pallas_call(
        body, grid=(T // bt,),
        in_specs=[pl.BlockSpec((bt, LANES), lambda i: (i, REST_Z_BLK)), pl.BlockSpec((1, LANES), lambda i: (0, 0))],
        out_specs=[pl.BlockSpec((bt, LANES), lambda i: (i, 0)), pl.BlockSpec((N_HEADS, bt), lambda i: (0, i))],
        out_shape=[_sds((T, LANES), F32), _sds((N_HEADS, T), F32)],
        scratch_shapes=[pltpu.VMEM((1, LANES), F32)],
        name=name, compiler_params=_cp(("arbitrary",)))(rest, bias)


def _fgate_bwd(dfk, rest, bias, name, bt=512):
    T = rest.shape[0]
    bt = _tile(T, bt)
    nb = T // bt

    def body(df_ref, z_ref, b_ref, dz_ref, db_ref, carry):
        i = pl.program_id(0)

        @pl.when(i == 0)
        def _():
            carry[...] = jnp.zeros_like(carry)

        dfv = df_ref[...]
        lane = lax.broadcasted_iota(jnp.int32, (bt, LANES), 1)
        x = jnp.zeros((bt, LANES), F32)
        for h in range(N_HEADS):
            x = jnp.where(lane == h, dfv[:, HEAD_DIM * h:HEAD_DIM * h + 1], x)
        row = lax.broadcasted_iota(jnp.int32, x.shape, 0)
        sh = 1
        while sh < bt:
            x = x + jnp.where(row + sh < bt, pltpu.roll(x, bt - sh, 0), 0.0)
            sh *= 2
        dlf = x + carry[...]
        carry[...] = dlf[0:1, :]
        zb = z_ref[...] + b_ref[...]
        dz = jnp.where(lane < N_HEADS, dlf * _sigmoid(-zb), 0.0)
        dz_ref[...] = dz.astype(dz_ref.dtype)
        part = jnp.sum(dz, axis=0, keepdims=True)

        @pl.when(i == 0)
        def _():
            db_ref[...] = part

        @pl.when(i > 0)
        def _():
            db_ref[...] += part

    return pl.pallas_call(
        body, grid=(nb,),
        in_specs=[pl.BlockSpec((bt, 4 * LANES), lambda i: (nb - 1 - i, 0)),
                  pl.BlockSpec((bt, LANES), lambda i: (nb - 1 - i, REST_Z_BLK)),
                  pl.BlockSpec((1, LANES), lambda i: (0, 0))],
        out_specs=[pl.BlockSpec((bt, LANES), lambda i: (nb - 1 - i, 0)), pl.BlockSpec((1, LANES), lambda i: (0, 0))],
        out_shape=[_sds((T, LANES), CDT), _sds((1, LANES), F32)],
        scratch_shapes=[pltpu.VMEM((1, LANES), F32)],
        name=name, compiler_params=_cp(("arbitrary",)))(dfk, rest, bias)


def _by_group(vals, lane):
    out = vals[-1]
    for g in range(len(vals) - 2, -1, -1):
        out = jnp.where(lane // 64 == g, vals[g], out)
    return out


def _pool_counts(t0, n, lane):
    t = t0 + lax.broadcasted_iota(jnp.int32, (n, 256), 0)
    return _by_group([jnp.minimum(t + 1, w) for w in POOL_WINDOWS], lane).astype(F32)


def _pooled(u, halo, i, bt):
    lane = lax.broadcasted_iota(jnp.int32, (bt, 256), 1)
    ext = jnp.concatenate([jnp.where(i > 0, halo, 0.0), u], axis=0)
    sums, s, sh = [], ext, 1
    for _ in POOL_WINDOWS:
        s = s + pltpu.roll(s, sh, 0)
        sums.append(s[POOL_HALO:, :])
        sh *= 2
    return _by_group(sums, lane) / _pool_counts(i * bt, bt, lane) - u


def _pool_fwd(rest, wbd, scale, name, bt=512):
    T = rest.shape[0]
    bt = _tile(T, bt)
    hb = bt // POOL_HALO

    def body(u_ref, halo_ref, w_ref, sc_ref, o_ref):
        i = pl.program_id(0)
        pooled = _pooled(u_ref[...], halo_ref[...], i, bt)
        mixed = jnp.dot(pooled.astype(CDT), w_ref[...], preferred_element_type=F32)
        o_ref[...] = (mixed * sc_ref[...]).astype(o_ref.dtype)

    return pl.pallas_call(
        body, grid=(T // bt,),
        in_specs=[pl.BlockSpec((bt, 256), lambda i: (i, 0)),
                  pl.BlockSpec((POOL_HALO, 256), lambda i: (jnp.maximum(i * hb - 1, 0), 0)),
                  pl.BlockSpec((256, 256), lambda i: (0, 0)), pl.BlockSpec((1, 256), lambda i: (0, 0))],
        out_specs=pl.BlockSpec((bt, 256), lambda i: (i, 0)),
        out_shape=_sds((T, 256), CDT), name=name, compiler_params=_cp(("parallel",)))(rest, rest, wbd, scale)


def _pool_bwd(dcat, rest, wbd, scale, name, bt=512):
    T = rest.shape[0]
    bt = _tile(T, bt)
    hb = bt // POOL_HALO
    nb = T // bt
    n = bt + POOL_HALO

    def body(dy_ref, dyn_ref, u_ref, halo_ref, w_ref, sc_ref, du_ref, dw_ref, dsc_ref):
        i = pl.program_id(0)
        lane = lax.broadcasted_iota(jnp.int32, (bt, 256), 1)
        w = w_ref[...]
        sc = sc_ref[...]
        pooled = _pooled(u_ref[...], halo_ref[...], i, bt)
        pooled_c = pooled.astype(CDT)
        mixed = jnp.dot(pooled_c, w, preferred_element_type=F32)
        dy = dy_ref[...]
        dm = (dy * sc).astype(CDT)
        dsc = jnp.sum(dy * mixed, axis=0, keepdims=True)
        dw = lax.dot_general(pooled_c, dm, (((0,), (0,)), ((), ())), preferred_element_type=F32)
        nt = (((1,), (1,)), ((), ()))
        dpl = lax.dot_general(dm, w, nt, preferred_element_type=F32)
        dmn = (jnp.where(i < nb - 1, dyn_ref[...], 0.0) * sc).astype(CDT)
        dpln = lax.dot_general(dmn, w, nt, preferred_element_type=F32)
        lane_h = lax.broadcasted_iota(jnp.int32, (POOL_HALO, 256), 1)
        ext = jnp.concatenate([dpl / _pool_counts(i * bt, bt, lane),
                               dpln / _pool_counts((i + 1) * bt, POOL_HALO, lane_h)], axis=0)
        sums, s, sh = [], ext, 1
        for _ in POOL_WINDOWS:
            s = s + pltpu.roll(s, n - sh, 0)
            sums.append(s[0:bt, :])
            sh *= 2
        du_ref[...] = (_by_group(sums, lane) - dpl).astype(du_ref.dtype)

        @pl.when(i == 0)
        def _():
            dw_ref[...] = dw
            dsc_ref[...] = dsc

        @pl.when(i > 0)
        def _():
            dw_ref[...] += dw
            dsc_ref[...] += dsc

    full = pl.BlockSpec((256, 256), lambda i: (0, 0))
    vec = pl.BlockSpec((1, 256), lambda i: (0, 0))
    return pl.pallas_call(
        body, grid=(nb,),
        in_specs=[pl.BlockSpec((bt, 256), lambda i: (i, 0)),
                  pl.BlockSpec((POOL_HALO, 256), lambda i: (jnp.minimum((i + 1) * hb, nb * hb - 1), 0)),
                  pl.BlockSpec((bt, 256), lambda i: (i, 0)),
                  pl.BlockSpec((POOL_HALO, 256), lambda i: (jnp.maximum(i * hb - 1, 0), 0)),
                  full, vec],
        out_specs=[pl.BlockSpec((bt, 256), lambda i: (i, 0)), full, vec],
        out_shape=[_sds((T, 256), CDT), _sds((256, 256), F32), _sds((1, 256), F32)],
        name=name, compiler_params=_cp(("arbitrary",)))(dcat, dcat, rest, rest, wbd, scale)


def _glu_ext(a_ref, g_ref, ah_ref, gh_ref, i):
    u = a_ref[...] * _sigmoid(g_ref[...])
    uh = jnp.where(i > 0, ah_ref[...] * _sigmoid(gh_ref[...]), 0.0)
    return jnp.concatenate([uh, u], axis=0)


def _conv_fwd(rest, cw, cb, lg, lb, name, bt=512):
    T = rest.shape[0]
    bt = _tile(T, bt)
    hb = bt // CONV_HALO

    def body(a_ref, g_ref, ah_ref, gh_ref, cw_ref, cb_ref, lg_ref, lb_ref, o_ref, y_ref):
        i = pl.program_id(0)
        ext = _glu_ext(a_ref, g_ref, ah_ref, gh_ref, i)
        w = cw_ref[...]
        acc = w[CONV_K - 1:CONV_K, :] * ext
        for k in range(CONV_K - 1):
            acc = acc + w[k:k + 1, :] * pltpu.roll(ext, CONV_K - 1 - k, 0)
        y = acc[CONV_HALO:, :] + cb_ref[...]
        y_ref[...] = y
        yc = y - jnp.mean(y, axis=-1, keepdims=True)
        yn = yc * lax.rsqrt(jnp.mean(yc * yc, axis=-1, keepdims=True) + NORM_EPS)
        z = yn * lg_ref[...] + lb_ref[...]
        o_ref[...] = (z * _sigmoid(z)).astype(o_ref.dtype)

    def cur(c):
        return pl.BlockSpec((bt, 256), lambda i: (i, c))

    def prev(c):
        return pl.BlockSpec((CONV_HALO, 256), lambda i: (jnp.maximum(i * hb - 1, 0), c))

    vec = pl.BlockSpec((1, 256), lambda i: (0, 0))
    return pl.pallas_call(
        body, grid=(T // bt,),
        in_specs=[cur(1), cur(2), prev(1), prev(2), pl.BlockSpec((CONV_HALO, 256), lambda i: (0, 0)), vec, vec, vec],
        out_specs=[pl.BlockSpec((bt, 256), lambda i: (i, 0)), pl.BlockSpec((bt, 256), lambda i: (i, 0))],
        out_shape=[_sds((T, 256), CDT), _sds((T, 256), F32)],
        name=name, compiler_params=_cp(("parallel",)))(rest, rest, rest, rest, cw, cb, lg, lb)


def _conv_bwd(dcat, yconv, rest, cw, lg, lb, name, bt=512):
    T = rest.shape[0]
    bt = _tile(T, bt)
    hb = bt // CONV_HALO
    nb = T // bt
    n = bt + CONV_HALO

    def body(dy_ref, dyn_ref, y_ref, yn_ref, a_ref, g_ref, ah_ref, gh_ref, cw_ref, lg_ref, lb_ref,
             da_ref, dg_ref, dcw_ref, dcb_ref, dlg_ref, dlb_ref):
        i = pl.program_id(0)
        lgv = lg_ref[...]
        lbv = lb_ref[...]

        def ln_swish_bwd(dout, y):
            yc = y - jnp.mean(y, axis=-1, keepdims=True)
            rs = lax.rsqrt(jnp.mean(yc * yc, axis=-1, keepdims=True) + NORM_EPS)
            yn = yc * rs
            z = yn * lgv + lbv
            sg = _sigmoid(z)
            dz = dout * (sg * (1.0 + z * (1.0 - sg)))
            dyn = dz * lgv
            dyc = rs * (dyn - jnp.mean(dyn, axis=-1, keepdims=True) - yn * jnp.mean(dyn * yn, axis=-1, keepdims=True))
            return dyc, dz, yn

        dyc, dz, yn = ln_swish_bwd(dy_ref[...], y_ref[...])
        dyc_next, _, _ = ln_swish_bwd(dyn_ref[...], yn_ref[...])
        dyc_next = jnp.where(i < nb - 1, dyc_next, 0.0)
        ext_u = _glu_ext(a_ref, g_ref, ah_ref, gh_ref, i)
        ext_d = jnp.concatenate([dyc, dyc_next], axis=0)
        w = cw_ref[...]
        du = w[CONV_K - 1:CONV_K, :] * ext_d
        rows = []
        for k in range(CONV_K):
            s = CONV_K - 1 - k
            if s > 0:
                du = du + w[k:k + 1, :] * pltpu.roll(ext_d, n - s, 0)
                us = pltpu.roll(ext_u, s, 0)[CONV_HALO:, :]
            else:
                us = ext_u[CONV_HALO:, :]
            rows.append(jnp.sum(dyc * us, axis=0, keepdims=True))
        rows.append(jnp.zeros((1, 256), F32))
        dcw = jnp.concatenate(rows, axis=0)
        du = du[0:bt, :]
        av = a_ref[...]
        sg = _sigmoid(g_ref[...])
        da_ref[...] = (du * sg).astype(da_ref.dtype)
        dg_ref[...] = (du * av * (sg * (1.0 - sg))).astype(dg_ref.dtype)
        dcb = jnp.sum(dyc, axis=0, keepdims=True)
        dlg = jnp.sum(dz * yn, axis=0, keepdims=True)
        dlb = jnp.sum(dz, axis=0, keepdims=True)

        @pl.when(i == 0)
        def _():
            dcw_ref[...] = dcw
            dcb_ref[...] = dcb
            dlg_ref[...] = dlg
            dlb_ref[...] = dlb

        @pl.when(i > 0)
        def _():
            dcw_ref[...] += dcw
            dcb_ref[...] += dcb
            dlg_ref[...] += dlg
            dlb_ref[...] += dlb

    def cur(c):
        return pl.BlockSpec((bt, 256), lambda i: (i, c))

    def prev(c):
        return pl.BlockSpec((CONV_HALO, 256), lambda i: (jnp.maximum(i * hb - 1, 0), c))

    def nxt(c):
        return pl.BlockSpec((CONV_HALO, 256), lambda i: (jnp.minimum((i + 1) * hb, nb * hb - 1), c))

    vec = pl.BlockSpec((1, 256), lambda i: (0, 0))
    wfull = pl.BlockSpec((CONV_HALO, 256), lambda i: (0, 0))
    return pl.pallas_call(
        body, grid=(nb,),
        in_specs=[cur(3), nxt(3), cur(0), nxt(0), cur(1), cur(2), prev(1), prev(2), wfull, vec, vec],
        out_specs=[cur(0), cur(0), wfull, vec, vec, vec],
        out_shape=[_sds((T, 256), CDT), _sds((T, 256), CDT), _sds((CONV_HALO, 256), F32),
                   _sds((1, 256), F32), _sds((1, 256), F32), _sds((1, 256), F32)],
        name=name, compiler_params=_cp(("arbitrary",)))(dcat, dcat, yconv, yconv, rest, rest, rest, rest, cw, lg, lb)


def _half_mask(shape, a):
    lane = lax.broadcasted_iota(jnp.int32, shape, 1)
    return (lane // HEAD_DIM) == a


def _attn_fwd(qkv, fcol, frow, name, blk=1024):
    T = qkv.shape[0]
    blk = _tile(T, blk)
    nq = T // blk
    nt = (((1,), (1,)), ((), ()))

    def body(q_ref, k_ref, v_ref, fc_ref, fr_ref, o_ref, lse_ref):
        p_id = pl.program_id(0)
        i = pl.program_id(1)
        q2 = q_ref[...]
        fc = fc_ref[...]
        lane = lax.broadcasted_iota(jnp.int32, (blk, LANES), 1)
        tri = lax.broadcasted_iota(jnp.int32, (blk, blk), 1) <= lax.broadcasted_iota(jnp.int32, (blk, blk), 0)
        masks = [_half_mask(q2.shape, a) for a in range(2)]
        qs = [jnp.where(hm, q2, jnp.zeros_like(q2)) * ATT_SCALE for hm in masks]
        fqs = [jnp.sum(jnp.where(lane == 2 * p_id + a, fc, 0.0), axis=1, keepdims=True) for a in range(2)]

        def tile(j, carry, masked):
            cols = pl.ds(pl.multiple_of(j * blk, blk), blk)
            kj = k_ref[cols, :]
            vj = v_ref[cols, :]
            out = []
            for a in range(2):
                m, acc = carry[2 * a:2 * a + 2]
                va = jnp.where(masks[a], vj, jnp.ones_like(vj))
                s = lax.dot_general(qs[a], kj, nt, preferred_element_type=F32) + (fqs[a] - fr_ref[a:a + 1, cols])
                if masked:
                    s = jnp.where(tri, s, NEG)
                m_new = jnp.maximum(m, jnp.max(s, axis=1, keepdims=True))
                alpha = jnp.exp(m - m_new)
                pr = jnp.exp(s - m_new)
                hi = lax.bitcast_convert_type(lax.bitcast_convert_type(pr, jnp.uint32) & jnp.uint32(0xFFFF0000), F32)
                pv = (jnp.dot(hi.astype(CDT), va, preferred_element_type=F32)
                      + jnp.dot((pr - hi).astype(CDT), va, preferred_element_type=F32))
                out += [m_new, alpha * acc + pv]
            return tuple(out)

        init = (jnp.full((blk, 1), NEG, F32), jnp.zeros((blk, LANES), F32)) * 2
        carry = lax.fori_loop(0, i, lambda j, c: tile(j, c, False), init)
        carry = tile(i, carry, True)
        ls = [carry[1][:, HEAD_DIM:HEAD_DIM + 1], carry[3][:, 0:1]]
        lo = lane < HEAD_DIM
        o_ref[...] = jnp.where(lo, carry[1] / ls[0], carry[3] / ls[1])
        lse_t = jnp.transpose(jnp.where(lo, carry[0] + jnp.log(ls[0]), carry[2] + jnp.log(ls[1])))
        lse_ref[...] = jnp.concatenate([lse_t[0:1, :], lse_t[HEAD_DIM:HEAD_DIM + 1, :]], axis=0)

    return pl.pallas_call(
        body, grid=(N_PAIRS, nq),
        in_specs=[pl.BlockSpec((blk, LANES), lambda p, i: (i, p)),
                  pl.BlockSpec((T, LANES), lambda p, i: (0, N_PAIRS + p)),
                  pl.BlockSpec((T, LANES), lambda p, i: (0, 2 * N_PAIRS + p)),
                  pl.BlockSpec((blk, LANES), lambda p, i: (i, 0)),
                  pl.BlockSpec((None, 2, T), lambda p, i: (p, 0, 0))],
        out_specs=[pl.BlockSpec((blk, LANES), lambda p, i: (i, p)), pl.BlockSpec((None, 2, blk), lambda p, i: (p, 0, i))],
        out_shape=[_sds((T, N_PAIRS * LANES), F32), _sds((N_PAIRS, 2, T), F32)],
        name=name, compiler_params=_cp(("parallel", "arbitrary")))(qkv, qkv, qkv, fcol, frow)


def _attn_delta(dcat, o, name, blk=512):
    T = o.shape[0]
    blk = _tile(T, blk)

    def body(d_ref, o_ref, out_ref):
        prod = d_ref[:, 256:768].astype(CDT).astype(F32) * o_ref[...]
        pt = jnp.transpose(prod)
        out_ref[...] = jnp.sum(pt.reshape(N_HEADS, HEAD_DIM, blk), axis=1)

    return pl.pallas_call(
        body, grid=(T // blk,),
        in_specs=[pl.BlockSpec((blk, 1024), lambda i: (i, 0)), pl.BlockSpec((blk, 512), lambda i: (i, 0))],
        out_specs=pl.BlockSpec((N_HEADS, blk), lambda i: (0, i)),
        out_shape=_sds((N_HEADS, T), F32), name=name, compiler_params=_cp(("parallel",)))(dcat, o)


def _attn_bwd(qkv, dcat, fcol, frow, lse, delta, name, blk=1024):
    T = qkv.shape[0]
    blk = _tile(T, blk)
    nq = T // blk
    nt = (((1,), (1,)), ((), ()))

    def body(q_ref, do_ref, k_ref, v_ref, fc_ref, fr_ref, lse_ref, dl_ref, dqt_ref, dk_ref, dv_ref, df_ref):
        p_id = pl.program_id(0)
        j = pl.program_id(1)

        @pl.when(j == 0)
        def _():
            dqt_ref[...] = jnp.zeros_like(dqt_ref)

        k2 = k_ref[...]
        v2 = v_ref[...]
        fc = fc_ref[...]
        lane = lax.broadcasted_iota(jnp.int32, (blk, LANES), 1)
        tri = lax.broadcasted_iota(jnp.int32, (blk, blk), 0) <= lax.broadcasted_iota(jnp.int32, (blk, blk), 1)
        masks = [_half_mask(k2.shape, a) for a in range(2)]
        kas = [jnp.where(hm, k2, jnp.zeros_like(k2)) * ATT_SCALE for hm in masks]
        kats = [jnp.transpose(ka) for ka in kas]
        vas = [jnp.where(hm, v2, jnp.zeros_like(v2)) for hm in masks]
        fks = [jnp.sum(jnp.where(lane == 2 * p_id + a, fc, 0.0), axis=1, keepdims=True) for a in range(2)]

        def tile(i, carry, masked):
            rows = pl.ds(pl.multiple_of(i * blk, blk), blk)
            qi = q_ref[rows, :]
            doi = do_ref[rows, :].astype(CDT)
            out = []
            dqt = None
            for a in range(2):
                dk_acc, dv_acc, df_acc = carry[3 * a:3 * a + 3]
                st = lax.dot_general(kas[a], qi, nt, preferred_element_type=F32)
                e = (st + (fr_ref[a:a + 1, rows] - fks[a])) - lse_ref[a:a + 1, rows]
                if masked:
                    e = jnp.where(tri, e, NEG)
                pt = jnp.exp(e)
                dpt = lax.dot_general(vas[a], doi, nt, preferred_element_type=F32)
                ds32 = pt * (dpt - dl_ref[a:a + 1, rows])
                dst = ds32.astype(CDT)
                df_acc = df_acc + jnp.sum(ds32, axis=1, keepdims=True)
                dv_acc = dv_acc + jnp.dot(pt.astype(CDT), doi, preferred_element_type=F32)
                dk_acc = dk_acc + jnp.dot(dst, qi, preferred_element_type=F32)
                part = jnp.dot(kats[a], dst, preferred_element_type=F32)
                dqt = part if dqt is None else dqt + part
                out += [dk_acc, dv_acc, df_acc]
            dqt_ref[:, rows] += dqt
            return tuple(out)

        init = (jnp.zeros((blk, LANES), F32), jnp.zeros((blk, LANES), F32), jnp.zeros((blk, 1), F32)) * 2
        carry = tile(j, init, True)
        carry = lax.fori_loop(j + 1, nq, lambda i, c: tile(i, c, False), carry)
        lo = lane < HEAD_DIM
        dk_ref[...] = (jnp.where(lo, carry[0], carry[3]) * ATT_SCALE).astype(dk_ref.dtype)
        dv_ref[...] = jnp.where(lo, carry[1], carry[4]).astype(dv_ref.dtype)
        df_ref[...] = -jnp.where(lo, carry[2], carry[5])

    res = pl.BlockSpec((T, LANES), lambda p, j: (0, p))
    rows = pl.BlockSpec((None, 2, T), lambda p, j: (p, 0, 0))
    kv_out = pl.BlockSpec((blk, LANES), lambda p, j: (j, p))
    return pl.pallas_call(
        body, grid=(N_PAIRS, nq),
        in_specs=[res, pl.BlockSpec((T, LANES), lambda p, j: (0, 2 + p)),
                  pl.BlockSpec((blk, LANES), lambda p, j: (j, N_PAIRS + p)),
                  pl.BlockSpec((blk, LANES), lambda p, j: (j, 2 * N_PAIRS + p)),
                  pl.BlockSpec((blk, LANES), lambda p, j: (j, 0)), rows, rows, rows],
        out_specs=[pl.BlockSpec((LANES, T), lambda p, j: (p, 0)), kv_out, kv_out, kv_out],
        out_shape=[_sds((N_PAIRS * LANES, T), F32), _sds((T, N_PAIRS * LANES), CDT), _sds((T, N_PAIRS * LANES), CDT),
                   _sds((T, N_PAIRS * LANES), F32)],
        name=name, compiler_params=_cp(("parallel", "arbitrary")))(qkv, dcat, qkv, qkv, fcol, frow, lse, delta)


def _mixer_fwd(x, wts, tag, dep=None):
    T = x.shape[0]
    h = _rms_fwd(x, wts["mix_norm"], f"{tag}_norm", dep)
    qkv = _mm([(h, wts["win_qkv"], False)], out_dtype=CDT, tm=1024, tn=768, name=f"{tag}_in_qkv")
    rest = _mm([(h, wts["win_rest"], False)], tm=1024, name=f"{tag}_in_rest")
    fcol, frow8 = _fgate_fwd(rest, wts["fbias"], f"{tag}_fgate")
    frow = frow8.reshape(N_PAIRS, 2, T)
    ya = _pool_fwd(rest, wts["pool_wbd"], wts["pool_scale"], f"{tag}_pool")
    o, lse = _attn_fwd(qkv, fcol, frow, f"{tag}_attn")
    yc, yconv = _conv_fwd(rest, wts["conv_w"], wts["conv_b"], wts["conv_ln_g"], wts["conv_ln_b"], f"{tag}_conv")
    cat = jnp.concatenate([ya, o.astype(CDT), yc], axis=1)
    y = _mm([(cat, wts["w_out"], False)], res=x, name=f"{tag}_out")
    return y, (x, h, qkv, rest, fcol, frow, o, lse, yconv, cat)


def _mixer_bwd(saved, wts, gout, tag, dep=None):
    x, h, qkv, rest, fcol, frow, o, lse, yconv, cat = saved
    T = x.shape[0]
    dcat = _mm([(gout, wts["w_out"], True)], name=f"{tag}_dcat", dep=dep)
    dwout = _mm_tn(cat, gout, name=f"{tag}_dwout")
    du, dpw, dpsc = _pool_bwd(dcat, rest, wts["pool_wbd"], wts["pool_scale"], f"{tag}_pool_bwd")
    delta = _attn_delta(dcat, o, f"{tag}_attn_delta").reshape(N_PAIRS, 2, T)
    dqt, dk, dv, dfk = _attn_bwd(qkv, dcat, fcol, frow, lse, delta, f"{tag}_attn_bwd")
    dq = dqt.T.astype(CDT)
    dz, dfb = _fgate_bwd(dfk, rest, wts["fbias"], f"{tag}_fgate_bwd")
    da, dg, dcw, dcb, dlg, dlb = _conv_bwd(dcat, yconv, rest, wts["conv_w"], wts["conv_ln_g"], wts["conv_ln_b"],
                                           f"{tag}_conv_bwd")
    dp_qkv = jnp.concatenate([dq, dk, dv], axis=1).astype(CDT)
    dp_rest = jnp.concatenate([du, da, dg, dz], axis=1)
    dwin_qkv = _mm_tn(h, dp_qkv, name=f"{tag}_dwin_qkv")
    dwin_rest = _mm_tn(h, dp_rest, name=f"{tag}_dwin_rest")
    gin, dgamma = _mm_norm_bwd([(dp_qkv, wts["win_qkv"], True), (dp_rest, wts["win_rest"], True)], x, wts["mix_norm"],
                               gout, name=f"{tag}_dh_norm_bwd")
    dwin = _split_win(dwin_qkv, dwin_rest, f"{tag}_dwin_split")
    dpool_w = jnp.stack([dpw[64 * g:64 * g + 64, 64 * g:64 * g + 64] for g in range(4)])
    grads = dict(mix_norm=dgamma[0], w_in=dwin, pool_w=dpool_w, pool_scale=dpsc[0], forget_bias=dfb[0, 0:N_HEADS],
                 conv_w=dcw[0:CONV_K], conv_b=dcb[0], conv_ln_g=dlg[0], conv_ln_b=dlb[0], w_out=dwout)
    return gin, grads


def _rep_layer(rep, l):
    pw = rep["pool_w"][l].astype(CDT)
    wbd = jnp.zeros((256, 256), CDT)
    for g in range(4):
        wbd = lax.dynamic_update_slice(wbd, pw[g], (64 * g, 64 * g))
    return dict(
        ffn1_norm=rep["ffn1_norm"][l][None], ffn2_norm=rep["ffn2_norm"][l][None], mix_norm=rep["mix_norm"][l][None],
        fbias=jnp.pad(rep["forget_bias"][l], (0, LANES - N_HEADS))[None],
        pool_wbd=wbd, pool_scale=rep["pool_scale"][l][None], conv_b=rep["conv_b"][l][None],
        conv_ln_g=rep["conv_ln_g"][l][None], conv_ln_b=rep["conv_ln_b"][l][None])


def _local_step(x, target, rep, weights_for, grads_ready):
    depth = rep["ffn1_norm"].shape[0]
    kept = []
    for l in range(depth):
        r = _rep_layer(rep, l)
        w1, dep = weights_for(l, "ffn1", x)
        x, s1 = _ffn_fwd(x, r["ffn1_norm"], w1["w_gate"], w1["w_up"], w1["w_down"], f"l{l}_ffn1", dep)
        wm, dep = weights_for(l, "mix", x)
        wm = dict(r, win_qkv=wm["win_qkv"], win_rest=wm["win_rest"], w_out=wm["w_out"],
                  conv_w=jnp.pad(wm["conv_w"], ((0, CONV_HALO - CONV_K), (0, 0))))
        x, s2 = _mixer_fwd(x, wm, f"l{l}_mix", dep)
        w2, dep = weights_for(l, "ffn2", x)
        x, s3 = _ffn_fwd(x, r["ffn2_norm"], w2["w_gate"], w2["w_up"], w2["w_down"], f"l{l}_ffn2", dep)
        kept.append((r, w1, wm, w2, s1, s2, s3))
    loss, g, dfinal = _loss_bwd(x, rep["final_norm"][None], target, "loss_head")
    dep = grads_ready(None, "final", dict(final_norm=dfinal[0]))
    for l in reversed(range(depth)):
        r, w1, wm, w2, s1, s2, s3 = kept[l]

        def ffn_grads(which, l=l):
            return lambda gr: grads_ready(l, which, {f"{which}_{k}": v for k, v in gr.items()})

        g, dn = _ffn_bwd(s3, r["ffn2_norm"], w2["w_gate"], w2["w_up"], w2["w_down"], g, f"l{l}_ffn2", dep, ffn_grads("ffn2"))
        grads_ready(l, "norm", dict(ffn2_norm=dn[0]))
        g, gm = _mixer_bwd(s2, wm, g, f"l{l}_mix")
        dep = grads_ready(l, "mix", gm)
        g, dn = _ffn_bwd(s1, r["ffn1_norm"], w1["w_gate"], w1["w_up"], w1["w_down"], g, f"l{l}_ffn1", dep, ffn_grads("ffn1"))
        dep = grads_ready(l, "norm", dict(ffn1_norm=dn[0]))
    return loss, g


def _mesh_pos():
    return lax.axis_index("x"), lax.axis_index("y"), lax.axis_index("c")


def _dev_block(ref, dev, by_rows):
    if by_rows:
        r = ref.shape[1] // N_DEV
        return ref.at[:, pl.ds(dev * r, r), :]
    return ref.at[dev]


def _all_gather(shards, by_rows, name):
    n_arr = len(shards)
    out_shape = [_sds((s.shape[0], N_DEV * s.shape[1], s.shape[2]) if br else (N_DEV,) + s.shape, s.dtype)
                 for s, br in zip(shards, by_rows)]

    def body(*refs):
        xs, outs = refs[:n_arr], refs[n_arr:2 * n_arr]
        send_sems, recv_sems, local_sems = refs[2 * n_arr:]
        x, y, c = _mesh_pos()
        me, sibling = (x, y, c), (x, y, 1 - c)
        chips = [(1 - x, y), (x, 1 - y), (1 - x, 1 - y)]

        def rows(a, px, py, pc):
            return _dev_block(outs[a], 4 * px + 2 * py + pc, by_rows[a])

        def copy(k, a, block, to, src=None):
            return pltpu.make_async_remote_copy(
                src_ref=rows(a, *block) if src is None else src, dst_ref=rows(a, *block),
                send_sem=send_sems.at[k, a], recv_sem=recv_sems.at[k, a],
                device_id=to, device_id_type=pl.DeviceIdType.MESH)

        arrs = range(n_arr)
        mine = [pltpu.make_async_copy(xs[a], rows(a, *me), local_sems.at[a]) for a in arrs]
        for cp in mine:
            cp.start()
        first = [copy(0, a, me, sibling, src=xs[a]) for a in arrs]
        first += [copy(1 + j, a, me, (*chip, c), src=xs[a]) for j, chip in enumerate(chips) for a in arrs]
        for cp in first:
            cp.start()
        passed = []
        for j, chip in enumerate(chips):
            for a in arrs:
                copy(1 + j, a, (*chip, c), me).wait_recv()
                passed.append(copy(4 + j, a, (*chip, c), sibling))
                passed[-1].start()
        for a in arrs:
            copy(0, a, sibling, me).wait_recv()
        for j, chip in enumerate(chips):
            for a in arrs:
                copy(4 + j, a, (*chip, 1 - c), me).wait_recv()
        for cp in first + passed:
            cp.wait_send()
        for cp in mine:
            cp.wait()

    hbm = pl.BlockSpec(memory_space=pl.ANY)
    return pl.pallas_call(
        body, out_shape=out_shape, in_specs=[hbm] * n_arr, out_specs=[hbm] * n_arr,
        scratch_shapes=[pltpu.SemaphoreType.DMA((7, n_arr)), pltpu.SemaphoreType.DMA((7, n_arr)),
                        pltpu.SemaphoreType.DMA((n_arr,))],
        name=name)(*shards)


def _exchange(parts, by_rows, name):
    n_arr = len(parts)
    out_shape = [_sds((N_DEV, p.shape[0], p.shape[1] // N_DEV, p.shape[2]) if br else p.shape, p.dtype)
                 for p, br in zip(parts, by_rows)]

    def body(*refs):
        ps, outs = refs[:n_arr], refs[n_arr:2 * n_arr]
        send_sems, recv_sems, local_sems = refs[2 * n_arr:]
        x, y, c = _mesh_pos()
        my = 4 * x + 2 * y + c
        arrs = range(n_arr)
        mine = [pltpu.make_async_copy(_dev_block(ps[a], my, by_rows[a]), outs[a].at[my], local_sems.at[a]) for a in arrs]
        for cp in mine:
            cp.start()
        copies = []
        for k in range(1, N_DEV):
            px, py, pc = x ^ (k >> 2), y ^ ((k >> 1) & 1), c ^ (k & 1)
            for a in arrs:
                copies.append(pltpu.make_async_remote_copy(
                    src_ref=_dev_block(ps[a], 4 * px + 2 * py + pc, by_rows[a]), dst_ref=outs[a].at[my],
                    send_sem=send_sems.at[k - 1, a], recv_sem=recv_sems.at[k - 1, a],
                    device_id=(px, py, pc), device_id_type=pl.DeviceIdType.MESH))
        for cp in copies:
            cp.start()
        for cp in copies:
            cp.wait()
        for cp in mine:
            cp.wait()

    hbm = pl.BlockSpec(memory_space=pl.ANY)
    return pl.pallas_call(
        body, out_shape=out_shape, in_specs=[hbm] * n_arr, out_specs=[hbm] * n_arr,
        scratch_shapes=[pltpu.SemaphoreType.DMA((7, n_arr)), pltpu.SemaphoreType.DMA((7, n_arr)),
                        pltpu.SemaphoreType.DMA((n_arr,))],
        name=name)(*parts)


def _peer_copies(srcs, lands, send_sems, recv_sems, gather, by_rows):
    n_arr = len(srcs)
    x, y, c = _mesh_pos()
    my = 4 * x + 2 * y + c
    out = []
    for k in range(1, N_DEV):
        px, py, pc = x ^ (k >> 2), y ^ ((k >> 1) & 1), c ^ (k & 1)
        peer = 4 * px + 2 * py + pc
        for a in range(n_arr):
            src = srcs[a] if gather else _dev_block(srcs[a], peer, by_rows[a])
            dst = _dev_block(lands[a], my, by_rows[a]) if gather else lands[a].at[my]
            out.append(pltpu.make_async_remote_copy(
                src_ref=src, dst_ref=dst, send_sem=send_sems.at[(k - 1) * n_arr + a],
                recv_sem=recv_sems.at[(k - 1) * n_arr + a], device_id=(px, py, pc), device_id_type=pl.DeviceIdType.MESH))
    return out


def _land_shape(s, gather, by_rows):
    if gather:
        return (s.shape[0], N_DEV * s.shape[1], s.shape[2]) if by_rows else (N_DEV,) + s.shape
    return (N_DEV, s.shape[0], s.shape[1] // N_DEV, s.shape[2]) if by_rows else s.shape


_HBM = pl.BlockSpec(memory_space=pltpu.HBM)
_SEM = pl.BlockSpec(memory_space=pltpu.SEMAPHORE)


def _own_blocks(srcs, gather, by_rows, name):
    n = len(srcs)
    lands = [lax.empty(_land_shape(s, gather, br), s.dtype) for s, br in zip(srcs, by_rows)]

    def body(*refs):
        x, y, c = _mesh_pos()
        my = 4 * x + 2 * y + c
        sems = refs[-1]
        copies = []
        for a in range(n):
            src, land = refs[a], refs[2 * n + a]
            if gather:
                copies.append(pltpu.make_async_copy(src, _dev_block(land, my, by_rows[a]), sems.at[a]))
            else:
                copies.append(pltpu.make_async_copy(_dev_block(src, my, by_rows[a]), land.at[my], sems.at[a]))
        for cp in copies:
            cp.start()
        for cp in copies:
            cp.wait()

    hbm = pl.BlockSpec(memory_space=pl.ANY)
    out = pl.pallas_call(
        body, name=name, out_shape=[_sds(a.shape, a.dtype) for a in lands], in_specs=[hbm] * (2 * n),
        out_specs=[hbm] * n, input_output_aliases={n + a: a for a in range(n)},
        scratch_shapes=[pltpu.SemaphoreType.DMA((n,))])(*srcs, *lands)
    return list(out)


def _xfer_start(srcs, gather, by_rows, name, dep=None):
    n = len(srcs)
    lands = _own_blocks(srcs, gather, by_rows, name + "_own")
    ins = [pltpu.with_memory_space_constraint(a, pltpu.HBM) for a in list(srcs) + lands]
    dspec, darg = _dep(dep)

    def body(*refs):
        s = 2 * n + len(darg)
        for cp in _peer_copies(refs[:n], refs[n:2 * n], refs[s], refs[s + 1], gather, by_rows):
            cp.start()
        refs[-1][...] = jnp.zeros_like(refs[-1])

    sems = pltpu.SemaphoreType.DMA(((N_DEV - 1) * n,))
    outs = pl.pallas_call(
        body, name=name,
        out_shape=(sems, sems, *[pltpu.HBM(a.shape, a.dtype) for a in ins], _sds((8, LANES), F32)),
        in_specs=[_HBM] * (2 * n) + dspec,
        out_specs=(_SEM, _SEM, *[_HBM] * (2 * n), pl.BlockSpec(memory_space=pltpu.VMEM)),
        input_output_aliases={i: 2 + i for i in range(2 * n)},
        compiler_params=pltpu.CompilerParams(has_side_effects=pltpu.SideEffectType.DATAFLOW_SIDE_EFFECTING))(*ins, *darg)
    return outs[0], outs[1], list(outs[2:-1]), outs[-1]


def _xfer_wait(started, after, gather, by_rows, name):
    send_sems, recv_sems, bufs, _ = started
    n = len(bufs) // 2

    def body(*refs):
        for cp in _peer_copies(refs[:n], refs[n:2 * n], refs[2 * n], refs[2 * n + 1], gather, by_rows):
            cp.wait_send()
            cp.wait_recv()

    outs = pl.pallas_call(
        body, name=name, out_shape=tuple(pltpu.HBM(a.shape, a.dtype) for a in bufs),
        in_specs=[_HBM] * (2 * n) + [_SEM, _SEM, pl.BlockSpec(memory_space=pl.ANY)], out_specs=tuple([_HBM] * (2 * n)),
        input_output_aliases={i: i for i in range(2 * n)},
        compiler_params=pltpu.CompilerParams(has_side_effects=pltpu.SideEffectType.DATAFLOW_SIDE_EFFECTING))(
            *bufs, send_sems, recv_sems, after)
    return list(outs[n:])


def _adam_update(g, w, m, v):
    c1 = 1.0 - ADAM_B1 ** ADAM_STEP
    c2 = 1.0 - ADAM_B2 ** ADAM_STEP
    nm = ADAM_B1 * m + (1.0 - ADAM_B1) * g
    nv = ADAM_B2 * v + (1.0 - ADAM_B2) * (g * g)
    return -ADAM_LR * ((nm / c1) / (jnp.sqrt(nv / c2) + ADAM_EPS) + ADAM_WD * w), nm, nv


def _adamw_body(p_ref, w_ref, m_ref, v_ref, g_ref, d_ref, nm_ref, nv_ref):
    g = p_ref[0]
    for i in range(1, N_DEV):
        g = g + p_ref[i]
    g_ref[...] = g
    d_ref[...], nm_ref[...], nv_ref[...] = _adam_update(g, w_ref[...], m_ref[...], v_ref[...])


def _adamw(parts, w, m, v, name, tr=1536):
    R = w.shape[0]
    tr = max(t for t in range(8, tr + 1, 8) if R % t == 0)

    def body(*refs):
        _adamw_body(*refs)

    row = pl.BlockSpec((tr, LANES), lambda i: (i, 0))
    return pl.pallas_call(
        body, grid=(R // tr,),
        in_specs=[pl.BlockSpec((N_DEV, tr, LANES), lambda i: (0, i, 0)), row, row, row],
        out_specs=[row, row, row, row], out_shape=[_sds((R, LANES), F32)] * 4,
        name=name, compiler_params=_cp(("parallel",)))(parts, w, m, v)


def _adamw_split(recvs, w, m, v, name, tr):
    depth, r, c = w.shape
    assert depth == len(recvs)
    tr = _tile(r, tr)

    def body(*refs):
        layer = pl.program_id(0)
        for ll in range(depth):
            @pl.when(layer == ll)
            def _(ll=ll):
                _adamw_body(refs[ll], *refs[depth:])

    wspec = pl.BlockSpec((None, tr, c), lambda l, i: (l, i, 0))
    rspecs = [pl.BlockSpec((N_DEV, None, tr, c), lambda l, i, ll=ll, t=t: (0, t, jnp.where(l == ll, i, 0), 0))
              for ll, (_, t) in enumerate(recvs)]
    return pl.pallas_call(
        body, grid=(depth, r // tr), in_specs=rspecs + [wspec, wspec, wspec],
        out_specs=[wspec] * 4, out_shape=[_sds(w.shape, F32)] * 4,
        name=name, compiler_params=_cp(("arbitrary", "arbitrary")))(*[a for a, _ in recvs], w, m, v)


def _merge_win(g, name, tr=256):
    _, nt, K, n = g.shape
    tr = _tile(K, tr)

    def body(g_ref, q_ref, r_ref):
        full = jnp.concatenate([g_ref[j] for j in range(N_DEV)], axis=1)
        q_ref[...] = full[:, 256:1792]
        zpad = jnp.zeros((tr, REST_W - 776), full.dtype)
        r_ref[...] = jnp.concatenate([full[:, 0:256], full[:, 1800:2312], full[:, 1792:1800], zpad], axis=1)

    return pl.pallas_call(
        body, grid=(nt, K // tr),
        in_specs=[pl.BlockSpec((N_DEV, None, tr, n), lambda t, i: (0, t, i, 0))],
        out_specs=[pl.BlockSpec((None, tr, 1536), lambda t, i: (t, i, 0)), pl.BlockSpec((None, tr, REST_W), lambda t, i: (t, i, 0))],
        out_shape=[_sds((nt, K, 1536), g.dtype), _sds((nt, K, REST_W), g.dtype)],
        name=name, compiler_params=_cp(("parallel", "parallel")))(g)


def _split_win(dq, dr, name, tr=256):
    K = dq.shape[0]
    tr = _tile(K, tr)
    n = (dq.shape[1] + 776) // N_DEV

    def body(q_ref, r_ref, o_ref):
        r = r_ref[...]
        full = jnp.concatenate([r[:, 0:256], q_ref[...], r[:, 768:776], r[:, 256:768]], axis=1)
        for j in range(N_DEV):
            o_ref[j] = full[:, n * j:n * (j + 1)]

    return pl.pallas_call(
        body, grid=(K // tr,),
        in_specs=[pl.BlockSpec((tr, dq.shape[1]), lambda i: (i, 0)), pl.BlockSpec((tr, REST_W), lambda i: (i, 0))],
        out_specs=pl.BlockSpec((N_DEV, tr, n), lambda i: (0, i, 0)),
        out_shape=_sds((N_DEV, K, n), F32), name=name, compiler_params=_cp(("parallel",)))(dq, dr)


WEIGHTS = ["ffn1_norm", "ffn1_w_gate", "ffn1_w_up", "ffn1_w_down", "mix_norm", "w_in", "pool_w", "pool_scale",
           "forget_bias", "conv_w", "conv_b", "conv_ln_g", "conv_ln_b", "w_out", "ffn2_norm", "ffn2_w_gate",
           "ffn2_w_up", "ffn2_w_down", "final_norm"]
FFN_PARTS = ("w_gate", "w_up", "w_down")
FFN_T = ["ffn1_w_gate", "ffn1_w_up", "ffn2_w_gate", "ffn2_w_up"]
BIG = FFN_T + ["ffn1_w_down", "ffn2_w_down", "w_in", "w_out"]
SMALL = [n for n in WEIGHTS if n not in BIG]


def _padded(n):
    return -(-n // PACK_ALIGN) * PACK_ALIGN


def _flat_pad(a):
    f = a.reshape(-1)
    return jnp.pad(f, (0, _padded(f.shape[0]) - f.shape[0]))


def _split8(a, axis):
    shp = a.shape
    a = a.reshape(shp[:axis] + (N_DEV, shp[axis] // N_DEV) + shp[axis + 1:])
    return jnp.moveaxis(a, axis, 0)


def _merge8(a, axis):
    a = jnp.moveaxis(a, 0, axis)
    shp = a.shape
    return a.reshape(shp[:axis] + (shp[axis] * shp[axis + 1],) + shp[axis + 2:])


def _pack_small(arrs):
    return jnp.concatenate([_flat_pad(arrs[n]) for n in SMALL]).reshape(-1, LANES)


def _pack_small_parts(grads):
    cols = []
    for n in SMALL:
        g = grads[n]
        if n == "conv_w":
            s = _split8(g, 2).reshape(N_DEV, -1)
        else:
            s = jnp.broadcast_to(g.reshape(1, -1), (N_DEV, g.size))
        cols.append(jnp.pad(s, ((0, 0), (0, _padded(s.shape[1]) - s.shape[1]))))
    return jnp.concatenate(cols, axis=1).reshape(N_DEV, -1, LANES)


def _unpack_small(buf, like):
    flat = buf.reshape(-1)
    out, off = {}, 0
    for n in SMALL:
        size = like[n].size
        out[n] = flat[off:off + size].reshape(like[n].shape)
        off += _padded(size)
    return out


class _Comm:
    def __init__(self, w):
        self.w = w
        self.bf = {n: (jnp.swapaxes(w[n], 1, 2) if n in FFN_T else w[n]).astype(CDT) for n in BIG}
        self.ready = {}
        self.grads = {}

    def _ffn_shards(self, l, which):
        return jnp.stack([self.bf[f"{which}_{k}"][l] for k in FFN_PARTS])

    def _put_ffn(self, l, which, rows, t):
        self.ready[(l, which)] = dict(w_gate=rows[t], w_up=rows[t + 1], w_down=rows[t + 2])

    def weights_for(self, l, stage, x):
        bf = self.bf
        dep = None
        if (l, stage) == (0, "ffn1"):
            gd, = _all_gather([self._ffn_shards(0, "ffn1")], [True], "gather_l0_ffn1")
            self._put_ffn(0, "ffn1", gd, 0)
            self.started = _xfer_start([bf["w_in"][0:1], bf["w_out"][0:1], self.w["conv_w"]], True,
                                       [False, True, False], "gather_mix0_start", dep=gd)
            dep = self.started[3]
        elif (l, stage) == (0, "mix"):
            gi, go, gc = _xfer_wait(self.started, x, True, [False, True, False], "gather_mix0_wait")
            q, r = _merge_win(gi, "merge_l0_w_in")
            self.conv_w = _merge8(gc, 2)
            self.ready[(0, "mix")] = dict(win_qkv=q[0], win_rest=r[0], w_out=go[0], conv_w=self.conv_w[0])
            rows = jnp.concatenate([self._ffn_shards(0, "ffn2"), self._ffn_shards(1, "ffn1"), self._ffn_shards(1, "ffn2")])
            self.started = _xfer_start([rows, bf["w_in"][1:2], bf["w_out"][1:2]], True, [True, False, True],
                                       "gather_rest_start")
            dep = self.started[3]
        elif (l, stage) == (0, "ffn2"):
            gd, gi, go = _xfer_wait(self.started, x, True, [True, False, True], "gather_rest_wait")
            self._put_ffn(0, "ffn2", gd, 0)
            self._put_ffn(1, "ffn1", gd, 3)
            self._put_ffn(1, "ffn2", gd, 6)
            q, r = _merge_win(gi, "merge_l1_w_in")
            self.ready[(1, "mix")] = dict(win_qkv=q[0], win_rest=r[0], w_out=go[0], conv_w=self.conv_w[1])
        return self.ready[(l, stage)], dep

    def grads_ready(self, l, stage, grads):
        for n, v in grads.items():
            self.grads[(l, n)] = v
        gr = self.grads

        def ffn_rows(layer, which, parts=FFN_PARTS):
            return [gr[(layer, f"{which}_{k}")] for k in parts]

        if l == 1 and "ffn1_w_gate" in grads:
            self.sent1 = _xfer_start(
                [jnp.stack(ffn_rows(1, "ffn1") + ffn_rows(1, "ffn2")), gr[(1, "w_in")][:, None], gr[(1, "w_out")][None]],
                False, [True, False, True], "grads_l1_start")
            return self.sent1[3]
        if l == 0 and "ffn2_w_gate" in grads:
            self.sent_ffn2 = _xfer_start([jnp.stack(ffn_rows(0, "ffn2"))], False, [True], "grads_l0_ffn2_start")
            return self.sent_ffn2[3]
        if (l, stage) == (0, "mix"):
            self.sent_mix = _xfer_start([gr[(0, "w_in")][:, None], gr[(0, "w_out")][None]], False, [False, True],
                                        "grads_l0_mix_start")
            return self.sent_mix[3]
        if l == 0 and "ffn1_w_down" in grads:
            self.sent_down = _xfer_start([gr[(0, "ffn1_w_down")][None]], False, [True], "grads_l0_ffn1_down_start")
            return self.sent_down[3]
        if l == 0 and "ffn1_w_gate" in grads:
            self.sent_gu = _xfer_start([jnp.stack(ffn_rows(0, "ffn1", FFN_PARTS[:2]))], False, [True],
                                       "grads_l0_ffn1_gate_up_start")
            return self.sent_gu[3]
        return None

    def finish(self, m, v, after):
        w, gr = self.w, self.grads
        depth = range(w["w_in"].shape[0])
        small = {n: (gr[(None, n)] if n == "final_norm" else jnp.stack([gr[(l, n)] for l in depth])) for n in SMALL}
        r1, i1, o1 = _xfer_wait(self.sent1, after, False, [True, False, True], "grads_l1_wait")
        r2, = _xfer_wait(self.sent_ffn2, after, False, [True], "grads_l0_ffn2_wait")
        i0, o0 = _xfer_wait(self.sent_mix, after, False, [False, True], "grads_l0_mix_wait")

        def adam(n, recvs, tr):
            if n in FFN_T:
                out = _adamw_split(recvs, *[jnp.swapaxes(t[n], 1, 2) for t in (w, m, v)], f"adamw_{n}", tr)
                return [jnp.swapaxes(o, 1, 2) for o in out]
            return _adamw_split(recvs, w[n], m[n], v[n], f"adamw_{n}", tr)

        res = {}
        for t, k in enumerate(FFN_PARTS):
            res[f"ffn2_{k}"] = adam(f"ffn2_{k}", [(r2, t), (r1, 3 + t)], 176)
        res["w_in"] = adam("w_in", [(i0, 0), (i1, 0)], 256)
        res["w_out"] = adam("w_out", [(o0, 0), (o1, 0)], 128)
        rs, = _exchange([_pack_small_parts(small)], [False], "exchange_small")
        r0, = _xfer_wait(self.sent_down, res["w_out"][0], False, [True], "grads_l0_ffn1_down_wait")
        res["ffn1_w_down"] = adam("ffn1_w_down", [(r0, 0), (r1, 2)], 176)
        g0, = _xfer_wait(self.sent_gu, res["ffn1_w_down"][0], False, [True], "grads_l0_ffn1_gate_up_wait")
        res["ffn1_w_gate"] = adam("ffn1_w_gate", [(g0, 0), (r1, 0)], 176)
        res["ffn1_w_up"] = adam("ffn1_w_up", [(g0, 1), (r1, 1)], 176)
        packed = _adamw(rs, _pack_small(w), _pack_small(m), _pack_small(v), "adamw_small")
        unpacked = [_unpack_small(b, w) for b in packed]
        for n in SMALL:
            res[n] = [u[n] for u in unpacked]
        return res


def kernel(x, ffn1_norm, ffn1_w_gate, ffn1_w_up, ffn1_w_down, mix_norm, w_in, pool_w, pool_scale, forget_bias, conv_w, conv_b, conv_ln_g, conv_ln_b, w_out, ffn2_norm, ffn2_w_gate, ffn2_w_up, ffn2_w_down, final_norm, loss_target, m_ffn1_norm, m_ffn1_w_gate, m_ffn1_w_up, m_ffn1_w_down, m_mix_norm, m_w_in, m_pool_w, m_pool_scale, m_forget_bias, m_conv_w, m_conv_b, m_conv_ln_g, m_conv_ln_b, m_w_out, m_ffn2_norm, m_ffn2_w_gate, m_ffn2_w_up, m_ffn2_w_down, m_final_norm, v_ffn1_norm, v_ffn1_w_gate, v_ffn1_w_up, v_ffn1_w_down, v_mix_norm, v_w_in, v_pool_w, v_pool_scale, v_forget_bias, v_conv_w, v_conv_b, v_conv_ln_g, v_conv_ln_b, v_w_out, v_ffn2_norm, v_ffn2_w_gate, v_ffn2_w_up, v_ffn2_w_down, v_final_norm):
    w = dict(zip(WEIGHTS, (ffn1_norm, ffn1_w_gate, ffn1_w_up, ffn1_w_down, mix_norm, w_in, pool_w, pool_scale, forget_bias,
                           conv_w, conv_b, conv_ln_g, conv_ln_b, w_out, ffn2_norm, ffn2_w_gate, ffn2_w_up, ffn2_w_down,
                           final_norm)))
    m = dict(zip(WEIGHTS, (m_ffn1_norm, m_ffn1_w_gate, m_ffn1_w_up, m_ffn1_w_down, m_mix_norm, m_w_in, m_pool_w, m_pool_scale,
                           m_forget_bias, m_conv_w, m_conv_b, m_conv_ln_g, m_conv_ln_b, m_w_out, m_ffn2_norm, m_ffn2_w_gate,
                           m_ffn2_w_up, m_ffn2_w_down, m_final_norm)))
    v = dict(zip(WEIGHTS, (v_ffn1_norm, v_ffn1_w_gate, v_ffn1_w_up, v_ffn1_w_down, v_mix_norm, v_w_in, v_pool_w, v_pool_scale,
                           v_forget_bias, v_conv_w, v_conv_b, v_conv_ln_g, v_conv_ln_b, v_w_out, v_ffn2_norm, v_ffn2_w_gate,
                           v_ffn2_w_up, v_ffn2_w_down, v_final_norm)))
    comm = _Comm(w)
    loss_row, gx = _local_step(x[0], loss_target[0], w, comm.weights_for, comm.grads_ready)
    loss = lax.psum(loss_row[0, 0], ("x", "y", "c"))
    res = comm.finish(m, v, gx)
    return (loss, gx[None], *[res[n][i] for i in range(4) for n in WEIGHTS])
```

```python
import math

import numpy as np
import jax
import jax.numpy as jnp
from jax import lax
from jax.experimental import pallas as pl
from jax.experimental.pallas import tpu as pltpu

F32 = jnp.float32
CDT = jnp.bfloat16
NORM_EPS = 1e-6
N_DEV = 8
LANES = 128
PACK_ALIGN = 8 * LANES
VMEM_LIMIT = 48 * 1024 * 1024

POOL_WINDOWS = (2, 4, 8, 16)
POOL_HALO = 16
CONV_K = 31
CONV_HALO = 32
HEAD_DIM = 64
N_HEADS = 8
N_PAIRS = N_HEADS // 2
ATT_SCALE = 1.0 / math.sqrt(HEAD_DIM)
NEG = -1e30

ADAM_LR, ADAM_B1, ADAM_B2, ADAM_EPS, ADAM_WD, ADAM_STEP = 0.001, 0.9, 0.999, 1e-08, 0.01, 10

REST_W = 896
REST_Z_BLK = 6


def _cp(sem):
    return pltpu.CompilerParams(dimension_semantics=sem, vmem_limit_bytes=VMEM_LIMIT)


def _tile(n, pref):
    t = min(n, pref)
    assert n % t == 0, (n, pref)
    return t


def _sigmoid(x):
    return 1.0 / (1.0 + jnp.exp(-x))


def _sds(shape, dtype):
    return jax.ShapeDtypeStruct(shape, dtype)


_ANY = pl.BlockSpec(memory_space=pl.ANY)


def _dep(dep):
    return ([], []) if dep is None else ([_ANY], [dep])


def _wshape(w):
    return w[0].shape[1:] if isinstance(w, tuple) else w.shape


def _wspec(w, block, index_map):
    if not isinstance(w, tuple):
        return w, pl.BlockSpec(block, index_map)
    arr, t = w
    return arr, pl.BlockSpec((None,) + block, lambda *g: (t,) + index_map(*g))


def _rms_fwd(x, g, name, dep=None):
    T, D = x.shape
    tm = _tile(T, 1024)

    def body(x_ref, g_ref, *rest):
        o_ref = rest[-1]
        xv = x_ref[...]
        r = lax.rsqrt(jnp.mean(xv * xv, axis=-1, keepdims=True) + NORM_EPS)
        o_ref[...] = (xv * r * g_ref[...]).astype(o_ref.dtype)

    dspec, darg = _dep(dep)
    return pl.pallas_call(
        body, grid=(T // tm,),
        in_specs=[pl.BlockSpec((tm, D), lambda i: (i, 0)), pl.BlockSpec((1, D), lambda i: (0, 0))] + dspec,
        out_specs=pl.BlockSpec((tm, D), lambda i: (i, 0)),
        out_shape=_sds((T, D), CDT), name=name, compiler_params=_cp(("parallel",)))(x, g, *darg)


def _rms_bwd(x, g, dh, gres, name):
    T, D = x.shape
    tm = _tile(T, 512)

    def body(x_ref, g_ref, dh_ref, gres_ref, gin_ref, dg_ref):
        i = pl.program_id(0)
        xv = x_ref[...]
        d = dh_ref[...]
        r = lax.rsqrt(jnp.mean(xv * xv, axis=-1, keepdims=True) + NORM_EPS)
        xh = xv * r
        dxh = d * g_ref[...]
        c = jnp.mean(dxh * xh, axis=-1, keepdims=True)
        gin_ref[...] = gres_ref[...] + r * (dxh - xh * c)
        part = jnp.sum(d * xh, axis=0, keepdims=True)

        @pl.when(i == 0)
        def _():
            dg_ref[...] = part

        @pl.when(i > 0)
        def _():
            dg_ref[...] += part

    row = pl.BlockSpec((tm, D), lambda i: (i, 0))
    vec = pl.BlockSpec((1, D), lambda i: (0, 0))
    return pl.pallas_call(
        body, grid=(T // tm,), in_specs=[row, vec, row, row], out_specs=[row, vec],
        out_shape=[_sds((T, D), F32), _sds((1, D), F32)], name=name, compiler_params=_cp(("arbitrary",)))(x, g, dh, gres)


def _loss_bwd(x, g, target, name):
    T, D = x.shape
    tm = _tile(T, 512)

    def body(x_ref, g_ref, t_ref, loss_ref, dx_ref, dg_ref):
        i = pl.program_id(0)
        xv = x_ref[...]
        gv = g_ref[...]
        r = lax.rsqrt(jnp.mean(xv * xv, axis=-1, keepdims=True) + NORM_EPS)
        xh = xv * r
        err = xh * gv - t_ref[...]
        lpart = 0.5 * jnp.sum(jnp.mean(err * err, axis=-1, keepdims=True), axis=0, keepdims=True)
        dy = err * (1.0 / D)
        dxh = dy * gv
        c = jnp.mean(dxh * xh, axis=-1, keepdims=True)
        dx_ref[...] = r * (dxh - xh * c)
        part = jnp.sum(dy * xh, axis=0, keepdims=True)
        lrow = jnp.broadcast_to(lpart, (1, LANES))

        @pl.when(i == 0)
        def _():
            dg_ref[...] = part
            loss_ref[...] = lrow

        @pl.when(i > 0)
        def _():
            dg_ref[...] += part
            loss_ref[...] += lrow

    row = pl.BlockSpec((tm, D), lambda i: (i, 0))
    vec = pl.BlockSpec((1, D), lambda i: (0, 0))
    return pl.pallas_call(
        body, grid=(T // tm,), in_specs=[row, vec, row],
        out_specs=[pl.BlockSpec((1, LANES), lambda i: (0, 0)), row, vec],
        out_shape=[_sds((1, LANES), F32), _sds((T, D), F32), _sds((1, D), F32)],
        name=name, compiler_params=_cp(("arbitrary",)))(x, g, target)


def _mm(pairs, *, name, res=None, alpha=1.0, out_dtype=F32, tm=512, tn=None, dep=None):
    T = pairs[0][0].shape[0]
    N = _wshape(pairs[0][1])[0 if pairs[0][2] else 1]
    tm = _tile(T, tm)
    tn = N if tn is None else _tile(N, tn)
    flags = [p[2] for p in pairs]
    n_in = 2 * len(pairs)

    def body(*refs):
        o_ref = refs[-1]
        acc = None
        for p, bt in enumerate(flags):
            a = refs[2 * p][...].astype(CDT)
            b = refs[2 * p + 1][...]
            dims = (((1,), (1,)), ((), ())) if bt else (((1,), (0,)), ((), ()))
            d = lax.dot_general(a, b, dims, preferred_element_type=F32)
            acc = d if acc is None else acc + d
        if alpha != 1.0:
            acc = acc * alpha
        if res is not None:
            acc = refs[n_in][...] + acc
        o_ref[...] = acc.astype(o_ref.dtype)

    in_specs, args = [], []
    for a, b, bt in pairs:
        K = a.shape[1]
        in_specs.append(pl.BlockSpec((tm, K), lambda i, j: (i, 0)))
        b, bspec = _wspec(b, (tn, K), lambda i, j: (j, 0)) if bt else _wspec(b, (K, tn), lambda i, j: (0, j))
        in_specs.append(bspec)
        args += [a, b]
    if res is not None:
        in_specs.append(pl.BlockSpec((tm, tn), lambda i, j: (i, j)))
        args.append(res)
    dspec, darg = _dep(dep)
    in_specs += dspec
    args += darg
    return pl.pallas_call(
        body, grid=(T // tm, N // tn), in_specs=in_specs,
        out_specs=pl.BlockSpec((tm, tn), lambda i, j: (i, j)),
        out_shape=_sds((T, N), out_dtype), name=name, compiler_params=_cp(("parallel", "arbitrary")))(*args)


def _mm_norm_bwd(pairs, x, g, gres, *, name, tm=256, dep=None):
    T, D = x.shape
    tm = _tile(T, tm)
    n_in = 2 * len(pairs)
    flags = [p[2] for p in pairs]

    def body(*refs):
        x_ref, g_ref, gres_ref = refs[n_in:n_in + 3]
        gin_ref, dg_ref = refs[-2:]
        i = pl.program_id(0)
        d = None
        for p, bt in enumerate(flags):
            dims = (((1,), (1,)), ((), ())) if bt else (((1,), (0,)), ((), ()))
            part = lax.dot_general(refs[2 * p][...].astype(CDT), refs[2 * p + 1][...], dims, preferred_element_type=F32)
            d = part if d is None else d + part
        xv = x_ref[...]
        r = lax.rsqrt(jnp.mean(xv * xv, axis=-1, keepdims=True) + NORM_EPS)
        xh = xv * r
        dxh = d * g_ref[...]
        c = jnp.mean(dxh * xh, axis=-1, keepdims=True)
        gin_ref[...] = gres_ref[...] + r * (dxh - xh * c)
        part = jnp.sum(d * xh, axis=0, keepdims=True)

        @pl.when(i == 0)
        def _():
            dg_ref[...] = part

        @pl.when(i > 0)
        def _():
            dg_ref[...] += part

    in_specs, args = [], []
    for a, b, bt in pairs:
        K = a.shape[1]
        b, bspec = _wspec(b, tuple(_wshape(b)), lambda i: (0, 0))
        in_specs += [pl.BlockSpec((tm, K), lambda i: (i, 0)), bspec]
        args += [a, b]
    row = pl.BlockSpec((tm, D), lambda i: (i, 0))
    vec = pl.BlockSpec((1, D), lambda i: (0, 0))
    dspec, darg = _dep(dep)
    return pl.pallas_call(
        body, grid=(T // tm,), in_specs=in_specs + [row, vec, row] + dspec, out_specs=[row, vec],
        out_shape=[_sds((T, D), F32), _sds((1, D), F32)], name=name,
        compiler_params=_cp(("arbitrary",)))(*args, x, g, gres, *darg)


def _mm_tn(a, b, *, name, alpha=1.0, tk=2048, dep=None):
    T, M = a.shape
    N = b.shape[1]
    tm = M if M <= 1024 else M // 2
    tn = N if N <= 1536 else N // 2
    assert M % tm == 0 and N % tn == 0 and tm % LANES == 0 and tn % LANES == 0
    tk = _tile(T, tk)
    nk = T // tk

    def body(a_ref, b_ref, *rest):
        o_ref = rest[-1]
        k = pl.program_id(2)
        d = lax.dot_general(a_ref[...].astype(CDT), b_ref[...].astype(CDT), (((0,), (0,)), ((), ())),
                            preferred_element_type=F32)

        @pl.when(k == 0)
        def _():
            o_ref[...] = d

        @pl.when(k > 0)
        def _():
            o_ref[...] += d

        if alpha != 1.0:
            @pl.when(k == nk - 1)
            def _():
                o_ref[...] *= alpha

    dspec, darg = _dep(dep)
    return pl.pallas_call(
        body, grid=(M // tm, N // tn, nk),
        in_specs=[pl.BlockSpec((tk, tm), lambda i, j, k: (k, i)), pl.BlockSpec((tk, tn), lambda i, j, k: (k, j))] + dspec,
        out_specs=pl.BlockSpec((tm, tn), lambda i, j, k: (i, j)),
        out_shape=_sds((M, N), F32), name=name, compiler_params=_cp(("parallel", "parallel", "arbitrary")))(a, b, *darg)


def _ffn_up(h, wgt, wut, name):
    T, D = h.shape
    Fh = _wshape(wgt)[0]
    tm = _tile(T, 2048)
    tn = _tile(Fh, 256)
    nt = (((1,), (1,)), ((), ()))

    def body(h_ref, wg_ref, wu_ref, a_ref, b_ref, s_ref):
        hv = h_ref[...]
        a = lax.dot_general(hv, wg_ref[...], nt, preferred_element_type=F32)
        b = lax.dot_general(hv, wu_ref[...], nt, preferred_element_type=F32)
        a_ref[...] = a.astype(a_ref.dtype)
        b_ref[...] = b.astype(b_ref.dtype)
        s_ref[...] = (a * _sigmoid(a) * b).astype(s_ref.dtype)

    wgt, gspec = _wspec(wgt, (tn, D), lambda i, j: (j, 0))
    wut, uspec = _wspec(wut, (tn, D), lambda i, j: (j, 0))
    ospec = pl.BlockSpec((tm, tn), lambda i, j: (i, j))
    return pl.pallas_call(
        body, grid=(T // tm, Fh // tn),
        in_specs=[pl.BlockSpec((tm, D), lambda i, j: (i, 0)), gspec, uspec],
        out_specs=[ospec, ospec, ospec],
        out_shape=[_sds((T, Fh), CDT), _sds((T, Fh), CDT), _sds((T, Fh), CDT)],
        name=name, compiler_params=_cp(("parallel", "arbitrary")))(h, wgt, wut)


def _ffn_bwd_ds(gout, wd, a, b, name, dep=None):
    T, D = gout.shape
    Fh = _wshape(wd)[0]
    tm = _tile(T, 2048)
    tn = _tile(Fh, 256)

    def body(g_ref, wd_ref, a_ref, b_ref, *rest):
        da_ref, db_ref = rest[-2:]
        dy = (0.5 * g_ref[...]).astype(CDT)
        ds = lax.dot_general(dy, wd_ref[...], (((1,), (1,)), ((), ())), preferred_element_type=F32)
        av = a_ref[...].astype(F32)
        sg = _sigmoid(av)
        da_ref[...] = (ds * b_ref[...].astype(F32) * (sg * (1.0 + av * (1.0 - sg)))).astype(da_ref.dtype)
        db_ref[...] = (ds * (av * sg)).astype(db_ref.dtype)

    ospec = pl.BlockSpec((tm, tn), lambda i, j: (i, j))
    dspec, darg = _dep(dep)
    wd, wspec = _wspec(wd, (tn, D), lambda i, j: (j, 0))
    return pl.pallas_call(
        body, grid=(T // tm, Fh // tn),
        in_specs=[pl.BlockSpec((tm, D), lambda i, j: (i, 0)), wspec, ospec, ospec] + dspec,
        out_specs=[ospec, ospec],
        out_shape=[_sds((T, Fh), CDT), _sds((T, Fh), CDT)],
        name=name, compiler_params=_cp(("parallel", "arbitrary")))(gout, wd, a, b, *darg)


def _ffn_fwd(x, gamma, wgt, wut, wd, tag, dep=None):
    h = _rms_fwd(x, gamma, f"{tag}_norm", dep)
    a, b, s = _ffn_up(h, wgt, wut, f"{tag}_up")
    y = _mm([(s, wd, False)], res=x, alpha=0.5, name=f"{tag}_down")
    return y, (x, h, a, b, s)


def _ffn_bwd(saved, gamma, wgt, wut, wd, gout, tag, dep, on_grads):
    x, h, a, b, s = saved
    dwd = _mm_tn(s, gout, alpha=0.5, name=f"{tag}_dwd", dep=dep)
    da, db = _ffn_bwd_ds(gout, wd, a, b, f"{tag}_bwd_ds", on_grads(dict(w_down=dwd)))
    dwgt = _mm_tn(da, h, name=f"{tag}_dwg")
    dwut = _mm_tn(db, h, name=f"{tag}_dwu")
    dep = on_grads(dict(w_gate=dwgt, w_up=dwut))
    return _mm_norm_bwd([(da, wgt, False), (db, wut, False)], x, gamma, gout, name=f"{tag}_dh_norm_bwd", dep=dep)


def _fgate_fwd(rest, bias, name, bt=512):
    T = rest.shape[0]
    bt = _tile(T, bt)

    def body(z_ref, b_ref, fc_ref, ft_ref, carry):
        i = pl.program_id(0)

        @pl.when(i == 0)
        def _():
            carry[...] = jnp.zeros_like(carry)

        zb = z_ref[...] + b_ref[...]
        e = jnp.exp(-jnp.abs(zb))
        u = 1.0 + e
        log1p_e = jnp.where(u == 1.0, e, jnp.log(u) * (e / (u - 1.0)))
        x = jnp.minimum(zb, 0.0) - log1p_e
        row = lax.broadcasted_iota(jnp.int32, x.shape, 0)
        sh = 1
        while sh < bt:
            x = x + jnp.where(row >= sh, pltpu.roll(x, sh, 0), 0.0)
            sh *= 2
        f = x + carry[...]
        carry[...] = f[bt - 1:bt, :]
        fc_ref[...] = f
        ft_ref[...] = jnp.transpose(f)[0:N_HEADS, :]

    return pl.pallas_call(
        body, grid=(T // bt,),
        in_specs=[pl.BlockSpec((bt, LANES), lambda i: (i, REST_Z_BLK)), pl.BlockSpec((1, LANES), lambda i: (0, 0))],
        out_specs=[pl.BlockSpec((bt, LANES), lambda i: (i, 0)), pl.BlockSpec((N_HEADS, bt), lambda i: (0, i))],
        out_shape=[_sds((T, LANES), F32), _sds((N_HEADS, T), F32)],
        scratch_shapes=[pltpu.VMEM((1, LANES), F32)],
        name=name, compiler_params=_cp(("arbitrary",)))(rest, bias)


def _fgate_bwd(dfk, rest, bias, name, bt=512):
    T = rest.shape[0]
    bt = _tile(T, bt)
    nb = T // bt

    def body(df_ref, z_ref, b_ref, dz_ref, db_ref, carry):
        i = pl.program_id(0)

        @pl.when(i == 0)
        def _():
            carry[...] = jnp.zeros_like(carry)

        dfv = df_ref[...]
        lane = lax.broadcasted_iota(jnp.int32, (bt, LANES), 1)
        x = jnp.zeros((bt, LANES), F32)
        for h in range(N_HEADS):
            x = jnp.where(lane == h, dfv[:, HEAD_DIM * h:HEAD_DIM * h + 1], x)
        row = lax.broadcasted_iota(jnp.int32, x.shape, 0)
        sh = 1
        while sh < bt:
            x = x + jnp.where(row + sh < bt, pltpu.roll(x, bt - sh, 0), 0.0)
            sh *= 2
        dlf = x + carry[...]
        carry[...] = dlf[0:1, :]
        zb = z_ref[...] + b_ref[...]
        dz = jnp.where(lane < N_HEADS, dlf * _sigmoid(-zb), 0.0)
        dz_ref[...] = dz.astype(dz_ref.dtype)
        part = jnp.sum(dz, axis=0, keepdims=True)

        @pl.when(i == 0)
        def _():
            db_ref[...] = part

        @pl.when(i > 0)
        def _():
            db_ref[...] += part

    return pl.pallas_call(
        body, grid=(nb,),
        in_specs=[pl.BlockSpec((bt, 4 * LANES), lambda i: (nb - 1 - i, 0)),
                  pl.BlockSpec((bt, LANES), lambda i: (nb - 1 - i, REST_Z_BLK)),
                  pl.BlockSpec((1, LANES), lambda i: (0, 0))],
        out_specs=[pl.BlockSpec((bt, LANES), lambda i: (nb - 1 - i, 0)), pl.BlockSpec((1, LANES), lambda i: (0, 0))],
        out_shape=[_sds((T, LANES), CDT), _sds((1, LANES), F32)],
        scratch_shapes=[pltpu.VMEM((1, LANES), F32)],
        name=name, compiler_params=_cp(("arbitrary",)))(dfk, rest, bias)


def _by_group(vals, lane):
    out = vals[-1]
    for g in range(len(vals) - 2, -1, -1):
        out = jnp.where(lane // 64 == g, vals[g], out)
    return out


def _pool_counts(t0, n, lane):
    t = t0 + lax.broadcasted_iota(jnp.int32, (n, 256), 0)
    return _by_group([jnp.minimum(t + 1, w) for w in POOL_WINDOWS], lane).astype(F32)


def _pooled(u, halo, i, bt):
    lane = lax.broadcasted_iota(jnp.int32, (bt, 256), 1)
    ext = jnp.concatenate([jnp.where(i > 0, halo, 0.0), u], axis=0)
    sums, s, sh = [], ext, 1
    for _ in POOL_WINDOWS:
        s = s + pltpu.roll(s, sh, 0)
        sums.append(s[POOL_HALO:, :])
        sh *= 2
    return _by_group(sums, lane) / _pool_counts(i * bt, bt, lane) - u


def _pool_fwd(rest, wbd, scale, name, bt=512):
    T = rest.shape[0]
    bt = _tile(T, bt)
    hb = bt // POOL_HALO

    def body(u_ref, halo_ref, w_ref, sc_ref, o_ref):
        i = pl.program_id(0)
        pooled = _pooled(u_ref[...], halo_ref[...], i, bt)
        mixed = jnp.dot(pooled.astype(CDT), w_ref[...], preferred_element_type=F32)
        o_ref[...] = (mixed * sc_ref[...]).astype(o_ref.dtype)

    return pl.pallas_call(
        body, grid=(T // bt,),
        in_specs=[pl.BlockSpec((bt, 256), lambda i: (i, 0)),
                  pl.BlockSpec((POOL_HALO, 256), lambda i: (jnp.maximum(i * hb - 1, 0), 0)),
                  pl.BlockSpec((256, 256), lambda i: (0, 0)), pl.BlockSpec((1, 256), lambda i: (0, 0))],
        out_specs=pl.BlockSpec((bt, 256), lambda i: (i, 0)),
        out_shape=_sds((T, 256), CDT), name=name, compiler_params=_cp(("parallel",)))(rest, rest, wbd, scale)


def _pool_bwd(dcat, rest, wbd, scale, name, bt=512):
    T = rest.shape[0]
    bt = _tile(T, bt)
    hb = bt // POOL_HALO
    nb = T // bt
    n = bt + POOL_HALO

    def body(dy_ref, dyn_ref, u_ref, halo_ref, w_ref, sc_ref, du_ref, dw_ref, dsc_ref):
        i = pl.program_id(0)
        lane = lax.broadcasted_iota(jnp.int32, (bt, 256), 1)
        w = w_ref[...]
        sc = sc_ref[...]
        pooled = _pooled(u_ref[...], halo_ref[...], i, bt)
        pooled_c = pooled.astype(CDT)
        mixed = jnp.dot(pooled_c, w, preferred_element_type=F32)
        dy = dy_ref[...]
        dm = (dy * sc).astype(CDT)
        dsc = jnp.sum(dy * mixed, axis=0, keepdims=True)
        dw = lax.dot_general(pooled_c, dm, (((0,), (0,)), ((), ())), preferred_element_type=F32)
        nt = (((1,), (1,)), ((), ()))
        dpl = lax.dot_general(dm, w, nt, preferred_element_type=F32)
        dmn = (jnp.where(i < nb - 1, dyn_ref[...], 0.0) * sc).astype(CDT)
        dpln = lax.dot_general(dmn, w, nt, preferred_element_type=F32)
        lane_h = lax.broadcasted_iota(jnp.int32, (POOL_HALO, 256), 1)
        ext = jnp.concatenate([dpl / _pool_counts(i * bt, bt, lane),
                               dpln / _pool_counts((i + 1) * bt, POOL_HALO, lane_h)], axis=0)
        sums, s, sh = [], ext, 1
        for _ in POOL_WINDOWS:
            s = s + pltpu.roll(s, n - sh, 0)
            sums.append(s[0:bt, :])
            sh *= 2
        du_ref[...] = (_by_group(sums, lane) - dpl).astype(du_ref.dtype)

        @pl.when(i == 0)
        def _():
            dw_ref[...] = dw
            dsc_ref[...] = dsc

        @pl.when(i > 0)
        def _():
            dw_ref[...] += dw
            dsc_ref[...] += dsc

    full = pl.BlockSpec((256, 256), lambda i: (0, 0))
    vec = pl.BlockSpec((1, 256), lambda i: (0, 0))
    return pl.pallas_call(
        body, grid=(nb,),
        in_specs=[pl.BlockSpec((bt, 256), lambda i: (i, 0)),
                  pl.BlockSpec((POOL_HALO, 256), lambda i: (jnp.minimum((i + 1) * hb, nb * hb - 1), 0)),
                  pl.BlockSpec((bt, 256), lambda i: (i, 0)),
                  pl.BlockSpec((POOL_HALO, 256), lambda i: (jnp.maximum(i * hb - 1, 0), 0)),
                  full, vec],
        out_specs=[pl.BlockSpec((bt, 256), lambda i: (i, 0)), full, vec],
        out_shape=[_sds((T, 256), CDT), _sds((256, 256), F32), _sds((1, 256), F32)],
        name=name, compiler_params=_cp(("arbitrary",)))(dcat, dcat, rest, rest, wbd, scale)


def _glu_ext(a_ref, g_ref, ah_ref, gh_ref, i):
    u = a_ref[...] * _sigmoid(g_ref[...])
    uh = jnp.where(i > 0, ah_ref[...] * _sigmoid(gh_ref[...]), 0.0)
    return jnp.concatenate([uh, u], axis=0)


def _conv_fwd(rest, cw, cb, lg, lb, name, bt=512):
    T = rest.shape[0]
    bt = _tile(T, bt)
    hb = bt // CONV_HALO

    def body(a_ref, g_ref, ah_ref, gh_ref, cw_ref, cb_ref, lg_ref, lb_ref, o_ref, y_ref):
        i = pl.program_id(0)
        ext = _glu_ext(a_ref, g_ref, ah_ref, gh_ref, i)
        w = cw_ref[...]
        acc = w[CONV_K - 1:CONV_K, :] * ext
        for k in range(CONV_K - 1):
            acc = acc + w[k:k + 1, :] * pltpu.roll(ext, CONV_K - 1 - k, 0)
        y = acc[CONV_HALO:, :] + cb_ref[...]
        y_ref[...] = y
        yc = y - jnp.mean(y, axis=-1, keepdims=True)
        yn = yc * lax.rsqrt(jnp.mean(yc * yc, axis=-1, keepdims=True) + NORM_EPS)
        z = yn * lg_ref[...] + lb_ref[...]
        o_ref[...] = (z * _sigmoid(z)).astype(o_ref.dtype)

    def cur(c):
        return pl.BlockSpec((bt, 256), lambda i: (i, c))

    def prev(c):
        return pl.BlockSpec((CONV_HALO, 256), lambda i: (jnp.maximum(i * hb - 1, 0), c))

    vec = pl.BlockSpec((1, 256), lambda i: (0, 0))
    return pl.pallas_call(
        body, grid=(T // bt,),
        in_specs=[cur(1), cur(2), prev(1), prev(2), pl.BlockSpec((CONV_HALO, 256), lambda i: (0, 0)), vec, vec, vec],
        out_specs=[pl.BlockSpec((bt, 256), lambda i: (i, 0)), pl.BlockSpec((bt, 256), lambda i: (i, 0))],
        out_shape=[_sds((T, 256), CDT), _sds((T, 256), F32)],
        name=name, compiler_params=_cp(("parallel",)))(rest, rest, rest, rest, cw, cb, lg, lb)


def _conv_bwd(dcat, yconv, rest, cw, lg, lb, name, bt=512):
    T = rest.shape[0]
    bt = _tile(T, bt)
    hb = bt // CONV_HALO
    nb = T // bt
    n = bt + CONV_HALO

    def body(dy_ref, dyn_ref, y_ref, yn_ref, a_ref, g_ref, ah_ref, gh_ref, cw_ref, lg_ref, lb_ref,
             da_ref, dg_ref, dcw_ref, dcb_ref, dlg_ref, dlb_ref):
        i = pl.program_id(0)
        lgv = lg_ref[...]
        lbv = lb_ref[...]

        def ln_swish_bwd(dout, y):
            yc = y - jnp.mean(y, axis=-1, keepdims=True)
            rs = lax.rsqrt(jnp.mean(yc * yc, axis=-1, keepdims=True) + NORM_EPS)
            yn = yc * rs
            z = yn * lgv + lbv
            sg = _sigmoid(z)
            dz = dout * (sg * (1.0 + z * (1.0 - sg)))
            dyn = dz * lgv
            dyc = rs * (dyn - jnp.mean(dyn, axis=-1, keepdims=True) - yn * jnp.mean(dyn * yn, axis=-1, keepdims=True))
            return dyc, dz, yn

        dyc, dz, yn = ln_swish_bwd(dy_ref[...], y_ref[...])
        dyc_next, _, _ = ln_swish_bwd(dyn_ref[...], yn_ref[...])
        dyc_next = jnp.where(i < nb - 1, dyc_next, 0.0)
        ext_u = _glu_ext(a_ref, g_ref, ah_ref, gh_ref, i)
        ext_d = jnp.concatenate([dyc, dyc_next], axis=0)
        w = cw_ref[...]
        du = w[CONV_K - 1:CONV_K, :] * ext_d
        rows = []
        for k in range(CONV_K):
            s = CONV_K - 1 - k
            if s > 0:
                du = du + w[k:k + 1, :] * pltpu.roll(ext_d, n - s, 0)
                us = pltpu.roll(ext_u, s, 0)[CONV_HALO:, :]
            else:
                us = ext_u[CONV_HALO:, :]
            rows.append(jnp.sum(dyc * us, axis=0, keepdims=True))
        rows.append(jnp.zeros((1, 256), F32))
        dcw = jnp.concatenate(rows, axis=0)
        du = du[0:bt, :]
        av = a_ref[...]
        sg = _sigmoid(g_ref[...])
        da_ref[...] = (du * sg).astype(da_ref.dtype)
        dg_ref[...] = (du * av * (sg * (1.0 - sg))).astype(dg_ref.dtype)
        dcb = jnp.sum(dyc, axis=0, keepdims=True)
        dlg = jnp.sum(dz * yn, axis=0, keepdims=True)
        dlb = jnp.sum(dz, axis=0, keepdims=True)

        @pl.when(i == 0)
        def _():
            dcw_ref[...] = dcw
            dcb_ref[...] = dcb
            dlg_ref[...] = dlg
            dlb_ref[...] = dlb

        @pl.when(i > 0)
        def _():
            dcw_ref[...] += dcw
            dcb_ref[...] += dcb
            dlg_ref[...] += dlg
            dlb_ref[...] += dlb

    def cur(c):
        return pl.BlockSpec((bt, 256), lambda i: (i, c))

    def prev(c):
        return pl.BlockSpec((CONV_HALO, 256), lambda i: (jnp.maximum(i * hb - 1, 0), c))

    def nxt(c):
        return pl.BlockSpec((CONV_HALO, 256), lambda i: (jnp.minimum((i + 1) * hb, nb * hb - 1), c))

    vec = pl.BlockSpec((1, 256), lambda i: (0, 0))
    wfull = pl.BlockSpec((CONV_HALO, 256), lambda i: (0, 0))
    return pl.pallas_call(
        body, grid=(nb,),
        in_specs=[cur(3), nxt(3), cur(0), nxt(0), cur(1), cur(2), prev(1), prev(2), wfull, vec, vec],
        out_specs=[cur(0), cur(0), wfull, vec, vec, vec],
        out_shape=[_sds((T, 256), CDT), _sds((T, 256), CDT), _sds((CONV_HALO, 256), F32),
                   _sds((1, 256), F32), _sds((1, 256), F32), _sds((1, 256), F32)],
        name=name, compiler_params=_cp(("arbitrary",)))(dcat, dcat, yconv, yconv, rest, rest, rest, rest, cw, lg, lb)


def _half_mask(shape, a):
    lane = lax.broadcasted_iota(jnp.int32, shape, 1)
    return (lane // HEAD_DIM) == a


def _attn_fwd(qkv, fcol, frow, name, blk=1024):
    T = qkv.shape[0]
    blk = _tile(T, blk)
    nq = T // blk
    nt = (((1,), (1,)), ((), ()))

    def body(q_ref, k_ref, v_ref, fc_ref, fr_ref, o_ref, lse_ref):
        p_id = pl.program_id(0)
        i = pl.program_id(1)
        q2 = q_ref[...]
        fc = fc_ref[...]
        lane = lax.broadcasted_iota(jnp.int32, (blk, LANES), 1)
        tri = lax.broadcasted_iota(jnp.int32, (blk, blk), 1) <= lax.broadcasted_iota(jnp.int32, (blk, blk), 0)
        masks = [_half_mask(q2.shape, a) for a in range(2)]
        qs = [jnp.where(hm, q2, jnp.zeros_like(q2)) * ATT_SCALE for hm in masks]
        fqs = [jnp.sum(jnp.where(lane == 2 * p_id + a, fc, 0.0), axis=1, keepdims=True) for a in range(2)]

        def tile(j, carry, masked):
            cols = pl.ds(pl.multiple_of(j * blk, blk), blk)
            kj = k_ref[cols, :]
            vj = v_ref[cols, :]
            out = []
            for a in range(2):
                m, acc = carry[2 * a:2 * a + 2]
                va = jnp.where(masks[a], vj, jnp.ones_like(vj))
                s = lax.dot_general(qs[a], kj, nt, preferred_element_type=F32) + (fqs[a] - fr_ref[a:a + 1, cols])
                if masked:
                    s = jnp.where(tri, s, NEG)
                m_new = jnp.maximum(m, jnp.max(s, axis=1, keepdims=True))
                alpha = jnp.exp(m - m_new)
                pr = jnp.exp(s - m_new)
                hi = lax.bitcast_convert_type(lax.bitcast_convert_type(pr, jnp.uint32) & jnp.uint32(0xFFFF0000), F32)
                pv = (jnp.dot(hi.astype(CDT), va, preferred_element_type=F32)
                      + jnp.dot((pr - hi).astype(CDT), va, preferred_element_type=F32))
                out += [m_new, alpha * acc + pv]
            return tuple(out)

        init = (jnp.full((blk, 1), NEG, F32), jnp.zeros((blk, LANES), F32)) * 2
        carry = lax.fori_loop(0, i, lambda j, c: tile(j, c, False), init)
        carry = tile(i, carry, True)
        ls = [carry[1][:, HEAD_DIM:HEAD_DIM + 1], carry[3][:, 0:1]]
        lo = lane < HEAD_DIM
        o_ref[...] = jnp.where(lo, carry[1] / ls[0], carry[3] / ls[1])
        lse_t = jnp.transpose(jnp.where(lo, carry[0] + jnp.log(ls[0]), carry[2] + jnp.log(ls[1])))
        lse_ref[...] = jnp.concatenate([lse_t[0:1, :], lse_t[HEAD_DIM:HEAD_DIM + 1, :]], axis=0)

    return pl.pallas_call(
        body, grid=(N_PAIRS, nq),
        in_specs=[pl.BlockSpec((blk, LANES), lambda p, i: (i, p)),
                  pl.BlockSpec((T, LANES), lambda p, i: (0, N_PAIRS + p)),
                  pl.BlockSpec((T, LANES), lambda p, i: (0, 2 * N_PAIRS + p)),
                  pl.BlockSpec((blk, LANES), lambda p, i: (i, 0)),
                  pl.BlockSpec((None, 2, T), lambda p, i: (p, 0, 0))],
        out_specs=[pl.BlockSpec((blk, LANES), lambda p, i: (i, p)), pl.BlockSpec((None, 2, blk), lambda p, i: (p, 0, i))],
        out_shape=[_sds((T, N_PAIRS * LANES), F32), _sds((N_PAIRS, 2, T), F32)],
        name=name, compiler_params=_cp(("parallel", "arbitrary")))(qkv, qkv, qkv, fcol, frow)


def _attn_delta(dcat, o, name, blk=512):
    T = o.shape[0]
    blk = _tile(T, blk)

    def body(d_ref, o_ref, out_ref):
        prod = d_ref[:, 256:768].astype(CDT).astype(F32) * o_ref[...]
        pt = jnp.transpose(prod)
        out_ref[...] = jnp.sum(pt.reshape(N_HEADS, HEAD_DIM, blk), axis=1)

    return pl.pallas_call(
        body, grid=(T // blk,),
        in_specs=[pl.BlockSpec((blk, 1024), lambda i: (i, 0)), pl.BlockSpec((blk, 512), lambda i: (i, 0))],
        out_specs=pl.BlockSpec((N_HEADS, blk), lambda i: (0, i)),
        out_shape=_sds((N_HEADS, T), F32), name=name, compiler_params=_cp(("parallel",)))(dcat, o)


def _attn_bwd(qkv, dcat, fcol, frow, lse, delta, name, blk=1024):
    T = qkv.shape[0]
    blk = _tile(T, blk)
    nq = T // blk
    nt = (((1,), (1,)), ((), ()))

    def body(q_ref, do_ref, k_ref, v_ref, fc_ref, fr_ref, lse_ref, dl_ref, dqt_ref, dk_ref, dv_ref, df_ref):
        p_id = pl.program_id(0)
        j = pl.program_id(1)

        @pl.when(j == 0)
        def _():
            dqt_ref[...] = jnp.zeros_like(dqt_ref)

        k2 = k_ref[...]
        v2 = v_ref[...]
        fc = fc_ref[...]
        lane = lax.broadcasted_iota(jnp.int32, (blk, LANES), 1)
        tri = lax.broadcasted_iota(jnp.int32, (blk, blk), 0) <= lax.broadcasted_iota(jnp.int32, (blk, blk), 1)
        masks = [_half_mask(k2.shape, a) for a in range(2)]
        kas = [jnp.where(hm, k2, jnp.zeros_like(k2)) * ATT_SCALE for hm in masks]
        kats = [jnp.transpose(ka) for ka in kas]
        vas = [jnp.where(hm, v2, jnp.zeros_like(v2)) for hm in masks]
        fks = [jnp.sum(jnp.where(lane == 2 * p_id + a, fc, 0.0), axis=1, keepdims=True) for a in range(2)]

        def tile(i, carry, masked):
            rows = pl.ds(pl.multiple_of(i * blk, blk), blk)
            qi = q_ref[rows, :]
            doi = do_ref[rows, :].astype(CDT)
            out = []
            dqt = None
            for a in range(2):
                dk_acc, dv_acc, df_acc = carry[3 * a:3 * a + 3]
                st = lax.dot_general(kas[a], qi, nt, preferred_element_type=F32)
                e = (st + (fr_ref[a:a + 1, rows] - fks[a])) - lse_ref[a:a + 1, rows]
                if masked:
                    e = jnp.where(tri, e, NEG)
                pt = jnp.exp(e)
                dpt = lax.dot_general(vas[a], doi, nt, preferred_element_type=F32)
                ds32 = pt * (dpt - dl_ref[a:a + 1, rows])
                dst = ds32.astype(CDT)
                df_acc = df_acc + jnp.sum(ds32, axis=1, keepdims=True)
                dv_acc = dv_acc + jnp.dot(pt.astype(CDT), doi, preferred_element_type=F32)
                dk_acc = dk_acc + jnp.dot(dst, qi, preferred_element_type=F32)
                part = jnp.dot(kats[a], dst, preferred_element_type=F32)
                dqt = part if dqt is None else dqt + part
                out += [dk_acc, dv_acc, df_acc]
            dqt_ref[:, rows] += dqt
            return tuple(out)

        init = (jnp.zeros((blk, LANES), F32), jnp.zeros((blk, LANES), F32), jnp.zeros((blk, 1), F32)) * 2
        carry = tile(j, init, True)
        carry = lax.fori_loop(j + 1, nq, lambda i, c: tile(i, c, False), carry)
        lo = lane < HEAD_DIM
        dk_ref[...] = (jnp.where(lo, carry[0], carry[3]) * ATT_SCALE).astype(dk_ref.dtype)
        dv_ref[...] = jnp.where(lo, carry[1], carry[4]).astype(dv_ref.dtype)
        df_ref[...] = -jnp.where(lo, carry[2], carry[5])

    res = pl.BlockSpec((T, LANES), lambda p, j: (0, p))
    rows = pl.BlockSpec((None, 2, T), lambda p, j: (p, 0, 0))
    kv_out = pl.BlockSpec((blk, LANES), lambda p, j: (j, p))
    return pl.pallas_call(
        body, grid=(N_PAIRS, nq),
        in_specs=[res, pl.BlockSpec((T, LANES), lambda p, j: (0, 2 + p)),
                  pl.BlockSpec((blk, LANES), lambda p, j: (j, N_PAIRS + p)),
                  pl.BlockSpec((blk, LANES), lambda p, j: (j, 2 * N_PAIRS + p)),
                  pl.BlockSpec((blk, LANES), lambda p, j: (j, 0)), rows, rows, rows],
        out_specs=[pl.BlockSpec((LANES, T), lambda p, j: (p, 0)), kv_out, kv_out, kv_out],
        out_shape=[_sds((N_PAIRS * LANES, T), F32), _sds((T, N_PAIRS * LANES), CDT), _sds((T, N_PAIRS * LANES), CDT),
                   _sds((T, N_PAIRS * LANES), F32)],
        name=name, compiler_params=_cp(("parallel", "arbitrary")))(qkv, dcat, qkv, qkv, fcol, frow, lse, delta)


def _mixer_fwd(x, wts, tag, dep=None):
    T = x.shape[0]
    h = _rms_fwd(x, wts["mix_norm"], f"{tag}_norm", dep)
    qkv = _mm([(h, wts["win_qkv"], False)], out_dtype=CDT, tm=1024, tn=768, name=f"{tag}_in_qkv")
    rest = _mm([(h, wts["win_rest"], False)], tm=1024, name=f"{tag}_in_rest")
    fcol, frow8 = _fgate_fwd(rest, wts["fbias"], f"{tag}_fgate")
    frow = frow8.reshape(N_PAIRS, 2, T)
    ya = _pool_fwd(rest, wts["pool_wbd"], wts["pool_scale"], f"{tag}_pool")
    o, lse = _attn_fwd(qkv, fcol, frow, f"{tag}_attn")
    yc, yconv = _conv_fwd(rest, wts["conv_w"], wts["conv_b"], wts["conv_ln_g"], wts["conv_ln_b"], f"{tag}_conv")
    cat = jnp.concatenate([ya, o.astype(CDT), yc], axis=1)
    y = _mm([(cat, wts["w_out"], False)], res=x, name=f"{tag}_out")
    return y, (x, h, qkv, rest, fcol, frow, o, lse, yconv, cat)


def _mixer_bwd(saved, wts, gout, tag, dep=None):
    x, h, qkv, rest, fcol, frow, o, lse, yconv, cat = saved
    T = x.shape[0]
    dcat = _mm([(gout, wts["w_out"], True)], name=f"{tag}_dcat", dep=dep)
    dwout = _mm_tn(cat, gout, name=f"{tag}_dwout")
    du, dpw, dpsc = _pool_bwd(dcat, rest, wts["pool_wbd"], wts["pool_scale"], f"{tag}_pool_bwd")
    delta = _attn_delta(dcat, o, f"{tag}_attn_delta").reshape(N_PAIRS, 2, T)
    dqt, dk, dv, dfk = _attn_bwd(qkv, dcat, fcol, frow, lse, delta, f"{tag}_attn_bwd")
    dq = dqt.T.astype(CDT)
    dz, dfb = _fgate_bwd(dfk, rest, wts["fbias"], f"{tag}_fgate_bwd")
    da, dg, dcw, dcb, dlg, dlb = _conv_bwd(dcat, yconv, rest, wts["conv_w"], wts["conv_ln_g"], wts["conv_ln_b"],
                                           f"{tag}_conv_bwd")
    dp_qkv = jnp.concatenate([dq, dk, dv], axis=1).astype(CDT)
    dp_rest = jnp.concatenate([du, da, dg, dz], axis=1)
    dwin_qkv = _mm_tn(h, dp_qkv, name=f"{tag}_dwin_qkv")
    dwin_rest = _mm_tn(h, dp_rest, name=f"{tag}_dwin_rest")
    gin, dgamma = _mm_norm_bwd([(dp_qkv, wts["win_qkv"], True), (dp_rest, wts["win_rest"], True)], x, wts["mix_norm"],
                               gout, name=f"{tag}_dh_norm_bwd")
    dwin = _split_win(dwin_qkv, dwin_rest, f"{tag}_dwin_split")
    dpool_w = jnp.stack([dpw[64 * g:64 * g + 64, 64 * g:64 * g + 64] for g in range(4)])
    grads = dict(mix_norm=dgamma[0], w_in=dwin, pool_w=dpool_w, pool_scale=dpsc[0], forget_bias=dfb[0, 0:N_HEADS],
                 conv_w=dcw[0:CONV_K], conv_b=dcb[0], conv_ln_g=dlg[0], conv_ln_b=dlb[0], w_out=dwout)
    return gin, grads


def _rep_layer(rep, l):
    pw = rep["pool_w"][l].astype(CDT)
    wbd = jnp.zeros((256, 256), CDT)
    for g in range(4):
        wbd = lax.dynamic_update_slice(wbd, pw[g], (64 * g, 64 * g))
    return dict(
        ffn1_norm=rep["ffn1_norm"][l][None], ffn2_norm=rep["ffn2_norm"][l][None], mix_norm=rep["mix_norm"][l][None],
        fbias=jnp.pad(rep["forget_bias"][l], (0, LANES - N_HEADS))[None],
        pool_wbd=wbd, pool_scale=rep["pool_scale"][l][None], conv_b=rep["conv_b"][l][None],
        conv_ln_g=rep["conv_ln_g"][l][None], conv_ln_b=rep["conv_ln_b"][l][None])


def _local_step(x, target, rep, weights_for, grads_ready):
    depth = rep["ffn1_norm"].shape[0]
    kept = []
    for l in range(depth):
        r = _rep_layer(rep, l)
        w1, dep = weights_for(l, "ffn1", x)
        x, s1 = _ffn_fwd(x, r["ffn1_norm"], w1["w_gate"], w1["w_up"], w1["w_down"], f"l{l}_ffn1", dep)
        wm, dep = weights_for(l, "mix", x)
        wm = dict(r, win_qkv=wm["win_qkv"], win_rest=wm["win_rest"], w_out=wm["w_out"],
                  conv_w=jnp.pad(wm["conv_w"], ((0, CONV_HALO - CONV_K), (0, 0))))
        x, s2 = _mixer_fwd(x, wm, f"l{l}_mix", dep)
        w2, dep = weights_for(l, "ffn2", x)
        x, s3 = _ffn_fwd(x, r["ffn2_norm"], w2["w_gate"], w2["w_up"], w2["w_down"], f"l{l}_ffn2", dep)
        kept.append((r, w1, wm, w2, s1, s2, s3))
    loss, g, dfinal = _loss_bwd(x, rep["final_norm"][None], target, "loss_head")
    dep = grads_ready(None, "final", dict(final_norm=dfinal[0]))
    for l in reversed(range(depth)):
        r, w1, wm, w2, s1, s2, s3 = kept[l]

        def ffn_grads(which, l=l):
            return lambda gr: grads_ready(l, which, {f"{which}_{k}": v for k, v in gr.items()})

        g, dn = _ffn_bwd(s3, r["ffn2_norm"], w2["w_gate"], w2["w_up"], w2["w_down"], g, f"l{l}_ffn2", dep, ffn_grads("ffn2"))
        grads_ready(l, "norm", dict(ffn2_norm=dn[0]))
        g, gm = _mixer_bwd(s2, wm, g, f"l{l}_mix")
        dep = grads_ready(l, "mix", gm)
        g, dn = _ffn_bwd(s1, r["ffn1_norm"], w1["w_gate"], w1["w_up"], w1["w_down"], g, f"l{l}_ffn1", dep, ffn_grads("ffn1"))
        dep = grads_ready(l, "norm", dict(ffn1_norm=dn[0]))
    return loss, g


def _mesh_pos():
    return lax.axis_index("x"), lax.axis_index("y"), lax.axis_index("c")


def _dev_block(ref, dev, by_rows):
    if by_rows:
        r = ref.shape[1] // N_DEV
        return ref.at[:, pl.ds(dev * r, r), :]
    return ref.at[dev]


def _all_gather(shards, by_rows, name):
    n_arr = len(shards)
    out_shape = [_sds((s.shape[0], N_DEV * s.shape[1], s.shape[2]) if br else (N_DEV,) + s.shape, s.dtype)
                 for s, br in zip(shards, by_rows)]

    def body(*refs):
        xs, outs = refs[:n_arr], refs[n_arr:2 * n_arr]
        send_sems, recv_sems, local_sems = refs[2 * n_arr:]
        x, y, c = _mesh_pos()
        me, sibling = (x, y, c), (x, y, 1 - c)
        chips = [(1 - x, y), (x, 1 - y), (1 - x, 1 - y)]

        def rows(a, px, py, pc):
            return _dev_block(outs[a], 4 * px + 2 * py + pc, by_rows[a])

        def copy(k, a, block, to, src=None):
            return pltpu.make_async_remote_copy(
                src_ref=rows(a, *block) if src is None else src, dst_ref=rows(a, *block),
                send_sem=send_sems.at[k, a], recv_sem=recv_sems.at[k, a],
                device_id=to, device_id_type=pl.DeviceIdType.MESH)

        arrs = range(n_arr)
        mine = [pltpu.make_async_copy(xs[a], rows(a, *me), local_sems.at[a]) for a in arrs]
        for cp in mine:
            cp.start()
        first = [copy(0, a, me, sibling, src=xs[a]) for a in arrs]
        first += [copy(1 + j, a, me, (*chip, c), src=xs[a]) for j, chip in enumerate(chips) for a in arrs]
        for cp in first:
            cp.start()
        passed = []
        for j, chip in enumerate(chips):
            for a in arrs:
                copy(1 + j, a, (*chip, c), me).wait_recv()
                passed.append(copy(4 + j, a, (*chip, c), sibling))
                passed[-1].start()
        for a in arrs:
            copy(0, a, sibling, me).wait_recv()
        for j, chip in enumerate(chips):
            for a in arrs:
                copy(4 + j, a, (*chip, 1 - c), me).wait_recv()
        for cp in first + passed:
            cp.wait_send()
        for cp in mine:
            cp.wait()

    hbm = pl.BlockSpec(memory_space=pl.ANY)
    return pl.pallas_call(
        body, out_shape=out_shape, in_specs=[hbm] * n_arr, out_specs=[hbm] * n_arr,
        scratch_shapes=[pltpu.SemaphoreType.DMA((7, n_arr)), pltpu.SemaphoreType.DMA((7, n_arr)),
                        pltpu.SemaphoreType.DMA((n_arr,))],
        name=name)(*shards)


def _exchange(parts, by_rows, name):
    n_arr = len(parts)
    out_shape = [_sds((N_DEV, p.shape[0], p.shape[1] // N_DEV, p.shape[2]) if br else p.shape, p.dtype)
                 for p, br in zip(parts, by_rows)]

    def body(*refs):
        ps, outs = refs[:n_arr], refs[n_arr:2 * n_arr]
        send_sems, recv_sems, local_sems = refs[2 * n_arr:]
        x, y, c = _mesh_pos()
        my = 4 * x + 2 * y + c
        arrs = range(n_arr)
        mine = [pltpu.make_async_copy(_dev_block(ps[a], my, by_rows[a]), outs[a].at[my], local_sems.at[a]) for a in arrs]
        for cp in mine:
            cp.start()
        copies = []
        for k in range(1, N_DEV):
            px, py, pc = x ^ (k >> 2), y ^ ((k >> 1) & 1), c ^ (k & 1)
            for a in arrs:
                copies.append(pltpu.make_async_remote_copy(
                    src_ref=_dev_block(ps[a], 4 * px + 2 * py + pc, by_rows[a]), dst_ref=outs[a].at[my],
                    send_sem=send_sems.at[k - 1, a], recv_sem=recv_sems.at[k - 1, a],
                    device_id=(px, py, pc), device_id_type=pl.DeviceIdType.MESH))
        for cp in copies:
            cp.start()
        for cp in copies:
            cp.wait()
        for cp in mine:
            cp.wait()

    hbm = pl.BlockSpec(memory_space=pl.ANY)
    return pl.pallas_call(
        body, out_shape=out_shape, in_specs=[hbm] * n_arr, out_specs=[hbm] * n_arr,
        scratch_shapes=[pltpu.SemaphoreType.DMA((7, n_arr)), pltpu.SemaphoreType.DMA((7, n_arr)),
                        pltpu.SemaphoreType.DMA((n_arr,))],
        name=name)(*parts)


def _peer_copies(srcs, lands, send_sems, recv_sems, gather, by_rows):
    n_arr = len(srcs)
    x, y, c = _mesh_pos()
    my = 4 * x + 2 * y + c
    out = []
    for k in range(1, N_DEV):
        px, py, pc = x ^ (k >> 2), y ^ ((k >> 1) & 1), c ^ (k & 1)
        peer = 4 * px + 2 * py + pc
        for a in range(n_arr):
            src = srcs[a] if gather else _dev_block(srcs[a], peer, by_rows[a])
            dst = _dev_block(lands[a], my, by_rows[a]) if gather else lands[a].at[my]
            out.append(pltpu.make_async_remote_copy(
                src_ref=src, dst_ref=dst, send_sem=send_sems.at[(k - 1) * n_arr + a],
                recv_sem=recv_sems.at[(k - 1) * n_arr + a], device_id=(px, py, pc), device_id_type=pl.DeviceIdType.MESH))
    return out


def _land_shape(s, gather, by_rows):
    if gather:
        return (s.shape[0], N_DEV * s.shape[1], s.shape[2]) if by_rows else (N_DEV,) + s.shape
    return (N_DEV, s.shape[0], s.shape[1] // N_DEV, s.shape[2]) if by_rows else s.shape


_HBM = pl.BlockSpec(memory_space=pltpu.HBM)
_SEM = pl.BlockSpec(memory_space=pltpu.SEMAPHORE)


def _xfer_start(srcs, gather, by_rows, name, dep=None):
    n = len(srcs)
    x, y, c = _mesh_pos()
    my = 4 * x + 2 * y + c
    lands = []
    for src, br in zip(srcs, by_rows):
        land = lax.empty(_land_shape(src, gather, br), src.dtype)
        zeros = (0,) * (land.ndim - 1)
        if gather and br:
            own, at = src, (0, my * src.shape[1], 0)
        elif gather:
            own, at = src[None], (my,) + zeros
        elif br:
            r = src.shape[1] // N_DEV
            own = lax.dynamic_slice(src, (0, my * r, 0), (src.shape[0], r, src.shape[2]))[None]
            at = (my,) + zeros
        else:
            own, at = lax.dynamic_index_in_dim(src, my, 0, keepdims=True), (my,) + zeros
        lands.append(lax.dynamic_update_slice(land, own, at))
    ins = [pltpu.with_memory_space_constraint(a, pltpu.HBM) for a in list(srcs) + lands]
    dspec, darg = _dep(dep)

    def body(*refs):
        s = 2 * n + len(darg)
        for cp in _peer_copies(refs[:n], refs[n:2 * n], refs[s], refs[s + 1], gather, by_rows):
            cp.start()
        refs[-1][...] = jnp.zeros_like(refs[-1])

    sems = pltpu.SemaphoreType.DMA(((N_DEV - 1) * n,))
    outs = pl.pallas_call(
        body, name=name,
        out_shape=(sems, sems, *[pltpu.HBM(a.shape, a.dtype) for a in ins], _sds((8, LANES), F32)),
        in_specs=[_HBM] * (2 * n) + dspec,
        out_specs=(_SEM, _SEM, *[_HBM] * (2 * n), pl.BlockSpec(memory_space=pltpu.VMEM)),
        input_output_aliases={i: 2 + i for i in range(2 * n)},
        compiler_params=pltpu.CompilerParams(has_side_effects=pltpu.SideEffectType.DATAFLOW_SIDE_EFFECTING))(*ins, *darg)
    return outs[0], outs[1], list(outs[2:-1]), outs[-1]


def _xfer_wait(started, after, gather, by_rows, name):
    send_sems, recv_sems, bufs, _ = started
    n = len(bufs) // 2

    def body(*refs):
        for cp in _peer_copies(refs[:n], refs[n:2 * n], refs[2 * n], refs[2 * n + 1], gather, by_rows):
            cp.wait_send()
            cp.wait_recv()

    outs = pl.pallas_call(
        body, name=name, out_shape=tuple(pltpu.HBM(a.shape, a.dtype) for a in bufs),
        in_specs=[_HBM] * (2 * n) + [_SEM, _SEM, pl.BlockSpec(memory_space=pl.ANY)], out_specs=tuple([_HBM] * (2 * n)),
        input_output_aliases={i: i for i in range(2 * n)},
        compiler_params=pltpu.CompilerParams(has_side_effects=pltpu.SideEffectType.DATAFLOW_SIDE_EFFECTING))(
            *bufs, send_sems, recv_sems, after)
    return list(outs[n:])


def _adam_update(g, w, m, v):
    c1 = 1.0 - ADAM_B1 ** ADAM_STEP
    c2 = 1.0 - ADAM_B2 ** ADAM_STEP
    nm = ADAM_B1 * m + (1.0 - ADAM_B1) * g
    nv = ADAM_B2 * v + (1.0 - ADAM_B2) * (g * g)
    return -ADAM_LR * ((nm / c1) / (jnp.sqrt(nv / c2) + ADAM_EPS) + ADAM_WD * w), nm, nv


def _adamw_body(p_ref, w_ref, m_ref, v_ref, g_ref, d_ref, nm_ref, nv_ref):
    g = p_ref[0]
    for i in range(1, N_DEV):
        g = g + p_ref[i]
    g_ref[...] = g
    d_ref[...], nm_ref[...], nv_ref[...] = _adam_update(g, w_ref[...], m_ref[...], v_ref[...])


def _adamw(parts, w, m, v, name, tr=1536):
    R = w.shape[0]
    tr = max(t for t in range(8, tr + 1, 8) if R % t == 0)

    def body(*refs):
        _adamw_body(*refs)

    row = pl.BlockSpec((tr, LANES), lambda i: (i, 0))
    return pl.pallas_call(
        body, grid=(R // tr,),
        in_specs=[pl.BlockSpec((N_DEV, tr, LANES), lambda i: (0, i, 0)), row, row, row],
        out_specs=[row, row, row, row], out_shape=[_sds((R, LANES), F32)] * 4,
        name=name, compiler_params=_cp(("parallel",)))(parts, w, m, v)


def _adamw_split(recvs, w, m, v, name, tr):
    depth, r, c = w.shape
    assert depth == len(recvs)
    tr = _tile(r, tr)

    def body(*refs):
        layer = pl.program_id(0)
        for ll in range(depth):
            @pl.when(layer == ll)
            def _(ll=ll):
                _adamw_body(refs[ll], *refs[depth:])

    wspec = pl.BlockSpec((None, tr, c), lambda l, i: (l, i, 0))
    rspecs = [pl.BlockSpec((N_DEV, None, tr, c), lambda l, i, ll=ll, t=t: (0, t, jnp.where(l == ll, i, 0), 0))
              for ll, (_, t) in enumerate(recvs)]
    return pl.pallas_call(
        body, grid=(depth, r // tr), in_specs=rspecs + [wspec, wspec, wspec],
        out_specs=[wspec] * 4, out_shape=[_sds(w.shape, F32)] * 4,
        name=name, compiler_params=_cp(("arbitrary", "arbitrary")))(*[a for a, _ in recvs], w, m, v)


def _merge_win(g, name, tr=256):
    _, nt, K, n = g.shape
    tr = _tile(K, tr)

    def body(g_ref, q_ref, r_ref):
        full = jnp.concatenate([g_ref[j] for j in range(N_DEV)], axis=1)
        q_ref[...] = full[:, 256:1792]
        zpad = jnp.zeros((tr, REST_W - 776), full.dtype)
        r_ref[...] = jnp.concatenate([full[:, 0:256], full[:, 1800:2312], full[:, 1792:1800], zpad], axis=1)

    return pl.pallas_call(
        body, grid=(nt, K // tr),
        in_specs=[pl.BlockSpec((N_DEV, None, tr, n), lambda t, i: (0, t, i, 0))],
        out_specs=[pl.BlockSpec((None, tr, 1536), lambda t, i: (t, i, 0)), pl.BlockSpec((None, tr, REST_W), lambda t, i: (t, i, 0))],
        out_shape=[_sds((nt, K, 1536), g.dtype), _sds((nt, K, REST_W), g.dtype)],
        name=name, compiler_params=_cp(("parallel", "parallel")))(g)


def _split_win(dq, dr, name, tr=256):
    K = dq.shape[0]
    tr = _tile(K, tr)
    n = (dq.shape[1] + 776) // N_DEV

    def body(q_ref, r_ref, o_ref):
        r = r_ref[...]
        full = jnp.concatenate([r[:, 0:256], q_ref[...], r[:, 768:776], r[:, 256:768]], axis=1)
        for j in range(N_DEV):
            o_ref[j] = full[:, n * j:n * (j + 1)]

    return pl.pallas_call(
        body, grid=(K // tr,),
        in_specs=[pl.BlockSpec((tr, dq.shape[1]), lambda i: (i, 0)), pl.BlockSpec((tr, REST_W), lambda i: (i, 0))],
        out_specs=pl.BlockSpec((N_DEV, tr, n), lambda i: (0, i, 0)),
        out_shape=_sds((N_DEV, K, n), F32), name=name, compiler_params=_cp(("parallel",)))(dq, dr)


WEIGHTS = ["ffn1_norm", "ffn1_w_gate", "ffn1_w_up", "ffn1_w_down", "mix_norm", "w_in", "pool_w", "pool_scale",
           "forget_bias", "conv_w", "conv_b", "conv_ln_g", "conv_ln_b", "w_out", "ffn2_norm", "ffn2_w_gate",
           "ffn2_w_up", "ffn2_w_down", "final_norm"]
FFN_PARTS = ("w_gate", "w_up", "w_down")
FFN_T = ["ffn1_w_gate", "ffn1_w_up", "ffn2_w_gate", "ffn2_w_up"]
BIG = FFN_T + ["ffn1_w_down", "ffn2_w_down", "w_in", "w_out"]
SMALL = [n for n in WEIGHTS if n not in BIG]


def _padded(n):
    return -(-n // PACK_ALIGN) * PACK_ALIGN


def _flat_pad(a):
    f = a.reshape(-1)
    return jnp.pad(f, (0, _padded(f.shape[0]) - f.shape[0]))


def _split8(a, axis):
    shp = a.shape
    a = a.reshape(shp[:axis] + (N_DEV, shp[axis] // N_DEV) + shp[axis + 1:])
    return jnp.moveaxis(a, axis, 0)


def _merge8(a, axis):
    a = jnp.moveaxis(a, 0, axis)
    shp = a.shape
    return a.reshape(shp[:axis] + (shp[axis] * shp[axis + 1],) + shp[axis + 2:])


def _pack_small(arrs):
    return jnp.concatenate([_flat_pad(arrs[n]) for n in SMALL]).reshape(-1, LANES)


def _pack_small_parts(grads):
    cols = []
    for n in SMALL:
        g = grads[n]
        if n == "conv_w":
            s = _split8(g, 2).reshape(N_DEV, -1)
        else:
            s = jnp.broadcast_to(g.reshape(1, -1), (N_DEV, g.size))
        cols.append(jnp.pad(s, ((0, 0), (0, _padded(s.shape[1]) - s.shape[1]))))
    return jnp.concatenate(cols, axis=1).reshape(N_DEV, -1, LANES)


def _unpack_small(buf, like):
    flat = buf.reshape(-1)
    out, off = {}, 0
    for n in SMALL:
        size = like[n].size
        out[n] = flat[off:off + size].reshape(like[n].shape)
        off += _padded(size)
    return out


class _Comm:
    def __init__(self, w):
        self.w = w
        self.bf = {n: (jnp.swapaxes(w[n], 1, 2) if n in FFN_T else w[n]).astype(CDT) for n in BIG}
        self.ready = {}
        self.grads = {}

    def _ffn_shards(self, l, which):
        return jnp.stack([self.bf[f"{which}_{k}"][l] for k in FFN_PARTS])

    def _put_ffn(self, l, which, rows, t):
        self.ready[(l, which)] = dict(w_gate=rows[t], w_up=rows[t + 1], w_down=rows[t + 2])

    def weights_for(self, l, stage, x):
        bf = self.bf
        dep = None
        if (l, stage) == (0, "ffn1"):
            gd, = _all_gather([self._ffn_shards(0, "ffn1")], [True], "gather_l0_ffn1")
            self._put_ffn(0, "ffn1", gd, 0)
            self.started = _xfer_start([bf["w_in"][0:1], bf["w_out"][0:1], self.w["conv_w"]], True,
                                       [False, True, False], "gather_mix0_start", dep=gd)
            dep = self.started[3]
        elif (l, stage) == (0, "mix"):
            gi, go, gc = _xfer_wait(self.started, x, True, [False, True, False], "gather_mix0_wait")
            q, r = _merge_win(gi, "merge_l0_w_in")
            self.conv_w = _merge8(gc, 2)
            self.ready[(0, "mix")] = dict(win_qkv=q[0], win_rest=r[0], w_out=go[0], conv_w=self.conv_w[0])
            rows = jnp.concatenate([self._ffn_shards(0, "ffn2"), self._ffn_shards(1, "ffn1"), self._ffn_shards(1, "ffn2")])
            self.started = _xfer_start([rows, bf["w_in"][1:2], bf["w_out"][1:2]], True, [True, False, True],
                                       "gather_rest_start")
            dep = self.started[3]
        elif (l, stage) == (0, "ffn2"):
            gd, gi, go = _xfer_wait(self.started, x, True, [True, False, True], "gather_rest_wait")
            self._put_ffn(0, "ffn2", gd, 0)
            self._put_ffn(1, "ffn1", gd, 3)
            self._put_ffn(1, "ffn2", gd, 6)
            q, r = _merge_win(gi, "merge_l1_w_in")
            self.ready[(1, "mix")] = dict(win_qkv=q[0], win_rest=r[0], w_out=go[0], conv_w=self.conv_w[1])
        return self.ready[(l, stage)], dep

    def grads_ready(self, l, stage, grads):
        for n, v in grads.items():
            self.grads[(l, n)] = v
        gr = self.grads

        def ffn_rows(layer, which, parts=FFN_PARTS):
            return [gr[(layer, f"{which}_{k}")] for k in parts]

        if l == 1 and "ffn1_w_gate" in grads:
            self.sent1 = _xfer_start(
                [jnp.stack(ffn_rows(1, "ffn1") + ffn_rows(1, "ffn2")), gr[(1, "w_in")][:, None], gr[(1, "w_out")][None]],
                False, [True, False, True], "grads_l1_start")
            return self.sent1[3]
        if l == 0 and "ffn2_w_gate" in grads:
            self.sent_ffn2 = _xfer_start([jnp.stack(ffn_rows(0, "ffn2"))], False, [True], "grads_l0_ffn2_start")
            return self.sent_ffn2[3]
        if (l, stage) == (0, "mix"):
            self.sent_mix = _xfer_start([gr[(0, "w_in")][:, None], gr[(0, "w_out")][None]], False, [False, True],
                                        "grads_l0_mix_start")
            return self.sent_mix[3]
        if l == 0 and "ffn1_w_down" in grads:
            self.sent_down = _xfer_start([gr[(0, "ffn1_w_down")][None]], False, [True], "grads_l0_ffn1_down_start")
            return self.sent_down[3]
        if l == 0 and "ffn1_w_gate" in grads:
            self.sent_gu = _xfer_start([jnp.stack(ffn_rows(0, "ffn1", FFN_PARTS[:2]))], False, [True],
                                       "grads_l0_ffn1_gate_up_start")
            return self.sent_gu[3]
        return None

    def finish(self, m, v, after):
        w, gr = self.w, self.grads
        depth = range(w["w_in"].shape[0])
        small = {n: (gr[(None, n)] if n == "final_norm" else jnp.stack([gr[(l, n)] for l in depth])) for n in SMALL}
        r1, i1, o1 = _xfer_wait(self.sent1, after, False, [True, False, True], "grads_l1_wait")
        r2, = _xfer_wait(self.sent_ffn2, after, False, [True], "grads_l0_ffn2_wait")
        i0, o0 = _xfer_wait(self.sent_mix, after, False, [False, True], "grads_l0_mix_wait")

        def adam(n, recvs, tr):
            if n in FFN_T:
                out = _adamw_split(recvs, *[jnp.swapaxes(t[n], 1, 2) for t in (w, m, v)], f"adamw_{n}", tr)
                return [jnp.swapaxes(o, 1, 2) for o in out]
            return _adamw_split(recvs, w[n], m[n], v[n], f"adamw_{n}", tr)

        res = {}
        for t, k in enumerate(FFN_PARTS):
            res[f"ffn2_{k}"] = adam(f"ffn2_{k}", [(r2, t), (r1, 3 + t)], 176)
        res["w_in"] = adam("w_in", [(i0, 0), (i1, 0)], 256)
        res["w_out"] = adam("w_out", [(o0, 0), (o1, 0)], 128)
        rs, = _exchange([_pack_small_parts(small)], [False], "exchange_small")
        r0, = _xfer_wait(self.sent_down, res["w_out"][0], False, [True], "grads_l0_ffn1_down_wait")
        res["ffn1_w_down"] = adam("ffn1_w_down", [(r0, 0), (r1, 2)], 176)
        g0, = _xfer_wait(self.sent_gu, res["ffn1_w_down"][0], False, [True], "grads_l0_ffn1_gate_up_wait")
        res["ffn1_w_gate"] = adam("ffn1_w_gate", [(g0, 0), (r1, 0)], 176)
        res["ffn1_w_up"] = adam("ffn1_w_up", [(g0, 1), (r1, 1)], 176)
        packed = _adamw(rs, _pack_small(w), _pack_small(m), _pack_small(v), "adamw_small")
        unpacked = [_unpack_small(b, w) for b in packed]
        for n in SMALL:
            res[n] = [u[n] for u in unpacked]
        return res


def kernel(x, ffn1_norm, ffn1_w_gate, ffn1_w_up, ffn1_w_down, mix_norm, w_in, pool_w, pool_scale, forget_bias, conv_w, conv_b, conv_ln_g, conv_ln_b, w_out, ffn2_norm, ffn2_w_gate, ffn2_w_up, ffn2_w_down, final_norm, loss_target, m_ffn1_norm, m_ffn1_w_gate, m_ffn1_w_up, m_ffn1_w_down, m_mix_norm, m_w_in, m_pool_w, m_pool_scale, m_forget_bias, m_conv_w, m_conv_b, m_conv_ln_g, m_conv_ln_b, m_w_out, m_ffn2_norm, m_ffn2_w_gate, m_ffn2_w_up, m_ffn2_w_down, m_final_norm, v_ffn1_norm, v_ffn1_w_gate, v_ffn1_w_up, v_ffn1_w_down, v_mix_norm, v_w_in, v_pool_w, v_pool_scale, v_forget_bias, v_conv_w, v_conv_b, v_conv_ln_g, v_conv_ln_b, v_w_out, v_ffn2_norm, v_ffn2_w_gate, v_ffn2_w_up, v_ffn2_w_down, v_final_norm):
    w = dict(zip(WEIGHTS, (ffn1_norm, ffn1_w_gate, ffn1_w_up, ffn1_w_down, mix_norm, w_in, pool_w, pool_scale, forget_bias,
                           conv_w, conv_b, conv_ln_g, conv_ln_b, w_out, ffn2_norm, ffn2_w_gate, ffn2_w_up, ffn2_w_down,
                           final_norm)))
    m = dict(zip(WEIGHTS, (m_ffn1_norm, m_ffn1_w_gate, m_ffn1_w_up, m_ffn1_w_down, m_mix_norm, m_w_in, m_pool_w, m_pool_scale,
                           m_forget_bias, m_conv_w, m_conv_b, m_conv_ln_g, m_conv_ln_b, m_w_out, m_ffn2_norm, m_ffn2_w_gate,
                           m_ffn2_w_up, m_ffn2_w_down, m_final_norm)))
    v = dict(zip(WEIGHTS, (v_ffn1_norm, v_ffn1_w_gate, v_ffn1_w_up, v_ffn1_w_down, v_mix_norm, v_w_in, v_pool_w, v_pool_scale,
                           v_forget_bias, v_conv_w, v_conv_b, v_conv_ln_g, v_conv_ln_b, v_w_out, v_ffn2_norm, v_ffn2_w_gate,
                           v_ffn2_w_up, v_ffn2_w_down, v_final_norm)))
    comm = _Comm(w)
    loss_row, gx = _local_step(x[0], loss_target[0], w, comm.weights_for, comm.grads_ready)
    loss = lax.psum(loss_row[0, 0], ("x", "y", "c"))
    res = comm.finish(m, v, gx)
    return (loss, gx[None], *[res[n][i] for i in range(4) for n in WEIGHTS])
```

```python
import math

import numpy as np
import jax
import jax.numpy as jnp
from jax import lax
from jax.experimental import pallas as pl
from jax.experimental.pallas import tpu as pltpu

F32 = jnp.float32
CDT = jnp.bfloat16
NORM_EPS = 1e-6
N_DEV = 8
LANES = 128
PACK_ALIGN = 8 * LANES
VMEM_LIMIT = 48 * 1024 * 1024

POOL_WINDOWS = (2, 4, 8, 16)
POOL_HALO = 16
CONV_K = 31
CONV_HALO = 32
HEAD_DIM = 64
N_HEADS = 8
N_PAIRS = N_HEADS // 2
ATT_SCALE = 1.0 / math.sqrt(HEAD_DIM)
NEG = -1e30

ADAM_LR, ADAM_B1, ADAM_B2, ADAM_EPS, ADAM_WD, ADAM_STEP = 0.001, 0.9, 0.999, 1e-08, 0.01, 10

REST_W = 896
REST_Z_BLK = 6


def _cp(sem):
    return pltpu.CompilerParams(dimension_semantics=sem, vmem_limit_bytes=VMEM_LIMIT)


def _tile(n, pref):
    t = min(n, pref)
    assert n % t == 0, (n, pref)
    return t


def _sigmoid(x):
    return 1.0 / (1.0 + jnp.exp(-x))


def _sds(shape, dtype):
    return jax.ShapeDtypeStruct(shape, dtype)


_ANY = pl.BlockSpec(memory_space=pl.ANY)


def _dep(dep):
    return ([], []) if dep is None else ([_ANY], [dep])


def _wshape(w):
    return w[0].shape[1:] if isinstance(w, tuple) else w.shape


def _wspec(w, block, index_map):
    if not isinstance(w, tuple):
        return w, pl.BlockSpec(block, index_map)
    arr, t = w
    return arr, pl.BlockSpec((None,) + block, lambda *g: (t,) + index_map(*g))


def _rms_fwd(x, g, name, dep=None):
    T, D = x.shape
    tm = _tile(T, 1024)

    def body(x_ref, g_ref, *rest):
        o_ref = rest[-1]
        xv = x_ref[...]
        r = lax.rsqrt(jnp.mean(xv * xv, axis=-1, keepdims=True) + NORM_EPS)
        o_ref[...] = (xv * r * g_ref[...]).astype(o_ref.dtype)

    dspec, darg = _dep(dep)
    return pl.pallas_call(
        body, grid=(T // tm,),
        in_specs=[pl.BlockSpec((tm, D), lambda i: (i, 0)), pl.BlockSpec((1, D), lambda i: (0, 0))] + dspec,
        out_specs=pl.BlockSpec((tm, D), lambda i: (i, 0)),
        out_shape=_sds((T, D), CDT), name=name, compiler_params=_cp(("parallel",)))(x, g, *darg)


def _rms_bwd(x, g, dh, gres, name):
    T, D = x.shape
    tm = _tile(T, 512)

    def body(x_ref, g_ref, dh_ref, gres_ref, gin_ref, dg_ref):
        i = pl.program_id(0)
        xv = x_ref[...]
        d = dh_ref[...]
        r = lax.rsqrt(jnp.mean(xv * xv, axis=-1, keepdims=True) + NORM_EPS)
        xh = xv * r
        dxh = d * g_ref[...]
        c = jnp.mean(dxh * xh, axis=-1, keepdims=True)
        gin_ref[...] = gres_ref[...] + r * (dxh - xh * c)
        part = jnp.sum(d * xh, axis=0, keepdims=True)

        @pl.when(i == 0)
        def _():
            dg_ref[...] = part

        @pl.when(i > 0)
        def _():
            dg_ref[...] += part

    row = pl.BlockSpec((tm, D), lambda i: (i, 0))
    vec = pl.BlockSpec((1, D), lambda i: (0, 0))
    return pl.pallas_call(
        body, grid=(T // tm,), in_specs=[row, vec, row, row], out_specs=[row, vec],
        out_shape=[_sds((T, D), F32), _sds((1, D), F32)], name=name, compiler_params=_cp(("arbitrary",)))(x, g, dh, gres)


def _loss_bwd(x, g, target, name):
    T, D = x.shape
    tm = _tile(T, 512)

    def body(x_ref, g_ref, t_ref, loss_ref, dx_ref, dg_ref):
        i = pl.program_id(0)
        xv = x_ref[...]
        gv = g_ref[...]
        r = lax.rsqrt(jnp.mean(xv * xv, axis=-1, keepdims=True) + NORM_EPS)
        xh = xv * r
        err = xh * gv - t_ref[...]
        lpart = 0.5 * jnp.sum(jnp.mean(err * err, axis=-1, keepdims=True), axis=0, keepdims=True)
        dy = err * (1.0 / D)
        dxh = dy * gv
        c = jnp.mean(dxh * xh, axis=-1, keepdims=True)
        dx_ref[...] = r * (dxh - xh * c)
        part = jnp.sum(dy * xh, axis=0, keepdims=True)
        lrow = jnp.broadcast_to(lpart, (1, LANES))

        @pl.when(i == 0)
        def _():
            dg_ref[...] = part
            loss_ref[...] = lrow

        @pl.when(i > 0)
        def _():
            dg_ref[...] += part
            loss_ref[...] += lrow

    row = pl.BlockSpec((tm, D), lambda i: (i, 0))
    vec = pl.BlockSpec((1, D), lambda i: (0, 0))
    return pl.pallas_call(
        body, grid=(T // tm,), in_specs=[row, vec, row],
        out_specs=[pl.BlockSpec((1, LANES), lambda i: (0, 0)), row, vec],
        out_shape=[_sds((1, LANES), F32), _sds((T, D), F32), _sds((1, D), F32)],
        name=name, compiler_params=_cp(("arbitrary",)))(x, g, target)


def _mm(pairs, *, name, res=None, alpha=1.0, out_dtype=F32, tm=512, tn=None, dep=None):
    T = pairs[0][0].shape[0]
    N = _wshape(pairs[0][1])[0 if pairs[0][2] else 1]
    tm = _tile(T, tm)
    tn = N if tn is None else _tile(N, tn)
    flags = [p[2] for p in pairs]
    n_in = 2 * len(pairs)

    def body(*refs):
        o_ref = refs[-1]
        acc = None
        for p, bt in enumerate(flags):
            a = refs[2 * p][...].astype(CDT)
            b = refs[2 * p + 1][...]
            dims = (((1,), (1,)), ((), ())) if bt else (((1,), (0,)), ((), ()))
            d = lax.dot_general(a, b, dims, preferred_element_type=F32)
            acc = d if acc is None else acc + d
        if alpha != 1.0:
            acc = acc * alpha
        if res is not None:
            acc = refs[n_in][...] + acc
        o_ref[...] = acc.astype(o_ref.dtype)

    in_specs, args = [], []
    for a, b, bt in pairs:
        K = a.shape[1]
        in_specs.append(pl.BlockSpec((tm, K), lambda i, j: (i, 0)))
        b, bspec = _wspec(b, (tn, K), lambda i, j: (j, 0)) if bt else _wspec(b, (K, tn), lambda i, j: (0, j))
        in_specs.append(bspec)
        args += [a, b]
    if res is not None:
        in_specs.append(pl.BlockSpec((tm, tn), lambda i, j: (i, j)))
        args.append(res)
    dspec, darg = _dep(dep)
    in_specs += dspec
    args += darg
    return pl.pallas_call(
        body, grid=(T // tm, N // tn), in_specs=in_specs,
        out_specs=pl.BlockSpec((tm, tn), lambda i, j: (i, j)),
        out_shape=_sds((T, N), out_dtype), name=name, compiler_params=_cp(("parallel", "arbitrary")))(*args)


def _mm_norm_bwd(pairs, x, g, gres, *, name, tm=256, dep=None):
    T, D = x.shape
    tm = _tile(T, tm)
    n_in = 2 * len(pairs)
    flags = [p[2] for p in pairs]

    def body(*refs):
        x_ref, g_ref, gres_ref = refs[n_in:n_in + 3]
        gin_ref, dg_ref = refs[-2:]
        i = pl.program_id(0)
        d = None
        for p, bt in enumerate(flags):
            dims = (((1,), (1,)), ((), ())) if bt else (((1,), (0,)), ((), ()))
            part = lax.dot_general(refs[2 * p][...].astype(CDT), refs[2 * p + 1][...], dims, preferred_element_type=F32)
            d = part if d is None else d + part
        xv = x_ref[...]
        r = lax.rsqrt(jnp.mean(xv * xv, axis=-1, keepdims=True) + NORM_EPS)
        xh = xv * r
        dxh = d * g_ref[...]
        c = jnp.mean(dxh * xh, axis=-1, keepdims=True)
        gin_ref[...] = gres_ref[...] + r * (dxh - xh * c)
        part = jnp.sum(d * xh, axis=0, keepdims=True)

        @pl.when(i == 0)
        def _():
            dg_ref[...] = part

        @pl.when(i > 0)
        def _():
            dg_ref[...] += part

    in_specs, args = [], []
    for a, b, bt in pairs:
        K = a.shape[1]
        b, bspec = _wspec(b, tuple(_wshape(b)), lambda i: (0, 0))
        in_specs += [pl.BlockSpec((tm, K), lambda i: (i, 0)), bspec]
        args += [a, b]
    row = pl.BlockSpec((tm, D), lambda i: (i, 0))
    vec = pl.BlockSpec((1, D), lambda i: (0, 0))
    dspec, darg = _dep(dep)
    return pl.pallas_call(
        body, grid=(T // tm,), in_specs=in_specs + [row, vec, row] + dspec, out_specs=[row, vec],
        out_shape=[_sds((T, D), F32), _sds((1, D), F32)], name=name,
        compiler_params=_cp(("arbitrary",)))(*args, x, g, gres, *darg)


def _mm_tn(a, b, *, name, alpha=1.0, tk=2048, dep=None):
    T, M = a.shape
    N = b.shape[1]
    tm = M if M <= 1024 else M // 2
    tn = N if N <= 1536 else N // 2
    assert M % tm == 0 and N % tn == 0 and tm % LANES == 0 and tn % LANES == 0
    tk = _tile(T, tk)
    nk = T // tk

    def body(a_ref, b_ref, *rest):
        o_ref = rest[-1]
        k = pl.program_id(2)
        d = lax.dot_general(a_ref[...].astype(CDT), b_ref[...].astype(CDT), (((0,), (0,)), ((), ())),
                            preferred_element_type=F32)

        @pl.when(k == 0)
        def _():
            o_ref[...] = d

        @pl.when(k > 0)
        def _():
            o_ref[...] += d

        if alpha != 1.0:
            @pl.when(k == nk - 1)
            def _():
                o_ref[...] *= alpha

    dspec, darg = _dep(dep)
    return pl.pallas_call(
        body, grid=(M // tm, N // tn, nk),
        in_specs=[pl.BlockSpec((tk, tm), lambda i, j, k: (k, i)), pl.BlockSpec((tk, tn), lambda i, j, k: (k, j))] + dspec,
        out_specs=pl.BlockSpec((tm, tn), lambda i, j, k: (i, j)),
        out_shape=_sds((M, N), F32), name=name, compiler_params=_cp(("parallel", "parallel", "arbitrary")))(a, b, *darg)


def _ffn_up(h, wgt, wut, name):
    T, D = h.shape
    Fh = _wshape(wgt)[0]
    tm = _tile(T, 2048)
    tn = _tile(Fh, 256)
    nt = (((1,), (1,)), ((), ()))

    def body(h_ref, wg_ref, wu_ref, a_ref, b_ref, s_ref):
        hv = h_ref[...]
        a = lax.dot_general(hv, wg_ref[...], nt, preferred_element_type=F32)
        b = lax.dot_general(hv, wu_ref[...], nt, preferred_element_type=F32)
        a_ref[...] = a.astype(a_ref.dtype)
        b_ref[...] = b.astype(b_ref.dtype)
        s_ref[...] = (a * _sigmoid(a) * b).astype(s_ref.dtype)

    wgt, gspec = _wspec(wgt, (tn, D), lambda i, j: (j, 0))
    wut, uspec = _wspec(wut, (tn, D), lambda i, j: (j, 0))
    ospec = pl.BlockSpec((tm, tn), lambda i, j: (i, j))
    return pl.pallas_call(
        body, grid=(T // tm, Fh // tn),
        in_specs=[pl.BlockSpec((tm, D), lambda i, j: (i, 0)), gspec, uspec],
        out_specs=[ospec, ospec, ospec],
        out_shape=[_sds((T, Fh), CDT), _sds((T, Fh), CDT), _sds((T, Fh), CDT)],
        name=name, compiler_params=_cp(("parallel", "arbitrary")))(h, wgt, wut)


def _ffn_bwd_ds(gout, wd, a, b, name, dep=None):
    T, D = gout.shape
    Fh = _wshape(wd)[0]
    tm = _tile(T, 2048)
    tn = _tile(Fh, 256)

    def body(g_ref, wd_ref, a_ref, b_ref, *rest):
        da_ref, db_ref = rest[-2:]
        dy = (0.5 * g_ref[...]).astype(CDT)
        ds = lax.dot_general(dy, wd_ref[...], (((1,), (1,)), ((), ())), preferred_element_type=F32)
        av = a_ref[...].astype(F32)
        sg = _sigmoid(av)
        da_ref[...] = (ds * b_ref[...].astype(F32) * (sg * (1.0 + av * (1.0 - sg)))).astype(da_ref.dtype)
        db_ref[...] = (ds * (av * sg)).astype(db_ref.dtype)

    ospec = pl.BlockSpec((tm, tn), lambda i, j: (i, j))
    dspec, darg = _dep(dep)
    wd, wspec = _wspec(wd, (tn, D), lambda i, j: (j, 0))
    return pl.pallas_call(
        body, grid=(T // tm, Fh // tn),
        in_specs=[pl.BlockSpec((tm, D), lambda i, j: (i, 0)), wspec, ospec, ospec] + dspec,
        out_specs=[ospec, ospec],
        out_shape=[_sds((T, Fh), CDT), _sds((T, Fh), CDT)],
        name=name, compiler_params=_cp(("parallel", "arbitrary")))(gout, wd, a, b, *darg)


def _ffn_fwd(x, gamma, wgt, wut, wd, tag, dep=None):
    h = _rms_fwd(x, gamma, f"{tag}_norm", dep)
    a, b, s = _ffn_up(h, wgt, wut, f"{tag}_up")
    y = _mm([(s, wd, False)], res=x, alpha=0.5, name=f"{tag}_down")
    return y, (x, h, a, b, s)


def _ffn_bwd(saved, gamma, wgt, wut, wd, gout, tag, dep, on_grads):
    x, h, a, b, s = saved
    dwd = _mm_tn(s, gout, alpha=0.5, name=f"{tag}_dwd", dep=dep)
    da, db = _ffn_bwd_ds(gout, wd, a, b, f"{tag}_bwd_ds", on_grads(dict(w_down=dwd)))
    dwgt = _mm_tn(da, h, name=f"{tag}_dwg")
    dwut = _mm_tn(db, h, name=f"{tag}_dwu")
    dep = on_grads(dict(w_gate=dwgt, w_up=dwut))
    return _mm_norm_bwd([(da, wgt, False), (db, wut, False)], x, gamma, gout, name=f"{tag}_dh_norm_bwd", dep=dep)


def _fgate_fwd(rest, bias, name, bt=512):
    T = rest.shape[0]
    bt = _tile(T, bt)

    def body(z_ref, b_ref, fc_ref, ft_ref, carry):
        i = pl.program_id(0)

        @pl.when(i == 0)
        def _():
            carry[...] = jnp.zeros_like(carry)

        zb = z_ref[...] + b_ref[...]
        e = jnp.exp(-jnp.abs(zb))
        u = 1.0 + e
        log1p_e = jnp.where(u == 1.0, e, jnp.log(u) * (e / (u - 1.0)))
        x = jnp.minimum(zb, 0.0) - log1p_e
        row = lax.broadcasted_iota(jnp.int32, x.shape, 0)
        sh = 1
        while sh < bt:
            x = x + jnp.where(row >= sh, pltpu.roll(x, sh, 0), 0.0)
            sh *= 2
        f = x + carry[...]
        carry[...] = f[bt - 1:bt, :]
        fc_ref[...] = f
        ft_ref[...] = jnp.transpose(f)[0:N_HEADS, :]

    return pl.pallas_call(
        body, grid=(T // bt,),
        in_specs=[pl.BlockSpec((bt, LANES), lambda i: (i, REST_Z_BLK)), pl.BlockSpec((1, LANES), lambda i: (0, 0))],
        out_specs=[pl.BlockSpec((bt, LANES), lambda i: (i, 0)), pl.BlockSpec((N_HEADS, bt), lambda i: (0, i))],
        out_shape=[_sds((T, LANES), F32), _sds((N_HEADS, T), F32)],
        scratch_shapes=[pltpu.VMEM((1, LANES), F32)],
        name=name, compiler_params=_cp(("arbitrary",)))(rest, bias)


def _fgate_bwd(dfk, rest, bias, name, bt=512):
    T = rest.shape[0]
    bt = _tile(T, bt)
    nb = T // bt

    def body(df_ref, z_ref, b_ref, dz_ref, db_ref, carry):
        i = pl.program_id(0)

        @pl.when(i == 0)
        def _():
            carry[...] = jnp.zeros_like(carry)

        dfv = df_ref[...]
        lane = lax.broadcasted_iota(jnp.int32, (bt, LANES), 1)
        x = jnp.zeros((bt, LANES), F32)
        for h in range(N_HEADS):
            x = jnp.where(lane == h, dfv[:, HEAD_DIM * h:HEAD_DIM * h + 1], x)
        row = lax.broadcasted_iota(jnp.int32, x.shape, 0)
        sh = 1
        while sh < bt:
            x = x + jnp.where(row + sh < bt, pltpu.roll(x, bt - sh, 0), 0.0)
            sh *= 2
        dlf = x + carry[...]
        carry[...] = dlf[0:1, :]
        zb = z_ref[...] + b_ref[...]
        dz = jnp.where(lane < N_HEADS, dlf * _sigmoid(-zb), 0.0)
        dz_ref[...] = dz.astype(dz_ref.dtype)
        part = jnp.sum(dz, axis=0, keepdims=True)

        @pl.when(i == 0)
        def _():
            db_ref[...] = part

        @pl.when(i > 0)
        def _():
            db_ref[...] += part

    return pl.pallas_call(
        body, grid=(nb,),
        in_specs=[pl.BlockSpec((bt, 4 * LANES), lambda i: (nb - 1 - i, 0)),
                  pl.BlockSpec((bt, LANES), lambda i: (nb - 1 - i, REST_Z_BLK)),
                  pl.BlockSpec((1, LANES), lambda i: (0, 0))],
        out_specs=[pl.BlockSpec((bt, LANES), lambda i: (nb - 1 - i, 0)), pl.BlockSpec((1, LANES), lambda i: (0, 0))],
        out_shape=[_sds((T, LANES), CDT), _sds((1, LANES), F32)],
        scratch_shapes=[pltpu.VMEM((1, LANES), F32)],
        name=name, compiler_params=_cp(("arbitrary",)))(dfk, rest, bias)


def _by_group(vals, lane):
    out = vals[-1]
    for g in range(len(vals) - 2, -1, -1):
        out = jnp.where(lane // 64 == g, vals[g], out)
    return out


def _pool_counts(t0, n, lane):
    t = t0 + lax.broadcasted_iota(jnp.int32, (n, 256), 0)
    return _by_group([jnp.minimum(t + 1, w) for w in POOL_WINDOWS], lane).astype(F32)


def _pooled(u, halo, i, bt):
    lane = lax.broadcasted_iota(jnp.int32, (bt, 256), 1)
    ext = jnp.concatenate([jnp.where(i > 0, halo, 0.0), u], axis=0)
    sums, s, sh = [], ext, 1
    for _ in POOL_WINDOWS:
        s = s + pltpu.roll(s, sh, 0)
        sums.append(s[POOL_HALO:, :])
        sh *= 2
    return _by_group(sums, lane) / _pool_counts(i * bt, bt, lane) - u


def _pool_fwd(rest, wbd, scale, name, bt=512):
    T = rest.shape[0]
    bt = _tile(T, bt)
    hb = bt // POOL_HALO

    def body(u_ref, halo_ref, w_ref, sc_ref, o_ref):
        i = pl.program_id(0)
        pooled = _pooled(u_ref[...], halo_ref[...], i, bt)
        mixed = jnp.dot(pooled.astype(CDT), w_ref[...], preferred_element_type=F32)
        o_ref[...] = (mixed * sc_ref[...]).astype(o_ref.dtype)

    return pl.pallas_call(
        body, grid=(T // bt,),
        in_specs=[pl.BlockSpec((bt, 256), lambda i: (i, 0)),
                  pl.BlockSpec((POOL_HALO, 256), lambda i: (jnp.maximum(i * hb - 1, 0), 0)),
                  pl.BlockSpec((256, 256), lambda i: (0, 0)), pl.BlockSpec((1, 256), lambda i: (0, 0))],
        out_specs=pl.BlockSpec((bt, 256), lambda i: (i, 0)),
        out_shape=_sds((T, 256), CDT), name=name, compiler_params=_cp(("parallel",)))(rest, rest, wbd, scale)


def _pool_bwd(dcat, rest, wbd, scale, name, bt=512):
    T = rest.shape[0]
    bt = _tile(T, bt)
    hb = bt // POOL_HALO
    nb = T // bt
    n = bt + POOL_HALO

    def body(dy_ref, dyn_ref, u_ref, halo_ref, w_ref, sc_ref, du_ref, dw_ref, dsc_ref):
        i = pl.program_id(0)
        lane = lax.broadcasted_iota(jnp.int32, (bt, 256), 1)
        w = w_ref[...]
        sc = sc_ref[...]
        pooled = _pooled(u_ref[...], halo_ref[...], i, bt)
        pooled_c = pooled.astype(CDT)
        mixed = jnp.dot(pooled_c, w, preferred_element_type=F32)
        dy = dy_ref[...]
        dm = (dy * sc).astype(CDT)
        dsc = jnp.sum(dy * mixed, axis=0, keepdims=True)
        dw = lax.dot_general(pooled_c, dm, (((0,), (0,)), ((), ())), preferred_element_type=F32)
        nt = (((1,), (1,)), ((), ()))
        dpl = lax.dot_general(dm, w, nt, preferred_element_type=F32)
        dmn = (jnp.where(i < nb - 1, dyn_ref[...], 0.0) * sc).astype(CDT)
        dpln = lax.dot_general(dmn, w, nt, preferred_element_type=F32)
        lane_h = lax.broadcasted_iota(jnp.int32, (POOL_HALO, 256), 1)
        ext = jnp.concatenate([dpl / _pool_counts(i * bt, bt, lane),
                               dpln / _pool_counts((i + 1) * bt, POOL_HALO, lane_h)], axis=0)
        sums, s, sh = [], ext, 1
        for _ in POOL_WINDOWS:
            s = s + pltpu.roll(s, n - sh, 0)
            sums.append(s[0:bt, :])
            sh *= 2
        du_ref[...] = (_by_group(sums, lane) - dpl).astype(du_ref.dtype)

        @pl.when(i == 0)
        def _():
            dw_ref[...] = dw
            dsc_ref[...] = dsc

        @pl.when(i > 0)
        def _():
            dw_ref[...] += dw
            dsc_ref[...] += dsc

    full = pl.BlockSpec((256, 256), lambda i: (0, 0))
    vec = pl.BlockSpec((1, 256), lambda i: (0, 0))
    return pl.pallas_call(
        body, grid=(nb,),
        in_specs=[pl.BlockSpec((bt, 256), lambda i: (i, 0)),
                  pl.BlockSpec((POOL_HALO, 256), lambda i: (jnp.minimum((i + 1) * hb, nb * hb - 1), 0)),
                  pl.BlockSpec((bt, 256), lambda i: (i, 0)),
                  pl.BlockSpec((POOL_HALO, 256), lambda i: (jnp.maximum(i * hb - 1, 0), 0)),
                  full, vec],
        out_specs=[pl.BlockSpec((bt, 256), lambda i: (i, 0)), full, vec],
        out_shape=[_sds((T, 256), CDT), _sds((256, 256), F32), _sds((1, 256), F32)],
        name=name, compiler_params=_cp(("arbitrary",)))(dcat, dcat, rest, rest, wbd, scale)


def _glu_ext(a_ref, g_ref, ah_ref, gh_ref, i):
    u = a_ref[...] * _sigmoid(g_ref[...])
    uh = jnp.where(i > 0, ah_ref[...] * _sigmoid(gh_ref[...]), 0.0)
    return jnp.concatenate([uh, u], axis=0)


def _conv_fwd(rest, cw, cb, lg, lb, name, bt=512):
    T = rest.shape[0]
    bt = _tile(T, bt)
    hb = bt // CONV_HALO

    def body(a_ref, g_ref, ah_ref, gh_ref, cw_ref, cb_ref, lg_ref, lb_ref, o_ref, y_ref):
        i = pl.program_id(0)
        ext = _glu_ext(a_ref, g_ref, ah_ref, gh_ref, i)
        w = cw_ref[...]
        acc = w[CONV_K - 1:CONV_K, :] * ext
        for k in range(CONV_K - 1):
            acc = acc + w[k:k + 1, :] * pltpu.roll(ext, CONV_K - 1 - k, 0)
        y = acc[CONV_HALO:, :] + cb_ref[...]
        y_ref[...] = y
        yc = y - jnp.mean(y, axis=-1, keepdims=True)
        yn = yc * lax.rsqrt(jnp.mean(yc * yc, axis=-1, keepdims=True) + NORM_EPS)
        z = yn * lg_ref[...] + lb_ref[...]
        o_ref[...] = (z * _sigmoid(z)).astype(o_ref.dtype)

    def cur(c):
        return pl.BlockSpec((bt, 256), lambda i: (i, c))

    def prev(c):
        return pl.BlockSpec((CONV_HALO, 256), lambda i: (jnp.maximum(i * hb - 1, 0), c))

    vec = pl.BlockSpec((1, 256), lambda i: (0, 0))
    return pl.pallas_call(
        body, grid=(T // bt,),
        in_specs=[cur(1), cur(2), prev(1), prev(2), pl.BlockSpec((CONV_HALO, 256), lambda i: (0, 0)), vec, vec, vec],
        out_specs=[pl.BlockSpec((bt, 256), lambda i: (i, 0)), pl.BlockSpec((bt, 256), lambda i: (i, 0))],
        out_shape=[_sds((T, 256), CDT), _sds((T, 256), F32)],
        name=name, compiler_params=_cp(("parallel",)))(rest, rest, rest, rest, cw, cb, lg, lb)


def _conv_bwd(dcat, yconv, rest, cw, lg, lb, name, bt=512):
    T = rest.shape[0]
    bt = _tile(T, bt)
    hb = bt // CONV_HALO
    nb = T // bt
    n = bt + CONV_HALO

    def body(dy_ref, dyn_ref, y_ref, yn_ref, a_ref, g_ref, ah_ref, gh_ref, cw_ref, lg_ref, lb_ref,
             da_ref, dg_ref, dcw_ref, dcb_ref, dlg_ref, dlb_ref):
        i = pl.program_id(0)
        lgv = lg_ref[...]
        lbv = lb_ref[...]

        def ln_swish_bwd(dout, y):
            yc = y - jnp.mean(y, axis=-1, keepdims=True)
            rs = lax.rsqrt(jnp.mean(yc * yc, axis=-1, keepdims=True) + NORM_EPS)
            yn = yc * rs
            z = yn * lgv + lbv
            sg = _sigmoid(z)
            dz = dout * (sg * (1.0 + z * (1.0 - sg)))
            dyn = dz * lgv
            dyc = rs * (dyn - jnp.mean(dyn, axis=-1, keepdims=True) - yn * jnp.mean(dyn * yn, axis=-1, keepdims=True))
            return dyc, dz, yn

        dyc, dz, yn = ln_swish_bwd(dy_ref[...], y_ref[...])
        dyc_next, _, _ = ln_swish_bwd(dyn_ref[...], yn_ref[...])
        dyc_next = jnp.where(i < nb - 1, dyc_next, 0.0)
        ext_u = _glu_ext(a_ref, g_ref, ah_ref, gh_ref, i)
        ext_d = jnp.concatenate([dyc, dyc_next], axis=0)
        w = cw_ref[...]
        du = w[CONV_K - 1:CONV_K, :] * ext_d
        rows = []
        for k in range(CONV_K):
            s = CONV_K - 1 - k
            if s > 0:
                du = du + w[k:k + 1, :] * pltpu.roll(ext_d, n - s, 0)
                us = pltpu.roll(ext_u, s, 0)[CONV_HALO:, :]
            else:
                us = ext_u[CONV_HALO:, :]
            rows.append(jnp.sum(dyc * us, axis=0, keepdims=True))
        rows.append(jnp.zeros((1, 256), F32))
        dcw = jnp.concatenate(rows, axis=0)
        du = du[0:bt, :]
        av = a_ref[...]
        sg = _sigmoid(g_ref[...])
        da_ref[...] = (du * sg).astype(da_ref.dtype)
        dg_ref[...] = (du * av * (sg * (1.0 - sg))).astype(dg_ref.dtype)
        dcb = jnp.sum(dyc, axis=0, keepdims=True)
        dlg = jnp.sum(dz * yn, axis=0, keepdims=True)
        dlb = jnp.sum(dz, axis=0, keepdims=True)

        @pl.when(i == 0)
        def _():
            dcw_ref[...] = dcw
            dcb_ref[...] = dcb
            dlg_ref[...] = dlg
            dlb_ref[...] = dlb

        @pl.when(i > 0)
        def _():
            dcw_ref[...] += dcw
            dcb_ref[...] += dcb
            dlg_ref[...] += dlg
            dlb_ref[...] += dlb

    def cur(c):
        return pl.BlockSpec((bt, 256), lambda i: (i, c))

    def prev(c):
        return pl.BlockSpec((CONV_HALO, 256), lambda i: (jnp.maximum(i * hb - 1, 0), c))

    def nxt(c):
        return pl.BlockSpec((CONV_HALO, 256), lambda i: (jnp.minimum((i + 1) * hb, nb * hb - 1), c))

    vec = pl.BlockSpec((1, 256), lambda i: (0, 0))
    wfull = pl.BlockSpec((CONV_HALO, 256), lambda i: (0, 0))
    return pl.pallas_call(
        body, grid=(nb,),
        in_specs=[cur(3), nxt(3), cur(0), nxt(0), cur(1), cur(2), prev(1), prev(2), wfull, vec, vec],
        out_specs=[cur(0), cur(0), wfull, vec, vec, vec],
        out_shape=[_sds((T, 256), CDT), _sds((T, 256), CDT), _sds((CONV_HALO, 256), F32),
                   _sds((1, 256), F32), _sds((1, 256), F32), _sds((1, 256), F32)],
        name=name, compiler_params=_cp(("arbitrary",)))(dcat, dcat, yconv, yconv, rest, rest, rest, rest, cw, lg, lb)


def _half_mask(shape, a):
    lane = lax.broadcasted_iota(jnp.int32, shape, 1)
    return (lane // HEAD_DIM) == a


def _attn_fwd(qkv, fcol, frow, name, blk=1024):
    T = qkv.shape[0]
    blk = _tile(T, blk)
    nq = T // blk
    nt = (((1,), (1,)), ((), ()))

    def body(q_ref, k_ref, v_ref, fc_ref, fr_ref, o_ref, lse_ref):
        p_id = pl.program_id(0)
        i = pl.program_id(1)
        q2 = q_ref[...]
        fc = fc_ref[...]
        lane = lax.broadcasted_iota(jnp.int32, (blk, LANES), 1)
        tri = lax.broadcasted_iota(jnp.int32, (blk, blk), 1) <= lax.broadcasted_iota(jnp.int32, (blk, blk), 0)
        masks = [_half_mask(q2.shape, a) for a in range(2)]
        qs = [jnp.where(hm, q2, jnp.zeros_like(q2)) * ATT_SCALE for hm in masks]
        fqs = [jnp.sum(jnp.where(lane == 2 * p_id + a, fc, 0.0), axis=1, keepdims=True) for a in range(2)]

        def tile(j, carry, masked):
            cols = pl.ds(pl.multiple_of(j * blk, blk), blk)
            kj = k_ref[cols, :]
            vj = v_ref[cols, :]
            out = []
            for a in range(2):
                m, acc = carry[2 * a:2 * a + 2]
                va = jnp.where(masks[a], vj, jnp.ones_like(vj))
                s = lax.dot_general(qs[a], kj, nt, preferred_element_type=F32) + (fqs[a] - fr_ref[a:a + 1, cols])
                if masked:
                    s = jnp.where(tri, s, NEG)
                m_new = jnp.maximum(m, jnp.max(s, axis=1, keepdims=True))
                alpha = jnp.exp(m - m_new)
                pr = jnp.exp(s - m_new)
                hi = lax.bitcast_convert_type(lax.bitcast_convert_type(pr, jnp.uint32) & jnp.uint32(0xFFFF0000), F32)
                pv = (jnp.dot(hi.astype(CDT), va, preferred_element_type=F32)
                      + jnp.dot((pr - hi).astype(CDT), va, preferred_element_type=F32))
                out += [m_new, alpha * acc + pv]
            return tuple(out)

        init = (jnp.full((blk, 1), NEG, F32), jnp.zeros((blk, LANES), F32)) * 2
        carry = lax.fori_loop(0, i, lambda j, c: tile(j, c, False), init)
        carry = tile(i, carry, True)
        ls = [carry[1][:, HEAD_DIM:HEAD_DIM + 1], carry[3][:, 0:1]]
        lo = lane < HEAD_DIM
        o_ref[...] = jnp.where(lo, carry[1] / ls[0], carry[3] / ls[1])
        lse_t = jnp.transpose(jnp.where(lo, carry[0] + jnp.log(ls[0]), carry[2] + jnp.log(ls[1])))
        lse_ref[...] = jnp.concatenate([lse_t[0:1, :], lse_t[HEAD_DIM:HEAD_DIM + 1, :]], axis=0)

    return pl.pallas_call(
        body, grid=(N_PAIRS, nq),
        in_specs=[pl.BlockSpec((blk, LANES), lambda p, i: (i, p)),
                  pl.BlockSpec((T, LANES), lambda p, i: (0, N_PAIRS + p)),
                  pl.BlockSpec((T, LANES), lambda p, i: (0, 2 * N_PAIRS + p)),
                  pl.BlockSpec((blk, LANES), lambda p, i: (i, 0)),
                  pl.BlockSpec((None, 2, T), lambda p, i: (p, 0, 0))],
        out_specs=[pl.BlockSpec((blk, LANES), lambda p, i: (i, p)), pl.BlockSpec((None, 2, blk), lambda p, i: (p, 0, i))],
        out_shape=[_sds((T, N_PAIRS * LANES), F32), _sds((N_PAIRS, 2, T), F32)],
        name=name, compiler_params=_cp(("parallel", "arbitrary")))(qkv, qkv, qkv, fcol, frow)


def _attn_delta(dcat, o, name, blk=512):
    T = o.shape[0]
    blk = _tile(T, blk)

    def body(d_ref, o_ref, out_ref):
        prod = d_ref[:, 256:768].astype(CDT).astype(F32) * o_ref[...]
        pt = jnp.transpose(prod)
        out_ref[...] = jnp.sum(pt.reshape(N_HEADS, HEAD_DIM, blk), axis=1)

    return pl.pallas_call(
        body, grid=(T // blk,),
        in_specs=[pl.BlockSpec((blk, 1024), lambda i: (i, 0)), pl.BlockSpec((blk, 512), lambda i: (i, 0))],
        out_specs=pl.BlockSpec((N_HEADS, blk), lambda i: (0, i)),
        out_shape=_sds((N_HEADS, T), F32), name=name, compiler_params=_cp(("parallel",)))(dcat, o)


def _attn_bwd(qkv, dcat, fcol, frow, lse, delta, name, blk=1024):
    T = qkv.shape[0]
    blk = _tile(T, blk)
    nq = T // blk
    nt = (((1,), (1,)), ((), ()))

    def body(q_ref, do_ref, k_ref, v_ref, fc_ref, fr_ref, lse_ref, dl_ref, dqt_ref, dk_ref, dv_ref, df_ref):
        p_id = pl.program_id(0)
        j = pl.program_id(1)

        @pl.when(j == 0)
        def _():
            dqt_ref[...] = jnp.zeros_like(dqt_ref)

        k2 = k_ref[...]
        v2 = v_ref[...]
        fc = fc_ref[...]
        lane = lax.broadcasted_iota(jnp.int32, (blk, LANES), 1)
        tri = lax.broadcasted_iota(jnp.int32, (blk, blk), 0) <= lax.broadcasted_iota(jnp.int32, (blk, blk), 1)
        masks = [_half_mask(k2.shape, a) for a in range(2)]
        kas = [jnp.where(hm, k2, jnp.zeros_like(k2)) * ATT_SCALE for hm in masks]
        kats = [jnp.transpose(ka) for ka in kas]
        vas = [jnp.where(hm, v2, jnp.zeros_like(v2)) for hm in masks]
        fks = [jnp.sum(jnp.where(lane == 2 * p_id + a, fc, 0.0), axis=1, keepdims=True) for a in range(2)]

        def tile(i, carry, masked):
            rows = pl.ds(pl.multiple_of(i * blk, blk), blk)
            qi = q_ref[rows, :]
            doi = do_ref[rows, :].astype(CDT)
            out = []
            dqt = None
            for a in range(2):
                dk_acc, dv_acc, df_acc = carry[3 * a:3 * a + 3]
                st = lax.dot_general(kas[a], qi, nt, preferred_element_type=F32)
                e = (st + (fr_ref[a:a + 1, rows] - fks[a])) - lse_ref[a:a + 1, rows]
                if masked:
                    e = jnp.where(tri, e, NEG)
                pt = jnp.exp(e)
                dpt = lax.dot_general(vas[a], doi, nt, preferred_element_type=F32)
                ds32 = pt * (dpt - dl_ref[a:a + 1, rows])
                dst = ds32.astype(CDT)
                df_acc = df_acc + jnp.sum(ds32, axis=1, keepdims=True)
                dv_acc = dv_acc + jnp.dot(pt.astype(CDT), doi, preferred_element_type=F32)
                dk_acc = dk_acc + jnp.dot(dst, qi, preferred_element_type=F32)
                part = jnp.dot(kats[a], dst, preferred_element_type=F32)
                dqt = part if dqt is None else dqt + part
                out += [dk_acc, dv_acc, df_acc]
            dqt_ref[:, rows] += dqt
            return tuple(out)

        init = (jnp.zeros((blk, LANES), F32), jnp.zeros((blk, LANES), F32), jnp.zeros((blk, 1), F32)) * 2
        carry = tile(j, init, True)
        carry = lax.fori_loop(j + 1, nq, lambda i, c: tile(i, c, False), carry)
        lo = lane < HEAD_DIM
        dk_ref[...] = (jnp.where(lo, carry[0], carry[3]) * ATT_SCALE).astype(dk_ref.dtype)
        dv_ref[...] = jnp.where(lo, carry[1], carry[4]).astype(dv_ref.dtype)
        df_ref[...] = -jnp.where(lo, carry[2], carry[5])

    res = pl.BlockSpec((T, LANES), lambda p, j: (0, p))
    rows = pl.BlockSpec((None, 2, T), lambda p, j: (p, 0, 0))
    kv_out = pl.BlockSpec((blk, LANES), lambda p, j: (j, p))
    return pl.pallas_call(
        body, grid=(N_PAIRS, nq),
        in_specs=[res, pl.BlockSpec((T, LANES), lambda p, j: (0, 2 + p)),
                  pl.BlockSpec((blk, LANES), lambda p, j: (j, N_PAIRS + p)),
                  pl.BlockSpec((blk, LANES), lambda p, j: (j, 2 * N_PAIRS + p)),
                  pl.BlockSpec((blk, LANES), lambda p, j: (j, 0)), rows, rows, rows],
        out_specs=[pl.BlockSpec((LANES, T), lambda p, j: (p, 0)), kv_out, kv_out, kv_out],
        out_shape=[_sds((N_PAIRS * LANES, T), F32), _sds((T, N_PAIRS * LANES), CDT), _sds((T, N_PAIRS * LANES), CDT),
                   _sds((T, N_PAIRS * LANES), F32)],
        name=name, compiler_params=_cp(("parallel", "arbitrary")))(qkv, dcat, qkv, qkv, fcol, frow, lse, delta)


def _mixer_fwd(x, wts, tag, dep=None):
    T = x.shape[0]
    h = _rms_fwd(x, wts["mix_norm"], f"{tag}_norm", dep)
    qkv = _mm([(h, wts["win_qkv"], False)], out_dtype=CDT, tm=1024, tn=768, name=f"{tag}_in_qkv")
    rest = _mm([(h, wts["win_rest"], False)], tm=1024, name=f"{tag}_in_rest")
    fcol, frow8 = _fgate_fwd(rest, wts["fbias"], f"{tag}_fgate")
    frow = frow8.reshape(N_PAIRS, 2, T)
    ya = _pool_fwd(rest, wts["pool_wbd"], wts["pool_scale"], f"{tag}_pool")
    o, lse = _attn_fwd(qkv, fcol, frow, f"{tag}_attn")
    yc, yconv = _conv_fwd(rest, wts["conv_w"], wts["conv_b"], wts["conv_ln_g"], wts["conv_ln_b"], f"{tag}_conv")
    cat = jnp.concatenate([ya, o.astype(CDT), yc], axis=1)
    y = _mm([(cat, wts["w_out"], False)], res=x, name=f"{tag}_out")
    return y, (x, h, qkv, rest, fcol, frow, o, lse, yconv, cat)


def _mixer_bwd(saved, wts, gout, tag, dep=None):
    x, h, qkv, rest, fcol, frow, o, lse, yconv, cat = saved
    T = x.shape[0]
    dcat = _mm([(gout, wts["w_out"], True)], name=f"{tag}_dcat", dep=dep)
    dwout = _mm_tn(cat, gout, name=f"{tag}_dwout")
    du, dpw, dpsc = _pool_bwd(dcat, rest, wts["pool_wbd"], wts["pool_scale"], f"{tag}_pool_bwd")
    delta = _attn_delta(dcat, o, f"{tag}_attn_delta").reshape(N_PAIRS, 2, T)
    dqt, dk, dv, dfk = _attn_bwd(qkv, dcat, fcol, frow, lse, delta, f"{tag}_attn_bwd")
    dq = dqt.T.astype(CDT)
    dz, dfb = _fgate_bwd(dfk, rest, wts["fbias"], f"{tag}_fgate_bwd")
    da, dg, dcw, dcb, dlg, dlb = _conv_bwd(dcat, yconv, rest, wts["conv_w"], wts["conv_ln_g"], wts["conv_ln_b"],
                                           f"{tag}_conv_bwd")
    dp_qkv = jnp.concatenate([dq, dk, dv], axis=1).astype(CDT)
    dp_rest = jnp.concatenate([du, da, dg, dz], axis=1)
    dwin_qkv = _mm_tn(h, dp_qkv, name=f"{tag}_dwin_qkv")
    dwin_rest = _mm_tn(h, dp_rest, name=f"{tag}_dwin_rest")
    gin, dgamma = _mm_norm_bwd([(dp_qkv, wts["win_qkv"], True), (dp_rest, wts["win_rest"], True)], x, wts["mix_norm"],
                               gout, name=f"{tag}_dh_norm_bwd")
    dwin = _split_win(dwin_qkv, dwin_rest, f"{tag}_dwin_split")
    dpool_w = jnp.stack([dpw[64 * g:64 * g + 64, 64 * g:64 * g + 64] for g in range(4)])
    grads = dict(mix_norm=dgamma[0], w_in=dwin, pool_w=dpool_w, pool_scale=dpsc[0], forget_bias=dfb[0, 0:N_HEADS],
                 conv_w=dcw[0:CONV_K], conv_b=dcb[0], conv_ln_g=dlg[0], conv_ln_b=dlb[0], w_out=dwout)
    return gin, grads


def _rep_layer(rep, l):
    pw = rep["pool_w"][l].astype(CDT)
    wbd = jnp.zeros((256, 256), CDT)
    for g in range(4):
        wbd = lax.dynamic_update_slice(wbd, pw[g], (64 * g, 64 * g))
    return dict(
        ffn1_norm=rep["ffn1_norm"][l][None], ffn2_norm=rep["ffn2_norm"][l][None], mix_norm=rep["mix_norm"][l][None],
        fbias=jnp.pad(rep["forget_bias"][l], (0, LANES - N_HEADS))[None],
        pool_wbd=wbd, pool_scale=rep["pool_scale"][l][None], conv_b=rep["conv_b"][l][None],
        conv_ln_g=rep["conv_ln_g"][l][None], conv_ln_b=rep["conv_ln_b"][l][None])


def _local_step(x, target, rep, weights_for, grads_ready):
    depth = rep["ffn1_norm"].shape[0]
    kept = []
    for l in range(depth):
        r = _rep_layer(rep, l)
        w1, dep = weights_for(l, "ffn1", x)
        x, s1 = _ffn_fwd(x, r["ffn1_norm"], w1["w_gate"], w1["w_up"], w1["w_down"], f"l{l}_ffn1", dep)
        wm, dep = weights_for(l, "mix", x)
        wm = dict(r, win_qkv=wm["win_qkv"], win_rest=wm["win_rest"], w_out=wm["w_out"],
                  conv_w=jnp.pad(wm["conv_w"], ((0, CONV_HALO - CONV_K), (0, 0))))
        x, s2 = _mixer_fwd(x, wm, f"l{l}_mix", dep)
        w2, dep = weights_for(l, "ffn2", x)
        x, s3 = _ffn_fwd(x, r["ffn2_norm"], w2["w_gate"], w2["w_up"], w2["w_down"], f"l{l}_ffn2", dep)
        kept.append((r, w1, wm, w2, s1, s2, s3))
    loss, g, dfinal = _loss_bwd(x, rep["final_norm"][None], target, "loss_head")
    dep = grads_ready(None, "final", dict(final_norm=dfinal[0]))
    for l in reversed(range(depth)):
        r, w1, wm, w2, s1, s2, s3 = kept[l]

        def ffn_grads(which, l=l):
            return lambda gr: grads_ready(l, which, {f"{which}_{k}": v for k, v in gr.items()})

        g, dn = _ffn_bwd(s3, r["ffn2_norm"], w2["w_gate"], w2["w_up"], w2["w_down"], g, f"l{l}_ffn2", dep, ffn_grads("ffn2"))
        grads_ready(l, "norm", dict(ffn2_norm=dn[0]))
        g, gm = _mixer_bwd(s2, wm, g, f"l{l}_mix")
        dep = grads_ready(l, "mix", gm)
        g, dn = _ffn_bwd(s1, r["ffn1_norm"], w1["w_gate"], w1["w_up"], w1["w_down"], g, f"l{l}_ffn1", dep, ffn_grads("ffn1"))
        dep = grads_ready(l, "norm", dict(ffn1_norm=dn[0]))
    return loss, g


def _mesh_pos():
    return lax.axis_index("x"), lax.axis_index("y"), lax.axis_index("c")


def _dev_block(ref, dev, by_rows):
    if by_rows:
        r = ref.shape[1] // N_DEV
        return ref.at[:, pl.ds(dev * r, r), :]
    return ref.at[dev]


def _all_gather(shards, by_rows, name):
    n_arr = len(shards)
    out_shape = [_sds((s.shape[0], N_DEV * s.shape[1], s.shape[2]) if br else (N_DEV,) + s.shape, s.dtype)
                 for s, br in zip(shards, by_rows)]

    def body(*refs):
        xs, outs = refs[:n_arr], refs[n_arr:2 * n_arr]
        send_sems, recv_sems, local_sems = refs[2 * n_arr:]
        x, y, c = _mesh_pos()
        me, sibling = (x, y, c), (x, y, 1 - c)
        chips = [(1 - x, y), (x, 1 - y), (1 - x, 1 - y)]

        def rows(a, px, py, pc):
            return _dev_block(outs[a], 4 * px + 2 * py + pc, by_rows[a])

        def copy(k, a, block, to, src=None):
            return pltpu.make_async_remote_copy(
                src_ref=rows(a, *block) if src is None else src, dst_ref=rows(a, *block),
                send_sem=send_sems.at[k, a], recv_sem=recv_sems.at[k, a],
                device_id=to, device_id_type=pl.DeviceIdType.MESH)

        arrs = range(n_arr)
        mine = [pltpu.make_async_copy(xs[a], rows(a, *me), local_sems.at[a]) for a in arrs]
        for cp in mine:
            cp.start()
        first = [copy(0, a, me, sibling, src=xs[a]) for a in arrs]
        first += [copy(1 + j, a, me, (*chip, c), src=xs[a]) for j, chip in enumerate(chips) for a in arrs]
        for cp in first:
            cp.start()
        passed = []
        for j, chip in enumerate(chips):
            for a in arrs:
                copy(1 + j, a, (*chip, c), me).wait_recv()
                passed.append(copy(4 + j, a, (*chip, c), sibling))
                passed[-1].start()
        for a in arrs:
            copy(0, a, sibling, me).wait_recv()
        for j, chip in enumerate(chips):
            for a in arrs:
                copy(4 + j, a, (*chip, 1 - c), me).wait_recv()
        for cp in first + passed:
            cp.wait_send()
        for cp in mine:
            cp.wait()

    hbm = pl.BlockSpec(memory_space=pl.ANY)
    return pl.pallas_call(
        body, out_shape=out_shape, in_specs=[hbm] * n_arr, out_specs=[hbm] * n_arr,
        scratch_shapes=[pltpu.SemaphoreType.DMA((7, n_arr)), pltpu.SemaphoreType.DMA((7, n_arr)),
                        pltpu.SemaphoreType.DMA((n_arr,))],
        name=name)(*shards)


def _exchange(parts, by_rows, name):
    n_arr = len(parts)
    out_shape = [_sds((N_DEV, p.shape[0], p.shape[1] // N_DEV, p.shape[2]) if br else p.shape, p.dtype)
                 for p, br in zip(parts, by_rows)]

    def body(*refs):
        ps, outs = refs[:n_arr], refs[n_arr:2 * n_arr]
        send_sems, recv_sems, local_sems = refs[2 * n_arr:]
        x, y, c = _mesh_pos()
        my = 4 * x + 2 * y + c
        arrs = range(n_arr)
        mine = [pltpu.make_async_copy(_dev_block(ps[a], my, by_rows[a]), outs[a].at[my], local_sems.at[a]) for a in arrs]
        for cp in mine:
            cp.start()
        copies = []
        for k in range(1, N_DEV):
            px, py, pc = x ^ (k >> 2), y ^ ((k >> 1) & 1), c ^ (k & 1)
            for a in arrs:
                copies.append(pltpu.make_async_remote_copy(
                    src_ref=_dev_block(ps[a], 4 * px + 2 * py + pc, by_rows[a]), dst_ref=outs[a].at[my],
                    send_sem=send_sems.at[k - 1, a], recv_sem=recv_sems.at[k - 1, a],
                    device_id=(px, py, pc), device_id_type=pl.DeviceIdType.MESH))
        for cp in copies:
            cp.start()
        for cp in copies:
            cp.wait()
        for cp in mine:
            cp.wait()

    hbm = pl.BlockSpec(memory_space=pl.ANY)
    return pl.pallas_call(
        body, out_shape=out_shape, in_specs=[hbm] * n_arr, out_specs=[hbm] * n_arr,
        scratch_shapes=[pltpu.SemaphoreType.DMA((7, n_arr)), pltpu.SemaphoreType.DMA((7, n_arr)),
                        pltpu.SemaphoreType.DMA((n_arr,))],
        name=name)(*parts)


def _peer_copies(srcs, lands, send_sems, recv_sems, gather, by_rows):
    n_arr = len(srcs)
    x, y, c = _mesh_pos()
    my = 4 * x + 2 * y + c
    out = []
    for k in range(1, N_DEV):
        px, py, pc = x ^ (k >> 2), y ^ ((k >> 1) & 1), c ^ (k & 1)
        peer = 4 * px + 2 * py + pc
        for a in range(n_arr):
            src = srcs[a] if gather else _dev_block(srcs[a], peer, by_rows[a])
            dst = _dev_block(lands[a], my, by_rows[a]) if gather else lands[a].at[my]
            out.append(pltpu.make_async_remote_copy(
                src_ref=src, dst_ref=dst, send_sem=send_sems.at[(k - 1) * n_arr + a],
                recv_sem=recv_sems.at[(k - 1) * n_arr + a], device_id=(px, py, pc), device_id_type=pl.DeviceIdType.MESH))
    return out


def _land_shape(s, gather, by_rows):
    if gather:
        return (s.shape[0], N_DEV * s.shape[1], s.shape[2]) if by_rows else (N_DEV,) + s.shape
    return (N_DEV, s.shape[0], s.shape[1] // N_DEV, s.shape[2]) if by_rows else s.shape


_HBM = pl.BlockSpec(memory_space=pltpu.HBM)
_SEM = pl.BlockSpec(memory_space=pltpu.SEMAPHORE)


def _xfer_start(srcs, gather, by_rows, name, dep=None):
    n = len(srcs)
    lands = [lax.empty(_land_shape(s, gather, br), s.dtype) for s, br in zip(srcs, by_rows)]
    ins = [pltpu.with_memory_space_constraint(a, pltpu.HBM) for a in list(srcs) + lands]
    dspec, darg = _dep(dep)

    def body(*refs):
        s = 2 * n + len(darg)
        for cp in _peer_copies(refs[:n], refs[n:2 * n], refs[s], refs[s + 1], gather, by_rows):
            cp.start()
        for cp in _own_copies(refs[:n], refs[n:2 * n], refs[s + 2], gather, by_rows):
            cp.start()
        refs[-1][...] = jnp.zeros_like(refs[-1])

    sems = pltpu.SemaphoreType.DMA(((N_DEV - 1) * n,))
    outs = pl.pallas_call(
        body, name=name,
        out_shape=(sems, sems, pltpu.SemaphoreType.DMA((n,)), *[pltpu.HBM(a.shape, a.dtype) for a in ins],
                   _sds((8, LANES), F32)),
        in_specs=[_HBM] * (2 * n) + dspec,
        out_specs=(_SEM, _SEM, _SEM, *[_HBM] * (2 * n), pl.BlockSpec(memory_space=pltpu.VMEM)),
        input_output_aliases={i: 3 + i for i in range(2 * n)},
        compiler_params=pltpu.CompilerParams(has_side_effects=pltpu.SideEffectType.DATAFLOW_SIDE_EFFECTING))(*ins, *darg)
    return outs[0], outs[1], list(outs[3:-1]), outs[-1], outs[2]


def _own_copies(srcs, lands, sems, gather, by_rows):
    x, y, c = _mesh_pos()
    my = 4 * x + 2 * y + c
    out = []
    for a in range(len(srcs)):
        if gather:
            out.append(pltpu.make_async_copy(srcs[a], _dev_block(lands[a], my, by_rows[a]), sems.at[a]))
        else:
            out.append(pltpu.make_async_copy(_dev_block(srcs[a], my, by_rows[a]), lands[a].at[my], sems.at[a]))
    return out


def _xfer_wait(started, after, gather, by_rows, name):
    send_sems, recv_sems, bufs, _, local_sems = started
    n = len(bufs) // 2

    def body(*refs):
        for cp in _peer_copies(refs[:n], refs[n:2 * n], refs[2 * n], refs[2 * n + 1], gather, by_rows):
            cp.wait_send()
            cp.wait_recv()
        for cp in _own_copies(refs[:n], refs[n:2 * n], refs[2 * n + 2], gather, by_rows):
            cp.wait()

    outs = pl.pallas_call(
        body, name=name, out_shape=tuple(pltpu.HBM(a.shape, a.dtype) for a in bufs),
        in_specs=[_HBM] * (2 * n) + [_SEM, _SEM, _SEM, pl.BlockSpec(memory_space=pl.ANY)],
        out_specs=tuple([_HBM] * (2 * n)), input_output_aliases={i: i for i in range(2 * n)},
        compiler_params=pltpu.CompilerParams(has_side_effects=pltpu.SideEffectType.DATAFLOW_SIDE_EFFECTING))(
            *bufs, send_sems, recv_sems, local_sems, after)
    return list(outs[n:])


def _adam_update(g, w, m, v):
    c1 = 1.0 - ADAM_B1 ** ADAM_STEP
    c2 = 1.0 - ADAM_B2 ** ADAM_STEP
    nm = ADAM_B1 * m + (1.0 - ADAM_B1) * g
    nv = ADAM_B2 * v + (1.0 - ADAM_B2) * (g * g)
    return -ADAM_LR * ((nm / c1) / (jnp.sqrt(nv / c2) + ADAM_EPS) + ADAM_WD * w), nm, nv


def _adamw_body(p_ref, w_ref, m_ref, v_ref, g_ref, d_ref, nm_ref, nv_ref):
    g = p_ref[0]
    for i in range(1, N_DEV):
        g = g + p_ref[i]
    g_ref[...] = g
    d_ref[...], nm_ref[...], nv_ref[...] = _adam_update(g, w_ref[...], m_ref[...], v_ref[...])


def _adamw(parts, w, m, v, name, tr=1536):
    R = w.shape[0]
    tr = max(t for t in range(8, tr + 1, 8) if R % t == 0)

    def body(*refs):
        _adamw_body(*refs)

    row = pl.BlockSpec((tr, LANES), lambda i: (i, 0))
    return pl.pallas_call(
        body, grid=(R // tr,),
        in_specs=[pl.BlockSpec((N_DEV, tr, LANES), lambda i: (0, i, 0)), row, row, row],
        out_specs=[row, row, row, row], out_shape=[_sds((R, LANES), F32)] * 4,
        name=name, compiler_params=_cp(("parallel",)))(parts, w, m, v)


def _adamw_split(recvs, w, m, v, name, tr):
    depth, r, c = w.shape
    assert depth == len(recvs)
    tr = _tile(r, tr)

    def body(*refs):
        layer = pl.program_id(0)
        for ll in range(depth):
            @pl.when(layer == ll)
            def _(ll=ll):
                _adamw_body(refs[ll], *refs[depth:])

    wspec = pl.BlockSpec((None, tr, c), lambda l, i: (l, i, 0))
    rspecs = [pl.BlockSpec((N_DEV, None, tr, c), lambda l, i, ll=ll, t=t: (0, t, jnp.where(l == ll, i, 0), 0))
              for ll, (_, t) in enumerate(recvs)]
    return pl.pallas_call(
        body, grid=(depth, r // tr), in_specs=rspecs + [wspec, wspec, wspec],
        out_specs=[wspec] * 4, out_shape=[_sds(w.shape, F32)] * 4,
        name=name, compiler_params=_cp(("arbitrary", "arbitrary")))(*[a for a, _ in recvs], w, m, v)


def _merge_win(g, name, tr=256):
    _, nt, K, n = g.shape
    tr = _tile(K, tr)

    def body(g_ref, q_ref, r_ref):
        full = jnp.concatenate([g_ref[j] for j in range(N_DEV)], axis=1)
        q_ref[...] = full[:, 256:1792]
        zpad = jnp.zeros((tr, REST_W - 776), full.dtype)
        r_ref[...] = jnp.concatenate([full[:, 0:256], full[:, 1800:2312], full[:, 1792:1800], zpad], axis=1)

    return pl.pallas_call(
        body, grid=(nt, K // tr),
        in_specs=[pl.BlockSpec((N_DEV, None, tr, n), lambda t, i: (0, t, i, 0))],
        out_specs=[pl.BlockSpec((None, tr, 1536), lambda t, i: (t, i, 0)), pl.BlockSpec((None, tr, REST_W), lambda t, i: (t, i, 0))],
        out_shape=[_sds((nt, K, 1536), g.dtype), _sds((nt, K, REST_W), g.dtype)],
        name=name, compiler_params=_cp(("parallel", "parallel")))(g)


def _split_win(dq, dr, name, tr=256):
    K = dq.shape[0]
    tr = _tile(K, tr)
    n = (dq.shape[1] + 776) // N_DEV

    def body(q_ref, r_ref, o_ref):
        r = r_ref[...]
        full = jnp.concatenate([r[:, 0:256], q_ref[...], r[:, 768:776], r[:, 256:768]], axis=1)
        for j in range(N_DEV):
            o_ref[j] = full[:, n * j:n * (j + 1)]

    return pl.pallas_call(
        body, grid=(K // tr,),
        in_specs=[pl.BlockSpec((tr, dq.shape[1]), lambda i: (i, 0)), pl.BlockSpec((tr, REST_W), lambda i: (i, 0))],
        out_specs=pl.BlockSpec((N_DEV, tr, n), lambda i: (0, i, 0)),
        out_shape=_sds((N_DEV, K, n), F32), name=name, compiler_params=_cp(("parallel",)))(dq, dr)


WEIGHTS = ["ffn1_norm", "ffn1_w_gate", "ffn1_w_up", "ffn1_w_down", "mix_norm", "w_in", "pool_w", "pool_scale",
           "forget_bias", "conv_w", "conv_b", "conv_ln_g", "conv_ln_b", "w_out", "ffn2_norm", "ffn2_w_gate",
           "ffn2_w_up", "ffn2_w_down", "final_norm"]
FFN_PARTS = ("w_gate", "w_up", "w_down")
FFN_T = ["ffn1_w_gate", "ffn1_w_up", "ffn2_w_gate", "ffn2_w_up"]
BIG = FFN_T + ["ffn1_w_down", "ffn2_w_down", "w_in", "w_out"]
SMALL = [n for n in WEIGHTS if n not in BIG]


def _padded(n):
    return -(-n // PACK_ALIGN) * PACK_ALIGN


def _flat_pad(a):
    f = a.reshape(-1)
    return jnp.pad(f, (0, _padded(f.shape[0]) - f.shape[0]))


def _split8(a, axis):
    shp = a.shape
    a = a.reshape(shp[:axis] + (N_DEV, shp[axis] // N_DEV) + shp[axis + 1:])
    return jnp.moveaxis(a, axis, 0)


def _merge8(a, axis):
    a = jnp.moveaxis(a, 0, axis)
    shp = a.shape
    return a.reshape(shp[:axis] + (shp[axis] * shp[axis + 1],) + shp[axis + 2:])


def _pack_small(arrs):
    return jnp.concatenate([_flat_pad(arrs[n]) for n in SMALL]).reshape(-1, LANES)


def _pack_small_parts(grads):
    cols = []
    for n in SMALL:
        g = grads[n]
        if n == "conv_w":
            s = _split8(g, 2).reshape(N_DEV, -1)
        else:
            s = jnp.broadcast_to(g.reshape(1, -1), (N_DEV, g.size))
        cols.append(jnp.pad(s, ((0, 0), (0, _padded(s.shape[1]) - s.shape[1]))))
    return jnp.concatenate(cols, axis=1).reshape(N_DEV, -1, LANES)


def _unpack_small(buf, like):
    flat = buf.reshape(-1)
    out, off = {}, 0
    for n in SMALL:
        size = like[n].size
        out[n] = flat[off:off + size].reshape(like[n].shape)
        off += _padded(size)
    return out


class _Comm:
    def __init__(self, w):
        self.w = w
        self.bf = {n: (jnp.swapaxes(w[n], 1, 2) if n in FFN_T else w[n]).astype(CDT) for n in BIG}
        self.ready = {}
        self.grads = {}

    def _ffn_shards(self, l, which):
        return jnp.stack([self.bf[f"{which}_{k}"][l] for k in FFN_PARTS])

    def _put_ffn(self, l, which, rows, t):
        self.ready[(l, which)] = dict(w_gate=rows[t], w_up=rows[t + 1], w_down=rows[t + 2])

    def weights_for(self, l, stage, x):
        bf = self.bf
        dep = None
        if (l, stage) == (0, "ffn1"):
            gd, = _all_gather([self._ffn_shards(0, "ffn1")], [True], "gather_l0_ffn1")
            self._put_ffn(0, "ffn1", gd, 0)
            self.started = _xfer_start([bf["w_in"][0:1], bf["w_out"][0:1], self.w["conv_w"]], True,
                                       [False, True, False], "gather_mix0_start", dep=gd)
            dep = self.started[3]
        elif (l, stage) == (0, "mix"):
            gi, go, gc = _xfer_wait(self.started, x, True, [False, True, False], "gather_mix0_wait")
            q, r = _merge_win(gi, "merge_l0_w_in")
            self.conv_w = _merge8(gc, 2)
            self.ready[(0, "mix")] = dict(win_qkv=q[0], win_rest=r[0], w_out=go[0], conv_w=self.conv_w[0])
            rows = jnp.concatenate([self._ffn_shards(0, "ffn2"), self._ffn_shards(1, "ffn1"), self._ffn_shards(1, "ffn2")])
            self.started = _xfer_start([rows, bf["w_in"][1:2], bf["w_out"][1:2]], True, [True, False, True],
                                       "gather_rest_start")
            dep = self.started[3]
        elif (l, stage) == (0, "ffn2"):
            gd, gi, go = _xfer_wait(self.started, x, True, [True, False, True], "gather_rest_wait")
            self._put_ffn(0, "ffn2", gd, 0)
            self._put_ffn(1, "ffn1", gd, 3)
            self._put_ffn(1, "ffn2", gd, 6)
            q, r = _merge_win(gi, "merge_l1_w_in")
            self.ready[(1, "mix")] = dict(win_qkv=q[0], win_rest=r[0], w_out=go[0], conv_w=self.conv_w[1])
        return self.ready[(l, stage)], dep

    def grads_ready(self, l, stage, grads):
        for n, v in grads.items():
            self.grads[(l, n)] = v
        gr = self.grads

        def ffn_rows(layer, which, parts=FFN_PARTS):
            return [gr[(layer, f"{which}_{k}")] for k in parts]

        if l == 1 and "ffn1_w_gate" in grads:
            self.sent1 = _xfer_start(
                [jnp.stack(ffn_rows(1, "ffn1") + ffn_rows(1, "ffn2")), gr[(1, "w_in")][:, None], gr[(1, "w_out")][None]],
                False, [True, False, True], "grads_l1_start")
            return self.sent1[3]
        if l == 0 and "ffn2_w_gate" in grads:
            self.sent_ffn2 = _xfer_start([jnp.stack(ffn_rows(0, "ffn2"))], False, [True], "grads_l0_ffn2_start")
            return self.sent_ffn2[3]
        if (l, stage) == (0, "mix"):
            self.sent_mix = _xfer_start([gr[(0, "w_in")][:, None], gr[(0, "w_out")][None]], False, [False, True],
                                        "grads_l0_mix_start")
            return self.sent_mix[3]
        if l == 0 and "ffn1_w_down" in grads:
            self.sent_down = _xfer_start([gr[(0, "ffn1_w_down")][None]], False, [True], "grads_l0_ffn1_down_start")
            return self.sent_down[3]
        if l == 0 and "ffn1_w_gate" in grads:
            self.sent_gu = _xfer_start([jnp.stack(ffn_rows(0, "ffn1", FFN_PARTS[:2]))], False, [True],
                                       "grads_l0_ffn1_gate_up_start")
            return self.sent_gu[3]
        return None

    def finish(self, m, v, after):
        w, gr = self.w, self.grads
        depth = range(w["w_in"].shape[0])
        small = {n: (gr[(None, n)] if n == "final_norm" else jnp.stack([gr[(l, n)] for l in depth])) for n in SMALL}
        r1, i1, o1 = _xfer_wait(self.sent1, after, False, [True, False, True], "grads_l1_wait")
        r2, = _xfer_wait(self.sent_ffn2, after, False, [True], "grads_l0_ffn2_wait")
        i0, o0 = _xfer_wait(self.sent_mix, after, False, [False, True], "grads_l0_mix_wait")

        def adam(n, recvs, tr):
            if n in FFN_T:
                out = _adamw_split(recvs, *[jnp.swapaxes(t[n], 1, 2) for t in (w, m, v)], f"adamw_{n}", tr)
                return [jnp.swapaxes(o, 1, 2) for o in out]
            return _adamw_split(recvs, w[n], m[n], v[n], f"adamw_{n}", tr)

        res = {}
        for t, k in enumerate(FFN_PARTS):
            res[f"ffn2_{k}"] = adam(f"ffn2_{k}", [(r2, t), (r1, 3 + t)], 176)
        res["w_in"] = adam("w_in", [(i0, 0), (i1, 0)], 256)
        res["w_out"] = adam("w_out", [(o0, 0), (o1, 0)], 128)
        rs, = _exchange([_pack_small_parts(small)], [False], "exchange_small")
        r0, = _xfer_wait(self.sent_down, res["w_out"][0], False, [True], "grads_l0_ffn1_down_wait")
        res["ffn1_w_down"] = adam("ffn1_w_down", [(r0, 0), (r1, 2)], 176)
        g0, = _xfer_wait(self.sent_gu, res["ffn1_w_down"][0], False, [True], "grads_l0_ffn1_gate_up_wait")
        res["ffn1_w_gate"] = adam("ffn1_w_gate", [(g0, 0), (r1, 0)], 176)
        res["ffn1_w_up"] = adam("ffn1_w_up", [(g0, 1), (r1, 1)], 176)
        packed = _adamw(rs, _pack_small(w), _pack_small(m), _pack_small(v), "adamw_small")
        unpacked = [_unpack_small(b, w) for b in packed]
        for n in SMALL:
            res[n] = [u[n] for u in unpacked]
        return res


def kernel(x, ffn1_norm, ffn1_w_gate, ffn1_w_up, ffn1_w_down, mix_norm, w_in, pool_w, pool_scale, forget_bias, conv_w, conv_b, conv_ln_g, conv_ln_b, w_out, ffn2_norm, ffn2_w_gate, ffn2_w_up, ffn2_w_down, final_norm, loss_target, m_ffn1_norm, m_ffn1_w_gate, m_ffn1_w_up, m_ffn1_w_down, m_mix_norm, m_w_in, m_pool_w, m_pool_scale, m_forget_bias, m_conv_w, m_conv_b, m_conv_ln_g, m_conv_ln_b, m_w_out, m_ffn2_norm, m_ffn2_w_gate, m_ffn2_w_up, m_ffn2_w_down, m_final_norm, v_ffn1_norm, v_ffn1_w_gate, v_ffn1_w_up, v_ffn1_w_down, v_mix_norm, v_w_in, v_pool_w, v_pool_scale, v_forget_bias, v_conv_w, v_conv_b, v_conv_ln_g, v_conv_ln_b, v_w_out, v_ffn2_norm, v_ffn2_w_gate, v_ffn2_w_up, v_ffn2_w_down, v_final_norm):
    w = dict(zip(WEIGHTS, (ffn1_norm, ffn1_w_gate, ffn1_w_up, ffn1_w_down, mix_norm, w_in, pool_w, pool_scale, forget_bias,
                           conv_w, conv_b, conv_ln_g, conv_ln_b, w_out, ffn2_norm, ffn2_w_gate, ffn2_w_up, ffn2_w_down,
                           final_norm)))
    m = dict(zip(WEIGHTS, (m_ffn1_norm, m_ffn1_w_gate, m_ffn1_w_up, m_ffn1_w_down, m_mix_norm, m_w_in, m_pool_w, m_pool_scale,
                           m_forget_bias, m_conv_w, m_conv_b, m_conv_ln_g, m_conv_ln_b, m_w_out, m_ffn2_norm, m_ffn2_w_gate,
                           m_ffn2_w_up, m_ffn2_w_down, m_final_norm)))
    v = dict(zip(WEIGHTS, (v_ffn1_norm, v_ffn1_w_gate, v_ffn1_w_up, v_ffn1_w_down, v_mix_norm, v_w_in, v_pool_w, v_pool_scale,
                           v_forget_bias, v_conv_w, v_conv_b, v_conv_ln_g, v_conv_ln_b, v_w_out, v_ffn2_norm, v_ffn2_w_gate,
                           v_ffn2_w_up, v_ffn2_w_down, v_final_norm)))
    comm = _Comm(w)
    loss_row, gx = _local_step(x[0], loss_target[0], w, comm.weights_for, comm.grads_ready)
    loss = lax.psum(loss_row[0, 0], ("x", "y", "c"))
    res = comm.finish(m, v, gx)
    return (loss, gx[None], *[res[n][i] for i in range(4) for n in WEIGHTS])
```

```python
import math

import numpy as np
import jax
import jax.numpy as jnp
from jax import lax
from jax.experimental import pallas as pl
from jax.experimental.pallas import tpu as pltpu

F32 = jnp.float32
CDT = jnp.bfloat16
NORM_EPS = 1e-6
N_DEV = 8
LANES = 128
PACK_ALIGN = 8 * LANES
VMEM_LIMIT = 48 * 1024 * 1024

POOL_WINDOWS = (2, 4, 8, 16)
POOL_HALO = 16
CONV_K = 31
CONV_HALO = 32
HEAD_DIM = 64
N_HEADS = 8
N_PAIRS = N_HEADS // 2
ATT_SCALE = 1.0 / math.sqrt(HEAD_DIM)
NEG = -1e30

ADAM_LR, ADAM_B1, ADAM_B2, ADAM_EPS, ADAM_WD, ADAM_STEP = 0.001, 0.9, 0.999, 1e-08, 0.01, 10

REST_W = 896
REST_Z_BLK = 6


def _cp(sem):
    return pltpu.CompilerParams(dimension_semantics=sem, vmem_limit_bytes=VMEM_LIMIT)


def _tile(n, pref):
    t = min(n, pref)
    assert n % t == 0, (n, pref)
    return t


def _sigmoid(x):
    return 1.0 / (1.0 + jnp.exp(-x))


def _sds(shape, dtype):
    return jax.ShapeDtypeStruct(shape, dtype)


_ANY = pl.BlockSpec(memory_space=pl.ANY)


def _dep(dep):
    return ([], []) if dep is None else ([_ANY], [dep])


def _wshape(w):
    return w[0].shape[1:] if isinstance(w, tuple) else w.shape


def _wspec(w, block, index_map):
    if not isinstance(w, tuple):
        return w, pl.BlockSpec(block, index_map)
    arr, t = w
    return arr, pl.BlockSpec((None,) + block, lambda *g: (t,) + index_map(*g))


def _rms_fwd(x, g, name, dep=None):
    T, D = x.shape
    tm = _tile(T, 1024)

    def body(x_ref, g_ref, *rest):
        o_ref = rest[-1]
        xv = x_ref[...]
        r = lax.rsqrt(jnp.mean(xv * xv, axis=-1, keepdims=True) + NORM_EPS)
        o_ref[...] = (xv * r * g_ref[...]).astype(o_ref.dtype)

    dspec, darg = _dep(dep)
    return pl.pallas_call(
        body, grid=(T // tm,),
        in_specs=[pl.BlockSpec((tm, D), lambda i: (i, 0)), pl.BlockSpec((1, D), lambda i: (0, 0))] + dspec,
        out_specs=pl.BlockSpec((tm, D), lambda i: (i, 0)),
        out_shape=_sds((T, D), CDT), name=name, compiler_params=_cp(("parallel",)))(x, g, *darg)


def _rms_bwd(x, g, dh, gres, name):
    T, D = x.shape
    tm = _tile(T, 512)

    def body(x_ref, g_ref, dh_ref, gres_ref, gin_ref, dg_ref):
        i = pl.program_id(0)
        xv = x_ref[...]
        d = dh_ref[...]
        r = lax.rsqrt(jnp.mean(xv * xv, axis=-1, keepdims=True) + NORM_EPS)
        xh = xv * r
        dxh = d * g_ref[...]
        c = jnp.mean(dxh * xh, axis=-1, keepdims=True)
        gin_ref[...] = gres_ref[...] + r * (dxh - xh * c)
        part = jnp.sum(d * xh, axis=0, keepdims=True)

        @pl.when(i == 0)
        def _():
            dg_ref[...] = part

        @pl.when(i > 0)
        def _():
            dg_ref[...] += part

    row = pl.BlockSpec((tm, D), lambda i: (i, 0))
    vec = pl.BlockSpec((1, D), lambda i: (0, 0))
    return pl.pallas_call(
        body, grid=(T // tm,), in_specs=[row, vec, row, row], out_specs=[row, vec],
        out_shape=[_sds((T, D), F32), _sds((1, D), F32)], name=name, compiler_params=_cp(("arbitrary",)))(x, g, dh, gres)


def _loss_bwd(x, g, target, name):
    T, D = x.shape
    tm = _tile(T, 512)

    def body(x_ref, g_ref, t_ref, loss_ref, dx_ref, dg_ref):
        i = pl.program_id(0)
        xv = x_ref[...]
        gv = g_ref[...]
        r = lax.rsqrt(jnp.mean(xv * xv, axis=-1, keepdims=True) + NORM_EPS)
        xh = xv * r
        err = xh * gv - t_ref[...]
        lpart = 0.5 * jnp.sum(jnp.mean(err * err, axis=-1, keepdims=True), axis=0, keepdims=True)
        dy = err * (1.0 / D)
        dxh = dy * gv
        c = jnp.mean(dxh * xh, axis=-1, keepdims=True)
        dx_ref[...] = r * (dxh - xh * c)
        part = jnp.sum(dy * xh, axis=0, keepdims=True)
        lrow = jnp.broadcast_to(lpart, (1, LANES))

        @pl.when(i == 0)
        def _():
            dg_ref[...] = part
            loss_ref[...] = lrow

        @pl.when(i > 0)
        def _():
            dg_ref[...] += part
            loss_ref[...] += lrow

    row = pl.BlockSpec((tm, D), lambda i: (i, 0))
    vec = pl.BlockSpec((1, D), lambda i: (0, 0))
    return pl.pallas_call(
        body, grid=(T // tm,), in_specs=[row, vec, row],
        out_specs=[pl.BlockSpec((1, LANES), lambda i: (0, 0)), row, vec],
        out_shape=[_sds((1, LANES), F32), _sds((T, D), F32), _sds((1, D), F32)],
        name=name, compiler_params=_cp(("arbitrary",)))(x, g, target)


def _mm(pairs, *, name, res=None, alpha=1.0, out_dtype=F32, tm=512, tn=None, dep=None):
    T = pairs[0][0].shape[0]
    N = _wshape(pairs[0][1])[0 if pairs[0][2] else 1]
    tm = _tile(T, tm)
    tn = N if tn is None else _tile(N, tn)
    flags = [p[2] for p in pairs]
    n_in = 2 * len(pairs)

    def body(*refs):
        o_ref = refs[-1]
        acc = None
        for p, bt in enumerate(flags):
            a = refs[2 * p][...].astype(CDT)
            b = refs[2 * p + 1][...]
            dims = (((1,), (1,)), ((), ())) if bt else (((1,), (0,)), ((), ()))
            d = lax.dot_general(a, b, dims, preferred_element_type=F32)
            acc = d if acc is None else acc + d
        if alpha != 1.0:
            acc = acc * alpha
        if res is not None:
            acc = refs[n_in][...] + acc
        o_ref[...] = acc.astype(o_ref.dtype)

    in_specs, args = [], []
    for a, b, bt in pairs:
        K = a.shape[1]
        in_specs.append(pl.BlockSpec((tm, K), lambda i, j: (i, 0)))
        b, bspec = _wspec(b, (tn, K), lambda i, j: (j, 0)) if bt else _wspec(b, (K, tn), lambda i, j: (0, j))
        in_specs.append(bspec)
        args += [a, b]
    if res is not None:
        in_specs.append(pl.BlockSpec((tm, tn), lambda i, j: (i, j)))
        args.append(res)
    dspec, darg = _dep(dep)
    in_specs += dspec
    args += darg
    return pl.pallas_call(
        body, grid=(T // tm, N // tn), in_specs=in_specs,
        out_specs=pl.BlockSpec((tm, tn), lambda i, j: (i, j)),
        out_shape=_sds((T, N), out_dtype), name=name, compiler_params=_cp(("parallel", "arbitrary")))(*args)


def _mm_norm_bwd(pairs, x, g, gres, *, name, tm=256, dep=None):
    T, D = x.shape
    tm = _tile(T, tm)
    n_in = 2 * len(pairs)
    flags = [p[2] for p in pairs]

    def body(*refs):
        x_ref, g_ref, gres_ref = refs[n_in:n_in + 3]
        gin_ref, dg_ref = refs[-2:]
        i = pl.program_id(0)
        d = None
        for p, bt in enumerate(flags):
            dims = (((1,), (1,)), ((), ())) if bt else (((1,), (0,)), ((), ()))
            part = lax.dot_general(refs[2 * p][...].astype(CDT), refs[2 * p + 1][...], dims, preferred_element_type=F32)
            d = part if d is None else d + part
        xv = x_ref[...]
        r = lax.rsqrt(jnp.mean(xv * xv, axis=-1, keepdims=True) + NORM_EPS)
        xh = xv * r
        dxh = d * g_ref[...]
        c = jnp.mean(dxh * xh, axis=-1, keepdims=True)
        gin_ref[...] = gres_ref[...] + r * (dxh - xh * c)
        part = jnp.sum(d * xh, axis=0, keepdims=True)

        @pl.when(i == 0)
        def _():
            dg_ref[...] = part

        @pl.when(i > 0)
        def _():
            dg_ref[...] += part

    in_specs, args = [], []
    for a, b, bt in pairs:
        K = a.shape[1]
        b, bspec = _wspec(b, tuple(_wshape(b)), lambda i: (0, 0))
        in_specs += [pl.BlockSpec((tm, K), lambda i: (i, 0)), bspec]
        args += [a, b]
    row = pl.BlockSpec((tm, D), lambda i: (i, 0))
    vec = pl.BlockSpec((1, D), lambda i: (0, 0))
    dspec, darg = _dep(dep)
    return pl.pallas_call(
        body, grid=(T // tm,), in_specs=in_specs + [row, vec, row] + dspec, out_specs=[row, vec],
        out_shape=[_sds((T, D), F32), _sds((1, D), F32)], name=name,
        compiler_params=_cp(("arbitrary",)))(*args, x, g, gres, *darg)


def _mm_tn(a, b, *, name, alpha=1.0, tk=2048, dep=None):
    T, M = a.shape
    N = b.shape[1]
    tm = M if M <= 1024 else M // 2
    tn = N if N <= 1536 else N // 2
    assert M % tm == 0 and N % tn == 0 and tm % LANES == 0 and tn % LANES == 0
    tk = _tile(T, tk)
    nk = T // tk

    def body(a_ref, b_ref, *rest):
        o_ref = rest[-1]
        k = pl.program_id(2)
        d = lax.dot_general(a_ref[...].astype(CDT), b_ref[...].astype(CDT), (((0,), (0,)), ((), ())),
                            preferred_element_type=F32)

        @pl.when(k == 0)
        def _():
            o_ref[...] = d

        @pl.when(k > 0)
        def _():
            o_ref[...] += d

        if alpha != 1.0:
            @pl.when(k == nk - 1)
            def _():
                o_ref[...] *= alpha

    dspec, darg = _dep(dep)
    return pl.pallas_call(
        body, grid=(M // tm, N // tn, nk),
        in_specs=[pl.BlockSpec((tk, tm), lambda i, j, k: (k, i)), pl.BlockSpec((tk, tn), lambda i, j, k: (k, j))] + dspec,
        out_specs=pl.BlockSpec((tm, tn), lambda i, j, k: (i, j)),
        out_shape=_sds((M, N), F32), name=name, compiler_params=_cp(("parallel", "parallel", "arbitrary")))(a, b, *darg)


def _ffn_up(h, wgt, wut, name):
    T, D = h.shape
    Fh = _wshape(wgt)[0]
    tm = _tile(T, 2048)
    tn = _tile(Fh, 256)
    nt = (((1,), (1,)), ((), ()))

    def body(h_ref, wg_ref, wu_ref, a_ref, b_ref, s_ref):
        hv = h_ref[...]
        a = lax.dot_general(hv, wg_ref[...], nt, preferred_element_type=F32)
        b = lax.dot_general(hv, wu_ref[...], nt, preferred_element_type=F32)
        a_ref[...] = a.astype(a_ref.dtype)
        b_ref[...] = b.astype(b_ref.dtype)
        s_ref[...] = (a * _sigmoid(a) * b).astype(s_ref.dtype)

    wgt, gspec = _wspec(wgt, (tn, D), lambda i, j: (j, 0))
    wut, uspec = _wspec(wut, (tn, D), lambda i, j: (j, 0))
    ospec = pl.BlockSpec((tm, tn), lambda i, j: (i, j))
    return pl.pallas_call(
        body, grid=(T // tm, Fh // tn),
        in_specs=[pl.BlockSpec((tm, D), lambda i, j: (i, 0)), gspec, uspec],
        out_specs=[ospec, ospec, ospec],
        out_shape=[_sds((T, Fh), CDT), _sds((T, Fh), CDT), _sds((T, Fh), CDT)],
        name=name, compiler_params=_cp(("parallel", "arbitrary")))(h, wgt, wut)


def _ffn_bwd_ds(gout, wd, a, b, name, dep=None):
    T, D = gout.shape
    Fh = _wshape(wd)[0]
    tm = _tile(T, 2048)
    tn = _tile(Fh, 256)

    def body(g_ref, wd_ref, a_ref, b_ref, *rest):
        da_ref, db_ref = rest[-2:]
        dy = (0.5 * g_ref[...]).astype(CDT)
        ds = lax.dot_general(dy, wd_ref[...], (((1,), (1,)), ((), ())), preferred_element_type=F32)
        av = a_ref[...].astype(F32)
        sg = _sigmoid(av)
        da_ref[...] = (ds * b_ref[...].astype(F32) * (sg * (1.0 + av * (1.0 - sg)))).astype(da_ref.dtype)
        db_ref[...] = (ds * (av * sg)).astype(db_ref.dtype)

    ospec = pl.BlockSpec((tm, tn), lambda i, j: (i, j))
    dspec, darg = _dep(dep)
    wd, wspec = _wspec(wd, (tn, D), lambda i, j: (j, 0))
    return pl.pallas_call(
        body, grid=(T // tm, Fh // tn),
        in_specs=[pl.BlockSpec((tm, D), lambda i, j: (i, 0)), wspec, ospec, ospec] + dspec,
        out_specs=[ospec, ospec],
        out_shape=[_sds((T, Fh), CDT), _sds((T, Fh), CDT)],
        name=name, compiler_params=_cp(("parallel", "arbitrary")))(gout, wd, a, b, *darg)


def _ffn_fwd(x, gamma, wgt, wut, wd, tag, dep=None):
    h = _rms_fwd(x, gamma, f"{tag}_norm", dep)
    a, b, s = _ffn_up(h, wgt, wut, f"{tag}_up")
    y = _mm([(s, wd, False)], res=x, alpha=0.5, name=f"{tag}_down")
    return y, (x, h, a, b, s)


def _ffn_bwd(saved, gamma, wgt, wut, wd, gout, tag, dep, on_grads):
    x, h, a, b, s = saved
    dwd = _mm_tn(s, gout, alpha=0.5, name=f"{tag}_dwd", dep=dep)
    da, db = _ffn_bwd_ds(gout, wd, a, b, f"{tag}_bwd_ds", on_grads(dict(w_down=dwd)))
    dwgt = _mm_tn(da, h, name=f"{tag}_dwg")
    dwut = _mm_tn(db, h, name=f"{tag}_dwu")
    dep = on_grads(dict(w_gate=dwgt, w_up=dwut))
    return _mm_norm_bwd([(da, wgt, False), (db, wut, False)], x, gamma, gout, name=f"{tag}_dh_norm_bwd", dep=dep)


def _fgate_fwd(rest, bias, name, bt=512):
    T = rest.shape[0]
    bt = _tile(T, bt)

    def body(z_ref, b_ref, fc_ref, ft_ref, carry):
        i = pl.program_id(0)

        @pl.when(i == 0)
        def _():
            carry[...] = jnp.zeros_like(carry)

        zb = z_ref[...] + b_ref[...]
        e = jnp.exp(-jnp.abs(zb))
        u = 1.0 + e
        log1p_e = jnp.where(u == 1.0, e, jnp.log(u) * (e / (u - 1.0)))
        x = jnp.minimum(zb, 0.0) - log1p_e
        row = lax.broadcasted_iota(jnp.int32, x.shape, 0)
        sh = 1
        while sh < bt:
            x = x + jnp.where(row >= sh, pltpu.roll(x, sh, 0), 0.0)
            sh *= 2
        f = x + carry[...]
        carry[...] = f[bt - 1:bt, :]
        fc_ref[...] = f
        ft_ref[...] = jnp.transpose(f)[0:N_HEADS, :]

    return pl.pallas_call(
        body, grid=(T // bt,),
        in_specs=[pl.BlockSpec((bt, LANES), lambda i: (i, REST_Z_BLK)), pl.BlockSpec((1, LANES), lambda i: (0, 0))],
        out_specs=[pl.BlockSpec((bt, LANES), lambda i: (i, 0)), pl.BlockSpec((N_HEADS, bt), lambda i: (0, i))],
        out_shape=[_sds((T, LANES), F32), _sds((N_HEADS, T), F32)],
        scratch_shapes=[pltpu.VMEM((1, LANES), F32)],
        name=name, compiler_params=_cp(("arbitrary",)))(rest, bias)


def _fgate_bwd(dfk, rest, bias, name, bt=512):
    T = rest.shape[0]
    bt = _tile(T, bt)
    nb = T // bt

    def body(df_ref, z_ref, b_ref, dz_ref, db_ref, carry):
        i = pl.program_id(0)

        @pl.when(i == 0)
        def _():
            carry[...] = jnp.zeros_like(carry)

        dfv = df_ref[...]
        lane = lax.broadcasted_iota(jnp.int32, (bt, LANES), 1)
        x = jnp.zeros((bt, LANES), F32)
        for h in range(N_HEADS):
            x = jnp.where(lane == h, dfv[:, HEAD_DIM * h:HEAD_DIM * h + 1], x)
        row = lax.broadcasted_iota(jnp.int32, x.shape, 0)
        sh = 1
        while sh < bt:
            x = x + jnp.where(row + sh < bt, pltpu.roll(x, bt - sh, 0), 0.0)
            sh *= 2
        dlf = x + carry[...]
        carry[...] = dlf[0:1, :]
        zb = z_ref[...] + b_ref[...]
        dz = jnp.where(lane < N_HEADS, dlf * _sigmoid(-zb), 0.0)
        dz_ref[...] = dz.astype(dz_ref.dtype)
        part = jnp.sum(dz, axis=0, keepdims=True)

        @pl.when(i == 0)
        def _():
            db_ref[...] = part

        @pl.when(i > 0)
        def _():
            db_ref[...] += part

    return pl.pallas_call(
        body, grid=(nb,),
        in_specs=[pl.BlockSpec((bt, 4 * LANES), lambda i: (nb - 1 - i, 0)),
                  pl.BlockSpec((bt, LANES), lambda i: (nb - 1 - i, REST_Z_BLK)),
                  pl.BlockSpec((1, LANES), lambda i: (0, 0))],
        out_specs=[pl.BlockSpec((bt, LANES), lambda i: (nb - 1 - i, 0)), pl.BlockSpec((1, LANES), lambda i: (0, 0))],
        out_shape=[_sds((T, LANES), CDT), _sds((1, LANES), F32)],
        scratch_shapes=[pltpu.VMEM((1, LANES), F32)],
        name=name, compiler_params=_cp(("arbitrary",)))(dfk, rest, bias)


def _by_group(vals, lane):
    out = vals[-1]
    for g in range(len(vals) - 2, -1, -1):
        out = jnp.where(lane // 64 == g, vals[g], out)
    return out


def _pool_counts(t0, n, lane):
    t = t0 + lax.broadcasted_iota(jnp.int32, (n, 256), 0)
    return _by_group([jnp.minimum(t + 1, w) for w in POOL_WINDOWS], lane).astype(F32)


def _pooled(u, halo, i, bt):
    lane = lax.broadcasted_iota(jnp.int32, (bt, 256), 1)
    ext = jnp.concatenate([jnp.where(i > 0, halo, 0.0), u], axis=0)
    sums, s, sh = [], ext, 1
    for _ in POOL_WINDOWS:
        s = s + pltpu.roll(s, sh, 0)
        sums.append(s[POOL_HALO:, :])
        sh *= 2
    return _by_group(sums, lane) / _pool_counts(i * bt, bt, lane) - u


def _pool_fwd(rest, wbd, scale, name, bt=512):
    T = rest.shape[0]
    bt = _tile(T, bt)
    hb = bt // POOL_HALO

    def body(u_ref, halo_ref, w_ref, sc_ref, o_ref):
        i = pl.program_id(0)
        pooled = _pooled(u_ref[...], halo_ref[...], i, bt)
        mixed = jnp.dot(pooled.astype(CDT), w_ref[...], preferred_element_type=F32)
        o_ref[...] = (mixed * sc_ref[...]).astype(o_ref.dtype)

    return pl.pallas_call(
        body, grid=(T // bt,),
        in_specs=[pl.BlockSpec((bt, 256), lambda i: (i, 0)),
                  pl.BlockSpec((POOL_HALO, 256), lambda i: (jnp.maximum(i * hb - 1, 0), 0)),
                  pl.BlockSpec((256, 256), lambda i: (0, 0)), pl.BlockSpec((1, 256), lambda i: (0, 0))],
        out_specs=pl.BlockSpec((bt, 256), lambda i: (i, 0)),
        out_shape=_sds((T, 256), CDT), name=name, compiler_params=_cp(("parallel",)))(rest, rest, wbd, scale)


def _pool_bwd(dcat, rest, wbd, scale, name, bt=512):
    T = rest.shape[0]
    bt = _tile(T, bt)
    hb = bt // POOL_HALO
    nb = T // bt
    n = bt + POOL_HALO

    def body(dy_ref, dyn_ref, u_ref, halo_ref, w_ref, sc_ref, du_ref, dw_ref, dsc_ref):
        i = pl.program_id(0)
        lane = lax.broadcasted_iota(jnp.int32, (bt, 256), 1)
        w = w_ref[...]
        sc = sc_ref[...]
        pooled = _pooled(u_ref[...], halo_ref[...], i, bt)
        pooled_c = pooled.astype(CDT)
        mixed = jnp.dot(pooled_c, w, preferred_element_type=F32)
        dy = dy_ref[...]
        dm = (dy * sc).astype(CDT)
        dsc = jnp.sum(dy * mixed, axis=0, keepdims=True)
        dw = lax.dot_general(pooled_c, dm, (((0,), (0,)), ((), ())), preferred_element_type=F32)
        nt = (((1,), (1,)), ((), ()))
        dpl = lax.dot_general(dm, w, nt, preferred_element_type=F32)
        dmn = (jnp.where(i < nb - 1, dyn_ref[...], 0.0) * sc).astype(CDT)
        dpln = lax.dot_general(dmn, w, nt, preferred_element_type=F32)
        lane_h = lax.broadcasted_iota(jnp.int32, (POOL_HALO, 256), 1)
        ext = jnp.concatenate([dpl / _pool_counts(i * bt, bt, lane),
                               dpln / _pool_counts((i + 1) * bt, POOL_HALO, lane_h)], axis=0)
        sums, s, sh = [], ext, 1
        for _ in POOL_WINDOWS:
            s = s + pltpu.roll(s, n - sh, 0)
            sums.append(s[0:bt, :])
            sh *= 2
        du_ref[...] = (_by_group(sums, lane) - dpl).astype(du_ref.dtype)

        @pl.when(i == 0)
        def _():
            dw_ref[...] = dw
            dsc_ref[...] = dsc

        @pl.when(i > 0)
        def _():
            dw_ref[...] += dw
            dsc_ref[...] += dsc

    full = pl.BlockSpec((256, 256), lambda i: (0, 0))
    vec = pl.BlockSpec((1, 256), lambda i: (0, 0))
    return pl.pallas_call(
        body, grid=(nb,),
        in_specs=[pl.BlockSpec((bt, 256), lambda i: (i, 0)),
                  pl.BlockSpec((POOL_HALO, 256), lambda i: (jnp.minimum((i + 1) * hb, nb * hb - 1), 0)),
                  pl.BlockSpec((bt, 256), lambda i: (i, 0)),
                  pl.BlockSpec((POOL_HALO, 256), lambda i: (jnp.maximum(i * hb - 1, 0), 0)),
                  full, vec],
        out_specs=[pl.BlockSpec((bt, 256), lambda i: (i, 0)), full, vec],
        out_shape=[_sds((T, 256), CDT), _sds((256, 256), F32), _sds((1, 256), F32)],
        name=name, compiler_params=_cp(("arbitrary",)))(dcat, dcat, rest, rest, wbd, scale)


def _glu_ext(a_ref, g_ref, ah_ref, gh_ref, i):
    u = a_ref[...] * _sigmoid(g_ref[...])
    uh = jnp.where(i > 0, ah_ref[...] * _sigmoid(gh_ref[...]), 0.0)
    return jnp.concatenate([uh, u], axis=0)


def _conv_fwd(rest, cw, cb, lg, lb, name, bt=512):
    T = rest.shape[0]
    bt = _tile(T, bt)
    hb = bt // CONV_HALO

    def body(a_ref, g_ref, ah_ref, gh_ref, cw_ref, cb_ref, lg_ref, lb_ref, o_ref, y_ref):
        i = pl.program_id(0)
        ext = _glu_ext(a_ref, g_ref, ah_ref, gh_ref, i)
        w = cw_ref[...]
        acc = w[CONV_K - 1:CONV_K, :] * ext
        for k in range(CONV_K - 1):
            acc = acc + w[k:k + 1, :] * pltpu.roll(ext, CONV_K - 1 - k, 0)
        y = acc[CONV_HALO:, :] + cb_ref[...]
        y_ref[...] = y
        yc = y - jnp.mean(y, axis=-1, keepdims=True)
        yn = yc * lax.rsqrt(jnp.mean(yc * yc, axis=-1, keepdims=True) + NORM_EPS)
        z = yn * lg_ref[...] + lb_ref[...]
        o_ref[...] = (z * _sigmoid(z)).astype(o_ref.dtype)

    def cur(c):
        return pl.BlockSpec((bt, 256), lambda i: (i, c))

    def prev(c):
        return pl.BlockSpec((CONV_HALO, 256), lambda i: (jnp.maximum(i * hb - 1, 0), c))

    vec = pl.BlockSpec((1, 256), lambda i: (0, 0))
    return pl.pallas_call(
        body, grid=(T // bt,),
        in_specs=[cur(1), cur(2), prev(1), prev(2), pl.BlockSpec((CONV_HALO, 256), lambda i: (0, 0)), vec, vec, vec],
        out_specs=[pl.BlockSpec((bt, 256), lambda i: (i, 0)), pl.BlockSpec((bt, 256), lambda i: (i, 0))],
        out_shape=[_sds((T, 256), CDT), _sds((T, 256), F32)],
        name=name, compiler_params=_cp(("parallel",)))(rest, rest, rest, rest, cw, cb, lg, lb)


def _conv_bwd(dcat, yconv, rest, cw, lg, lb, name, bt=512):
    T = rest.shape[0]
    bt = _tile(T, bt)
    hb = bt // CONV_HALO
    nb = T // bt
    n = bt + CONV_HALO

    def body(dy_ref, dyn_ref, y_ref, yn_ref, a_ref, g_ref, ah_ref, gh_ref, cw_ref, lg_ref, lb_ref,
             da_ref, dg_ref, dcw_ref, dcb_ref, dlg_ref, dlb_ref):
        i = pl.program_id(0)
        lgv = lg_ref[...]
        lbv = lb_ref[...]

        def ln_swish_bwd(dout, y):
            yc = y - jnp.mean(y, axis=-1, keepdims=True)
            rs = lax.rsqrt(jnp.mean(yc * yc, axis=-1, keepdims=True) + NORM_EPS)
            yn = yc * rs
            z = yn * lgv + lbv
            sg = _sigmoid(z)
            dz = dout * (sg * (1.0 + z * (1.0 - sg)))
            dyn = dz * lgv
            dyc = rs * (dyn - jnp.mean(dyn, axis=-1, keepdims=True) - yn * jnp.mean(dyn * yn, axis=-1, keepdims=True))
            return dyc, dz, yn

        dyc, dz, yn = ln_swish_bwd(dy_ref[...], y_ref[...])
        dyc_next, _, _ = ln_swish_bwd(dyn_ref[...], yn_ref[...])
        dyc_next = jnp.where(i < nb - 1, dyc_next, 0.0)
        ext_u = _glu_ext(a_ref, g_ref, ah_ref, gh_ref, i)
        ext_d = jnp.concatenate([dyc, dyc_next], axis=0)
        w = cw_ref[...]
        du = w[CONV_K - 1:CONV_K, :] * ext_d
        rows = []
        for k in range(CONV_K):
            s = CONV_K - 1 - k
            if s > 0:
                du = du + w[k:k + 1, :] * pltpu.roll(ext_d, n - s, 0)
                us = pltpu.roll(ext_u, s, 0)[CONV_HALO:, :]
            else:
                us = ext_u[CONV_HALO:, :]
            rows.append(jnp.sum(dyc * us, axis=0, keepdims=True))
        rows.append(jnp.zeros((1, 256), F32))
        dcw = jnp.concatenate(rows, axis=0)
        du = du[0:bt, :]
        av = a_ref[...]
        sg = _sigmoid(g_ref[...])
        da_ref[...] = (du * sg).astype(da_ref.dtype)
        dg_ref[...] = (du * av * (sg * (1.0 - sg))).astype(dg_ref.dtype)
        dcb = jnp.sum(dyc, axis=0, keepdims=True)
        dlg = jnp.sum(dz * yn, axis=0, keepdims=True)
        dlb = jnp.sum(dz, axis=0, keepdims=True)

        @pl.when(i == 0)
        def _():
            dcw_ref[...] = dcw
            dcb_ref[...] = dcb
            dlg_ref[...] = dlg
            dlb_ref[...] = dlb

        @pl.when(i > 0)
        def _():
            dcw_ref[...] += dcw
            dcb_ref[...] += dcb
            dlg_ref[...] += dlg
            dlb_ref[...] += dlb

    def cur(c):
        return pl.BlockSpec((bt, 256), lambda i: (i, c))

    def prev(c):
        return pl.BlockSpec((CONV_HALO, 256), lambda i: (jnp.maximum(i * hb - 1, 0), c))

    def nxt(c):
        return pl.BlockSpec((CONV_HALO, 256), lambda i: (jnp.minimum((i + 1) * hb, nb * hb - 1), c))

    vec = pl.BlockSpec((1, 256), lambda i: (0, 0))
    wfull = pl.BlockSpec((CONV_HALO, 256), lambda i: (0, 0))
    return pl.pallas_call(
        body, grid=(nb,),
        in_specs=[cur(3), nxt(3), cur(0), nxt(0), cur(1), cur(2), prev(1), prev(2), wfull, vec, vec],
        out_specs=[cur(0), cur(0), wfull, vec, vec, vec],
        out_shape=[_sds((T, 256), CDT), _sds((T, 256), CDT), _sds((CONV_HALO, 256), F32),
                   _sds((1, 256), F32), _sds((1, 256), F32), _sds((1, 256), F32)],
        name=name, compiler_params=_cp(("arbitrary",)))(dcat, dcat, yconv, yconv, rest, rest, rest, rest, cw, lg, lb)


def _half_mask(shape, a):
    lane = lax.broadcasted_iota(jnp.int32, shape, 1)
    return (lane // HEAD_DIM) == a


def _attn_fwd(qkv, fcol, frow, name, blk=1024):
    T = qkv.shape[0]
    blk = _tile(T, blk)
    nq = T // blk
    nt = (((1,), (1,)), ((), ()))

    def body(q_ref, k_ref, v_ref, fc_ref, fr_ref, o_ref, lse_ref):
        p_id = pl.program_id(0)
        i = pl.program_id(1)
        q2 = q_ref[...]
        fc = fc_ref[...]
        lane = lax.broadcasted_iota(jnp.int32, (blk, LANES), 1)
        tri = lax.broadcasted_iota(jnp.int32, (blk, blk), 1) <= lax.broadcasted_iota(jnp.int32, (blk, blk), 0)
        masks = [_half_mask(q2.shape, a) for a in range(2)]
        qs = [jnp.where(hm, q2, jnp.zeros_like(q2)) * ATT_SCALE for hm in masks]
        fqs = [jnp.sum(jnp.where(lane == 2 * p_id + a, fc, 0.0), axis=1, keepdims=True) for a in range(2)]

        def tile(j, carry, masked):
            cols = pl.ds(pl.multiple_of(j * blk, blk), blk)
            kj = k_ref[cols, :]
            vj = v_ref[cols, :]
            out = []
            for a in range(2):
                m, acc = carry[2 * a:2 * a + 2]
                va = jnp.where(masks[a], vj, jnp.ones_like(vj))
                s = lax.dot_general(qs[a], kj, nt, preferred_element_type=F32) + (fqs[a] - fr_ref[a:a + 1, cols])
                if masked:
                    s = jnp.where(tri, s, NEG)
                m_new = jnp.maximum(m, jnp.max(s, axis=1, keepdims=True))
                alpha = jnp.exp(m - m_new)
                pr = jnp.exp(s - m_new)
                hi = lax.bitcast_convert_type(lax.bitcast_convert_type(pr, jnp.uint32) & jnp.uint32(0xFFFF0000), F32)
                pv = (jnp.dot(hi.astype(CDT), va, preferred_element_type=F32)
                      + jnp.dot((pr - hi).astype(CDT), va, preferred_element_type=F32))
                out += [m_new, alpha * acc + pv]
            return tuple(out)

        init = (jnp.full((blk, 1), NEG, F32), jnp.zeros((blk, LANES), F32)) * 2
        carry = lax.fori_loop(0, i, lambda j, c: tile(j, c, False), init)
        carry = tile(i, carry, True)
        ls = [carry[1][:, HEAD_DIM:HEAD_DIM + 1], carry[3][:, 0:1]]
        lo = lane < HEAD_DIM
        o_ref[...] = jnp.where(lo, carry[1] / ls[0], carry[3] / ls[1])
        lse_t = jnp.transpose(jnp.where(lo, carry[0] + jnp.log(ls[0]), carry[2] + jnp.log(ls[1])))
        lse_ref[...] = jnp.concatenate([lse_t[0:1, :], lse_t[HEAD_DIM:HEAD_DIM + 1, :]], axis=0)

    return pl.pallas_call(
        body, grid=(N_PAIRS, nq),
        in_specs=[pl.BlockSpec((blk, LANES), lambda p, i: (i, p)),
                  pl.BlockSpec((T, LANES), lambda p, i: (0, N_PAIRS + p)),
                  pl.BlockSpec((T, LANES), lambda p, i: (0, 2 * N_PAIRS + p)),
                  pl.BlockSpec((blk, LANES), lambda p, i: (i, 0)),
                  pl.BlockSpec((None, 2, T), lambda p, i: (p, 0, 0))],
        out_specs=[pl.BlockSpec((blk, LANES), lambda p, i: (i, p)), pl.BlockSpec((None, 2, blk), lambda p, i: (p, 0, i))],
        out_shape=[_sds((T, N_PAIRS * LANES), F32), _sds((N_PAIRS, 2, T), F32)],
        name=name, compiler_params=_cp(("parallel", "arbitrary")))(qkv, qkv, qkv, fcol, frow)


def _attn_delta(dcat, o, name, blk=512):
    T = o.shape[0]
    blk = _tile(T, blk)

    def body(d_ref, o_ref, out_ref):
        prod = d_ref[:, 256:768].astype(CDT).astype(F32) * o_ref[...]
        pt = jnp.transpose(prod)
        out_ref[...] = jnp.sum(pt.reshape(N_HEADS, HEAD_DIM, blk), axis=1)

    return pl.pallas_call(
        body, grid=(T // blk,),
        in_specs=[pl.BlockSpec((blk, 1024), lambda i: (i, 0)), pl.BlockSpec((blk, 512), lambda i: (i, 0))],
        out_specs=pl.BlockSpec((N_HEADS, blk), lambda i: (0, i)),
        out_shape=_sds((N_HEADS, T), F32), name=name, compiler_params=_cp(("parallel",)))(dcat, o)


def _attn_bwd(qkv, dcat, fcol, frow, lse, delta, name, blk=1024):
    T = qkv.shape[0]
    blk = _tile(T, blk)
    nq = T // blk
    nt = (((1,), (1,)), ((), ()))

    def body(q_ref, do_ref, k_ref, v_ref, fc_ref, fr_ref, lse_ref, dl_ref, dqt_ref, dk_ref, dv_ref, df_ref):
        p_id = pl.program_id(0)
        j = pl.program_id(1)

        @pl.when(j == 0)
        def _():
            dqt_ref[...] = jnp.zeros_like(dqt_ref)

        k2 = k_ref[...]
        v2 = v_ref[...]
        fc = fc_ref[...]
        lane = lax.broadcasted_iota(jnp.int32, (blk, LANES), 1)
        tri = lax.broadcasted_iota(jnp.int32, (blk, blk), 0) <= lax.broadcasted_iota(jnp.int32, (blk, blk), 1)
        masks = [_half_mask(k2.shape, a) for a in range(2)]
        kas = [jnp.where(hm, k2, jnp.zeros_like(k2)) * ATT_SCALE for hm in masks]
        kats = [jnp.transpose(ka) for ka in kas]
        vas = [jnp.where(hm, v2, jnp.zeros_like(v2)) for hm in masks]
        fks = [jnp.sum(jnp.where(lane == 2 * p_id + a, fc, 0.0), axis=1, keepdims=True) for a in range(2)]

        def tile(i, carry, masked):
            rows = pl.ds(pl.multiple_of(i * blk, blk), blk)
            qi = q_ref[rows, :]
            doi = do_ref[rows, :].astype(CDT)
            out = []
            dqt = None
            for a in range(2):
                dk_acc, dv_acc, df_acc = carry[3 * a:3 * a + 3]
                st = lax.dot_general(kas[a], qi, nt, preferred_element_type=F32)
                e = (st + (fr_ref[a:a + 1, rows] - fks[a])) - lse_ref[a:a + 1, rows]
                if masked:
                    e = jnp.where(tri, e, NEG)
                pt = jnp.exp(e)
                dpt = lax.dot_general(vas[a], doi, nt, preferred_element_type=F32)
                ds32 = pt * (dpt - dl_ref[a:a + 1, rows])
                dst = ds32.astype(CDT)
                df_acc = df_acc + jnp.sum(ds32, axis=1, keepdims=True)
                dv_acc = dv_acc + jnp.dot(pt.astype(CDT), doi, preferred_element_type=F32)
                dk_acc = dk_acc + jnp.dot(dst, qi, preferred_element_type=F32)
                part = jnp.dot(kats[a], dst, preferred_element_type=F32)
                dqt = part if dqt is None else dqt + part
                out += [dk_acc, dv_acc, df_acc]
            dqt_ref[:, rows] += dqt
            return tuple(out)

        init = (jnp.zeros((blk, LANES), F32), jnp.zeros((blk, LANES), F32), jnp.zeros((blk, 1), F32)) * 2
        carry = tile(j, init, True)
        carry = lax.fori_loop(j + 1, nq, lambda i, c: tile(i, c, False), carry)
        lo = lane < HEAD_DIM
        dk_ref[...] = (jnp.where(lo, carry[0], carry[3]) * ATT_SCALE).astype(dk_ref.dtype)
        dv_ref[...] = jnp.where(lo, carry[1], carry[4]).astype(dv_ref.dtype)
        df_ref[...] = -jnp.where(lo, carry[2], carry[5])

    res = pl.BlockSpec((T, LANES), lambda p, j: (0, p))
    rows = pl.BlockSpec((None, 2, T), lambda p, j: (p, 0, 0))
    kv_out = pl.BlockSpec((blk, LANES), lambda p, j: (j, p))
    return pl.pallas_call(
        body, grid=(N_PAIRS, nq),
        in_specs=[res, pl.BlockSpec((T, LANES), lambda p, j: (0, 2 + p)),
                  pl.BlockSpec((blk, LANES), lambda p, j: (j, N_PAIRS + p)),
                  pl.BlockSpec((blk, LANES), lambda p, j: (j, 2 * N_PAIRS + p)),
                  pl.BlockSpec((blk, LANES), lambda p, j: (j, 0)), rows, rows, rows],
        out_specs=[pl.BlockSpec((LANES, T), lambda p, j: (p, 0)), kv_out, kv_out, kv_out],
        out_shape=[_sds((N_PAIRS * LANES, T), F32), _sds((T, N_PAIRS * LANES), CDT), _sds((T, N_PAIRS * LANES), CDT),
                   _sds((T, N_PAIRS * LANES), F32)],
        name=name, compiler_params=_cp(("parallel", "arbitrary")))(qkv, dcat, qkv, qkv, fcol, frow, lse, delta)


def _mixer_fwd(x, wts, tag, dep=None):
    T = x.shape[0]
    h = _rms_fwd(x, wts["mix_norm"], f"{tag}_norm", dep)
    qkv = _mm([(h, wts["win_qkv"], False)], out_dtype=CDT, tm=1024, tn=768, name=f"{tag}_in_qkv")
    rest = _mm([(h, wts["win_rest"], False)], tm=1024, name=f"{tag}_in_rest")
    fcol, frow8 = _fgate_fwd(rest, wts["fbias"], f"{tag}_fgate")
    frow = frow8.reshape(N_PAIRS, 2, T)
    ya = _pool_fwd(rest, wts["pool_wbd"], wts["pool_scale"], f"{tag}_pool")
    o, lse = _attn_fwd(qkv, fcol, frow, f"{tag}_attn")
    yc, yconv = _conv_fwd(rest, wts["conv_w"], wts["conv_b"], wts["conv_ln_g"], wts["conv_ln_b"], f"{tag}_conv")
    cat = jnp.concatenate([ya, o.astype(CDT), yc], axis=1)
    y = _mm([(cat, wts["w_out"], False)], res=x, name=f"{tag}_out")
    return y, (x, h, qkv, rest, fcol, frow, o, lse, yconv, cat)


def _mixer_bwd(saved, wts, gout, tag, dep=None):
    x, h, qkv, rest, fcol, frow, o, lse, yconv, cat = saved
    T = x.shape[0]
    dcat = _mm([(gout, wts["w_out"], True)], name=f"{tag}_dcat", dep=dep)
    dwout = _mm_tn(cat, gout, name=f"{tag}_dwout")
    du, dpw, dpsc = _pool_bwd(dcat, rest, wts["pool_wbd"], wts["pool_scale"], f"{tag}_pool_bwd")
    delta = _attn_delta(dcat, o, f"{tag}_attn_delta").reshape(N_PAIRS, 2, T)
    dqt, dk, dv, dfk = _attn_bwd(qkv, dcat, fcol, frow, lse, delta, f"{tag}_attn_bwd")
    dq = dqt.T.astype(CDT)
    dz, dfb = _fgate_bwd(dfk, rest, wts["fbias"], f"{tag}_fgate_bwd")
    da, dg, dcw, dcb, dlg, dlb = _conv_bwd(dcat, yconv, rest, wts["conv_w"], wts["conv_ln_g"], wts["conv_ln_b"],
                                           f"{tag}_conv_bwd")
    dp_qkv = jnp.concatenate([dq, dk, dv], axis=1).astype(CDT)
    dp_rest = jnp.concatenate([du, da, dg, dz], axis=1)
    dwin_qkv = _mm_tn(h, dp_qkv, name=f"{tag}_dwin_qkv")
    dwin_rest = _mm_tn(h, dp_rest, name=f"{tag}_dwin_rest")
    gin, dgamma = _mm_norm_bwd([(dp_qkv, wts["win_qkv"], True), (dp_rest, wts["win_rest"], True)], x, wts["mix_norm"],
                               gout, name=f"{tag}_dh_norm_bwd")
    dwin = _split_win(dwin_qkv, dwin_rest, f"{tag}_dwin_split")
    dpool_w = jnp.stack([dpw[64 * g:64 * g + 64, 64 * g:64 * g + 64] for g in range(4)])
    grads = dict(mix_norm=dgamma[0], w_in=dwin, pool_w=dpool_w, pool_scale=dpsc[0], forget_bias=dfb[0, 0:N_HEADS],
                 conv_w=dcw[0:CONV_K], conv_b=dcb[0], conv_ln_g=dlg[0], conv_ln_b=dlb[0], w_out=dwout)
    return gin, grads


def _rep_layer(rep, l):
    pw = rep["pool_w"][l].astype(CDT)
    wbd = jnp.zeros((256, 256), CDT)
    for g in range(4):
        wbd = lax.dynamic_update_slice(wbd, pw[g], (64 * g, 64 * g))
    return dict(
        ffn1_norm=rep["ffn1_norm"][l][None], ffn2_norm=rep["ffn2_norm"][l][None], mix_norm=rep["mix_norm"][l][None],
        fbias=jnp.pad(rep["forget_bias"][l], (0, LANES - N_HEADS))[None],
        pool_wbd=wbd, pool_scale=rep["pool_scale"][l][None], conv_b=rep["conv_b"][l][None],
        conv_ln_g=rep["conv_ln_g"][l][None], conv_ln_b=rep["conv_ln_b"][l][None])


def _local_step(x, target, rep, weights_for, grads_ready):
    depth = rep["ffn1_norm"].shape[0]
    kept = []
    for l in range(depth):
        r = _rep_layer(rep, l)
        w1, dep = weights_for(l, "ffn1", x)
        x, s1 = _ffn_fwd(x, r["ffn1_norm"], w1["w_gate"], w1["w_up"], w1["w_down"], f"l{l}_ffn1", dep)
        wm, dep = weights_for(l, "mix", x)
        wm = dict(r, win_qkv=wm["win_qkv"], win_rest=wm["win_rest"], w_out=wm["w_out"],
                  conv_w=jnp.pad(wm["conv_w"], ((0, CONV_HALO - CONV_K), (0, 0))))
        x, s2 = _mixer_fwd(x, wm, f"l{l}_mix", dep)
        w2, dep = weights_for(l, "ffn2", x)
        x, s3 = _ffn_fwd(x, r["ffn2_norm"], w2["w_gate"], w2["w_up"], w2["w_down"], f"l{l}_ffn2", dep)
        kept.append((r, w1, wm, w2, s1, s2, s3))
    loss, g, dfinal = _loss_bwd(x, rep["final_norm"][None], target, "loss_head")
    dep = grads_ready(None, "final", dict(final_norm=dfinal[0]))
    for l in reversed(range(depth)):
        r, w1, wm, w2, s1, s2, s3 = kept[l]

        def ffn_grads(which, l=l):
            return lambda gr: grads_ready(l, which, {f"{which}_{k}": v for k, v in gr.items()})

        g, dn = _ffn_bwd(s3, r["ffn2_norm"], w2["w_gate"], w2["w_up"], w2["w_down"], g, f"l{l}_ffn2", dep, ffn_grads("ffn2"))
        grads_ready(l, "norm", dict(ffn2_norm=dn[0]))
        g, gm = _mixer_bwd(s2, wm, g, f"l{l}_mix")
        dep = grads_ready(l, "mix", gm)
        g, dn = _ffn_bwd(s1, r["ffn1_norm"], w1["w_gate"], w1["w_up"], w1["w_down"], g, f"l{l}_ffn1", dep, ffn_grads("ffn1"))
        dep = grads_ready(l, "norm", dict(ffn1_norm=dn[0]))
    return loss, g


def _mesh_pos():
    return lax.axis_index("x"), lax.axis_index("y"), lax.axis_index("c")


def _dev_block(ref, dev, by_rows):
    if by_rows:
        r = ref.shape[1] // N_DEV
        return ref.at[:, pl.ds(dev * r, r), :]
    return ref.at[dev]


def _all_gather(shards, by_rows, name):
    n_arr = len(shards)
    out_shape = [_sds((s.shape[0], N_DEV * s.shape[1], s.shape[2]) if br else (N_DEV,) + s.shape, s.dtype)
                 for s, br in zip(shards, by_rows)]

    def body(*refs):
        xs, outs = refs[:n_arr], refs[n_arr:2 * n_arr]
        send_sems, recv_sems, local_sems = refs[2 * n_arr:]
        x, y, c = _mesh_pos()
        me, sibling = (x, y, c), (x, y, 1 - c)
        chips = [(1 - x, y), (x, 1 - y), (1 - x, 1 - y)]

        def rows(a, px, py, pc):
            return _dev_block(outs[a], 4 * px + 2 * py + pc, by_rows[a])

        def copy(k, a, block, to, src=None):
            return pltpu.make_async_remote_copy(
                src_ref=rows(a, *block) if src is None else src, dst_ref=rows(a, *block),
                send_sem=send_sems.at[k, a], recv_sem=recv_sems.at[k, a],
                device_id=to, device_id_type=pl.DeviceIdType.MESH)

        arrs = range(n_arr)
        mine = [pltpu.make_async_copy(xs[a], rows(a, *me), local_sems.at[a]) for a in arrs]
        for cp in mine:
            cp.start()
        first = [copy(0, a, me, sibling, src=xs[a]) for a in arrs]
        first += [copy(1 + j, a, me, (*chip, c), src=xs[a]) for j, chip in enumerate(chips) for a in arrs]
        for cp in first:
            cp.start()
        passed = []
        for j, chip in enumerate(chips):
            for a in arrs:
                copy(1 + j, a, (*chip, c), me).wait_recv()
                passed.append(copy(4 + j, a, (*chip, c), sibling))
                passed[-1].start()
        for a in arrs:
            copy(0, a, sibling, me).wait_recv()
        for j, chip in enumerate(chips):
            for a in arrs:
                copy(4 + j, a, (*chip, 1 - c), me).wait_recv()
        for cp in first + passed:
            cp.wait_send()
        for cp in mine:
            cp.wait()

    hbm = pl.BlockSpec(memory_space=pl.ANY)
    return pl.pallas_call(
        body, out_shape=out_shape, in_specs=[hbm] * n_arr, out_specs=[hbm] * n_arr,
        scratch_shapes=[pltpu.SemaphoreType.DMA((7, n_arr)), pltpu.SemaphoreType.DMA((7, n_arr)),
                        pltpu.SemaphoreType.DMA((n_arr,))],
        name=name)(*shards)


def _exchange(parts, by_rows, name):
    n_arr = len(parts)
    out_shape = [_sds((N_DEV, p.shape[0], p.shape[1] // N_DEV, p.shape[2]) if br else p.shape, p.dtype)
                 for p, br in zip(parts, by_rows)]

    def body(*refs):
        ps, outs = refs[:n_arr], refs[n_arr:2 * n_arr]
        send_sems, recv_sems, local_sems = refs[2 * n_arr:]
        x, y, c = _mesh_pos()
        my = 4 * x + 2 * y + c
        arrs = range(n_arr)
        mine = [pltpu.make_async_copy(_dev_block(ps[a], my, by_rows[a]), outs[a].at[my], local_sems.at[a]) for a in arrs]
        for cp in mine:
            cp.start()
        copies = []
        for k in range(1, N_DEV):
            px, py, pc = x ^ (k >> 2), y ^ ((k >> 1) & 1), c ^ (k & 1)
            for a in arrs:
                copies.append(pltpu.make_async_remote_copy(
                    src_ref=_dev_block(ps[a], 4 * px + 2 * py + pc, by_rows[a]), dst_ref=outs[a].at[my],
                    send_sem=send_sems.at[k - 1, a], recv_sem=recv_sems.at[k - 1, a],
                    device_id=(px, py, pc), device_id_type=pl.DeviceIdType.MESH))
        for cp in copies:
            cp.start()
        for cp in copies:
            cp.wait()
        for cp in mine:
            cp.wait()

    hbm = pl.BlockSpec(memory_space=pl.ANY)
    return pl.pallas_call(
        body, out_shape=out_shape, in_specs=[hbm] * n_arr, out_specs=[hbm] * n_arr,
        scratch_shapes=[pltpu.SemaphoreType.DMA((7, n_arr)), pltpu.SemaphoreType.DMA((7, n_arr)),
                        pltpu.SemaphoreType.DMA((n_arr,))],
        name=name)(*parts)


def _peer_copies(srcs, lands, send_sems, recv_sems, gather, by_rows):
    n_arr = len(srcs)
    x, y, c = _mesh_pos()
    my = 4 * x + 2 * y + c
    out = []
    for k in range(1, N_DEV):
        px, py, pc = x ^ (k >> 2), y ^ ((k >> 1) & 1), c ^ (k & 1)
        peer = 4 * px + 2 * py + pc
        for a in range(n_arr):
            src = srcs[a] if gather else _dev_block(srcs[a], peer, by_rows[a])
            dst = _dev_block(lands[a], my, by_rows[a]) if gather else lands[a].at[my]
            out.append(pltpu.make_async_remote_copy(
                src_ref=src, dst_ref=dst, send_sem=send_sems.at[(k - 1) * n_arr + a],
                recv_sem=recv_sems.at[(k - 1) * n_arr + a], device_id=(px, py, pc), device_id_type=pl.DeviceIdType.MESH))
    return out


def _land_shape(s, gather, by_rows):
    if gather:
        return (s.shape[0], N_DEV * s.shape[1], s.shape[2]) if by_rows else (N_DEV,) + s.shape
    return (N_DEV, s.shape[0], s.shape[1] // N_DEV, s.shape[2]) if by_rows else s.shape


_HBM = pl.BlockSpec(memory_space=pltpu.HBM)
_SEM = pl.BlockSpec(memory_space=pltpu.SEMAPHORE)


def _xfer_start(srcs, gather, by_rows, name, dep=None):
    n = len(srcs)
    lands = [lax.empty(_land_shape(s, gather, br), s.dtype) for s, br in zip(srcs, by_rows)]
    ins = [pltpu.with_memory_space_constraint(a, pltpu.HBM) for a in list(srcs) + lands]
    dspec, darg = _dep(dep)

    def body(*refs):
        s = 2 * n + len(darg)
        for cp in _peer_copies(refs[:n], refs[n:2 * n], refs[s], refs[s + 1], gather, by_rows):
            cp.start()
        for cp in _own_copies(refs[:n], refs[n:2 * n], refs[s + 2], gather, by_rows):
            cp.start()
        refs[-1][...] = jnp.zeros_like(refs[-1])

    sems = pltpu.SemaphoreType.DMA(((N_DEV - 1) * n,))
    outs = pl.pallas_call(
        body, name=name,
        out_shape=(sems, sems, pltpu.SemaphoreType.DMA((n,)), *[pltpu.HBM(a.shape, a.dtype) for a in ins],
                   _sds((8, LANES), F32)),
        in_specs=[_HBM] * (2 * n) + dspec,
        out_specs=(_SEM, _SEM, _SEM, *[_HBM] * (2 * n), pl.BlockSpec(memory_space=pltpu.VMEM)),
        input_output_aliases={i: 3 + i for i in range(2 * n)},
        compiler_params=pltpu.CompilerParams(has_side_effects=pltpu.SideEffectType.DATAFLOW_SIDE_EFFECTING))(*ins, *darg)
    return outs[0], outs[1], list(outs[3:-1]), outs[-1], outs[2]


def _own_copies(srcs, lands, sems, gather, by_rows):
    x, y, c = _mesh_pos()
    my = 4 * x + 2 * y + c
    out = []
    for a in range(len(srcs)):
        if gather:
            out.append(pltpu.make_async_copy(srcs[a], _dev_block(lands[a], my, by_rows[a]), sems.at[a]))
        else:
            out.append(pltpu.make_async_copy(_dev_block(srcs[a], my, by_rows[a]), lands[a].at[my], sems.at[a]))
    return out


def _xfer_wait(started, after, gather, by_rows, name):
    send_sems, recv_sems, bufs, _, local_sems = started
    n = len(bufs) // 2

    def body(*refs):
        for cp in _peer_copies(refs[:n], refs[n:2 * n], refs[2 * n], refs[2 * n + 1], gather, by_rows):
            cp.wait_send()
            cp.wait_recv()
        for cp in _own_copies(refs[:n], refs[n:2 * n], refs[2 * n + 2], gather, by_rows):
            cp.wait()

    outs = pl.pallas_call(
        body, name=name, out_shape=tuple(pltpu.HBM(a.shape, a.dtype) for a in bufs),
        in_specs=[_HBM] * (2 * n) + [_SEM, _SEM, _SEM, pl.BlockSpec(memory_space=pl.ANY)],
        out_specs=tuple([_HBM] * (2 * n)), input_output_aliases={i: i for i in range(2 * n)},
        compiler_params=pltpu.CompilerParams(has_side_effects=pltpu.SideEffectType.DATAFLOW_SIDE_EFFECTING))(
            *bufs, send_sems, recv_sems, local_sems, after)
    return list(outs[n:])


def _adam_update(g, w, m, v):
    c1 = 1.0 - ADAM_B1 ** ADAM_STEP
    c2 = 1.0 - ADAM_B2 ** ADAM_STEP
    nm = ADAM_B1 * m + (1.0 - ADAM_B1) * g
    nv = ADAM_B2 * v + (1.0 - ADAM_B2) * (g * g)
    return -ADAM_LR * ((nm / c1) / (jnp.sqrt(nv / c2) + ADAM_EPS) + ADAM_WD * w), nm, nv


def _adamw_body(p_ref, w_ref, m_ref, v_ref, g_ref, d_ref, nm_ref, nv_ref):
    g = p_ref[0]
    for i in range(1, N_DEV):
        g = g + p_ref[i]
    g_ref[...] = g
    d_ref[...], nm_ref[...], nv_ref[...] = _adam_update(g, w_ref[...], m_ref[...], v_ref[...])


def _adamw(parts, w, m, v, name, tr=1536):
    R = w.shape[0]
    tr = max(t for t in range(8, tr + 1, 8) if R % t == 0)

    def body(*refs):
        _adamw_body(*refs)

    row = pl.BlockSpec((tr, LANES), lambda i: (i, 0))
    return pl.pallas_call(
        body, grid=(R // tr,),
        in_specs=[pl.BlockSpec((N_DEV, tr, LANES), lambda i: (0, i, 0)), row, row, row],
        out_specs=[row, row, row, row], out_shape=[_sds((R, LANES), F32)] * 4,
        name=name, compiler_params=_cp(("parallel",)))(parts, w, m, v)


def _adamw_split(recvs, w, m, v, name, tr):
    depth, r, c = w.shape
    assert depth == len(recvs)
    tr = _tile(r, tr)

    def body(*refs):
        layer = pl.program_id(0)
        for ll in range(depth):
            @pl.when(layer == ll)
            def _(ll=ll):
                _adamw_body(refs[ll], *refs[depth:])

    wspec = pl.BlockSpec((None, tr, c), lambda l, i: (l, i, 0))
    rspecs = [pl.BlockSpec((N_DEV, None, tr, c), lambda l, i, ll=ll, t=t: (0, t, jnp.where(l == ll, i, 0), 0))
              for ll, (_, t) in enumerate(recvs)]
    return pl.pallas_call(
        body, grid=(depth, r // tr), in_specs=rspecs + [wspec, wspec, wspec],
        out_specs=[wspec] * 4, out_shape=[_sds(w.shape, F32)] * 4,
        name=name, compiler_params=_cp(("arbitrary", "arbitrary")))(*[a for a, _ in recvs], w, m, v)


def _merge_win(g, name, tr=256):
    _, nt, K, n = g.shape
    tr = _tile(K, tr)

    def body(g_ref, q_ref, r_ref):
        full = jnp.concatenate([g_ref[j] for j in range(N_DEV)], axis=1)
        q_ref[...] = full[:, 256:1792]
        zpad = jnp.zeros((tr, REST_W - 776), full.dtype)
        r_ref[...] = jnp.concatenate([full[:, 0:256], full[:, 1800:2312], full[:, 1792:1800], zpad], axis=1)

    return pl.pallas_call(
        body, grid=(nt, K // tr),
        in_specs=[pl.BlockSpec((N_DEV, None, tr, n), lambda t, i: (0, t, i, 0))],
        out_specs=[pl.BlockSpec((None, tr, 1536), lambda t, i: (t, i, 0)), pl.BlockSpec((None, tr, REST_W), lambda t, i: (t, i, 0))],
        out_shape=[_sds((nt, K, 1536), g.dtype), _sds((nt, K, REST_W), g.dtype)],
        name=name, compiler_params=_cp(("parallel", "parallel")))(g)


def _split_win(dq, dr, name, tr=256):
    K = dq.shape[0]
    tr = _tile(K, tr)
    n = (dq.shape[1] + 776) // N_DEV

    def body(q_ref, r_ref, o_ref):
        r = r_ref[...]
        full = jnp.concatenate([r[:, 0:256], q_ref[...], r[:, 768:776], r[:, 256:768]], axis=1)
        for j in range(N_DEV):
            o_ref[j] = full[:, n * j:n * (j + 1)]

    return pl.pallas_call(
        body, grid=(K // tr,),
        in_specs=[pl.BlockSpec((tr, dq.shape[1]), lambda i: (i, 0)), pl.BlockSpec((tr, REST_W), lambda i: (i, 0))],
        out_specs=pl.BlockSpec((N_DEV, tr, n), lambda i: (0, i, 0)),
        out_shape=_sds((N_DEV, K, n), F32), name=name, compiler_params=_cp(("parallel",)))(dq, dr)


WEIGHTS = ["ffn1_norm", "ffn1_w_gate", "ffn1_w_up", "ffn1_w_down", "mix_norm", "w_in", "pool_w", "pool_scale",
           "forget_bias", "conv_w", "conv_b", "conv_ln_g", "conv_ln_b", "w_out", "ffn2_norm", "ffn2_w_gate",
           "ffn2_w_up", "ffn2_w_down", "final_norm"]
FFN_PARTS = ("w_gate", "w_up", "w_down")
FFN_T = ["ffn1_w_gate", "ffn1_w_up", "ffn2_w_gate", "ffn2_w_up"]
BIG = FFN_T + ["ffn1_w_down", "ffn2_w_down", "w_in", "w_out"]
SMALL = [n for n in WEIGHTS if n not in BIG]


def _padded(n):
    return -(-n // PACK_ALIGN) * PACK_ALIGN


def _flat_pad(a):
    f = a.reshape(-1)
    return jnp.pad(f, (0, _padded(f.shape[0]) - f.shape[0]))


def _split8(a, axis):
    shp = a.shape
    a = a.reshape(shp[:axis] + (N_DEV, shp[axis] // N_DEV) + shp[axis + 1:])
    return jnp.moveaxis(a, axis, 0)


def _merge8(a, axis):
    a = jnp.moveaxis(a, 0, axis)
    shp = a.shape
    return a.reshape(shp[:axis] + (shp[axis] * shp[axis + 1],) + shp[axis + 2:])


def _pack_small(arrs):
    return jnp.concatenate([_flat_pad(arrs[n]) for n in SMALL]).reshape(-1, LANES)


def _pack_small_parts(grads):
    cols = []
    for n in SMALL:
        g = grads[n]
        if n == "conv_w":
            s = _split8(g, 2).reshape(N_DEV, -1)
        else:
            s = jnp.broadcast_to(g.reshape(1, -1), (N_DEV, g.size))
        cols.append(jnp.pad(s, ((0, 0), (0, _padded(s.shape[1]) - s.shape[1]))))
    return jnp.concatenate(cols, axis=1).reshape(N_DEV, -1, LANES)


def _unpack_small(buf, like):
    flat = buf.reshape(-1)
    out, off = {}, 0
    for n in SMALL:
        size = like[n].size
        out[n] = flat[off:off + size].reshape(like[n].shape)
        off += _padded(size)
    return out


class _Comm:
    def __init__(self, w):
        self.w = w
        self.bf = {n: (jnp.swapaxes(w[n], 1, 2) if n in FFN_T else w[n]).astype(CDT) for n in BIG}
        self.ready = {}
        self.grads = {}

    def _ffn_shards(self, l, which):
        return jnp.stack([self.bf[f"{which}_{k}"][l] for k in FFN_PARTS])

    def _put_ffn(self, l, which, rows, t):
        self.ready[(l, which)] = dict(w_gate=rows[t], w_up=rows[t + 1], w_down=rows[t + 2])

    def weights_for(self, l, stage, x):
        bf = self.bf
        dep = None
        if (l, stage) == (0, "ffn1"):
            gd, = _all_gather([self._ffn_shards(0, "ffn1")], [True], "gather_l0_ffn1")
            self._put_ffn(0, "ffn1", gd, 0)
            self.started = _xfer_start([bf["w_in"][0:1], bf["w_out"][0:1], self.w["conv_w"]], True,
                                       [False, True, False], "gather_mix0_start", dep=gd)
            dep = self.started[3]
        elif (l, stage) == (0, "mix"):
            gi, go, gc = _xfer_wait(self.started, x, True, [False, True, False], "gather_mix0_wait")
            q, r = _merge_win(gi, "merge_l0_w_in")
            self.conv_w = _merge8(gc, 2)
            self.ready[(0, "mix")] = dict(win_qkv=q[0], win_rest=r[0], w_out=go[0], conv_w=self.conv_w[0])
            rows = jnp.concatenate([self._ffn_shards(0, "ffn2"), self._ffn_shards(1, "ffn1"), self._ffn_shards(1, "ffn2")])
            self.started = _xfer_start([rows, bf["w_in"][1:2], bf["w_out"][1:2]], True, [True, False, True],
                                       "gather_rest_start")
            dep = self.started[3]
        elif (l, stage) == (0, "ffn2"):
            gd, gi, go = _xfer_wait(self.started, x, True, [True, False, True], "gather_rest_wait")
            self._put_ffn(0, "ffn2", gd, 0)
            self._put_ffn(1, "ffn1", gd, 3)
            self._put_ffn(1, "ffn2", gd, 6)
            q, r = _merge_win(gi, "merge_l1_w_in")
            self.ready[(1, "mix")] = dict(win_qkv=q[0], win_rest=r[0], w_out=go[0], conv_w=self.conv_w[1])
        return self.ready[(l, stage)], dep

    def grads_ready(self, l, stage, grads):
        for n, v in grads.items():
            self.grads[(l, n)] = v
        gr = self.grads

        def ffn_rows(layer, which, parts=FFN_PARTS):
            return [gr[(layer, f"{which}_{k}")][None] for k in parts]

        if l == 1 and "ffn1_w_gate" in grads:
            self.sent1 = _xfer_start(
                ffn_rows(1, "ffn1") + ffn_rows(1, "ffn2") + [gr[(1, "w_in")][:, None], gr[(1, "w_out")][None]],
                False, [True] * 6 + [False, True], "grads_l1_start")
            return self.sent1[3]
        if l == 0 and "ffn2_w_gate" in grads:
            self.sent_ffn2 = _xfer_start(ffn_rows(0, "ffn2"), False, [True] * 3, "grads_l0_ffn2_start")
            return self.sent_ffn2[3]
        if (l, stage) == (0, "mix"):
            self.sent_mix = _xfer_start([gr[(0, "w_in")][:, None], gr[(0, "w_out")][None]], False, [False, True],
                                        "grads_l0_mix_start")
            return self.sent_mix[3]
        if l == 0 and "ffn1_w_down" in grads:
            self.sent_down = _xfer_start([gr[(0, "ffn1_w_down")][None]], False, [True], "grads_l0_ffn1_down_start")
            return self.sent_down[3]
        if l == 0 and "ffn1_w_gate" in grads:
            self.sent_gu = _xfer_start(ffn_rows(0, "ffn1", FFN_PARTS[:2]), False, [True] * 2,
                                       "grads_l0_ffn1_gate_up_start")
            return self.sent_gu[3]
        return None

    def finish(self, m, v, after):
        w, gr = self.w, self.grads
        depth = range(w["w_in"].shape[0])
        small = {n: (gr[(None, n)] if n == "final_norm" else jnp.stack([gr[(l, n)] for l in depth])) for n in SMALL}
        *r1, i1, o1 = _xfer_wait(self.sent1, after, False, [True] * 6 + [False, True], "grads_l1_wait")
        r2 = _xfer_wait(self.sent_ffn2, after, False, [True] * 3, "grads_l0_ffn2_wait")
        i0, o0 = _xfer_wait(self.sent_mix, after, False, [False, True], "grads_l0_mix_wait")

        def adam(n, recvs, tr):
            if n in FFN_T:
                out = _adamw_split(recvs, *[jnp.swapaxes(t[n], 1, 2) for t in (w, m, v)], f"adamw_{n}", tr)
                return [jnp.swapaxes(o, 1, 2) for o in out]
            return _adamw_split(recvs, w[n], m[n], v[n], f"adamw_{n}", tr)

        res = {}
        for t, k in enumerate(FFN_PARTS):
            res[f"ffn2_{k}"] = adam(f"ffn2_{k}", [(r2[t], 0), (r1[3 + t], 0)], 176)
        res["w_in"] = adam("w_in", [(i0, 0), (i1, 0)], 256)
        res["w_out"] = adam("w_out", [(o0, 0), (o1, 0)], 128)
        rs, = _exchange([_pack_small_parts(small)], [False], "exchange_small")
        r0, = _xfer_wait(self.sent_down, res["w_out"][0], False, [True], "grads_l0_ffn1_down_wait")
        res["ffn1_w_down"] = adam("ffn1_w_down", [(r0, 0), (r1[2], 0)], 176)
        g0 = _xfer_wait(self.sent_gu, res["ffn1_w_down"][0], False, [True] * 2, "grads_l0_ffn1_gate_up_wait")
        res["ffn1_w_gate"] = adam("ffn1_w_gate", [(g0[0], 0), (r1[0], 0)], 176)
        res["ffn1_w_up"] = adam("ffn1_w_up", [(g0[1], 0), (r1[1], 0)], 176)
        packed = _adamw(rs, _pack_small(w), _pack_small(m), _pack_small(v), "adamw_small")
        unpacked = [_unpack_small(b, w) for b in packed]
        for n in SMALL:
            res[n] = [u[n] for u in unpacked]
        return res


def kernel(x, ffn1_norm, ffn1_w_gate, ffn1_w_up, ffn1_w_down, mix_norm, w_in, pool_w, pool_scale, forget_bias, conv_w, conv_b, conv_ln_g, conv_ln_b, w_out, ffn2_norm, ffn2_w_gate, ffn2_w_up, ffn2_w_down, final_norm, loss_target, m_ffn1_norm, m_ffn1_w_gate, m_ffn1_w_up, m_ffn1_w_down, m_mix_norm, m_w_in, m_pool_w, m_pool_scale, m_forget_bias, m_conv_w, m_conv_b, m_conv_ln_g, m_conv_ln_b, m_w_out, m_ffn2_norm, m_ffn2_w_gate, m_ffn2_w_up, m_ffn2_w_down, m_final_norm, v_ffn1_norm, v_ffn1_w_gate, v_ffn1_w_up, v_ffn1_w_down, v_mix_norm, v_w_in, v_pool_w, v_pool_scale, v_forget_bias, v_conv_w, v_conv_b, v_conv_ln_g, v_conv_ln_b, v_w_out, v_ffn2_norm, v_ffn2_w_gate, v_ffn2_w_up, v_ffn2_w_down, v_final_norm):
    w = dict(zip(WEIGHTS, (ffn1_norm, ffn1_w_gate, ffn1_w_up, ffn1_w_down, mix_norm, w_in, pool_w, pool_scale, forget_bias,
                           conv_w, conv_b, conv_ln_g, conv_ln_b, w_out, ffn2_norm, ffn2_w_gate, ffn2_w_up, ffn2_w_down,
                           final_norm)))
    m = dict(zip(WEIGHTS, (m_ffn1_norm, m_ffn1_w_gate, m_ffn1_w_up, m_ffn1_w_down, m_mix_norm, m_w_in, m_pool_w, m_pool_scale,
                           m_forget_bias, m_conv_w, m_conv_b, m_conv_ln_g, m_conv_ln_b, m_w_out, m_ffn2_norm, m_ffn2_w_gate,
                           m_ffn2_w_up, m_ffn2_w_down, m_final_norm)))
    v = dict(zip(WEIGHTS, (v_ffn1_norm, v_ffn1_w_gate, v_ffn1_w_up, v_ffn1_w_down, v_mix_norm, v_w_in, v_pool_w, v_pool_scale,
                           v_forget_bias, v_conv_w, v_conv_b, v_conv_ln_g, v_conv_ln_b, v_w_out, v_ffn2_norm, v_ffn2_w_gate,
                           v_ffn2_w_up, v_ffn2_w_down, v_final_norm)))
    comm = _Comm(w)
    loss_row, gx = _local_step(x[0], loss_target[0], w, comm.weights_for, comm.grads_ready)
    loss = lax.psum(loss_row[0, 0], ("x", "y", "c"))
    res = comm.finish(m, v, gx)
    return (loss, gx[None], *[res[n][i] for i in range(4) for n in WEIGHTS])
```

```python
import math

import numpy as np
import jax
import jax.numpy as jnp
from jax import lax
from jax.experimental import pallas as pl
from jax.experimental.pallas import tpu as pltpu

F32 = jnp.float32
CDT = jnp.bfloat16
NORM_EPS = 1e-6
N_DEV = 8
LANES = 128
PACK_ALIGN = 8 * LANES
VMEM_LIMIT = 48 * 1024 * 1024

POOL_WINDOWS = (2, 4, 8, 16)
POOL_HALO = 16
CONV_K = 31
CONV_HALO = 32
HEAD_DIM = 64
N_HEADS = 8
N_PAIRS = N_HEADS // 2
ATT_SCALE = 1.0 / math.sqrt(HEAD_DIM)
NEG = -1e30

ADAM_LR, ADAM_B1, ADAM_B2, ADAM_EPS, ADAM_WD, ADAM_STEP = 0.001, 0.9, 0.999, 1e-08, 0.01, 10

REST_W = 896
REST_Z_BLK = 6


def _cp(sem):
    return pltpu.CompilerParams(dimension_semantics=sem, vmem_limit_bytes=VMEM_LIMIT)


def _tile(n, pref):
    t = min(n, pref)
    assert n % t == 0, (n, pref)
    return t


def _sigmoid(x):
    return 1.0 / (1.0 + jnp.exp(-x))


def _sds(shape, dtype):
    return jax.ShapeDtypeStruct(shape, dtype)


_ANY = pl.BlockSpec(memory_space=pl.ANY)


def _dep(dep):
    return ([], []) if dep is None else ([_ANY], [dep])


def _wshape(w):
    return w[0].shape[1:] if isinstance(w, tuple) else w.shape


def _wspec(w, block, index_map):
    if not isinstance(w, tuple):
        return w, pl.BlockSpec(block, index_map)
    arr, t = w
    return arr, pl.BlockSpec((None,) + block, lambda *g: (t,) + index_map(*g))


def _rms_fwd(x, g, name, dep=None):
    T, D = x.shape
    tm = _tile(T, 1024)

    def body(x_ref, g_ref, *rest):
        o_ref = rest[-1]
        xv = x_ref[...]
        r = lax.rsqrt(jnp.mean(xv * xv, axis=-1, keepdims=True) + NORM_EPS)
        o_ref[...] = (xv * r * g_ref[...]).astype(o_ref.dtype)

    dspec, darg = _dep(dep)
    return pl.pallas_call(
        body, grid=(T // tm,),
        in_specs=[pl.BlockSpec((tm, D), lambda i: (i, 0)), pl.BlockSpec((1, D), lambda i: (0, 0))] + dspec,
        out_specs=pl.BlockSpec((tm, D), lambda i: (i, 0)),
        out_shape=_sds((T, D), CDT), name=name, compiler_params=_cp(("parallel",)))(x, g, *darg)


def _rms_bwd(x, g, dh, gres, name):
    T, D = x.shape
    tm = _tile(T, 512)

    def body(x_ref, g_ref, dh_ref, gres_ref, gin_ref, dg_ref):
        i = pl.program_id(0)
        xv = x_ref[...]
        d = dh_ref[...]
        r = lax.rsqrt(jnp.mean(xv * xv, axis=-1, keepdims=True) + NORM_EPS)
        xh = xv * r
        dxh = d * g_ref[...]
        c = jnp.mean(dxh * xh, axis=-1, keepdims=True)
        gin_ref[...] = gres_ref[...] + r * (dxh - xh * c)
        part = jnp.sum(d * xh, axis=0, keepdims=True)

        @pl.when(i == 0)
        def _():
            dg_ref[...] = part

        @pl.when(i > 0)
        def _():
            dg_ref[...] += part

    row = pl.BlockSpec((tm, D), lambda i: (i, 0))
    vec = pl.BlockSpec((1, D), lambda i: (0, 0))
    return pl.pallas_call(
        body, grid=(T // tm,), in_specs=[row, vec, row, row], out_specs=[row, vec],
        out_shape=[_sds((T, D), F32), _sds((1, D), F32)], name=name, compiler_params=_cp(("arbitrary",)))(x, g, dh, gres)


def _loss_bwd(x, g, target, name):
    T, D = x.shape
    tm = _tile(T, 512)

    def body(x_ref, g_ref, t_ref, loss_ref, dx_ref, dg_ref):
        i = pl.program_id(0)
        xv = x_ref[...]
        gv = g_ref[...]
        r = lax.rsqrt(jnp.mean(xv * xv, axis=-1, keepdims=True) + NORM_EPS)
        xh = xv * r
        err = xh * gv - t_ref[...]
        lpart = 0.5 * jnp.sum(jnp.mean(err * err, axis=-1, keepdims=True), axis=0, keepdims=True)
        dy = err * (1.0 / D)
        dxh = dy * gv
        c = jnp.mean(dxh * xh, axis=-1, keepdims=True)
        dx_ref[...] = r * (dxh - xh * c)
        part = jnp.sum(dy * xh, axis=0, keepdims=True)
        lrow = jnp.broadcast_to(lpart, (1, LANES))

        @pl.when(i == 0)
        def _():
            dg_ref[...] = part
            loss_ref[...] = lrow

        @pl.when(i > 0)
        def _():
            dg_ref[...] += part
            loss_ref[...] += lrow

    row = pl.BlockSpec((tm, D), lambda i: (i, 0))
    vec = pl.BlockSpec((1, D), lambda i: (0, 0))
    return pl.pallas_call(
        body, grid=(T // tm,), in_specs=[row, vec, row],
        out_specs=[pl.BlockSpec((1, LANES), lambda i: (0, 0)), row, vec],
        out_shape=[_sds((1, LANES), F32), _sds((T, D), F32), _sds((1, D), F32)],
        name=name, compiler_params=_cp(("arbitrary",)))(x, g, target)


def _mm(pairs, *, name, res=None, alpha=1.0, out_dtype=F32, tm=512, tn=None, dep=None):
    T = pairs[0][0].shape[0]
    N = _wshape(pairs[0][1])[0 if pairs[0][2] else 1]
    tm = _tile(T, tm)
    tn = N if tn is None else _tile(N, tn)
    flags = [p[2] for p in pairs]
    n_in = 2 * len(pairs)

    def body(*refs):
        o_ref = refs[-1]
        acc = None
        for p, bt in enumerate(flags):
            a = refs[2 * p][...].astype(CDT)
            b = refs[2 * p + 1][...]
            dims = (((1,), (1,)), ((), ())) if bt else (((1,), (0,)), ((), ()))
            d = lax.dot_general(a, b, dims, preferred_element_type=F32)
            acc = d if acc is None else acc + d
        if alpha != 1.0:
            acc = acc * alpha
        if res is not None:
            acc = refs[n_in][...] + acc
        o_ref[...] = acc.astype(o_ref.dtype)

    in_specs, args = [], []
    for a, b, bt in pairs:
        K = a.shape[1]
        in_specs.append(pl.BlockSpec((tm, K), lambda i, j: (i, 0)))
        b, bspec = _wspec(b, (tn, K), lambda i, j: (j, 0)) if bt else _wspec(b, (K, tn), lambda i, j: (0, j))
        in_specs.append(bspec)
        args += [a, b]
    if res is not None:
        in_specs.append(pl.BlockSpec((tm, tn), lambda i, j: (i, j)))
        args.append(res)
    dspec, darg = _dep(dep)
    in_specs += dspec
    args += darg
    return pl.pallas_call(
        body, grid=(T // tm, N // tn), in_specs=in_specs,
        out_specs=pl.BlockSpec((tm, tn), lambda i, j: (i, j)),
        out_shape=_sds((T, N), out_dtype), name=name, compiler_params=_cp(("parallel", "arbitrary")))(*args)


def _mm_norm_bwd(pairs, x, g, gres, *, name, tm=256, dep=None):
    T, D = x.shape
    tm = _tile(T, tm)
    n_in = 2 * len(pairs)
    flags = [p[2] for p in pairs]

    def body(*refs):
        x_ref, g_ref, gres_ref = refs[n_in:n_in + 3]
        gin_ref, dg_ref = refs[-2:]
        i = pl.program_id(0)
        d = None
        for p, bt in enumerate(flags):
            dims = (((1,), (1,)), ((), ())) if bt else (((1,), (0,)), ((), ()))
            part = lax.dot_general(refs[2 * p][...].astype(CDT), refs[2 * p + 1][...], dims, preferred_element_type=F32)
            d = part if d is None else d + part
        xv = x_ref[...]
        r = lax.rsqrt(jnp.mean(xv * xv, axis=-1, keepdims=True) + NORM_EPS)
        xh = xv * r
        dxh = d * g_ref[...]
        c = jnp.mean(dxh * xh, axis=-1, keepdims=True)
        gin_ref[...] = gres_ref[...] + r * (dxh - xh * c)
        part = jnp.sum(d * xh, axis=0, keepdims=True)

        @pl.when(i == 0)
        def _():
            dg_ref[...] = part

        @pl.when(i > 0)
        def _():
            dg_ref[...] += part

    in_specs, args = [], []
    for a, b, bt in pairs:
        K = a.shape[1]
        b, bspec = _wspec(b, tuple(_wshape(b)), lambda i: (0, 0))
        in_specs += [pl.BlockSpec((tm, K), lambda i: (i, 0)), bspec]
        args += [a, b]
    row = pl.BlockSpec((tm, D), lambda i: (i, 0))
    vec = pl.BlockSpec((1, D), lambda i: (0, 0))
    dspec, darg = _dep(dep)
    return pl.pallas_call(
        body, grid=(T // tm,), in_specs=in_specs + [row, vec, row] + dspec, out_specs=[row, vec],
        out_shape=[_sds((T, D), F32), _sds((1, D), F32)], name=name,
        compiler_params=_cp(("arbitrary",)))(*args, x, g, gres, *darg)


def _mm_tn(a, b, *, name, alpha=1.0, tk=2048, dep=None):
    T, M = a.shape
    N = b.shape[1]
    tm = M if M <= 1024 else M // 2
    tn = N if N <= 1536 else N // 2
    assert M % tm == 0 and N % tn == 0 and tm % LANES == 0 and tn % LANES == 0
    tk = _tile(T, tk)
    nk = T // tk

    def body(a_ref, b_ref, *rest):
        o_ref = rest[-1]
        k = pl.program_id(2)
        d = lax.dot_general(a_ref[...].astype(CDT), b_ref[...].astype(CDT), (((0,), (0,)), ((), ())),
                            preferred_element_type=F32)

        @pl.when(k == 0)
        def _():
            o_ref[...] = d

        @pl.when(k > 0)
        def _():
            o_ref[...] += d

        if alpha != 1.0:
            @pl.when(k == nk - 1)
            def _():
                o_ref[...] *= alpha

    dspec, darg = _dep(dep)
    return pl.pallas_call(
        body, grid=(M // tm, N // tn, nk),
        in_specs=[pl.BlockSpec((tk, tm), lambda i, j, k: (k, i)), pl.BlockSpec((tk, tn), lambda i, j, k: (k, j))] + dspec,
        out_specs=pl.BlockSpec((tm, tn), lambda i, j, k: (i, j)),
        out_shape=_sds((M, N), F32), name=name, compiler_params=_cp(("parallel", "parallel", "arbitrary")))(a, b, *darg)


def _ffn_up(h, wgt, wut, name):
    T, D = h.shape
    Fh = _wshape(wgt)[0]
    tm = _tile(T, 2048)
    tn = _tile(Fh, 256)
    nt = (((1,), (1,)), ((), ()))

    def body(h_ref, wg_ref, wu_ref, a_ref, b_ref, s_ref):
        hv = h_ref[...]
        a = lax.dot_general(hv, wg_ref[...], nt, preferred_element_type=F32)
        b = lax.dot_general(hv, wu_ref[...], nt, preferred_element_type=F32)
        a_ref[...] = a.astype(a_ref.dtype)
        b_ref[...] = b.astype(b_ref.dtype)
        s_ref[...] = (a * _sigmoid(a) * b).astype(s_ref.dtype)

    wgt, gspec = _wspec(wgt, (tn, D), lambda i, j: (j, 0))
    wut, uspec = _wspec(wut, (tn, D), lambda i, j: (j, 0))
    ospec = pl.BlockSpec((tm, tn), lambda i, j: (i, j))
    return pl.pallas_call(
        body, grid=(T // tm, Fh // tn),
        in_specs=[pl.BlockSpec((tm, D), lambda i, j: (i, 0)), gspec, uspec],
        out_specs=[ospec, ospec, ospec],
        out_shape=[_sds((T, Fh), CDT), _sds((T, Fh), CDT), _sds((T, Fh), CDT)],
        name=name, compiler_params=_cp(("parallel", "arbitrary")))(h, wgt, wut)


def _ffn_bwd_ds(gout, wd, a, b, name, dep=None):
    T, D = gout.shape
    Fh = _wshape(wd)[0]
    tm = _tile(T, 2048)
    tn = _tile(Fh, 256)

    def body(g_ref, wd_ref, a_ref, b_ref, *rest):
        da_ref, db_ref = rest[-2:]
        dy = (0.5 * g_ref[...]).astype(CDT)
        ds = lax.dot_general(dy, wd_ref[...], (((1,), (1,)), ((), ())), preferred_element_type=F32)
        av = a_ref[...].astype(F32)
        sg = _sigmoid(av)
        da_ref[...] = (ds * b_ref[...].astype(F32) * (sg * (1.0 + av * (1.0 - sg)))).astype(da_ref.dtype)
        db_ref[...] = (ds * (av * sg)).astype(db_ref.dtype)

    ospec = pl.BlockSpec((tm, tn), lambda i, j: (i, j))
    dspec, darg = _dep(dep)
    wd, wspec = _wspec(wd, (tn, D), lambda i, j: (j, 0))
    return pl.pallas_call(
        body, grid=(T // tm, Fh // tn),
        in_specs=[pl.BlockSpec((tm, D), lambda i, j: (i, 0)), wspec, ospec, ospec] + dspec,
        out_specs=[ospec, ospec],
        out_shape=[_sds((T, Fh), CDT), _sds((T, Fh), CDT)],
        name=name, compiler_params=_cp(("parallel", "arbitrary")))(gout, wd, a, b, *darg)


def _ffn_fwd(x, gamma, wgt, wut, wd, tag, dep=None):
    h = _rms_fwd(x, gamma, f"{tag}_norm", dep)
    a, b, s = _ffn_up(h, wgt, wut, f"{tag}_up")
    y = _mm([(s, wd, False)], res=x, alpha=0.5, name=f"{tag}_down")
    return y, (x, h, a, b, s)


def _ffn_bwd(saved, gamma, wgt, wut, wd, gout, tag, dep, on_grads):
    x, h, a, b, s = saved
    dwd = _mm_tn(s, gout, alpha=0.5, name=f"{tag}_dwd", dep=dep)
    da, db = _ffn_bwd_ds(gout, wd, a, b, f"{tag}_bwd_ds", on_grads(dict(w_down=dwd)))
    dwgt = _mm_tn(da, h, name=f"{tag}_dwg")
    dwut = _mm_tn(db, h, name=f"{tag}_dwu")
    dep = on_grads(dict(w_gate=dwgt, w_up=dwut))
    return _mm_norm_bwd([(da, wgt, False), (db, wut, False)], x, gamma, gout, name=f"{tag}_dh_norm_bwd", dep=dep)


def _fgate_fwd(rest, bias, name, bt=512):
    T = rest.shape[0]
    bt = _tile(T, bt)

    def body(z_ref, b_ref, fc_ref, ft_ref, carry):
        i = pl.program_id(0)

        @pl.when(i == 0)
        def _():
            carry[...] = jnp.zeros_like(carry)

        zb = z_ref[...] + b_ref[...]
        e = jnp.exp(-jnp.abs(zb))
        u = 1.0 + e
        log1p_e = jnp.where(u == 1.0, e, jnp.log(u) * (e / (u - 1.0)))
        x = jnp.minimum(zb, 0.0) - log1p_e
        row = lax.broadcasted_iota(jnp.int32, x.shape, 0)
        sh = 1
        while sh < bt:
            x = x + jnp.where(row >= sh, pltpu.roll(x, sh, 0), 0.0)
            sh *= 2
        f = x + carry[...]
        carry[...] = f[bt - 1:bt, :]
        fc_ref[...] = f
        ft_ref[...] = jnp.transpose(f)[0:N_HEADS, :]

    return pl.pallas_call(
        body, grid=(T // bt,),
        in_specs=[pl.BlockSpec((bt, LANES), lambda i: (i, REST_Z_BLK)), pl.BlockSpec((1, LANES), lambda i: (0, 0))],
        out_specs=[pl.BlockSpec((bt, LANES), lambda i: (i, 0)), pl.BlockSpec((N_HEADS, bt), lambda i: (0, i))],
        out_shape=[_sds((T, LANES), F32), _sds((N_HEADS, T), F32)],
        scratch_shapes=[pltpu.VMEM((1, LANES), F32)],
        name=name, compiler_params=_cp(("arbitrary",)))(rest, bias)


def _fgate_bwd(dfk, rest, bias, name, bt=512):
    T = rest.shape[0]
    bt = _tile(T, bt)
    nb = T // bt

    def body(df_ref, z_ref, b_ref, dz_ref, db_ref, carry):
        i = pl.program_id(0)

        @pl.when(i == 0)
        def _():
            carry[...] = jnp.zeros_like(carry)

        dfv = df_ref[...]
        lane = lax.broadcasted_iota(jnp.int32, (bt, LANES), 1)
        x = jnp.zeros((bt, LANES), F32)
        for h in range(N_HEADS):
            x = jnp.where(lane == h, dfv[:, HEAD_DIM * h:HEAD_DIM * h + 1], x)
        row = lax.broadcasted_iota(jnp.int32, x.shape, 0)
        sh = 1
        while sh < bt:
            x = x + jnp.where(row + sh < bt, pltpu.roll(x, bt - sh, 0), 0.0)
            sh *= 2
        dlf = x + carry[...]
        carry[...] = dlf[0:1, :]
        zb = z_ref[...] + b_ref[...]
        dz = jnp.where(lane < N_HEADS, dlf * _sigmoid(-zb), 0.0)
        dz_ref[...] = dz.astype(dz_ref.dtype)
        part = jnp.sum(dz, axis=0, keepdims=True)

        @pl.when(i == 0)
        def _():
            db_ref[...] = part

        @pl.when(i > 0)
        def _():
            db_ref[...] += part

    return pl.pallas_call(
        body, grid=(nb,),
        in_specs=[pl.BlockSpec((bt, 4 * LANES), lambda i: (nb - 1 - i, 0)),
                  pl.BlockSpec((bt, LANES), lambda i: (nb - 1 - i, REST_Z_BLK)),
                  pl.BlockSpec((1, LANES), lambda i: (0, 0))],
        out_specs=[pl.BlockSpec((bt, LANES), lambda i: (nb - 1 - i, 0)), pl.BlockSpec((1, LANES), lambda i: (0, 0))],
        out_shape=[_sds((T, LANES), CDT), _sds((1, LANES), F32)],
        scratch_shapes=[pltpu.VMEM((1, LANES), F32)],
        name=name, compiler_params=_cp(("arbitrary",)))(dfk, rest, bias)


def _by_group(vals, lane):
    out = vals[-1]
    for g in range(len(vals) - 2, -1, -1):
        out = jnp.where(lane // 64 == g, vals[g], out)
    return out


def _pool_counts(t0, n, lane):
    t = t0 + lax.broadcasted_iota(jnp.int32, (n, 256), 0)
    return _by_group([jnp.minimum(t + 1, w) for w in POOL_WINDOWS], lane).astype(F32)


def _pooled(u, halo, i, bt):
    lane = lax.broadcasted_iota(jnp.int32, (bt, 256), 1)
    ext = jnp.concatenate([jnp.where(i > 0, halo, 0.0), u], axis=0)
    sums, s, sh = [], ext, 1
    for _ in POOL_WINDOWS:
        s = s + pltpu.roll(s, sh, 0)
        sums.append(s[POOL_HALO:, :])
        sh *= 2
    return _by_group(sums, lane) / _pool_counts(i * bt, bt, lane) - u


def _pool_fwd(rest, wbd, scale, name, bt=512):
    T = rest.shape[0]
    bt = _tile(T, bt)
    hb = bt // POOL_HALO

    def body(u_ref, halo_ref, w_ref, sc_ref, o_ref):
        i = pl.program_id(0)
        pooled = _pooled(u_ref[...], halo_ref[...], i, bt)
        mixed = jnp.dot(pooled.astype(CDT), w_ref[...], preferred_element_type=F32)
        o_ref[...] = (mixed * sc_ref[...]).astype(o_ref.dtype)

    return pl.pallas_call(
        body, grid=(T // bt,),
        in_specs=[pl.BlockSpec((bt, 256), lambda i: (i, 0)),
                  pl.BlockSpec((POOL_HALO, 256), lambda i: (jnp.maximum(i * hb - 1, 0), 0)),
                  pl.BlockSpec((256, 256), lambda i: (0, 0)), pl.BlockSpec((1, 256), lambda i: (0, 0))],
        out_specs=pl.BlockSpec((bt, 256), lambda i: (i, 0)),
        out_shape=_sds((T, 256), CDT), name=name, compiler_params=_cp(("parallel",)))(rest, rest, wbd, scale)


def _pool_bwd(dcat, rest, wbd, scale, name, bt=512):
    T = rest.shape[0]
    bt = _tile(T, bt)
    hb = bt // POOL_HALO
    nb = T // bt
    n = bt + POOL_HALO

    def body(dy_ref, dyn_ref, u_ref, halo_ref, w_ref, sc_ref, du_ref, dw_ref, dsc_ref):
        i = pl.program_id(0)
        lane = lax.broadcasted_iota(jnp.int32, (bt, 256), 1)
        w = w_ref[...]
        sc = sc_ref[...]
        pooled = _pooled(u_ref[...], halo_ref[...], i, bt)
        pooled_c = pooled.astype(CDT)
        mixed = jnp.dot(pooled_c, w, preferred_element_type=F32)
        dy = dy_ref[...]
        dm = (dy * sc).astype(CDT)
        dsc = jnp.sum(dy * mixed, axis=0, keepdims=True)
        dw = lax.dot_general(pooled_c, dm, (((0,), (0,)), ((), ())), preferred_element_type=F32)
        nt = (((1,), (1,)), ((), ()))
        dpl = lax.dot_general(dm, w, nt, preferred_element_type=F32)
        dmn = (jnp.where(i < nb - 1, dyn_ref[...], 0.0) * sc).astype(CDT)
        dpln = lax.dot_general(dmn, w, nt, preferred_element_type=F32)
        lane_h = lax.broadcasted_iota(jnp.int32, (POOL_HALO, 256), 1)
        ext = jnp.concatenate([dpl / _pool_counts(i * bt, bt, lane),
                               dpln / _pool_counts((i + 1) * bt, POOL_HALO, lane_h)], axis=0)
        sums, s, sh = [], ext, 1
        for _ in POOL_WINDOWS:
            s = s + pltpu.roll(s, n - sh, 0)
            sums.append(s[0:bt, :])
            sh *= 2
        du_ref[...] = (_by_group(sums, lane) - dpl).astype(du_ref.dtype)

        @pl.when(i == 0)
        def _():
            dw_ref[...] = dw
            dsc_ref[...] = dsc

        @pl.when(i > 0)
        def _():
            dw_ref[...] += dw
            dsc_ref[...] += dsc

    full = pl.BlockSpec((256, 256), lambda i: (0, 0))
    vec = pl.BlockSpec((1, 256), lambda i: (0, 0))
    return pl.pallas_call(
        body, grid=(nb,),
        in_specs=[pl.BlockSpec((bt, 256), lambda i: (i, 0)),
                  pl.BlockSpec((POOL_HALO, 256), lambda i: (jnp.minimum((i + 1) * hb, nb * hb - 1), 0)),
                  pl.BlockSpec((bt, 256), lambda i: (i, 0)),
                  pl.BlockSpec((POOL_HALO, 256), lambda i: (jnp.maximum(i * hb - 1, 0), 0)),
                  full, vec],
        out_specs=[pl.BlockSpec((bt, 256), lambda i: (i, 0)), full, vec],
        out_shape=[_sds((T, 256), CDT), _sds((256, 256), F32), _sds((1, 256), F32)],
        name=name, compiler_params=_cp(("arbitrary",)))(dcat, dcat, rest, rest, wbd, scale)


def _glu_ext(a_ref, g_ref, ah_ref, gh_ref, i):
    u = a_ref[...] * _sigmoid(g_ref[...])
    uh = jnp.where(i > 0, ah_ref[...] * _sigmoid(gh_ref[...]), 0.0)
    return jnp.concatenate([uh, u], axis=0)


def _conv_fwd(rest, cw, cb, lg, lb, name, bt=512):
    T = rest.shape[0]
    bt = _tile(T, bt)
    hb = bt // CONV_HALO

    def body(a_ref, g_ref, ah_ref, gh_ref, cw_ref, cb_ref, lg_ref, lb_ref, o_ref, y_ref):
        i = pl.program_id(0)
        ext = _glu_ext(a_ref, g_ref, ah_ref, gh_ref, i)
        w = cw_ref[...]
        acc = w[CONV_K - 1:CONV_K, :] * ext
        for k in range(CONV_K - 1):
            acc = acc + w[k:k + 1, :] * pltpu.roll(ext, CONV_K - 1 - k, 0)
        y = acc[CONV_HALO:, :] + cb_ref[...]
        y_ref[...] = y
        yc = y - jnp.mean(y, axis=-1, keepdims=True)
        yn = yc * lax.rsqrt(jnp.mean(yc * yc, axis=-1, keepdims=True) + NORM_EPS)
        z = yn * lg_ref[...] + lb_ref[...]
        o_ref[...] = (z * _sigmoid(z)).astype(o_ref.dtype)

    def cur(c):
        return pl.BlockSpec((bt, 256), lambda i: (i, c))

    def prev(c):
        return pl.BlockSpec((CONV_HALO, 256), lambda i: (jnp.maximum(i * hb - 1, 0), c))

    vec = pl.BlockSpec((1, 256), lambda i: (0, 0))
    return pl.pallas_call(
        body, grid=(T // bt,),
        in_specs=[cur(1), cur(2), prev(1), prev(2), pl.BlockSpec((CONV_HALO, 256), lambda i: (0, 0)), vec, vec, vec],
        out_specs=[pl.BlockSpec((bt, 256), lambda i: (i, 0)), pl.BlockSpec((bt, 256), lambda i: (i, 0))],
        out_shape=[_sds((T, 256), CDT), _sds((T, 256), F32)],
        name=name, compiler_params=_cp(("parallel",)))(rest, rest, rest, rest, cw, cb, lg, lb)


def _conv_bwd(dcat, yconv, rest, cw, lg, lb, name, bt=512):
    T = rest.shape[0]
    bt = _tile(T, bt)
    hb = bt // CONV_HALO
    nb = T // bt
    n = bt + CONV_HALO

    def body(dy_ref, dyn_ref, y_ref, yn_ref, a_ref, g_ref, ah_ref, gh_ref, cw_ref, lg_ref, lb_ref,
             da_ref, dg_ref, dcw_ref, dcb_ref, dlg_ref, dlb_ref):
        i = pl.program_id(0)
        lgv = lg_ref[...]
        lbv = lb_ref[...]

        def ln_swish_bwd(dout, y):
            yc = y - jnp.mean(y, axis=-1, keepdims=True)
            rs = lax.rsqrt(jnp.mean(yc * yc, axis=-1, keepdims=True) + NORM_EPS)
            yn = yc * rs
            z = yn * lgv + lbv
            sg = _sigmoid(z)
            dz = dout * (sg * (1.0 + z * (1.0 - sg)))
            dyn = dz * lgv
            dyc = rs * (dyn - jnp.mean(dyn, axis=-1, keepdims=True) - yn * jnp.mean(dyn * yn, axis=-1, keepdims=True))
            return dyc, dz, yn

        dyc, dz, yn = ln_swish_bwd(dy_ref[...], y_ref[...])
        dyc_next, _, _ = ln_swish_bwd(dyn_ref[...], yn_ref[...])
        dyc_next = jnp.where(i < nb - 1, dyc_next, 0.0)
        ext_u = _glu_ext(a_ref, g_ref, ah_ref, gh_ref, i)
        ext_d = jnp.concatenate([dyc, dyc_next], axis=0)
        w = cw_ref[...]
        du = w[CONV_K - 1:CONV_K, :] * ext_d
        rows = []
        for k in range(CONV_K):
            s = CONV_K - 1 - k
            if s > 0:
                du = du + w[k:k + 1, :] * pltpu.roll(ext_d, n - s, 0)
                us = pltpu.roll(ext_u, s, 0)[CONV_HALO:, :]
            else:
                us = ext_u[CONV_HALO:, :]
            rows.append(jnp.sum(dyc * us, axis=0, keepdims=True))
        rows.append(jnp.zeros((1, 256), F32))
        dcw = jnp.concatenate(rows, axis=0)
        du = du[0:bt, :]
        av = a_ref[...]
        sg = _sigmoid(g_ref[...])
        da_ref[...] = (du * sg).astype(da_ref.dtype)
        dg_ref[...] = (du * av * (sg * (1.0 - sg))).astype(dg_ref.dtype)
        dcb = jnp.sum(dyc, axis=0, keepdims=True)
        dlg = jnp.sum(dz * yn, axis=0, keepdims=True)
        dlb = jnp.sum(dz, axis=0, keepdims=True)

        @pl.when(i == 0)
        def _():
            dcw_ref[...] = dcw
            dcb_ref[...] = dcb
            dlg_ref[...] = dlg
            dlb_ref[...] = dlb

        @pl.when(i > 0)
        def _():
            dcw_ref[...] += dcw
            dcb_ref[...] += dcb
            dlg_ref[...] += dlg
            dlb_ref[...] += dlb

    def cur(c):
        return pl.BlockSpec((bt, 256), lambda i: (i, c))

    def prev(c):
        return pl.BlockSpec((CONV_HALO, 256), lambda i: (jnp.maximum(i * hb - 1, 0), c))

    def nxt(c):
        return pl.BlockSpec((CONV_HALO, 256), lambda i: (jnp.minimum((i + 1) * hb, nb * hb - 1), c))

    vec = pl.BlockSpec((1, 256), lambda i: (0, 0))
    wfull = pl.BlockSpec((CONV_HALO, 256), lambda i: (0, 0))
    return pl.pallas_call(
        body, grid=(nb,),
        in_specs=[cur(3), nxt(3), cur(0), nxt(0), cur(1), cur(2), prev(1), prev(2), wfull, vec, vec],
        out_specs=[cur(0), cur(0), wfull, vec, vec, vec],
        out_shape=[_sds((T, 256), CDT), _sds((T, 256), CDT), _sds((CONV_HALO, 256), F32),
                   _sds((1, 256), F32), _sds((1, 256), F32), _sds((1, 256), F32)],
        name=name, compiler_params=_cp(("arbitrary",)))(dcat, dcat, yconv, yconv, rest, rest, rest, rest, cw, lg, lb)


def _half_mask(shape, a):
    lane = lax.broadcasted_iota(jnp.int32, shape, 1)
    return (lane // HEAD_DIM) == a


def _attn_fwd(qkv, fcol, frow, name, blk=1024):
    T = qkv.shape[0]
    blk = _tile(T, blk)
    nq = T // blk
    nt = (((1,), (1,)), ((), ()))

    def body(q_ref, k_ref, v_ref, fc_ref, fr_ref, o_ref, lse_ref):
        p_id = pl.program_id(0)
        i = pl.program_id(1)
        q2 = q_ref[...]
        fc = fc_ref[...]
        lane = lax.broadcasted_iota(jnp.int32, (blk, LANES), 1)
        tri = lax.broadcasted_iota(jnp.int32, (blk, blk), 1) <= lax.broadcasted_iota(jnp.int32, (blk, blk), 0)
        masks = [_half_mask(q2.shape, a) for a in range(2)]
        qs = [jnp.where(hm, q2, jnp.zeros_like(q2)) * ATT_SCALE for hm in masks]
        fqs = [jnp.sum(jnp.where(lane == 2 * p_id + a, fc, 0.0), axis=1, keepdims=True) for a in range(2)]

        def tile(j, carry, masked):
            cols = pl.ds(pl.multiple_of(j * blk, blk), blk)
            kj = k_ref[cols, :]
            vj = v_ref[cols, :]
            out = []
            for a in range(2):
                m, acc = carry[2 * a:2 * a + 2]
                va = jnp.where(masks[a], vj, jnp.ones_like(vj))
                s = lax.dot_general(qs[a], kj, nt, preferred_element_type=F32) + (fqs[a] - fr_ref[a:a + 1, cols])
                if masked:
                    s = jnp.where(tri, s, NEG)
                m_new = jnp.maximum(m, jnp.max(s, axis=1, keepdims=True))
                alpha = jnp.exp(m - m_new)
                pr = jnp.exp(s - m_new)
                hi = lax.bitcast_convert_type(lax.bitcast_convert_type(pr, jnp.uint32) & jnp.uint32(0xFFFF0000), F32)
                pv = (jnp.dot(hi.astype(CDT), va, preferred_element_type=F32)
                      + jnp.dot((pr - hi).astype(CDT), va, preferred_element_type=F32))
                out += [m_new, alpha * acc + pv]
            return tuple(out)

        init = (jnp.full((blk, 1), NEG, F32), jnp.zeros((blk, LANES), F32)) * 2
        carry = lax.fori_loop(0, i, lambda j, c: tile(j, c, False), init)
        carry = tile(i, carry, True)
        ls = [carry[1][:, HEAD_DIM:HEAD_DIM + 1], carry[3][:, 0:1]]
        lo = lane < HEAD_DIM
        o_ref[...] = jnp.where(lo, carry[1] / ls[0], carry[3] / ls[1])
        lse_t = jnp.transpose(jnp.where(lo, carry[0] + jnp.log(ls[0]), carry[2] + jnp.log(ls[1])))
        lse_ref[...] = jnp.concatenate([lse_t[0:1, :], lse_t[HEAD_DIM:HEAD_DIM + 1, :]], axis=0)

    return pl.pallas_call(
        body, grid=(N_PAIRS, nq),
        in_specs=[pl.BlockSpec((blk, LANES), lambda p, i: (i, p)),
                  pl.BlockSpec((T, LANES), lambda p, i: (0, N_PAIRS + p)),
                  pl.BlockSpec((T, LANES), lambda p, i: (0, 2 * N_PAIRS + p)),
                  pl.BlockSpec((blk, LANES), lambda p, i: (i, 0)),
                  pl.BlockSpec((None, 2, T), lambda p, i: (p, 0, 0))],
        out_specs=[pl.BlockSpec((blk, LANES), lambda p, i: (i, p)), pl.BlockSpec((None, 2, blk), lambda p, i: (p, 0, i))],
        out_shape=[_sds((T, N_PAIRS * LANES), F32), _sds((N_PAIRS, 2, T), F32)],
        name=name, compiler_params=_cp(("parallel", "arbitrary")))(qkv, qkv, qkv, fcol, frow)


def _attn_delta(dcat, o, name, blk=512):
    T = o.shape[0]
    blk = _tile(T, blk)

    def body(d_ref, o_ref, out_ref):
        prod = d_ref[:, 256:768].astype(CDT).astype(F32) * o_ref[...]
        pt = jnp.transpose(prod)
        out_ref[...] = jnp.sum(pt.reshape(N_HEADS, HEAD_DIM, blk), axis=1)

    return pl.pallas_call(
        body, grid=(T // blk,),
        in_specs=[pl.BlockSpec((blk, 1024), lambda i: (i, 0)), pl.BlockSpec((blk, 512), lambda i: (i, 0))],
        out_specs=pl.BlockSpec((N_HEADS, blk), lambda i: (0, i)),
        out_shape=_sds((N_HEADS, T), F32), name=name, compiler_params=_cp(("parallel",)))(dcat, o)


def _attn_bwd(qkv, dcat, fcol, frow, lse, delta, name, blk=1024):
    T = qkv.shape[0]
    blk = _tile(T, blk)
    nq = T // blk
    nt = (((1,), (1,)), ((), ()))

    def body(q_ref, do_ref, k_ref, v_ref, fc_ref, fr_ref, lse_ref, dl_ref, dqt_ref, dk_ref, dv_ref, df_ref):
        p_id = pl.program_id(0)
        j = pl.program_id(1)

        @pl.when(j == 0)
        def _():
            dqt_ref[...] = jnp.zeros_like(dqt_ref)

        k2 = k_ref[...]
        v2 = v_ref[...]
        fc = fc_ref[...]
        lane = lax.broadcasted_iota(jnp.int32, (blk, LANES), 1)
        tri = lax.broadcasted_iota(jnp.int32, (blk, blk), 0) <= lax.broadcasted_iota(jnp.int32, (blk, blk), 1)
        masks = [_half_mask(k2.shape, a) for a in range(2)]
        kas = [jnp.where(hm, k2, jnp.zeros_like(k2)) * ATT_SCALE for hm in masks]
        kats = [jnp.transpose(ka) for ka in kas]
        vas = [jnp.where(hm, v2, jnp.zeros_like(v2)) for hm in masks]
        fks = [jnp.sum(jnp.where(lane == 2 * p_id + a, fc, 0.0), axis=1, keepdims=True) for a in range(2)]

        def tile(i, carry, masked):
            rows = pl.ds(pl.multiple_of(i * blk, blk), blk)
            qi = q_ref[rows, :]
            doi = do_ref[rows, :].astype(CDT)
            out = []
            dqt = None
            for a in range(2):
                dk_acc, dv_acc, df_acc = carry[3 * a:3 * a + 3]
                st = lax.dot_general(kas[a], qi, nt, preferred_element_type=F32)
                e = (st + (fr_ref[a:a + 1, rows] - fks[a])) - lse_ref[a:a + 1, rows]
                if masked:
                    e = jnp.where(tri, e, NEG)
                pt = jnp.exp(e)
                dpt = lax.dot_general(vas[a], doi, nt, preferred_element_type=F32)
                ds32 = pt * (dpt - dl_ref[a:a + 1, rows])
                dst = ds32.astype(CDT)
                df_acc = df_acc + jnp.sum(ds32, axis=1, keepdims=True)
                dv_acc = dv_acc + jnp.dot(pt.astype(CDT), doi, preferred_element_type=F32)
                dk_acc = dk_acc + jnp.dot(dst, qi, preferred_element_type=F32)
                part = jnp.dot(kats[a], dst, preferred_element_type=F32)
                dqt = part if dqt is None else dqt + part
                out += [dk_acc, dv_acc, df_acc]
            dqt_ref[:, rows] += dqt
            return tuple(out)

        init = (jnp.zeros((blk, LANES), F32), jnp.zeros((blk, LANES), F32), jnp.zeros((blk, 1), F32)) * 2
        carry = tile(j, init, True)
        carry = lax.fori_loop(j + 1, nq, lambda i, c: tile(i, c, False), carry)
        lo = lane < HEAD_DIM
        dk_ref[...] = (jnp.where(lo, carry[0], carry[3]) * ATT_SCALE).astype(dk_ref.dtype)
        dv_ref[...] = jnp.where(lo, carry[1], carry[4]).astype(dv_ref.dtype)
        df_ref[...] = -jnp.where(lo, carry[2], carry[5])

    res = pl.BlockSpec((T, LANES), lambda p, j: (0, p))
    rows = pl.BlockSpec((None, 2, T), lambda p, j: (p, 0, 0))
    kv_out = pl.BlockSpec((blk, LANES), lambda p, j: (j, p))
    return pl.pallas_call(
        body, grid=(N_PAIRS, nq),
        in_specs=[res, pl.BlockSpec((T, LANES), lambda p, j: (0, 2 + p)),
                  pl.BlockSpec((blk, LANES), lambda p, j: (j, N_PAIRS + p)),
                  pl.BlockSpec((blk, LANES), lambda p, j: (j, 2 * N_PAIRS + p)),
                  pl.BlockSpec((blk, LANES), lambda p, j: (j, 0)), rows, rows, rows],
        out_specs=[pl.BlockSpec((LANES, T), lambda p, j: (p, 0)), kv_out, kv_out, kv_out],
        out_shape=[_sds((N_PAIRS * LANES, T), F32), _sds((T, N_PAIRS * LANES), CDT), _sds((T, N_PAIRS * LANES), CDT),
                   _sds((T, N_PAIRS * LANES), F32)],
        name=name, compiler_params=_cp(("parallel", "arbitrary")))(qkv, dcat, qkv, qkv, fcol, frow, lse, delta)


def _mixer_fwd(x, wts, tag, dep=None):
    T = x.shape[0]
    h = _rms_fwd(x, wts["mix_norm"], f"{tag}_norm", dep)
    qkv = _mm([(h, wts["win_qkv"], False)], out_dtype=CDT, tm=1024, tn=768, name=f"{tag}_in_qkv")
    rest = _mm([(h, wts["win_rest"], False)], tm=1024, name=f"{tag}_in_rest")
    fcol, frow8 = _fgate_fwd(rest, wts["fbias"], f"{tag}_fgate")
    frow = frow8.reshape(N_PAIRS, 2, T)
    ya = _pool_fwd(rest, wts["pool_wbd"], wts["pool_scale"], f"{tag}_pool")
    o, lse = _attn_fwd(qkv, fcol, frow, f"{tag}_attn")
    yc, yconv = _conv_fwd(rest, wts["conv_w"], wts["conv_b"], wts["conv_ln_g"], wts["conv_ln_b"], f"{tag}_conv")
    cat = jnp.concatenate([ya, o.astype(CDT), yc], axis=1)
    y = _mm([(cat, wts["w_out"], False)], res=x, name=f"{tag}_out")
    return y, (x, h, qkv, rest, fcol, frow, o, lse, yconv, cat)


def _mixer_bwd(saved, wts, gout, tag, dep=None):
    x, h, qkv, rest, fcol, frow, o, lse, yconv, cat = saved
    T = x.shape[0]
    dcat = _mm([(gout, wts["w_out"], True)], name=f"{tag}_dcat", dep=dep)
    dwout = _mm_tn(cat, gout, name=f"{tag}_dwout")
    du, dpw, dpsc = _pool_bwd(dcat, rest, wts["pool_wbd"], wts["pool_scale"], f"{tag}_pool_bwd")
    delta = _attn_delta(dcat, o, f"{tag}_attn_delta").reshape(N_PAIRS, 2, T)
    dqt, dk, dv, dfk = _attn_bwd(qkv, dcat, fcol, frow, lse, delta, f"{tag}_attn_bwd")
    dq = dqt.T.astype(CDT)
    dz, dfb = _fgate_bwd(dfk, rest, wts["fbias"], f"{tag}_fgate_bwd")
    da, dg, dcw, dcb, dlg, dlb = _conv_bwd(dcat, yconv, rest, wts["conv_w"], wts["conv_ln_g"], wts["conv_ln_b"],
                                           f"{tag}_conv_bwd")
    dp_qkv = jnp.concatenate([dq, dk, dv], axis=1).astype(CDT)
    dp_rest = jnp.concatenate([du, da, dg, dz], axis=1)
    dwin_qkv = _mm_tn(h, dp_qkv, name=f"{tag}_dwin_qkv")
    dwin_rest = _mm_tn(h, dp_rest, name=f"{tag}_dwin_rest")
    gin, dgamma = _mm_norm_bwd([(dp_qkv, wts["win_qkv"], True), (dp_rest, wts["win_rest"], True)], x, wts["mix_norm"],
                               gout, name=f"{tag}_dh_norm_bwd")
    dwin = _split_win(dwin_qkv, dwin_rest, f"{tag}_dwin_split")
    dpool_w = jnp.stack([dpw[64 * g:64 * g + 64, 64 * g:64 * g + 64] for g in range(4)])
    grads = dict(mix_norm=dgamma[0], w_in=dwin, pool_w=dpool_w, pool_scale=dpsc[0], forget_bias=dfb[0, 0:N_HEADS],
                 conv_w=dcw[0:CONV_K], conv_b=dcb[0], conv_ln_g=dlg[0], conv_ln_b=dlb[0], w_out=dwout)
    return gin, grads


def _rep_layer(rep, l):
    pw = rep["pool_w"][l].astype(CDT)
    wbd = jnp.zeros((256, 256), CDT)
    for g in range(4):
        wbd = lax.dynamic_update_slice(wbd, pw[g], (64 * g, 64 * g))
    return dict(
        ffn1_norm=rep["ffn1_norm"][l][None], ffn2_norm=rep["ffn2_norm"][l][None], mix_norm=rep["mix_norm"][l][None],
        fbias=jnp.pad(rep["forget_bias"][l], (0, LANES - N_HEADS))[None],
        pool_wbd=wbd, pool_scale=rep["pool_scale"][l][None], conv_b=rep["conv_b"][l][None],
        conv_ln_g=rep["conv_ln_g"][l][None], conv_ln_b=rep["conv_ln_b"][l][None])


def _local_step(x, target, rep, weights_for, grads_ready):
    depth = rep["ffn1_norm"].shape[0]
    kept = []
    for l in range(depth):
        r = _rep_layer(rep, l)
        w1, dep = weights_for(l, "ffn1", x)
        x, s1 = _ffn_fwd(x, r["ffn1_norm"], w1["w_gate"], w1["w_up"], w1["w_down"], f"l{l}_ffn1", dep)
        wm, dep = weights_for(l, "mix", x)
        wm = dict(r, win_qkv=wm["win_qkv"], win_rest=wm["win_rest"], w_out=wm["w_out"],
                  conv_w=jnp.pad(wm["conv_w"], ((0, CONV_HALO - CONV_K), (0, 0))))
        x, s2 = _mixer_fwd(x, wm, f"l{l}_mix", dep)
        w2, dep = weights_for(l, "ffn2", x)
        x, s3 = _ffn_fwd(x, r["ffn2_norm"], w2["w_gate"], w2["w_up"], w2["w_down"], f"l{l}_ffn2", dep)
        kept.append((r, w1, wm, w2, s1, s2, s3))
    loss, g, dfinal = _loss_bwd(x, rep["final_norm"][None], target, "loss_head")
    dep = grads_ready(None, "final", dict(final_norm=dfinal[0]))
    for l in reversed(range(depth)):
        r, w1, wm, w2, s1, s2, s3 = kept[l]

        def ffn_grads(which, l=l):
            return lambda gr: grads_ready(l, which, {f"{which}_{k}": v for k, v in gr.items()})

        g, dn = _ffn_bwd(s3, r["ffn2_norm"], w2["w_gate"], w2["w_up"], w2["w_down"], g, f"l{l}_ffn2", dep, ffn_grads("ffn2"))
        grads_ready(l, "norm", dict(ffn2_norm=dn[0]))
        g, gm = _mixer_bwd(s2, wm, g, f"l{l}_mix")
        dep = grads_ready(l, "mix", gm)
        g, dn = _ffn_bwd(s1, r["ffn1_norm"], w1["w_gate"], w1["w_up"], w1["w_down"], g, f"l{l}_ffn1", dep, ffn_grads("ffn1"))
        dep = grads_ready(l, "norm", dict(ffn1_norm=dn[0]))
    return loss, g


def _mesh_pos():
    return lax.axis_index("x"), lax.axis_index("y"), lax.axis_index("c")


def _dev_block(ref, dev, by_rows):
    if by_rows:
        r = ref.shape[1] // N_DEV
        return ref.at[:, pl.ds(dev * r, r), :]
    return ref.at[dev]


def _all_gather(shards, by_rows, name):
    n_arr = len(shards)
    out_shape = [_sds((s.shape[0], N_DEV * s.shape[1], s.shape[2]) if br else (N_DEV,) + s.shape, s.dtype)
                 for s, br in zip(shards, by_rows)]

    def body(*refs):
        xs, outs = refs[:n_arr], refs[n_arr:2 * n_arr]
        send_sems, recv_sems, local_sems = refs[2 * n_arr:]
        x, y, c = _mesh_pos()
        me, sibling = (x, y, c), (x, y, 1 - c)
        chips = [(1 - x, y), (x, 1 - y), (1 - x, 1 - y)]

        def rows(a, px, py, pc):
            return _dev_block(outs[a], 4 * px + 2 * py + pc, by_rows[a])

        def copy(k, a, block, to, src=None):
            return pltpu.make_async_remote_copy(
                src_ref=rows(a, *block) if src is None else src, dst_ref=rows(a, *block),
                send_sem=send_sems.at[k, a], recv_sem=recv_sems.at[k, a],
                device_id=to, device_id_type=pl.DeviceIdType.MESH)

        arrs = range(n_arr)
        mine = [pltpu.make_async_copy(xs[a], rows(a, *me), local_sems.at[a]) for a in arrs]
        for cp in mine:
            cp.start()
        first = [copy(0, a, me, sibling, src=xs[a]) for a in arrs]
        first += [copy(1 + j, a, me, (*chip, c), src=xs[a]) for j, chip in enumerate(chips) for a in arrs]
        for cp in first:
            cp.start()
        passed = []
        for j, chip in enumerate(chips):
            for a in arrs:
                copy(1 + j, a, (*chip, c), me).wait_recv()
                passed.append(copy(4 + j, a, (*chip, c), sibling))
                passed[-1].start()
        for a in arrs:
            copy(0, a, sibling, me).wait_recv()
        for j, chip in enumerate(chips):
            for a in arrs:
                copy(4 + j, a, (*chip, 1 - c), me).wait_recv()
        for cp in first + passed:
            cp.wait_send()
        for cp in mine:
            cp.wait()

    hbm = pl.BlockSpec(memory_space=pl.ANY)
    return pl.pallas_call(
        body, out_shape=out_shape, in_specs=[hbm] * n_arr, out_specs=[hbm] * n_arr,
        scratch_shapes=[pltpu.SemaphoreType.DMA((7, n_arr)), pltpu.SemaphoreType.DMA((7, n_arr)),
                        pltpu.SemaphoreType.DMA((n_arr,))],
        name=name)(*shards)


def _peer_copies(srcs, lands, send_sems, recv_sems, gather, by_rows):
    n_arr = len(srcs)
    x, y, c = _mesh_pos()
    my = 4 * x + 2 * y + c
    out = []
    for k in range(1, N_DEV):
        px, py, pc = x ^ (k >> 2), y ^ ((k >> 1) & 1), c ^ (k & 1)
        peer = 4 * px + 2 * py + pc
        for a in range(n_arr):
            src = srcs[a] if gather else _dev_block(srcs[a], peer, by_rows[a])
            dst = _dev_block(lands[a], my, by_rows[a]) if gather else lands[a].at[my]
            out.append(pltpu.make_async_remote_copy(
                src_ref=src, dst_ref=dst, send_sem=send_sems.at[(k - 1) * n_arr + a],
                recv_sem=recv_sems.at[(k - 1) * n_arr + a], device_id=(px, py, pc), device_id_type=pl.DeviceIdType.MESH))
    return out


def _land_shape(s, gather, by_rows):
    if gather:
        return (s.shape[0], N_DEV * s.shape[1], s.shape[2]) if by_rows else (N_DEV,) + s.shape
    return (N_DEV, s.shape[0], s.shape[1] // N_DEV, s.shape[2]) if by_rows else s.shape


_HBM = pl.BlockSpec(memory_space=pltpu.HBM)
_SEM = pl.BlockSpec(memory_space=pltpu.SEMAPHORE)


def _xfer_start(srcs, gather, by_rows, name, dep=None):
    n = len(srcs)
    lands = [lax.empty(_land_shape(s, gather, br), s.dtype) for s, br in zip(srcs, by_rows)]
    ins = [pltpu.with_memory_space_constraint(a, pltpu.HBM) for a in list(srcs) + lands]
    dspec, darg = _dep(dep)

    def body(*refs):
        s = 2 * n + len(darg)
        for cp in _peer_copies(refs[:n], refs[n:2 * n], refs[s], refs[s + 1], gather, by_rows):
            cp.start()
        for cp in _own_copies(refs[:n], refs[n:2 * n], refs[s + 2], gather, by_rows):
            cp.start()
        refs[-1][...] = jnp.zeros_like(refs[-1])

    sems = pltpu.SemaphoreType.DMA(((N_DEV - 1) * n,))
    outs = pl.pallas_call(
        body, name=name,
        out_shape=(sems, sems, pltpu.SemaphoreType.DMA((n,)), *[pltpu.HBM(a.shape, a.dtype) for a in ins],
                   _sds((8, LANES), F32)),
        in_specs=[_HBM] * (2 * n) + dspec,
        out_specs=(_SEM, _SEM, _SEM, *[_HBM] * (2 * n), pl.BlockSpec(memory_space=pltpu.VMEM)),
        input_output_aliases={i: 3 + i for i in range(2 * n)},
        compiler_params=pltpu.CompilerParams(has_side_effects=pltpu.SideEffectType.DATAFLOW_SIDE_EFFECTING))(*ins, *darg)
    return outs[0], outs[1], list(outs[3:-1]), outs[-1], outs[2]


def _own_copies(srcs, lands, sems, gather, by_rows):
    x, y, c = _mesh_pos()
    my = 4 * x + 2 * y + c
    out = []
    for a in range(len(srcs)):
        if gather:
            out.append(pltpu.make_async_copy(srcs[a], _dev_block(lands[a], my, by_rows[a]), sems.at[a]))
        else:
            out.append(pltpu.make_async_copy(_dev_block(srcs[a], my, by_rows[a]), lands[a].at[my], sems.at[a]))
    return out


def _xfer_wait(started, after, gather, by_rows, name):
    send_sems, recv_sems, bufs, _, local_sems = started
    n = len(bufs) // 2

    def body(*refs):
        for cp in _peer_copies(refs[:n], refs[n:2 * n], refs[2 * n], refs[2 * n + 1], gather, by_rows):
            cp.wait_send()
            cp.wait_recv()
        for cp in _own_copies(refs[:n], refs[n:2 * n], refs[2 * n + 2], gather, by_rows):
            cp.wait()

    outs = pl.pallas_call(
        body, name=name, out_shape=tuple(pltpu.HBM(a.shape, a.dtype) for a in bufs),
        in_specs=[_HBM] * (2 * n) + [_SEM, _SEM, _SEM, pl.BlockSpec(memory_space=pl.ANY)],
        out_specs=tuple([_HBM] * (2 * n)), input_output_aliases={i: i for i in range(2 * n)},
        compiler_params=pltpu.CompilerParams(has_side_effects=pltpu.SideEffectType.DATAFLOW_SIDE_EFFECTING))(
            *bufs, send_sems, recv_sems, local_sems, after)
    return list(outs[n:])


def _adam_update(g, w, m, v):
    c1 = 1.0 - ADAM_B1 ** ADAM_STEP
    c2 = 1.0 - ADAM_B2 ** ADAM_STEP
    nm = ADAM_B1 * m + (1.0 - ADAM_B1) * g
    nv = ADAM_B2 * v + (1.0 - ADAM_B2) * (g * g)
    return -ADAM_LR * ((nm / c1) / (jnp.sqrt(nv / c2) + ADAM_EPS) + ADAM_WD * w), nm, nv


def _adamw_body(p_ref, w_ref, m_ref, v_ref, g_ref, d_ref, nm_ref, nv_ref):
    g = p_ref[0]
    for i in range(1, N_DEV):
        g = g + p_ref[i]
    g_ref[...] = g
    d_ref[...], nm_ref[...], nv_ref[...] = _adam_update(g, w_ref[...], m_ref[...], v_ref[...])


def _adamw(parts, w, m, v, name, tr=1536):
    R = w.shape[0]
    tr = max(t for t in range(8, tr + 1, 8) if R % t == 0)

    def body(*refs):
        _adamw_body(*refs)

    row = pl.BlockSpec((tr, LANES), lambda i: (i, 0))
    return pl.pallas_call(
        body, grid=(R // tr,),
        in_specs=[pl.BlockSpec((N_DEV, tr, LANES), lambda i: (0, i, 0)), row, row, row],
        out_specs=[row, row, row, row], out_shape=[_sds((R, LANES), F32)] * 4,
        name=name, compiler_params=_cp(("parallel",)))(parts, w, m, v)


def _adamw_split(recvs, w, m, v, name, tr):
    depth, r, c = w.shape
    assert depth == len(recvs)
    tr = _tile(r, tr)

    def body(*refs):
        layer = pl.program_id(0)
        for ll in range(depth):
            @pl.when(layer == ll)
            def _(ll=ll):
                _adamw_body(refs[ll], *refs[depth:])

    wspec = pl.BlockSpec((None, tr, c), lambda l, i: (l, i, 0))
    rspecs = [pl.BlockSpec((N_DEV, None, tr, c), lambda l, i, ll=ll, t=t: (0, t, jnp.where(l == ll, i, 0), 0))
              for ll, (_, t) in enumerate(recvs)]
    return pl.pallas_call(
        body, grid=(depth, r // tr), in_specs=rspecs + [wspec, wspec, wspec],
        out_specs=[wspec] * 4, out_shape=[_sds(w.shape, F32)] * 4,
        name=name, compiler_params=_cp(("arbitrary", "arbitrary")))(*[a for a, _ in recvs], w, m, v)


def _merge_win(g, name, tr=256):
    _, nt, K, n = g.shape
    tr = _tile(K, tr)

    def body(g_ref, q_ref, r_ref):
        full = jnp.concatenate([g_ref[j] for j in range(N_DEV)], axis=1)
        q_ref[...] = full[:, 256:1792]
        zpad = jnp.zeros((tr, REST_W - 776), full.dtype)
        r_ref[...] = jnp.concatenate([full[:, 0:256], full[:, 1800:2312], full[:, 1792:1800], zpad], axis=1)

    return pl.pallas_call(
        body, grid=(nt, K // tr),
        in_specs=[pl.BlockSpec((N_DEV, None, tr, n), lambda t, i: (0, t, i, 0))],
        out_specs=[pl.BlockSpec((None, tr, 1536), lambda t, i: (t, i, 0)), pl.BlockSpec((None, tr, REST_W), lambda t, i: (t, i, 0))],
        out_shape=[_sds((nt, K, 1536), g.dtype), _sds((nt, K, REST_W), g.dtype)],
        name=name, compiler_params=_cp(("parallel", "parallel")))(g)


def _split_win(dq, dr, name, tr=256):
    K = dq.shape[0]
    tr = _tile(K, tr)
    n = (dq.shape[1] + 776) // N_DEV

    def body(q_ref, r_ref, o_ref):
        r = r_ref[...]
        full = jnp.concatenate([r[:, 0:256], q_ref[...], r[:, 768:776], r[:, 256:768]], axis=1)
        for j in range(N_DEV):
            o_ref[j] = full[:, n * j:n * (j + 1)]

    return pl.pallas_call(
        body, grid=(K // tr,),
        in_specs=[pl.BlockSpec((tr, dq.shape[1]), lambda i: (i, 0)), pl.BlockSpec((tr, REST_W), lambda i: (i, 0))],
        out_specs=pl.BlockSpec((N_DEV, tr, n), lambda i: (0, i, 0)),
        out_shape=_sds((N_DEV, K, n), F32), name=name, compiler_params=_cp(("parallel",)))(dq, dr)


WEIGHTS = ["ffn1_norm", "ffn1_w_gate", "ffn1_w_up", "ffn1_w_down", "mix_norm", "w_in", "pool_w", "pool_scale",
           "forget_bias", "conv_w", "conv_b", "conv_ln_g", "conv_ln_b", "w_out", "ffn2_norm", "ffn2_w_gate",
           "ffn2_w_up", "ffn2_w_down", "final_norm"]
FFN_PARTS = ("w_gate", "w_up", "w_down")
FFN_T = ["ffn1_w_gate", "ffn1_w_up", "ffn2_w_gate", "ffn2_w_up"]
BIG = FFN_T + ["ffn1_w_down", "ffn2_w_down", "w_in", "w_out"]
SMALL = [n for n in WEIGHTS if n not in BIG]


def _padded(n):
    return -(-n // PACK_ALIGN) * PACK_ALIGN


def _flat_pad(a):
    f = a.reshape(-1)
    return jnp.pad(f, (0, _padded(f.shape[0]) - f.shape[0]))


def _split8(a, axis):
    shp = a.shape
    a = a.reshape(shp[:axis] + (N_DEV, shp[axis] // N_DEV) + shp[axis + 1:])
    return jnp.moveaxis(a, axis, 0)


def _merge8(a, axis):
    a = jnp.moveaxis(a, 0, axis)
    shp = a.shape
    return a.reshape(shp[:axis] + (shp[axis] * shp[axis + 1],) + shp[axis + 2:])


def _pack_small(arrs):
    return jnp.concatenate([_flat_pad(arrs[n]) for n in SMALL]).reshape(-1, LANES)


def _pack_small_parts(grads):
    cols = []
    for n in SMALL:
        g = grads[n]
        if n == "conv_w":
            s = _split8(g, 2).reshape(N_DEV, -1)
        else:
            s = jnp.broadcast_to(g.reshape(1, -1), (N_DEV, g.size))
        cols.append(jnp.pad(s, ((0, 0), (0, _padded(s.shape[1]) - s.shape[1]))))
    return jnp.concatenate(cols, axis=1).reshape(N_DEV, -1, LANES)


def _unpack_small(buf, like):
    flat = buf.reshape(-1)
    out, off = {}, 0
    for n in SMALL:
        size = like[n].size
        out[n] = flat[off:off + size].reshape(like[n].shape)
        off += _padded(size)
    return out


class _Comm:
    def __init__(self, w):
        self.w = w
        self.bf = {n: (jnp.swapaxes(w[n], 1, 2) if n in FFN_T else w[n]).astype(CDT) for n in BIG}
        self.ready = {}
        self.grads = {}

    def _ffn_shards(self, l, which):
        return jnp.stack([self.bf[f"{which}_{k}"][l] for k in FFN_PARTS])

    def _put_ffn(self, l, which, rows, t):
        self.ready[(l, which)] = dict(w_gate=rows[t], w_up=rows[t + 1], w_down=rows[t + 2])

    def weights_for(self, l, stage, x):
        bf = self.bf
        dep = None
        if (l, stage) == (0, "ffn1"):
            gd, = _all_gather([self._ffn_shards(0, "ffn1")], [True], "gather_l0_ffn1")
            self._put_ffn(0, "ffn1", gd, 0)
            self.started = _xfer_start([bf["w_in"][0:1], bf["w_out"][0:1], self.w["conv_w"]], True,
                                       [False, True, False], "gather_mix0_start", dep=gd)
            dep = self.started[3]
        elif (l, stage) == (0, "mix"):
            gi, go, gc = _xfer_wait(self.started, x, True, [False, True, False], "gather_mix0_wait")
            q, r = _merge_win(gi, "merge_l0_w_in")
            self.conv_w = _merge8(gc, 2)
            self.ready[(0, "mix")] = dict(win_qkv=q[0], win_rest=r[0], w_out=go[0], conv_w=self.conv_w[0])
            rows = jnp.concatenate([self._ffn_shards(0, "ffn2"), self._ffn_shards(1, "ffn1"), self._ffn_shards(1, "ffn2")])
            self.started = _xfer_start([rows, bf["w_in"][1:2], bf["w_out"][1:2]], True, [True, False, True],
                                       "gather_rest_start")
            dep = self.started[3]
        elif (l, stage) == (0, "ffn2"):
            gd, gi, go = _xfer_wait(self.started, x, True, [True, False, True], "gather_rest_wait")
            self._put_ffn(0, "ffn2", gd, 0)
            self._put_ffn(1, "ffn1", gd, 3)
            self._put_ffn(1, "ffn2", gd, 6)
            q, r = _merge_win(gi, "merge_l1_w_in")
            self.ready[(1, "mix")] = dict(win_qkv=q[0], win_rest=r[0], w_out=go[0], conv_w=self.conv_w[1])
        return self.ready[(l, stage)], dep

    def grads_ready(self, l, stage, grads):
        for n, v in grads.items():
            self.grads[(l, n)] = v
        gr = self.grads

        def ffn_rows(layer, which, parts=FFN_PARTS):
            return [gr[(layer, f"{which}_{k}")][None] for k in parts]

        if l == 1 and "ffn1_w_gate" in grads:
            self.sent1 = _xfer_start(
                ffn_rows(1, "ffn1") + ffn_rows(1, "ffn2") + [gr[(1, "w_in")][:, None], gr[(1, "w_out")][None]],
                False, [True] * 6 + [False, True], "grads_l1_start")
            return self.sent1[3]
        if l == 0 and "ffn2_w_gate" in grads:
            self.sent_ffn2 = _xfer_start(ffn_rows(0, "ffn2"), False, [True] * 3, "grads_l0_ffn2_start")
            return self.sent_ffn2[3]
        if (l, stage) == (0, "mix"):
            self.sent_mix = _xfer_start([gr[(0, "w_in")][:, None], gr[(0, "w_out")][None]], False, [False, True],
                                        "grads_l0_mix_start")
            return self.sent_mix[3]
        if l == 0 and "ffn1_w_down" in grads:
            self.sent_down = _xfer_start([gr[(0, "ffn1_w_down")][None]], False, [True], "grads_l0_ffn1_down_start")
            return self.sent_down[3]
        if l == 0 and "ffn1_w_gate" in grads:
            self.sent_gu = _xfer_start(ffn_rows(0, "ffn1", FFN_PARTS[:2]), False, [True] * 2,
                                       "grads_l0_ffn1_gate_up_start")
            return self.sent_gu[3]
        return None

    def finish(self, m, v, after):
        w, gr = self.w, self.grads
        depth = range(w["w_in"].shape[0])
        small = {n: (gr[(None, n)] if n == "final_norm" else jnp.stack([gr[(l, n)] for l in depth])) for n in SMALL}
        sent_small = _xfer_start([_pack_small_parts(small)], False, [False], "grads_small_start")
        *r1, i1, o1 = _xfer_wait(self.sent1, after, False, [True] * 6 + [False, True], "grads_l1_wait")
        r2 = _xfer_wait(self.sent_ffn2, after, False, [True] * 3, "grads_l0_ffn2_wait")
        i0, o0 = _xfer_wait(self.sent_mix, after, False, [False, True], "grads_l0_mix_wait")

        def adam(n, recvs, tr):
            if n in FFN_T:
                out = _adamw_split(recvs, *[jnp.swapaxes(t[n], 1, 2) for t in (w, m, v)], f"adamw_{n}", tr)
                return [jnp.swapaxes(o, 1, 2) for o in out]
            return _adamw_split(recvs, w[n], m[n], v[n], f"adamw_{n}", tr)

        res = {}
        for t, k in enumerate(FFN_PARTS):
            res[f"ffn2_{k}"] = adam(f"ffn2_{k}", [(r2[t], 0), (r1[3 + t], 0)], 176)
        res["w_in"] = adam("w_in", [(i0, 0), (i1, 0)], 256)
        res["w_out"] = adam("w_out", [(o0, 0), (o1, 0)], 128)
        r0, = _xfer_wait(self.sent_down, res["w_out"][0], False, [True], "grads_l0_ffn1_down_wait")
        res["ffn1_w_down"] = adam("ffn1_w_down", [(r0, 0), (r1[2], 0)], 176)
        g0 = _xfer_wait(self.sent_gu, res["ffn1_w_down"][0], False, [True] * 2, "grads_l0_ffn1_gate_up_wait")
        res["ffn1_w_gate"] = adam("ffn1_w_gate", [(g0[0], 0), (r1[0], 0)], 176)
        res["ffn1_w_up"] = adam("ffn1_w_up", [(g0[1], 0), (r1[1], 0)], 176)
        rs, = _xfer_wait(sent_small, res["ffn1_w_up"][0], False, [False], "grads_small_wait")
        packed = _adamw(rs, _pack_small(w), _pack_small(m), _pack_small(v), "adamw_small")
        unpacked = [_unpack_small(b, w) for b in packed]
        for n in SMALL:
            res[n] = [u[n] for u in unpacked]
        return res


def kernel(x, ffn1_norm, ffn1_w_gate, ffn1_w_up, ffn1_w_down, mix_norm, w_in, pool_w, pool_scale, forget_bias, conv_w, conv_b, conv_ln_g, conv_ln_b, w_out, ffn2_norm, ffn2_w_gate, ffn2_w_up, ffn2_w_down, final_norm, loss_target, m_ffn1_norm, m_ffn1_w_gate, m_ffn1_w_up, m_ffn1_w_down, m_mix_norm, m_w_in, m_pool_w, m_pool_scale, m_forget_bias, m_conv_w, m_conv_b, m_conv_ln_g, m_conv_ln_b, m_w_out, m_ffn2_norm, m_ffn2_w_gate, m_ffn2_w_up, m_ffn2_w_down, m_final_norm, v_ffn1_norm, v_ffn1_w_gate, v_ffn1_w_up, v_ffn1_w_down, v_mix_norm, v_w_in, v_pool_w, v_pool_scale, v_forget_bias, v_conv_w, v_conv_b, v_conv_ln_g, v_conv_ln_b, v_w_out, v_ffn2_norm, v_ffn2_w_gate, v_ffn2_w_up, v_ffn2_w_down, v_final_norm):
    w = dict(zip(WEIGHTS, (ffn1_norm, ffn1_w_gate, ffn1_w_up, ffn1_w_down, mix_norm, w_in, pool_w, pool_scale, forget_bias,
                           conv_w, conv_b, conv_ln_g, conv_ln_b, w_out, ffn2_norm, ffn2_w_gate, ffn2_w_up, ffn2_w_down,
                           final_norm)))
    m = dict(zip(WEIGHTS, (m_ffn1_norm, m_ffn1_w_gate, m_ffn1_w_up, m_ffn1_w_down, m_mix_norm, m_w_in, m_pool_w, m_pool_scale,
                           m_forget_bias, m_conv_w, m_conv_b, m_conv_ln_g, m_conv_ln_b, m_w_out, m_ffn2_norm, m_ffn2_w_gate,
                           m_ffn2_w_up, m_ffn2_w_down, m_final_norm)))
    v = dict(zip(WEIGHTS, (v_ffn1_norm, v_ffn1_w_gate, v_ffn1_w_up, v_ffn1_w_down, v_mix_norm, v_w_in, v_pool_w, v_pool_scale,
                           v_forget_bias, v_conv_w, v_conv_b, v_conv_ln_g, v_conv_ln_b, v_w_out, v_ffn2_norm, v_ffn2_w_gate,
                           v_ffn2_w_up, v_ffn2_w_down, v_final_norm)))
    comm = _Comm(w)
    loss_row, gx = _local_step(x[0], loss_target[0], w, comm.weights_for, comm.grads_ready)
    loss = lax.psum(loss_row[0, 0], ("x", "y", "c"))
    res = comm.finish(m, v, gx)
    return (loss, gx[None], *[res[n][i] for i in range(4) for n in WEIGHTS])
```

```python
import math

import numpy as np
import jax
import jax.numpy as jnp
from jax import lax
from jax.experimental import pallas as pl
from jax.experimental.pallas import tpu as pltpu

F32 = jnp.float32
CDT = jnp.bfloat16
NORM_EPS = 1e-6
N_DEV = 8
LANES = 128
PACK_ALIGN = 8 * LANES
VMEM_LIMIT = 48 * 1024 * 1024

POOL_WINDOWS = (2, 4, 8, 16)
POOL_HALO = 16
CONV_K = 31
CONV_HALO = 32
HEAD_DIM = 64
N_HEADS = 8
N_PAIRS = N_HEADS // 2
ATT_SCALE = 1.0 / math.sqrt(HEAD_DIM)
NEG = -1e30

ADAM_LR, ADAM_B1, ADAM_B2, ADAM_EPS, ADAM_WD, ADAM_STEP = 0.001, 0.9, 0.999, 1e-08, 0.01, 10

REST_W = 896
REST_Z_BLK = 6


def _cp(sem):
    return pltpu.CompilerParams(dimension_semantics=sem, vmem_limit_bytes=VMEM_LIMIT)


def _tile(n, pref):
    t = min(n, pref)
    assert n % t == 0, (n, pref)
    return t


def _sigmoid(x):
    return 1.0 / (1.0 + jnp.exp(-x))


def _sds(shape, dtype):
    return jax.ShapeDtypeStruct(shape, dtype)


_ANY = pl.BlockSpec(memory_space=pl.ANY)


def _dep(dep):
    return ([], []) if dep is None else ([_ANY], [dep])


def _wshape(w):
    return w[0].shape[1:] if isinstance(w, tuple) else w.shape


def _wspec(w, block, index_map):
    if not isinstance(w, tuple):
        return w, pl.BlockSpec(block, index_map)
    arr, t = w
    return arr, pl.BlockSpec((None,) + block, lambda *g: (t,) + index_map(*g))


def _rms_fwd(x, g, name, dep=None):
    T, D = x.shape
    tm = _tile(T, 1024)

    def body(x_ref, g_ref, *rest):
        o_ref = rest[-1]
        xv = x_ref[...]
        r = lax.rsqrt(jnp.mean(xv * xv, axis=-1, keepdims=True) + NORM_EPS)
        o_ref[...] = (xv * r * g_ref[...]).astype(o_ref.dtype)

    dspec, darg = _dep(dep)
    return pl.pallas_call(
        body, grid=(T // tm,),
        in_specs=[pl.BlockSpec((tm, D), lambda i: (i, 0)), pl.BlockSpec((1, D), lambda i: (0, 0))] + dspec,
        out_specs=pl.BlockSpec((tm, D), lambda i: (i, 0)),
        out_shape=_sds((T, D), CDT), name=name, compiler_params=_cp(("parallel",)))(x, g, *darg)


def _rms_bwd(x, g, dh, gres, name):
    T, D = x.shape
    tm = _tile(T, 512)

    def body(x_ref, g_ref, dh_ref, gres_ref, gin_ref, dg_ref):
        i = pl.program_id(0)
        xv = x_ref[...]
        d = dh_ref[...]
        r = lax.rsqrt(jnp.mean(xv * xv, axis=-1, keepdims=True) + NORM_EPS)
        xh = xv * r
        dxh = d * g_ref[...]
        c = jnp.mean(dxh * xh, axis=-1, keepdims=True)
        gin_ref[...] = gres_ref[...] + r * (dxh - xh * c)
        part = jnp.sum(d * xh, axis=0, keepdims=True)

        @pl.when(i == 0)
        def _():
            dg_ref[...] = part

        @pl.when(i > 0)
        def _():
            dg_ref[...] += part

    row = pl.BlockSpec((tm, D), lambda i: (i, 0))
    vec = pl.BlockSpec((1, D), lambda i: (0, 0))
    return pl.pallas_call(
        body, grid=(T // tm,), in_specs=[row, vec, row, row], out_specs=[row, vec],
        out_shape=[_sds((T, D), F32), _sds((1, D), F32)], name=name, compiler_params=_cp(("arbitrary",)))(x, g, dh, gres)


def _loss_bwd(x, g, target, name):
    T, D = x.shape
    tm = _tile(T, 512)

    def body(x_ref, g_ref, t_ref, loss_ref, dx_ref, dg_ref):
        i = pl.program_id(0)
        xv = x_ref[...]
        gv = g_ref[...]
        r = lax.rsqrt(jnp.mean(xv * xv, axis=-1, keepdims=True) + NORM_EPS)
        xh = xv * r
        err = xh * gv - t_ref[...]
        lpart = 0.5 * jnp.sum(jnp.mean(err * err, axis=-1, keepdims=True), axis=0, keepdims=True)
        dy = err * (1.0 / D)
        dxh = dy * gv
        c = jnp.mean(dxh * xh, axis=-1, keepdims=True)
        dx_ref[...] = r * (dxh - xh * c)
        part = jnp.sum(dy * xh, axis=0, keepdims=True)
        lrow = jnp.broadcast_to(lpart, (1, LANES))

        @pl.when(i == 0)
        def _():
            dg_ref[...] = part
            loss_ref[...] = lrow

        @pl.when(i > 0)
        def _():
            dg_ref[...] += part
            loss_ref[...] += lrow

    row = pl.BlockSpec((tm, D), lambda i: (i, 0))
    vec = pl.BlockSpec((1, D), lambda i: (0, 0))
    return pl.pallas_call(
        body, grid=(T // tm,), in_specs=[row, vec, row],
        out_specs=[pl.BlockSpec((1, LANES), lambda i: (0, 0)), row, vec],
        out_shape=[_sds((1, LANES), F32), _sds((T, D), F32), _sds((1, D), F32)],
        name=name, compiler_params=_cp(("arbitrary",)))(x, g, target)


def _mm(pairs, *, name, res=None, alpha=1.0, out_dtype=F32, tm=512, tn=None, dep=None):
    T = pairs[0][0].shape[0]
    N = _wshape(pairs[0][1])[0 if pairs[0][2] else 1]
    tm = _tile(T, tm)
    tn = N if tn is None else _tile(N, tn)
    flags = [p[2] for p in pairs]
    n_in = 2 * len(pairs)

    def body(*refs):
        o_ref = refs[-1]
        acc = None
        for p, bt in enumerate(flags):
            a = refs[2 * p][...].astype(CDT)
            b = refs[2 * p + 1][...]
            dims = (((1,), (1,)), ((), ())) if bt else (((1,), (0,)), ((), ()))
            d = lax.dot_general(a, b, dims, preferred_element_type=F32)
            acc = d if acc is None else acc + d
        if alpha != 1.0:
            acc = acc * alpha
        if res is not None:
            acc = refs[n_in][...] + acc
        o_ref[...] = acc.astype(o_ref.dtype)

    in_specs, args = [], []
    for a, b, bt in pairs:
        K = a.shape[1]
        in_specs.append(pl.BlockSpec((tm, K), lambda i, j: (i, 0)))
        b, bspec = _wspec(b, (tn, K), lambda i, j: (j, 0)) if bt else _wspec(b, (K, tn), lambda i, j: (0, j))
        in_specs.append(bspec)
        args += [a, b]
    if res is not None:
        in_specs.append(pl.BlockSpec((tm, tn), lambda i, j: (i, j)))
        args.append(res)
    dspec, darg = _dep(dep)
    in_specs += dspec
    args += darg
    return pl.pallas_call(
        body, grid=(T // tm, N // tn), in_specs=in_specs,
        out_specs=pl.BlockSpec((tm, tn), lambda i, j: (i, j)),
        out_shape=_sds((T, N), out_dtype), name=name, compiler_params=_cp(("parallel", "arbitrary")))(*args)


def _mm_norm_bwd(pairs, x, g, gres, *, name, tm=256, dep=None):
    T, D = x.shape
    tm = _tile(T, tm)
    n_in = 2 * len(pairs)
    flags = [p[2] for p in pairs]

    def body(*refs):
        x_ref, g_ref, gres_ref = refs[n_in:n_in + 3]
        gin_ref, dg_ref = refs[-2:]
        i = pl.program_id(0)
        d = None
        for p, bt in enumerate(flags):
            dims = (((1,), (1,)), ((), ())) if bt else (((1,), (0,)), ((), ()))
            part = lax.dot_general(refs[2 * p][...].astype(CDT), refs[2 * p + 1][...], dims, preferred_element_type=F32)
            d = part if d is None else d + part
        xv = x_ref[...]
        r = lax.rsqrt(jnp.mean(xv * xv, axis=-1, keepdims=True) + NORM_EPS)
        xh = xv * r
        dxh = d * g_ref[...]
        c = jnp.mean(dxh * xh, axis=-1, keepdims=True)
        gin_ref[...] = gres_ref[...] + r * (dxh - xh * c)
        part = jnp.sum(d * xh, axis=0, keepdims=True)

        @pl.when(i == 0)
        def _():
            dg_ref[...] = part

        @pl.when(i > 0)
        def _():
            dg_ref[...] += part

    in_specs, args = [], []
    for a, b, bt in pairs:
        K = a.shape[1]
        b, bspec = _wspec(b, tuple(_wshape(b)), lambda i: (0, 0))
        in_specs += [pl.BlockSpec((tm, K), lambda i: (i, 0)), bspec]
        args += [a, b]
    row = pl.BlockSpec((tm, D), lambda i: (i, 0))
    vec = pl.BlockSpec((1, D), lambda i: (0, 0))
    dspec, darg = _dep(dep)
    return pl.pallas_call(
        body, grid=(T // tm,), in_specs=in_specs + [row, vec, row] + dspec, out_specs=[row, vec],
        out_shape=[_sds((T, D), F32), _sds((1, D), F32)], name=name,
        compiler_params=_cp(("arbitrary",)))(*args, x, g, gres, *darg)


def _mm_tn(a, b, *, name, alpha=1.0, tk=2048, dep=None):
    T, M = a.shape
    N = b.shape[1]
    tm = M if M <= 1024 else M // 2
    tn = N if N <= 1536 else N // 2
    assert M % tm == 0 and N % tn == 0 and tm % LANES == 0 and tn % LANES == 0
    tk = _tile(T, tk)
    nk = T // tk

    def body(a_ref, b_ref, *rest):
        o_ref = rest[-1]
        k = pl.program_id(2)
        d = lax.dot_general(a_ref[...].astype(CDT), b_ref[...].astype(CDT), (((0,), (0,)), ((), ())),
                            preferred_element_type=F32)

        @pl.when(k == 0)
        def _():
            o_ref[...] = d

        @pl.when(k > 0)
        def _():
            o_ref[...] += d

        if alpha != 1.0:
            @pl.when(k == nk - 1)
            def _():
                o_ref[...] *= alpha

    dspec, darg = _dep(dep)
    return pl.pallas_call(
        body, grid=(M // tm, N // tn, nk),
        in_specs=[pl.BlockSpec((tk, tm), lambda i, j, k: (k, i)), pl.BlockSpec((tk, tn), lambda i, j, k: (k, j))] + dspec,
        out_specs=pl.BlockSpec((tm, tn), lambda i, j, k: (i, j)),
        out_shape=_sds((M, N), F32), name=name, compiler_params=_cp(("parallel", "parallel", "arbitrary")))(a, b, *darg)


def _ffn_up(h, wgt, wut, name):
    T, D = h.shape
    Fh = _wshape(wgt)[0]
    tm = _tile(T, 4096)
    tn = _tile(Fh, 256)
    nt = (((1,), (1,)), ((), ()))

    def body(h_ref, wg_ref, wu_ref, a_ref, b_ref, s_ref):
        hv = h_ref[...]
        a = lax.dot_general(hv, wg_ref[...], nt, preferred_element_type=F32)
        b = lax.dot_general(hv, wu_ref[...], nt, preferred_element_type=F32)
        a_ref[...] = a.astype(a_ref.dtype)
        b_ref[...] = b.astype(b_ref.dtype)
        s_ref[...] = (a * _sigmoid(a) * b).astype(s_ref.dtype)

    wgt, gspec = _wspec(wgt, (tn, D), lambda i, j: (j, 0))
    wut, uspec = _wspec(wut, (tn, D), lambda i, j: (j, 0))
    ospec = pl.BlockSpec((tm, tn), lambda i, j: (i, j))
    return pl.pallas_call(
        body, grid=(T // tm, Fh // tn),
        in_specs=[pl.BlockSpec((tm, D), lambda i, j: (i, 0)), gspec, uspec],
        out_specs=[ospec, ospec, ospec],
        out_shape=[_sds((T, Fh), CDT), _sds((T, Fh), CDT), _sds((T, Fh), CDT)],
        name=name, compiler_params=_cp(("parallel", "arbitrary")))(h, wgt, wut)


def _ffn_bwd_ds(gout, wd, a, b, name, dep=None):
    T, D = gout.shape
    Fh = _wshape(wd)[0]
    tm = _tile(T, 2048)
    tn = _tile(Fh, 256)

    def body(g_ref, wd_ref, a_ref, b_ref, *rest):
        da_ref, db_ref = rest[-2:]
        dy = (0.5 * g_ref[...]).astype(CDT)
        ds = lax.dot_general(dy, wd_ref[...], (((1,), (1,)), ((), ())), preferred_element_type=F32)
        av = a_ref[...].astype(F32)
        sg = _sigmoid(av)
        da_ref[...] = (ds * b_ref[...].astype(F32) * (sg * (1.0 + av * (1.0 - sg)))).astype(da_ref.dtype)
        db_ref[...] = (ds * (av * sg)).astype(db_ref.dtype)

    ospec = pl.BlockSpec((tm, tn), lambda i, j: (i, j))
    dspec, darg = _dep(dep)
    wd, wspec = _wspec(wd, (tn, D), lambda i, j: (j, 0))
    return pl.pallas_call(
        body, grid=(T // tm, Fh // tn),
        in_specs=[pl.BlockSpec((tm, D), lambda i, j: (i, 0)), wspec, ospec, ospec] + dspec,
        out_specs=[ospec, ospec],
        out_shape=[_sds((T, Fh), CDT), _sds((T, Fh), CDT)],
        name=name, compiler_params=_cp(("parallel", "arbitrary")))(gout, wd, a, b, *darg)


def _ffn_fwd(x, gamma, wgt, wut, wd, tag, dep=None):
    h = _rms_fwd(x, gamma, f"{tag}_norm", dep)
    a, b, s = _ffn_up(h, wgt, wut, f"{tag}_up")
    dep = None
    if callable(wd):
        wd, dep = wd(a)
    y = _mm([(s, wd, False)], res=x, alpha=0.5, name=f"{tag}_down", dep=dep)
    return y, (x, h, a, b, s), wd


def _ffn_bwd(saved, gamma, wgt, wut, wd, gout, tag, dep, on_grads):
    x, h, a, b, s = saved
    dwd = _mm_tn(s, gout, alpha=0.5, name=f"{tag}_dwd", dep=dep)
    da, db = _ffn_bwd_ds(gout, wd, a, b, f"{tag}_bwd_ds", on_grads(dict(w_down=dwd)))
    dwgt = _mm_tn(da, h, name=f"{tag}_dwg")
    dwut = _mm_tn(db, h, name=f"{tag}_dwu")
    dep = on_grads(dict(w_gate=dwgt, w_up=dwut))
    return _mm_norm_bwd([(da, wgt, False), (db, wut, False)], x, gamma, gout, name=f"{tag}_dh_norm_bwd", dep=dep)


def _fgate_fwd(rest, bias, name, bt=512):
    T = rest.shape[0]
    bt = _tile(T, bt)

    def body(z_ref, b_ref, fc_ref, ft_ref, carry):
        i = pl.program_id(0)

        @pl.when(i == 0)
        def _():
            carry[...] = jnp.zeros_like(carry)

        zb = z_ref[...] + b_ref[...]
        e = jnp.exp(-jnp.abs(zb))
        u = 1.0 + e
        log1p_e = jnp.where(u == 1.0, e, jnp.log(u) * (e / (u - 1.0)))
        x = jnp.minimum(zb, 0.0) - log1p_e
        row = lax.broadcasted_iota(jnp.int32, x.shape, 0)
        sh = 1
        while sh < bt:
            x = x + jnp.where(row >= sh, pltpu.roll(x, sh, 0), 0.0)
            sh *= 2
        f = x + carry[...]
        carry[...] = f[bt - 1:bt, :]
        fc_ref[...] = f
        ft_ref[...] = jnp.transpose(f)[0:N_HEADS, :]

    return pl.pallas_call(
        body, grid=(T // bt,),
        in_specs=[pl.BlockSpec((bt, LANES), lambda i: (i, REST_Z_BLK)), pl.BlockSpec((1, LANES), lambda i: (0, 0))],
        out_specs=[pl.BlockSpec((bt, LANES), lambda i: (i, 0)), pl.BlockSpec((N_HEADS, bt), lambda i: (0, i))],
        out_shape=[_sds((T, LANES), F32), _sds((N_HEADS, T), F32)],
        scratch_shapes=[pltpu.VMEM((1, LANES), F32)],
        name=name, compiler_params=_cp(("arbitrary",)))(rest, bias)


def _fgate_bwd(dfk, rest, bias, name, bt=512):
    T = rest.shape[0]
    bt = _tile(T, bt)
    nb = T // bt

    def body(df_ref, z_ref, b_ref, dz_ref, db_ref, carry):
        i = pl.program_id(0)

        @pl.when(i == 0)
        def _():
            carry[...] = jnp.zeros_like(carry)

        dfv = df_ref[...]
        lane = lax.broadcasted_iota(jnp.int32, (bt, LANES), 1)
        x = jnp.zeros((bt, LANES), F32)
        for h in range(N_HEADS):
            x = jnp.where(lane == h, dfv[:, HEAD_DIM * h:HEAD_DIM * h + 1], x)
        row = lax.broadcasted_iota(jnp.int32, x.shape, 0)
        sh = 1
        while sh < bt:
            x = x + jnp.where(row + sh < bt, pltpu.roll(x, bt - sh, 0), 0.0)
            sh *= 2
        dlf = x + carry[...]
        carry[...] = dlf[0:1, :]
        zb = z_ref[...] + b_ref[...]
        dz = jnp.where(lane < N_HEADS, dlf * _sigmoid(-zb), 0.0)
        dz_ref[...] = dz.astype(dz_ref.dtype)
        part = jnp.sum(dz, axis=0, keepdims=True)

        @pl.when(i == 0)
        def _():
            db_ref[...] = part

        @pl.when(i > 0)
        def _():
            db_ref[...] += part

    return pl.pallas_call(
        body, grid=(nb,),
        in_specs=[pl.BlockSpec((bt, 4 * LANES), lambda i: (nb - 1 - i, 0)),
                  pl.BlockSpec((bt, LANES), lambda i: (nb - 1 - i, REST_Z_BLK)),
                  pl.BlockSpec((1, LANES), lambda i: (0, 0))],
        out_specs=[pl.BlockSpec((bt, LANES), lambda i: (nb - 1 - i, 0)), pl.BlockSpec((1, LANES), lambda i: (0, 0))],
        out_shape=[_sds((T, LANES), CDT), _sds((1, LANES), F32)],
        scratch_shapes=[pltpu.VMEM((1, LANES), F32)],
        name=name, compiler_params=_cp(("arbitrary",)))(dfk, rest, bias)


def _by_group(vals, lane):
    out = vals[-1]
    for g in range(len(vals) - 2, -1, -1):
        out = jnp.where(lane // 64 == g, vals[g], out)
    return out


def _pool_counts(t0, n, lane):
    t = t0 + lax.broadcasted_iota(jnp.int32, (n, 256), 0)
    return _by_group([jnp.minimum(t + 1, w) for w in POOL_WINDOWS], lane).astype(F32)


def _pooled(u, halo, i, bt):
    lane = lax.broadcasted_iota(jnp.int32, (bt, 256), 1)
    ext = jnp.concatenate([jnp.where(i > 0, halo, 0.0), u], axis=0)
    sums, s, sh = [], ext, 1
    for _ in POOL_WINDOWS:
        s = s + pltpu.roll(s, sh, 0)
        sums.append(s[POOL_HALO:, :])
        sh *= 2
    return _by_group(sums, lane) / _pool_counts(i * bt, bt, lane) - u


def _pool_fwd(rest, wbd, scale, name, bt=512):
    T = rest.shape[0]
    bt = _tile(T, bt)
    hb = bt // POOL_HALO

    def body(u_ref, halo_ref, w_ref, sc_ref, o_ref):
        i = pl.program_id(0)
        pooled = _pooled(u_ref[...], halo_ref[...], i, bt)
        mixed = jnp.dot(pooled.astype(CDT), w_ref[...], preferred_element_type=F32)
        o_ref[...] = (mixed * sc_ref[...]).astype(o_ref.dtype)

    return pl.pallas_call(
        body, grid=(T // bt,),
        in_specs=[pl.BlockSpec((bt, 256), lambda i: (i, 0)),
                  pl.BlockSpec((POOL_HALO, 256), lambda i: (jnp.maximum(i * hb - 1, 0), 0)),
                  pl.BlockSpec((256, 256), lambda i: (0, 0)), pl.BlockSpec((1, 256), lambda i: (0, 0))],
        out_specs=pl.BlockSpec((bt, 256), lambda i: (i, 0)),
        out_shape=_sds((T, 256), CDT), name=name, compiler_params=_cp(("parallel",)))(rest, rest, wbd, scale)


def _pool_bwd(dcat, rest, wbd, scale, name, bt=512):
    T = rest.shape[0]
    bt = _tile(T, bt)
    hb = bt // POOL_HALO
    nb = T // bt
    n = bt + POOL_HALO

    def body(dy_ref, dyn_ref, u_ref, halo_ref, w_ref, sc_ref, du_ref, dw_ref, dsc_ref):
        i = pl.program_id(0)
        lane = lax.broadcasted_iota(jnp.int32, (bt, 256), 1)
        w = w_ref[...]
        sc = sc_ref[...]
        pooled = _pooled(u_ref[...], halo_ref[...], i, bt)
        pooled_c = pooled.astype(CDT)
        mixed = jnp.dot(pooled_c, w, preferred_element_type=F32)
        dy = dy_ref[...]
        dm = (dy * sc).astype(CDT)
        dsc = jnp.sum(dy * mixed, axis=0, keepdims=True)
        dw = lax.dot_general(pooled_c, dm, (((0,), (0,)), ((), ())), preferred_element_type=F32)
        nt = (((1,), (1,)), ((), ()))
        dpl = lax.dot_general(dm, w, nt, preferred_element_type=F32)
        dmn = (jnp.where(i < nb - 1, dyn_ref[...], 0.0) * sc).astype(CDT)
        dpln = lax.dot_general(dmn, w, nt, preferred_element_type=F32)
        lane_h = lax.broadcasted_iota(jnp.int32, (POOL_HALO, 256), 1)
        ext = jnp.concatenate([dpl / _pool_counts(i * bt, bt, lane),
                               dpln / _pool_counts((i + 1) * bt, POOL_HALO, lane_h)], axis=0)
        sums, s, sh = [], ext, 1
        for _ in POOL_WINDOWS:
            s = s + pltpu.roll(s, n - sh, 0)
            sums.append(s[0:bt, :])
            sh *= 2
        du_ref[...] = (_by_group(sums, lane) - dpl).astype(du_ref.dtype)

        @pl.when(i == 0)
        def _():
            dw_ref[...] = dw
            dsc_ref[...] = dsc

        @pl.when(i > 0)
        def _():
            dw_ref[...] += dw
            dsc_ref[...] += dsc

    full = pl.BlockSpec((256, 256), lambda i: (0, 0))
    vec = pl.BlockSpec((1, 256), lambda i: (0, 0))
    return pl.pallas_call(
        body, grid=(nb,),
        in_specs=[pl.BlockSpec((bt, 256), lambda i: (i, 0)),
                  pl.BlockSpec((POOL_HALO, 256), lambda i: (jnp.minimum((i + 1) * hb, nb * hb - 1), 0)),
                  pl.BlockSpec((bt, 256), lambda i: (i, 0)),
                  pl.BlockSpec((POOL_HALO, 256), lambda i: (jnp.maximum(i * hb - 1, 0), 0)),
                  full, vec],
        out_specs=[pl.BlockSpec((bt, 256), lambda i: (i, 0)), full, vec],
        out_shape=[_sds((T, 256), CDT), _sds((256, 256), F32), _sds((1, 256), F32)],
        name=name, compiler_params=_cp(("arbitrary",)))(dcat, dcat, rest, rest, wbd, scale)


def _glu_ext(a_ref, g_ref, ah_ref, gh_ref, i):
    u = a_ref[...] * _sigmoid(g_ref[...])
    uh = jnp.where(i > 0, ah_ref[...] * _sigmoid(gh_ref[...]), 0.0)
    return jnp.concatenate([uh, u], axis=0)


def _conv_fwd(rest, cw, cb, lg, lb, name, bt=512):
    T = rest.shape[0]
    bt = _tile(T, bt)
    hb = bt // CONV_HALO

    def body(a_ref, g_ref, ah_ref, gh_ref, cw_ref, cb_ref, lg_ref, lb_ref, o_ref, y_ref):
        i = pl.program_id(0)
        ext = _glu_ext(a_ref, g_ref, ah_ref, gh_ref, i)
        w = cw_ref[...]
        acc = w[CONV_K - 1:CONV_K, :] * ext
        for k in range(CONV_K - 1):
            acc = acc + w[k:k + 1, :] * pltpu.roll(ext, CONV_K - 1 - k, 0)
        y = acc[CONV_HALO:, :] + cb_ref[...]
        y_ref[...] = y
        yc = y - jnp.mean(y, axis=-1, keepdims=True)
        yn = yc * lax.rsqrt(jnp.mean(yc * yc, axis=-1, keepdims=True) + NORM_EPS)
        z = yn * lg_ref[...] + lb_ref[...]
        o_ref[...] = (z * _sigmoid(z)).astype(o_ref.dtype)

    def cur(c):
        return pl.BlockSpec((bt, 256), lambda i: (i, c))

    def prev(c):
        return pl.BlockSpec((CONV_HALO, 256), lambda i: (jnp.maximum(i * hb - 1, 0), c))

    vec = pl.BlockSpec((1, 256), lambda i: (0, 0))
    return pl.pallas_call(
        body, grid=(T // bt,),
        in_specs=[cur(1), cur(2), prev(1), prev(2), pl.BlockSpec((CONV_HALO, 256), lambda i: (0, 0)), vec, vec, vec],
        out_specs=[pl.BlockSpec((bt, 256), lambda i: (i, 0)), pl.BlockSpec((bt, 256), lambda i: (i, 0))],
        out_shape=[_sds((T, 256), CDT), _sds((T, 256), F32)],
        name=name, compiler_params=_cp(("parallel",)))(rest, rest, rest, rest, cw, cb, lg, lb)


def _conv_bwd(dcat, yconv, rest, cw, lg, lb, name, bt=512):
    T = rest.shape[0]
    bt = _tile(T, bt)
    hb = bt // CONV_HALO
    nb = T // bt
    n = bt + CONV_HALO

    def body(dy_ref, dyn_ref, y_ref, yn_ref, a_ref, g_ref, ah_ref, gh_ref, cw_ref, lg_ref, lb_ref,
             da_ref, dg_ref, dcw_ref, dcb_ref, dlg_ref, dlb_ref):
        i = pl.program_id(0)
        lgv = lg_ref[...]
        lbv = lb_ref[...]

        def ln_swish_bwd(dout, y):
            yc = y - jnp.mean(y, axis=-1, keepdims=True)
            rs = lax.rsqrt(jnp.mean(yc * yc, axis=-1, keepdims=True) + NORM_EPS)
            yn = yc * rs
            z = yn * lgv + lbv
            sg = _sigmoid(z)
            dz = dout * (sg * (1.0 + z * (1.0 - sg)))
            dyn = dz * lgv
            dyc = rs * (dyn - jnp.mean(dyn, axis=-1, keepdims=True) - yn * jnp.mean(dyn * yn, axis=-1, keepdims=True))
            return dyc, dz, yn

        dyc, dz, yn = ln_swish_bwd(dy_ref[...], y_ref[...])
        dyc_next, _, _ = ln_swish_bwd(dyn_ref[...], yn_ref[...])
        dyc_next = jnp.where(i < nb - 1, dyc_next, 0.0)
        ext_u = _glu_ext(a_ref, g_ref, ah_ref, gh_ref, i)
        ext_d = jnp.concatenate([dyc, dyc_next], axis=0)
        w = cw_ref[...]
        du = w[CONV_K - 1:CONV_K, :] * ext_d
        rows = []
        for k in range(CONV_K):
            s = CONV_K - 1 - k
            if s > 0:
                du = du + w[k:k + 1, :] * pltpu.roll(ext_d, n - s, 0)
                us = pltpu.roll(ext_u, s, 0)[CONV_HALO:, :]
            else:
                us = ext_u[CONV_HALO:, :]
            rows.append(jnp.sum(dyc * us, axis=0, keepdims=True))
        rows.append(jnp.zeros((1, 256), F32))
        dcw = jnp.concatenate(rows, axis=0)
        du = du[0:bt, :]
        av = a_ref[...]
        sg = _sigmoid(g_ref[...])
        da_ref[...] = (du * sg).astype(da_ref.dtype)
        dg_ref[...] = (du * av * (sg * (1.0 - sg))).astype(dg_ref.dtype)
        dcb = jnp.sum(dyc, axis=0, keepdims=True)
        dlg = jnp.sum(dz * yn, axis=0, keepdims=True)
        dlb = jnp.sum(dz, axis=0, keepdims=True)

        @pl.when(i == 0)
        def _():
            dcw_ref[...] = dcw
            dcb_ref[...] = dcb
            dlg_ref[...] = dlg
            dlb_ref[...] = dlb

        @pl.when(i > 0)
        def _():
            dcw_ref[...] += dcw
            dcb_ref[...] += dcb
            dlg_ref[...] += dlg
            dlb_ref[...] += dlb

    def cur(c):
        return pl.BlockSpec((bt, 256), lambda i: (i, c))

    def prev(c):
        return pl.BlockSpec((CONV_HALO, 256), lambda i: (jnp.maximum(i * hb - 1, 0), c))

    def nxt(c):
        return pl.BlockSpec((CONV_HALO, 256), lambda i: (jnp.minimum((i + 1) * hb, nb * hb - 1), c))

    vec = pl.BlockSpec((1, 256), lambda i: (0, 0))
    wfull = pl.BlockSpec((CONV_HALO, 256), lambda i: (0, 0))
    return pl.pallas_call(
        body, grid=(nb,),
        in_specs=[cur(3), nxt(3), cur(0), nxt(0), cur(1), cur(2), prev(1), prev(2), wfull, vec, vec],
        out_specs=[cur(0), cur(0), wfull, vec, vec, vec],
        out_shape=[_sds((T, 256), CDT), _sds((T, 256), CDT), _sds((CONV_HALO, 256), F32),
                   _sds((1, 256), F32), _sds((1, 256), F32), _sds((1, 256), F32)],
        name=name, compiler_params=_cp(("arbitrary",)))(dcat, dcat, yconv, yconv, rest, rest, rest, rest, cw, lg, lb)


def _half_mask(shape, a):
    lane = lax.broadcasted_iota(jnp.int32, shape, 1)
    return (lane // HEAD_DIM) == a


def _attn_fwd(qkv, fcol, frow, name, blk=1024):
    T = qkv.shape[0]
    blk = _tile(T, blk)
    nq = T // blk
    nt = (((1,), (1,)), ((), ()))

    def body(q_ref, k_ref, v_ref, fc_ref, fr_ref, o_ref, lse_ref):
        p_id = pl.program_id(0)
        i = pl.program_id(1)
        q2 = q_ref[...]
        fc = fc_ref[...]
        lane = lax.broadcasted_iota(jnp.int32, (blk, LANES), 1)
        tri = lax.broadcasted_iota(jnp.int32, (blk, blk), 1) <= lax.broadcasted_iota(jnp.int32, (blk, blk), 0)
        masks = [_half_mask(q2.shape, a) for a in range(2)]
        qs = [jnp.where(hm, q2, jnp.zeros_like(q2)) * ATT_SCALE for hm in masks]
        fqs = [jnp.sum(jnp.where(lane == 2 * p_id + a, fc, 0.0), axis=1, keepdims=True) for a in range(2)]

        def tile(j, carry, masked):
            cols = pl.ds(pl.multiple_of(j * blk, blk), blk)
            kj = k_ref[cols, :]
            vj = v_ref[cols, :]
            out = []
            for a in range(2):
                m, acc = carry[2 * a:2 * a + 2]
                va = jnp.where(masks[a], vj, jnp.ones_like(vj))
                s = lax.dot_general(qs[a], kj, nt, preferred_element_type=F32) + (fqs[a] - fr_ref[a:a + 1, cols])
                if masked:
                    s = jnp.where(tri, s, NEG)
                m_new = jnp.maximum(m, jnp.max(s, axis=1, keepdims=True))
                alpha = jnp.exp(m - m_new)
                pr = jnp.exp(s - m_new)
                hi = lax.bitcast_convert_type(lax.bitcast_convert_type(pr, jnp.uint32) & jnp.uint32(0xFFFF0000), F32)
                pv = (jnp.dot(hi.astype(CDT), va, preferred_element_type=F32)
                      + jnp.dot((pr - hi).astype(CDT), va, preferred_element_type=F32))
                out += [m_new, alpha * acc + pv]
            return tuple(out)

        init = (jnp.full((blk, 1), NEG, F32), jnp.zeros((blk, LANES), F32)) * 2
        carry = lax.fori_loop(0, i, lambda j, c: tile(j, c, False), init)
        carry = tile(i, carry, True)
        ls = [carry[1][:, HEAD_DIM:HEAD_DIM + 1], carry[3][:, 0:1]]
        lo = lane < HEAD_DIM
        o_ref[...] = jnp.where(lo, carry[1] / ls[0], carry[3] / ls[1])
        lse_t = jnp.transpose(jnp.where(lo, carry[0] + jnp.log(ls[0]), carry[2] + jnp.log(ls[1])))
        lse_ref[...] = jnp.concatenate([lse_t[0:1, :], lse_t[HEAD_DIM:HEAD_DIM + 1, :]], axis=0)

    return pl.pallas_call(
        body, grid=(N_PAIRS, nq),
        in_specs=[pl.BlockSpec((blk, LANES), lambda p, i: (i, p)),
                  pl.BlockSpec((T, LANES), lambda p, i: (0, N_PAIRS + p)),
                  pl.BlockSpec((T, LANES), lambda p, i: (0, 2 * N_PAIRS + p)),
                  pl.BlockSpec((blk, LANES), lambda p, i: (i, 0)),
                  pl.BlockSpec((None, 2, T), lambda p, i: (p, 0, 0))],
        out_specs=[pl.BlockSpec((blk, LANES), lambda p, i: (i, p)), pl.BlockSpec((None, 2, blk), lambda p, i: (p, 0, i))],
        out_shape=[_sds((T, N_PAIRS * LANES), F32), _sds((N_PAIRS, 2, T), F32)],
        name=name, compiler_params=_cp(("parallel", "arbitrary")))(qkv, qkv, qkv, fcol, frow)


def _attn_delta(dcat, o, name, blk=512):
    T = o.shape[0]
    blk = _tile(T, blk)

    def body(d_ref, o_ref, out_ref):
        prod = d_ref[:, 256:768].astype(CDT).astype(F32) * o_ref[...]
        pt = jnp.transpose(prod)
        out_ref[...] = jnp.sum(pt.reshape(N_HEADS, HEAD_DIM, blk), axis=1)

    return pl.pallas_call(
        body, grid=(T // blk,),
        in_specs=[pl.BlockSpec((blk, 1024), lambda i: (i, 0)), pl.BlockSpec((blk, 512), lambda i: (i, 0))],
        out_specs=pl.BlockSpec((N_HEADS, blk), lambda i: (0, i)),
        out_shape=_sds((N_HEADS, T), F32), name=name, compiler_params=_cp(("parallel",)))(dcat, o)


def _attn_bwd(qkv, dcat, fcol, frow, lse, delta, name, blk=1024):
    T = qkv.shape[0]
    blk = _tile(T, blk)
    nq = T // blk
    nt = (((1,), (1,)), ((), ()))

    def body(q_ref, do_ref, k_ref, v_ref, fc_ref, fr_ref, lse_ref, dl_ref, dqt_ref, dk_ref, dv_ref, df_ref):
        p_id = pl.program_id(0)
        j = pl.program_id(1)

        @pl.when(j == 0)
        def _():
            dqt_ref[...] = jnp.zeros_like(dqt_ref)

        k2 = k_ref[...]
        v2 = v_ref[...]
        fc = fc_ref[...]
        lane = lax.broadcasted_iota(jnp.int32, (blk, LANES), 1)
        tri = lax.broadcasted_iota(jnp.int32, (blk, blk), 0) <= lax.broadcasted_iota(jnp.int32, (blk, blk), 1)
        masks = [_half_mask(k2.shape, a) for a in range(2)]
        kas = [jnp.where(hm, k2, jnp.zeros_like(k2)) * ATT_SCALE for hm in masks]
        kats = [jnp.transpose(ka) for ka in kas]
        vas = [jnp.where(hm, v2, jnp.zeros_like(v2)) for hm in masks]
        fks = [jnp.sum(jnp.where(lane == 2 * p_id + a, fc, 0.0), axis=1, keepdims=True) for a in range(2)]

        def tile(i, carry, masked):
            rows = pl.ds(pl.multiple_of(i * blk, blk), blk)
            qi = q_ref[rows, :]
            doi = do_ref[rows, :].astype(CDT)
            out = []
            dqt = None
            for a in range(2):
                dk_acc, dv_acc, df_acc = carry[3 * a:3 * a + 3]
                st = lax.dot_general(kas[a], qi, nt, preferred_element_type=F32)
                e = (st + (fr_ref[a:a + 1, rows] - fks[a])) - lse_ref[a:a + 1, rows]
                if masked:
                    e = jnp.where(tri, e, NEG)
                pt = jnp.exp(e)
                dpt = lax.dot_general(vas[a], doi, nt, preferred_element_type=F32)
                ds32 = pt * (dpt - dl_ref[a:a + 1, rows])
                dst = ds32.astype(CDT)
                df_acc = df_acc + jnp.sum(ds32, axis=1, keepdims=True)
                dv_acc = dv_acc + jnp.dot(pt.astype(CDT), doi, preferred_element_type=F32)
                dk_acc = dk_acc + jnp.dot(dst, qi, preferred_element_type=F32)
                part = jnp.dot(kats[a], dst, preferred_element_type=F32)
                dqt = part if dqt is None else dqt + part
                out += [dk_acc, dv_acc, df_acc]
            dqt_ref[:, rows] += dqt
            return tuple(out)

        init = (jnp.zeros((blk, LANES), F32), jnp.zeros((blk, LANES), F32), jnp.zeros((blk, 1), F32)) * 2
        carry = tile(j, init, True)
        carry = lax.fori_loop(j + 1, nq, lambda i, c: tile(i, c, False), carry)
        lo = lane < HEAD_DIM
        dk_ref[...] = (jnp.where(lo, carry[0], carry[3]) * ATT_SCALE).astype(dk_ref.dtype)
        dv_ref[...] = jnp.where(lo, carry[1], carry[4]).astype(dv_ref.dtype)
        df_ref[...] = -jnp.where(lo, carry[2], carry[5])

    res = pl.BlockSpec((T, LANES), lambda p, j: (0, p))
    rows = pl.BlockSpec((None, 2, T), lambda p, j: (p, 0, 0))
    kv_out = pl.BlockSpec((blk, LANES), lambda p, j: (j, p))
    return pl.pallas_call(
        body, grid=(N_PAIRS, nq),
        in_specs=[res, pl.BlockSpec((T, LANES), lambda p, j: (0, 2 + p)),
                  pl.BlockSpec((blk, LANES), lambda p, j: (j, N_PAIRS + p)),
                  pl.BlockSpec((blk, LANES), lambda p, j: (j, 2 * N_PAIRS + p)),
                  pl.BlockSpec((blk, LANES), lambda p, j: (j, 0)), rows, rows, rows],
        out_specs=[pl.BlockSpec((LANES, T), lambda p, j: (p, 0)), kv_out, kv_out, kv_out],
        out_shape=[_sds((N_PAIRS * LANES, T), F32), _sds((T, N_PAIRS * LANES), CDT), _sds((T, N_PAIRS * LANES), CDT),
                   _sds((T, N_PAIRS * LANES), F32)],
        name=name, compiler_params=_cp(("parallel", "arbitrary")))(qkv, dcat, qkv, qkv, fcol, frow, lse, delta)


def _mixer_fwd(x, wts, tag, dep=None):
    T = x.shape[0]
    h = _rms_fwd(x, wts["mix_norm"], f"{tag}_norm", dep)
    qkv = _mm([(h, wts["win_qkv"], False)], out_dtype=CDT, tm=1024, tn=768, name=f"{tag}_in_qkv")
    rest = _mm([(h, wts["win_rest"], False)], tm=1024, name=f"{tag}_in_rest")
    fcol, frow8 = _fgate_fwd(rest, wts["fbias"], f"{tag}_fgate")
    frow = frow8.reshape(N_PAIRS, 2, T)
    ya = _pool_fwd(rest, wts["pool_wbd"], wts["pool_scale"], f"{tag}_pool")
    o, lse = _attn_fwd(qkv, fcol, frow, f"{tag}_attn")
    yc, yconv = _conv_fwd(rest, wts["conv_w"], wts["conv_b"], wts["conv_ln_g"], wts["conv_ln_b"], f"{tag}_conv")
    cat = jnp.concatenate([ya, o.astype(CDT), yc], axis=1)
    y = _mm([(cat, wts["w_out"], False)], res=x, name=f"{tag}_out")
    return y, (x, h, qkv, rest, fcol, frow, o, lse, yconv, cat)


def _mixer_bwd(saved, wts, gout, tag, dep=None):
    x, h, qkv, rest, fcol, frow, o, lse, yconv, cat = saved
    T = x.shape[0]
    dcat = _mm([(gout, wts["w_out"], True)], name=f"{tag}_dcat", dep=dep)
    dwout = _mm_tn(cat, gout, name=f"{tag}_dwout")
    du, dpw, dpsc = _pool_bwd(dcat, rest, wts["pool_wbd"], wts["pool_scale"], f"{tag}_pool_bwd")
    delta = _attn_delta(dcat, o, f"{tag}_attn_delta").reshape(N_PAIRS, 2, T)
    dqt, dk, dv, dfk = _attn_bwd(qkv, dcat, fcol, frow, lse, delta, f"{tag}_attn_bwd")
    dq = dqt.T.astype(CDT)
    dz, dfb = _fgate_bwd(dfk, rest, wts["fbias"], f"{tag}_fgate_bwd")
    da, dg, dcw, dcb, dlg, dlb = _conv_bwd(dcat, yconv, rest, wts["conv_w"], wts["conv_ln_g"], wts["conv_ln_b"],
                                           f"{tag}_conv_bwd")
    dp_qkv = jnp.concatenate([dq, dk, dv], axis=1).astype(CDT)
    dp_rest = jnp.concatenate([du, da, dg, dz], axis=1)
    dwin_qkv = _mm_tn(h, dp_qkv, name=f"{tag}_dwin_qkv")
    dwin_rest = _mm_tn(h, dp_rest, name=f"{tag}_dwin_rest")
    gin, dgamma = _mm_norm_bwd([(dp_qkv, wts["win_qkv"], True), (dp_rest, wts["win_rest"], True)], x, wts["mix_norm"],
                               gout, name=f"{tag}_dh_norm_bwd")
    dwin = _split_win(dwin_qkv, dwin_rest, f"{tag}_dwin_split")
    dpool_w = jnp.stack([dpw[64 * g:64 * g + 64, 64 * g:64 * g + 64] for g in range(4)])
    grads = dict(mix_norm=dgamma[0], w_in=dwin, pool_w=dpool_w, pool_scale=dpsc[0], forget_bias=dfb[0, 0:N_HEADS],
                 conv_w=dcw[0:CONV_K], conv_b=dcb[0], conv_ln_g=dlg[0], conv_ln_b=dlb[0], w_out=dwout)
    return gin, grads


def _rep_layer(rep, l):
    pw = rep["pool_w"][l].astype(CDT)
    wbd = jnp.zeros((256, 256), CDT)
    for g in range(4):
        wbd = lax.dynamic_update_slice(wbd, pw[g], (64 * g, 64 * g))
    return dict(
        ffn1_norm=rep["ffn1_norm"][l][None], ffn2_norm=rep["ffn2_norm"][l][None], mix_norm=rep["mix_norm"][l][None],
        fbias=jnp.pad(rep["forget_bias"][l], (0, LANES - N_HEADS))[None],
        pool_wbd=wbd, pool_scale=rep["pool_scale"][l][None], conv_b=rep["conv_b"][l][None],
        conv_ln_g=rep["conv_ln_g"][l][None], conv_ln_b=rep["conv_ln_b"][l][None])


def _local_step(x, target, rep, weights_for, grads_ready):
    depth = rep["ffn1_norm"].shape[0]
    kept = []
    for l in range(depth):
        r = _rep_layer(rep, l)
        w1, dep = weights_for(l, "ffn1", x)

        def late_down(after, l=l):
            got, tok = weights_for(l, "ffn1_down", after)
            return got["w_down"], tok

        x, s1, wd = _ffn_fwd(x, r["ffn1_norm"], w1["w_gate"], w1["w_up"], w1["w_down"] if "w_down" in w1 else late_down,
                             f"l{l}_ffn1", dep)
        w1 = dict(w1, w_down=wd)
        wm, dep = weights_for(l, "mix", x)
        wm = dict(r, win_qkv=wm["win_qkv"], win_rest=wm["win_rest"], w_out=wm["w_out"],
                  conv_w=jnp.pad(wm["conv_w"], ((0, CONV_HALO - CONV_K), (0, 0))))
        x, s2 = _mixer_fwd(x, wm, f"l{l}_mix", dep)
        w2, dep = weights_for(l, "ffn2", x)
        x, s3, _ = _ffn_fwd(x, r["ffn2_norm"], w2["w_gate"], w2["w_up"], w2["w_down"], f"l{l}_ffn2", dep)
        kept.append((r, w1, wm, w2, s1, s2, s3))
    loss, g, dfinal = _loss_bwd(x, rep["final_norm"][None], target, "loss_head")
    dep = grads_ready(None, "final", dict(final_norm=dfinal[0]))
    for l in reversed(range(depth)):
        r, w1, wm, w2, s1, s2, s3 = kept[l]

        def ffn_grads(which, l=l):
            return lambda gr: grads_ready(l, which, {f"{which}_{k}": v for k, v in gr.items()})

        g, dn = _ffn_bwd(s3, r["ffn2_norm"], w2["w_gate"], w2["w_up"], w2["w_down"], g, f"l{l}_ffn2", dep, ffn_grads("ffn2"))
        grads_ready(l, "norm", dict(ffn2_norm=dn[0]))
        g, gm = _mixer_bwd(s2, wm, g, f"l{l}_mix")
        dep = grads_ready(l, "mix", gm)
        g, dn = _ffn_bwd(s1, r["ffn1_norm"], w1["w_gate"], w1["w_up"], w1["w_down"], g, f"l{l}_ffn1", dep, ffn_grads("ffn1"))
        dep = grads_ready(l, "norm", dict(ffn1_norm=dn[0]))
    return loss, g


def _mesh_pos():
    return lax.axis_index("x"), lax.axis_index("y"), lax.axis_index("c")


def _dev_block(ref, dev, by_rows):
    if by_rows:
        r = ref.shape[1] // N_DEV
        return ref.at[:, pl.ds(dev * r, r), :]
    return ref.at[dev]


def _all_gather(shards, by_rows, name):
    n_arr = len(shards)
    out_shape = [_sds((s.shape[0], N_DEV * s.shape[1], s.shape[2]) if br else (N_DEV,) + s.shape, s.dtype)
                 for s, br in zip(shards, by_rows)]

    def body(*refs):
        xs, outs = refs[:n_arr], refs[n_arr:2 * n_arr]
        send_sems, recv_sems, local_sems = refs[2 * n_arr:]
        x, y, c = _mesh_pos()
        me, sibling = (x, y, c), (x, y, 1 - c)
        chips = [(1 - x, y), (x, 1 - y), (1 - x, 1 - y)]

        def rows(a, px, py, pc):
            return _dev_block(outs[a], 4 * px + 2 * py + pc, by_rows[a])

        def copy(k, a, block, to, src=None):
            return pltpu.make_async_remote_copy(
                src_ref=rows(a, *block) if src is None else src, dst_ref=rows(a, *block),
                send_sem=send_sems.at[k, a], recv_sem=recv_sems.at[k, a],
                device_id=to, device_id_type=pl.DeviceIdType.MESH)

        arrs = range(n_arr)
        mine = [pltpu.make_async_copy(xs[a], rows(a, *me), local_sems.at[a]) for a in arrs]
        for cp in mine:
            cp.start()
        first = [copy(0, a, me, sibling, src=xs[a]) for a in arrs]
        first += [copy(1 + j, a, me, (*chip, c), src=xs[a]) for j, chip in enumerate(chips) for a in arrs]
        for cp in first:
            cp.start()
        passed = []
        for j, chip in enumerate(chips):
            for a in arrs:
                copy(1 + j, a, (*chip, c), me).wait_recv()
                passed.append(copy(4 + j, a, (*chip, c), sibling))
                passed[-1].start()
        for a in arrs:
            copy(0, a, sibling, me).wait_recv()
        for j, chip in enumerate(chips):
            for a in arrs:
                copy(4 + j, a, (*chip, 1 - c), me).wait_recv()
        for cp in first + passed:
            cp.wait_send()
        for cp in mine:
            cp.wait()

    hbm = pl.BlockSpec(memory_space=pl.ANY)
    return pl.pallas_call(
        body, out_shape=out_shape, in_specs=[hbm] * n_arr, out_specs=[hbm] * n_arr,
        scratch_shapes=[pltpu.SemaphoreType.DMA((7, n_arr)), pltpu.SemaphoreType.DMA((7, n_arr)),
                        pltpu.SemaphoreType.DMA((n_arr,))],
        name=name)(*shards)


def _peer_copies(srcs, lands, send_sems, recv_sems, gather, by_rows):
    n_arr = len(srcs)
    x, y, c = _mesh_pos()
    my = 4 * x + 2 * y + c
    out = []
    for k in range(1, N_DEV):
        px, py, pc = x ^ (k >> 2), y ^ ((k >> 1) & 1), c ^ (k & 1)
        peer = 4 * px + 2 * py + pc
        for a in range(n_arr):
            src = srcs[a] if gather else _dev_block(srcs[a], peer, by_rows[a])
            dst = _dev_block(lands[a], my, by_rows[a]) if gather else lands[a].at[my]
            out.append(pltpu.make_async_remote_copy(
                src_ref=src, dst_ref=dst, send_sem=send_sems.at[(k - 1) * n_arr + a],
                recv_sem=recv_sems.at[(k - 1) * n_arr + a], device_id=(px, py, pc), device_id_type=pl.DeviceIdType.MESH))
    return out


def _land_shape(s, gather, by_rows):
    if gather:
        return (s.shape[0], N_DEV * s.shape[1], s.shape[2]) if by_rows else (N_DEV,) + s.shape
    return (N_DEV, s.shape[0], s.shape[1] // N_DEV, s.shape[2]) if by_rows else s.shape


_HBM = pl.BlockSpec(memory_space=pltpu.HBM)
_SEM = pl.BlockSpec(memory_space=pltpu.SEMAPHORE)


def _xfer_start(srcs, gather, by_rows, name, dep=None):
    n = len(srcs)
    lands = [lax.empty(_land_shape(s, gather, br), s.dtype) for s, br in zip(srcs, by_rows)]
    ins = [pltpu.with_memory_space_constraint(a, pltpu.HBM) for a in list(srcs) + lands]
    dspec, darg = _dep(dep)

    def body(*refs):
        s = 2 * n + len(darg)
        for cp in _peer_copies(refs[:n], refs[n:2 * n], refs[s], refs[s + 1], gather, by_rows):
            cp.start()
        for cp in _own_copies(refs[:n], refs[n:2 * n], refs[s + 2], gather, by_rows):
            cp.start()
        refs[-1][...] = jnp.zeros_like(refs[-1])

    sems = pltpu.SemaphoreType.DMA(((N_DEV - 1) * n,))
    outs = pl.pallas_call(
        body, name=name,
        out_shape=(sems, sems, pltpu.SemaphoreType.DMA((n,)), *[pltpu.HBM(a.shape, a.dtype) for a in ins],
                   _sds((8, LANES), F32)),
        in_specs=[_HBM] * (2 * n) + dspec,
        out_specs=(_SEM, _SEM, _SEM, *[_HBM] * (2 * n), pl.BlockSpec(memory_space=pltpu.VMEM)),
        input_output_aliases={i: 3 + i for i in range(2 * n)},
        compiler_params=pltpu.CompilerParams(has_side_effects=pltpu.SideEffectType.DATAFLOW_SIDE_EFFECTING))(*ins, *darg)
    return outs[0], outs[1], list(outs[3:-1]), outs[-1], outs[2]


def _own_copies(srcs, lands, sems, gather, by_rows):
    x, y, c = _mesh_pos()
    my = 4 * x + 2 * y + c
    out = []
    for a in range(len(srcs)):
        if gather:
            out.append(pltpu.make_async_copy(srcs[a], _dev_block(lands[a], my, by_rows[a]), sems.at[a]))
        else:
            out.append(pltpu.make_async_copy(_dev_block(srcs[a], my, by_rows[a]), lands[a].at[my], sems.at[a]))
    return out


def _xfer_wait(started, after, gather, by_rows, name):
    send_sems, recv_sems, bufs, _, local_sems = started
    n = len(bufs) // 2

    def body(*refs):
        for cp in _peer_copies(refs[:n], refs[n:2 * n], refs[2 * n], refs[2 * n + 1], gather, by_rows):
            cp.wait_send()
            cp.wait_recv()
        for cp in _own_copies(refs[:n], refs[n:2 * n], refs[2 * n + 2], gather, by_rows):
            cp.wait()

    outs = pl.pallas_call(
        body, name=name, out_shape=tuple(pltpu.HBM(a.shape, a.dtype) for a in bufs),
        in_specs=[_HBM] * (2 * n) + [_SEM, _SEM, _SEM, pl.BlockSpec(memory_space=pl.ANY)],
        out_specs=tuple([_HBM] * (2 * n)), input_output_aliases={i: i for i in range(2 * n)},
        compiler_params=pltpu.CompilerParams(has_side_effects=pltpu.SideEffectType.DATAFLOW_SIDE_EFFECTING))(
            *bufs, send_sems, recv_sems, local_sems, after)
    return list(outs[n:])


def _adam_update(g, w, m, v):
    c1 = 1.0 - ADAM_B1 ** ADAM_STEP
    c2 = 1.0 - ADAM_B2 ** ADAM_STEP
    nm = ADAM_B1 * m + (1.0 - ADAM_B1) * g
    nv = ADAM_B2 * v + (1.0 - ADAM_B2) * (g * g)
    return -ADAM_LR * ((nm / c1) / (jnp.sqrt(nv / c2) + ADAM_EPS) + ADAM_WD * w), nm, nv


def _adamw_body(p_ref, w_ref, m_ref, v_ref, g_ref, d_ref, nm_ref, nv_ref):
    g = p_ref[0]
    for i in range(1, N_DEV):
        g = g + p_ref[i]
    g_ref[...] = g
    d_ref[...], nm_ref[...], nv_ref[...] = _adam_update(g, w_ref[...], m_ref[...], v_ref[...])


def _adamw(parts, w, m, v, name, tr=1536):
    R = w.shape[0]
    tr = max(t for t in range(8, tr + 1, 8) if R % t == 0)

    def body(*refs):
        _adamw_body(*refs)

    row = pl.BlockSpec((tr, LANES), lambda i: (i, 0))
    return pl.pallas_call(
        body, grid=(R // tr,),
        in_specs=[pl.BlockSpec((N_DEV, tr, LANES), lambda i: (0, i, 0)), row, row, row],
        out_specs=[row, row, row, row], out_shape=[_sds((R, LANES), F32)] * 4,
        name=name, compiler_params=_cp(("parallel",)))(parts, w, m, v)


def _adamw_split(recvs, w, m, v, name, tr):
    depth, r, c = w.shape
    assert depth == len(recvs)
    tr = _tile(r, tr)

    def body(*refs):
        layer = pl.program_id(0)
        for ll in range(depth):
            @pl.when(layer == ll)
            def _(ll=ll):
                _adamw_body(refs[ll], *refs[depth:])

    wspec = pl.BlockSpec((None, tr, c), lambda l, i: (l, i, 0))
    rspecs = [pl.BlockSpec((N_DEV, None, tr, c), lambda l, i, ll=ll, t=t: (0, t, jnp.where(l == ll, i, 0), 0))
              for ll, (_, t) in enumerate(recvs)]
    return pl.pallas_call(
        body, grid=(depth, r // tr), in_specs=rspecs + [wspec, wspec, wspec],
        out_specs=[wspec] * 4, out_shape=[_sds(w.shape, F32)] * 4,
        name=name, compiler_params=_cp(("arbitrary", "arbitrary")))(*[a for a, _ in recvs], w, m, v)


def _merge_win(g, name, tr=256):
    _, nt, K, n = g.shape
    tr = _tile(K, tr)

    def body(g_ref, q_ref, r_ref):
        full = jnp.concatenate([g_ref[j] for j in range(N_DEV)], axis=1)
        q_ref[...] = full[:, 256:1792]
        zpad = jnp.zeros((tr, REST_W - 776), full.dtype)
        r_ref[...] = jnp.concatenate([full[:, 0:256], full[:, 1800:2312], full[:, 1792:1800], zpad], axis=1)

    return pl.pallas_call(
        body, grid=(nt, K // tr),
        in_specs=[pl.BlockSpec((N_DEV, None, tr, n), lambda t, i: (0, t, i, 0))],
        out_specs=[pl.BlockSpec((None, tr, 1536), lambda t, i: (t, i, 0)), pl.BlockSpec((None, tr, REST_W), lambda t, i: (t, i, 0))],
        out_shape=[_sds((nt, K, 1536), g.dtype), _sds((nt, K, REST_W), g.dtype)],
        name=name, compiler_params=_cp(("parallel", "parallel")))(g)


def _split_win(dq, dr, name, tr=256):
    K = dq.shape[0]
    tr = _tile(K, tr)
    n = (dq.shape[1] + 776) // N_DEV

    def body(q_ref, r_ref, o_ref):
        r = r_ref[...]
        full = jnp.concatenate([r[:, 0:256], q_ref[...], r[:, 768:776], r[:, 256:768]], axis=1)
        for j in range(N_DEV):
            o_ref[j] = full[:, n * j:n * (j + 1)]

    return pl.pallas_call(
        body, grid=(K // tr,),
        in_specs=[pl.BlockSpec((tr, dq.shape[1]), lambda i: (i, 0)), pl.BlockSpec((tr, REST_W), lambda i: (i, 0))],
        out_specs=pl.BlockSpec((N_DEV, tr, n), lambda i: (0, i, 0)),
        out_shape=_sds((N_DEV, K, n), F32), name=name, compiler_params=_cp(("parallel",)))(dq, dr)


WEIGHTS = ["ffn1_norm", "ffn1_w_gate", "ffn1_w_up", "ffn1_w_down", "mix_norm", "w_in", "pool_w", "pool_scale",
           "forget_bias", "conv_w", "conv_b", "conv_ln_g", "conv_ln_b", "w_out", "ffn2_norm", "ffn2_w_gate",
           "ffn2_w_up", "ffn2_w_down", "final_norm"]
FFN_PARTS = ("w_gate", "w_up", "w_down")
FFN_T = ["ffn1_w_gate", "ffn1_w_up", "ffn2_w_gate", "ffn2_w_up"]
BIG = FFN_T + ["ffn1_w_down", "ffn2_w_down", "w_in", "w_out"]
SMALL = [n for n in WEIGHTS if n not in BIG]


def _padded(n):
    return -(-n // PACK_ALIGN) * PACK_ALIGN


def _flat_pad(a):
    f = a.reshape(-1)
    return jnp.pad(f, (0, _padded(f.shape[0]) - f.shape[0]))


def _split8(a, axis):
    shp = a.shape
    a = a.reshape(shp[:axis] + (N_DEV, shp[axis] // N_DEV) + shp[axis + 1:])
    return jnp.moveaxis(a, axis, 0)


def _merge8(a, axis):
    a = jnp.moveaxis(a, 0, axis)
    shp = a.shape
    return a.reshape(shp[:axis] + (shp[axis] * shp[axis + 1],) + shp[axis + 2:])


def _pack_small(arrs):
    return jnp.concatenate([_flat_pad(arrs[n]) for n in SMALL]).reshape(-1, LANES)


def _pack_small_parts(grads):
    cols = []
    for n in SMALL:
        g = grads[n]
        if n == "conv_w":
            s = _split8(g, 2).reshape(N_DEV, -1)
        else:
            s = jnp.broadcast_to(g.reshape(1, -1), (N_DEV, g.size))
        cols.append(jnp.pad(s, ((0, 0), (0, _padded(s.shape[1]) - s.shape[1]))))
    return jnp.concatenate(cols, axis=1).reshape(N_DEV, -1, LANES)


def _unpack_small(buf, like):
    flat = buf.reshape(-1)
    out, off = {}, 0
    for n in SMALL:
        size = like[n].size
        out[n] = flat[off:off + size].reshape(like[n].shape)
        off += _padded(size)
    return out


class _Comm:
    def __init__(self, w):
        self.w = w
        self.bf = {n: (jnp.swapaxes(w[n], 1, 2) if n in FFN_T else w[n]).astype(CDT) for n in BIG}
        self.ready = {}
        self.grads = {}

    def _ffn_shards(self, l, which):
        return jnp.stack([self.bf[f"{which}_{k}"][l] for k in FFN_PARTS])

    def _put_ffn(self, l, which, rows, t):
        self.ready[(l, which)] = dict(w_gate=rows[t], w_up=rows[t + 1], w_down=rows[t + 2])

    def weights_for(self, l, stage, x):
        bf = self.bf
        dep = None
        if (l, stage) == (0, "ffn1"):
            gd, = _all_gather([self._ffn_shards(0, "ffn1")[0:2]], [True], "gather_l0_ffn1")
            self.ready[(0, "ffn1")] = dict(w_gate=gd[0], w_up=gd[1])
            self.started = _xfer_start([bf["ffn1_w_down"][0:1], bf["w_in"][0:1], bf["w_out"][0:1], self.w["conv_w"]], True,
                                       [True, False, True, False], "gather_mix0_start", dep=gd)
            dep = self.started[3]
        elif (l, stage) == (0, "ffn1_down"):
            gdn, gi, go, gc = _xfer_wait(self.started, x, True, [True, False, True, False], "gather_mix0_wait")
            self.ready[(0, "ffn1_down")] = dict(w_down=gdn[0])
            q, r = _merge_win(gi, "merge_l0_w_in")
            self.conv_w = _merge8(gc, 2)
            self.ready[(0, "mix")] = dict(win_qkv=q[0], win_rest=r[0], w_out=go[0], conv_w=self.conv_w[0])
            rows = jnp.concatenate([self._ffn_shards(0, "ffn2"), self._ffn_shards(1, "ffn1"), self._ffn_shards(1, "ffn2")])
            self.started = _xfer_start([rows, bf["w_in"][1:2], bf["w_out"][1:2]], True, [True, False, True],
                                       "gather_rest_start")
            dep = self.started[3]
        elif (l, stage) == (0, "ffn2"):
            gd, gi, go = _xfer_wait(self.started, x, True, [True, False, True], "gather_rest_wait")
            self._put_ffn(0, "ffn2", gd, 0)
            self._put_ffn(1, "ffn1", gd, 3)
            self._put_ffn(1, "ffn2", gd, 6)
            q, r = _merge_win(gi, "merge_l1_w_in")
            self.ready[(1, "mix")] = dict(win_qkv=q[0], win_rest=r[0], w_out=go[0], conv_w=self.conv_w[1])
        return self.ready[(l, stage)], dep

    def grads_ready(self, l, stage, grads):
        for n, v in grads.items():
            self.grads[(l, n)] = v
        gr = self.grads

        def ffn_rows(layer, which, parts=FFN_PARTS):
            return [gr[(layer, f"{which}_{k}")][None] for k in parts]

        if l == 1 and "ffn1_w_gate" in grads:
            self.sent1 = _xfer_start(
                ffn_rows(1, "ffn1") + ffn_rows(1, "ffn2") + [gr[(1, "w_in")][:, None], gr[(1, "w_out")][None]],
                False, [True] * 6 + [False, True], "grads_l1_start")
            return self.sent1[3]
        if l == 0 and "ffn2_w_gate" in grads:
            self.sent_ffn2 = _xfer_start(ffn_rows(0, "ffn2"), False, [True] * 3, "grads_l0_ffn2_start")
            return self.sent_ffn2[3]
        if (l, stage) == (0, "mix"):
            self.sent_mix = _xfer_start([gr[(0, "w_in")][:, None], gr[(0, "w_out")][None]], False, [False, True],
                                        "grads_l0_mix_start")
            return self.sent_mix[3]
        if l == 0 and "ffn1_w_down" in grads:
            self.sent_down = _xfer_start([gr[(0, "ffn1_w_down")][None]], False, [True], "grads_l0_ffn1_down_start")
            return self.sent_down[3]
        if l == 0 and "ffn1_w_gate" in grads:
            self.sent_gu = _xfer_start(ffn_rows(0, "ffn1", FFN_PARTS[:2]), False, [True] * 2,
                                       "grads_l0_ffn1_gate_up_start")
            return self.sent_gu[3]
        return None

    def finish(self, m, v, after):
        w, gr = self.w, self.grads
        depth = range(w["w_in"].shape[0])
        small = {n: (gr[(None, n)] if n == "final_norm" else jnp.stack([gr[(l, n)] for l in depth])) for n in SMALL}
        sent_small = _xfer_start([_pack_small_parts(small)], False, [False], "grads_small_start")
        *r1, i1, o1 = _xfer_wait(self.sent1, after, False, [True] * 6 + [False, True], "grads_l1_wait")
        r2 = _xfer_wait(self.sent_ffn2, after, False, [True] * 3, "grads_l0_ffn2_wait")
        i0, o0 = _xfer_wait(self.sent_mix, after, False, [False, True], "grads_l0_mix_wait")

        def adam(n, recvs, tr):
            if n in FFN_T:
                out = _adamw_split(recvs, *[jnp.swapaxes(t[n], 1, 2) for t in (w, m, v)], f"adamw_{n}", tr)
                return [jnp.swapaxes(o, 1, 2) for o in out]
            return _adamw_split(recvs, w[n], m[n], v[n], f"adamw_{n}", tr)

        res = {}
        for t, k in enumerate(FFN_PARTS):
            res[f"ffn2_{k}"] = adam(f"ffn2_{k}", [(r2[t], 0), (r1[3 + t], 0)], 176)
        res["w_in"] = adam("w_in", [(i0, 0), (i1, 0)], 256)
        res["w_out"] = adam("w_out", [(o0, 0), (o1, 0)], 128)
        r0, = _xfer_wait(self.sent_down, res["w_out"][0], False, [True], "grads_l0_ffn1_down_wait")
        res["ffn1_w_down"] = adam("ffn1_w_down", [(r0, 0), (r1[2], 0)], 176)
        g0 = _xfer_wait(self.sent_gu, res["ffn1_w_down"][0], False, [True] * 2, "grads_l0_ffn1_gate_up_wait")
        res["ffn1_w_gate"] = adam("ffn1_w_gate", [(g0[0], 0), (r1[0], 0)], 176)
        res["ffn1_w_up"] = adam("ffn1_w_up", [(g0[1], 0), (r1[1], 0)], 176)
        rs, = _xfer_wait(sent_small, res["ffn1_w_up"][0], False, [False], "grads_small_wait")
        packed = _adamw(rs, _pack_small(w), _pack_small(m), _pack_small(v), "adamw_small")
        unpacked = [_unpack_small(b, w) for b in packed]
        for n in SMALL:
            res[n] = [u[n] for u in unpacked]
        return res


def kernel(x, ffn1_norm, ffn1_w_gate, ffn1_w_up, ffn1_w_down, mix_norm, w_in, pool_w, pool_scale, forget_bias, conv_w, conv_b, conv_ln_g, conv_ln_b, w_out, ffn2_norm, ffn2_w_gate, ffn2_w_up, ffn2_w_down, final_norm, loss_target, m_ffn1_norm, m_ffn1_w_gate, m_ffn1_w_up, m_ffn1_w_down, m_mix_norm, m_w_in, m_pool_w, m_pool_scale, m_forget_bias, m_conv_w, m_conv_b, m_conv_ln_g, m_conv_ln_b, m_w_out, m_ffn2_norm, m_ffn2_w_gate, m_ffn2_w_up, m_ffn2_w_down, m_final_norm, v_ffn1_norm, v_ffn1_w_gate, v_ffn1_w_up, v_ffn1_w_down, v_mix_norm, v_w_in, v_pool_w, v_pool_scale, v_forget_bias, v_conv_w, v_conv_b, v_conv_ln_g, v_conv_ln_b, v_w_out, v_ffn2_norm, v_ffn2_w_gate, v_ffn2_w_up, v_ffn2_w_down, v_final_norm):
    w = dict(zip(WEIGHTS, (ffn1_norm, ffn1_w_gate, ffn1_w_up, ffn1_w_down, mix_norm, w_in, pool_w, pool_scale, forget_bias,
                           conv_w, conv_b, conv_ln_g, conv_ln_b, w_out, ffn2_norm, ffn2_w_gate, ffn2_w_up, ffn2_w_down,
                           final_norm)))
    m = dict(zip(WEIGHTS, (m_ffn1_norm, m_ffn1_w_gate, m_ffn1_w_up, m_ffn1_w_down, m_mix_norm, m_w_in, m_pool_w, m_pool_scale,
                           m_forget_bias, m_conv_w, m_conv_b, m_conv_ln_g, m_conv_ln_b, m_w_out, m_ffn2_norm, m_ffn2_w_gate,
                           m_ffn2_w_up, m_ffn2_w_down, m_final_norm)))
    v = dict(zip(WEIGHTS, (v_ffn1_norm, v_ffn1_w_gate, v_ffn1_w_up, v_ffn1_w_down, v_mix_norm, v_w_in, v_pool_w, v_pool_scale,
                           v_forget_bias, v_conv_w, v_conv_b, v_conv_ln_g, v_conv_ln_b, v_w_out, v_ffn2_norm, v_ffn2_w_gate,
                           v_ffn2_w_up, v_ffn2_w_down, v_final_norm)))
    comm = _Comm(w)
    loss_row, gx = _local_step(x[0], loss_target[0], w, comm.weights_for, comm.grads_ready)
    loss = lax.psum(loss_row[0, 0], ("x", "y", "c"))
    res = comm.finish(m, v, gx)
    return (loss, gx[None], *[res[n][i] for i in range(4) for n in WEIGHTS])
```

```python
import math

import numpy as np
import jax
import jax.numpy as jnp
from jax import lax
from jax.experimental import pallas as pl
from jax.experimental.pallas import tpu as pltpu

F32 = jnp.float32
CDT = jnp.bfloat16
WIRE = jnp.bfloat16
NORM_EPS = 1e-6
N_DEV = 8
LANES = 128
PACK_ALIGN = 8 * LANES
VMEM_LIMIT = 48 * 1024 * 1024

POOL_WINDOWS = (2, 4, 8, 16)
POOL_HALO = 16
CONV_K = 31
CONV_HALO = 32
HEAD_DIM = 64
N_HEADS = 8
N_PAIRS = N_HEADS // 2
ATT_SCALE = 1.0 / math.sqrt(HEAD_DIM)
NEG = -1e30

ADAM_LR, ADAM_B1, ADAM_B2, ADAM_EPS, ADAM_WD, ADAM_STEP = 0.001, 0.9, 0.999, 1e-08, 0.01, 10

REST_W = 896
REST_Z_BLK = 6


def _cp(sem):
    return pltpu.CompilerParams(dimension_semantics=sem, vmem_limit_bytes=VMEM_LIMIT)


def _tile(n, pref):
    t = min(n, pref)
    assert n % t == 0, (n, pref)
    return t


def _sigmoid(x):
    return 1.0 / (1.0 + jnp.exp(-x))


def _sds(shape, dtype):
    return jax.ShapeDtypeStruct(shape, dtype)


_ANY = pl.BlockSpec(memory_space=pl.ANY)


def _dep(dep):
    return ([], []) if dep is None else ([_ANY], [dep])


def _wshape(w):
    return w[0].shape[1:] if isinstance(w, tuple) else w.shape


def _wspec(w, block, index_map):
    if not isinstance(w, tuple):
        return w, pl.BlockSpec(block, index_map)
    arr, t = w
    return arr, pl.BlockSpec((None,) + block, lambda *g: (t,) + index_map(*g))


def _rms_fwd(x, g, name, dep=None):
    T, D = x.shape
    tm = _tile(T, 1024)

    def body(x_ref, g_ref, *rest):
        o_ref = rest[-1]
        xv = x_ref[...]
        r = lax.rsqrt(jnp.mean(xv * xv, axis=-1, keepdims=True) + NORM_EPS)
        o_ref[...] = (xv * r * g_ref[...]).astype(o_ref.dtype)

    dspec, darg = _dep(dep)
    return pl.pallas_call(
        body, grid=(T // tm,),
        in_specs=[pl.BlockSpec((tm, D), lambda i: (i, 0)), pl.BlockSpec((1, D), lambda i: (0, 0))] + dspec,
        out_specs=pl.BlockSpec((tm, D), lambda i: (i, 0)),
        out_shape=_sds((T, D), CDT), name=name, compiler_params=_cp(("parallel",)))(x, g, *darg)


def _rms_bwd(x, g, dh, gres, name):
    T, D = x.shape
    tm = _tile(T, 512)

    def body(x_ref, g_ref, dh_ref, gres_ref, gin_ref, dg_ref):
        i = pl.program_id(0)
        xv = x_ref[...]
        d = dh_ref[...]
        r = lax.rsqrt(jnp.mean(xv * xv, axis=-1, keepdims=True) + NORM_EPS)
        xh = xv * r
        dxh = d * g_ref[...]
        c = jnp.mean(dxh * xh, axis=-1, keepdims=True)
        gin_ref[...] = gres_ref[...] + r * (dxh - xh * c)
        part = jnp.sum(d * xh, axis=0, keepdims=True)

        @pl.when(i == 0)
        def _():
            dg_ref[...] = part

        @pl.when(i > 0)
        def _():
            dg_ref[...] += part

    row = pl.BlockSpec((tm, D), lambda i: (i, 0))
    vec = pl.BlockSpec((1, D), lambda i: (0, 0))
    return pl.pallas_call(
        body, grid=(T // tm,), in_specs=[row, vec, row, row], out_specs=[row, vec],
        out_shape=[_sds((T, D), F32), _sds((1, D), F32)], name=name, compiler_params=_cp(("arbitrary",)))(x, g, dh, gres)


def _loss_bwd(x, g, target, name):
    T, D = x.shape
    tm = _tile(T, 512)

    def body(x_ref, g_ref, t_ref, loss_ref, dx_ref, dg_ref):
        i = pl.program_id(0)
        xv = x_ref[...]
        gv = g_ref[...]
        r = lax.rsqrt(jnp.mean(xv * xv, axis=-1, keepdims=True) + NORM_EPS)
        xh = xv * r
        err = xh * gv - t_ref[...]
        lpart = 0.5 * jnp.sum(jnp.mean(err * err, axis=-1, keepdims=True), axis=0, keepdims=True)
        dy = err * (1.0 / D)
        dxh = dy * gv
        c = jnp.mean(dxh * xh, axis=-1, keepdims=True)
        dx_ref[...] = r * (dxh - xh * c)
        part = jnp.sum(dy * xh, axis=0, keepdims=True)
        lrow = jnp.broadcast_to(lpart, (1, LANES))

        @pl.when(i == 0)
        def _():
            dg_ref[...] = part
            loss_ref[...] = lrow

        @pl.when(i > 0)
        def _():
            dg_ref[...] += part
            loss_ref[...] += lrow

    row = pl.BlockSpec((tm, D), lambda i: (i, 0))
    vec = pl.BlockSpec((1, D), lambda i: (0, 0))
    return pl.pallas_call(
        body, grid=(T // tm,), in_specs=[row, vec, row],
        out_specs=[pl.BlockSpec((1, LANES), lambda i: (0, 0)), row, vec],
        out_shape=[_sds((1, LANES), F32), _sds((T, D), F32), _sds((1, D), F32)],
        name=name, compiler_params=_cp(("arbitrary",)))(x, g, target)


def _mm(pairs, *, name, res=None, alpha=1.0, out_dtype=F32, tm=512, tn=None, dep=None):
    T = pairs[0][0].shape[0]
    N = _wshape(pairs[0][1])[0 if pairs[0][2] else 1]
    tm = _tile(T, tm)
    tn = N if tn is None else _tile(N, tn)
    flags = [p[2] for p in pairs]
    n_in = 2 * len(pairs)

    def body(*refs):
        o_ref = refs[-1]
        acc = None
        for p, bt in enumerate(flags):
            a = refs[2 * p][...].astype(CDT)
            b = refs[2 * p + 1][...]
            dims = (((1,), (1,)), ((), ())) if bt else (((1,), (0,)), ((), ()))
            d = lax.dot_general(a, b, dims, preferred_element_type=F32)
            acc = d if acc is None else acc + d
        if alpha != 1.0:
            acc = acc * alpha
        if res is not None:
            acc = refs[n_in][...] + acc
        o_ref[...] = acc.astype(o_ref.dtype)

    in_specs, args = [], []
    for a, b, bt in pairs:
        K = a.shape[1]
        in_specs.append(pl.BlockSpec((tm, K), lambda i, j: (i, 0)))
        b, bspec = _wspec(b, (tn, K), lambda i, j: (j, 0)) if bt else _wspec(b, (K, tn), lambda i, j: (0, j))
        in_specs.append(bspec)
        args += [a, b]
    if res is not None:
        in_specs.append(pl.BlockSpec((tm, tn), lambda i, j: (i, j)))
        args.append(res)
    dspec, darg = _dep(dep)
    in_specs += dspec
    args += darg
    return pl.pallas_call(
        body, grid=(T // tm, N // tn), in_specs=in_specs,
        out_specs=pl.BlockSpec((tm, tn), lambda i, j: (i, j)),
        out_shape=_sds((T, N), out_dtype), name=name, compiler_params=_cp(("parallel", "arbitrary")))(*args)


def _mm_norm_bwd(pairs, x, g, gres, *, name, tm=256, dep=None):
    T, D = x.shape
    tm = _tile(T, tm)
    n_in = 2 * len(pairs)
    flags = [p[2] for p in pairs]

    def body(*refs):
        x_ref, g_ref, gres_ref = refs[n_in:n_in + 3]
        gin_ref, dg_ref = refs[-2:]
        i = pl.program_id(0)
        d = None
        for p, bt in enumerate(flags):
            dims = (((1,), (1,)), ((), ())) if bt else (((1,), (0,)), ((), ()))
            part = lax.dot_general(refs[2 * p][...].astype(CDT), refs[2 * p + 1][...], dims, preferred_element_type=F32)
            d = part if d is None else d + part
        xv = x_ref[...]
        r = lax.rsqrt(jnp.mean(xv * xv, axis=-1, keepdims=True) + NORM_EPS)
        xh = xv * r
        dxh = d * g_ref[...]
        c = jnp.mean(dxh * xh, axis=-1, keepdims=True)
        gin_ref[...] = gres_ref[...] + r * (dxh - xh * c)
        part = jnp.sum(d * xh, axis=0, keepdims=True)

        @pl.when(i == 0)
        def _():
            dg_ref[...] = part

        @pl.when(i > 0)
        def _():
            dg_ref[...] += part

    in_specs, args = [], []
    for a, b, bt in pairs:
        K = a.shape[1]
        b, bspec = _wspec(b, tuple(_wshape(b)), lambda i: (0, 0))
        in_specs += [pl.BlockSpec((tm, K), lambda i: (i, 0)), bspec]
        args += [a, b]
    row = pl.BlockSpec((tm, D), lambda i: (i, 0))
    vec = pl.BlockSpec((1, D), lambda i: (0, 0))
    dspec, darg = _dep(dep)
    return pl.pallas_call(
        body, grid=(T // tm,), in_specs=in_specs + [row, vec, row] + dspec, out_specs=[row, vec],
        out_shape=[_sds((T, D), F32), _sds((1, D), F32)], name=name,
        compiler_params=_cp(("arbitrary",)))(*args, x, g, gres, *darg)


def _mm_tn(a, b, *, name, alpha=1.0, tk=2048, dep=None, out_dtype=F32):
    T, M = a.shape
    N = b.shape[1]
    tm = M if M <= 1024 else M // 2
    tn = N if N <= 1536 else N // 2
    assert M % tm == 0 and N % tn == 0 and tm % LANES == 0 and tn % LANES == 0
    tk = _tile(T, tk)
    nk = T // tk
    direct = out_dtype == F32

    def body(a_ref, b_ref, *rest):
        o_ref = rest[-1] if direct else rest[-2]
        acc = o_ref if direct else rest[-1]
        k = pl.program_id(2)
        d = lax.dot_general(a_ref[...].astype(CDT), b_ref[...].astype(CDT), (((0,), (0,)), ((), ())),
                            preferred_element_type=F32)

        @pl.when(k == 0)
        def _():
            acc[...] = d

        @pl.when(k > 0)
        def _():
            acc[...] += d

        if direct:
            if alpha != 1.0:
                @pl.when(k == nk - 1)
                def _():
                    o_ref[...] *= alpha
        else:
            @pl.when(k == nk - 1)
            def _():
                o_ref[...] = (acc[...] * alpha).astype(o_ref.dtype)

    dspec, darg = _dep(dep)
    return pl.pallas_call(
        body, grid=(M // tm, N // tn, nk),
        in_specs=[pl.BlockSpec((tk, tm), lambda i, j, k: (k, i)), pl.BlockSpec((tk, tn), lambda i, j, k: (k, j))] + dspec,
        out_specs=pl.BlockSpec((tm, tn), lambda i, j, k: (i, j)),
        out_shape=_sds((M, N), out_dtype), scratch_shapes=[] if direct else [pltpu.VMEM((tm, tn), F32)],
        name=name, compiler_params=_cp(("parallel", "parallel", "arbitrary")))(a, b, *darg)


def _ffn_up(h, wgt, wut, name):
    T, D = h.shape
    Fh = _wshape(wgt)[0]
    tm = _tile(T, 4096)
    tn = _tile(Fh, 256)
    nt = (((1,), (1,)), ((), ()))

    def body(h_ref, wg_ref, wu_ref, a_ref, b_ref, s_ref):
        hv = h_ref[...]
        a = lax.dot_general(hv, wg_ref[...], nt, preferred_element_type=F32)
        b = lax.dot_general(hv, wu_ref[...], nt, preferred_element_type=F32)
        a_ref[...] = a.astype(a_ref.dtype)
        b_ref[...] = b.astype(b_ref.dtype)
        s_ref[...] = (a * _sigmoid(a) * b).astype(s_ref.dtype)

    wgt, gspec = _wspec(wgt, (tn, D), lambda i, j: (j, 0))
    wut, uspec = _wspec(wut, (tn, D), lambda i, j: (j, 0))
    ospec = pl.BlockSpec((tm, tn), lambda i, j: (i, j))
    return pl.pallas_call(
        body, grid=(T // tm, Fh // tn),
        in_specs=[pl.BlockSpec((tm, D), lambda i, j: (i, 0)), gspec, uspec],
        out_specs=[ospec, ospec, ospec],
        out_shape=[_sds((T, Fh), CDT), _sds((T, Fh), CDT), _sds((T, Fh), CDT)],
        name=name, compiler_params=_cp(("parallel", "arbitrary")))(h, wgt, wut)


def _ffn_bwd_ds(gout, wd, a, b, name, dep=None):
    T, D = gout.shape
    Fh = _wshape(wd)[0]
    tm = _tile(T, 2048)
    tn = _tile(Fh, 256)

    def body(g_ref, wd_ref, a_ref, b_ref, *rest):
        da_ref, db_ref = rest[-2:]
        dy = (0.5 * g_ref[...]).astype(CDT)
        ds = lax.dot_general(dy, wd_ref[...], (((1,), (1,)), ((), ())), preferred_element_type=F32)
        av = a_ref[...].astype(F32)
        sg = _sigmoid(av)
        da_ref[...] = (ds * b_ref[...].astype(F32) * (sg * (1.0 + av * (1.0 - sg)))).astype(da_ref.dtype)
        db_ref[...] = (ds * (av * sg)).astype(db_ref.dtype)

    ospec = pl.BlockSpec((tm, tn), lambda i, j: (i, j))
    dspec, darg = _dep(dep)
    wd, wspec = _wspec(wd, (tn, D), lambda i, j: (j, 0))
    return pl.pallas_call(
        body, grid=(T // tm, Fh // tn),
        in_specs=[pl.BlockSpec((tm, D), lambda i, j: (i, 0)), wspec, ospec, ospec] + dspec,
        out_specs=[ospec, ospec],
        out_shape=[_sds((T, Fh), CDT), _sds((T, Fh), CDT)],
        name=name, compiler_params=_cp(("parallel", "arbitrary")))(gout, wd, a, b, *darg)


def _ffn_fwd(x, gamma, wgt, wut, wd, tag, dep=None):
    h = _rms_fwd(x, gamma, f"{tag}_norm", dep)
    a, b, s = _ffn_up(h, wgt, wut, f"{tag}_up")
    dep = None
    if callable(wd):
        wd, dep = wd(a)
    y = _mm([(s, wd, False)], res=x, alpha=0.5, name=f"{tag}_down", dep=dep)
    return y, (x, h, a, b, s), wd


def _ffn_bwd(saved, gamma, wgt, wut, wd, gout, tag, dep, on_grads):
    x, h, a, b, s = saved
    dwd = _mm_tn(s, gout, alpha=0.5, name=f"{tag}_dwd", dep=dep, out_dtype=WIRE)
    da, db = _ffn_bwd_ds(gout, wd, a, b, f"{tag}_bwd_ds", on_grads(dict(w_down=dwd)))
    dwgt = _mm_tn(da, h, name=f"{tag}_dwg", out_dtype=WIRE)
    dwut = _mm_tn(db, h, name=f"{tag}_dwu", out_dtype=WIRE)
    dep = on_grads(dict(w_gate=dwgt, w_up=dwut))
    return _mm_norm_bwd([(da, wgt, False), (db, wut, False)], x, gamma, gout, name=f"{tag}_dh_norm_bwd", dep=dep)


def _fgate_fwd(rest, bias, name, bt=512):
    T = rest.shape[0]
    bt = _tile(T, bt)

    def body(z_ref, b_ref, fc_ref, ft_ref, carry):
        i = pl.program_id(0)

        @pl.when(i == 0)
        def _():
            carry[...] = jnp.zeros_like(carry)

        zb = z_ref[...] + b_ref[...]
        e = jnp.exp(-jnp.abs(zb))
        u = 1.0 + e
        log1p_e = jnp.where(u == 1.0, e, jnp.log(u) * (e / (u - 1.0)))
        x = jnp.minimum(zb, 0.0) - log1p_e
        row = lax.broadcasted_iota(jnp.int32, x.shape, 0)
        sh = 1
        while sh < bt:
            x = x + jnp.where(row >= sh, pltpu.roll(x, sh, 0), 0.0)
            sh *= 2
        f = x + carry[...]
        carry[...] = f[bt - 1:bt, :]
        fc_ref[...] = f
        ft_ref[...] = jnp.transpose(f)[0:N_HEADS, :]

    return pl.pallas_call(
        body, grid=(T // bt,),
        in_specs=[pl.BlockSpec((bt, LANES), lambda i: (i, REST_Z_BLK)), pl.BlockSpec((1, LANES), lambda i: (0, 0))],
        out_specs=[pl.BlockSpec((bt, LANES), lambda i: (i, 0)), pl.BlockSpec((N_HEADS, bt), lambda i: (0, i))],
        out_shape=[_sds((T, LANES), F32), _sds((N_HEADS, T), F32)],
        scratch_shapes=[pltpu.VMEM((1, LANES), F32)],
        name=name, compiler_params=_cp(("arbitrary",)))(rest, bias)


def _fgate_bwd(dfk, rest, bias, name, bt=512):
    T = rest.shape[0]
    bt = _tile(T, bt)
    nb = T // bt

    def body(df_ref, z_ref, b_ref, dz_ref, db_ref, carry):
        i = pl.program_id(0)

        @pl.when(i == 0)
        def _():
            carry[...] = jnp.zeros_like(carry)

        dfv = df_ref[...]
        lane = lax.broadcasted_iota(jnp.int32, (bt, LANES), 1)
        x = jnp.zeros((bt, LANES), F32)
        for h in range(N_HEADS):
            x = jnp.where(lane == h, dfv[:, HEAD_DIM * h:HEAD_DIM * h + 1], x)
        row = lax.broadcasted_iota(jnp.int32, x.shape, 0)
        sh = 1
        while sh < bt:
            x = x + jnp.where(row + sh < bt, pltpu.roll(x, bt - sh, 0), 0.0)
            sh *= 2
        dlf = x + carry[...]
        carry[...] = dlf[0:1, :]
        zb = z_ref[...] + b_ref[...]
        dz = jnp.where(lane < N_HEADS, dlf * _sigmoid(-zb), 0.0)
        dz_ref[...] = dz.astype(dz_ref.dtype)
        part = jnp.sum(dz, axis=0, keepdims=True)

        @pl.when(i == 0)
        def _():
            db_ref[...] = part

        @pl.when(i > 0)
        def _():
            db_ref[...] += part

    return pl.pallas_call(
        body, grid=(nb,),
        in_specs=[pl.BlockSpec((bt, 4 * LANES), lambda i: (nb - 1 - i, 0)),
                  pl.BlockSpec((bt, LANES), lambda i: (nb - 1 - i, REST_Z_BLK)),
                  pl.BlockSpec((1, LANES), lambda i: (0, 0))],
        out_specs=[pl.BlockSpec((bt, LANES), lambda i: (nb - 1 - i, 0)), pl.BlockSpec((1, LANES), lambda i: (0, 0))],
        out_shape=[_sds((T, LANES), CDT), _sds((1, LANES), F32)],
        scratch_shapes=[pltpu.VMEM((1, LANES), F32)],
        name=name, compiler_params=_cp(("arbitrary",)))(dfk, rest, bias)


def _by_group(vals, lane):
    out = vals[-1]
    for g in range(len(vals) - 2, -1, -1):
        out = jnp.where(lane // 64 == g, vals[g], out)
    return out


def _pool_counts(t0, n, lane):
    t = t0 + lax.broadcasted_iota(jnp.int32, (n, 256), 0)
    return _by_group([jnp.minimum(t + 1, w) for w in POOL_WINDOWS], lane).astype(F32)


def _pooled(u, halo, i, bt):
    lane = lax.broadcasted_iota(jnp.int32, (bt, 256), 1)
    ext = jnp.concatenate([jnp.where(i > 0, halo, 0.0), u], axis=0)
    sums, s, sh = [], ext, 1
    for _ in POOL_WINDOWS:
        s = s + pltpu.roll(s, sh, 0)
        sums.append(s[POOL_HALO:, :])
        sh *= 2
    return _by_group(sums, lane) / _pool_counts(i * bt, bt, lane) - u


def _pool_fwd(rest, wbd, scale, name, bt=512):
    T = rest.shape[0]
    bt = _tile(T, bt)
    hb = bt // POOL_HALO

    def body(u_ref, halo_ref, w_ref, sc_ref, o_ref):
        i = pl.program_id(0)
        pooled = _pooled(u_ref[...], halo_ref[...], i, bt)
        mixed = jnp.dot(pooled.astype(CDT), w_ref[...], preferred_element_type=F32)
        o_ref[...] = (mixed * sc_ref[...]).astype(o_ref.dtype)

    return pl.pallas_call(
        body, grid=(T // bt,),
        in_specs=[pl.BlockSpec((bt, 256), lambda i: (i, 0)),
                  pl.BlockSpec((POOL_HALO, 256), lambda i: (jnp.maximum(i * hb - 1, 0), 0)),
                  pl.BlockSpec((256, 256), lambda i: (0, 0)), pl.BlockSpec((1, 256), lambda i: (0, 0))],
        out_specs=pl.BlockSpec((bt, 256), lambda i: (i, 0)),
        out_shape=_sds((T, 256), CDT), name=name, compiler_params=_cp(("parallel",)))(rest, rest, wbd, scale)


def _pool_bwd(dcat, rest, wbd, scale, name, bt=512):
    T = rest.shape[0]
    bt = _tile(T, bt)
    hb = bt // POOL_HALO
    nb = T // bt
    n = bt + POOL_HALO

    def body(dy_ref, dyn_ref, u_ref, halo_ref, w_ref, sc_ref, du_ref, dw_ref, dsc_ref):
        i = pl.program_id(0)
        lane = lax.broadcasted_iota(jnp.int32, (bt, 256), 1)
        w = w_ref[...]
        sc = sc_ref[...]
        pooled = _pooled(u_ref[...], halo_ref[...], i, bt)
        pooled_c = pooled.astype(CDT)
        mixed = jnp.dot(pooled_c, w, preferred_element_type=F32)
        dy = dy_ref[...]
        dm = (dy * sc).astype(CDT)
        dsc = jnp.sum(dy * mixed, axis=0, keepdims=True)
        dw = lax.dot_general(pooled_c, dm, (((0,), (0,)), ((), ())), preferred_element_type=F32)
        nt = (((1,), (1,)), ((), ()))
        dpl = lax.dot_general(dm, w, nt, preferred_element_type=F32)
        dmn = (jnp.where(i < nb - 1, dyn_ref[...], 0.0) * sc).astype(CDT)
        dpln = lax.dot_general(dmn, w, nt, preferred_element_type=F32)
        lane_h = lax.broadcasted_iota(jnp.int32, (POOL_HALO, 256), 1)
        ext = jnp.concatenate([dpl / _pool_counts(i * bt, bt, lane),
                               dpln / _pool_counts((i + 1) * bt, POOL_HALO, lane_h)], axis=0)
        sums, s, sh = [], ext, 1
        for _ in POOL_WINDOWS:
            s = s + pltpu.roll(s, n - sh, 0)
            sums.append(s[0:bt, :])
            sh *= 2
        du_ref[...] = (_by_group(sums, lane) - dpl).astype(du_ref.dtype)

        @pl.when(i == 0)
        def _():
            dw_ref[...] = dw
            dsc_ref[...] = dsc

        @pl.when(i > 0)
        def _():
            dw_ref[...] += dw
            dsc_ref[...] += dsc

    full = pl.BlockSpec((256, 256), lambda i: (0, 0))
    vec = pl.BlockSpec((1, 256), lambda i: (0, 0))
    return pl.pallas_call(
        body, grid=(nb,),
        in_specs=[pl.BlockSpec((bt, 256), lambda i: (i, 0)),
                  pl.BlockSpec((POOL_HALO, 256), lambda i: (jnp.minimum((i + 1) * hb, nb * hb - 1), 0)),
                  pl.BlockSpec((bt, 256), lambda i: (i, 0)),
                  pl.BlockSpec((POOL_HALO, 256), lambda i: (jnp.maximum(i * hb - 1, 0), 0)),
                  full, vec],
        out_specs=[pl.BlockSpec((bt, 256), lambda i: (i, 0)), full, vec],
        out_shape=[_sds((T, 256), CDT), _sds((256, 256), F32), _sds((1, 256), F32)],
        name=name, compiler_params=_cp(("arbitrary",)))(dcat, dcat, rest, rest, wbd, scale)


def _glu_ext(a_ref, g_ref, ah_ref, gh_ref, i):
    u = a_ref[...] * _sigmoid(g_ref[...])
    uh = jnp.where(i > 0, ah_ref[...] * _sigmoid(gh_ref[...]), 0.0)
    return jnp.concatenate([uh, u], axis=0)


def _conv_fwd(rest, cw, cb, lg, lb, name, bt=512):
    T = rest.shape[0]
    bt = _tile(T, bt)
    hb = bt // CONV_HALO

    def body(a_ref, g_ref, ah_ref, gh_ref, cw_ref, cb_ref, lg_ref, lb_ref, o_ref, y_ref):
        i = pl.program_id(0)
        ext = _glu_ext(a_ref, g_ref, ah_ref, gh_ref, i)
        w = cw_ref[...]
        acc = w[CONV_K - 1:CONV_K, :] * ext
        for k in range(CONV_K - 1):
            acc = acc + w[k:k + 1, :] * pltpu.roll(ext, CONV_K - 1 - k, 0)
        y = acc[CONV_HALO:, :] + cb_ref[...]
        y_ref[...] = y
        yc = y - jnp.mean(y, axis=-1, keepdims=True)
        yn = yc * lax.rsqrt(jnp.mean(yc * yc, axis=-1, keepdims=True) + NORM_EPS)
        z = yn * lg_ref[...] + lb_ref[...]
        o_ref[...] = (z * _sigmoid(z)).astype(o_ref.dtype)

    def cur(c):
        return pl.BlockSpec((bt, 256), lambda i: (i, c))

    def prev(c):
        return pl.BlockSpec((CONV_HALO, 256), lambda i: (jnp.maximum(i * hb - 1, 0), c))

    vec = pl.BlockSpec((1, 256), lambda i: (0, 0))
    return pl.pallas_call(
        body, grid=(T // bt,),
        in_specs=[cur(1), cur(2), prev(1), prev(2), pl.BlockSpec((CONV_HALO, 256), lambda i: (0, 0)), vec, vec, vec],
        out_specs=[pl.BlockSpec((bt, 256), lambda i: (i, 0)), pl.BlockSpec((bt, 256), lambda i: (i, 0))],
        out_shape=[_sds((T, 256), CDT), _sds((T, 256), F32)],
        name=name, compiler_params=_cp(("parallel",)))(rest, rest, rest, rest, cw, cb, lg, lb)


def _conv_bwd(dcat, yconv, rest, cw, lg, lb, name, bt=512):
    T = rest.shape[0]
    bt = _tile(T, bt)
    hb = bt // CONV_HALO
    nb = T // bt
    n = bt + CONV_HALO

    def body(dy_ref, dyn_ref, y_ref, yn_ref, a_ref, g_ref, ah_ref, gh_ref, cw_ref, lg_ref, lb_ref,
             da_ref, dg_ref, dcw_ref, dcb_ref, dlg_ref, dlb_ref):
        i = pl.program_id(0)
        lgv = lg_ref[...]
        lbv = lb_ref[...]

        def ln_swish_bwd(dout, y):
            yc = y - jnp.mean(y, axis=-1, keepdims=True)
            rs = lax.rsqrt(jnp.mean(yc * yc, axis=-1, keepdims=True) + NORM_EPS)
            yn = yc * rs
            z = yn * lgv + lbv
            sg = _sigmoid(z)
            dz = dout * (sg * (1.0 + z * (1.0 - sg)))
            dyn = dz * lgv
            dyc = rs * (dyn - jnp.mean(dyn, axis=-1, keepdims=True) - yn * jnp.mean(dyn * yn, axis=-1, keepdims=True))
            return dyc, dz, yn

        dyc, dz, yn = ln_swish_bwd(dy_ref[...], y_ref[...])
        dyc_next, _, _ = ln_swish_bwd(dyn_ref[...], yn_ref[...])
        dyc_next = jnp.where(i < nb - 1, dyc_next, 0.0)
        ext_u = _glu_ext(a_ref, g_ref, ah_ref, gh_ref, i)
        ext_d = jnp.concatenate([dyc, dyc_next], axis=0)
        w = cw_ref[...]
        du = w[CONV_K - 1:CONV_K, :] * ext_d
        rows = []
        for k in range(CONV_K):
            s = CONV_K - 1 - k
            if s > 0:
                du = du + w[k:k + 1, :] * pltpu.roll(ext_d, n - s, 0)
                us = pltpu.roll(ext_u, s, 0)[CONV_HALO:, :]
            else:
                us = ext_u[CONV_HALO:, :]
            rows.append(jnp.sum(dyc * us, axis=0, keepdims=True))
        rows.append(jnp.zeros((1, 256), F32))
        dcw = jnp.concatenate(rows, axis=0)
        du = du[0:bt, :]
        av = a_ref[...]
        sg = _sigmoid(g_ref[...])
        da_ref[...] = (du * sg).astype(da_ref.dtype)
        dg_ref[...] = (du * av * (sg * (1.0 - sg))).astype(dg_ref.dtype)
        dcb = jnp.sum(dyc, axis=0, keepdims=True)
        dlg = jnp.sum(dz * yn, axis=0, keepdims=True)
        dlb = jnp.sum(dz, axis=0, keepdims=True)

        @pl.when(i == 0)
        def _():
            dcw_ref[...] = dcw
            dcb_ref[...] = dcb
            dlg_ref[...] = dlg
            dlb_ref[...] = dlb

        @pl.when(i > 0)
        def _():
            dcw_ref[...] += dcw
            dcb_ref[...] += dcb
            dlg_ref[...] += dlg
            dlb_ref[...] += dlb

    def cur(c):
        return pl.BlockSpec((bt, 256), lambda i: (i, c))

    def prev(c):
        return pl.BlockSpec((CONV_HALO, 256), lambda i: (jnp.maximum(i * hb - 1, 0), c))

    def nxt(c):
        return pl.BlockSpec((CONV_HALO, 256), lambda i: (jnp.minimum((i + 1) * hb, nb * hb - 1), c))

    vec = pl.BlockSpec((1, 256), lambda i: (0, 0))
    wfull = pl.BlockSpec((CONV_HALO, 256), lambda i: (0, 0))
    return pl.pallas_call(
        body, grid=(nb,),
        in_specs=[cur(3), nxt(3), cur(0), nxt(0), cur(1), cur(2), prev(1), prev(2), wfull, vec, vec],
        out_specs=[cur(0), cur(0), wfull, vec, vec, vec],
        out_shape=[_sds((T, 256), CDT), _sds((T, 256), CDT), _sds((CONV_HALO, 256), F32),
                   _sds((1, 256), F32), _sds((1, 256), F32), _sds((1, 256), F32)],
        name=name, compiler_params=_cp(("arbitrary",)))(dcat, dcat, yconv, yconv, rest, rest, rest, rest, cw, lg, lb)


def _half_mask(shape, a):
    lane = lax.broadcasted_iota(jnp.int32, shape, 1)
    return (lane // HEAD_DIM) == a


def _attn_fwd(qkv, fcol, frow, name, blk=1024):
    T = qkv.shape[0]
    blk = _tile(T, blk)
    nq = T // blk
    nt = (((1,), (1,)), ((), ()))

    def body(q_ref, k_ref, v_ref, fc_ref, fr_ref, o_ref, lse_ref):
        p_id = pl.program_id(0)
        i = pl.program_id(1)
        q2 = q_ref[...]
        fc = fc_ref[...]
        lane = lax.broadcasted_iota(jnp.int32, (blk, LANES), 1)
        tri = lax.broadcasted_iota(jnp.int32, (blk, blk), 1) <= lax.broadcasted_iota(jnp.int32, (blk, blk), 0)
        masks = [_half_mask(q2.shape, a) for a in range(2)]
        qs = [jnp.where(hm, q2, jnp.zeros_like(q2)) * ATT_SCALE for hm in masks]
        fqs = [jnp.sum(jnp.where(lane == 2 * p_id + a, fc, 0.0), axis=1, keepdims=True) for a in range(2)]

        def tile(j, carry, masked):
            cols = pl.ds(pl.multiple_of(j * blk, blk), blk)
            kj = k_ref[cols, :]
            vj = v_ref[cols, :]
            out = []
            for a in range(2):
                m, acc = carry[2 * a:2 * a + 2]
                va = jnp.where(masks[a], vj, jnp.ones_like(vj))
                s = lax.dot_general(qs[a], kj, nt, preferred_element_type=F32) + (fqs[a] - fr_ref[a:a + 1, cols])
                if masked:
                    s = jnp.where(tri, s, NEG)
                m_new = jnp.maximum(m, jnp.max(s, axis=1, keepdims=True))
                alpha = jnp.exp(m - m_new)
                pr = jnp.exp(s - m_new)
                hi = lax.bitcast_convert_type(lax.bitcast_convert_type(pr, jnp.uint32) & jnp.uint32(0xFFFF0000), F32)
                pv = (jnp.dot(hi.astype(CDT), va, preferred_element_type=F32)
                      + jnp.dot((pr - hi).astype(CDT), va, preferred_element_type=F32))
                out += [m_new, alpha * acc + pv]
            return tuple(out)

        init = (jnp.full((blk, 1), NEG, F32), jnp.zeros((blk, LANES), F32)) * 2
        carry = lax.fori_loop(0, i, lambda j, c: tile(j, c, False), init)
        carry = tile(i, carry, True)
        ls = [carry[1][:, HEAD_DIM:HEAD_DIM + 1], carry[3][:, 0:1]]
        lo = lane < HEAD_DIM
        o_ref[...] = jnp.where(lo, carry[1] / ls[0], carry[3] / ls[1])
        lse_t = jnp.transpose(jnp.where(lo, carry[0] + jnp.log(ls[0]), carry[2] + jnp.log(ls[1])))
        lse_ref[...] = jnp.concatenate([lse_t[0:1, :], lse_t[HEAD_DIM:HEAD_DIM + 1, :]], axis=0)

    return pl.pallas_call(
        body, grid=(N_PAIRS, nq),
        in_specs=[pl.BlockSpec((blk, LANES), lambda p, i: (i, p)),
                  pl.BlockSpec((T, LANES), lambda p, i: (0, N_PAIRS + p)),
                  pl.BlockSpec((T, LANES), lambda p, i: (0, 2 * N_PAIRS + p)),
                  pl.BlockSpec((blk, LANES), lambda p, i: (i, 0)),
                  pl.BlockSpec((None, 2, T), lambda p, i: (p, 0, 0))],
        out_specs=[pl.BlockSpec((blk, LANES), lambda p, i: (i, p)), pl.BlockSpec((None, 2, blk), lambda p, i: (p, 0, i))],
        out_shape=[_sds((T, N_PAIRS * LANES), F32), _sds((N_PAIRS, 2, T), F32)],
        name=name, compiler_params=_cp(("parallel", "arbitrary")))(qkv, qkv, qkv, fcol, frow)


def _attn_delta(dcat, o, name, blk=512):
    T = o.shape[0]
    blk = _tile(T, blk)

    def body(d_ref, o_ref, out_ref):
        prod = d_ref[:, 256:768].astype(CDT).astype(F32) * o_ref[...]
        pt = jnp.transpose(prod)
        out_ref[...] = jnp.sum(pt.reshape(N_HEADS, HEAD_DIM, blk), axis=1)

    return pl.pallas_call(
        body, grid=(T // blk,),
        in_specs=[pl.BlockSpec((blk, 1024), lambda i: (i, 0)), pl.BlockSpec((blk, 512), lambda i: (i, 0))],
        out_specs=pl.BlockSpec((N_HEADS, blk), lambda i: (0, i)),
        out_shape=_sds((N_HEADS, T), F32), name=name, compiler_params=_cp(("parallel",)))(dcat, o)


def _attn_bwd(qkv, dcat, fcol, frow, lse, delta, name, blk=1024):
    T = qkv.shape[0]
    blk = _tile(T, blk)
    nq = T // blk
    nt = (((1,), (1,)), ((), ()))

    def body(q_ref, do_ref, k_ref, v_ref, fc_ref, fr_ref, lse_ref, dl_ref, dqt_ref, dk_ref, dv_ref, df_ref):
        p_id = pl.program_id(0)
        j = pl.program_id(1)

        @pl.when(j == 0)
        def _():
            dqt_ref[...] = jnp.zeros_like(dqt_ref)

        k2 = k_ref[...]
        v2 = v_ref[...]
        fc = fc_ref[...]
        lane = lax.broadcasted_iota(jnp.int32, (blk, LANES), 1)
        tri = lax.broadcasted_iota(jnp.int32, (blk, blk), 0) <= lax.broadcasted_iota(jnp.int32, (blk, blk), 1)
        masks = [_half_mask(k2.shape, a) for a in range(2)]
        kas = [jnp.where(hm, k2, jnp.zeros_like(k2)) * ATT_SCALE for hm in masks]
        kats = [jnp.transpose(ka) for ka in kas]
        vas = [jnp.where(hm, v2, jnp.zeros_like(v2)) for hm in masks]
        fks = [jnp.sum(jnp.where(lane == 2 * p_id + a, fc, 0.0), axis=1, keepdims=True) for a in range(2)]

        def tile(i, carry, masked):
            rows = pl.ds(pl.multiple_of(i * blk, blk), blk)
            qi = q_ref[rows, :]
            doi = do_ref[rows, :].astype(CDT)
            out = []
            dqt = None
            for a in range(2):
                dk_acc, dv_acc, df_acc = carry[3 * a:3 * a + 3]
                st = lax.dot_general(kas[a], qi, nt, preferred_element_type=F32)
                e = (st + (fr_ref[a:a + 1, rows] - fks[a])) - lse_ref[a:a + 1, rows]
                if masked:
                    e = jnp.where(tri, e, NEG)
                pt = jnp.exp(e)
                dpt = lax.dot_general(vas[a], doi, nt, preferred_element_type=F32)
                ds32 = pt * (dpt - dl_ref[a:a + 1, rows])
                dst = ds32.astype(CDT)
                df_acc = df_acc + jnp.sum(ds32, axis=1, keepdims=True)
                dv_acc = dv_acc + jnp.dot(pt.astype(CDT), doi, preferred_element_type=F32)
                dk_acc = dk_acc + jnp.dot(dst, qi, preferred_element_type=F32)
                part = jnp.dot(kats[a], dst, preferred_element_type=F32)
                dqt = part if dqt is None else dqt + part
                out += [dk_acc, dv_acc, df_acc]
            dqt_ref[:, rows] += dqt
            return tuple(out)

        init = (jnp.zeros((blk, LANES), F32), jnp.zeros((blk, LANES), F32), jnp.zeros((blk, 1), F32)) * 2
        carry = tile(j, init, True)
        carry = lax.fori_loop(j + 1, nq, lambda i, c: tile(i, c, False), carry)
        lo = lane < HEAD_DIM
        dk_ref[...] = (jnp.where(lo, carry[0], carry[3]) * ATT_SCALE).astype(dk_ref.dtype)
        dv_ref[...] = jnp.where(lo, carry[1], carry[4]).astype(dv_ref.dtype)
        df_ref[...] = -jnp.where(lo, carry[2], carry[5])

    res = pl.BlockSpec((T, LANES), lambda p, j: (0, p))
    rows = pl.BlockSpec((None, 2, T), lambda p, j: (p, 0, 0))
    kv_out = pl.BlockSpec((blk, LANES), lambda p, j: (j, p))
    return pl.pallas_call(
        body, grid=(N_PAIRS, nq),
        in_specs=[res, pl.BlockSpec((T, LANES), lambda p, j: (0, 2 + p)),
                  pl.BlockSpec((blk, LANES), lambda p, j: (j, N_PAIRS + p)),
                  pl.BlockSpec((blk, LANES), lambda p, j: (j, 2 * N_PAIRS + p)),
                  pl.BlockSpec((blk, LANES), lambda p, j: (j, 0)), rows, rows, rows],
        out_specs=[pl.BlockSpec((LANES, T), lambda p, j: (p, 0)), kv_out, kv_out, kv_out],
        out_shape=[_sds((N_PAIRS * LANES, T), F32), _sds((T, N_PAIRS * LANES), CDT), _sds((T, N_PAIRS * LANES), CDT),
                   _sds((T, N_PAIRS * LANES), F32)],
        name=name, compiler_params=_cp(("parallel", "arbitrary")))(qkv, dcat, qkv, qkv, fcol, frow, lse, delta)


def _mixer_fwd(x, wts, tag, dep=None):
    T = x.shape[0]
    h = _rms_fwd(x, wts["mix_norm"], f"{tag}_norm", dep)
    qkv = _mm([(h, wts["win_qkv"], False)], out_dtype=CDT, tm=1024, tn=768, name=f"{tag}_in_qkv")
    rest = _mm([(h, wts["win_rest"], False)], tm=1024, name=f"{tag}_in_rest")
    fcol, frow8 = _fgate_fwd(rest, wts["fbias"], f"{tag}_fgate")
    frow = frow8.reshape(N_PAIRS, 2, T)
    ya = _pool_fwd(rest, wts["pool_wbd"], wts["pool_scale"], f"{tag}_pool")
    o, lse = _attn_fwd(qkv, fcol, frow, f"{tag}_attn")
    yc, yconv = _conv_fwd(rest, wts["conv_w"], wts["conv_b"], wts["conv_ln_g"], wts["conv_ln_b"], f"{tag}_conv")
    cat = jnp.concatenate([ya, o.astype(CDT), yc], axis=1)
    y = _mm([(cat, wts["w_out"], False)], res=x, name=f"{tag}_out")
    return y, (x, h, qkv, rest, fcol, frow, o, lse, yconv, cat)


def _mixer_bwd(saved, wts, gout, tag, dep=None):
    x, h, qkv, rest, fcol, frow, o, lse, yconv, cat = saved
    T = x.shape[0]
    dcat = _mm([(gout, wts["w_out"], True)], name=f"{tag}_dcat", dep=dep)
    dwout = _mm_tn(cat, gout, name=f"{tag}_dwout", out_dtype=WIRE)
    du, dpw, dpsc = _pool_bwd(dcat, rest, wts["pool_wbd"], wts["pool_scale"], f"{tag}_pool_bwd")
    delta = _attn_delta(dcat, o, f"{tag}_attn_delta").reshape(N_PAIRS, 2, T)
    dqt, dk, dv, dfk = _attn_bwd(qkv, dcat, fcol, frow, lse, delta, f"{tag}_attn_bwd")
    dq = dqt.T.astype(CDT)
    dz, dfb = _fgate_bwd(dfk, rest, wts["fbias"], f"{tag}_fgate_bwd")
    da, dg, dcw, dcb, dlg, dlb = _conv_bwd(dcat, yconv, rest, wts["conv_w"], wts["conv_ln_g"], wts["conv_ln_b"],
                                           f"{tag}_conv_bwd")
    dp_qkv = jnp.concatenate([dq, dk, dv], axis=1).astype(CDT)
    dp_rest = jnp.concatenate([du, da, dg, dz], axis=1)
    dwin_qkv = _mm_tn(h, dp_qkv, name=f"{tag}_dwin_qkv")
    dwin_rest = _mm_tn(h, dp_rest, name=f"{tag}_dwin_rest")
    gin, dgamma = _mm_norm_bwd([(dp_qkv, wts["win_qkv"], True), (dp_rest, wts["win_rest"], True)], x, wts["mix_norm"],
                               gout, name=f"{tag}_dh_norm_bwd")
    dwin = _split_win(dwin_qkv, dwin_rest, f"{tag}_dwin_split")
    dpool_w = jnp.stack([dpw[64 * g:64 * g + 64, 64 * g:64 * g + 64] for g in range(4)])
    grads = dict(mix_norm=dgamma[0], w_in=dwin, pool_w=dpool_w, pool_scale=dpsc[0], forget_bias=dfb[0, 0:N_HEADS],
                 conv_w=dcw[0:CONV_K], conv_b=dcb[0], conv_ln_g=dlg[0], conv_ln_b=dlb[0], w_out=dwout)
    return gin, grads


def _rep_layer(rep, l):
    pw = rep["pool_w"][l].astype(CDT)
    wbd = jnp.zeros((256, 256), CDT)
    for g in range(4):
        wbd = lax.dynamic_update_slice(wbd, pw[g], (64 * g, 64 * g))
    return dict(
        ffn1_norm=rep["ffn1_norm"][l][None], ffn2_norm=rep["ffn2_norm"][l][None], mix_norm=rep["mix_norm"][l][None],
        fbias=jnp.pad(rep["forget_bias"][l], (0, LANES - N_HEADS))[None],
        pool_wbd=wbd, pool_scale=rep["pool_scale"][l][None], conv_b=rep["conv_b"][l][None],
        conv_ln_g=rep["conv_ln_g"][l][None], conv_ln_b=rep["conv_ln_b"][l][None])


def _local_step(x, target, rep, weights_for, grads_ready):
    depth = rep["ffn1_norm"].shape[0]
    kept = []
    for l in range(depth):
        r = _rep_layer(rep, l)
        w1, dep = weights_for(l, "ffn1", x)

        def late_down(after, l=l):
            got, tok = weights_for(l, "ffn1_down", after)
            return got["w_down"], tok

        x, s1, wd = _ffn_fwd(x, r["ffn1_norm"], w1["w_gate"], w1["w_up"], w1["w_down"] if "w_down" in w1 else late_down,
                             f"l{l}_ffn1", dep)
        w1 = dict(w1, w_down=wd)
        wm, dep = weights_for(l, "mix", x)
        wm = dict(r, win_qkv=wm["win_qkv"], win_rest=wm["win_rest"], w_out=wm["w_out"],
                  conv_w=jnp.pad(wm["conv_w"], ((0, CONV_HALO - CONV_K), (0, 0))))
        x, s2 = _mixer_fwd(x, wm, f"l{l}_mix", dep)
        w2, dep = weights_for(l, "ffn2", x)
        x, s3, _ = _ffn_fwd(x, r["ffn2_norm"], w2["w_gate"], w2["w_up"], w2["w_down"], f"l{l}_ffn2", dep)
        kept.append((r, w1, wm, w2, s1, s2, s3))
    loss, g, dfinal = _loss_bwd(x, rep["final_norm"][None], target, "loss_head")
    dep = grads_ready(None, "final", dict(final_norm=dfinal[0]))
    for l in reversed(range(depth)):
        r, w1, wm, w2, s1, s2, s3 = kept[l]

        def ffn_grads(which, l=l):
            return lambda gr: grads_ready(l, which, {f"{which}_{k}": v for k, v in gr.items()})

        g, dn = _ffn_bwd(s3, r["ffn2_norm"], w2["w_gate"], w2["w_up"], w2["w_down"], g, f"l{l}_ffn2", dep, ffn_grads("ffn2"))
        grads_ready(l, "norm", dict(ffn2_norm=dn[0]))
        g, gm = _mixer_bwd(s2, wm, g, f"l{l}_mix")
        dep = grads_ready(l, "mix", gm)
        g, dn = _ffn_bwd(s1, r["ffn1_norm"], w1["w_gate"], w1["w_up"], w1["w_down"], g, f"l{l}_ffn1", dep, ffn_grads("ffn1"))
        dep = grads_ready(l, "norm", dict(ffn1_norm=dn[0]))
    return loss, g


def _mesh_pos():
    return lax.axis_index("x"), lax.axis_index("y"), lax.axis_index("c")


def _dev_block(ref, dev, by_rows):
    if by_rows:
        r = ref.shape[1] // N_DEV
        return ref.at[:, pl.ds(dev * r, r), :]
    return ref.at[dev]


def _all_gather(shards, by_rows, name):
    n_arr = len(shards)
    out_shape = [_sds((s.shape[0], N_DEV * s.shape[1], s.shape[2]) if br else (N_DEV,) + s.shape, s.dtype)
                 for s, br in zip(shards, by_rows)]

    def body(*refs):
        xs, outs = refs[:n_arr], refs[n_arr:2 * n_arr]
        send_sems, recv_sems, local_sems = refs[2 * n_arr:]
        x, y, c = _mesh_pos()
        me, sibling = (x, y, c), (x, y, 1 - c)
        chips = [(1 - x, y), (x, 1 - y), (1 - x, 1 - y)]

        def rows(a, px, py, pc):
            return _dev_block(outs[a], 4 * px + 2 * py + pc, by_rows[a])

        def copy(k, a, block, to, src=None):
            return pltpu.make_async_remote_copy(
                src_ref=rows(a, *block) if src is None else src, dst_ref=rows(a, *block),
                send_sem=send_sems.at[k, a], recv_sem=recv_sems.at[k, a],
                device_id=to, device_id_type=pl.DeviceIdType.MESH)

        arrs = range(n_arr)
        mine = [pltpu.make_async_copy(xs[a], rows(a, *me), local_sems.at[a]) for a in arrs]
        for cp in mine:
            cp.start()
        first = [copy(0, a, me, sibling, src=xs[a]) for a in arrs]
        first += [copy(1 + j, a, me, (*chip, c), src=xs[a]) for j, chip in enumerate(chips) for a in arrs]
        for cp in first:
            cp.start()
        passed = []
        for j, chip in enumerate(chips):
            for a in arrs:
                copy(1 + j, a, (*chip, c), me).wait_recv()
                passed.append(copy(4 + j, a, (*chip, c), sibling))
                passed[-1].start()
        for a in arrs:
            copy(0, a, sibling, me).wait_recv()
        for j, chip in enumerate(chips):
            for a in arrs:
                copy(4 + j, a, (*chip, 1 - c), me).wait_recv()
        for cp in first + passed:
            cp.wait_send()
        for cp in mine:
            cp.wait()

    hbm = pl.BlockSpec(memory_space=pl.ANY)
    return pl.pallas_call(
        body, out_shape=out_shape, in_specs=[hbm] * n_arr, out_specs=[hbm] * n_arr,
        scratch_shapes=[pltpu.SemaphoreType.DMA((7, n_arr)), pltpu.SemaphoreType.DMA((7, n_arr)),
                        pltpu.SemaphoreType.DMA((n_arr,))],
        name=name)(*shards)


def _peer_copies(srcs, lands, send_sems, recv_sems, gather, by_rows):
    n_arr = len(srcs)
    x, y, c = _mesh_pos()
    my = 4 * x + 2 * y + c
    out = []
    for k in range(1, N_DEV):
        px, py, pc = x ^ (k >> 2), y ^ ((k >> 1) & 1), c ^ (k & 1)
        peer = 4 * px + 2 * py + pc
        for a in range(n_arr):
            src = srcs[a] if gather else _dev_block(srcs[a], peer, by_rows[a])
            dst = _dev_block(lands[a], my, by_rows[a]) if gather else lands[a].at[my]
            out.append(pltpu.make_async_remote_copy(
                src_ref=src, dst_ref=dst, send_sem=send_sems.at[(k - 1) * n_arr + a],
                recv_sem=recv_sems.at[(k - 1) * n_arr + a], device_id=(px, py, pc), device_id_type=pl.DeviceIdType.MESH))
    return out


def _land_shape(s, gather, by_rows):
    if gather:
        return (s.shape[0], N_DEV * s.shape[1], s.shape[2]) if by_rows else (N_DEV,) + s.shape
    return (N_DEV, s.shape[0], s.shape[1] // N_DEV, s.shape[2]) if by_rows else s.shape


_HBM = pl.BlockSpec(memory_space=pltpu.HBM)
_SEM = pl.BlockSpec(memory_space=pltpu.SEMAPHORE)


def _xfer_start(srcs, gather, by_rows, name, dep=None):
    n = len(srcs)
    lands = [lax.empty(_land_shape(s, gather, br), s.dtype) for s, br in zip(srcs, by_rows)]
    ins = [pltpu.with_memory_space_constraint(a, pltpu.HBM) for a in list(srcs) + lands]
    dspec, darg = _dep(dep)

    def body(*refs):
        s = 2 * n + len(darg)
        for cp in _peer_copies(refs[:n], refs[n:2 * n], refs[s], refs[s + 1], gather, by_rows):
            cp.start()
        for cp in _own_copies(refs[:n], refs[n:2 * n], refs[s + 2], gather, by_rows):
            cp.start()
        refs[-1][...] = jnp.zeros_like(refs[-1])

    sems = pltpu.SemaphoreType.DMA(((N_DEV - 1) * n,))
    outs = pl.pallas_call(
        body, name=name,
        out_shape=(sems, sems, pltpu.SemaphoreType.DMA((n,)), *[pltpu.HBM(a.shape, a.dtype) for a in ins],
                   _sds((8, LANES), F32)),
        in_specs=[_HBM] * (2 * n) + dspec,
        out_specs=(_SEM, _SEM, _SEM, *[_HBM] * (2 * n), pl.BlockSpec(memory_space=pltpu.VMEM)),
        input_output_aliases={i: 3 + i for i in range(2 * n)},
        compiler_params=pltpu.CompilerParams(has_side_effects=pltpu.SideEffectType.DATAFLOW_SIDE_EFFECTING))(*ins, *darg)
    return outs[0], outs[1], list(outs[3:-1]), outs[-1], outs[2]


def _own_copies(srcs, lands, sems, gather, by_rows):
    x, y, c = _mesh_pos()
    my = 4 * x + 2 * y + c
    out = []
    for a in range(len(srcs)):
        if gather:
            out.append(pltpu.make_async_copy(srcs[a], _dev_block(lands[a], my, by_rows[a]), sems.at[a]))
        else:
            out.append(pltpu.make_async_copy(_dev_block(srcs[a], my, by_rows[a]), lands[a].at[my], sems.at[a]))
    return out


def _xfer_wait(started, after, gather, by_rows, name):
    send_sems, recv_sems, bufs, _, local_sems = started
    n = len(bufs) // 2

    def body(*refs):
        for cp in _peer_copies(refs[:n], refs[n:2 * n], refs[2 * n], refs[2 * n + 1], gather, by_rows):
            cp.wait_send()
            cp.wait_recv()
        for cp in _own_copies(refs[:n], refs[n:2 * n], refs[2 * n + 2], gather, by_rows):
            cp.wait()

    outs = pl.pallas_call(
        body, name=name, out_shape=tuple(pltpu.HBM(a.shape, a.dtype) for a in bufs),
        in_specs=[_HBM] * (2 * n) + [_SEM, _SEM, _SEM, pl.BlockSpec(memory_space=pl.ANY)],
        out_specs=tuple([_HBM] * (2 * n)), input_output_aliases={i: i for i in range(2 * n)},
        compiler_params=pltpu.CompilerParams(has_side_effects=pltpu.SideEffectType.DATAFLOW_SIDE_EFFECTING))(
            *bufs, send_sems, recv_sems, local_sems, after)
    return list(outs[n:])


def _adam_update(g, w, m, v):
    c1 = 1.0 - ADAM_B1 ** ADAM_STEP
    c2 = 1.0 - ADAM_B2 ** ADAM_STEP
    nm = ADAM_B1 * m + (1.0 - ADAM_B1) * g
    nv = ADAM_B2 * v + (1.0 - ADAM_B2) * (g * g)
    return -ADAM_LR * ((nm / c1) / (jnp.sqrt(nv / c2) + ADAM_EPS) + ADAM_WD * w), nm, nv


def _adamw_body(p_ref, w_ref, m_ref, v_ref, g_ref, d_ref, nm_ref, nv_ref):
    g = p_ref[0].astype(F32)
    for i in range(1, N_DEV):
        g = g + p_ref[i].astype(F32)
    g_ref[...] = g
    d_ref[...], nm_ref[...], nv_ref[...] = _adam_update(g, w_ref[...], m_ref[...], v_ref[...])


def _adamw(parts, w, m, v, name, tr=1536):
    R = w.shape[0]
    tr = max(t for t in range(8, tr + 1, 8) if R % t == 0)

    def body(*refs):
        _adamw_body(*refs)

    row = pl.BlockSpec((tr, LANES), lambda i: (i, 0))
    return pl.pallas_call(
        body, grid=(R // tr,),
        in_specs=[pl.BlockSpec((N_DEV, tr, LANES), lambda i: (0, i, 0)), row, row, row],
        out_specs=[row, row, row, row], out_shape=[_sds((R, LANES), F32)] * 4,
        name=name, compiler_params=_cp(("parallel",)))(parts, w, m, v)


def _adamw_split(recvs, w, m, v, name, tr):
    depth, r, c = w.shape
    assert depth == len(recvs)
    tr = _tile(r, tr)

    def body(*refs):
        layer = pl.program_id(0)
        for ll in range(depth):
            @pl.when(layer == ll)
            def _(ll=ll):
                _adamw_body(refs[ll], *refs[depth:])

    wspec = pl.BlockSpec((None, tr, c), lambda l, i: (l, i, 0))
    rspecs = [pl.BlockSpec((N_DEV, None, tr, c), lambda l, i, ll=ll, t=t: (0, t, jnp.where(l == ll, i, 0), 0))
              for ll, (_, t) in enumerate(recvs)]
    return pl.pallas_call(
        body, grid=(depth, r // tr), in_specs=rspecs + [wspec, wspec, wspec],
        out_specs=[wspec] * 4, out_shape=[_sds(w.shape, F32)] * 4,
        name=name, compiler_params=_cp(("arbitrary", "arbitrary")))(*[a for a, _ in recvs], w, m, v)


def _merge_win(g, name, tr=256):
    _, nt, K, n = g.shape
    tr = _tile(K, tr)

    def body(g_ref, q_ref, r_ref):
        full = jnp.concatenate([g_ref[j] for j in range(N_DEV)], axis=1)
        q_ref[...] = full[:, 256:1792]
        zpad = jnp.zeros((tr, REST_W - 776), full.dtype)
        r_ref[...] = jnp.concatenate([full[:, 0:256], full[:, 1800:2312], full[:, 1792:1800], zpad], axis=1)

    return pl.pallas_call(
        body, grid=(nt, K // tr),
        in_specs=[pl.BlockSpec((N_DEV, None, tr, n), lambda t, i: (0, t, i, 0))],
        out_specs=[pl.BlockSpec((None, tr, 1536), lambda t, i: (t, i, 0)), pl.BlockSpec((None, tr, REST_W), lambda t, i: (t, i, 0))],
        out_shape=[_sds((nt, K, 1536), g.dtype), _sds((nt, K, REST_W), g.dtype)],
        name=name, compiler_params=_cp(("parallel", "parallel")))(g)


def _split_win(dq, dr, name, tr=256):
    K = dq.shape[0]
    tr = _tile(K, tr)
    n = (dq.shape[1] + 776) // N_DEV

    def body(q_ref, r_ref, o_ref):
        r = r_ref[...]
        full = jnp.concatenate([r[:, 0:256], q_ref[...], r[:, 768:776], r[:, 256:768]], axis=1)
        for j in range(N_DEV):
            o_ref[j] = full[:, n * j:n * (j + 1)].astype(o_ref.dtype)

    return pl.pallas_call(
        body, grid=(K // tr,),
        in_specs=[pl.BlockSpec((tr, dq.shape[1]), lambda i: (i, 0)), pl.BlockSpec((tr, REST_W), lambda i: (i, 0))],
        out_specs=pl.BlockSpec((N_DEV, tr, n), lambda i: (0, i, 0)),
        out_shape=_sds((N_DEV, K, n), WIRE), name=name, compiler_params=_cp(("parallel",)))(dq, dr)


WEIGHTS = ["ffn1_norm", "ffn1_w_gate", "ffn1_w_up", "ffn1_w_down", "mix_norm", "w_in", "pool_w", "pool_scale",
           "forget_bias", "conv_w", "conv_b", "conv_ln_g", "conv_ln_b", "w_out", "ffn2_norm", "ffn2_w_gate",
           "ffn2_w_up", "ffn2_w_down", "final_norm"]
FFN_PARTS = ("w_gate", "w_up", "w_down")
FFN_T = ["ffn1_w_gate", "ffn1_w_up", "ffn2_w_gate", "ffn2_w_up"]
BIG = FFN_T + ["ffn1_w_down", "ffn2_w_down", "w_in", "w_out"]
SMALL = [n for n in WEIGHTS if n not in BIG]


def _padded(n):
    return -(-n // PACK_ALIGN) * PACK_ALIGN


def _flat_pad(a):
    f = a.reshape(-1)
    return jnp.pad(f, (0, _padded(f.shape[0]) - f.shape[0]))


def _split8(a, axis):
    shp = a.shape
    a = a.reshape(shp[:axis] + (N_DEV, shp[axis] // N_DEV) + shp[axis + 1:])
    return jnp.moveaxis(a, axis, 0)


def _merge8(a, axis):
    a = jnp.moveaxis(a, 0, axis)
    shp = a.shape
    return a.reshape(shp[:axis] + (shp[axis] * shp[axis + 1],) + shp[axis + 2:])


def _pack_small(arrs):
    return jnp.concatenate([_flat_pad(arrs[n]) for n in SMALL]).reshape(-1, LANES)


def _pack_small_parts(grads):
    cols = []
    for n in SMALL:
        g = grads[n]
        if n == "conv_w":
            s = _split8(g, 2).reshape(N_DEV, -1)
        else:
            s = jnp.broadcast_to(g.reshape(1, -1), (N_DEV, g.size))
        cols.append(jnp.pad(s, ((0, 0), (0, _padded(s.shape[1]) - s.shape[1]))))
    return jnp.concatenate(cols, axis=1).reshape(N_DEV, -1, LANES)


def _unpack_small(buf, like):
    flat = buf.reshape(-1)
    out, off = {}, 0
    for n in SMALL:
        size = like[n].size
        out[n] = flat[off:off + size].reshape(like[n].shape)
        off += _padded(size)
    return out


class _Comm:
    def __init__(self, w):
        self.w = w
        self.bf = {n: (jnp.swapaxes(w[n], 1, 2) if n in FFN_T else w[n]).astype(CDT) for n in BIG}
        self.ready = {}
        self.grads = {}

    def _ffn_shards(self, l, which):
        return jnp.stack([self.bf[f"{which}_{k}"][l] for k in FFN_PARTS])

    def _put_ffn(self, l, which, rows, t):
        self.ready[(l, which)] = dict(w_gate=rows[t], w_up=rows[t + 1], w_down=rows[t + 2])

    def weights_for(self, l, stage, x):
        bf = self.bf
        dep = None
        if (l, stage) == (0, "ffn1"):
            gd, = _all_gather([self._ffn_shards(0, "ffn1")[0:2]], [True], "gather_l0_ffn1")
            self.ready[(0, "ffn1")] = dict(w_gate=gd[0], w_up=gd[1])
            self.started = _xfer_start([bf["ffn1_w_down"][0:1], bf["w_in"][0:1], bf["w_out"][0:1], self.w["conv_w"]], True,
                                       [True, False, True, False], "gather_mix0_start", dep=gd)
            dep = self.started[3]
        elif (l, stage) == (0, "ffn1_down"):
            gdn, gi, go, gc = _xfer_wait(self.started, x, True, [True, False, True, False], "gather_mix0_wait")
            self.ready[(0, "ffn1_down")] = dict(w_down=gdn[0])
            q, r = _merge_win(gi, "merge_l0_w_in")
            self.conv_w = _merge8(gc, 2)
            self.ready[(0, "mix")] = dict(win_qkv=q[0], win_rest=r[0], w_out=go[0], conv_w=self.conv_w[0])
            rows = jnp.concatenate([self._ffn_shards(0, "ffn2"), self._ffn_shards(1, "ffn1"), self._ffn_shards(1, "ffn2")])
            self.started = _xfer_start([rows, bf["w_in"][1:2], bf["w_out"][1:2]], True, [True, False, True],
                                       "gather_rest_start")
            dep = self.started[3]
        elif (l, stage) == (0, "ffn2"):
            gd, gi, go = _xfer_wait(self.started, x, True, [True, False, True], "gather_rest_wait")
            self._put_ffn(0, "ffn2", gd, 0)
            self._put_ffn(1, "ffn1", gd, 3)
            self._put_ffn(1, "ffn2", gd, 6)
            q, r = _merge_win(gi, "merge_l1_w_in")
            self.ready[(1, "mix")] = dict(win_qkv=q[0], win_rest=r[0], w_out=go[0], conv_w=self.conv_w[1])
        return self.ready[(l, stage)], dep

    def grads_ready(self, l, stage, grads):
        for n, v in grads.items():
            self.grads[(l, n)] = v
        gr = self.grads

        def ffn_rows(layer, which, parts=FFN_PARTS):
            return [gr[(layer, f"{which}_{k}")][None] for k in parts]

        if l == 1 and "ffn1_w_gate" in grads:
            self.sent1 = _xfer_start(
                ffn_rows(1, "ffn1") + ffn_rows(1, "ffn2") + [gr[(1, "w_in")][:, None], gr[(1, "w_out")][None]],
                False, [True] * 6 + [False, True], "grads_l1_start")
            return self.sent1[3]
        if l == 0 and "ffn2_w_gate" in grads:
            self.sent_ffn2 = _xfer_start(ffn_rows(0, "ffn2"), False, [True] * 3, "grads_l0_ffn2_start")
            return self.sent_ffn2[3]
        if (l, stage) == (0, "mix"):
            self.sent_mix = _xfer_start([gr[(0, "w_in")][:, None], gr[(0, "w_out")][None]], False, [False, True],
                                        "grads_l0_mix_start")
            return self.sent_mix[3]
        if l == 0 and "ffn1_w_down" in grads:
            self.sent_down = _xfer_start([gr[(0, "ffn1_w_down")][None]], False, [True], "grads_l0_ffn1_down_start")
            return self.sent_down[3]
        if l == 0 and "ffn1_w_gate" in grads:
            self.sent_gu = _xfer_start(ffn_rows(0, "ffn1", FFN_PARTS[:2]), False, [True] * 2,
                                       "grads_l0_ffn1_gate_up_start")
            return self.sent_gu[3]
        return None

    def finish(self, m, v, after):
        w, gr = self.w, self.grads
        depth = range(w["w_in"].shape[0])
        small = {n: (gr[(None, n)] if n == "final_norm" else jnp.stack([gr[(l, n)] for l in depth])) for n in SMALL}
        sent_small = _xfer_start([_pack_small_parts(small)], False, [False], "grads_small_start")
        *r1, i1, o1 = _xfer_wait(self.sent1, after, False, [True] * 6 + [False, True], "grads_l1_wait")
        r2 = _xfer_wait(self.sent_ffn2, after, False, [True] * 3, "grads_l0_ffn2_wait")
        i0, o0 = _xfer_wait(self.sent_mix, after, False, [False, True], "grads_l0_mix_wait")

        def adam(n, recvs, tr):
            if n in FFN_T:
                out = _adamw_split(recvs, *[jnp.swapaxes(t[n], 1, 2) for t in (w, m, v)], f"adamw_{n}", tr)
                return [jnp.swapaxes(o, 1, 2) for o in out]
            return _adamw_split(recvs, w[n], m[n], v[n], f"adamw_{n}", tr)

        res = {}
        for t, k in enumerate(FFN_PARTS):
            res[f"ffn2_{k}"] = adam(f"ffn2_{k}", [(r2[t], 0), (r1[3 + t], 0)], 176)
        res["w_in"] = adam("w_in", [(i0, 0), (i1, 0)], 256)
        res["w_out"] = adam("w_out", [(o0, 0), (o1, 0)], 128)
        r0, = _xfer_wait(self.sent_down, res["w_out"][0], False, [True], "grads_l0_ffn1_down_wait")
        res["ffn1_w_down"] = adam("ffn1_w_down", [(r0, 0), (r1[2], 0)], 176)
        g0 = _xfer_wait(self.sent_gu, res["ffn1_w_down"][0], False, [True] * 2, "grads_l0_ffn1_gate_up_wait")
        res["ffn1_w_gate"] = adam("ffn1_w_gate", [(g0[0], 0), (r1[0], 0)], 176)
        res["ffn1_w_up"] = adam("ffn1_w_up", [(g0[1], 0), (r1[1], 0)], 176)
        rs, = _xfer_wait(sent_small, res["ffn1_w_up"][0], False, [False], "grads_small_wait")
        packed = _adamw(rs, _pack_small(w), _pack_small(m), _pack_small(v), "adamw_small")
        unpacked = [_unpack_small(b, w) for b in packed]
        for n in SMALL:
            res[n] = [u[n] for u in unpacked]
        return res


def kernel(x, ffn1_norm, ffn1_w_gate, ffn1_w_up, ffn1_w_down, mix_norm, w_in, pool_w, pool_scale, forget_bias, conv_w, conv_b, conv_ln_g, conv_ln_b, w_out, ffn2_norm, ffn2_w_gate, ffn2_w_up, ffn2_w_down, final_norm, loss_target, m_ffn1_norm, m_ffn1_w_gate, m_ffn1_w_up, m_ffn1_w_down, m_mix_norm, m_w_in, m_pool_w, m_pool_scale, m_forget_bias, m_conv_w, m_conv_b, m_conv_ln_g, m_conv_ln_b, m_w_out, m_ffn2_norm, m_ffn2_w_gate, m_ffn2_w_up, m_ffn2_w_down, m_final_norm, v_ffn1_norm, v_ffn1_w_gate, v_ffn1_w_up, v_ffn1_w_down, v_mix_norm, v_w_in, v_pool_w, v_pool_scale, v_forget_bias, v_conv_w, v_conv_b, v_conv_ln_g, v_conv_ln_b, v_w_out, v_ffn2_norm, v_ffn2_w_gate, v_ffn2_w_up, v_ffn2_w_down, v_final_norm):
    w = dict(zip(WEIGHTS, (ffn1_norm, ffn1_w_gate, ffn1_w_up, ffn1_w_down, mix_norm, w_in, pool_w, pool_scale, forget_bias,
                           conv_w, conv_b, conv_ln_g, conv_ln_b, w_out, ffn2_norm, ffn2_w_gate, ffn2_w_up, ffn2_w_down,
                           final_norm)))
    m = dict(zip(WEIGHTS, (m_ffn1_norm, m_ffn1_w_gate, m_ffn1_w_up, m_ffn1_w_down, m_mix_norm, m_w_in, m_pool_w, m_pool_scale,
                           m_forget_bias, m_conv_w, m_conv_b, m_conv_ln_g, m_conv_ln_b, m_w_out, m_ffn2_norm, m_ffn2_w_gate,
                           m_ffn2_w_up, m_ffn2_w_down, m_final_norm)))
    v = dict(zip(WEIGHTS, (v_ffn1_norm, v_ffn1_w_gate, v_ffn1_w_up, v_ffn1_w_down, v_mix_norm, v_w_in, v_pool_w, v_pool_scale,
                           v_forget_bias, v_conv_w, v_conv_b, v_conv_ln_g, v_conv_ln_b, v_w_out, v_ffn2_norm, v_ffn2_w_gate,
                           v_ffn2_w_up, v_ffn2_w_down, v_final_norm)))
    comm = _Comm(w)
    loss_row, gx = _local_step(x[0], loss_target[0], w, comm.weights_for, comm.grads_ready)
    loss = lax.psum(loss_row[0, 0], ("x", "y", "c"))
    res = comm.finish(m, v, gx)
    return (loss, gx[None], *[res[n][i] for i in range(4) for n in WEIGHTS])
```

```python
import math

import numpy as np
import jax
import jax.numpy as jnp
from jax import lax
from jax.experimental import pallas as pl
from jax.experimental.pallas import tpu as pltpu

F32 = jnp.float32
CDT = jnp.bfloat16
WIRE = jnp.bfloat16
NORM_EPS = 1e-6
N_DEV = 8
LANES = 128
PACK_ALIGN = 8 * LANES
VMEM_LIMIT = 48 * 1024 * 1024

POOL_WINDOWS = (2, 4, 8, 16)
POOL_HALO = 16
CONV_K = 31
CONV_HALO = 32
HEAD_DIM = 64
N_HEADS = 8
N_PAIRS = N_HEADS // 2
ATT_SCALE = 1.0 / math.sqrt(HEAD_DIM)
NEG = -1e30

ADAM_LR, ADAM_B1, ADAM_B2, ADAM_EPS, ADAM_WD, ADAM_STEP = 0.001, 0.9, 0.999, 1e-08, 0.01, 10

REST_W = 896
REST_Z_BLK = 6


def _cp(sem):
    return pltpu.CompilerParams(dimension_semantics=sem, vmem_limit_bytes=VMEM_LIMIT)


def _tile(n, pref):
    t = min(n, pref)
    assert n % t == 0, (n, pref)
    return t


def _sigmoid(x):
    return 1.0 / (1.0 + jnp.exp(-x))


def _sds(shape, dtype):
    return jax.ShapeDtypeStruct(shape, dtype)


_ANY = pl.BlockSpec(memory_space=pl.ANY)


def _dep(dep):
    return ([], []) if dep is None else ([_ANY], [dep])


def _wshape(w):
    return w[0].shape[1:] if isinstance(w, tuple) else w.shape


def _wspec(w, block, index_map):
    if not isinstance(w, tuple):
        return w, pl.BlockSpec(block, index_map)
    arr, t = w
    return arr, pl.BlockSpec((None,) + block, lambda *g: (t,) + index_map(*g))


def _rms_fwd(x, g, name, dep=None):
    T, D = x.shape
    tm = _tile(T, 1024)

    def body(x_ref, g_ref, *rest):
        o_ref = rest[-1]
        xv = x_ref[...]
        r = lax.rsqrt(jnp.mean(xv * xv, axis=-1, keepdims=True) + NORM_EPS)
        o_ref[...] = (xv * r * g_ref[...]).astype(o_ref.dtype)

    dspec, darg = _dep(dep)
    return pl.pallas_call(
        body, grid=(T // tm,),
        in_specs=[pl.BlockSpec((tm, D), lambda i: (i, 0)), pl.BlockSpec((1, D), lambda i: (0, 0))] + dspec,
        out_specs=pl.BlockSpec((tm, D), lambda i: (i, 0)),
        out_shape=_sds((T, D), CDT), name=name, compiler_params=_cp(("parallel",)))(x, g, *darg)


def _rms_bwd(x, g, dh, gres, name):
    T, D = x.shape
    tm = _tile(T, 512)

    def body(x_ref, g_ref, dh_ref, gres_ref, gin_ref, dg_ref):
        i = pl.program_id(0)
        xv = x_ref[...]
        d = dh_ref[...]
        r = lax.rsqrt(jnp.mean(xv * xv, axis=-1, keepdims=True) + NORM_EPS)
        xh = xv * r
        dxh = d * g_ref[...]
        c = jnp.mean(dxh * xh, axis=-1, keepdims=True)
        gin_ref[...] = gres_ref[...] + r * (dxh - xh * c)
        part = jnp.sum(d * xh, axis=0, keepdims=True)

        @pl.when(i == 0)
        def _():
            dg_ref[...] = part

        @pl.when(i > 0)
        def _():
            dg_ref[...] += part

    row = pl.BlockSpec((tm, D), lambda i: (i, 0))
    vec = pl.BlockSpec((1, D), lambda i: (0, 0))
    return pl.pallas_call(
        body, grid=(T // tm,), in_specs=[row, vec, row, row], out_specs=[row, vec],
        out_shape=[_sds((T, D), F32), _sds((1, D), F32)], name=name, compiler_params=_cp(("arbitrary",)))(x, g, dh, gres)


def _loss_bwd(x, g, target, name):
    T, D = x.shape
    tm = _tile(T, 512)

    def body(x_ref, g_ref, t_ref, loss_ref, dx_ref, dg_ref):
        i = pl.program_id(0)
        xv = x_ref[...]
        gv = g_ref[...]
        r = lax.rsqrt(jnp.mean(xv * xv, axis=-1, keepdims=True) + NORM_EPS)
        xh = xv * r
        err = xh * gv - t_ref[...]
        lpart = 0.5 * jnp.sum(jnp.mean(err * err, axis=-1, keepdims=True), axis=0, keepdims=True)
        dy = err * (1.0 / D)
        dxh = dy * gv
        c = jnp.mean(dxh * xh, axis=-1, keepdims=True)
        dx_ref[...] = r * (dxh - xh * c)
        part = jnp.sum(dy * xh, axis=0, keepdims=True)
        lrow = jnp.broadcast_to(lpart, (1, LANES))

        @pl.when(i == 0)
        def _():
            dg_ref[...] = part
            loss_ref[...] = lrow

        @pl.when(i > 0)
        def _():
            dg_ref[...] += part
            loss_ref[...] += lrow

    row = pl.BlockSpec((tm, D), lambda i: (i, 0))
    vec = pl.BlockSpec((1, D), lambda i: (0, 0))
    return pl.pallas_call(
        body, grid=(T // tm,), in_specs=[row, vec, row],
        out_specs=[pl.BlockSpec((1, LANES), lambda i: (0, 0)), row, vec],
        out_shape=[_sds((1, LANES), F32), _sds((T, D), F32), _sds((1, D), F32)],
        name=name, compiler_params=_cp(("arbitrary",)))(x, g, target)


def _mm(pairs, *, name, res=None, alpha=1.0, out_dtype=F32, tm=512, tn=None, dep=None):
    T = pairs[0][0].shape[0]
    N = _wshape(pairs[0][1])[0 if pairs[0][2] else 1]
    tm = _tile(T, tm)
    tn = N if tn is None else _tile(N, tn)
    flags = [p[2] for p in pairs]
    n_in = 2 * len(pairs)

    def body(*refs):
        o_ref = refs[-1]
        acc = None
        for p, bt in enumerate(flags):
            a = refs[2 * p][...].astype(CDT)
            b = refs[2 * p + 1][...]
            dims = (((1,), (1,)), ((), ())) if bt else (((1,), (0,)), ((), ()))
            d = lax.dot_general(a, b, dims, preferred_element_type=F32)
            acc = d if acc is None else acc + d
        if alpha != 1.0:
            acc = acc * alpha
        if res is not None:
            acc = refs[n_in][...] + acc
        o_ref[...] = acc.astype(o_ref.dtype)

    in_specs, args = [], []
    for a, b, bt in pairs:
        K = a.shape[1]
        in_specs.append(pl.BlockSpec((tm, K), lambda i, j: (i, 0)))
        b, bspec = _wspec(b, (tn, K), lambda i, j: (j, 0)) if bt else _wspec(b, (K, tn), lambda i, j: (0, j))
        in_specs.append(bspec)
        args += [a, b]
    if res is not None:
        in_specs.append(pl.BlockSpec((tm, tn), lambda i, j: (i, j)))
        args.append(res)
    dspec, darg = _dep(dep)
    in_specs += dspec
    args += darg
    return pl.pallas_call(
        body, grid=(T // tm, N // tn), in_specs=in_specs,
        out_specs=pl.BlockSpec((tm, tn), lambda i, j: (i, j)),
        out_shape=_sds((T, N), out_dtype), name=name, compiler_params=_cp(("parallel", "arbitrary")))(*args)


def _mm_norm_bwd(pairs, x, g, gres, *, name, tm=512, dep=None):
    T, D = x.shape
    tm = _tile(T, tm)
    n_in = 2 * len(pairs)
    flags = [p[2] for p in pairs]

    def body(*refs):
        x_ref, g_ref, gres_ref = refs[n_in:n_in + 3]
        gin_ref, dg_ref = refs[-2:]
        i = pl.program_id(0)
        d = None
        for p, bt in enumerate(flags):
            dims = (((1,), (1,)), ((), ())) if bt else (((1,), (0,)), ((), ()))
            part = lax.dot_general(refs[2 * p][...].astype(CDT), refs[2 * p + 1][...], dims, preferred_element_type=F32)
            d = part if d is None else d + part
        xv = x_ref[...]
        r = lax.rsqrt(jnp.mean(xv * xv, axis=-1, keepdims=True) + NORM_EPS)
        xh = xv * r
        dxh = d * g_ref[...]
        c = jnp.mean(dxh * xh, axis=-1, keepdims=True)
        gin_ref[...] = gres_ref[...] + r * (dxh - xh * c)
        part = jnp.sum(d * xh, axis=0, keepdims=True)

        @pl.when(i == 0)
        def _():
            dg_ref[...] = part

        @pl.when(i > 0)
        def _():
            dg_ref[...] += part

    in_specs, args = [], []
    for a, b, bt in pairs:
        K = a.shape[1]
        b, bspec = _wspec(b, tuple(_wshape(b)), lambda i: (0, 0))
        in_specs += [pl.BlockSpec((tm, K), lambda i: (i, 0)), bspec]
        args += [a, b]
    row = pl.BlockSpec((tm, D), lambda i: (i, 0))
    vec = pl.BlockSpec((1, D), lambda i: (0, 0))
    dspec, darg = _dep(dep)
    return pl.pallas_call(
        body, grid=(T // tm,), in_specs=in_specs + [row, vec, row] + dspec, out_specs=[row, vec],
        out_shape=[_sds((T, D), F32), _sds((1, D), F32)], name=name,
        compiler_params=_cp(("arbitrary",)))(*args, x, g, gres, *darg)


def _mm_tn(a, b, *, name, alpha=1.0, tk=2048, dep=None, out_dtype=F32):
    T, M = a.shape
    N = b.shape[1]
    tm = M if M <= 1024 else M // 2
    tn = N if N <= 1536 else N // 2
    assert M % tm == 0 and N % tn == 0 and tm % LANES == 0 and tn % LANES == 0
    tk = _tile(T, tk)
    nk = T // tk
    direct = out_dtype == F32

    def body(a_ref, b_ref, *rest):
        o_ref = rest[-1] if direct else rest[-2]
        acc = o_ref if direct else rest[-1]
        k = pl.program_id(2)
        d = lax.dot_general(a_ref[...].astype(CDT), b_ref[...].astype(CDT), (((0,), (0,)), ((), ())),
                            preferred_element_type=F32)

        @pl.when(k == 0)
        def _():
            acc[...] = d

        @pl.when(k > 0)
        def _():
            acc[...] += d

        if direct:
            if alpha != 1.0:
                @pl.when(k == nk - 1)
                def _():
                    o_ref[...] *= alpha
        else:
            @pl.when(k == nk - 1)
            def _():
                o_ref[...] = (acc[...] * alpha).astype(o_ref.dtype)

    dspec, darg = _dep(dep)
    return pl.pallas_call(
        body, grid=(M // tm, N // tn, nk),
        in_specs=[pl.BlockSpec((tk, tm), lambda i, j, k: (k, i)), pl.BlockSpec((tk, tn), lambda i, j, k: (k, j))] + dspec,
        out_specs=pl.BlockSpec((tm, tn), lambda i, j, k: (i, j)),
        out_shape=_sds((M, N), out_dtype), scratch_shapes=[] if direct else [pltpu.VMEM((tm, tn), F32)],
        name=name, compiler_params=_cp(("parallel", "parallel", "arbitrary")))(a, b, *darg)


def _ffn_up(h, wgt, wut, name):
    T, D = h.shape
    Fh = _wshape(wgt)[0]
    tm = _tile(T, 4096)
    tn = _tile(Fh, 256)
    nt = (((1,), (1,)), ((), ()))

    def body(h_ref, wg_ref, wu_ref, a_ref, b_ref, s_ref):
        hv = h_ref[...]
        a = lax.dot_general(hv, wg_ref[...], nt, preferred_element_type=F32)
        b = lax.dot_general(hv, wu_ref[...], nt, preferred_element_type=F32)
        a_ref[...] = a.astype(a_ref.dtype)
        b_ref[...] = b.astype(b_ref.dtype)
        s_ref[...] = (a * _sigmoid(a) * b).astype(s_ref.dtype)

    wgt, gspec = _wspec(wgt, (tn, D), lambda i, j: (j, 0))
    wut, uspec = _wspec(wut, (tn, D), lambda i, j: (j, 0))
    ospec = pl.BlockSpec((tm, tn), lambda i, j: (i, j))
    return pl.pallas_call(
        body, grid=(T // tm, Fh // tn),
        in_specs=[pl.BlockSpec((tm, D), lambda i, j: (i, 0)), gspec, uspec],
        out_specs=[ospec, ospec, ospec],
        out_shape=[_sds((T, Fh), CDT), _sds((T, Fh), CDT), _sds((T, Fh), CDT)],
        name=name, compiler_params=_cp(("parallel", "arbitrary")))(h, wgt, wut)


def _ffn_bwd_ds(gout, wd, a, b, name, dep=None):
    T, D = gout.shape
    Fh = _wshape(wd)[0]
    tm = _tile(T, 2048)
    tn = _tile(Fh, 256)

    def body(g_ref, wd_ref, a_ref, b_ref, *rest):
        da_ref, db_ref = rest[-2:]
        dy = (0.5 * g_ref[...]).astype(CDT)
        ds = lax.dot_general(dy, wd_ref[...], (((1,), (1,)), ((), ())), preferred_element_type=F32)
        av = a_ref[...].astype(F32)
        sg = _sigmoid(av)
        da_ref[...] = (ds * b_ref[...].astype(F32) * (sg * (1.0 + av * (1.0 - sg)))).astype(da_ref.dtype)
        db_ref[...] = (ds * (av * sg)).astype(db_ref.dtype)

    ospec = pl.BlockSpec((tm, tn), lambda i, j: (i, j))
    dspec, darg = _dep(dep)
    wd, wspec = _wspec(wd, (tn, D), lambda i, j: (j, 0))
    return pl.pallas_call(
        body, grid=(T // tm, Fh // tn),
        in_specs=[pl.BlockSpec((tm, D), lambda i, j: (i, 0)), wspec, ospec, ospec] + dspec,
        out_specs=[ospec, ospec],
        out_shape=[_sds((T, Fh), CDT), _sds((T, Fh), CDT)],
        name=name, compiler_params=_cp(("parallel", "arbitrary")))(gout, wd, a, b, *darg)


def _ffn_fwd(x, gamma, wgt, wut, wd, tag, dep=None):
    h = _rms_fwd(x, gamma, f"{tag}_norm", dep)
    a, b, s = _ffn_up(h, wgt, wut, f"{tag}_up")
    dep = None
    if callable(wd):
        wd, dep = wd(a)
    y = _mm([(s, wd, False)], res=x, alpha=0.5, tm=1024, name=f"{tag}_down", dep=dep)
    return y, (x, h, a, b, s), wd


def _ffn_bwd(saved, gamma, wgt, wut, wd, gout, tag, dep, on_grads):
    x, h, a, b, s = saved
    dwd = _mm_tn(s, gout, alpha=0.5, name=f"{tag}_dwd", dep=dep, out_dtype=WIRE)
    da, db = _ffn_bwd_ds(gout, wd, a, b, f"{tag}_bwd_ds", on_grads(dict(w_down=dwd)))
    dwgt = _mm_tn(da, h, name=f"{tag}_dwg", out_dtype=WIRE)
    dwut = _mm_tn(db, h, name=f"{tag}_dwu", out_dtype=WIRE)
    dep = on_grads(dict(w_gate=dwgt, w_up=dwut))
    return _mm_norm_bwd([(da, wgt, False), (db, wut, False)], x, gamma, gout, name=f"{tag}_dh_norm_bwd", dep=dep)


def _fgate_fwd(rest, bias, name, bt=512):
    T = rest.shape[0]
    bt = _tile(T, bt)

    def body(z_ref, b_ref, fc_ref, ft_ref, carry):
        i = pl.program_id(0)

        @pl.when(i == 0)
        def _():
            carry[...] = jnp.zeros_like(carry)

        zb = z_ref[...] + b_ref[...]
        e = jnp.exp(-jnp.abs(zb))
        u = 1.0 + e
        log1p_e = jnp.where(u == 1.0, e, jnp.log(u) * (e / (u - 1.0)))
        x = jnp.minimum(zb, 0.0) - log1p_e
        row = lax.broadcasted_iota(jnp.int32, x.shape, 0)
        sh = 1
        while sh < bt:
            x = x + jnp.where(row >= sh, pltpu.roll(x, sh, 0), 0.0)
            sh *= 2
        f = x + carry[...]
        carry[...] = f[bt - 1:bt, :]
        fc_ref[...] = f
        ft_ref[...] = jnp.transpose(f)[0:N_HEADS, :]

    return pl.pallas_call(
        body, grid=(T // bt,),
        in_specs=[pl.BlockSpec((bt, LANES), lambda i: (i, REST_Z_BLK)), pl.BlockSpec((1, LANES), lambda i: (0, 0))],
        out_specs=[pl.BlockSpec((bt, LANES), lambda i: (i, 0)), pl.BlockSpec((N_HEADS, bt), lambda i: (0, i))],
        out_shape=[_sds((T, LANES), F32), _sds((N_HEADS, T), F32)],
        scratch_shapes=[pltpu.VMEM((1, LANES), F32)],
        name=name, compiler_params=_cp(("arbitrary",)))(rest, bias)


def _fgate_bwd(dfk, rest, bias, name, bt=512):
    T = rest.shape[0]
    bt = _tile(T, bt)
    nb = T // bt

    def body(df_ref, z_ref, b_ref, dz_ref, db_ref, carry):
        i = pl.program_id(0)

        @pl.when(i == 0)
        def _():
            carry[...] = jnp.zeros_like(carry)

        dfv = df_ref[...]
        lane = lax.broadcasted_iota(jnp.int32, (bt, LANES), 1)
        x = jnp.zeros((bt, LANES), F32)
        for h in range(N_HEADS):
            x = jnp.where(lane == h, dfv[:, HEAD_DIM * h:HEAD_DIM * h + 1], x)
        row = lax.broadcasted_iota(jnp.int32, x.shape, 0)
        sh = 1
        while sh < bt:
            x = x + jnp.where(row + sh < bt, pltpu.roll(x, bt - sh, 0), 0.0)
            sh *= 2
        dlf = x + carry[...]
        carry[...] = dlf[0:1, :]
        zb = z_ref[...] + b_ref[...]
        dz = jnp.where(lane < N_HEADS, dlf * _sigmoid(-zb), 0.0)
        dz_ref[...] = dz.astype(dz_ref.dtype)
        part = jnp.sum(dz, axis=0, keepdims=True)

        @pl.when(i == 0)
        def _():
            db_ref[...] = part

        @pl.when(i > 0)
        def _():
            db_ref[...] += part

    return pl.pallas_call(
        body, grid=(nb,),
        in_specs=[pl.BlockSpec((bt, 4 * LANES), lambda i: (nb - 1 - i, 0)),
                  pl.BlockSpec((bt, LANES), lambda i: (nb - 1 - i, REST_Z_BLK)),
                  pl.BlockSpec((1, LANES), lambda i: (0, 0))],
        out_specs=[pl.BlockSpec((bt, LANES), lambda i: (nb - 1 - i, 0)), pl.BlockSpec((1, LANES), lambda i: (0, 0))],
        out_shape=[_sds((T, LANES), CDT), _sds((1, LANES), F32)],
        scratch_shapes=[pltpu.VMEM((1, LANES), F32)],
        name=name, compiler_params=_cp(("arbitrary",)))(dfk, rest, bias)


def _by_group(vals, lane):
    out = vals[-1]
    for g in range(len(vals) - 2, -1, -1):
        out = jnp.where(lane // 64 == g, vals[g], out)
    return out


def _pool_counts(t0, n, lane):
    t = t0 + lax.broadcasted_iota(jnp.int32, (n, 256), 0)
    return _by_group([jnp.minimum(t + 1, w) for w in POOL_WINDOWS], lane).astype(F32)


def _pooled(u, halo, i, bt):
    lane = lax.broadcasted_iota(jnp.int32, (bt, 256), 1)
    ext = jnp.concatenate([jnp.where(i > 0, halo, 0.0), u], axis=0)
    sums, s, sh = [], ext, 1
    for _ in POOL_WINDOWS:
        s = s + pltpu.roll(s, sh, 0)
        sums.append(s[POOL_HALO:, :])
        sh *= 2
    return _by_group(sums, lane) / _pool_counts(i * bt, bt, lane) - u


def _pool_fwd(rest, wbd, scale, name, bt=512):
    T = rest.shape[0]
    bt = _tile(T, bt)
    hb = bt // POOL_HALO

    def body(u_ref, halo_ref, w_ref, sc_ref, o_ref):
        i = pl.program_id(0)
        pooled = _pooled(u_ref[...], halo_ref[...], i, bt)
        mixed = jnp.dot(pooled.astype(CDT), w_ref[...], preferred_element_type=F32)
        o_ref[...] = (mixed * sc_ref[...]).astype(o_ref.dtype)

    return pl.pallas_call(
        body, grid=(T // bt,),
        in_specs=[pl.BlockSpec((bt, 256), lambda i: (i, 0)),
                  pl.BlockSpec((POOL_HALO, 256), lambda i: (jnp.maximum(i * hb - 1, 0), 0)),
                  pl.BlockSpec((256, 256), lambda i: (0, 0)), pl.BlockSpec((1, 256), lambda i: (0, 0))],
        out_specs=pl.BlockSpec((bt, 256), lambda i: (i, 0)),
        out_shape=_sds((T, 256), CDT), name=name, compiler_params=_cp(("parallel",)))(rest, rest, wbd, scale)


def _pool_bwd(dcat, rest, wbd, scale, name, bt=512):
    T = rest.shape[0]
    bt = _tile(T, bt)
    hb = bt // POOL_HALO
    nb = T // bt
    n = bt + POOL_HALO

    def body(dy_ref, dyn_ref, u_ref, halo_ref, w_ref, sc_ref, du_ref, dw_ref, dsc_ref):
        i = pl.program_id(0)
        lane = lax.broadcasted_iota(jnp.int32, (bt, 256), 1)
        w = w_ref[...]
        sc = sc_ref[...]
        pooled = _pooled(u_ref[...], halo_ref[...], i, bt)
        pooled_c = pooled.astype(CDT)
        mixed = jnp.dot(pooled_c, w, preferred_element_type=F32)
        dy = dy_ref[...]
        dm = (dy * sc).astype(CDT)
        dsc = jnp.sum(dy * mixed, axis=0, keepdims=True)
        dw = lax.dot_general(pooled_c, dm, (((0,), (0,)), ((), ())), preferred_element_type=F32)
        nt = (((1,), (1,)), ((), ()))
        dpl = lax.dot_general(dm, w, nt, preferred_element_type=F32)
        dmn = (jnp.where(i < nb - 1, dyn_ref[...], 0.0) * sc).astype(CDT)
        dpln = lax.dot_general(dmn, w, nt, preferred_element_type=F32)
        lane_h = lax.broadcasted_iota(jnp.int32, (POOL_HALO, 256), 1)
        ext = jnp.concatenate([dpl / _pool_counts(i * bt, bt, lane),
                               dpln / _pool_counts((i + 1) * bt, POOL_HALO, lane_h)], axis=0)
        sums, s, sh = [], ext, 1
        for _ in POOL_WINDOWS:
            s = s + pltpu.roll(s, n - sh, 0)
            sums.append(s[0:bt, :])
            sh *= 2
        du_ref[...] = (_by_group(sums, lane) - dpl).astype(du_ref.dtype)

        @pl.when(i == 0)
        def _():
            dw_ref[...] = dw
            dsc_ref[...] = dsc

        @pl.when(i > 0)
        def _():
            dw_ref[...] += dw
            dsc_ref[...] += dsc

    full = pl.BlockSpec((256, 256), lambda i: (0, 0))
    vec = pl.BlockSpec((1, 256), lambda i: (0, 0))
    return pl.pallas_call(
        body, grid=(nb,),
        in_specs=[pl.BlockSpec((bt, 256), lambda i: (i, 0)),
                  pl.BlockSpec((POOL_HALO, 256), lambda i: (jnp.minimum((i + 1) * hb, nb * hb - 1), 0)),
                  pl.BlockSpec((bt, 256), lambda i: (i, 0)),
                  pl.BlockSpec((POOL_HALO, 256), lambda i: (jnp.maximum(i * hb - 1, 0), 0)),
                  full, vec],
        out_specs=[pl.BlockSpec((bt, 256), lambda i: (i, 0)), full, vec],
        out_shape=[_sds((T, 256), CDT), _sds((256, 256), F32), _sds((1, 256), F32)],
        name=name, compiler_params=_cp(("arbitrary",)))(dcat, dcat, rest, rest, wbd, scale)


def _glu_ext(a_ref, g_ref, ah_ref, gh_ref, i):
    u = a_ref[...] * _sigmoid(g_ref[...])
    uh = jnp.where(i > 0, ah_ref[...] * _sigmoid(gh_ref[...]), 0.0)
    return jnp.concatenate([uh, u], axis=0)


def _conv_fwd(rest, cw, cb, lg, lb, name, bt=512):
    T = rest.shape[0]
    bt = _tile(T, bt)
    hb = bt // CONV_HALO

    def body(a_ref, g_ref, ah_ref, gh_ref, cw_ref, cb_ref, lg_ref, lb_ref, o_ref, y_ref):
        i = pl.program_id(0)
        ext = _glu_ext(a_ref, g_ref, ah_ref, gh_ref, i)
        w = cw_ref[...]
        acc = w[CONV_K - 1:CONV_K, :] * ext
        for k in range(CONV_K - 1):
            acc = acc + w[k:k + 1, :] * pltpu.roll(ext, CONV_K - 1 - k, 0)
        y = acc[CONV_HALO:, :] + cb_ref[...]
        y_ref[...] = y
        yc = y - jnp.mean(y, axis=-1, keepdims=True)
        yn = yc * lax.rsqrt(jnp.mean(yc * yc, axis=-1, keepdims=True) + NORM_EPS)
        z = yn * lg_ref[...] + lb_ref[...]
        o_ref[...] = (z * _sigmoid(z)).astype(o_ref.dtype)

    def cur(c):
        return pl.BlockSpec((bt, 256), lambda i: (i, c))

    def prev(c):
        return pl.BlockSpec((CONV_HALO, 256), lambda i: (jnp.maximum(i * hb - 1, 0), c))

    vec = pl.BlockSpec((1, 256), lambda i: (0, 0))
    return pl.pallas_call(
        body, grid=(T // bt,),
        in_specs=[cur(1), cur(2), prev(1), prev(2), pl.BlockSpec((CONV_HALO, 256), lambda i: (0, 0)), vec, vec, vec],
        out_specs=[pl.BlockSpec((bt, 256), lambda i: (i, 0)), pl.BlockSpec((bt, 256), lambda i: (i, 0))],
        out_shape=[_sds((T, 256), CDT), _sds((T, 256), F32)],
        name=name, compiler_params=_cp(("parallel",)))(rest, rest, rest, rest, cw, cb, lg, lb)


def _conv_bwd(dcat, yconv, rest, cw, lg, lb, name, bt=512):
    T = rest.shape[0]
    bt = _tile(T, bt)
    hb = bt // CONV_HALO
    nb = T // bt
    n = bt + CONV_HALO

    def body(dy_ref, dyn_ref, y_ref, yn_ref, a_ref, g_ref, ah_ref, gh_ref, cw_ref, lg_ref, lb_ref,
             da_ref, dg_ref, dcw_ref, dcb_ref, dlg_ref, dlb_ref):
        i = pl.program_id(0)
        lgv = lg_ref[...]
        lbv = lb_ref[...]

        def ln_swish_bwd(dout, y):
            yc = y - jnp.mean(y, axis=-1, keepdims=True)
            rs = lax.rsqrt(jnp.mean(yc * yc, axis=-1, keepdims=True) + NORM_EPS)
            yn = yc * rs
            z = yn * lgv + lbv
            sg = _sigmoid(z)
            dz = dout * (sg * (1.0 + z * (1.0 - sg)))
            dyn = dz * lgv
            dyc = rs * (dyn - jnp.mean(dyn, axis=-1, keepdims=True) - yn * jnp.mean(dyn * yn, axis=-1, keepdims=True))
            return dyc, dz, yn

        dyc, dz, yn = ln_swish_bwd(dy_ref[...], y_ref[...])
        dyc_next, _, _ = ln_swish_bwd(dyn_ref[...], yn_ref[...])
        dyc_next = jnp.where(i < nb - 1, dyc_next, 0.0)
        ext_u = _glu_ext(a_ref, g_ref, ah_ref, gh_ref, i)
        ext_d = jnp.concatenate([dyc, dyc_next], axis=0)
        w = cw_ref[...]
        du = w[CONV_K - 1:CONV_K, :] * ext_d
        rows = []
        for k in range(CONV_K):
            s = CONV_K - 1 - k
            if s > 0:
                du = du + w[k:k + 1, :] * pltpu.roll(ext_d, n - s, 0)
                us = pltpu.roll(ext_u, s, 0)[CONV_HALO:, :]
            else:
                us = ext_u[CONV_HALO:, :]
            rows.append(jnp.sum(dyc * us, axis=0, keepdims=True))
        rows.append(jnp.zeros((1, 256), F32))
        dcw = jnp.concatenate(rows, axis=0)
        du = du[0:bt, :]
        av = a_ref[...]
        sg = _sigmoid(g_ref[...])
        da_ref[...] = (du * sg).astype(da_ref.dtype)
        dg_ref[...] = (du * av * (sg * (1.0 - sg))).astype(dg_ref.dtype)
        dcb = jnp.sum(dyc, axis=0, keepdims=True)
        dlg = jnp.sum(dz * yn, axis=0, keepdims=True)
        dlb = jnp.sum(dz, axis=0, keepdims=True)

        @pl.when(i == 0)
        def _():
            dcw_ref[...] = dcw
            dcb_ref[...] = dcb
            dlg_ref[...] = dlg
            dlb_ref[...] = dlb

        @pl.when(i > 0)
        def _():
            dcw_ref[...] += dcw
            dcb_ref[...] += dcb
            dlg_ref[...] += dlg
            dlb_ref[...] += dlb

    def cur(c):
        return pl.BlockSpec((bt, 256), lambda i: (i, c))

    def prev(c):
        return pl.BlockSpec((CONV_HALO, 256), lambda i: (jnp.maximum(i * hb - 1, 0), c))

    def nxt(c):
        return pl.BlockSpec((CONV_HALO, 256), lambda i: (jnp.minimum((i + 1) * hb, nb * hb - 1), c))

    vec = pl.BlockSpec((1, 256), lambda i: (0, 0))
    wfull = pl.BlockSpec((CONV_HALO, 256), lambda i: (0, 0))
    return pl.pallas_call(
        body, grid=(nb,),
        in_specs=[cur(3), nxt(3), cur(0), nxt(0), cur(1), cur(2), prev(1), prev(2), wfull, vec, vec],
        out_specs=[cur(0), cur(0), wfull, vec, vec, vec],
        out_shape=[_sds((T, 256), CDT), _sds((T, 256), CDT), _sds((CONV_HALO, 256), F32),
                   _sds((1, 256), F32), _sds((1, 256), F32), _sds((1, 256), F32)],
        name=name, compiler_params=_cp(("arbitrary",)))(dcat, dcat, yconv, yconv, rest, rest, rest, rest, cw, lg, lb)


def _half_mask(shape, a):
    lane = lax.broadcasted_iota(jnp.int32, shape, 1)
    return (lane // HEAD_DIM) == a


def _attn_fwd(qkv, fcol, frow, name, blk=1024):
    T = qkv.shape[0]
    blk = _tile(T, blk)
    nq = T // blk
    nt = (((1,), (1,)), ((), ()))

    def body(q_ref, k_ref, v_ref, fc_ref, fr_ref, o_ref, lse_ref):
        p_id = pl.program_id(0)
        i = pl.program_id(1)
        q2 = q_ref[...]
        fc = fc_ref[...]
        lane = lax.broadcasted_iota(jnp.int32, (blk, LANES), 1)
        tri = lax.broadcasted_iota(jnp.int32, (blk, blk), 1) <= lax.broadcasted_iota(jnp.int32, (blk, blk), 0)
        masks = [_half_mask(q2.shape, a) for a in range(2)]
        qs = [jnp.where(hm, q2, jnp.zeros_like(q2)) * ATT_SCALE for hm in masks]
        fqs = [jnp.sum(jnp.where(lane == 2 * p_id + a, fc, 0.0), axis=1, keepdims=True) for a in range(2)]

        def tile(j, carry, masked):
            cols = pl.ds(pl.multiple_of(j * blk, blk), blk)
            kj = k_ref[cols, :]
            vj = v_ref[cols, :]
            out = []
            for a in range(2):
                m, acc = carry[2 * a:2 * a + 2]
                va = jnp.where(masks[a], vj, jnp.ones_like(vj))
                s = lax.dot_general(qs[a], kj, nt, preferred_element_type=F32) + (fqs[a] - fr_ref[a:a + 1, cols])
                if masked:
                    s = jnp.where(tri, s, NEG)
                m_new = jnp.maximum(m, jnp.max(s, axis=1, keepdims=True))
                alpha = jnp.exp(m - m_new)
                pr = jnp.exp(s - m_new)
                hi = lax.bitcast_convert_type(lax.bitcast_convert_type(pr, jnp.uint32) & jnp.uint32(0xFFFF0000), F32)
                pv = (jnp.dot(hi.astype(CDT), va, preferred_element_type=F32)
                      + jnp.dot((pr - hi).astype(CDT), va, preferred_element_type=F32))
                out += [m_new, alpha * acc + pv]
            return tuple(out)

        init = (jnp.full((blk, 1), NEG, F32), jnp.zeros((blk, LANES), F32)) * 2
        carry = lax.fori_loop(0, i, lambda j, c: tile(j, c, False), init)
        carry = tile(i, carry, True)
        ls = [carry[1][:, HEAD_DIM:HEAD_DIM + 1], carry[3][:, 0:1]]
        lo = lane < HEAD_DIM
        o_ref[...] = jnp.where(lo, carry[1] / ls[0], carry[3] / ls[1])
        lse_t = jnp.transpose(jnp.where(lo, carry[0] + jnp.log(ls[0]), carry[2] + jnp.log(ls[1])))
        lse_ref[...] = jnp.concatenate([lse_t[0:1, :], lse_t[HEAD_DIM:HEAD_DIM + 1, :]], axis=0)

    return pl.pallas_call(
        body, grid=(N_PAIRS, nq),
        in_specs=[pl.BlockSpec((blk, LANES), lambda p, i: (i, p)),
                  pl.BlockSpec((T, LANES), lambda p, i: (0, N_PAIRS + p)),
                  pl.BlockSpec((T, LANES), lambda p, i: (0, 2 * N_PAIRS + p)),
                  pl.BlockSpec((blk, LANES), lambda p, i: (i, 0)),
                  pl.BlockSpec((None, 2, T), lambda p, i: (p, 0, 0))],
        out_specs=[pl.BlockSpec((blk, LANES), lambda p, i: (i, p)), pl.BlockSpec((None, 2, blk), lambda p, i: (p, 0, i))],
        out_shape=[_sds((T, N_PAIRS * LANES), F32), _sds((N_PAIRS, 2, T), F32)],
        name=name, compiler_params=_cp(("parallel", "arbitrary")))(qkv, qkv, qkv, fcol, frow)


def _attn_delta(dcat, o, name, blk=512):
    T = o.shape[0]
    blk = _tile(T, blk)

    def body(d_ref, o_ref, out_ref):
        prod = d_ref[:, 256:768].astype(CDT).astype(F32) * o_ref[...]
        pt = jnp.transpose(prod)
        out_ref[...] = jnp.sum(pt.reshape(N_HEADS, HEAD_DIM, blk), axis=1)

    return pl.pallas_call(
        body, grid=(T // blk,),
        in_specs=[pl.BlockSpec((blk, 1024), lambda i: (i, 0)), pl.BlockSpec((blk, 512), lambda i: (i, 0))],
        out_specs=pl.BlockSpec((N_HEADS, blk), lambda i: (0, i)),
        out_shape=_sds((N_HEADS, T), F32), name=name, compiler_params=_cp(("parallel",)))(dcat, o)


def _attn_bwd(qkv, dcat, fcol, frow, lse, delta, name, blk=1024):
    T = qkv.shape[0]
    blk = _tile(T, blk)
    nq = T // blk
    nt = (((1,), (1,)), ((), ()))

    def body(q_ref, do_ref, k_ref, v_ref, fc_ref, fr_ref, lse_ref, dl_ref, dqt_ref, dk_ref, dv_ref, df_ref):
        p_id = pl.program_id(0)
        j = pl.program_id(1)

        @pl.when(j == 0)
        def _():
            dqt_ref[...] = jnp.zeros_like(dqt_ref)

        k2 = k_ref[...]
        v2 = v_ref[...]
        fc = fc_ref[...]
        lane = lax.broadcasted_iota(jnp.int32, (blk, LANES), 1)
        tri = lax.broadcasted_iota(jnp.int32, (blk, blk), 0) <= lax.broadcasted_iota(jnp.int32, (blk, blk), 1)
        masks = [_half_mask(k2.shape, a) for a in range(2)]
        kas = [jnp.where(hm, k2, jnp.zeros_like(k2)) * ATT_SCALE for hm in masks]
        kats = [jnp.transpose(ka) for ka in kas]
        vas = [jnp.where(hm, v2, jnp.zeros_like(v2)) for hm in masks]
        fks = [jnp.sum(jnp.where(lane == 2 * p_id + a, fc, 0.0), axis=1, keepdims=True) for a in range(2)]

        def tile(i, carry, masked):
            rows = pl.ds(pl.multiple_of(i * blk, blk), blk)
            qi = q_ref[rows, :]
            doi = do_ref[rows, :].astype(CDT)
            out = []
            dqt = None
            for a in range(2):
                dk_acc, dv_acc, df_acc = carry[3 * a:3 * a + 3]
                st = lax.dot_general(kas[a], qi, nt, preferred_element_type=F32)
                e = (st + (fr_ref[a:a + 1, rows] - fks[a])) - lse_ref[a:a + 1, rows]
                if masked:
                    e = jnp.where(tri, e, NEG)
                pt = jnp.exp(e)
                dpt = lax.dot_general(vas[a], doi, nt, preferred_element_type=F32)
                ds32 = pt * (dpt - dl_ref[a:a + 1, rows])
                dst = ds32.astype(CDT)
                df_acc = df_acc + jnp.sum(ds32, axis=1, keepdims=True)
                dv_acc = dv_acc + jnp.dot(pt.astype(CDT), doi, preferred_element_type=F32)
                dk_acc = dk_acc + jnp.dot(dst, qi, preferred_element_type=F32)
                part = jnp.dot(kats[a], dst, preferred_element_type=F32)
                dqt = part if dqt is None else dqt + part
                out += [dk_acc, dv_acc, df_acc]
            dqt_ref[:, rows] += dqt
            return tuple(out)

        init = (jnp.zeros((blk, LANES), F32), jnp.zeros((blk, LANES), F32), jnp.zeros((blk, 1), F32)) * 2
        carry = tile(j, init, True)
        carry = lax.fori_loop(j + 1, nq, lambda i, c: tile(i, c, False), carry)
        lo = lane < HEAD_DIM
        dk_ref[...] = (jnp.where(lo, carry[0], carry[3]) * ATT_SCALE).astype(dk_ref.dtype)
        dv_ref[...] = jnp.where(lo, carry[1], carry[4]).astype(dv_ref.dtype)
        df_ref[...] = -jnp.where(lo, carry[2], carry[5])

    res = pl.BlockSpec((T, LANES), lambda p, j: (0, p))
    rows = pl.BlockSpec((None, 2, T), lambda p, j: (p, 0, 0))
    kv_out = pl.BlockSpec((blk, LANES), lambda p, j: (j, p))
    return pl.pallas_call(
        body, grid=(N_PAIRS, nq),
        in_specs=[res, pl.BlockSpec((T, LANES), lambda p, j: (0, 2 + p)),
                  pl.BlockSpec((blk, LANES), lambda p, j: (j, N_PAIRS + p)),
                  pl.BlockSpec((blk, LANES), lambda p, j: (j, 2 * N_PAIRS + p)),
                  pl.BlockSpec((blk, LANES), lambda p, j: (j, 0)), rows, rows, rows],
        out_specs=[pl.BlockSpec((LANES, T), lambda p, j: (p, 0)), kv_out, kv_out, kv_out],
        out_shape=[_sds((N_PAIRS * LANES, T), F32), _sds((T, N_PAIRS * LANES), CDT), _sds((T, N_PAIRS * LANES), CDT),
                   _sds((T, N_PAIRS * LANES), F32)],
        name=name, compiler_params=_cp(("parallel", "arbitrary")))(qkv, dcat, qkv, qkv, fcol, frow, lse, delta)


def _mixer_fwd(x, wts, tag, dep=None):
    T = x.shape[0]
    h = _rms_fwd(x, wts["mix_norm"], f"{tag}_norm", dep)
    qkv = _mm([(h, wts["win_qkv"], False)], out_dtype=CDT, tm=1024, tn=768, name=f"{tag}_in_qkv")
    rest = _mm([(h, wts["win_rest"], False)], tm=1024, name=f"{tag}_in_rest")
    fcol, frow8 = _fgate_fwd(rest, wts["fbias"], f"{tag}_fgate")
    frow = frow8.reshape(N_PAIRS, 2, T)
    ya = _pool_fwd(rest, wts["pool_wbd"], wts["pool_scale"], f"{tag}_pool")
    o, lse = _attn_fwd(qkv, fcol, frow, f"{tag}_attn")
    yc, yconv = _conv_fwd(rest, wts["conv_w"], wts["conv_b"], wts["conv_ln_g"], wts["conv_ln_b"], f"{tag}_conv")
    cat = jnp.concatenate([ya, o.astype(CDT), yc], axis=1)
    y = _mm([(cat, wts["w_out"], False)], res=x, name=f"{tag}_out")
    return y, (x, h, qkv, rest, fcol, frow, o, lse, yconv, cat)


def _mixer_bwd(saved, wts, gout, tag, dep=None):
    x, h, qkv, rest, fcol, frow, o, lse, yconv, cat = saved
    T = x.shape[0]
    dcat = _mm([(gout, wts["w_out"], True)], name=f"{tag}_dcat", dep=dep)
    dwout = _mm_tn(cat, gout, name=f"{tag}_dwout", out_dtype=WIRE)
    du, dpw, dpsc = _pool_bwd(dcat, rest, wts["pool_wbd"], wts["pool_scale"], f"{tag}_pool_bwd")
    delta = _attn_delta(dcat, o, f"{tag}_attn_delta").reshape(N_PAIRS, 2, T)
    dqt, dk, dv, dfk = _attn_bwd(qkv, dcat, fcol, frow, lse, delta, f"{tag}_attn_bwd")
    dq = dqt.T.astype(CDT)
    dz, dfb = _fgate_bwd(dfk, rest, wts["fbias"], f"{tag}_fgate_bwd")
    da, dg, dcw, dcb, dlg, dlb = _conv_bwd(dcat, yconv, rest, wts["conv_w"], wts["conv_ln_g"], wts["conv_ln_b"],
                                           f"{tag}_conv_bwd")
    dp_qkv = jnp.concatenate([dq, dk, dv], axis=1).astype(CDT)
    dp_rest = jnp.concatenate([du, da, dg, dz], axis=1)
    dwin_qkv = _mm_tn(h, dp_qkv, name=f"{tag}_dwin_qkv")
    dwin_rest = _mm_tn(h, dp_rest, name=f"{tag}_dwin_rest")
    gin, dgamma = _mm_norm_bwd([(dp_qkv, wts["win_qkv"], True), (dp_rest, wts["win_rest"], True)], x, wts["mix_norm"],
                               gout, name=f"{tag}_dh_norm_bwd")
    dwin = _split_win(dwin_qkv, dwin_rest, f"{tag}_dwin_split")
    dpool_w = jnp.stack([dpw[64 * g:64 * g + 64, 64 * g:64 * g + 64] for g in range(4)])
    grads = dict(mix_norm=dgamma[0], w_in=dwin, pool_w=dpool_w, pool_scale=dpsc[0], forget_bias=dfb[0, 0:N_HEADS],
                 conv_w=dcw[0:CONV_K], conv_b=dcb[0], conv_ln_g=dlg[0], conv_ln_b=dlb[0], w_out=dwout)
    return gin, grads


def _rep_layer(rep, l):
    pw = rep["pool_w"][l].astype(CDT)
    wbd = jnp.zeros((256, 256), CDT)
    for g in range(4):
        wbd = lax.dynamic_update_slice(wbd, pw[g], (64 * g, 64 * g))
    return dict(
        ffn1_norm=rep["ffn1_norm"][l][None], ffn2_norm=rep["ffn2_norm"][l][None], mix_norm=rep["mix_norm"][l][None],
        fbias=jnp.pad(rep["forget_bias"][l], (0, LANES - N_HEADS))[None],
        pool_wbd=wbd, pool_scale=rep["pool_scale"][l][None], conv_b=rep["conv_b"][l][None],
        conv_ln_g=rep["conv_ln_g"][l][None], conv_ln_b=rep["conv_ln_b"][l][None])


def _local_step(x, target, rep, weights_for, grads_ready):
    depth = rep["ffn1_norm"].shape[0]
    kept = []
    for l in range(depth):
        r = _rep_layer(rep, l)
        w1, dep = weights_for(l, "ffn1", x)

        def late_down(after, l=l):
            got, tok = weights_for(l, "ffn1_down", after)
            return got["w_down"], tok

        x, s1, wd = _ffn_fwd(x, r["ffn1_norm"], w1["w_gate"], w1["w_up"], w1["w_down"] if "w_down" in w1 else late_down,
                             f"l{l}_ffn1", dep)
        w1 = dict(w1, w_down=wd)
        wm, dep = weights_for(l, "mix", x)
        wm = dict(r, win_qkv=wm["win_qkv"], win_rest=wm["win_rest"], w_out=wm["w_out"],
                  conv_w=jnp.pad(wm["conv_w"], ((0, CONV_HALO - CONV_K), (0, 0))))
        x, s2 = _mixer_fwd(x, wm, f"l{l}_mix", dep)
        w2, dep = weights_for(l, "ffn2", x)
        x, s3, _ = _ffn_fwd(x, r["ffn2_norm"], w2["w_gate"], w2["w_up"], w2["w_down"], f"l{l}_ffn2", dep)
        kept.append((r, w1, wm, w2, s1, s2, s3))
    loss, g, dfinal = _loss_bwd(x, rep["final_norm"][None], target, "loss_head")
    dep = grads_ready(None, "final", dict(final_norm=dfinal[0]))
    for l in reversed(range(depth)):
        r, w1, wm, w2, s1, s2, s3 = kept[l]

        def ffn_grads(which, l=l):
            return lambda gr: grads_ready(l, which, {f"{which}_{k}": v for k, v in gr.items()})

        g, dn = _ffn_bwd(s3, r["ffn2_norm"], w2["w_gate"], w2["w_up"], w2["w_down"], g, f"l{l}_ffn2", dep, ffn_grads("ffn2"))
        grads_ready(l, "norm", dict(ffn2_norm=dn[0]))
        g, gm = _mixer_bwd(s2, wm, g, f"l{l}_mix")
        dep = grads_ready(l, "mix", gm)
        g, dn = _ffn_bwd(s1, r["ffn1_norm"], w1["w_gate"], w1["w_up"], w1["w_down"], g, f"l{l}_ffn1", dep, ffn_grads("ffn1"))
        dep = grads_ready(l, "norm", dict(ffn1_norm=dn[0]))
    return loss, g


def _mesh_pos():
    return lax.axis_index("x"), lax.axis_index("y"), lax.axis_index("c")


def _dev_block(ref, dev, by_rows):
    if by_rows:
        r = ref.shape[1] // N_DEV
        return ref.at[:, pl.ds(dev * r, r), :]
    return ref.at[dev]


def _all_gather(shards, by_rows, name):
    n_arr = len(shards)
    out_shape = [_sds((s.shape[0], N_DEV * s.shape[1], s.shape[2]) if br else (N_DEV,) + s.shape, s.dtype)
                 for s, br in zip(shards, by_rows)]

    def body(*refs):
        xs, outs = refs[:n_arr], refs[n_arr:2 * n_arr]
        send_sems, recv_sems, local_sems = refs[2 * n_arr:]
        x, y, c = _mesh_pos()
        me, sibling = (x, y, c), (x, y, 1 - c)
        chips = [(1 - x, y), (x, 1 - y), (1 - x, 1 - y)]

        def rows(a, px, py, pc):
            return _dev_block(outs[a], 4 * px + 2 * py + pc, by_rows[a])

        def copy(k, a, block, to, src=None):
            return pltpu.make_async_remote_copy(
                src_ref=rows(a, *block) if src is None else src, dst_ref=rows(a, *block),
                send_sem=send_sems.at[k, a], recv_sem=recv_sems.at[k, a],
                device_id=to, device_id_type=pl.DeviceIdType.MESH)

        arrs = range(n_arr)
        mine = [pltpu.make_async_copy(xs[a], rows(a, *me), local_sems.at[a]) for a in arrs]
        for cp in mine:
            cp.start()
        first = [copy(0, a, me, sibling, src=xs[a]) for a in arrs]
        first += [copy(1 + j, a, me, (*chip, c), src=xs[a]) for j, chip in enumerate(chips) for a in arrs]
        for cp in first:
            cp.start()
        passed = []
        for j, chip in enumerate(chips):
            for a in arrs:
                copy(1 + j, a, (*chip, c), me).wait_recv()
                passed.append(copy(4 + j, a, (*chip, c), sibling))
                passed[-1].start()
        for a in arrs:
            copy(0, a, sibling, me).wait_recv()
        for j, chip in enumerate(chips):
            for a in arrs:
                copy(4 + j, a, (*chip, 1 - c), me).wait_recv()
        for cp in first + passed:
            cp.wait_send()
        for cp in mine:
            cp.wait()

    hbm = pl.BlockSpec(memory_space=pl.ANY)
    return pl.pallas_call(
        body, out_shape=out_shape, in_specs=[hbm] * n_arr, out_specs=[hbm] * n_arr,
        scratch_shapes=[pltpu.SemaphoreType.DMA((7, n_arr)), pltpu.SemaphoreType.DMA((7, n_arr)),
                        pltpu.SemaphoreType.DMA((n_arr,))],
        name=name)(*shards)


def _peer_copies(srcs, lands, send_sems, recv_sems, gather, by_rows):
    n_arr = len(srcs)
    x, y, c = _mesh_pos()
    my = 4 * x + 2 * y + c
    out = []
    for k in range(1, N_DEV):
        px, py, pc = x ^ (k >> 2), y ^ ((k >> 1) & 1), c ^ (k & 1)
        peer = 4 * px + 2 * py + pc
        for a in range(n_arr):
            src = srcs[a] if gather else _dev_block(srcs[a], peer, by_rows[a])
            dst = _dev_block(lands[a], my, by_rows[a]) if gather else lands[a].at[my]
            out.append(pltpu.make_async_remote_copy(
                src_ref=src, dst_ref=dst, send_sem=send_sems.at[(k - 1) * n_arr + a],
                recv_sem=recv_sems.at[(k - 1) * n_arr + a], device_id=(px, py, pc), device_id_type=pl.DeviceIdType.MESH))
    return out


def _land_shape(s, gather, by_rows):
    if gather:
        return (s.shape[0], N_DEV * s.shape[1], s.shape[2]) if by_rows else (N_DEV,) + s.shape
    return (N_DEV, s.shape[0], s.shape[1] // N_DEV, s.shape[2]) if by_rows else s.shape


_HBM = pl.BlockSpec(memory_space=pltpu.HBM)
_SEM = pl.BlockSpec(memory_space=pltpu.SEMAPHORE)


def _xfer_start(srcs, gather, by_rows, name, dep=None):
    n = len(srcs)
    lands = [lax.empty(_land_shape(s, gather, br), s.dtype) for s, br in zip(srcs, by_rows)]
    ins = [pltpu.with_memory_space_constraint(a, pltpu.HBM) for a in list(srcs) + lands]
    dspec, darg = _dep(dep)

    def body(*refs):
        s = 2 * n + len(darg)
        for cp in _peer_copies(refs[:n], refs[n:2 * n], refs[s], refs[s + 1], gather, by_rows):
            cp.start()
        for cp in _own_copies(refs[:n], refs[n:2 * n], refs[s + 2], gather, by_rows):
            cp.start()
        refs[-1][...] = jnp.zeros_like(refs[-1])

    sems = pltpu.SemaphoreType.DMA(((N_DEV - 1) * n,))
    outs = pl.pallas_call(
        body, name=name,
        out_shape=(sems, sems, pltpu.SemaphoreType.DMA((n,)), *[pltpu.HBM(a.shape, a.dtype) for a in ins],
                   _sds((8, LANES), F32)),
        in_specs=[_HBM] * (2 * n) + dspec,
        out_specs=(_SEM, _SEM, _SEM, *[_HBM] * (2 * n), pl.BlockSpec(memory_space=pltpu.VMEM)),
        input_output_aliases={i: 3 + i for i in range(2 * n)},
        compiler_params=pltpu.CompilerParams(has_side_effects=pltpu.SideEffectType.DATAFLOW_SIDE_EFFECTING))(*ins, *darg)
    return outs[0], outs[1], list(outs[3:-1]), outs[-1], outs[2]


def _own_copies(srcs, lands, sems, gather, by_rows):
    x, y, c = _mesh_pos()
    my = 4 * x + 2 * y + c
    out = []
    for a in range(len(srcs)):
        if gather:
            out.append(pltpu.make_async_copy(srcs[a], _dev_block(lands[a], my, by_rows[a]), sems.at[a]))
        else:
            out.append(pltpu.make_async_copy(_dev_block(srcs[a], my, by_rows[a]), lands[a].at[my], sems.at[a]))
    return out


def _xfer_wait(started, after, gather, by_rows, name):
    send_sems, recv_sems, bufs, _, local_sems = started
    n = len(bufs) // 2

    def body(*refs):
        for cp in _peer_copies(refs[:n], refs[n:2 * n], refs[2 * n], refs[2 * n + 1], gather, by_rows):
            cp.wait_send()
            cp.wait_recv()
        for cp in _own_copies(refs[:n], refs[n:2 * n], refs[2 * n + 2], gather, by_rows):
            cp.wait()

    outs = pl.pallas_call(
        body, name=name, out_shape=tuple(pltpu.HBM(a.shape, a.dtype) for a in bufs),
        in_specs=[_HBM] * (2 * n) + [_SEM, _SEM, _SEM, pl.BlockSpec(memory_space=pl.ANY)],
        out_specs=tuple([_HBM] * (2 * n)), input_output_aliases={i: i for i in range(2 * n)},
        compiler_params=pltpu.CompilerParams(has_side_effects=pltpu.SideEffectType.DATAFLOW_SIDE_EFFECTING))(
            *bufs, send_sems, recv_sems, local_sems, after)
    return list(outs[n:])


def _adam_update(g, w, m, v):
    c1 = 1.0 - ADAM_B1 ** ADAM_STEP
    c2 = 1.0 - ADAM_B2 ** ADAM_STEP
    nm = ADAM_B1 * m + (1.0 - ADAM_B1) * g
    nv = ADAM_B2 * v + (1.0 - ADAM_B2) * (g * g)
    return -ADAM_LR * ((nm / c1) / (jnp.sqrt(nv / c2) + ADAM_EPS) + ADAM_WD * w), nm, nv


def _adamw_body(p_ref, w_ref, m_ref, v_ref, g_ref, d_ref, nm_ref, nv_ref):
    g = p_ref[0].astype(F32)
    for i in range(1, N_DEV):
        g = g + p_ref[i].astype(F32)
    g_ref[...] = g
    d_ref[...], nm_ref[...], nv_ref[...] = _adam_update(g, w_ref[...], m_ref[...], v_ref[...])


def _adamw(parts, w, m, v, name, tr=1536):
    R = w.shape[0]
    tr = max(t for t in range(8, tr + 1, 8) if R % t == 0)

    def body(*refs):
        _adamw_body(*refs)

    row = pl.BlockSpec((tr, LANES), lambda i: (i, 0))
    return pl.pallas_call(
        body, grid=(R // tr,),
        in_specs=[pl.BlockSpec((N_DEV, tr, LANES), lambda i: (0, i, 0)), row, row, row],
        out_specs=[row, row, row, row], out_shape=[_sds((R, LANES), F32)] * 4,
        name=name, compiler_params=_cp(("parallel",)))(parts, w, m, v)


def _adamw_split(recvs, w, m, v, name, tr):
    depth, r, c = w.shape
    assert depth == len(recvs)
    tr = _tile(r, tr)

    def body(*refs):
        layer = pl.program_id(0)
        for ll in range(depth):
            @pl.when(layer == ll)
            def _(ll=ll):
                _adamw_body(refs[ll], *refs[depth:])

    wspec = pl.BlockSpec((None, tr, c), lambda l, i: (l, i, 0))
    rspecs = [pl.BlockSpec((N_DEV, None, tr, c), lambda l, i, ll=ll, t=t: (0, t, jnp.where(l == ll, i, 0), 0))
              for ll, (_, t) in enumerate(recvs)]
    return pl.pallas_call(
        body, grid=(depth, r // tr), in_specs=rspecs + [wspec, wspec, wspec],
        out_specs=[wspec] * 4, out_shape=[_sds(w.shape, F32)] * 4,
        name=name, compiler_params=_cp(("arbitrary", "arbitrary")))(*[a for a, _ in recvs], w, m, v)


def _merge_win(g, name, tr=256):
    _, nt, K, n = g.shape
    tr = _tile(K, tr)

    def body(g_ref, q_ref, r_ref):
        full = jnp.concatenate([g_ref[j] for j in range(N_DEV)], axis=1)
        q_ref[...] = full[:, 256:1792]
        zpad = jnp.zeros((tr, REST_W - 776), full.dtype)
        r_ref[...] = jnp.concatenate([full[:, 0:256], full[:, 1800:2312], full[:, 1792:1800], zpad], axis=1)

    return pl.pallas_call(
        body, grid=(nt, K // tr),
        in_specs=[pl.BlockSpec((N_DEV, None, tr, n), lambda t, i: (0, t, i, 0))],
        out_specs=[pl.BlockSpec((None, tr, 1536), lambda t, i: (t, i, 0)), pl.BlockSpec((None, tr, REST_W), lambda t, i: (t, i, 0))],
        out_shape=[_sds((nt, K, 1536), g.dtype), _sds((nt, K, REST_W), g.dtype)],
        name=name, compiler_params=_cp(("parallel", "parallel")))(g)


def _split_win(dq, dr, name, tr=256):
    K = dq.shape[0]
    tr = _tile(K, tr)
    n = (dq.shape[1] + 776) // N_DEV

    def body(q_ref, r_ref, o_ref):
        r = r_ref[...]
        full = jnp.concatenate([r[:, 0:256], q_ref[...], r[:, 768:776], r[:, 256:768]], axis=1)
        for j in range(N_DEV):
            o_ref[j] = full[:, n * j:n * (j + 1)].astype(o_ref.dtype)

    return pl.pallas_call(
        body, grid=(K // tr,),
        in_specs=[pl.BlockSpec((tr, dq.shape[1]), lambda i: (i, 0)), pl.BlockSpec((tr, REST_W), lambda i: (i, 0))],
        out_specs=pl.BlockSpec((N_DEV, tr, n), lambda i: (0, i, 0)),
        out_shape=_sds((N_DEV, K, n), WIRE), name=name, compiler_params=_cp(("parallel",)))(dq, dr)


WEIGHTS = ["ffn1_norm", "ffn1_w_gate", "ffn1_w_up", "ffn1_w_down", "mix_norm", "w_in", "pool_w", "pool_scale",
           "forget_bias", "conv_w", "conv_b", "conv_ln_g", "conv_ln_b", "w_out", "ffn2_norm", "ffn2_w_gate",
           "ffn2_w_up", "ffn2_w_down", "final_norm"]
FFN_PARTS = ("w_gate", "w_up", "w_down")
FFN_T = ["ffn1_w_gate", "ffn1_w_up", "ffn2_w_gate", "ffn2_w_up"]
BIG = FFN_T + ["ffn1_w_down", "ffn2_w_down", "w_in", "w_out"]
SMALL = [n for n in WEIGHTS if n not in BIG]


def _padded(n):
    return -(-n // PACK_ALIGN) * PACK_ALIGN


def _flat_pad(a):
    f = a.reshape(-1)
    return jnp.pad(f, (0, _padded(f.shape[0]) - f.shape[0]))


def _split8(a, axis):
    shp = a.shape
    a = a.reshape(shp[:axis] + (N_DEV, shp[axis] // N_DEV) + shp[axis + 1:])
    return jnp.moveaxis(a, axis, 0)


def _merge8(a, axis):
    a = jnp.moveaxis(a, 0, axis)
    shp = a.shape
    return a.reshape(shp[:axis] + (shp[axis] * shp[axis + 1],) + shp[axis + 2:])


def _pack_small(arrs):
    return jnp.concatenate([_flat_pad(arrs[n]) for n in SMALL]).reshape(-1, LANES)


def _pack_small_parts(grads):
    cols = []
    for n in SMALL:
        g = grads[n]
        if n == "conv_w":
            s = _split8(g, 2).reshape(N_DEV, -1)
        else:
            s = jnp.broadcast_to(g.reshape(1, -1), (N_DEV, g.size))
        cols.append(jnp.pad(s, ((0, 0), (0, _padded(s.shape[1]) - s.shape[1]))))
    return jnp.concatenate(cols, axis=1).reshape(N_DEV, -1, LANES)


def _unpack_small(buf, like):
    flat = buf.reshape(-1)
    out, off = {}, 0
    for n in SMALL:
        size = like[n].size
        out[n] = flat[off:off + size].reshape(like[n].shape)
        off += _padded(size)
    return out


class _Comm:
    def __init__(self, w):
        self.w = w
        self.bf = {n: (jnp.swapaxes(w[n], 1, 2) if n in FFN_T else w[n]).astype(CDT) for n in BIG}
        self.ready = {}
        self.grads = {}

    def _ffn_shards(self, l, which):
        return jnp.stack([self.bf[f"{which}_{k}"][l] for k in FFN_PARTS])

    def _put_ffn(self, l, which, rows, t):
        self.ready[(l, which)] = dict(w_gate=rows[t], w_up=rows[t + 1], w_down=rows[t + 2])

    def weights_for(self, l, stage, x):
        bf = self.bf
        dep = None
        if (l, stage) == (0, "ffn1"):
            gd, = _all_gather([self._ffn_shards(0, "ffn1")[0:2]], [True], "gather_l0_ffn1")
            self.ready[(0, "ffn1")] = dict(w_gate=gd[0], w_up=gd[1])
            self.started = _xfer_start([bf["ffn1_w_down"][0:1], bf["w_in"][0:1], bf["w_out"][0:1], self.w["conv_w"]], True,
                                       [True, False, True, False], "gather_mix0_start", dep=gd)
            dep = self.started[3]
        elif (l, stage) == (0, "ffn1_down"):
            gdn, gi, go, gc = _xfer_wait(self.started, x, True, [True, False, True, False], "gather_mix0_wait")
            self.ready[(0, "ffn1_down")] = dict(w_down=gdn[0])
            q, r = _merge_win(gi, "merge_l0_w_in")
            self.conv_w = _merge8(gc, 2)
            self.ready[(0, "mix")] = dict(win_qkv=q[0], win_rest=r[0], w_out=go[0], conv_w=self.conv_w[0])
            rows = jnp.concatenate([self._ffn_shards(0, "ffn2"), self._ffn_shards(1, "ffn1"), self._ffn_shards(1, "ffn2")])
            self.started = _xfer_start([rows, bf["w_in"][1:2], bf["w_out"][1:2]], True, [True, False, True],
                                       "gather_rest_start")
            dep = self.started[3]
        elif (l, stage) == (0, "ffn2"):
            gd, gi, go = _xfer_wait(self.started, x, True, [True, False, True], "gather_rest_wait")
            self._put_ffn(0, "ffn2", gd, 0)
            self._put_ffn(1, "ffn1", gd, 3)
            self._put_ffn(1, "ffn2", gd, 6)
            q, r = _merge_win(gi, "merge_l1_w_in")
            self.ready[(1, "mix")] = dict(win_qkv=q[0], win_rest=r[0], w_out=go[0], conv_w=self.conv_w[1])
        return self.ready[(l, stage)], dep

    def grads_ready(self, l, stage, grads):
        for n, v in grads.items():
            self.grads[(l, n)] = v
        gr = self.grads

        def ffn_rows(layer, which, parts=FFN_PARTS):
            return [gr[(layer, f"{which}_{k}")][None] for k in parts]

        if l == 1 and "ffn1_w_gate" in grads:
            self.sent1 = _xfer_start(
                ffn_rows(1, "ffn1") + ffn_rows(1, "ffn2") + [gr[(1, "w_in")][:, None], gr[(1, "w_out")][None]],
                False, [True] * 6 + [False, True], "grads_l1_start")
            return self.sent1[3]
        if l == 0 and "ffn2_w_gate" in grads:
            self.sent_ffn2 = _xfer_start(ffn_rows(0, "ffn2"), False, [True] * 3, "grads_l0_ffn2_start")
            return self.sent_ffn2[3]
        if (l, stage) == (0, "mix"):
            self.sent_mix = _xfer_start([gr[(0, "w_in")][:, None], gr[(0, "w_out")][None]], False, [False, True],
                                        "grads_l0_mix_start")
            return self.sent_mix[3]
        if l == 0 and "ffn1_w_down" in grads:
            self.sent_down = _xfer_start([gr[(0, "ffn1_w_down")][None]], False, [True], "grads_l0_ffn1_down_start")
            return self.sent_down[3]
        if l == 0 and "ffn1_w_gate" in grads:
            self.sent_gu = _xfer_start(ffn_rows(0, "ffn1", FFN_PARTS[:2]), False, [True] * 2,
                                       "grads_l0_ffn1_gate_up_start")
            return self.sent_gu[3]
        return None

    def finish(self, m, v, after):
        w, gr = self.w, self.grads
        depth = range(w["w_in"].shape[0])
        small = {n: (gr[(None, n)] if n == "final_norm" else jnp.stack([gr[(l, n)] for l in depth])) for n in SMALL}
        sent_small = _xfer_start([_pack_small_parts(small)], False, [False], "grads_small_start")
        *r1, i1, o1 = _xfer_wait(self.sent1, after, False, [True] * 6 + [False, True], "grads_l1_wait")
        r2 = _xfer_wait(self.sent_ffn2, after, False, [True] * 3, "grads_l0_ffn2_wait")
        i0, o0 = _xfer_wait(self.sent_mix, after, False, [False, True], "grads_l0_mix_wait")

        def adam(n, recvs, tr):
            if n in FFN_T:
                out = _adamw_split(recvs, *[jnp.swapaxes(t[n], 1, 2) for t in (w, m, v)], f"adamw_{n}", tr)
                return [jnp.swapaxes(o, 1, 2) for o in out]
            return _adamw_split(recvs, w[n], m[n], v[n], f"adamw_{n}", tr)

        res = {}
        for t, k in enumerate(FFN_PARTS):
            res[f"ffn2_{k}"] = adam(f"ffn2_{k}", [(r2[t], 0), (r1[3 + t], 0)], 176)
        res["w_in"] = adam("w_in", [(i0, 0), (i1, 0)], 256)
        res["w_out"] = adam("w_out", [(o0, 0), (o1, 0)], 128)
        r0, = _xfer_wait(self.sent_down, res["w_out"][0], False, [True], "grads_l0_ffn1_down_wait")
        res["ffn1_w_down"] = adam("ffn1_w_down", [(r0, 0), (r1[2], 0)], 176)
        g0 = _xfer_wait(self.sent_gu, res["ffn1_w_down"][0], False, [True] * 2, "grads_l0_ffn1_gate_up_wait")
        res["ffn1_w_gate"] = adam("ffn1_w_gate", [(g0[0], 0), (r1[0], 0)], 176)
        res["ffn1_w_up"] = adam("ffn1_w_up", [(g0[1], 0), (r1[1], 0)], 176)
        rs, = _xfer_wait(sent_small, res["ffn1_w_up"][0], False, [False], "grads_small_wait")
        packed = _adamw(rs, _pack_small(w), _pack_small(m), _pack_small(v), "adamw_small")
        unpacked = [_unpack_small(b, w) for b in packed]
        for n in SMALL:
            res[n] = [u[n] for u in unpacked]
        return res


def kernel(x, ffn1_norm, ffn1_w_gate, ffn1_w_up, ffn1_w_down, mix_norm, w_in, pool_w, pool_scale, forget_bias, conv_w, conv_b, conv_ln_g, conv_ln_b, w_out, ffn2_norm, ffn2_w_gate, ffn2_w_up, ffn2_w_down, final_norm, loss_target, m_ffn1_norm, m_ffn1_w_gate, m_ffn1_w_up, m_ffn1_w_down, m_mix_norm, m_w_in, m_pool_w, m_pool_scale, m_forget_bias, m_conv_w, m_conv_b, m_conv_ln_g, m_conv_ln_b, m_w_out, m_ffn2_norm, m_ffn2_w_gate, m_ffn2_w_up, m_ffn2_w_down, m_final_norm, v_ffn1_norm, v_ffn1_w_gate, v_ffn1_w_up, v_ffn1_w_down, v_mix_norm, v_w_in, v_pool_w, v_pool_scale, v_forget_bias, v_conv_w, v_conv_b, v_conv_ln_g, v_conv_ln_b, v_w_out, v_ffn2_norm, v_ffn2_w_gate, v_ffn2_w_up, v_ffn2_w_down, v_final_norm):
    w = dict(zip(WEIGHTS, (ffn1_norm, ffn1_w_gate, ffn1_w_up, ffn1_w_down, mix_norm, w_in, pool_w, pool_scale, forget_bias,
                           conv_w, conv_b, conv_ln_g, conv_ln_b, w_out, ffn2_norm, ffn2_w_gate, ffn2_w_up, ffn2_w_down,
                           final_norm)))
    m = dict(zip(WEIGHTS, (m_ffn1_norm, m_ffn1_w_gate, m_ffn1_w_up, m_ffn1_w_down, m_mix_norm, m_w_in, m_pool_w, m_pool_scale,
                           m_forget_bias, m_conv_w, m_conv_b, m_conv_ln_g, m_conv_ln_b, m_w_out, m_ffn2_norm, m_ffn2_w_gate,
                           m_ffn2_w_up, m_ffn2_w_down, m_final_norm)))
    v = dict(zip(WEIGHTS, (v_ffn1_norm, v_ffn1_w_gate, v_ffn1_w_up, v_ffn1_w_down, v_mix_norm, v_w_in, v_pool_w, v_pool_scale,
                           v_forget_bias, v_conv_w, v_conv_b, v_conv_ln_g, v_conv_ln_b, v_w_out, v_ffn2_norm, v_ffn2_w_gate,
                           v_ffn2_w_up, v_ffn2_w_down, v_final_norm)))
    comm = _Comm(w)
    loss_row, gx = _local_step(x[0], loss_target[0], w, comm.weights_for, comm.grads_ready)
    loss = lax.psum(loss_row[0, 0], ("x", "y", "c"))
    res = comm.finish(m, v, gx)
    return (loss, gx[None], *[res[n][i] for i in range(4) for n in WEIGHTS])
```
